```python
import jax, jax.numpy as jnp
from jax import lax
import numpy as np

D_MODEL = 1024
BATCH = 8
SEQ = 8192
DEPTH = 2

CHUNK = 64
N_MIXERS = 2
N_FOX = (DEPTH + 1) // 2
N_SGU = DEPTH // 2
FOX_HEADS = 16
FOX_HEAD_DIM = D_MODEL // FOX_HEADS
Q_BLOCK = 128
SGU_WIDTH = 2 * D_MODEL
SGU_GROUPS = 8
SGU_GROUP_DIM = SGU_WIDTH // SGU_GROUPS
SGU_BLOCK = 128
D_FF = 2816
CONV_WIDTH = 3
EPS = 1e-6

kernel_name = "fox_gmlp_convffn_adaln_hybrid"


def rmsnorm(x, g):
    xf = x.astype(jnp.float32)
    y = xf * lax.rsqrt(jnp.mean(xf * xf, axis=-1, keepdims=True) + EPS)
    return (y * g.astype(jnp.float32)).astype(x.dtype)


def layernorm(x, g, b):
    xf = x.astype(jnp.float32)
    mu = jnp.mean(xf, axis=-1, keepdims=True)
    var = jnp.mean(jnp.square(xf - mu), axis=-1, keepdims=True)
    y = (xf - mu) * lax.rsqrt(var + EPS)
    return (y * g.astype(jnp.float32) + b.astype(jnp.float32)).astype(x.dtype)


def modulate(h, shift, scale):
    return h * (1 + scale[:, None, :]) + shift[:, None, :]


def forgetting_attention(h, w_in, b_f, q_gain, k_gain, w_out):
    B, S, D = h.shape
    H, Dh = FOX_HEADS, FOX_HEAD_DIM
    proj = h @ w_in
    q, k, v, o, fl = jnp.split(proj, [D, 2 * D, 3 * D, 4 * D], axis=-1)
    q = rmsnorm(q.reshape(B, S, H, Dh), q_gain)
    k = rmsnorm(k.reshape(B, S, H, Dh), k_gain)
    v = v.reshape(B, S, H, Dh)
    logf = jax.nn.log_sigmoid((fl + b_f).astype(jnp.float32))
    F = jnp.cumsum(logf, axis=1).transpose(0, 2, 1)
    scale = Dh ** -0.5
    outs = []
    for qb in range(S // Q_BLOCK):
        q0, q1 = qb * Q_BLOCK, (qb + 1) * Q_BLOCK
        logits = jnp.einsum('bqhd,bkhd->bhqk', q[:, q0:q1], k[:, :q1]).astype(jnp.float32) * scale
        decay = F[:, :, q0:q1, None] - F[:, :, None, :q1]
        qpos = jnp.arange(q0, q1)[:, None]
        kpos = jnp.arange(q1)[None, :]
        logits = jnp.where(kpos <= qpos, logits + decay, -jnp.inf)
        p = jax.nn.softmax(logits, axis=-1).astype(v.dtype)
        outs.append(jnp.einsum('bhqk,bkhd->bqhd', p, v[:, :q1]))
    att = jnp.concatenate(outs, axis=1).reshape(B, S, D)
    return (att * jax.nn.sigmoid(o)) @ w_out


def spatial_gating_mlp(h, w_in, b_in, v_gain, v_bias, w_s, b_s, w_out):
    B, S, _ = h.shape
    z = jax.nn.gelu(h @ w_in + b_in)
    u, v = jnp.split(z, 2, axis=-1)
    v = layernorm(v, v_gain, v_bias)
    n = S // SGU_BLOCK
    v = v.reshape(B, n, SGU_BLOCK, SGU_GROUPS, SGU_GROUP_DIM)
    t = jnp.arange(SGU_BLOCK)
    mask = (t[None, :] // CHUNK) <= (t[:, None] // CHUNK)
    ws = jnp.where(mask[None], w_s, 0)
    mixed = jnp.einsum('gts,bnsgc->bntgc', ws, v) + b_s.T[None, None, :, :, None]
    y = u * mixed.reshape(B, S, SGU_WIDTH)
    return y @ w_out


def conv_gated_ffn(h, w_up, conv_w, conv_b, w_down):
    S = h.shape[1]
    a = h @ w_up
    ap = jnp.pad(a, ((0, 0), (CONV_WIDTH - 1, 0), (0, 0)))
    acc = ap[:, 0:S] * conv_w[0]
    for j in range(1, CONV_WIDTH):
        acc = acc + ap[:, j:j + S] * conv_w[j]
    a = acc + conv_b
    g, val = jnp.split(a, 2, axis=-1)
    return (jax.nn.silu(g) * val) @ w_down


def _fwd_setup_inputs(seed: int = 0) -> dict:
    key = jax.random.key(seed)
    ks = jax.random.split(key, 24)
    D = D_MODEL
    nrm = jax.random.normal
    f32 = jnp.float32
    return {
        "x": nrm(ks[0], (BATCH, SEQ, D), f32),
        "c": nrm(ks[1], (BATCH, D), f32),
        "fox_w_in": nrm(ks[2], (N_FOX, D, 4 * D + FOX_HEADS), f32) * D ** -0.5,
        "fox_b_f": 3.0 + 0.5 * nrm(ks[3], (N_FOX, FOX_HEADS), f32),
        "fox_q_gain": 1.0 + 0.02 * nrm(ks[4], (N_FOX, FOX_HEAD_DIM), f32),
        "fox_k_gain": 1.0 + 0.02 * nrm(ks[5], (N_FOX, FOX_HEAD_DIM), f32),
        "fox_w_out": nrm(ks[6], (N_FOX, D, D), f32) * D ** -0.5,
        "sgu_w_in": nrm(ks[7], (N_SGU, D, 2 * SGU_WIDTH), f32) * D ** -0.5,
        "sgu_b_in": 0.02 * nrm(ks[8], (N_SGU, 2 * SGU_WIDTH), f32),
        "sgu_v_gain": 1.0 + 0.02 * nrm(ks[9], (N_SGU, SGU_WIDTH), f32),
        "sgu_v_bias": 0.02 * nrm(ks[10], (N_SGU, SGU_WIDTH), f32),
        "sgu_w_s": nrm(ks[11], (N_SGU, SGU_GROUPS, SGU_BLOCK, SGU_BLOCK), f32) * SGU_BLOCK ** -0.5,
        "sgu_b_s": 1.0 + 0.1 * nrm(ks[12], (N_SGU, SGU_GROUPS, SGU_BLOCK), f32),
        "sgu_w_out": nrm(ks[13], (N_SGU, SGU_WIDTH, D), f32) * SGU_WIDTH ** -0.5,
        "ffn_w_up": nrm(ks[14], (DEPTH, D, 2 * D_FF), f32) * D ** -0.5,
        "ffn_conv_w": nrm(ks[15], (DEPTH, CONV_WIDTH, 2 * D_FF), f32) * CONV_WIDTH ** -0.5,
        "ffn_conv_b": 0.02 * nrm(ks[16], (DEPTH, 2 * D_FF), f32),
        "ffn_w_down": nrm(ks[17], (DEPTH, D_FF, D), f32) * D_FF ** -0.5,
        "ada_w": nrm(ks[18], (DEPTH, D, 6 * D), f32) * (0.5 * D ** -0.5),
        "ada_b": 0.02 * nrm(ks[19], (DEPTH, 6 * D), f32),
        "norm1_g": 1.0 + 0.02 * nrm(ks[20], (DEPTH, D), f32),
        "norm2_g": 1.0 + 0.02 * nrm(ks[21], (DEPTH, D), f32),
        "final_g": 1.0 + 0.02 * nrm(ks[22], (D,), f32),
    }


def _fwd_reference(x, c, fox_w_in, fox_b_f, fox_q_gain, fox_k_gain, fox_w_out,
              sgu_w_in, sgu_b_in, sgu_v_gain, sgu_v_bias, sgu_w_s, sgu_b_s, sgu_w_out,
              ffn_w_up, ffn_conv_w, ffn_conv_b, ffn_w_down,
              ada_w, ada_b, norm1_g, norm2_g, final_g):
    c_act = jax.nn.silu(c)
    for i in range(DEPTH):
        mod = c_act @ ada_w[i] + ada_b[i]
        sh1, sc1, g1, sh2, sc2, g2 = jnp.split(mod, 6, axis=-1)
        h = modulate(rmsnorm(x, norm1_g[i]), sh1, sc1)
        j = i // N_MIXERS
        if i % N_MIXERS == 0:
            y = forgetting_attention(h, fox_w_in[j], fox_b_f[j], fox_q_gain[j],
                                     fox_k_gain[j], fox_w_out[j])
        else:
            y = spatial_gating_mlp(h, sgu_w_in[j], sgu_b_in[j], sgu_v_gain[j],
                                   sgu_v_bias[j], sgu_w_s[j], sgu_b_s[j], sgu_w_out[j])
        x = x + g1[:, None, :] * y
        h = modulate(rmsnorm(x, norm2_g[i]), sh2, sc2)
        x = x + g2[:, None, :] * conv_gated_ffn(h, ffn_w_up[i], ffn_conv_w[i],
                                                ffn_conv_b[i], ffn_w_down[i])
    return rmsnorm(x, final_g)


import jax as _jax
import jax.numpy as _jnp

TWIN_FORMAT = 'train_step'
FWD_PARAMS = ['x', 'c', 'fox_w_in', 'fox_b_f', 'fox_q_gain', 'fox_k_gain', 'fox_w_out', 'sgu_w_in', 'sgu_b_in', 'sgu_v_gain', 'sgu_v_bias', 'sgu_w_s', 'sgu_b_s', 'sgu_w_out', 'ffn_w_up', 'ffn_conv_w', 'ffn_conv_b', 'ffn_w_down', 'ada_w', 'ada_b', 'norm1_g', 'norm2_g', 'final_g']
TWIN_WEIGHTS = ['fox_w_in', 'fox_b_f', 'fox_q_gain', 'fox_k_gain', 'fox_w_out', 'sgu_w_in', 'sgu_b_in', 'sgu_v_gain', 'sgu_v_bias', 'sgu_w_s', 'sgu_b_s', 'sgu_w_out', 'ffn_w_up', 'ffn_conv_w', 'ffn_conv_b', 'ffn_w_down', 'ada_w', 'ada_b', 'norm1_g', 'norm2_g', 'final_g']
TWIN_DIFF_INPUT = 'x'
TWIN_INPUTS = ['x', 'c', 'fox_w_in', 'fox_b_f', 'fox_q_gain', 'fox_k_gain', 'fox_w_out', 'sgu_w_in', 'sgu_b_in', 'sgu_v_gain', 'sgu_v_bias', 'sgu_w_s', 'sgu_b_s', 'sgu_w_out', 'ffn_w_up', 'ffn_conv_w', 'ffn_conv_b', 'ffn_w_down', 'ada_w', 'ada_b', 'norm1_g', 'norm2_g', 'final_g', 'loss_target', 'm_fox_w_in', 'm_fox_b_f', 'm_fox_q_gain', 'm_fox_k_gain', 'm_fox_w_out', 'm_sgu_w_in', 'm_sgu_b_in', 'm_sgu_v_gain', 'm_sgu_v_bias', 'm_sgu_w_s', 'm_sgu_b_s', 'm_sgu_w_out', 'm_ffn_w_up', 'm_ffn_conv_w', 'm_ffn_conv_b', 'm_ffn_w_down', 'm_ada_w', 'm_ada_b', 'm_norm1_g', 'm_norm2_g', 'm_final_g', 'v_fox_w_in', 'v_fox_b_f', 'v_fox_q_gain', 'v_fox_k_gain', 'v_fox_w_out', 'v_sgu_w_in', 'v_sgu_b_in', 'v_sgu_v_gain', 'v_sgu_v_bias', 'v_sgu_w_s', 'v_sgu_b_s', 'v_sgu_w_out', 'v_ffn_w_up', 'v_ffn_conv_w', 'v_ffn_conv_b', 'v_ffn_w_down', 'v_ada_w', 'v_ada_b', 'v_norm1_g', 'v_norm2_g', 'v_final_g']
TWIN_OUTPUTS = ['loss', 'grad_x', 'grad_fox_w_in', 'grad_fox_b_f', 'grad_fox_q_gain', 'grad_fox_k_gain', 'grad_fox_w_out', 'grad_sgu_w_in', 'grad_sgu_b_in', 'grad_sgu_v_gain', 'grad_sgu_v_bias', 'grad_sgu_w_s', 'grad_sgu_b_s', 'grad_sgu_w_out', 'grad_ffn_w_up', 'grad_ffn_conv_w', 'grad_ffn_conv_b', 'grad_ffn_w_down', 'grad_ada_w', 'grad_ada_b', 'grad_norm1_g', 'grad_norm2_g', 'grad_final_g', 'delta_fox_w_in', 'delta_fox_b_f', 'delta_fox_q_gain', 'delta_fox_k_gain', 'delta_fox_w_out', 'delta_sgu_w_in', 'delta_sgu_b_in', 'delta_sgu_v_gain', 'delta_sgu_v_bias', 'delta_sgu_w_s', 'delta_sgu_b_s', 'delta_sgu_w_out', 'delta_ffn_w_up', 'delta_ffn_conv_w', 'delta_ffn_conv_b', 'delta_ffn_w_down', 'delta_ada_w', 'delta_ada_b', 'delta_norm1_g', 'delta_norm2_g', 'delta_final_g', 'new_m_fox_w_in', 'new_m_fox_b_f', 'new_m_fox_q_gain', 'new_m_fox_k_gain', 'new_m_fox_w_out', 'new_m_sgu_w_in', 'new_m_sgu_b_in', 'new_m_sgu_v_gain', 'new_m_sgu_v_bias', 'new_m_sgu_w_s', 'new_m_sgu_b_s', 'new_m_sgu_w_out', 'new_m_ffn_w_up', 'new_m_ffn_conv_w', 'new_m_ffn_conv_b', 'new_m_ffn_w_down', 'new_m_ada_w', 'new_m_ada_b', 'new_m_norm1_g', 'new_m_norm2_g', 'new_m_final_g', 'new_v_fox_w_in', 'new_v_fox_b_f', 'new_v_fox_q_gain', 'new_v_fox_k_gain', 'new_v_fox_w_out', 'new_v_sgu_w_in', 'new_v_sgu_b_in', 'new_v_sgu_v_gain', 'new_v_sgu_v_bias', 'new_v_sgu_w_s', 'new_v_sgu_b_s', 'new_v_sgu_w_out', 'new_v_ffn_w_up', 'new_v_ffn_conv_w', 'new_v_ffn_conv_b', 'new_v_ffn_w_down', 'new_v_ada_w', 'new_v_ada_b', 'new_v_norm1_g', 'new_v_norm2_g', 'new_v_final_g']
TWIN_LEAF_KINDS = {'loss': 'loss', 'grad_x': 'grad_x', 'grad_fox_w_in': 'grad_w', 'grad_fox_b_f': 'grad_w', 'grad_fox_q_gain': 'grad_w', 'grad_fox_k_gain': 'grad_w', 'grad_fox_w_out': 'grad_w', 'grad_sgu_w_in': 'grad_w', 'grad_sgu_b_in': 'grad_w', 'grad_sgu_v_gain': 'grad_w', 'grad_sgu_v_bias': 'grad_w', 'grad_sgu_w_s': 'grad_w', 'grad_sgu_b_s': 'grad_w', 'grad_sgu_w_out': 'grad_w', 'grad_ffn_w_up': 'grad_w', 'grad_ffn_conv_w': 'grad_w', 'grad_ffn_conv_b': 'grad_w', 'grad_ffn_w_down': 'grad_w', 'grad_ada_w': 'grad_w', 'grad_ada_b': 'grad_w', 'grad_norm1_g': 'grad_w', 'grad_norm2_g': 'grad_w', 'grad_final_g': 'grad_w', 'delta_fox_w_in': 'delta_w', 'delta_fox_b_f': 'delta_w', 'delta_fox_q_gain': 'delta_w', 'delta_fox_k_gain': 'delta_w', 'delta_fox_w_out': 'delta_w', 'delta_sgu_w_in': 'delta_w', 'delta_sgu_b_in': 'delta_w', 'delta_sgu_v_gain': 'delta_w', 'delta_sgu_v_bias': 'delta_w', 'delta_sgu_w_s': 'delta_w', 'delta_sgu_b_s': 'delta_w', 'delta_sgu_w_out': 'delta_w', 'delta_ffn_w_up': 'delta_w', 'delta_ffn_conv_w': 'delta_w', 'delta_ffn_conv_b': 'delta_w', 'delta_ffn_w_down': 'delta_w', 'delta_ada_w': 'delta_w', 'delta_ada_b': 'delta_w', 'delta_norm1_g': 'delta_w', 'delta_norm2_g': 'delta_w', 'delta_final_g': 'delta_w', 'new_m_fox_w_in': 'new_m', 'new_m_fox_b_f': 'new_m', 'new_m_fox_q_gain': 'new_m', 'new_m_fox_k_gain': 'new_m', 'new_m_fox_w_out': 'new_m', 'new_m_sgu_w_in': 'new_m', 'new_m_sgu_b_in': 'new_m', 'new_m_sgu_v_gain': 'new_m', 'new_m_sgu_v_bias': 'new_m', 'new_m_sgu_w_s': 'new_m', 'new_m_sgu_b_s': 'new_m', 'new_m_sgu_w_out': 'new_m', 'new_m_ffn_w_up': 'new_m', 'new_m_ffn_conv_w': 'new_m', 'new_m_ffn_conv_b': 'new_m', 'new_m_ffn_w_down': 'new_m', 'new_m_ada_w': 'new_m', 'new_m_ada_b': 'new_m', 'new_m_norm1_g': 'new_m', 'new_m_norm2_g': 'new_m', 'new_m_final_g': 'new_m', 'new_v_fox_w_in': 'new_v', 'new_v_fox_b_f': 'new_v', 'new_v_fox_q_gain': 'new_v', 'new_v_fox_k_gain': 'new_v', 'new_v_fox_w_out': 'new_v', 'new_v_sgu_w_in': 'new_v', 'new_v_sgu_b_in': 'new_v', 'new_v_sgu_v_gain': 'new_v', 'new_v_sgu_v_bias': 'new_v', 'new_v_sgu_w_s': 'new_v', 'new_v_sgu_b_s': 'new_v', 'new_v_sgu_w_out': 'new_v', 'new_v_ffn_w_up': 'new_v', 'new_v_ffn_conv_w': 'new_v', 'new_v_ffn_conv_b': 'new_v', 'new_v_ffn_w_down': 'new_v', 'new_v_ada_w': 'new_v', 'new_v_ada_b': 'new_v', 'new_v_norm1_g': 'new_v', 'new_v_norm2_g': 'new_v', 'new_v_final_g': 'new_v'}


def _forward(args):
    return _fwd_reference(*[args[k] for k in FWD_PARAMS])


def _output_shape():
    def fwd():
        inp = _fwd_setup_inputs(0)
        return _fwd_reference(*[inp[k] for k in FWD_PARAMS])
    out = _jax.eval_shape(fwd)
    return out.shape, out.dtype

N_MICROBATCH = 1
ADAM_LR = 0.001
ADAM_B1 = 0.9
ADAM_B2 = 0.999
ADAM_EPS = 1e-08
ADAM_WD = 0.01
ADAM_STEP = 10
PER_EXAMPLE_BATCH_AXIS = {'x': 0, 'c': 0, 'loss_target': 0}
SHARED_INPUTS = []
_WEIGHT_DTYPES = {'fox_w_in': _jnp.float32, 'fox_b_f': _jnp.float32, 'fox_q_gain': _jnp.float32, 'fox_k_gain': _jnp.float32, 'fox_w_out': _jnp.float32, 'sgu_w_in': _jnp.float32, 'sgu_b_in': _jnp.float32, 'sgu_v_gain': _jnp.float32, 'sgu_v_bias': _jnp.float32, 'sgu_w_s': _jnp.float32, 'sgu_b_s': _jnp.float32, 'sgu_w_out': _jnp.float32, 'ffn_w_up': _jnp.float32, 'ffn_conv_w': _jnp.float32, 'ffn_conv_b': _jnp.float32, 'ffn_w_down': _jnp.float32, 'ada_w': _jnp.float32, 'ada_b': _jnp.float32, 'norm1_g': _jnp.float32, 'norm2_g': _jnp.float32, 'final_g': _jnp.float32}
MOMENT_SCALE = {'fox_w_in': 1.339896e-02, 'fox_b_f': 6.204265e-02, 'fox_q_gain': 6.892118e-02, 'fox_k_gain': 6.742841e-02, 'fox_w_out': 1.954181e-02, 'sgu_w_in': 4.386166e-02, 'sgu_b_in': 4.035239e-02, 'sgu_v_gain': 3.417907e-02, 'sgu_v_bias': 3.213187e-02, 'sgu_w_s': 4.469010e-02, 'sgu_b_s': 5.341143e-02, 'sgu_w_out': 6.914456e-02, 'ffn_w_up': 3.275166e-02, 'ffn_conv_w': 3.269476e-02, 'ffn_conv_b': 3.008176e-02, 'ffn_w_down': 5.334901e-02, 'ada_w': 7.416092e-02, 'ada_b': 1.310213e-01, 'norm1_g': 6.139871e-02, 'norm2_g': 7.390837e-02, 'final_g': 6.408013e+01}


def _to_microbatches(a, axis):
    t = _jnp.moveaxis(a, axis, 0)
    t = t.reshape((N_MICROBATCH, t.shape[0] // N_MICROBATCH) + t.shape[1:])
    return _jnp.moveaxis(t, 1, axis + 1)


def setup_inputs(seed: int = 0) -> dict:
    inp = _fwd_setup_inputs(seed)
    key = _jax.random.fold_in(_jax.random.key(seed), 7919)
    shape, _ = _output_shape()
    out = dict(inp)
    out["loss_target"] = _jax.random.normal(_jax.random.fold_in(key, 0), shape, _jnp.float32)
    for i, name in enumerate(TWIN_WEIGHTS):
        w = inp[name].astype(_jnp.float32)
        if MOMENT_SCALE is None:
            s = _jnp.sqrt(_jnp.mean(_jnp.square(w)) + 1e-30)
        else:
            s = MOMENT_SCALE[name]
        km, kv = _jax.random.split(_jax.random.fold_in(key, i + 1))
        out[name] = w
        out["m_" + name] = s * _jax.random.normal(km, w.shape, _jnp.float32)
        out["v_" + name] = (s * s) * _jax.random.uniform(kv, w.shape, _jnp.float32, 0.5, 1.5)
    if N_MICROBATCH > 1:
        for name, axis in PER_EXAMPLE_BATCH_AXIS.items():
            out[name] = _to_microbatches(out[name], axis)
    return {'x': out['x'], 'c': out['c'], 'fox_w_in': out['fox_w_in'], 'fox_b_f': out['fox_b_f'], 'fox_q_gain': out['fox_q_gain'], 'fox_k_gain': out['fox_k_gain'], 'fox_w_out': out['fox_w_out'], 'sgu_w_in': out['sgu_w_in'], 'sgu_b_in': out['sgu_b_in'], 'sgu_v_gain': out['sgu_v_gain'], 'sgu_v_bias': out['sgu_v_bias'], 'sgu_w_s': out['sgu_w_s'], 'sgu_b_s': out['sgu_b_s'], 'sgu_w_out': out['sgu_w_out'], 'ffn_w_up': out['ffn_w_up'], 'ffn_conv_w': out['ffn_conv_w'], 'ffn_conv_b': out['ffn_conv_b'], 'ffn_w_down': out['ffn_w_down'], 'ada_w': out['ada_w'], 'ada_b': out['ada_b'], 'norm1_g': out['norm1_g'], 'norm2_g': out['norm2_g'], 'final_g': out['final_g'], 'loss_target': out['loss_target'], 'm_fox_w_in': out['m_fox_w_in'], 'm_fox_b_f': out['m_fox_b_f'], 'm_fox_q_gain': out['m_fox_q_gain'], 'm_fox_k_gain': out['m_fox_k_gain'], 'm_fox_w_out': out['m_fox_w_out'], 'm_sgu_w_in': out['m_sgu_w_in'], 'm_sgu_b_in': out['m_sgu_b_in'], 'm_sgu_v_gain': out['m_sgu_v_gain'], 'm_sgu_v_bias': out['m_sgu_v_bias'], 'm_sgu_w_s': out['m_sgu_w_s'], 'm_sgu_b_s': out['m_sgu_b_s'], 'm_sgu_w_out': out['m_sgu_w_out'], 'm_ffn_w_up': out['m_ffn_w_up'], 'm_ffn_conv_w': out['m_ffn_conv_w'], 'm_ffn_conv_b': out['m_ffn_conv_b'], 'm_ffn_w_down': out['m_ffn_w_down'], 'm_ada_w': out['m_ada_w'], 'm_ada_b': out['m_ada_b'], 'm_norm1_g': out['m_norm1_g'], 'm_norm2_g': out['m_norm2_g'], 'm_final_g': out['m_final_g'], 'v_fox_w_in': out['v_fox_w_in'], 'v_fox_b_f': out['v_fox_b_f'], 'v_fox_q_gain': out['v_fox_q_gain'], 'v_fox_k_gain': out['v_fox_k_gain'], 'v_fox_w_out': out['v_fox_w_out'], 'v_sgu_w_in': out['v_sgu_w_in'], 'v_sgu_b_in': out['v_sgu_b_in'], 'v_sgu_v_gain': out['v_sgu_v_gain'], 'v_sgu_v_bias': out['v_sgu_v_bias'], 'v_sgu_w_s': out['v_sgu_w_s'], 'v_sgu_b_s': out['v_sgu_b_s'], 'v_sgu_w_out': out['v_sgu_w_out'], 'v_ffn_w_up': out['v_ffn_w_up'], 'v_ffn_conv_w': out['v_ffn_conv_w'], 'v_ffn_conv_b': out['v_ffn_conv_b'], 'v_ffn_w_down': out['v_ffn_w_down'], 'v_ada_w': out['v_ada_w'], 'v_ada_b': out['v_ada_b'], 'v_norm1_g': out['v_norm1_g'], 'v_norm2_g': out['v_norm2_g'], 'v_final_g': out['v_final_g']}


def _loss(weights, diff, rest, loss_target):
    with _jax.named_scope("forward"):
        args = {**rest, TWIN_DIFF_INPUT: diff, **{k: w.astype(_WEIGHT_DTYPES[k]) for k, w in weights.items()}}
        y = _forward(args)
    with _jax.named_scope("loss_head"):
        err = _jnp.square(y.astype(_jnp.float32) - loss_target)
        return 0.5 * _jnp.sum(_jnp.mean(err, axis=-1)) if err.ndim else 0.5 * err


def _adamw(w, g, m, v):
    m = ADAM_B1 * m + (1.0 - ADAM_B1) * g
    v = ADAM_B2 * v + (1.0 - ADAM_B2) * _jnp.square(g)
    m_hat = m / (1.0 - ADAM_B1 ** ADAM_STEP)
    v_hat = v / (1.0 - ADAM_B2 ** ADAM_STEP)
    delta = -ADAM_LR * (m_hat / (_jnp.sqrt(v_hat) + ADAM_EPS) + ADAM_WD * w)
    return delta, m, v


def reference(x, c, fox_w_in, fox_b_f, fox_q_gain, fox_k_gain, fox_w_out, sgu_w_in, sgu_b_in, sgu_v_gain, sgu_v_bias, sgu_w_s, sgu_b_s, sgu_w_out, ffn_w_up, ffn_conv_w, ffn_conv_b, ffn_w_down, ada_w, ada_b, norm1_g, norm2_g, final_g, loss_target, m_fox_w_in, m_fox_b_f, m_fox_q_gain, m_fox_k_gain, m_fox_w_out, m_sgu_w_in, m_sgu_b_in, m_sgu_v_gain, m_sgu_v_bias, m_sgu_w_s, m_sgu_b_s, m_sgu_w_out, m_ffn_w_up, m_ffn_conv_w, m_ffn_conv_b, m_ffn_w_down, m_ada_w, m_ada_b, m_norm1_g, m_norm2_g, m_final_g, v_fox_w_in, v_fox_b_f, v_fox_q_gain, v_fox_k_gain, v_fox_w_out, v_sgu_w_in, v_sgu_b_in, v_sgu_v_gain, v_sgu_v_bias, v_sgu_w_s, v_sgu_b_s, v_sgu_w_out, v_ffn_w_up, v_ffn_conv_w, v_ffn_conv_b, v_ffn_w_down, v_ada_w, v_ada_b, v_norm1_g, v_norm2_g, v_final_g):
    given = dict(x=x, c=c, fox_w_in=fox_w_in, fox_b_f=fox_b_f, fox_q_gain=fox_q_gain, fox_k_gain=fox_k_gain, fox_w_out=fox_w_out, sgu_w_in=sgu_w_in, sgu_b_in=sgu_b_in, sgu_v_gain=sgu_v_gain, sgu_v_bias=sgu_v_bias, sgu_w_s=sgu_w_s, sgu_b_s=sgu_b_s, sgu_w_out=sgu_w_out, ffn_w_up=ffn_w_up, ffn_conv_w=ffn_conv_w, ffn_conv_b=ffn_conv_b, ffn_w_down=ffn_w_down, ada_w=ada_w, ada_b=ada_b, norm1_g=norm1_g, norm2_g=norm2_g, final_g=final_g, loss_target=loss_target, m_fox_w_in=m_fox_w_in, m_fox_b_f=m_fox_b_f, m_fox_q_gain=m_fox_q_gain, m_fox_k_gain=m_fox_k_gain, m_fox_w_out=m_fox_w_out, m_sgu_w_in=m_sgu_w_in, m_sgu_b_in=m_sgu_b_in, m_sgu_v_gain=m_sgu_v_gain, m_sgu_v_bias=m_sgu_v_bias, m_sgu_w_s=m_sgu_w_s, m_sgu_b_s=m_sgu_b_s, m_sgu_w_out=m_sgu_w_out, m_ffn_w_up=m_ffn_w_up, m_ffn_conv_w=m_ffn_conv_w, m_ffn_conv_b=m_ffn_conv_b, m_ffn_w_down=m_ffn_w_down, m_ada_w=m_ada_w, m_ada_b=m_ada_b, m_norm1_g=m_norm1_g, m_norm2_g=m_norm2_g, m_final_g=m_final_g, v_fox_w_in=v_fox_w_in, v_fox_b_f=v_fox_b_f, v_fox_q_gain=v_fox_q_gain, v_fox_k_gain=v_fox_k_gain, v_fox_w_out=v_fox_w_out, v_sgu_w_in=v_sgu_w_in, v_sgu_b_in=v_sgu_b_in, v_sgu_v_gain=v_sgu_v_gain, v_sgu_v_bias=v_sgu_v_bias, v_sgu_w_s=v_sgu_w_s, v_sgu_b_s=v_sgu_b_s, v_sgu_w_out=v_sgu_w_out, v_ffn_w_up=v_ffn_w_up, v_ffn_conv_w=v_ffn_conv_w, v_ffn_conv_b=v_ffn_conv_b, v_ffn_w_down=v_ffn_w_down, v_ada_w=v_ada_w, v_ada_b=v_ada_b, v_norm1_g=v_norm1_g, v_norm2_g=v_norm2_g, v_final_g=v_final_g)
    weights = {n: given[n] for n in TWIN_WEIGHTS}
    shared = {n: given[n] for n in SHARED_INPUTS}
    per_example = {n: given[n] for n in ['x', 'c']}
    grad_fn = _jax.value_and_grad(_loss, argnums=(0, 1))

    def one_microbatch(ex, loss_target):
        ex = dict(ex)
        diff = ex.pop(TWIN_DIFF_INPUT)
        return grad_fn(weights, diff, {**shared, **ex}, loss_target)

    if N_MICROBATCH == 1:
        loss, (grad_w, grad_x) = one_microbatch(per_example, given["loss_target"])
    else:
        def body(carry, xs):
            loss_sum, grad_sum = carry
            l_k, (gw_k, gx_k) = one_microbatch(xs[0], xs[1])
            with _jax.named_scope("update"):
                return (loss_sum + l_k, _jax.tree.map(_jnp.add, grad_sum, gw_k)), gx_k

        init = (_jnp.zeros((), _jnp.float32), _jax.tree.map(_jnp.zeros_like, weights))
        (loss, grad_w), grad_x = _jax.lax.scan(body, init, (per_example, given["loss_target"]))
    with _jax.named_scope("update"):
        delta_w, new_m, new_v = {}, {}, {}
        for n in TWIN_WEIGHTS:
            delta_w[n], new_m[n], new_v[n] = _adamw(weights[n], grad_w[n], given["m_" + n], given["v_" + n])
    return (loss, grad_x, *[grad_w[n] for n in TWIN_WEIGHTS], *[delta_w[n] for n in TWIN_WEIGHTS],
            *[new_m[n] for n in TWIN_WEIGHTS], *[new_v[n] for n in TWIN_WEIGHTS])
```

```python
import functools
import math

import jax
import jax.numpy as jnp
from jax import lax
from jax.experimental import pallas as pl
from jax.experimental.pallas import tpu as pltpu

F32, BF16 = jnp.float32, jnp.bfloat16
N_DEV = 8
HEADS, HEAD_DIM = 16, 64
HEAD_PAIRS = HEADS // 2
LANES = 128
EPS = 1e-6
SGU_BLOCK, SGU_GROUPS, SGU_CHUNK = 128, 8, 64
CONV_WIDTH = 3
ADAM_LR, ADAM_B1, ADAM_B2, ADAM_EPS, ADAM_WD, ADAM_STEP = 0.001, 0.9, 0.999, 1e-08, 0.01, 10
NEG = -1e30
GELU_C0, GELU_C1 = math.sqrt(2.0 / math.pi), 0.044715
MESH = pl.DeviceIdType.MESH
VMEM_LIMIT = 56 * 1024 * 1024


def _tile(dim, pref):
    if dim <= pref:
        return dim
    t = (pref // LANES) * LANES
    while t >= LANES:
        if dim % t == 0:
            return t
        t -= LANES
    return dim


def _params(*sem):
    return pltpu.CompilerParams(dimension_semantics=sem, vmem_limit_bytes=VMEM_LIMIT)


def _mm(a, b, *, name, ta=False, tb=False, out_dtype=F32, tm=1024, tn=1024, tk=1024, res=None):
    M = a.shape[1] if ta else a.shape[0]
    K = a.shape[0] if ta else a.shape[1]
    N = b.shape[0] if tb else b.shape[1]
    tm, tn, tk = _tile(M, tm), _tile(N, tn), _tile(K, tk)
    nk = K // tk
    dims = (((0 if ta else 1,), (1 if tb else 0,)), ((), ()))
    a_spec = pl.BlockSpec((tk, tm), lambda i, j, k: (k, i)) if ta else pl.BlockSpec((tm, tk), lambda i, j, k: (i, k))
    b_spec = pl.BlockSpec((tn, tk), lambda i, j, k: (j, k)) if tb else pl.BlockSpec((tk, tn), lambda i, j, k: (k, j))
    o_spec = pl.BlockSpec((tm, tn), lambda i, j, k: (i, j))

    def accumulate(a_ref, b_ref, acc):
        @pl.when(pl.program_id(2) == 0)
        def _():
            acc[...] = jnp.zeros_like(acc)
        acc[...] += lax.dot_general(a_ref[...], b_ref[...], dims, preferred_element_type=F32)

    if res is None:
        def body(a_ref, b_ref, o_ref, acc):
            accumulate(a_ref, b_ref, acc)

            @pl.when(pl.program_id(2) == nk - 1)
            def _():
                o_ref[...] = acc[...].astype(o_ref.dtype)

        return pl.pallas_call(
            body, grid=(M // tm, N // tn, nk), in_specs=[a_spec, b_spec], out_specs=o_spec,
            out_shape=jax.ShapeDtypeStruct((M, N), out_dtype), scratch_shapes=[pltpu.VMEM((tm, tn), F32)],
            compiler_params=_params("parallel", "parallel", "arbitrary"), name=name)(a, b)

    x, gate = res

    def body_res(a_ref, b_ref, x_ref, g_ref, o_ref, y_ref, acc):
        accumulate(a_ref, b_ref, acc)

        @pl.when(pl.program_id(2) == nk - 1)
        def _():
            y = acc[...]
            o_ref[...] = x_ref[...] + g_ref[...] * y
            y_ref[...] = y.astype(BF16)

    return pl.pallas_call(
        body_res, grid=(M // tm, N // tn, nk),
        in_specs=[a_spec, b_spec, o_spec, pl.BlockSpec((1, tn), lambda i, j, k: (0, j))],
        out_specs=[o_spec, o_spec],
        out_shape=[jax.ShapeDtypeStruct((M, N), F32), jax.ShapeDtypeStruct((M, N), BF16)],
        scratch_shapes=[pltpu.VMEM((tm, tn), F32)],
        compiler_params=_params("parallel", "parallel", "arbitrary"), name=name)(a, b, x, gate)


def _ada_mod(c, w, b, *, name):
    D, N = w.shape
    tn = _tile(N, 1536)
    rows = 16

    def body(c_ref, w_ref, b_ref, o_ref, ca_ref):
        cv = c_ref[...]
        ca = cv * jax.nn.sigmoid(cv)
        row0 = lax.broadcasted_iota(jnp.int32, (rows, D), 0) == 0
        ca16 = jnp.where(row0, jnp.broadcast_to(ca, (rows, D)), 0.0).astype(BF16)
        ca_ref[...] = ca16
        o_ref[...] = (jnp.dot(ca16, w_ref[...], preferred_element_type=F32) + b_ref[...])[0:8]

    out, ca = pl.pallas_call(
        body, grid=(N // tn,),
        in_specs=[pl.BlockSpec((1, D), lambda j: (0, 0)), pl.BlockSpec((D, tn), lambda j: (0, j)),
                  pl.BlockSpec((1, tn), lambda j: (0, j))],
        out_specs=[pl.BlockSpec((8, tn), lambda j: (0, j)), pl.BlockSpec((rows, D), lambda j: (0, 0))],
        out_shape=[jax.ShapeDtypeStruct((8, N), F32), jax.ShapeDtypeStruct((rows, D), BF16)],
        compiler_params=_params("arbitrary"), name=name)(c, w, b)
    return out[0:1], ca


def _norm_mod_fwd(x, g, shift, scale, *, name, ts=512):
    S, D = x.shape
    ts = _tile(S, ts)
    row = pl.BlockSpec((ts, D), lambda i: (i, 0))
    vec = pl.BlockSpec((1, D), lambda i: (0, 0))

    def body(x_ref, g_ref, sh_ref, sc_ref, h_ref):
        xv = x_ref[...]
        r = lax.rsqrt(jnp.mean(xv * xv, axis=-1, keepdims=True) + EPS)
        h_ref[...] = ((xv * r * g_ref[...]) * (1.0 + sc_ref[...]) + sh_ref[...]).astype(BF16)

    return pl.pallas_call(body, grid=(S // ts,), in_specs=[row, vec, vec, vec], out_specs=row,
                          out_shape=jax.ShapeDtypeStruct((S, D), BF16),
                          compiler_params=_params("parallel"), name=name)(x, g, shift, scale)


def _acc_init(step, *refs):
    @pl.when(step == 0)
    def _():
        for r in refs:
            r[...] = jnp.zeros_like(r)


def _colsum(v):
    return jnp.sum(v, axis=0, keepdims=True)


def _norm_mod_bwd(dh, x, g, scale, dres, prev=None, *, name, ts=512):
    S, D = x.shape
    ts = _tile(S, ts)
    row = pl.BlockSpec((ts, D), lambda i: (i, 0))
    vec = pl.BlockSpec((1, D), lambda i: (0, 0))
    has_prev = prev is not None

    def body(*refs):
        if has_prev:
            dh_ref, x_ref, g_ref, sc_ref, dres_ref, y_ref, gate_ref, dx_ref, dg_ref, dsh_ref, dsc_ref, dy_ref, dgate_ref = refs
            _acc_init(pl.program_id(0), dg_ref, dsh_ref, dsc_ref, dgate_ref)
        else:
            dh_ref, x_ref, g_ref, sc_ref, dres_ref, dx_ref, dg_ref, dsh_ref, dsc_ref = refs
            _acc_init(pl.program_id(0), dg_ref, dsh_ref, dsc_ref)
        xv, dhv, gv = x_ref[...], dh_ref[...], g_ref[...]
        r = lax.rsqrt(jnp.mean(xv * xv, axis=-1, keepdims=True) + EPS)
        xh = xv * r
        dsh_ref[...] += _colsum(dhv)
        dsc_ref[...] += _colsum(dhv * (xh * gv))
        dn = dhv * (1.0 + sc_ref[...])
        dg_ref[...] += _colsum(dn * xh)
        dxh = dn * gv
        dx = dres_ref[...] + r * (dxh - xh * jnp.mean(dxh * xh, axis=-1, keepdims=True))
        dx_ref[...] = dx
        if has_prev:
            dy_ref[...] = (gate_ref[...] * dx).astype(BF16)
            dgate_ref[...] += _colsum(dx * y_ref[...].astype(F32))

    ins, in_specs = [dh, x, g, scale, dres], [row, row, vec, vec, row]
    outs = [jax.ShapeDtypeStruct((S, D), F32)] + [jax.ShapeDtypeStruct((1, D), F32)] * 3
    out_specs = [row, vec, vec, vec]
    if has_prev:
        ins += list(prev)
        in_specs += [row, vec]
        outs += [jax.ShapeDtypeStruct((S, D), BF16), jax.ShapeDtypeStruct((1, D), F32)]
        out_specs += [row, vec]
    return pl.pallas_call(body, grid=(S // ts,), in_specs=in_specs, out_specs=out_specs, out_shape=outs,
                          compiler_params=_params("arbitrary"), name=name)(*ins)


def _final_loss(x, g, target, y, gate, *, name, ts=512):
    S, D = x.shape
    ts = _tile(S, ts)
    row = pl.BlockSpec((ts, D), lambda i: (i, 0))
    vec = pl.BlockSpec((1, D), lambda i: (0, 0))
    lvec = pl.BlockSpec((1, LANES), lambda i: (0, 0))

    def body(x_ref, g_ref, t_ref, y_ref, gate_ref, loss_ref, dg_ref, dx_ref, dy_ref, dgate_ref):
        _acc_init(pl.program_id(0), loss_ref, dg_ref, dgate_ref)
        xv, gv = x_ref[...], g_ref[...]
        r = lax.rsqrt(jnp.mean(xv * xv, axis=-1, keepdims=True) + EPS)
        xh = xv * r
        e = xh * gv - t_ref[...]
        loss_ref[...] += 0.5 * jnp.sum(jnp.mean(e * e, axis=-1, keepdims=True), axis=0, keepdims=True)
        dout = e * (1.0 / D)
        dg_ref[...] += _colsum(dout * xh)
        dxh = dout * gv
        dx = r * (dxh - xh * jnp.mean(dxh * xh, axis=-1, keepdims=True))
        dx_ref[...] = dx
        dy_ref[...] = (gate_ref[...] * dx).astype(BF16)
        dgate_ref[...] += _colsum(dx * y_ref[...].astype(F32))

    return pl.pallas_call(
        body, grid=(S // ts,), in_specs=[row, vec, row, row, vec], out_specs=[lvec, vec, row, row, vec],
        out_shape=[jax.ShapeDtypeStruct((1, LANES), F32), jax.ShapeDtypeStruct((1, D), F32),
                   jax.ShapeDtypeStruct((S, D), F32), jax.ShapeDtypeStruct((S, D), BF16),
                   jax.ShapeDtypeStruct((1, D), F32)],
        compiler_params=_params("arbitrary"), name=name)(x, g, target, y, gate)


def _head_mean(v, gmat):
    hi = v.astype(BF16)
    lo = (v - hi.astype(F32)).astype(BF16)
    return jnp.dot(hi, gmat, preferred_element_type=F32) + jnp.dot(lo, gmat, preferred_element_type=F32)


def _fox_prep_fwd(proj, qgain, kgain, gmat, *, name, ts=512):
    S = proj.shape[0]
    D = HEADS * HEAD_DIM
    ts = _tile(S, ts)
    scale = HEAD_DIM ** -0.5

    def body(p_ref, qg_ref, kg_ref, gm_ref, o_ref):
        gm = gm_ref[...]
        for half, g_ref, mult in ((0, qg_ref, scale), (1, kg_ref, 1.0)):
            for cpair in range(HEAD_PAIRS):
                cols = pl.ds(half * D + cpair * LANES, LANES)
                v = p_ref[:, cols].astype(F32)
                r = lax.rsqrt(_head_mean(v * v, gm) + EPS)
                o_ref[:, cols] = ((v * r * g_ref[...]) * mult).astype(BF16)

    vec = pl.BlockSpec((1, LANES), lambda i: (0, 0))
    return pl.pallas_call(
        body, grid=(S // ts,),
        in_specs=[pl.BlockSpec((ts, 2 * D), lambda i: (i, 0)), vec, vec, pl.BlockSpec((LANES, LANES), lambda i: (0, 0))],
        out_specs=pl.BlockSpec((ts, 2 * D), lambda i: (i, 0)),
        out_shape=jax.ShapeDtypeStruct((S, 2 * D), BF16),
        compiler_params=_params("parallel"), name=name)(proj, qgain, kgain, gmat)


def _fox_prep_bwd(proj, dqs, dkh, dv, dog, dfl, qgain, kgain, gmat, *, name, ts=512):
    S = proj.shape[0]
    D = HEADS * HEAD_DIM
    ts = _tile(S, ts)
    scale = HEAD_DIM ** -0.5

    def body(p_ref, dq_ref, dk_ref, dv_ref, dog_ref, dfl_ref, qg_ref, kg_ref, gm_ref, o_ref, dqg_ref, dkg_ref):
        _acc_init(pl.program_id(0), dqg_ref, dkg_ref)
        gm = gm_ref[...]
        for half, d_ref, g_ref, dg_ref, mult in ((0, dq_ref, qg_ref, dqg_ref, scale), (1, dk_ref, kg_ref, dkg_ref, 1.0)):
            for cpair in range(HEAD_PAIRS):
                cols = pl.ds(cpair * LANES, LANES)
                v = p_ref[:, pl.ds(half * D + cpair * LANES, LANES)].astype(F32)
                r = lax.rsqrt(_head_mean(v * v, gm) + EPS)
                xh = v * r
                dn = d_ref[:, cols].astype(F32) * mult
                dg_ref[...] += _colsum(dn * xh)
                dxh = dn * g_ref[...]
                dproj = r * (dxh - xh * _head_mean(dxh * xh, gm))
                o_ref[:, pl.ds(half * D + cpair * LANES, LANES)] = dproj.astype(BF16)
        o_ref[:, pl.ds(2 * D, D)] = dv_ref[...]
        o_ref[:, pl.ds(3 * D, D)] = dog_ref[...]
        o_ref[:, pl.ds(4 * D, LANES)] = dfl_ref[...]

    row = pl.BlockSpec((ts, D), lambda i: (i, 0))
    vec = pl.BlockSpec((1, LANES), lambda i: (0, 0))
    return pl.pallas_call(
        body, grid=(S // ts,),
        in_specs=[pl.BlockSpec((ts, 2 * D), lambda i: (i, 0)), row, row, row, row,
                  pl.BlockSpec((ts, LANES), lambda i: (i, 0)), vec, vec, pl.BlockSpec((LANES, LANES), lambda i: (0, 0))],
        out_specs=[pl.BlockSpec((ts, 4 * D + LANES), lambda i: (i, 0)), vec, vec],
        out_shape=[jax.ShapeDtypeStruct((S, 4 * D + LANES), BF16), jax.ShapeDtypeStruct((1, LANES), F32),
                   jax.ShapeDtypeStruct((1, LANES), F32)],
        compiler_params=_params("arbitrary"), name=name)(proj, dqs, dkh, dv, dog, dfl, qgain, kgain, gmat)


def _log_sigmoid(z):
    return jnp.minimum(z, 0.0) - jnp.log(1.0 + jnp.exp(-jnp.abs(z)))


def _fox_decay_fwd(fl, bf, *, name, tb=256):
    S = fl.shape[0]
    tb = _tile(S, tb)

    def body(fl_ref, b_ref, o_ref, carry):
        @pl.when(pl.program_id(0) == 0)
        def _():
            carry[...] = jnp.zeros_like(carry)
        logf = _log_sigmoid(fl_ref[...] + b_ref[...])
        tri = (lax.broadcasted_iota(jnp.int32, (tb, tb), 1) <= lax.broadcasted_iota(jnp.int32, (tb, tb), 0)).astype(F32)
        cs = jnp.dot(tri, logf, preferred_element_type=F32, precision=lax.Precision.HIGHEST) + carry[...]
        o_ref[...] = cs
        carry[...] = _row_of(cs, tb - 1)

    return pl.pallas_call(
        body, grid=(S // tb,),
        in_specs=[pl.BlockSpec((tb, LANES), lambda i: (i, 0)), pl.BlockSpec((1, LANES), lambda i: (0, 0))],
        out_specs=pl.BlockSpec((tb, LANES), lambda i: (i, 0)),
        out_shape=jax.ShapeDtypeStruct((S, LANES), F32), scratch_shapes=[pltpu.VMEM((1, LANES), F32)],
        compiler_params=_params("arbitrary"), name=name)(fl, bf)


def _fox_decay_bwd(dF, fl, bf, *, name, tb=256):
    S = fl.shape[0]
    tb = _tile(S, tb)
    n = S // tb

    def body(df_ref, fl_ref, b_ref, o_ref, db_ref, carry):
        @pl.when(pl.program_id(0) == 0)
        def _():
            carry[...] = jnp.zeros_like(carry)
            db_ref[...] = jnp.zeros_like(db_ref)
        tri = (lax.broadcasted_iota(jnp.int32, (tb, tb), 1) >= lax.broadcasted_iota(jnp.int32, (tb, tb), 0)).astype(F32)
        rc = jnp.dot(tri, df_ref[...], preferred_element_type=F32, precision=lax.Precision.HIGHEST) + carry[...]
        carry[...] = _row_of(rc, 0)
        dfl = rc * jax.nn.sigmoid(-(fl_ref[...] + b_ref[...]))
        o_ref[...] = dfl.astype(BF16)
        db_ref[...] += _colsum(dfl)

    rev = pl.BlockSpec((tb, LANES), lambda i: (n - 1 - i, 0))
    vec = pl.BlockSpec((1, LANES), lambda i: (0, 0))
    return pl.pallas_call(
        body, grid=(n,), in_specs=[rev, rev, vec], out_specs=[rev, vec],
        out_shape=[jax.ShapeDtypeStruct((S, LANES), BF16), jax.ShapeDtypeStruct((1, LANES), F32)],
        scratch_shapes=[pltpu.VMEM((1, LANES), F32)],
        compiler_params=_params("arbitrary"), name=name)(dF, fl, bf)


_NT = (((1,), (1,)), ((), ()))
_TN = (((0,), (0,)), ((), ()))


def _pair_masks():
    lo = lax.broadcasted_iota(jnp.int32, (1, LANES), 1) < HEAD_DIM
    return lo, jnp.logical_not(lo)


def _scores(qh, k2, fq, fk, causal_mask):
    s = lax.dot_general(qh, k2, _NT, preferred_element_type=F32) + fq - fk
    if causal_mask is not None:
        s = jnp.where(causal_mask, s, NEG)
    return s


def _flash_fwd(qk, proj, fcol, frow, *, name, T=512):
    S = qk.shape[0]
    D = HEADS * HEAD_DIM
    T = _tile(S, T)
    n = S // T

    def body(q_ref, k_ref, v_ref, fq_ref, fk_ref, o_ref, lse_ref, m_s, l_s, acc_s):
        i, j = pl.program_id(1), pl.program_id(2)
        sels = _pair_masks()

        @pl.when(j == 0)
        def _():
            m_s[...] = jnp.full_like(m_s, NEG)
            l_s[...] = jnp.zeros_like(l_s)
            acc_s[...] = jnp.zeros_like(acc_s)

        def step(diag):
            q2, k2, v2 = q_ref[...], k_ref[...], v_ref[...]
            mask = None
            if diag:
                mask = lax.broadcasted_iota(jnp.int32, (T, T), 1) <= lax.broadcasted_iota(jnp.int32, (T, T), 0)
            pv, alphas = None, []
            for hh in range(2):
                s = _scores(jnp.where(sels[hh], q2, 0), k2, fq_ref[hh], fk_ref[hh], mask)
                m_prev = m_s[hh]
                m_new = jnp.maximum(m_prev, jnp.max(s, axis=-1, keepdims=True))
                p = jnp.exp(s - m_new)
                alpha = jnp.exp(m_prev - m_new)
                l_s[hh] = alpha * l_s[hh] + jnp.sum(p, axis=-1, keepdims=True)
                m_s[hh] = m_new
                c = jnp.dot(p.astype(BF16), jnp.where(sels[hh], v2, 0), preferred_element_type=F32)
                pv = c if pv is None else pv + c
                alphas.append(alpha)
            acc_s[...] = acc_s[...] * jnp.where(sels[0], alphas[0], alphas[1]) + pv

        @pl.when(j < i)
        def _():
            step(False)

        @pl.when(j == i)
        def _():
            step(True)
            o_ref[...] = acc_s[...] / jnp.where(sels[0], l_s[0], l_s[1])
            for hh in range(2):
                lse_ref[hh] = m_s[hh] + jnp.log(l_s[hh])

    kv = lambda off: pl.BlockSpec((T, LANES), lambda h, i, j: (jnp.minimum(j, i), off + h))
    return pl.pallas_call(
        body, grid=(HEAD_PAIRS, n, n),
        in_specs=[pl.BlockSpec((T, LANES), lambda h, i, j: (i, h)), kv(HEAD_PAIRS), kv(2 * HEAD_PAIRS),
                  pl.BlockSpec((2, T, 1), lambda h, i, j: (h, i, 0)),
                  pl.BlockSpec((2, 1, T), lambda h, i, j: (h, 0, jnp.minimum(j, i)))],
        out_specs=[pl.BlockSpec((T, LANES), lambda h, i, j: (i, h)), pl.BlockSpec((2, T, 1), lambda h, i, j: (h, i, 0))],
        out_shape=[jax.ShapeDtypeStruct((S, D), F32), jax.ShapeDtypeStruct((HEADS, S, 1), F32)],
        scratch_shapes=[pltpu.VMEM((2, T, 1), F32), pltpu.VMEM((2, T, 1), F32), pltpu.VMEM((T, LANES), F32)],
        compiler_params=_params("parallel", "parallel", "arbitrary"), name=name)(qk, qk, proj, fcol, frow)


def _flash_bwd_dq(qk, proj, datt, lse, delta, fcol, frow, *, name, T=512):
    S = qk.shape[0]
    D = HEADS * HEAD_DIM
    T = _tile(S, T)
    n = S // T

    def body(q_ref, k_ref, v_ref, do_ref, lse_ref, dl_ref, fq_ref, fk_ref, dq_ref, dfq_ref, acc_s, df_s):
        i, j = pl.program_id(1), pl.program_id(2)
        sels = _pair_masks()

        @pl.when(j == 0)
        def _():
            acc_s[...] = jnp.zeros_like(acc_s)
            df_s[...] = jnp.zeros_like(df_s)

        def step(diag):
            q2, k2, v2, do2 = q_ref[...], k_ref[...], v_ref[...], do_ref[...]
            mask = None
            if diag:
                mask = lax.broadcasted_iota(jnp.int32, (T, T), 1) <= lax.broadcasted_iota(jnp.int32, (T, T), 0)
            upd = None
            for hh in range(2):
                s = _scores(jnp.where(sels[hh], q2, 0), k2, fq_ref[hh], fk_ref[hh], mask)
                p = jnp.exp(s - lse_ref[hh])
                dp = lax.dot_general(jnp.where(sels[hh], do2, 0), v2, _NT, preferred_element_type=F32)
                ds = p * (dp - dl_ref[hh])
                c = jnp.dot(ds.astype(BF16), jnp.where(sels[hh], k2, 0), preferred_element_type=F32)
                upd = c if upd is None else upd + c
                df_s[hh] += jnp.sum(ds, axis=-1, keepdims=True)
            acc_s[...] += upd

        @pl.when(j < i)
        def _():
            step(False)

        @pl.when(j == i)
        def _():
            step(True)
            dq_ref[...] = acc_s[...].astype(BF16)
            dfq_ref[...] = df_s[...]

    qrow = lambda off: pl.BlockSpec((T, LANES), lambda h, i, j: (i, off + h))
    kv = lambda off: pl.BlockSpec((T, LANES), lambda h, i, j: (jnp.minimum(j, i), off + h))
    col = pl.BlockSpec((2, T, 1), lambda h, i, j: (h, i, 0))
    return pl.pallas_call(
        body, grid=(HEAD_PAIRS, n, n),
        in_specs=[qrow(0), kv(HEAD_PAIRS), kv(2 * HEAD_PAIRS), qrow(0), col, col, col,
                  pl.BlockSpec((2, 1, T), lambda h, i, j: (h, 0, jnp.minimum(j, i)))],
        out_specs=[qrow(0), col],
        out_shape=[jax.ShapeDtypeStruct((S, D), BF16), jax.ShapeDtypeStruct((HEADS, S, 1), F32)],
        scratch_shapes=[pltpu.VMEM((T, LANES), F32), pltpu.VMEM((2, T, 1), F32)],
        compiler_params=_params("parallel", "parallel", "arbitrary"), name=name)(qk, qk, proj, datt, lse, delta, fcol, frow)


def _flash_bwd_dkv(qk, proj, datt, lse, delta, fcol, frow, *, name, T=512):
    S = qk.shape[0]
    D = HEADS * HEAD_DIM
    T = _tile(S, T)
    n = S // T

    def body(q_ref, k_ref, v_ref, do_ref, lse_ref, dl_ref, fq_ref, fk_ref, dk_ref, dv_ref, dfk_ref, dk_s, dv_s, df_s):
        jb, ib = pl.program_id(1), pl.program_id(2)
        sels = _pair_masks()

        @pl.when(ib == 0)
        def _():
            dk_s[...] = jnp.zeros_like(dk_s)
            dv_s[...] = jnp.zeros_like(dv_s)
            df_s[...] = jnp.zeros_like(df_s)

        def step(diag):
            q2, k2, v2, do2 = q_ref[...], k_ref[...], v_ref[...], do_ref[...]
            mask = None
            if diag:
                mask = lax.broadcasted_iota(jnp.int32, (T, T), 1) <= lax.broadcasted_iota(jnp.int32, (T, T), 0)
            dk_u, dv_u = None, None
            for hh in range(2):
                qh = jnp.where(sels[hh], q2, 0)
                doh = jnp.where(sels[hh], do2, 0)
                s = _scores(qh, k2, fq_ref[hh], fk_ref[hh], mask)
                p = jnp.exp(s - lse_ref[hh])
                dp = lax.dot_general(doh, v2, _NT, preferred_element_type=F32)
                ds = p * (dp - dl_ref[hh])
                cv = lax.dot_general(p.astype(BF16), doh, _TN, preferred_element_type=F32)
                ck = lax.dot_general(ds.astype(BF16), qh, _TN, preferred_element_type=F32)
                dv_u = cv if dv_u is None else dv_u + cv
                dk_u = ck if dk_u is None else dk_u + ck
                df_s[hh] -= jnp.sum(ds, axis=0, keepdims=True)
            dk_s[...] += dk_u
            dv_s[...] += dv_u

        @pl.when(ib > jb)
        def _():
            step(False)

        @pl.when(ib == jb)
        def _():
            step(True)

        @pl.when(ib == n - 1)
        def _():
            dk_ref[...] = dk_s[...].astype(BF16)
            dv_ref[...] = dv_s[...].astype(BF16)
            dfk_ref[...] = df_s[...]

    qrow = lambda off: pl.BlockSpec((T, LANES), lambda h, jb, ib: (jnp.maximum(ib, jb), off + h))
    kv = lambda off: pl.BlockSpec((T, LANES), lambda h, jb, ib: (jb, off + h))
    col = pl.BlockSpec((2, T, 1), lambda h, jb, ib: (h, jnp.maximum(ib, jb), 0))
    out_kv = pl.BlockSpec((T, LANES), lambda h, jb, ib: (jb, h))
    frow_spec = pl.BlockSpec((2, 1, T), lambda h, jb, ib: (h, 0, jb))
    return pl.pallas_call(
        body, grid=(HEAD_PAIRS, n, n),
        in_specs=[qrow(0), kv(HEAD_PAIRS), kv(2 * HEAD_PAIRS), qrow(0), col, col, col, frow_spec],
        out_specs=[out_kv, out_kv, frow_spec],
        out_shape=[jax.ShapeDtypeStruct((S, D), BF16), jax.ShapeDtypeStruct((S, D), BF16),
                   jax.ShapeDtypeStruct((HEADS, 1, S), F32)],
        scratch_shapes=[pltpu.VMEM((T, LANES), F32), pltpu.VMEM((T, LANES), F32), pltpu.VMEM((2, 1, T), F32)],
        compiler_params=_params("parallel", "parallel", "arbitrary"), name=name)(qk, qk, proj, datt, lse, delta, fcol, frow)


def _fox_gate_fwd(att, proj, *, name, ts=512):
    S, D = att.shape
    ts = _tile(S, ts)

    def body(a_ref, o_ref, out_ref):
        out_ref[...] = (a_ref[...].astype(F32) * jax.nn.sigmoid(o_ref[...].astype(F32))).astype(BF16)

    row = pl.BlockSpec((ts, D), lambda i: (i, 0))
    return pl.pallas_call(body, grid=(S // ts,), in_specs=[row, pl.BlockSpec((ts, D), lambda i: (i, 3))], out_specs=row,
                          out_shape=jax.ShapeDtypeStruct((S, D), BF16),
                          compiler_params=_params("parallel"), name=name)(att, proj)


def _fox_gate_bwd(da, att, proj, *, name, ts=512):
    S, D = att.shape
    ts = _tile(S, ts)

    def body(da_ref, a_ref, o_ref, datt_ref, dog_ref, dl_ref):
        sels = _pair_masks()
        dav, av = da_ref[...].astype(F32), a_ref[...].astype(F32)
        sg = jax.nn.sigmoid(o_ref[...].astype(F32))
        datt = (dav * sg).astype(BF16)
        datt_ref[...] = datt
        dog_ref[...] = (dav * av * sg * (1.0 - sg)).astype(BF16)
        prod = datt.astype(F32) * av
        for cpair in range(HEAD_PAIRS):
            pc = prod[:, cpair * LANES:(cpair + 1) * LANES]
            for hh in range(2):
                dl_ref[2 * cpair + hh] = jnp.sum(jnp.where(sels[hh], pc, 0.0), axis=-1, keepdims=True)

    row = pl.BlockSpec((ts, D), lambda i: (i, 0))
    return pl.pallas_call(
        body, grid=(S // ts,), in_specs=[row, row, pl.BlockSpec((ts, D), lambda i: (i, 3))],
        out_specs=[row, row, pl.BlockSpec((HEADS, ts, 1), lambda i: (0, i, 0))],
        out_shape=[jax.ShapeDtypeStruct((S, D), BF16), jax.ShapeDtypeStruct((S, D), BF16),
                   jax.ShapeDtypeStruct((HEADS, S, 1), F32)],
        compiler_params=_params("parallel"), name=name)(da, att, proj)


def _row_of(block, r):
    rows = lax.broadcasted_iota(jnp.int32, block.shape, 0)
    return jnp.sum(jnp.where(rows == r, block, 0.0), axis=0, keepdims=True)


def _shift_down(cur, tail, k):
    out = pltpu.roll(cur, k, 0)
    rows = lax.broadcasted_iota(jnp.int32, cur.shape, 0)
    for r in range(k):
        out = jnp.where(rows == r, _row_of(tail, tail.shape[0] - k + r), out)
    return out


def _shift_up(cur, head, k):
    n = cur.shape[0]
    out = pltpu.roll(cur, n - k, 0)
    rows = lax.broadcasted_iota(jnp.int32, cur.shape, 0)
    for r in range(k):
        out = jnp.where(rows == n - k + r, _row_of(head, r), out)
    return out


HALO = 16


def _conv_taps(a_ref, tail_ref, first):
    cur = a_ref[...].astype(F32)
    tail = jnp.where(first, 0.0, tail_ref[...].astype(F32))
    return cur, _shift_down(cur, tail, 1), _shift_down(cur, tail, 2)


def _conv_gate_fwd(a, cw, cb, *, name, ts=256, tc=1408):
    S, F2 = a.shape
    F = F2 // 2
    ts, tc = _tile(S, ts), _tile(F, tc)
    nc = F // tc
    sub = ts // HALO

    def body(g_ref, gt_ref, v_ref, vt_ref, wg_ref, wv_ref, bg_ref, bv_ref, o_ref):
        first = pl.program_id(1) == 0
        halves = []
        for a_ref, t_ref, w_ref, b_ref in ((g_ref, gt_ref, wg_ref, bg_ref), (v_ref, vt_ref, wv_ref, bv_ref)):
            a0, a1, a2 = _conv_taps(a_ref, t_ref, first)
            halves.append(a2 * w_ref[0:1, :] + a1 * w_ref[1:2, :] + a0 * w_ref[2:3, :] + b_ref[...])
        g, val = halves
        o_ref[...] = (g * jax.nn.sigmoid(g) * val).astype(BF16)

    cur = lambda off: pl.BlockSpec((ts, tc), lambda j, i: (i, off + j))
    tail = lambda off: pl.BlockSpec((HALO, tc), lambda j, i: (jnp.maximum(i * sub - 1, 0), off + j))
    wsp = lambda off: pl.BlockSpec((CONV_WIDTH, tc), lambda j, i: (0, off + j))
    bsp = lambda off: pl.BlockSpec((1, tc), lambda j, i: (0, off + j))
    return pl.pallas_call(
        body, grid=(nc, S // ts),
        in_specs=[cur(0), tail(0), cur(nc), tail(nc), wsp(0), wsp(nc), bsp(0), bsp(nc)],
        out_specs=pl.BlockSpec((ts, tc), lambda j, i: (i, j)),
        out_shape=jax.ShapeDtypeStruct((S, F), BF16),
        compiler_params=_params("parallel", "parallel"), name=name)(a, a, a, a, cw, cw, cb, cb)


def _conv_gate_bwd_pre(a, dact, cw, cb, *, name, ts=256, tc=1408):
    S, F2 = a.shape
    F = F2 // 2
    ts, tc = _tile(S, ts), _tile(F, tc)
    nc = F // tc
    sub = ts // HALO

    def body(g_ref, gt_ref, v_ref, vt_ref, d_ref, wg_ref, wv_ref, bg_ref, bv_ref, dg_ref, dv_ref, sg_ref, sv_ref):
        first = pl.program_id(1) == 0
        _acc_init(pl.program_id(1), sg_ref, sv_ref)
        taps, pre = [], []
        for a_ref, t_ref, w_ref, b_ref in ((g_ref, gt_ref, wg_ref, bg_ref), (v_ref, vt_ref, wv_ref, bv_ref)):
            a0, a1, a2 = _conv_taps(a_ref, t_ref, first)
            taps.append((a2, a1, a0))
            pre.append(a2 * w_ref[0:1, :] + a1 * w_ref[1:2, :] + a0 * w_ref[2:3, :] + b_ref[...])
        g, val = pre
        d = d_ref[...].astype(F32)
        sg = jax.nn.sigmoid(g)
        dg = d * val * (sg * (1.0 + g * (1.0 - sg)))
        dval = d * (g * sg)
        for dd, o_ref, s_ref, tp in ((dg, dg_ref, sg_ref, taps[0]), (dval, dv_ref, sv_ref, taps[1])):
            o_ref[...] = dd.astype(BF16)
            for r in range(CONV_WIDTH):
                s_ref[r:r + 1, :] += _colsum(dd * tp[r])
            s_ref[CONV_WIDTH:CONV_WIDTH + 1, :] += _colsum(dd)

    cur = lambda off: pl.BlockSpec((ts, tc), lambda j, i: (i, off + j))
    tail = lambda off: pl.BlockSpec((HALO, tc), lambda j, i: (jnp.maximum(i * sub - 1, 0), off + j))
    wsp = lambda off: pl.BlockSpec((CONV_WIDTH, tc), lambda j, i: (0, off + j))
    bsp = lambda off: pl.BlockSpec((1, tc), lambda j, i: (0, off + j))
    acc = lambda off: pl.BlockSpec((8, tc), lambda j, i: (0, off + j))
    dpre_g, dpre_v, sums_g, sums_v = pl.pallas_call(
        body, grid=(nc, S // ts),
        in_specs=[cur(0), tail(0), cur(nc), tail(nc), cur(0), wsp(0), wsp(nc), bsp(0), bsp(nc)],
        out_specs=[cur(0), cur(0), acc(0), acc(0)],
        out_shape=[jax.ShapeDtypeStruct((S, F), BF16), jax.ShapeDtypeStruct((S, F), BF16),
                   jax.ShapeDtypeStruct((8, F), F32), jax.ShapeDtypeStruct((8, F), F32)],
        compiler_params=_params("parallel", "arbitrary"), name=name)(a, a, a, a, dact, cw, cw, cb, cb)
    return dpre_g, dpre_v, sums_g, sums_v


def _conv_bwd_taps(dpre_g, dpre_v, cw, *, name, ts=256, tc=1408):
    S, F = dpre_g.shape
    ts, tc = _tile(S, ts), _tile(F, tc)
    nc = F // tc
    sub = ts // HALO
    last = S // ts - 1

    def one_half(dpre, w_off, call_name):
        def body(d_ref, h_ref, w_ref, o_ref):
            cur = d_ref[...].astype(F32)
            head = jnp.where(pl.program_id(1) == last, 0.0, h_ref[...].astype(F32))
            o_ref[...] = (cur * w_ref[2:3, :] + _shift_up(cur, head, 1) * w_ref[1:2, :]
                          + _shift_up(cur, head, 2) * w_ref[0:1, :]).astype(BF16)

        return pl.pallas_call(
            body, grid=(nc, S // ts),
            in_specs=[pl.BlockSpec((ts, tc), lambda j, i: (i, j)),
                      pl.BlockSpec((HALO, tc), lambda j, i: (jnp.minimum((i + 1) * sub, S // HALO - 1), j)),
                      pl.BlockSpec((CONV_WIDTH, tc), lambda j, i: (0, w_off + j))],
            out_specs=pl.BlockSpec((ts, tc), lambda j, i: (i, j)),
            out_shape=jax.ShapeDtypeStruct((S, F), BF16),
            compiler_params=_params("parallel", "parallel"), name=call_name)(dpre, dpre, cw)

    return one_half(dpre_g, 0, f"{name}_g"), one_half(dpre_v, nc, f"{name}_v")


def _gelu_parts(z):
    z2 = z * z
    t = jnp.tanh(GELU_C0 * (z + GELU_C1 * z * z2))
    val = 0.5 * z * (1.0 + t)
    grad = 0.5 * (1.0 + t) + 0.5 * z * (1.0 - t * t) * GELU_C0 * (1.0 + 3.0 * GELU_C1 * z2)
    return val, grad


def _sgu_fwd(pre, b_in, vgain, vbias, wm, bsb, *, name, ts=256):
    S, W2 = pre.shape
    W = W2 // 2
    gd = W // SGU_GROUPS
    ts = _tile(S, ts)

    def body(p_ref, b_ref, vg_ref, vb_ref, wm_ref, bs_ref, y_ref):
        u = _gelu_parts(p_ref[:, pl.ds(0, W)].astype(F32) + b_ref[:, pl.ds(0, W)])[0]
        v = _gelu_parts(p_ref[:, pl.ds(W, W)].astype(F32) + b_ref[:, pl.ds(W, W)])[0]
        mu = jnp.mean(v, axis=-1, keepdims=True)
        vc = v - mu
        rstd = lax.rsqrt(jnp.mean(vc * vc, axis=-1, keepdims=True) + EPS)
        vn = ((vc * rstd) * vg_ref[...] + vb_ref[...]).astype(BF16)
        for blk in range(ts // SGU_BLOCK):
            r0 = blk * SGU_BLOCK
            for g in range(SGU_GROUPS):
                c0 = g * gd
                mixed = jnp.dot(wm_ref[g], vn[r0:r0 + SGU_BLOCK, c0:c0 + gd], preferred_element_type=F32) + bs_ref[g]
                y_ref[pl.ds(r0, SGU_BLOCK), pl.ds(c0, gd)] = (u[r0:r0 + SGU_BLOCK, c0:c0 + gd] * mixed).astype(BF16)

    full = lambda shape: pl.BlockSpec(shape, lambda i: (0,) * len(shape))
    return pl.pallas_call(
        body, grid=(S // ts,),
        in_specs=[pl.BlockSpec((ts, W2), lambda i: (i, 0)), full((1, W2)), full((1, W)), full((1, W)),
                  full((SGU_GROUPS, SGU_BLOCK, SGU_BLOCK)), full((SGU_GROUPS, SGU_BLOCK, gd))],
        out_specs=pl.BlockSpec((ts, W), lambda i: (i, 0)), out_shape=jax.ShapeDtypeStruct((S, W), BF16),
        compiler_params=_params("parallel"), name=name)(pre, b_in, vgain, vbias, wm, bsb)


def _sgu_bwd(pre, dy, b_in, vgain, vbias, wm, wmt, bsb, *, name, ts=256):
    S, W2 = pre.shape
    W = W2 // 2
    gd = W // SGU_GROUPS
    ts = _tile(S, ts)
    last = S // ts - 1

    def body(p_ref, dy_ref, b_ref, vg_ref, vb_ref, wm_ref, wmt_ref, bs_ref,
             dp_ref, db_ref, dvg_ref, dvb_ref, dws_ref, dbs_ref, du_s, dvn_s, dbs_s):
        step = pl.program_id(0)
        _acc_init(step, db_ref, dvg_ref, dvb_ref, dws_ref, dbs_s)
        u, gu = _gelu_parts(p_ref[:, pl.ds(0, W)].astype(F32) + b_ref[:, pl.ds(0, W)])
        v, gv = _gelu_parts(p_ref[:, pl.ds(W, W)].astype(F32) + b_ref[:, pl.ds(W, W)])
        mu = jnp.mean(v, axis=-1, keepdims=True)
        vc = v - mu
        rstd = lax.rsqrt(jnp.mean(vc * vc, axis=-1, keepdims=True) + EPS)
        vhat = vc * rstd
        vn = (vhat * vg_ref[...] + vb_ref[...]).astype(BF16)
        dyv = dy_ref[...].astype(F32)
        for blk in range(ts // SGU_BLOCK):
            r0 = blk * SGU_BLOCK
            for g in range(SGU_GROUPS):
                c0 = g * gd
                vn_g = vn[r0:r0 + SGU_BLOCK, c0:c0 + gd]
                dy_g = dyv[r0:r0 + SGU_BLOCK, c0:c0 + gd]
                mixed = jnp.dot(wm_ref[g], vn_g, preferred_element_type=F32) + bs_ref[g]
                dmix = dy_g * u[r0:r0 + SGU_BLOCK, c0:c0 + gd]
                dmix_b = dmix.astype(BF16)
                du_s[pl.ds(r0, SGU_BLOCK), pl.ds(c0, gd)] = dy_g * mixed
                dvn_s[pl.ds(r0, SGU_BLOCK), pl.ds(c0, gd)] = jnp.dot(wmt_ref[g], dmix_b, preferred_element_type=F32)
                dws_ref[g] += lax.dot_general(dmix_b, vn_g, _NT, preferred_element_type=F32)
                dbs_s[g] += dmix
        dvn = dvn_s[...]
        dvg_ref[...] += _colsum(dvn * vhat)
        dvb_ref[...] += _colsum(dvn)
        dvh = dvn * vg_ref[...]
        dv = rstd * (dvh - jnp.mean(dvh, axis=-1, keepdims=True) - vhat * jnp.mean(dvh * vhat, axis=-1, keepdims=True))
        dpu = du_s[...] * gu
        dpv = dv * gv
        dp_ref[:, pl.ds(0, W)] = dpu.astype(BF16)
        dp_ref[:, pl.ds(W, W)] = dpv.astype(BF16)
        db_ref[:, pl.ds(0, W)] += _colsum(dpu)
        db_ref[:, pl.ds(W, W)] += _colsum(dpv)

        @pl.when(step == last)
        def _():
            for g in range(SGU_GROUPS):
                dbs_ref[g] = jnp.broadcast_to(jnp.sum(dbs_s[g], axis=-1, keepdims=True), (SGU_BLOCK, SGU_BLOCK))

    full = lambda shape: pl.BlockSpec(shape, lambda i: (0,) * len(shape))
    gsq = (SGU_GROUPS, SGU_BLOCK, SGU_BLOCK)
    return pl.pallas_call(
        body, grid=(S // ts,),
        in_specs=[pl.BlockSpec((ts, W2), lambda i: (i, 0)), pl.BlockSpec((ts, W), lambda i: (i, 0)),
                  full((1, W2)), full((1, W)), full((1, W)), full(gsq), full(gsq), full((SGU_GROUPS, SGU_BLOCK, gd))],
        out_specs=[pl.BlockSpec((ts, W2), lambda i: (i, 0)), full((1, W2)), full((1, W)), full((1, W)), full(gsq), full(gsq)],
        out_shape=[jax.ShapeDtypeStruct((S, W2), BF16), jax.ShapeDtypeStruct((1, W2), F32),
                   jax.ShapeDtypeStruct((1, W), F32), jax.ShapeDtypeStruct((1, W), F32),
                   jax.ShapeDtypeStruct(gsq, F32), jax.ShapeDtypeStruct(gsq, F32)],
        scratch_shapes=[pltpu.VMEM((ts, W), F32), pltpu.VMEM((ts, W), F32), pltpu.VMEM((SGU_GROUPS, SGU_BLOCK, gd), F32)],
        compiler_params=_params("arbitrary"), name=name)(pre, dy, b_in, vgain, vbias, wm, wmt, bsb)


def _ffn_fwd(x, mods, n2g, w_up, cw, cb, w_down, tag):
    sh, sc, gate = mods
    h = _norm_mod_fwd(x, n2g, sh, sc, name=f"{tag}_norm_fwd")
    a = _mm(h, w_up, out_dtype=BF16, tn=1408, name=f"{tag}_up")
    act = _conv_gate_fwd(a, cw, cb, name=f"{tag}_conv_fwd")
    x_out, y = _mm(act, w_down, tk=1408, tm=512, res=(x, gate), name=f"{tag}_down")
    return x_out, (x, h, a, act, y)


def _ffn_bwd(dy, saved, mods, n2g, w_up, cw, cb, w_down, dres, prev, tag):
    x, h, a, act, _ = saved
    sh, sc, gate = mods
    dact = _mm(dy, w_down, tb=True, out_dtype=BF16, tn=1408, name=f"{tag}_down_dx")
    dw_down = _mm(act, dy, ta=True, out_dtype=BF16, tn=1024, name=f"{tag}_down_dw")
    dpre_g, dpre_v, sums_g, sums_v = _conv_gate_bwd_pre(a, dact, cw, cb, name=f"{tag}_conv_bwd")
    da_g, da_v = _conv_bwd_taps(dpre_g, dpre_v, cw, name=f"{tag}_conv_taps")
    da = jnp.concatenate([da_g, da_v], axis=1)
    dh = _mm(da, w_up, tb=True, tk=1408, name=f"{tag}_up_dx")
    dw_up = _mm(h, da, ta=True, out_dtype=BF16, tn=1408, name=f"{tag}_up_dw")
    outs = _norm_mod_bwd(dh, x, n2g, sc, dres, prev, name=f"{tag}_norm_bwd")
    sums = jnp.concatenate([sums_g, sums_v], axis=1)
    return outs, dict(w_up=dw_up, w_down=dw_down, conv_w=sums[0:CONV_WIDTH], conv_b=sums[CONV_WIDTH])


def _local_step(x, c, target, w):
    S, D = x.shape
    lane = jnp.arange(LANES)
    gmat = jnp.where((lane[:, None] // HEAD_DIM) == (lane[None, :] // HEAD_DIM), 1.0 / HEAD_DIM, 0.0).astype(BF16)
    qg2 = jnp.tile(w["fox_q_gain"].reshape(1, HEAD_DIM), (1, 2))
    kg2 = jnp.tile(w["fox_k_gain"].reshape(1, HEAD_DIM), (1, 2))
    bf_pad = jnp.pad(w["fox_b_f"].reshape(1, HEADS), ((0, 0), (0, LANES - HEADS)))
    w_in_pad = jnp.pad(w["fox_w_in"], ((0, 0), (0, 4 * D + LANES - w["fox_w_in"].shape[1])))
    w_qkvo, w_f = w_in_pad[:, :4 * D], w_in_pad[:, 4 * D:]
    tpos = jnp.arange(SGU_BLOCK)
    smask = (tpos[None, :] // SGU_CHUNK) <= (tpos[:, None] // SGU_CHUNK)
    wm32 = jnp.where(smask[None], w["sgu_w_s"], 0.0)
    wm, wmt = wm32.astype(BF16), jnp.swapaxes(wm32, 1, 2).astype(BF16)
    gd = w["sgu_v_gain"].shape[-1] // SGU_GROUPS
    bsb = jnp.broadcast_to(w["sgu_b_s"][:, :, None], (SGU_GROUPS, SGU_BLOCK, gd))
    vec = lambda v: v.reshape(1, -1)

    mods, cas = [], []
    for i in range(2):
        m, ca = _ada_mod(c, w["ada_w"][i], vec(w["ada_b"][i]), name=f"ada_mod_{i}")
        mods.append([m[:, k * D:(k + 1) * D] for k in range(6)])
        cas.append(ca)

    sh1, sc1, g1 = mods[0][0:3]
    h0 = _norm_mod_fwd(x, vec(w["norm1_g"][0]), sh1, sc1, name="fox_norm_fwd")
    proj = _mm(h0, w_qkvo, out_dtype=BF16, name="fox_proj")
    fl = _mm(h0, w_f, name="fox_forget_proj")
    qk = _fox_prep_fwd(proj, qg2, kg2, gmat, name="fox_qk_norm")
    Fcum = _fox_decay_fwd(fl, bf_pad, name="fox_decay")
    Fh = Fcum[:, :HEADS].T
    fcol, frow = Fh[:, :, None], Fh[:, None, :]
    att, lse = _flash_fwd(qk, proj, fcol, frow, name="fox_attn_fwd")
    ag = _fox_gate_fwd(att, proj, name="fox_gate_fwd")
    x1, y_fox = _mm(ag, w["fox_w_out"], tm=512, res=(x, g1), name="fox_out")
    x2, ffn0 = _ffn_fwd(x1, mods[0][3:6], vec(w["norm2_g"][0]), w["ffn_w_up"][0], w["ffn_conv_w"][0],
                        vec(w["ffn_conv_b"][0]), w["ffn_w_down"][0], "ffn0")

    sh1b, sc1b, g1b = mods[1][0:3]
    h1 = _norm_mod_fwd(x2, vec(w["norm1_g"][1]), sh1b, sc1b, name="sgu_norm_fwd")
    pre = _mm(h1, w["sgu_w_in"], out_dtype=BF16, name="sgu_in")
    b_in, vg, vb = vec(w["sgu_b_in"]), vec(w["sgu_v_gain"]), vec(w["sgu_v_bias"])
    ys = _sgu_fwd(pre, b_in, vg, vb, wm, bsb, name="sgu_core_fwd")
    x3, y_sgu = _mm(ys, w["sgu_w_out"], tm=512, res=(x2, g1b), name="sgu_out")
    x4, ffn1 = _ffn_fwd(x3, mods[1][3:6], vec(w["norm2_g"][1]), w["ffn_w_up"][1], w["ffn_conv_w"][1],
                        vec(w["ffn_conv_b"][1]), w["ffn_w_down"][1], "ffn1")

    loss, d_final_g, dx4, dy_ffn1, dgate_ffn1 = _final_loss(x4, vec(w["final_g"]), target, ffn1[4], mods[1][5], name="final_loss")

    (dx3, dn2g_1, dsh2_1, dsc2_1, dy_sgu, dgate_sgu), g_ffn1 = _ffn_bwd(
        dy_ffn1, ffn1, mods[1][3:6], vec(w["norm2_g"][1]), w["ffn_w_up"][1], w["ffn_conv_w"][1], vec(w["ffn_conv_b"][1]),
        w["ffn_w_down"][1], dx4, (y_sgu, g1b), "ffn1")

    dys = _mm(dy_sgu, w["sgu_w_out"], tb=True, out_dtype=BF16, name="sgu_out_dx")
    dw_sgu_out = _mm(ys, dy_sgu, ta=True, out_dtype=BF16, name="sgu_out_dw")
    dpre, db_in, dvg, dvb, dws, dbs = _sgu_bwd(pre, dys, b_in, vg, vb, wm, wmt, bsb, name="sgu_core_bwd")
    dh1 = _mm(dpre, w["sgu_w_in"], tb=True, name="sgu_in_dx")
    dw_sgu_in = _mm(h1, dpre, ta=True, out_dtype=BF16, name="sgu_in_dw")
    dx2, dn1g_1, dsh1_1, dsc1_1, dy_ffn0, dgate_ffn0 = _norm_mod_bwd(
        dh1, x2, vec(w["norm1_g"][1]), sc1b, dx3, (ffn0[4], mods[0][5]), name="sgu_norm_bwd")

    (dx1, dn2g_0, dsh2_0, dsc2_0, dy_fox, dgate_fox), g_ffn0 = _ffn_bwd(
        dy_ffn0, ffn0, mods[0][3:6], vec(w["norm2_g"][0]), w["ffn_w_up"][0], w["ffn_conv_w"][0], vec(w["ffn_conv_b"][0]),
        w["ffn_w_down"][0], dx2, (y_fox, g1), "ffn0")

    dag = _mm(dy_fox, w["fox_w_out"], tb=True, out_dtype=BF16, name="fox_out_dx")
    dw_fox_out = _mm(ag, dy_fox, ta=True, out_dtype=BF16, name="fox_out_dw")
    datt, dog, delta = _fox_gate_bwd(dag, att, proj, name="fox_gate_bwd")
    dqs, dfq = _flash_bwd_dq(qk, proj, datt, lse, delta, fcol, frow, name="fox_attn_bwd_dq")
    dkh, dv, dfk = _flash_bwd_dkv(qk, proj, datt, lse, delta, fcol, frow, name="fox_attn_bwd_dkv")
    dF = jnp.pad((dfq[:, :, 0] + dfk[:, 0, :]).T, ((0, 0), (0, LANES - HEADS)))
    dfl, dbf = _fox_decay_bwd(dF, fl, bf_pad, name="fox_decay_bwd")
    dproj, dqg, dkg = _fox_prep_bwd(proj, dqs, dkh, dv, dog, dfl, qg2, kg2, gmat, name="fox_qk_norm_bwd")
    dh0 = _mm(dproj, w_in_pad, tb=True, tk=1408, name="fox_proj_dx")
    dw_fox_in = _mm(h0, dproj, ta=True, out_dtype=BF16, tn=1408, name="fox_proj_dw")
    dx0, dn1g_0, dsh1_0, dsc1_0 = _norm_mod_bwd(dh0, x, vec(w["norm1_g"][0]), sc1, dx1, None, name="fox_norm_bwd")

    dmod0 = jnp.concatenate([dsh1_0, dsc1_0, dgate_fox, dsh2_0, dsc2_0, dgate_ffn0], axis=1)
    dmod1 = jnp.concatenate([dsh1_1, dsc1_1, dgate_sgu, dsh2_1, dsc2_1, dgate_ffn1], axis=1)
    d_ada_w = []
    for i, dm in enumerate((dmod0, dmod1)):
        dm16 = jnp.pad(dm, ((0, 15), (0, 0))).astype(BF16)
        d_ada_w.append(_mm(cas[i], dm16, ta=True, out_dtype=BF16, tn=1536, name=f"ada_dw_{i}"))

    grads = dict(
        fox_w_in=dw_fox_in[:, :w["fox_w_in"].shape[1]],
        fox_b_f=dbf[0, :HEADS],
        fox_q_gain=dqg[0, :HEAD_DIM] + dqg[0, HEAD_DIM:],
        fox_k_gain=dkg[0, :HEAD_DIM] + dkg[0, HEAD_DIM:],
        fox_w_out=dw_fox_out,
        sgu_w_in=dw_sgu_in, sgu_b_in=db_in[0], sgu_v_gain=dvg[0], sgu_v_bias=dvb[0],
        sgu_w_s=jnp.where(smask[None], dws, 0.0), sgu_b_s=dbs[:, :, 0], sgu_w_out=dw_sgu_out,
        ffn_w_up=jnp.stack([g_ffn0["w_up"], g_ffn1["w_up"]]),
        ffn_conv_w=jnp.stack([g_ffn0["conv_w"], g_ffn1["conv_w"]]),
        ffn_conv_b=jnp.stack([g_ffn0["conv_b"], g_ffn1["conv_b"]]),
        ffn_w_down=jnp.stack([g_ffn0["w_down"], g_ffn1["w_down"]]),
        ada_w=jnp.stack(d_ada_w), ada_b=jnp.concatenate([dmod0, dmod1], axis=0),
        norm1_g=jnp.concatenate([dn1g_0, dn1g_1], axis=0), norm2_g=jnp.concatenate([dn2g_0, dn2g_1], axis=0),
        final_g=d_final_g[0],
    )
    return loss[0, 0], dx0, grads


def _exchange(arrs, scatter, *, name):
    n = len(arrs)
    n_peer = N_DEV - 1

    def body(*refs):
        ins, outs = refs[:n], refs[n:2 * n]
        send, recv, loc = refs[2 * n:]
        x, y, c = lax.axis_index("x"), lax.axis_index("y"), lax.axis_index("c")
        me = 4 * x + 2 * y + c
        copies = []
        for a in range(n):
            src = ins[a].at[me] if scatter[a] else ins[a]
            cp = pltpu.make_async_copy(src, outs[a].at[me], loc.at[a])
            cp.start()
            copies.append(cp)
        remote = []
        for a in range(n):
            for k in range(1, N_DEV):
                px = 1 - x if k & 4 else x
                py = 1 - y if k & 2 else y
                pc = 1 - c if k & 1 else c
                src = ins[a].at[4 * px + 2 * py + pc] if scatter[a] else ins[a]
                r = pltpu.make_async_remote_copy(
                    src_ref=src, dst_ref=outs[a].at[me], send_sem=send.at[a * n_peer + k - 1],
                    recv_sem=recv.at[a * n_peer + k - 1], device_id=(px, py, pc), device_id_type=MESH)
                r.start()
                remote.append(r)
        for cp in copies:
            cp.wait()
        for r in remote:
            r.wait()

    out_shape = [jax.ShapeDtypeStruct(a.shape if s else (N_DEV,) + a.shape, a.dtype) for a, s in zip(arrs, scatter)]
    hbm = pl.BlockSpec(memory_space=pl.ANY)
    return pl.pallas_call(
        body, in_specs=[hbm] * n, out_specs=[hbm] * n, out_shape=out_shape,
        scratch_shapes=[pltpu.SemaphoreType.DMA((n * n_peer,)), pltpu.SemaphoreType.DMA((n * n_peer,)),
                        pltpu.SemaphoreType.DMA((n,))],
        compiler_params=pltpu.CompilerParams(has_side_effects=True), name=name)(*arrs)


def _adamw(w, parts, m, v, *, name, tr=256):
    R, C = w.shape
    P = parts.shape[0]
    tr = next(t for t in range(min(R, tr), 0, -1) if R % t == 0 and (t % 16 == 0 or t == R))
    c1 = 1.0 - ADAM_B1 ** ADAM_STEP
    c2 = 1.0 - ADAM_B2 ** ADAM_STEP

    def body(w_ref, p_ref, m_ref, v_ref, g_ref, d_ref, mo_ref, vo_ref):
        g = p_ref[0].astype(F32)
        for p in range(1, P):
            g = g + p_ref[p].astype(F32)
        mn = ADAM_B1 * m_ref[...] + (1.0 - ADAM_B1) * g
        vn = ADAM_B2 * v_ref[...] + (1.0 - ADAM_B2) * (g * g)
        g_ref[...] = g
        mo_ref[...] = mn
        vo_ref[...] = vn
        d_ref[...] = -ADAM_LR * ((mn / c1) / (jnp.sqrt(vn / c2) + ADAM_EPS) + ADAM_WD * w_ref[...])

    row = pl.BlockSpec((tr, C), lambda i: (i, 0))
    return pl.pallas_call(
        body, grid=(R // tr,), in_specs=[row, pl.BlockSpec((P, tr, C), lambda i: (0, i, 0)), row, row],
        out_specs=[row] * 4, out_shape=[jax.ShapeDtypeStruct((R, C), F32)] * 4,
        compiler_params=_params("parallel"), name=name)(w, parts, m, v)


def _sum_parts(parts, *, name):
    P, R, C = parts.shape

    def body(p_ref, o_ref):
        g = p_ref[0]
        for p in range(1, P):
            g = g + p_ref[p]
        o_ref[...] = g

    return pl.pallas_call(body, out_shape=jax.ShapeDtypeStruct((R, C), F32), name=name)(parts)


WEIGHTS = ["fox_w_in", "fox_b_f", "fox_q_gain", "fox_k_gain", "fox_w_out", "sgu_w_in", "sgu_b_in", "sgu_v_gain",
           "sgu_v_bias", "sgu_w_s", "sgu_b_s", "sgu_w_out", "ffn_w_up", "ffn_conv_w", "ffn_conv_b", "ffn_w_down",
           "ada_w", "ada_b", "norm1_g", "norm2_g", "final_g"]
BIG_AXIS = dict(fox_w_in=1, fox_w_out=0, sgu_w_in=1, sgu_w_out=0, ffn_w_up=1, ffn_w_down=0, ada_w=1)
SMALL_SHARDED = ["sgu_b_in", "sgu_v_gain", "sgu_v_bias", "ffn_conv_w"]
SINGLE_LAYER = ("fox_", "sgu_")


def _assemble(stacked, layers, axis):
    _, lr, cc = stacked.shape
    r = lr // layers
    s4 = stacked.reshape(N_DEV, layers, r, cc)
    if axis == 0:
        return s4.transpose(1, 0, 2, 3).reshape(layers, N_DEV * r, cc)
    return s4.transpose(1, 2, 0, 3).reshape(layers, r, N_DEV * cc)


def _disassemble(full, axis):
    layers, R, C = full.shape
    if axis == 0:
        r = R // N_DEV
        return full.reshape(layers, N_DEV, r, C).transpose(1, 0, 2, 3).reshape(N_DEV, layers * r, C)
    cc = C // N_DEV
    return full.reshape(layers, R, N_DEV, cc).transpose(2, 0, 1, 3).reshape(N_DEV, layers * R, cc)


def kernel(x, c, fox_w_in, fox_b_f, fox_q_gain, fox_k_gain, fox_w_out, sgu_w_in, sgu_b_in, sgu_v_gain, sgu_v_bias, sgu_w_s, sgu_b_s, sgu_w_out, ffn_w_up, ffn_conv_w, ffn_conv_b, ffn_w_down, ada_w, ada_b, norm1_g, norm2_g, final_g, loss_target, m_fox_w_in, m_fox_b_f, m_fox_q_gain, m_fox_k_gain, m_fox_w_out, m_sgu_w_in, m_sgu_b_in, m_sgu_v_gain, m_sgu_v_bias, m_sgu_w_s, m_sgu_b_s, m_sgu_w_out, m_ffn_w_up, m_ffn_conv_w, m_ffn_conv_b, m_ffn_w_down, m_ada_w, m_ada_b, m_norm1_g, m_norm2_g, m_final_g, v_fox_w_in, v_fox_b_f, v_fox_q_gain, v_fox_k_gain, v_fox_w_out, v_sgu_w_in, v_sgu_b_in, v_sgu_v_gain, v_sgu_v_bias, v_sgu_w_s, v_sgu_b_s, v_sgu_w_out, v_ffn_w_up, v_ffn_conv_w, v_ffn_conv_b, v_ffn_w_down, v_ada_w, v_ada_b, v_norm1_g, v_norm2_g, v_final_g):
    args = dict(locals())
    wts = {n: args[n] for n in WEIGHTS}
    ms = {n: args["m_" + n] for n in WEIGHTS}
    vs = {n: args["v_" + n] for n in WEIGHTS}
    me = 4 * lax.axis_index("x") + 2 * lax.axis_index("y") + lax.axis_index("c")

    big = list(BIG_AXIS)
    send = [wts[n].astype(BF16).reshape(-1, wts[n].shape[-1]) for n in big]
    send += [wts[n].reshape(-1, wts[n].shape[-1]) for n in SMALL_SHARDED]
    got = _exchange(send, [False] * len(send), name="gather_weights")
    full = {}
    for n, g in zip(big, got):
        f = _assemble(g, wts[n].shape[0], BIG_AXIS[n])
        full[n] = f[0] if n.startswith(SINGLE_LAYER) else f
    for n, g in zip(SMALL_SHARDED, got[len(big):]):
        lead = wts[n].shape[:-1]
        f = jnp.moveaxis(g.reshape((N_DEV,) + wts[n].shape), 0, -2).reshape(lead + (-1,))
        full[n] = f[0] if n.startswith(SINGLE_LAYER) else f
    for n in WEIGHTS:
        if n not in full:
            full[n] = wts[n][0] if n.startswith(SINGLE_LAYER) else wts[n]

    loss, grad_x, grads = _local_step(x[0], c, loss_target[0], full)

    parts = []
    for n in big:
        g = grads[n] if grads[n].ndim == 3 else grads[n][None]
        parts.append(_disassemble(g, BIG_AXIS[n]))
    small = [n for n in WEIGHTS if n not in BIG_AXIS]
    flat = jnp.concatenate([loss.reshape(1)] + [grads[n].reshape(-1).astype(F32) for n in small])
    n_flat = flat.shape[0]
    rows = -(-n_flat // (8 * LANES)) * 8
    flat = jnp.pad(flat, (0, rows * LANES - n_flat)).reshape(rows, LANES)
    got = _exchange(parts + [flat], [True] * len(big) + [False], name="exchange_grads")
    total = _sum_parts(got[-1], name="sum_small_grads").reshape(-1)
    loss_out = total[0]

    out_g, out_d, out_m, out_v = {}, {}, {}, {}
    for n, p in zip(big, got):
        shp = wts[n].shape
        two_d = lambda a: a.reshape(-1, shp[-1])
        g, d, mn, vn = _adamw(two_d(wts[n]), p, two_d(ms[n]), two_d(vs[n]), name=f"adamw_{n}")
        out_g[n], out_d[n], out_m[n], out_v[n] = (a.reshape(shp) for a in (g, d, mn, vn))
    off = 1
    small_g = {}
    for n in small:
        full_shape = grads[n].shape
        size = math.prod(full_shape)
        g = total[off:off + size].reshape(full_shape)
        off += size
        if n in SMALL_SHARDED:
            blk = full_shape[-1] // N_DEV
            g = lax.dynamic_slice_in_dim(g, me * blk, blk, axis=g.ndim - 1)
        small_g[n] = g.reshape(wts[n].shape)
    cat = lambda d: jnp.concatenate([d[n].reshape(-1) for n in small])
    n_small = sum(math.prod(wts[n].shape) for n in small)
    rows2 = -(-n_small // (256 * LANES)) * 256
    pack = lambda d, fill: jnp.pad(cat(d), (0, rows2 * LANES - n_small), constant_values=fill).reshape(rows2, LANES)
    g, d, mn, vn = _adamw(pack(wts, 0.0), pack(small_g, 0.0)[None], pack(ms, 0.0), pack(vs, 1.0), name="adamw_small")
    off = 0
    for n in small:
        size = math.prod(wts[n].shape)
        for src, dst in ((g, out_g), (d, out_d), (mn, out_m), (vn, out_v)):
            dst[n] = src.reshape(-1)[off:off + size].reshape(wts[n].shape)
        off += size

    return (loss_out, grad_x[None], *[out_g[n] for n in WEIGHTS], *[out_d[n] for n in WEIGHTS],
            *[out_m[n] for n in WEIGHTS], *[out_v[n] for n in WEIGHTS])
```

```python
import functools
import math

import jax
import jax.numpy as jnp
from jax import lax
from jax.experimental import pallas as pl
from jax.experimental.pallas import tpu as pltpu

F32, BF16 = jnp.float32, jnp.bfloat16
N_DEV = 8
HEADS, HEAD_DIM = 16, 64
HEAD_PAIRS = HEADS // 2
LANES = 128
EPS = 1e-6
SGU_BLOCK, SGU_GROUPS, SGU_CHUNK = 128, 8, 64
CONV_WIDTH = 3
ADAM_LR, ADAM_B1, ADAM_B2, ADAM_EPS, ADAM_WD, ADAM_STEP = 0.001, 0.9, 0.999, 1e-08, 0.01, 10
NEG = -1e30
GELU_C0, GELU_C1 = math.sqrt(2.0 / math.pi), 0.044715
MESH = pl.DeviceIdType.MESH
VMEM_LIMIT = 56 * 1024 * 1024


def _tile(dim, pref):
    if dim <= pref:
        return dim
    t = (pref // LANES) * LANES
    while t >= LANES:
        if dim % t == 0:
            return t
        t -= LANES
    return dim


def _params(*sem):
    return pltpu.CompilerParams(dimension_semantics=sem, vmem_limit_bytes=VMEM_LIMIT)


def _mm(a, b, *, name, ta=False, tb=False, out_dtype=F32, tm=1024, tn=1024, tk=1024, res=None):
    M = a.shape[1] if ta else a.shape[0]
    K = a.shape[0] if ta else a.shape[1]
    N = b.shape[0] if tb else b.shape[1]
    tm, tn, tk = _tile(M, tm), _tile(N, tn), _tile(K, tk)
    nk = K // tk
    dims = (((0 if ta else 1,), (1 if tb else 0,)), ((), ()))
    a_spec = pl.BlockSpec((tk, tm), lambda i, j, k: (k, i)) if ta else pl.BlockSpec((tm, tk), lambda i, j, k: (i, k))
    b_spec = pl.BlockSpec((tn, tk), lambda i, j, k: (j, k)) if tb else pl.BlockSpec((tk, tn), lambda i, j, k: (k, j))
    o_spec = pl.BlockSpec((tm, tn), lambda i, j, k: (i, j))

    def accumulate(a_ref, b_ref, acc):
        @pl.when(pl.program_id(2) == 0)
        def _():
            acc[...] = jnp.zeros_like(acc)
        acc[...] += lax.dot_general(a_ref[...], b_ref[...], dims, preferred_element_type=F32)

    if res is None:
        def body(a_ref, b_ref, o_ref, acc):
            accumulate(a_ref, b_ref, acc)

            @pl.when(pl.program_id(2) == nk - 1)
            def _():
                o_ref[...] = acc[...].astype(o_ref.dtype)

        return pl.pallas_call(
            body, grid=(M // tm, N // tn, nk), in_specs=[a_spec, b_spec], out_specs=o_spec,
            out_shape=jax.ShapeDtypeStruct((M, N), out_dtype), scratch_shapes=[pltpu.VMEM((tm, tn), F32)],
            compiler_params=_params("parallel", "parallel", "arbitrary"), name=name)(a, b)

    x, gate = res

    def body_res(a_ref, b_ref, x_ref, g_ref, o_ref, y_ref, acc):
        accumulate(a_ref, b_ref, acc)

        @pl.when(pl.program_id(2) == nk - 1)
        def _():
            y = acc[...]
            o_ref[...] = x_ref[...] + g_ref[...] * y
            y_ref[...] = y.astype(BF16)

    return pl.pallas_call(
        body_res, grid=(M // tm, N // tn, nk),
        in_specs=[a_spec, b_spec, o_spec, pl.BlockSpec((1, tn), lambda i, j, k: (0, j))],
        out_specs=[o_spec, o_spec],
        out_shape=[jax.ShapeDtypeStruct((M, N), F32), jax.ShapeDtypeStruct((M, N), BF16)],
        scratch_shapes=[pltpu.VMEM((tm, tn), F32)],
        compiler_params=_params("parallel", "parallel", "arbitrary"), name=name)(a, b, x, gate)


def _ada_mod(c, w, b, *, name):
    D, N = w.shape
    tn = _tile(N, 1536)
    rows = 16

    def body(c_ref, w_ref, b_ref, o_ref, ca_ref):
        cv = c_ref[...]
        ca = cv * jax.nn.sigmoid(cv)
        row0 = lax.broadcasted_iota(jnp.int32, (rows, D), 0) == 0
        ca16 = jnp.where(row0, jnp.broadcast_to(ca, (rows, D)), 0.0).astype(BF16)
        ca_ref[...] = ca16
        o_ref[...] = (jnp.dot(ca16, w_ref[...], preferred_element_type=F32) + b_ref[...])[0:8]

    out, ca = pl.pallas_call(
        body, grid=(N // tn,),
        in_specs=[pl.BlockSpec((1, D), lambda j: (0, 0)), pl.BlockSpec((D, tn), lambda j: (0, j)),
                  pl.BlockSpec((1, tn), lambda j: (0, j))],
        out_specs=[pl.BlockSpec((8, tn), lambda j: (0, j)), pl.BlockSpec((rows, D), lambda j: (0, 0))],
        out_shape=[jax.ShapeDtypeStruct((8, N), F32), jax.ShapeDtypeStruct((rows, D), BF16)],
        compiler_params=_params("arbitrary"), name=name)(c, w, b)
    return out[0:1], ca


def _norm_mod_fwd(x, g, shift, scale, *, name, ts=512):
    S, D = x.shape
    ts = _tile(S, ts)
    row = pl.BlockSpec((ts, D), lambda i: (i, 0))
    vec = pl.BlockSpec((1, D), lambda i: (0, 0))

    def body(x_ref, g_ref, sh_ref, sc_ref, h_ref):
        xv = x_ref[...]
        r = lax.rsqrt(jnp.mean(xv * xv, axis=-1, keepdims=True) + EPS)
        h_ref[...] = ((xv * r * g_ref[...]) * (1.0 + sc_ref[...]) + sh_ref[...]).astype(BF16)

    return pl.pallas_call(body, grid=(S // ts,), in_specs=[row, vec, vec, vec], out_specs=row,
                          out_shape=jax.ShapeDtypeStruct((S, D), BF16),
                          compiler_params=_params("parallel"), name=name)(x, g, shift, scale)


def _acc_init(step, *refs):
    @pl.when(step == 0)
    def _():
        for r in refs:
            r[...] = jnp.zeros_like(r)


def _colsum(v):
    return jnp.sum(v, axis=0, keepdims=True)


def _norm_mod_bwd(dh, x, g, scale, dres, prev=None, *, name, ts=512):
    S, D = x.shape
    ts = _tile(S, ts)
    row = pl.BlockSpec((ts, D), lambda i: (i, 0))
    vec = pl.BlockSpec((1, D), lambda i: (0, 0))
    has_prev = prev is not None

    def body(*refs):
        if has_prev:
            dh_ref, x_ref, g_ref, sc_ref, dres_ref, y_ref, gate_ref, dx_ref, dg_ref, dsh_ref, dsc_ref, dy_ref, dgate_ref = refs
            _acc_init(pl.program_id(0), dg_ref, dsh_ref, dsc_ref, dgate_ref)
        else:
            dh_ref, x_ref, g_ref, sc_ref, dres_ref, dx_ref, dg_ref, dsh_ref, dsc_ref = refs
            _acc_init(pl.program_id(0), dg_ref, dsh_ref, dsc_ref)
        xv, dhv, gv = x_ref[...], dh_ref[...], g_ref[...]
        r = lax.rsqrt(jnp.mean(xv * xv, axis=-1, keepdims=True) + EPS)
        xh = xv * r
        dsh_ref[...] += _colsum(dhv)
        dsc_ref[...] += _colsum(dhv * (xh * gv))
        dn = dhv * (1.0 + sc_ref[...])
        dg_ref[...] += _colsum(dn * xh)
        dxh = dn * gv
        dx = dres_ref[...] + r * (dxh - xh * jnp.mean(dxh * xh, axis=-1, keepdims=True))
        dx_ref[...] = dx
        if has_prev:
            dy_ref[...] = (gate_ref[...] * dx).astype(BF16)
            dgate_ref[...] += _colsum(dx * y_ref[...].astype(F32))

    ins, in_specs = [dh, x, g, scale, dres], [row, row, vec, vec, row]
    outs = [jax.ShapeDtypeStruct((S, D), F32)] + [jax.ShapeDtypeStruct((1, D), F32)] * 3
    out_specs = [row, vec, vec, vec]
    if has_prev:
        ins += list(prev)
        in_specs += [row, vec]
        outs += [jax.ShapeDtypeStruct((S, D), BF16), jax.ShapeDtypeStruct((1, D), F32)]
        out_specs += [row, vec]
    return pl.pallas_call(body, grid=(S // ts,), in_specs=in_specs, out_specs=out_specs, out_shape=outs,
                          compiler_params=_params("arbitrary"), name=name)(*ins)


def _final_loss(x, g, target, y, gate, *, name, ts=512):
    S, D = x.shape
    ts = _tile(S, ts)
    row = pl.BlockSpec((ts, D), lambda i: (i, 0))
    vec = pl.BlockSpec((1, D), lambda i: (0, 0))
    lvec = pl.BlockSpec((1, LANES), lambda i: (0, 0))

    def body(x_ref, g_ref, t_ref, y_ref, gate_ref, loss_ref, dg_ref, dx_ref, dy_ref, dgate_ref):
        _acc_init(pl.program_id(0), loss_ref, dg_ref, dgate_ref)
        xv, gv = x_ref[...], g_ref[...]
        r = lax.rsqrt(jnp.mean(xv * xv, axis=-1, keepdims=True) + EPS)
        xh = xv * r
        e = xh * gv - t_ref[...]
        loss_ref[...] += 0.5 * jnp.sum(jnp.mean(e * e, axis=-1, keepdims=True), axis=0, keepdims=True)
        dout = e * (1.0 / D)
        dg_ref[...] += _colsum(dout * xh)
        dxh = dout * gv
        dx = r * (dxh - xh * jnp.mean(dxh * xh, axis=-1, keepdims=True))
        dx_ref[...] = dx
        dy_ref[...] = (gate_ref[...] * dx).astype(BF16)
        dgate_ref[...] += _colsum(dx * y_ref[...].astype(F32))

    return pl.pallas_call(
        body, grid=(S // ts,), in_specs=[row, vec, row, row, vec], out_specs=[lvec, vec, row, row, vec],
        out_shape=[jax.ShapeDtypeStruct((1, LANES), F32), jax.ShapeDtypeStruct((1, D), F32),
                   jax.ShapeDtypeStruct((S, D), F32), jax.ShapeDtypeStruct((S, D), BF16),
                   jax.ShapeDtypeStruct((1, D), F32)],
        compiler_params=_params("arbitrary"), name=name)(x, g, target, y, gate)


def _head_mean(v, gmat):
    hi = v.astype(BF16)
    lo = (v - hi.astype(F32)).astype(BF16)
    return jnp.dot(hi, gmat, preferred_element_type=F32) + jnp.dot(lo, gmat, preferred_element_type=F32)


L_F, L_ONE, L_SHIFT = HEAD_DIM, HEAD_DIM + 3, HEAD_DIM + 6


def _lane():
    return lax.broadcasted_iota(jnp.int32, (1, LANES), 1)


def _split3(v):
    p1 = v.astype(BF16).astype(F32)
    r1 = v - p1
    p2 = r1.astype(BF16).astype(F32)
    p3 = (r1 - p2).astype(BF16).astype(F32)
    return p1, p2, p3


def _put3(lane, first, pieces):
    out = jnp.where(lane == first, pieces[0], 0.0)
    for k in (1, 2):
        out = out + jnp.where(lane == first + k, pieces[k], 0.0)
    return out


def _ones3(lane, first):
    return jnp.where((lane >= first) & (lane < first + 3), 1.0, 0.0)


def _lane_col(v, lane, idx):
    return jnp.sum(jnp.where(lane == idx, v, 0.0), axis=-1, keepdims=True)


def _head_of_pair(pair, e, lane):
    return jnp.where(lane < HEAD_DIM, pair if e == 0 else pltpu.roll(pair, HEAD_DIM, 1), 0.0)


def _pair_of_heads(even, odd, lane):
    return jnp.where(lane < HEAD_DIM, even, pltpu.roll(odd, HEAD_DIM, 1))


def _fox_prep_fwd(proj, fcum, qgain, kgain, gmat, *, name, ts=256):
    S = proj.shape[0]
    D = HEADS * HEAD_DIM
    ts = _tile(S, ts)
    scale = HEAD_DIM ** -0.5

    def body(p_ref, f_ref, qg_ref, kg_ref, gm_ref, q_ref, k_ref, v_ref):
        gm, lane, fc = gm_ref[...], _lane(), f_ref[...]
        for cpair in range(HEAD_PAIRS):
            qv = p_ref[:, pl.ds(cpair * LANES, LANES)].astype(F32)
            kv = p_ref[:, pl.ds(D + cpair * LANES, LANES)].astype(F32)
            vv = p_ref[:, pl.ds(2 * D + cpair * LANES, LANES)].astype(F32)
            qn = (qv * lax.rsqrt(_head_mean(qv * qv, gm) + EPS) * qg_ref[...]) * scale
            kn = kv * lax.rsqrt(_head_mean(kv * kv, gm) + EPS) * kg_ref[...]
            for e in range(2):
                h = 2 * cpair + e
                cols = pl.ds(h * LANES, LANES)
                f3 = _split3(_lane_col(fc, lane, h))
                q_ref[:, cols] = (_head_of_pair(qn, e, lane) + _put3(lane, L_F, f3) + _ones3(lane, L_ONE)).astype(BF16)
                k_ref[:, cols] = (_head_of_pair(kn, e, lane) + _ones3(lane, L_F)
                                  - _put3(lane, L_ONE, f3) + _ones3(lane, L_SHIFT)).astype(BF16)
                v_ref[:, cols] = (_head_of_pair(vv, e, lane) + _ones3(lane, L_F)).astype(BF16)

    vec = pl.BlockSpec((1, LANES), lambda i: (0, 0))
    wide = pl.BlockSpec((ts, HEADS * LANES), lambda i: (i, 0))
    return pl.pallas_call(
        body, grid=(S // ts,),
        in_specs=[pl.BlockSpec((ts, 3 * D), lambda i: (i, 0)), pl.BlockSpec((ts, LANES), lambda i: (i, 0)), vec, vec,
                  pl.BlockSpec((LANES, LANES), lambda i: (0, 0))],
        out_specs=[wide, wide, wide], out_shape=[jax.ShapeDtypeStruct((S, HEADS * LANES), BF16)] * 3,
        compiler_params=_params("parallel"), name=name)(proj, fcum, qgain, kgain, gmat)


def _fox_prep_bwd(proj, dq_aug, dk_aug, dv_aug, dog, qgain, kgain, gmat, *, name, ts=256):
    S = proj.shape[0]
    D = HEADS * HEAD_DIM
    ts = _tile(S, ts)
    scale = HEAD_DIM ** -0.5

    def body(p_ref, dq_ref, dk_ref, dv_ref, dog_ref, qg_ref, kg_ref, gm_ref, o_ref, df_ref, dqg_ref, dkg_ref):
        _acc_init(pl.program_id(0), dqg_ref, dkg_ref)
        gm, lane = gm_ref[...], _lane()
        df = jnp.zeros((ts, LANES), F32)
        for cpair in range(HEAD_PAIRS):
            tiles = []
            for e in range(2):
                h = 2 * cpair + e
                cols = pl.ds(h * LANES, LANES)
                tq, tk = dq_ref[:, cols], dk_ref[:, cols]
                df = jnp.where(lane == h, _lane_col(tq, lane, L_F) - _lane_col(tk, lane, L_ONE), df)
                tiles.append((tq, tk, dv_ref[:, cols].astype(F32)))
            pair = [_pair_of_heads(tiles[0][k], tiles[1][k], lane) for k in range(3)]
            for half, g_ref, dg_ref, mult in ((0, qg_ref, dqg_ref, scale), (1, kg_ref, dkg_ref, 1.0)):
                v = p_ref[:, pl.ds(half * D + cpair * LANES, LANES)].astype(F32)
                r = lax.rsqrt(_head_mean(v * v, gm) + EPS)
                xh = v * r
                dn = pair[half] * mult
                dg_ref[...] += _colsum(dn * xh)
                dxh = dn * g_ref[...]
                o_ref[:, pl.ds(half * D + cpair * LANES, LANES)] = (r * (dxh - xh * _head_mean(dxh * xh, gm))).astype(BF16)
            o_ref[:, pl.ds(2 * D + cpair * LANES, LANES)] = pair[2].astype(BF16)
        o_ref[:, pl.ds(3 * D, D)] = dog_ref[...]
        o_ref[:, pl.ds(4 * D, LANES)] = jnp.zeros((ts, LANES), BF16)
        df_ref[...] = df

    row = pl.BlockSpec((ts, D), lambda i: (i, 0))
    wide = pl.BlockSpec((ts, HEADS * LANES), lambda i: (i, 0))
    vec = pl.BlockSpec((1, LANES), lambda i: (0, 0))
    return pl.pallas_call(
        body, grid=(S // ts,),
        in_specs=[pl.BlockSpec((ts, 2 * D), lambda i: (i, 0)), wide, wide, wide, row, vec, vec,
                  pl.BlockSpec((LANES, LANES), lambda i: (0, 0))],
        out_specs=[pl.BlockSpec((ts, 4 * D + LANES), lambda i: (i, 0)), pl.BlockSpec((ts, LANES), lambda i: (i, 0)), vec, vec],
        out_shape=[jax.ShapeDtypeStruct((S, 4 * D + LANES), BF16), jax.ShapeDtypeStruct((S, LANES), F32),
                   jax.ShapeDtypeStruct((1, LANES), F32), jax.ShapeDtypeStruct((1, LANES), F32)],
        compiler_params=_params("arbitrary"), name=name)(proj, dq_aug, dk_aug, dv_aug, dog, qgain, kgain, gmat)


def _log_sigmoid(z):
    return jnp.minimum(z, 0.0) - jnp.log(1.0 + jnp.exp(-jnp.abs(z)))


def _fox_decay_fwd(fl, bf, *, name, tb=256):
    S = fl.shape[0]
    tb = _tile(S, tb)

    def body(fl_ref, b_ref, o_ref, carry):
        @pl.when(pl.program_id(0) == 0)
        def _():
            carry[...] = jnp.zeros_like(carry)
        logf = _log_sigmoid(fl_ref[...] + b_ref[...])
        tri = (lax.broadcasted_iota(jnp.int32, (tb, tb), 1) <= lax.broadcasted_iota(jnp.int32, (tb, tb), 0)).astype(F32)
        cs = jnp.dot(tri, logf, preferred_element_type=F32, precision=lax.Precision.HIGHEST) + carry[...]
        o_ref[...] = cs
        carry[...] = _row_of(cs, tb - 1)

    return pl.pallas_call(
        body, grid=(S // tb,),
        in_specs=[pl.BlockSpec((tb, LANES), lambda i: (i, 0)), pl.BlockSpec((1, LANES), lambda i: (0, 0))],
        out_specs=pl.BlockSpec((tb, LANES), lambda i: (i, 0)),
        out_shape=jax.ShapeDtypeStruct((S, LANES), F32), scratch_shapes=[pltpu.VMEM((1, LANES), F32)],
        compiler_params=_params("arbitrary"), name=name)(fl, bf)


def _fox_decay_bwd(dF, fl, bf, dproj, *, name, tb=256):
    S = fl.shape[0]
    tb = _tile(S, tb)
    n = S // tb
    last_col = dproj.shape[1] // LANES - 1

    def body(df_ref, fl_ref, b_ref, dproj_hbm, o_ref, db_ref, carry):
        del dproj_hbm
        @pl.when(pl.program_id(0) == 0)
        def _():
            carry[...] = jnp.zeros_like(carry)
            db_ref[...] = jnp.zeros_like(db_ref)
        tri = (lax.broadcasted_iota(jnp.int32, (tb, tb), 1) >= lax.broadcasted_iota(jnp.int32, (tb, tb), 0)).astype(F32)
        rc = jnp.dot(tri, df_ref[...], preferred_element_type=F32, precision=lax.Precision.HIGHEST) + carry[...]
        carry[...] = _row_of(rc, 0)
        dfl = rc * jax.nn.sigmoid(-(fl_ref[...] + b_ref[...]))
        o_ref[...] = dfl.astype(BF16)
        db_ref[...] += _colsum(dfl)

    rev = pl.BlockSpec((tb, LANES), lambda i: (n - 1 - i, 0))
    vec = pl.BlockSpec((1, LANES), lambda i: (0, 0))
    return pl.pallas_call(
        body, grid=(n,), in_specs=[rev, rev, vec, pl.BlockSpec(memory_space=pl.ANY)],
        out_specs=[pl.BlockSpec((tb, LANES), lambda i: (n - 1 - i, last_col)), vec],
        out_shape=[jax.ShapeDtypeStruct(dproj.shape, BF16), jax.ShapeDtypeStruct((1, LANES), F32)],
        scratch_shapes=[pltpu.VMEM((1, LANES), F32)], input_output_aliases={3: 0},
        compiler_params=_params("arbitrary"), name=name)(dF, fl, bf, dproj)


_NT = (((1,), (1,)), ((), ()))
_TN = (((0,), (0,)), ((), ()))


def _causal(T, transposed=False):
    r, c = lax.broadcasted_iota(jnp.int32, (T, T), 0), lax.broadcasted_iota(jnp.int32, (T, T), 1)
    return r <= c if transposed else c <= r


def _with_shift(q_tile, shift, lane):
    keep = jnp.where((lane >= L_SHIFT) & (lane < L_SHIFT + 3), 0.0, q_tile)
    return (keep + _put3(lane, L_SHIFT, _split3(-shift))).astype(BF16)


def _attn_rowmax(q_aug, k_aug, *, name, T=1024):
    S = q_aug.shape[0]
    T = _tile(S, T)
    n = S // T

    def body(q_ref, k_ref, o_ref, m_s):
        i, j = pl.program_id(1), pl.program_id(2)

        @pl.when(j == 0)
        def _():
            m_s[...] = jnp.full_like(m_s, NEG)

        def step(diag):
            s = lax.dot_general(q_ref[...], k_ref[...], _NT, preferred_element_type=F32)
            if diag:
                s = jnp.where(_causal(T), s, NEG)
            m = m_s[...]
            for cb in range(T // LANES):
                m = jnp.maximum(m, s[:, cb * LANES:(cb + 1) * LANES])
            m_s[...] = m

        @pl.when(j < i)
        def _():
            step(False)

        @pl.when(j == i)
        def _():
            step(True)
            o_ref[...] = _with_shift(q_ref[...].astype(F32), jnp.max(m_s[...], axis=-1, keepdims=True), _lane())

    qrow = pl.BlockSpec((T, LANES), lambda h, i, j: (i, h))
    return pl.pallas_call(
        body, grid=(HEADS, n, n),
        in_specs=[qrow, pl.BlockSpec((T, LANES), lambda h, i, j: (jnp.minimum(j, i), h))],
        out_specs=qrow, out_shape=jax.ShapeDtypeStruct(q_aug.shape, BF16),
        scratch_shapes=[pltpu.VMEM((T, LANES), F32)],
        compiler_params=_params("parallel", "parallel", "arbitrary"), name=name)(q_aug, k_aug)


def _attn_fwd(q_max, k_aug, v_aug, *, name, T=1024):
    S = q_max.shape[0]
    T = _tile(S, T)
    n = S // T

    def body(q_ref, k_ref, v_ref, o_ref, qb_ref, acc_s):
        i, j = pl.program_id(1), pl.program_id(2)

        @pl.when(j == 0)
        def _():
            acc_s[...] = jnp.zeros_like(acc_s)

        def step(diag):
            s = lax.dot_general(q_ref[...], k_ref[...], _NT, preferred_element_type=F32)
            if diag:
                s = jnp.where(_causal(T), s, NEG)
            acc_s[...] += jnp.dot(jnp.exp(s).astype(BF16), v_ref[...], preferred_element_type=F32)

        @pl.when(j < i)
        def _():
            step(False)

        @pl.when(j == i)
        def _():
            step(True)
            lane = _lane()
            acc = acc_s[...]
            l = _lane_col(acc, lane, L_F)
            o_ref[...] = acc / l
            qf = q_ref[...].astype(F32)
            row_max = -jnp.sum(jnp.where((lane >= L_SHIFT) & (lane < L_SHIFT + 3), qf, 0.0), axis=-1, keepdims=True)
            qb_ref[...] = _with_shift(qf, row_max + jnp.log(l), lane)

    qrow = pl.BlockSpec((T, LANES), lambda h, i, j: (i, h))
    kv = pl.BlockSpec((T, LANES), lambda h, i, j: (jnp.minimum(j, i), h))
    return pl.pallas_call(
        body, grid=(HEADS, n, n), in_specs=[qrow, kv, kv], out_specs=[qrow, qrow],
        out_shape=[jax.ShapeDtypeStruct(q_max.shape, F32), jax.ShapeDtypeStruct(q_max.shape, BF16)],
        scratch_shapes=[pltpu.VMEM((T, LANES), F32)],
        compiler_params=_params("parallel", "parallel", "arbitrary"), name=name)(q_max, k_aug, v_aug)


def _attn_bwd(q_lse, k_aug, v_aug, do_aug, *, name, T=1024):
    S = q_lse.shape[0]
    T = _tile(S, T)
    n = S // T

    def body(q_ref, do_ref, k_ref, v_ref, dq_ref, dk_ref, dv_ref, dq_s, dk_s, dv_s):
        j, i = pl.program_id(1), pl.program_id(2)

        def step(diag):
            q, do, k, v = q_ref[...], do_ref[...], k_ref[...], v_ref[...]
            st = lax.dot_general(k, q, _NT, preferred_element_type=F32)
            if diag:
                st = jnp.where(_causal(T, transposed=True), st, NEG)
            pt = jnp.exp(st)
            dst = (pt * lax.dot_general(v, do, _NT, preferred_element_type=F32)).astype(BF16)
            dv_s[...] += jnp.dot(pt.astype(BF16), do, preferred_element_type=F32)
            dk_s[...] += jnp.dot(dst, q, preferred_element_type=F32)
            upd = lax.dot_general(dst, k, _TN, preferred_element_type=F32)

            @pl.when(j == 0)
            def _():
                dq_s[i] = upd

            @pl.when(j > 0)
            def _():
                dq_s[i] += upd

        @pl.when(i == j)
        def _():
            dk_s[...] = jnp.zeros_like(dk_s)
            dv_s[...] = jnp.zeros_like(dv_s)
            step(True)
            dq_ref[...] = dq_s[j]

        @pl.when(i > j)
        def _():
            step(False)

        @pl.when(i == n - 1)
        def _():
            dk_ref[...] = dk_s[...]
            dv_ref[...] = dv_s[...].astype(BF16)

    qrow = pl.BlockSpec((T, LANES), lambda h, j, i: (jnp.maximum(i, j), h))
    kv = pl.BlockSpec((T, LANES), lambda h, j, i: (j, h))
    return pl.pallas_call(
        body, grid=(HEADS, n, n), in_specs=[qrow, qrow, kv, kv], out_specs=[kv, kv, kv],
        out_shape=[jax.ShapeDtypeStruct(q_lse.shape, F32), jax.ShapeDtypeStruct(q_lse.shape, F32),
                   jax.ShapeDtypeStruct(q_lse.shape, BF16)],
        scratch_shapes=[pltpu.VMEM((n, T, LANES), F32), pltpu.VMEM((T, LANES), F32), pltpu.VMEM((T, LANES), F32)],
        compiler_params=_params("parallel", "arbitrary", "arbitrary"), name=name)(q_lse, do_aug, k_aug, v_aug)


def _fox_gate_fwd(att_aug, proj, *, name, ts=256):
    S = att_aug.shape[0]
    D = HEADS * HEAD_DIM
    ts = _tile(S, ts)

    def body(a_ref, o_ref, att_ref, out_ref):
        lane = _lane()
        for cpair in range(HEAD_PAIRS):
            cols = pl.ds(cpair * LANES, LANES)
            pair = _pair_of_heads(a_ref[:, pl.ds(2 * cpair * LANES, LANES)], a_ref[:, pl.ds((2 * cpair + 1) * LANES, LANES)], lane)
            att_ref[:, cols] = pair
            out_ref[:, cols] = (pair * jax.nn.sigmoid(o_ref[:, cols].astype(F32))).astype(BF16)

    row = pl.BlockSpec((ts, D), lambda i: (i, 0))
    return pl.pallas_call(
        body, grid=(S // ts,),
        in_specs=[pl.BlockSpec((ts, HEADS * LANES), lambda i: (i, 0)), pl.BlockSpec((ts, D), lambda i: (i, 3))],
        out_specs=[row, row], out_shape=[jax.ShapeDtypeStruct((S, D), F32), jax.ShapeDtypeStruct((S, D), BF16)],
        compiler_params=_params("parallel"), name=name)(att_aug, proj)


def _fox_gate_bwd(da, att, proj, *, name, ts=256):
    S, D = att.shape
    ts = _tile(S, ts)

    def body(da_ref, a_ref, o_ref, do_ref, dog_ref):
        lane = _lane()
        for cpair in range(HEAD_PAIRS):
            cols = pl.ds(cpair * LANES, LANES)
            dav, av = da_ref[:, cols].astype(F32), a_ref[:, cols]
            sg = jax.nn.sigmoid(o_ref[:, cols].astype(F32))
            datt = (dav * sg).astype(BF16).astype(F32)
            dog_ref[:, cols] = (dav * av * sg * (1.0 - sg)).astype(BF16)
            prod = datt * av
            for e in range(2):
                in_head = (lane < HEAD_DIM) if e == 0 else (lane >= HEAD_DIM)
                delta = jnp.sum(jnp.where(in_head, prod, 0.0), axis=-1, keepdims=True)
                tile = _head_of_pair(datt, e, lane) + _put3(lane, L_F, _split3(-delta))
                do_ref[:, pl.ds((2 * cpair + e) * LANES, LANES)] = tile.astype(BF16)

    row = pl.BlockSpec((ts, D), lambda i: (i, 0))
    return pl.pallas_call(
        body, grid=(S // ts,), in_specs=[row, row, pl.BlockSpec((ts, D), lambda i: (i, 3))],
        out_specs=[pl.BlockSpec((ts, HEADS * LANES), lambda i: (i, 0)), row],
        out_shape=[jax.ShapeDtypeStruct((S, HEADS * LANES), BF16), jax.ShapeDtypeStruct((S, D), BF16)],
        compiler_params=_params("parallel"), name=name)(da, att, proj)


def _row_of(block, r):
    rows = lax.broadcasted_iota(jnp.int32, block.shape, 0)
    return jnp.sum(jnp.where(rows == r, block, 0.0), axis=0, keepdims=True)


def _shift_down(cur, tail, k):
    out = pltpu.roll(cur, k, 0)
    rows = lax.broadcasted_iota(jnp.int32, cur.shape, 0)
    for r in range(k):
        out = jnp.where(rows == r, _row_of(tail, tail.shape[0] - k + r), out)
    return out


def _shift_up(cur, head, k):
    n = cur.shape[0]
    out = pltpu.roll(cur, n - k, 0)
    rows = lax.broadcasted_iota(jnp.int32, cur.shape, 0)
    for r in range(k):
        out = jnp.where(rows == n - k + r, _row_of(head, r), out)
    return out


HALO = 16


def _conv_taps(a_ref, tail_ref, first):
    cur = a_ref[...].astype(F32)
    tail = jnp.where(first, 0.0, tail_ref[...].astype(F32))
    return cur, _shift_down(cur, tail, 1), _shift_down(cur, tail, 2)


def _conv_gate_fwd(a, cw, cb, *, name, ts=256, tc=1408):
    S, F2 = a.shape
    F = F2 // 2
    ts, tc = _tile(S, ts), _tile(F, tc)
    nc = F // tc
    sub = ts // HALO

    def body(g_ref, gt_ref, v_ref, vt_ref, wg_ref, wv_ref, bg_ref, bv_ref, o_ref):
        first = pl.program_id(1) == 0
        halves = []
        for a_ref, t_ref, w_ref, b_ref in ((g_ref, gt_ref, wg_ref, bg_ref), (v_ref, vt_ref, wv_ref, bv_ref)):
            a0, a1, a2 = _conv_taps(a_ref, t_ref, first)
            halves.append(a2 * w_ref[0:1, :] + a1 * w_ref[1:2, :] + a0 * w_ref[2:3, :] + b_ref[...])
        g, val = halves
        o_ref[...] = (g * jax.nn.sigmoid(g) * val).astype(BF16)

    cur = lambda off: pl.BlockSpec((ts, tc), lambda j, i: (i, off + j))
    tail = lambda off: pl.BlockSpec((HALO, tc), lambda j, i: (jnp.maximum(i * sub - 1, 0), off + j))
    wsp = lambda off: pl.BlockSpec((CONV_WIDTH, tc), lambda j, i: (0, off + j))
    bsp = lambda off: pl.BlockSpec((1, tc), lambda j, i: (0, off + j))
    return pl.pallas_call(
        body, grid=(nc, S // ts),
        in_specs=[cur(0), tail(0), cur(nc), tail(nc), wsp(0), wsp(nc), bsp(0), bsp(nc)],
        out_specs=pl.BlockSpec((ts, tc), lambda j, i: (i, j)),
        out_shape=jax.ShapeDtypeStruct((S, F), BF16),
        compiler_params=_params("parallel", "parallel"), name=name)(a, a, a, a, cw, cw, cb, cb)


def _conv_gate_bwd_pre(a, dact, cw, cb, *, name, ts=256, tc=1408):
    S, F2 = a.shape
    F = F2 // 2
    ts, tc = _tile(S, ts), _tile(F, tc)
    nc = F // tc
    sub = ts // HALO

    def body(g_ref, gt_ref, v_ref, vt_ref, d_ref, wg_ref, wv_ref, bg_ref, bv_ref, dg_ref, dv_ref, sg_ref, sv_ref):
        first = pl.program_id(1) == 0
        _acc_init(pl.program_id(1), sg_ref, sv_ref)
        taps, pre = [], []
        for a_ref, t_ref, w_ref, b_ref in ((g_ref, gt_ref, wg_ref, bg_ref), (v_ref, vt_ref, wv_ref, bv_ref)):
            a0, a1, a2 = _conv_taps(a_ref, t_ref, first)
            taps.append((a2, a1, a0))
            pre.append(a2 * w_ref[0:1, :] + a1 * w_ref[1:2, :] + a0 * w_ref[2:3, :] + b_ref[...])
        g, val = pre
        d = d_ref[...].astype(F32)
        sg = jax.nn.sigmoid(g)
        dg = d * val * (sg * (1.0 + g * (1.0 - sg)))
        dval = d * (g * sg)
        for dd, o_ref, s_ref, tp in ((dg, dg_ref, sg_ref, taps[0]), (dval, dv_ref, sv_ref, taps[1])):
            o_ref[...] = dd.astype(BF16)
            for r in range(CONV_WIDTH):
                s_ref[r:r + 1, :] += _colsum(dd * tp[r])
            s_ref[CONV_WIDTH:CONV_WIDTH + 1, :] += _colsum(dd)

    cur = lambda off: pl.BlockSpec((ts, tc), lambda j, i: (i, off + j))
    tail = lambda off: pl.BlockSpec((HALO, tc), lambda j, i: (jnp.maximum(i * sub - 1, 0), off + j))
    wsp = lambda off: pl.BlockSpec((CONV_WIDTH, tc), lambda j, i: (0, off + j))
    bsp = lambda off: pl.BlockSpec((1, tc), lambda j, i: (0, off + j))
    acc = lambda off: pl.BlockSpec((8, tc), lambda j, i: (0, off + j))
    dpre_g, dpre_v, sums_g, sums_v = pl.pallas_call(
        body, grid=(nc, S // ts),
        in_specs=[cur(0), tail(0), cur(nc), tail(nc), cur(0), wsp(0), wsp(nc), bsp(0), bsp(nc)],
        out_specs=[cur(0), cur(0), acc(0), acc(0)],
        out_shape=[jax.ShapeDtypeStruct((S, F), BF16), jax.ShapeDtypeStruct((S, F), BF16),
                   jax.ShapeDtypeStruct((8, F), F32), jax.ShapeDtypeStruct((8, F), F32)],
        compiler_params=_params("parallel", "arbitrary"), name=name)(a, a, a, a, dact, cw, cw, cb, cb)
    return dpre_g, dpre_v, sums_g, sums_v


def _conv_bwd_taps(dpre_g, dpre_v, cw, *, name, ts=256, tc=1408):
    S, F = dpre_g.shape
    ts, tc = _tile(S, ts), _tile(F, tc)
    nc = F // tc
    sub = ts // HALO
    last = S // ts - 1

    def one_half(dpre, w_off, call_name):
        def body(d_ref, h_ref, w_ref, o_ref):
            cur = d_ref[...].astype(F32)
            head = jnp.where(pl.program_id(1) == last, 0.0, h_ref[...].astype(F32))
            o_ref[...] = (cur * w_ref[2:3, :] + _shift_up(cur, head, 1) * w_ref[1:2, :]
                          + _shift_up(cur, head, 2) * w_ref[0:1, :]).astype(BF16)

        return pl.pallas_call(
            body, grid=(nc, S // ts),
            in_specs=[pl.BlockSpec((ts, tc), lambda j, i: (i, j)),
                      pl.BlockSpec((HALO, tc), lambda j, i: (jnp.minimum((i + 1) * sub, S // HALO - 1), j)),
                      pl.BlockSpec((CONV_WIDTH, tc), lambda j, i: (0, w_off + j))],
            out_specs=pl.BlockSpec((ts, tc), lambda j, i: (i, j)),
            out_shape=jax.ShapeDtypeStruct((S, F), BF16),
            compiler_params=_params("parallel", "parallel"), name=call_name)(dpre, dpre, cw)

    return one_half(dpre_g, 0, f"{name}_g"), one_half(dpre_v, nc, f"{name}_v")


def _gelu_parts(z):
    z2 = z * z
    t = jnp.tanh(GELU_C0 * (z + GELU_C1 * z * z2))
    val = 0.5 * z * (1.0 + t)
    grad = 0.5 * (1.0 + t) + 0.5 * z * (1.0 - t * t) * GELU_C0 * (1.0 + 3.0 * GELU_C1 * z2)
    return val, grad


def _sgu_fwd(pre, b_in, vgain, vbias, wm, bsb, *, name, ts=256):
    S, W2 = pre.shape
    W = W2 // 2
    gd = W // SGU_GROUPS
    ts = _tile(S, ts)

    def body(p_ref, b_ref, vg_ref, vb_ref, wm_ref, bs_ref, y_ref):
        u = _gelu_parts(p_ref[:, pl.ds(0, W)].astype(F32) + b_ref[:, pl.ds(0, W)])[0]
        v = _gelu_parts(p_ref[:, pl.ds(W, W)].astype(F32) + b_ref[:, pl.ds(W, W)])[0]
        mu = jnp.mean(v, axis=-1, keepdims=True)
        vc = v - mu
        rstd = lax.rsqrt(jnp.mean(vc * vc, axis=-1, keepdims=True) + EPS)
        vn = ((vc * rstd) * vg_ref[...] + vb_ref[...]).astype(BF16)
        for blk in range(ts // SGU_BLOCK):
            r0 = blk * SGU_BLOCK
            for g in range(SGU_GROUPS):
                c0 = g * gd
                mixed = jnp.dot(wm_ref[g], vn[r0:r0 + SGU_BLOCK, c0:c0 + gd], preferred_element_type=F32) + bs_ref[g]
                y_ref[pl.ds(r0, SGU_BLOCK), pl.ds(c0, gd)] = (u[r0:r0 + SGU_BLOCK, c0:c0 + gd] * mixed).astype(BF16)

    full = lambda shape: pl.BlockSpec(shape, lambda i: (0,) * len(shape))
    return pl.pallas_call(
        body, grid=(S // ts,),
        in_specs=[pl.BlockSpec((ts, W2), lambda i: (i, 0)), full((1, W2)), full((1, W)), full((1, W)),
                  full((SGU_GROUPS, SGU_BLOCK, SGU_BLOCK)), full((SGU_GROUPS, SGU_BLOCK, gd))],
        out_specs=pl.BlockSpec((ts, W), lambda i: (i, 0)), out_shape=jax.ShapeDtypeStruct((S, W), BF16),
        compiler_params=_params("parallel"), name=name)(pre, b_in, vgain, vbias, wm, bsb)


def _sgu_bwd(pre, dy, b_in, vgain, vbias, wm, wmt, bsb, *, name, ts=256):
    S, W2 = pre.shape
    W = W2 // 2
    gd = W // SGU_GROUPS
    ts = _tile(S, ts)
    last = S // ts - 1

    def body(p_ref, dy_ref, b_ref, vg_ref, vb_ref, wm_ref, wmt_ref, bs_ref,
             dp_ref, db_ref, dvg_ref, dvb_ref, dws_ref, dbs_ref, du_s, dvn_s, dbs_s):
        step = pl.program_id(0)
        _acc_init(step, db_ref, dvg_ref, dvb_ref, dws_ref, dbs_s)
        u, gu = _gelu_parts(p_ref[:, pl.ds(0, W)].astype(F32) + b_ref[:, pl.ds(0, W)])
        v, gv = _gelu_parts(p_ref[:, pl.ds(W, W)].astype(F32) + b_ref[:, pl.ds(W, W)])
        mu = jnp.mean(v, axis=-1, keepdims=True)
        vc = v - mu
        rstd = lax.rsqrt(jnp.mean(vc * vc, axis=-1, keepdims=True) + EPS)
        vhat = vc * rstd
        vn = (vhat * vg_ref[...] + vb_ref[...]).astype(BF16)
        dyv = dy_ref[...].astype(F32)
        for blk in range(ts // SGU_BLOCK):
            r0 = blk * SGU_BLOCK
            for g in range(SGU_GROUPS):
                c0 = g * gd
                vn_g = vn[r0:r0 + SGU_BLOCK, c0:c0 + gd]
                dy_g = dyv[r0:r0 + SGU_BLOCK, c0:c0 + gd]
                mixed = jnp.dot(wm_ref[g], vn_g, preferred_element_type=F32) + bs_ref[g]
                dmix = dy_g * u[r0:r0 + SGU_BLOCK, c0:c0 + gd]
                dmix_b = dmix.astype(BF16)
                du_s[pl.ds(r0, SGU_BLOCK), pl.ds(c0, gd)] = dy_g * mixed
                dvn_s[pl.ds(r0, SGU_BLOCK), pl.ds(c0, gd)] = jnp.dot(wmt_ref[g], dmix_b, preferred_element_type=F32)
                dws_ref[g] += lax.dot_general(dmix_b, vn_g, _NT, preferred_element_type=F32)
                dbs_s[g] += dmix
        dvn = dvn_s[...]
        dvg_ref[...] += _colsum(dvn * vhat)
        dvb_ref[...] += _colsum(dvn)
        dvh = dvn * vg_ref[...]
        dv = rstd * (dvh - jnp.mean(dvh, axis=-1, keepdims=True) - vhat * jnp.mean(dvh * vhat, axis=-1, keepdims=True))
        dpu = du_s[...] * gu
        dpv = dv * gv
        dp_ref[:, pl.ds(0, W)] = dpu.astype(BF16)
        dp_ref[:, pl.ds(W, W)] = dpv.astype(BF16)
        db_ref[:, pl.ds(0, W)] += _colsum(dpu)
        db_ref[:, pl.ds(W, W)] += _colsum(dpv)

        @pl.when(step == last)
        def _():
            for g in range(SGU_GROUPS):
                dbs_ref[g] = jnp.broadcast_to(jnp.sum(dbs_s[g], axis=-1, keepdims=True), (SGU_BLOCK, SGU_BLOCK))

    full = lambda shape: pl.BlockSpec(shape, lambda i: (0,) * len(shape))
    gsq = (SGU_GROUPS, SGU_BLOCK, SGU_BLOCK)
    return pl.pallas_call(
        body, grid=(S // ts,),
        in_specs=[pl.BlockSpec((ts, W2), lambda i: (i, 0)), pl.BlockSpec((ts, W), lambda i: (i, 0)),
                  full((1, W2)), full((1, W)), full((1, W)), full(gsq), full(gsq), full((SGU_GROUPS, SGU_BLOCK, gd))],
        out_specs=[pl.BlockSpec((ts, W2), lambda i: (i, 0)), full((1, W2)), full((1, W)), full((1, W)), full(gsq), full(gsq)],
        out_shape=[jax.ShapeDtypeStruct((S, W2), BF16), jax.ShapeDtypeStruct((1, W2), F32),
                   jax.ShapeDtypeStruct((1, W), F32), jax.ShapeDtypeStruct((1, W), F32),
                   jax.ShapeDtypeStruct(gsq, F32), jax.ShapeDtypeStruct(gsq, F32)],
        scratch_shapes=[pltpu.VMEM((ts, W), F32), pltpu.VMEM((ts, W), F32), pltpu.VMEM((SGU_GROUPS, SGU_BLOCK, gd), F32)],
        compiler_params=_params("arbitrary"), name=name)(pre, dy, b_in, vgain, vbias, wm, wmt, bsb)


def _ffn_fwd(x, mods, n2g, w_up, cw, cb, w_down, tag):
    sh, sc, gate = mods
    h = _norm_mod_fwd(x, n2g, sh, sc, name=f"{tag}_norm_fwd")
    a = _mm(h, w_up, out_dtype=BF16, tn=1408, name=f"{tag}_up")
    act = _conv_gate_fwd(a, cw, cb, name=f"{tag}_conv_fwd")
    x_out, y = _mm(act, w_down, tk=1408, tm=512, res=(x, gate), name=f"{tag}_down")
    return x_out, (x, h, a, act, y)


def _ffn_bwd(dy, saved, mods, n2g, w_up, cw, cb, w_down, dres, prev, tag):
    x, h, a, act, _ = saved
    sh, sc, gate = mods
    dact = _mm(dy, w_down, tb=True, out_dtype=BF16, tn=1408, name=f"{tag}_down_dx")
    dw_down = _mm(act, dy, ta=True, out_dtype=BF16, tn=1024, name=f"{tag}_down_dw")
    dpre_g, dpre_v, sums_g, sums_v = _conv_gate_bwd_pre(a, dact, cw, cb, name=f"{tag}_conv_bwd")
    da_g, da_v = _conv_bwd_taps(dpre_g, dpre_v, cw, name=f"{tag}_conv_taps")
    da = jnp.concatenate([da_g, da_v], axis=1)
    dh = _mm(da, w_up, tb=True, tk=1408, name=f"{tag}_up_dx")
    dw_up = _mm(h, da, ta=True, out_dtype=BF16, tn=1408, name=f"{tag}_up_dw")
    outs = _norm_mod_bwd(dh, x, n2g, sc, dres, prev, name=f"{tag}_norm_bwd")
    sums = jnp.concatenate([sums_g, sums_v], axis=1)
    return outs, dict(w_up=dw_up, w_down=dw_down, conv_w=sums[0:CONV_WIDTH], conv_b=sums[CONV_WIDTH])


def _local_step(x, c, target, w):
    S, D = x.shape
    lane = jnp.arange(LANES)
    gmat = jnp.where((lane[:, None] // HEAD_DIM) == (lane[None, :] // HEAD_DIM), 1.0 / HEAD_DIM, 0.0).astype(BF16)
    qg2 = jnp.tile(w["fox_q_gain"].reshape(1, HEAD_DIM), (1, 2))
    kg2 = jnp.tile(w["fox_k_gain"].reshape(1, HEAD_DIM), (1, 2))
    bf_pad = jnp.pad(w["fox_b_f"].reshape(1, HEADS), ((0, 0), (0, LANES - HEADS)))
    w_in_pad = jnp.pad(w["fox_w_in"], ((0, 0), (0, 4 * D + LANES - w["fox_w_in"].shape[1])))
    w_qkvo, w_f = w_in_pad[:, :4 * D], w_in_pad[:, 4 * D:]
    tpos = jnp.arange(SGU_BLOCK)
    smask = (tpos[None, :] // SGU_CHUNK) <= (tpos[:, None] // SGU_CHUNK)
    wm32 = jnp.where(smask[None], w["sgu_w_s"], 0.0)
    wm, wmt = wm32.astype(BF16), jnp.swapaxes(wm32, 1, 2).astype(BF16)
    gd = w["sgu_v_gain"].shape[-1] // SGU_GROUPS
    bsb = jnp.broadcast_to(w["sgu_b_s"][:, :, None], (SGU_GROUPS, SGU_BLOCK, gd))
    vec = lambda v: v.reshape(1, -1)

    mods, cas = [], []
    for i in range(2):
        m, ca = _ada_mod(c, w["ada_w"][i], vec(w["ada_b"][i]), name=f"ada_mod_{i}")
        mods.append([m[:, k * D:(k + 1) * D] for k in range(6)])
        cas.append(ca)

    sh1, sc1, g1 = mods[0][0:3]
    h0 = _norm_mod_fwd(x, vec(w["norm1_g"][0]), sh1, sc1, name="fox_norm_fwd")
    proj = _mm(h0, w_qkvo, out_dtype=BF16, name="fox_proj")
    fl = _mm(h0, w_f, name="fox_forget_proj")
    fcum = _fox_decay_fwd(fl, bf_pad, name="fox_decay")
    q_aug, k_aug, v_aug = _fox_prep_fwd(proj, fcum, qg2, kg2, gmat, name="fox_qk_norm")
    q_max = _attn_rowmax(q_aug, k_aug, name="fox_attn_rowmax")
    att_aug, q_lse = _attn_fwd(q_max, k_aug, v_aug, name="fox_attn_fwd")
    att, ag = _fox_gate_fwd(att_aug, proj, name="fox_gate_fwd")
    x1, y_fox = _mm(ag, w["fox_w_out"], tm=512, res=(x, g1), name="fox_out")
    x2, ffn0 = _ffn_fwd(x1, mods[0][3:6], vec(w["norm2_g"][0]), w["ffn_w_up"][0], w["ffn_conv_w"][0],
                        vec(w["ffn_conv_b"][0]), w["ffn_w_down"][0], "ffn0")

    sh1b, sc1b, g1b = mods[1][0:3]
    h1 = _norm_mod_fwd(x2, vec(w["norm1_g"][1]), sh1b, sc1b, name="sgu_norm_fwd")
    pre = _mm(h1, w["sgu_w_in"], out_dtype=BF16, name="sgu_in")
    b_in, vg, vb = vec(w["sgu_b_in"]), vec(w["sgu_v_gain"]), vec(w["sgu_v_bias"])
    ys = _sgu_fwd(pre, b_in, vg, vb, wm, bsb, name="sgu_core_fwd")
    x3, y_sgu = _mm(ys, w["sgu_w_out"], tm=512, res=(x2, g1b), name="sgu_out")
    x4, ffn1 = _ffn_fwd(x3, mods[1][3:6], vec(w["norm2_g"][1]), w["ffn_w_up"][1], w["ffn_conv_w"][1],
                        vec(w["ffn_conv_b"][1]), w["ffn_w_down"][1], "ffn1")

    loss, d_final_g, dx4, dy_ffn1, dgate_ffn1 = _final_loss(x4, vec(w["final_g"]), target, ffn1[4], mods[1][5], name="final_loss")

    (dx3, dn2g_1, dsh2_1, dsc2_1, dy_sgu, dgate_sgu), g_ffn1 = _ffn_bwd(
        dy_ffn1, ffn1, mods[1][3:6], vec(w["norm2_g"][1]), w["ffn_w_up"][1], w["ffn_conv_w"][1], vec(w["ffn_conv_b"][1]),
        w["ffn_w_down"][1], dx4, (y_sgu, g1b), "ffn1")

    dys = _mm(dy_sgu, w["sgu_w_out"], tb=True, out_dtype=BF16, name="sgu_out_dx")
    dw_sgu_out = _mm(ys, dy_sgu, ta=True, out_dtype=BF16, name="sgu_out_dw")
    dpre, db_in, dvg, dvb, dws, dbs = _sgu_bwd(pre, dys, b_in, vg, vb, wm, wmt, bsb, name="sgu_core_bwd")
    dh1 = _mm(dpre, w["sgu_w_in"], tb=True, name="sgu_in_dx")
    dw_sgu_in = _mm(h1, dpre, ta=True, out_dtype=BF16, name="sgu_in_dw")
    dx2, dn1g_1, dsh1_1, dsc1_1, dy_ffn0, dgate_ffn0 = _norm_mod_bwd(
        dh1, x2, vec(w["norm1_g"][1]), sc1b, dx3, (ffn0[4], mods[0][5]), name="sgu_norm_bwd")

    (dx1, dn2g_0, dsh2_0, dsc2_0, dy_fox, dgate_fox), g_ffn0 = _ffn_bwd(
        dy_ffn0, ffn0, mods[0][3:6], vec(w["norm2_g"][0]), w["ffn_w_up"][0], w["ffn_conv_w"][0], vec(w["ffn_conv_b"][0]),
        w["ffn_w_down"][0], dx2, (y_fox, g1), "ffn0")

    dag = _mm(dy_fox, w["fox_w_out"], tb=True, out_dtype=BF16, name="fox_out_dx")
    dw_fox_out = _mm(ag, dy_fox, ta=True, out_dtype=BF16, name="fox_out_dw")
    do_aug, dog = _fox_gate_bwd(dag, att, proj, name="fox_gate_bwd")
    dq_aug, dk_aug, dv_aug = _attn_bwd(q_lse, k_aug, v_aug, do_aug, name="fox_attn_bwd")
    dproj, dF, dqg, dkg = _fox_prep_bwd(proj, dq_aug, dk_aug, dv_aug, dog, qg2, kg2, gmat, name="fox_qk_norm_bwd")
    dproj, dbf = _fox_decay_bwd(dF, fl, bf_pad, dproj, name="fox_decay_bwd")
    dh0 = _mm(dproj, w_in_pad, tb=True, tk=1408, name="fox_proj_dx")
    dw_fox_in = _mm(h0, dproj, ta=True, out_dtype=BF16, tn=1408, name="fox_proj_dw")
    dx0, dn1g_0, dsh1_0, dsc1_0 = _norm_mod_bwd(dh0, x, vec(w["norm1_g"][0]), sc1, dx1, None, name="fox_norm_bwd")

    dmod0 = jnp.concatenate([dsh1_0, dsc1_0, dgate_fox, dsh2_0, dsc2_0, dgate_ffn0], axis=1)
    dmod1 = jnp.concatenate([dsh1_1, dsc1_1, dgate_sgu, dsh2_1, dsc2_1, dgate_ffn1], axis=1)
    d_ada_w = []
    for i, dm in enumerate((dmod0, dmod1)):
        dm16 = jnp.pad(dm, ((0, 15), (0, 0))).astype(BF16)
        d_ada_w.append(_mm(cas[i], dm16, ta=True, out_dtype=BF16, tn=1536, name=f"ada_dw_{i}"))

    grads = dict(
        fox_w_in=dw_fox_in[:, :w["fox_w_in"].shape[1]],
        fox_b_f=dbf[0, :HEADS],
        fox_q_gain=dqg[0, :HEAD_DIM] + dqg[0, HEAD_DIM:],
        fox_k_gain=dkg[0, :HEAD_DIM] + dkg[0, HEAD_DIM:],
        fox_w_out=dw_fox_out,
        sgu_w_in=dw_sgu_in, sgu_b_in=db_in[0], sgu_v_gain=dvg[0], sgu_v_bias=dvb[0],
        sgu_w_s=jnp.where(smask[None], dws, 0.0), sgu_b_s=dbs[:, :, 0], sgu_w_out=dw_sgu_out,
        ffn_w_up=jnp.stack([g_ffn0["w_up"], g_ffn1["w_up"]]),
        ffn_conv_w=jnp.stack([g_ffn0["conv_w"], g_ffn1["conv_w"]]),
        ffn_conv_b=jnp.stack([g_ffn0["conv_b"], g_ffn1["conv_b"]]),
        ffn_w_down=jnp.stack([g_ffn0["w_down"], g_ffn1["w_down"]]),
        ada_w=jnp.stack(d_ada_w), ada_b=jnp.concatenate([dmod0, dmod1], axis=0),
        norm1_g=jnp.concatenate([dn1g_0, dn1g_1], axis=0), norm2_g=jnp.concatenate([dn2g_0, dn2g_1], axis=0),
        final_g=d_final_g[0],
    )
    return loss[0, 0], dx0, grads


def _exchange(arrs, scatter, *, name):
    n = len(arrs)
    n_peer = N_DEV - 1

    def body(*refs):
        ins, outs = refs[:n], refs[n:2 * n]
        send, recv, loc = refs[2 * n:]
        x, y, c = lax.axis_index("x"), lax.axis_index("y"), lax.axis_index("c")
        me = 4 * x + 2 * y + c
        copies = []
        for a in range(n):
            src = ins[a].at[me] if scatter[a] else ins[a]
            cp = pltpu.make_async_copy(src, outs[a].at[me], loc.at[a])
            cp.start()
            copies.append(cp)
        remote = []
        for a in range(n):
            for k in range(1, N_DEV):
                px = 1 - x if k & 4 else x
                py = 1 - y if k & 2 else y
                pc = 1 - c if k & 1 else c
                src = ins[a].at[4 * px + 2 * py + pc] if scatter[a] else ins[a]
                r = pltpu.make_async_remote_copy(
                    src_ref=src, dst_ref=outs[a].at[me], send_sem=send.at[a * n_peer + k - 1],
                    recv_sem=recv.at[a * n_peer + k - 1], device_id=(px, py, pc), device_id_type=MESH)
                r.start()
                remote.append(r)
        for cp in copies:
            cp.wait()
        for r in remote:
            r.wait()

    out_shape = [jax.ShapeDtypeStruct(a.shape if s else (N_DEV,) + a.shape, a.dtype) for a, s in zip(arrs, scatter)]
    hbm = pl.BlockSpec(memory_space=pl.ANY)
    return pl.pallas_call(
        body, in_specs=[hbm] * n, out_specs=[hbm] * n, out_shape=out_shape,
        scratch_shapes=[pltpu.SemaphoreType.DMA((n * n_peer,)), pltpu.SemaphoreType.DMA((n * n_peer,)),
                        pltpu.SemaphoreType.DMA((n,))],
        compiler_params=pltpu.CompilerParams(has_side_effects=True), name=name)(*arrs)


def _adamw(w, parts, m, v, *, name, tr=256):
    R, C = w.shape
    P = parts.shape[0]
    tr = next(t for t in range(min(R, tr), 0, -1) if R % t == 0 and (t % 16 == 0 or t == R))
    c1 = 1.0 - ADAM_B1 ** ADAM_STEP
    c2 = 1.0 - ADAM_B2 ** ADAM_STEP

    def body(w_ref, p_ref, m_ref, v_ref, g_ref, d_ref, mo_ref, vo_ref):
        g = p_ref[0].astype(F32)
        for p in range(1, P):
            g = g + p_ref[p].astype(F32)
        mn = ADAM_B1 * m_ref[...] + (1.0 - ADAM_B1) * g
        vn = ADAM_B2 * v_ref[...] + (1.0 - ADAM_B2) * (g * g)
        g_ref[...] = g
        mo_ref[...] = mn
        vo_ref[...] = vn
        d_ref[...] = -ADAM_LR * ((mn / c1) / (jnp.sqrt(vn / c2) + ADAM_EPS) + ADAM_WD * w_ref[...])

    row = pl.BlockSpec((tr, C), lambda i: (i, 0))
    return pl.pallas_call(
        body, grid=(R // tr,), in_specs=[row, pl.BlockSpec((P, tr, C), lambda i: (0, i, 0)), row, row],
        out_specs=[row] * 4, out_shape=[jax.ShapeDtypeStruct((R, C), F32)] * 4,
        compiler_params=_params("parallel"), name=name)(w, parts, m, v)


def _sum_parts(parts, *, name):
    P, R, C = parts.shape

    def body(p_ref, o_ref):
        g = p_ref[0]
        for p in range(1, P):
            g = g + p_ref[p]
        o_ref[...] = g

    return pl.pallas_call(body, out_shape=jax.ShapeDtypeStruct((R, C), F32), name=name)(parts)


WEIGHTS = ["fox_w_in", "fox_b_f", "fox_q_gain", "fox_k_gain", "fox_w_out", "sgu_w_in", "sgu_b_in", "sgu_v_gain",
           "sgu_v_bias", "sgu_w_s", "sgu_b_s", "sgu_w_out", "ffn_w_up", "ffn_conv_w", "ffn_conv_b", "ffn_w_down",
           "ada_w", "ada_b", "norm1_g", "norm2_g", "final_g"]
BIG_AXIS = dict(fox_w_in=1, fox_w_out=0, sgu_w_in=1, sgu_w_out=0, ffn_w_up=1, ffn_w_down=0, ada_w=1)
SMALL_SHARDED = ["sgu_b_in", "sgu_v_gain", "sgu_v_bias", "ffn_conv_w"]
SINGLE_LAYER = ("fox_", "sgu_")


def _assemble(stacked, layers, axis):
    _, lr, cc = stacked.shape
    r = lr // layers
    s4 = stacked.reshape(N_DEV, layers, r, cc)
    if axis == 0:
        return s4.transpose(1, 0, 2, 3).reshape(layers, N_DEV * r, cc)
    return s4.transpose(1, 2, 0, 3).reshape(layers, r, N_DEV * cc)


def _disassemble(full, axis):
    layers, R, C = full.shape
    if axis == 0:
        r = R // N_DEV
        return full.reshape(layers, N_DEV, r, C).transpose(1, 0, 2, 3).reshape(N_DEV, layers * r, C)
    cc = C // N_DEV
    return full.reshape(layers, R, N_DEV, cc).transpose(2, 0, 1, 3).reshape(N_DEV, layers * R, cc)


def kernel(x, c, fox_w_in, fox_b_f, fox_q_gain, fox_k_gain, fox_w_out, sgu_w_in, sgu_b_in, sgu_v_gain, sgu_v_bias, sgu_w_s, sgu_b_s, sgu_w_out, ffn_w_up, ffn_conv_w, ffn_conv_b, ffn_w_down, ada_w, ada_b, norm1_g, norm2_g, final_g, loss_target, m_fox_w_in, m_fox_b_f, m_fox_q_gain, m_fox_k_gain, m_fox_w_out, m_sgu_w_in, m_sgu_b_in, m_sgu_v_gain, m_sgu_v_bias, m_sgu_w_s, m_sgu_b_s, m_sgu_w_out, m_ffn_w_up, m_ffn_conv_w, m_ffn_conv_b, m_ffn_w_down, m_ada_w, m_ada_b, m_norm1_g, m_norm2_g, m_final_g, v_fox_w_in, v_fox_b_f, v_fox_q_gain, v_fox_k_gain, v_fox_w_out, v_sgu_w_in, v_sgu_b_in, v_sgu_v_gain, v_sgu_v_bias, v_sgu_w_s, v_sgu_b_s, v_sgu_w_out, v_ffn_w_up, v_ffn_conv_w, v_ffn_conv_b, v_ffn_w_down, v_ada_w, v_ada_b, v_norm1_g, v_norm2_g, v_final_g):
    args = dict(locals())
    wts = {n: args[n] for n in WEIGHTS}
    ms = {n: args["m_" + n] for n in WEIGHTS}
    vs = {n: args["v_" + n] for n in WEIGHTS}
    me = 4 * lax.axis_index("x") + 2 * lax.axis_index("y") + lax.axis_index("c")

    big = list(BIG_AXIS)
    send = [wts[n].astype(BF16).reshape(-1, wts[n].shape[-1]) for n in big]
    send += [wts[n].reshape(-1, wts[n].shape[-1]) for n in SMALL_SHARDED]
    got = _exchange(send, [False] * len(send), name="gather_weights")
    full = {}
    for n, g in zip(big, got):
        f = _assemble(g, wts[n].shape[0], BIG_AXIS[n])
        full[n] = f[0] if n.startswith(SINGLE_LAYER) else f
    for n, g in zip(SMALL_SHARDED, got[len(big):]):
        lead = wts[n].shape[:-1]
        f = jnp.moveaxis(g.reshape((N_DEV,) + wts[n].shape), 0, -2).reshape(lead + (-1,))
        full[n] = f[0] if n.startswith(SINGLE_LAYER) else f
    for n in WEIGHTS:
        if n not in full:
            full[n] = wts[n][0] if n.startswith(SINGLE_LAYER) else wts[n]

    loss, grad_x, grads = _local_step(x[0], c, loss_target[0], full)

    parts = []
    for n in big:
        g = grads[n] if grads[n].ndim == 3 else grads[n][None]
        parts.append(_disassemble(g, BIG_AXIS[n]))
    small = [n for n in WEIGHTS if n not in BIG_AXIS]
    flat = jnp.concatenate([loss.reshape(1)] + [grads[n].reshape(-1).astype(F32) for n in small])
    n_flat = flat.shape[0]
    rows = -(-n_flat // (8 * LANES)) * 8
    flat = jnp.pad(flat, (0, rows * LANES - n_flat)).reshape(rows, LANES)
    got = _exchange(parts + [flat], [True] * len(big) + [False], name="exchange_grads")
    total = _sum_parts(got[-1], name="sum_small_grads").reshape(-1)
    loss_out = total[0]

    out_g, out_d, out_m, out_v = {}, {}, {}, {}
    for n, p in zip(big, got):
        shp = wts[n].shape
        two_d = lambda a: a.reshape(-1, shp[-1])
        g, d, mn, vn = _adamw(two_d(wts[n]), p, two_d(ms[n]), two_d(vs[n]), name=f"adamw_{n}")
        out_g[n], out_d[n], out_m[n], out_v[n] = (a.reshape(shp) for a in (g, d, mn, vn))
    off = 1
    small_g = {}
    for n in small:
        full_shape = grads[n].shape
        size = math.prod(full_shape)
        g = total[off:off + size].reshape(full_shape)
        off += size
        if n in SMALL_SHARDED:
            blk = full_shape[-1] // N_DEV
            g = lax.dynamic_slice_in_dim(g, me * blk, blk, axis=g.ndim - 1)
        small_g[n] = g.reshape(wts[n].shape)
    cat = lambda d: jnp.concatenate([d[n].reshape(-1) for n in small])
    n_small = sum(math.prod(wts[n].shape) for n in small)
    rows2 = -(-n_small // (256 * LANES)) * 256
    pack = lambda d, fill: jnp.pad(cat(d), (0, rows2 * LANES - n_small), constant_values=fill).reshape(rows2, LANES)
    g, d, mn, vn = _adamw(pack(wts, 0.0), pack(small_g, 0.0)[None], pack(ms, 0.0), pack(vs, 1.0), name="adamw_small")
    off = 0
    for n in small:
        size = math.prod(wts[n].shape)
        for src, dst in ((g, out_g), (d, out_d), (mn, out_m), (vn, out_v)):
            dst[n] = src.reshape(-1)[off:off + size].reshape(wts[n].shape)
        off += size

    return (loss_out, grad_x[None], *[out_g[n] for n in WEIGHTS], *[out_d[n] for n in WEIGHTS],
            *[out_m[n] for n in WEIGHTS], *[out_v[n] for n in WEIGHTS])
```

```python
import functools
import math

import jax
import jax.numpy as jnp
from jax import lax
from jax.experimental import pallas as pl
from jax.experimental.pallas import tpu as pltpu

F32, BF16 = jnp.float32, jnp.bfloat16
N_DEV = 8
HEADS, HEAD_DIM = 16, 64
HEAD_PAIRS = HEADS // 2
LANES = 128
EPS = 1e-6
SGU_BLOCK, SGU_GROUPS, SGU_CHUNK = 128, 8, 64
CONV_WIDTH = 3
ADAM_LR, ADAM_B1, ADAM_B2, ADAM_EPS, ADAM_WD, ADAM_STEP = 0.001, 0.9, 0.999, 1e-08, 0.01, 10
NEG = -1e30
GELU_C0, GELU_C1 = math.sqrt(2.0 / math.pi), 0.044715
MESH = pl.DeviceIdType.MESH
VMEM_LIMIT = 56 * 1024 * 1024


def _tile(dim, pref):
    if dim <= pref:
        return dim
    t = (pref // LANES) * LANES
    while t >= LANES:
        if dim % t == 0:
            return t
        t -= LANES
    return dim


def _params(*sem):
    return pltpu.CompilerParams(dimension_semantics=sem, vmem_limit_bytes=VMEM_LIMIT)


def _mm(a, b, *, name, ta=False, tb=False, out_dtype=F32, tm=1024, tn=1024, tk=1024, res=None):
    M = a.shape[1] if ta else a.shape[0]
    K = a.shape[0] if ta else a.shape[1]
    N = b.shape[0] if tb else b.shape[1]
    tm, tn, tk = _tile(M, tm), _tile(N, tn), _tile(K, tk)
    nk = K // tk
    dims = (((0 if ta else 1,), (1 if tb else 0,)), ((), ()))
    a_spec = pl.BlockSpec((tk, tm), lambda i, j, k: (k, i)) if ta else pl.BlockSpec((tm, tk), lambda i, j, k: (i, k))
    b_spec = pl.BlockSpec((tn, tk), lambda i, j, k: (j, k)) if tb else pl.BlockSpec((tk, tn), lambda i, j, k: (k, j))
    o_spec = pl.BlockSpec((tm, tn), lambda i, j, k: (i, j))

    def accumulate(a_ref, b_ref, acc):
        @pl.when(pl.program_id(2) == 0)
        def _():
            acc[...] = jnp.zeros_like(acc)
        acc[...] += lax.dot_general(a_ref[...], b_ref[...], dims, preferred_element_type=F32)

    if res is None:
        def body(a_ref, b_ref, o_ref, acc):
            accumulate(a_ref, b_ref, acc)

            @pl.when(pl.program_id(2) == nk - 1)
            def _():
                o_ref[...] = acc[...].astype(o_ref.dtype)

        return pl.pallas_call(
            body, grid=(M // tm, N // tn, nk), in_specs=[a_spec, b_spec], out_specs=o_spec,
            out_shape=jax.ShapeDtypeStruct((M, N), out_dtype), scratch_shapes=[pltpu.VMEM((tm, tn), F32)],
            compiler_params=_params("parallel", "parallel", "arbitrary"), name=name)(a, b)

    x, gate = res

    def body_res(a_ref, b_ref, x_ref, g_ref, o_ref, y_ref, acc):
        accumulate(a_ref, b_ref, acc)

        @pl.when(pl.program_id(2) == nk - 1)
        def _():
            y = acc[...]
            o_ref[...] = x_ref[...] + g_ref[...] * y
            y_ref[...] = y.astype(BF16)

    return pl.pallas_call(
        body_res, grid=(M // tm, N // tn, nk),
        in_specs=[a_spec, b_spec, o_spec, pl.BlockSpec((1, tn), lambda i, j, k: (0, j))],
        out_specs=[o_spec, o_spec],
        out_shape=[jax.ShapeDtypeStruct((M, N), F32), jax.ShapeDtypeStruct((M, N), BF16)],
        scratch_shapes=[pltpu.VMEM((tm, tn), F32)],
        compiler_params=_params("parallel", "parallel", "arbitrary"), name=name)(a, b, x, gate)


def _ada_mod(c_rows, w, b, *, name):
    R, D = c_rows.shape
    N = w.shape[1]
    tn = _tile(N, 1536)
    rows = 16
    c_pad = jnp.pad(c_rows, ((0, rows - R), (0, 0)))

    def body(c_ref, w_ref, b_ref, o_ref, ca_ref):
        cv = c_ref[...]
        ca16 = (cv * jax.nn.sigmoid(cv)).astype(BF16)
        ca_ref[...] = ca16
        o_ref[...] = jnp.dot(ca16, w_ref[...], preferred_element_type=F32) + b_ref[...]

    out, ca = pl.pallas_call(
        body, grid=(N // tn,),
        in_specs=[pl.BlockSpec((rows, D), lambda j: (0, 0)), pl.BlockSpec((D, tn), lambda j: (0, j)),
                  pl.BlockSpec((1, tn), lambda j: (0, j))],
        out_specs=[pl.BlockSpec((rows, tn), lambda j: (0, j)), pl.BlockSpec((rows, D), lambda j: (0, 0))],
        out_shape=[jax.ShapeDtypeStruct((rows, N), F32), jax.ShapeDtypeStruct((rows, D), BF16)],
        compiler_params=_params("arbitrary"), name=name)(c_pad, w, b)
    return out[0:R], ca


def _norm_mod_fwd(x, g, shift, scale, *, name, ts=512):
    S, D = x.shape
    ts = _tile(S, ts)
    row = pl.BlockSpec((ts, D), lambda i: (i, 0))
    vec = pl.BlockSpec((1, D), lambda i: (0, 0))

    def body(x_ref, g_ref, sh_ref, sc_ref, h_ref):
        xv = x_ref[...]
        r = lax.rsqrt(jnp.mean(xv * xv, axis=-1, keepdims=True) + EPS)
        h_ref[...] = ((xv * r * g_ref[...]) * (1.0 + sc_ref[...]) + sh_ref[...]).astype(BF16)

    return pl.pallas_call(body, grid=(S // ts,), in_specs=[row, vec, vec, vec], out_specs=row,
                          out_shape=jax.ShapeDtypeStruct((S, D), BF16),
                          compiler_params=_params("parallel"), name=name)(x, g, shift, scale)


def _acc_init(step, *refs):
    @pl.when(step == 0)
    def _():
        for r in refs:
            r[...] = jnp.zeros_like(r)


def _colsum(v):
    return jnp.sum(v, axis=0, keepdims=True)


def _norm_mod_bwd(dh, x, g, scale, dres, prev=None, *, name, ts=512):
    S, D = x.shape
    ts = _tile(S, ts)
    row = pl.BlockSpec((ts, D), lambda i: (i, 0))
    vec = pl.BlockSpec((1, D), lambda i: (0, 0))
    has_prev = prev is not None

    def body(*refs):
        if has_prev:
            dh_ref, x_ref, g_ref, sc_ref, dres_ref, y_ref, gate_ref, dx_ref, dg_ref, dsh_ref, dsc_ref, dy_ref, dgate_ref = refs
            _acc_init(pl.program_id(0), dg_ref, dsh_ref, dsc_ref, dgate_ref)
        else:
            dh_ref, x_ref, g_ref, sc_ref, dres_ref, dx_ref, dg_ref, dsh_ref, dsc_ref = refs
            _acc_init(pl.program_id(0), dg_ref, dsh_ref, dsc_ref)
        xv, dhv, gv = x_ref[...], dh_ref[...], g_ref[...]
        r = lax.rsqrt(jnp.mean(xv * xv, axis=-1, keepdims=True) + EPS)
        xh = xv * r
        dsh_ref[...] += _colsum(dhv)
        dsc_ref[...] += _colsum(dhv * (xh * gv))
        dn = dhv * (1.0 + sc_ref[...])
        dg_ref[...] += _colsum(dn * xh)
        dxh = dn * gv
        dx = dres_ref[...] + r * (dxh - xh * jnp.mean(dxh * xh, axis=-1, keepdims=True))
        dx_ref[...] = dx
        if has_prev:
            dy_ref[...] = (gate_ref[...] * dx).astype(BF16)
            dgate_ref[...] += _colsum(dx * y_ref[...].astype(F32))

    ins, in_specs = [dh, x, g, scale, dres], [row, row, vec, vec, row]
    outs = [jax.ShapeDtypeStruct((S, D), F32)] + [jax.ShapeDtypeStruct((1, D), F32)] * 3
    out_specs = [row, vec, vec, vec]
    if has_prev:
        ins += list(prev)
        in_specs += [row, vec]
        outs += [jax.ShapeDtypeStruct((S, D), BF16), jax.ShapeDtypeStruct((1, D), F32)]
        out_specs += [row, vec]
    return pl.pallas_call(body, grid=(S // ts,), in_specs=in_specs, out_specs=out_specs, out_shape=outs,
                          compiler_params=_params("arbitrary"), name=name)(*ins)


def _final_loss(x, g, target, y, gate, *, name, ts=512):
    S, D = x.shape
    ts = _tile(S, ts)
    row = pl.BlockSpec((ts, D), lambda i: (i, 0))
    vec = pl.BlockSpec((1, D), lambda i: (0, 0))
    lvec = pl.BlockSpec((1, LANES), lambda i: (0, 0))

    def body(x_ref, g_ref, t_ref, y_ref, gate_ref, loss_ref, dg_ref, dx_ref, dy_ref, dgate_ref):
        _acc_init(pl.program_id(0), loss_ref, dg_ref, dgate_ref)
        xv, gv = x_ref[...], g_ref[...]
        r = lax.rsqrt(jnp.mean(xv * xv, axis=-1, keepdims=True) + EPS)
        xh = xv * r
        e = xh * gv - t_ref[...]
        loss_ref[...] += 0.5 * jnp.sum(jnp.mean(e * e, axis=-1, keepdims=True), axis=0, keepdims=True)
        dout = e * (1.0 / D)
        dg_ref[...] += _colsum(dout * xh)
        dxh = dout * gv
        dx = r * (dxh - xh * jnp.mean(dxh * xh, axis=-1, keepdims=True))
        dx_ref[...] = dx
        dy_ref[...] = (gate_ref[...] * dx).astype(BF16)
        dgate_ref[...] += _colsum(dx * y_ref[...].astype(F32))

    return pl.pallas_call(
        body, grid=(S // ts,), in_specs=[row, vec, row, row, vec], out_specs=[lvec, vec, row, row, vec],
        out_shape=[jax.ShapeDtypeStruct((1, LANES), F32), jax.ShapeDtypeStruct((1, D), F32),
                   jax.ShapeDtypeStruct((S, D), F32), jax.ShapeDtypeStruct((S, D), BF16),
                   jax.ShapeDtypeStruct((1, D), F32)],
        compiler_params=_params("arbitrary"), name=name)(x, g, target, y, gate)


def _head_mean(v, gmat):
    hi = v.astype(BF16)
    lo = (v - hi.astype(F32)).astype(BF16)
    return jnp.dot(hi, gmat, preferred_element_type=F32) + jnp.dot(lo, gmat, preferred_element_type=F32)


L_F, L_ONE, L_SHIFT = HEAD_DIM, HEAD_DIM + 3, HEAD_DIM + 6


def _lane():
    return lax.broadcasted_iota(jnp.int32, (1, LANES), 1)


def _split3(v):
    p1 = v.astype(BF16).astype(F32)
    r1 = v - p1
    p2 = r1.astype(BF16).astype(F32)
    p3 = (r1 - p2).astype(BF16).astype(F32)
    return p1, p2, p3


def _put3(lane, first, pieces):
    out = jnp.where(lane == first, pieces[0], 0.0)
    for k in (1, 2):
        out = out + jnp.where(lane == first + k, pieces[k], 0.0)
    return out


def _ones3(lane, first):
    return jnp.where((lane >= first) & (lane < first + 3), 1.0, 0.0)


def _lane_col(v, lane, idx):
    return jnp.sum(jnp.where(lane == idx, v, 0.0), axis=-1, keepdims=True)


def _head_of_pair(pair, e, lane):
    return jnp.where(lane < HEAD_DIM, pair if e == 0 else pltpu.roll(pair, HEAD_DIM, 1), 0.0)


def _pair_of_heads(even, odd, lane):
    return jnp.where(lane < HEAD_DIM, even, pltpu.roll(odd, HEAD_DIM, 1))


def _fox_prep_fwd(proj, fcum, qgain, kgain, gmat, *, name, ts=256):
    S = proj.shape[0]
    D = HEADS * HEAD_DIM
    ts = _tile(S, ts)
    scale = HEAD_DIM ** -0.5

    def body(p_ref, f_ref, qg_ref, kg_ref, gm_ref, q_ref, k_ref, v_ref):
        gm, lane, fc = gm_ref[...], _lane(), f_ref[...]
        for cpair in range(HEAD_PAIRS):
            qv = p_ref[:, pl.ds(cpair * LANES, LANES)].astype(F32)
            kv = p_ref[:, pl.ds(D + cpair * LANES, LANES)].astype(F32)
            vv = p_ref[:, pl.ds(2 * D + cpair * LANES, LANES)].astype(F32)
            qn = (qv * lax.rsqrt(_head_mean(qv * qv, gm) + EPS) * qg_ref[...]) * scale
            kn = kv * lax.rsqrt(_head_mean(kv * kv, gm) + EPS) * kg_ref[...]
            for e in range(2):
                h = 2 * cpair + e
                cols = pl.ds(h * LANES, LANES)
                f3 = _split3(_lane_col(fc, lane, h))
                q_ref[:, cols] = (_head_of_pair(qn, e, lane) + _put3(lane, L_F, f3) + _ones3(lane, L_ONE)).astype(BF16)
                k_ref[:, cols] = (_head_of_pair(kn, e, lane) + _ones3(lane, L_F)
                                  - _put3(lane, L_ONE, f3) + _ones3(lane, L_SHIFT)).astype(BF16)
                v_ref[:, cols] = (_head_of_pair(vv, e, lane) + _ones3(lane, L_F)).astype(BF16)

    vec = pl.BlockSpec((1, LANES), lambda i: (0, 0))
    wide = pl.BlockSpec((ts, HEADS * LANES), lambda i: (i, 0))
    return pl.pallas_call(
        body, grid=(S // ts,),
        in_specs=[pl.BlockSpec((ts, 3 * D), lambda i: (i, 0)), pl.BlockSpec((ts, LANES), lambda i: (i, 0)), vec, vec,
                  pl.BlockSpec((LANES, LANES), lambda i: (0, 0))],
        out_specs=[wide, wide, wide], out_shape=[jax.ShapeDtypeStruct((S, HEADS * LANES), BF16)] * 3,
        compiler_params=_params("parallel"), name=name)(proj, fcum, qgain, kgain, gmat)


def _fox_prep_bwd(proj, dq_aug, dk_aug, dv_aug, dog, qgain, kgain, gmat, *, name, ts=256):
    S = proj.shape[0]
    D = HEADS * HEAD_DIM
    ts = _tile(S, ts)
    scale = HEAD_DIM ** -0.5

    def body(p_ref, dq_ref, dk_ref, dv_ref, dog_ref, qg_ref, kg_ref, gm_ref, o_ref, df_ref, dqg_ref, dkg_ref):
        _acc_init(pl.program_id(0), dqg_ref, dkg_ref)
        gm, lane = gm_ref[...], _lane()
        df = jnp.zeros((ts, LANES), F32)
        for cpair in range(HEAD_PAIRS):
            tiles = []
            for e in range(2):
                h = 2 * cpair + e
                cols = pl.ds(h * LANES, LANES)
                tq, tk = dq_ref[:, cols], dk_ref[:, cols]
                df = jnp.where(lane == h, _lane_col(tq, lane, L_F) - _lane_col(tk, lane, L_ONE), df)
                tiles.append((tq, tk, dv_ref[:, cols].astype(F32)))
            pair = [_pair_of_heads(tiles[0][k], tiles[1][k], lane) for k in range(3)]
            for half, g_ref, dg_ref, mult in ((0, qg_ref, dqg_ref, scale), (1, kg_ref, dkg_ref, 1.0)):
                v = p_ref[:, pl.ds(half * D + cpair * LANES, LANES)].astype(F32)
                r = lax.rsqrt(_head_mean(v * v, gm) + EPS)
                xh = v * r
                dn = pair[half] * mult
                dg_ref[...] += _colsum(dn * xh)
                dxh = dn * g_ref[...]
                o_ref[:, pl.ds(half * D + cpair * LANES, LANES)] = (r * (dxh - xh * _head_mean(dxh * xh, gm))).astype(BF16)
            o_ref[:, pl.ds(2 * D + cpair * LANES, LANES)] = pair[2].astype(BF16)
        o_ref[:, pl.ds(3 * D, D)] = dog_ref[...]
        o_ref[:, pl.ds(4 * D, LANES)] = jnp.zeros((ts, LANES), BF16)
        df_ref[...] = df

    row = pl.BlockSpec((ts, D), lambda i: (i, 0))
    wide = pl.BlockSpec((ts, HEADS * LANES), lambda i: (i, 0))
    vec = pl.BlockSpec((1, LANES), lambda i: (0, 0))
    return pl.pallas_call(
        body, grid=(S // ts,),
        in_specs=[pl.BlockSpec((ts, 2 * D), lambda i: (i, 0)), wide, wide, wide, row, vec, vec,
                  pl.BlockSpec((LANES, LANES), lambda i: (0, 0))],
        out_specs=[pl.BlockSpec((ts, 4 * D + LANES), lambda i: (i, 0)), pl.BlockSpec((ts, LANES), lambda i: (i, 0)), vec, vec],
        out_shape=[jax.ShapeDtypeStruct((S, 4 * D + LANES), BF16), jax.ShapeDtypeStruct((S, LANES), F32),
                   jax.ShapeDtypeStruct((1, LANES), F32), jax.ShapeDtypeStruct((1, LANES), F32)],
        compiler_params=_params("arbitrary"), name=name)(proj, dq_aug, dk_aug, dv_aug, dog, qgain, kgain, gmat)


def _log_sigmoid(z):
    return jnp.minimum(z, 0.0) - jnp.log(1.0 + jnp.exp(-jnp.abs(z)))


def _fox_decay_fwd(fl, bf, *, name, tb=256):
    S = fl.shape[0]
    tb = _tile(S, tb)

    def body(fl_ref, b_ref, o_ref, carry):
        @pl.when(pl.program_id(0) == 0)
        def _():
            carry[...] = jnp.zeros_like(carry)
        logf = _log_sigmoid(fl_ref[...] + b_ref[...])
        tri = (lax.broadcasted_iota(jnp.int32, (tb, tb), 1) <= lax.broadcasted_iota(jnp.int32, (tb, tb), 0)).astype(F32)
        cs = jnp.dot(tri, logf, preferred_element_type=F32, precision=lax.Precision.HIGHEST) + carry[...]
        o_ref[...] = cs
        carry[...] = _row_of(cs, tb - 1)

    return pl.pallas_call(
        body, grid=(S // tb,),
        in_specs=[pl.BlockSpec((tb, LANES), lambda i: (i, 0)), pl.BlockSpec((1, LANES), lambda i: (0, 0))],
        out_specs=pl.BlockSpec((tb, LANES), lambda i: (i, 0)),
        out_shape=jax.ShapeDtypeStruct((S, LANES), F32), scratch_shapes=[pltpu.VMEM((1, LANES), F32)],
        compiler_params=_params("arbitrary"), name=name)(fl, bf)


def _fox_decay_bwd(dF, fl, bf, dproj, *, name, tb=256):
    S = fl.shape[0]
    tb = _tile(S, tb)
    n = S // tb
    last_col = dproj.shape[1] // LANES - 1

    def body(df_ref, fl_ref, b_ref, dproj_hbm, o_ref, db_ref, carry):
        del dproj_hbm
        @pl.when(pl.program_id(0) == 0)
        def _():
            carry[...] = jnp.zeros_like(carry)
            db_ref[...] = jnp.zeros_like(db_ref)
        tri = (lax.broadcasted_iota(jnp.int32, (tb, tb), 1) >= lax.broadcasted_iota(jnp.int32, (tb, tb), 0)).astype(F32)
        rc = jnp.dot(tri, df_ref[...], preferred_element_type=F32, precision=lax.Precision.HIGHEST) + carry[...]
        carry[...] = _row_of(rc, 0)
        dfl = rc * jax.nn.sigmoid(-(fl_ref[...] + b_ref[...]))
        o_ref[...] = dfl.astype(BF16)
        db_ref[...] += _colsum(dfl)

    rev = pl.BlockSpec((tb, LANES), lambda i: (n - 1 - i, 0))
    vec = pl.BlockSpec((1, LANES), lambda i: (0, 0))
    return pl.pallas_call(
        body, grid=(n,), in_specs=[rev, rev, vec, pl.BlockSpec(memory_space=pl.ANY)],
        out_specs=[pl.BlockSpec((tb, LANES), lambda i: (n - 1 - i, last_col)), vec],
        out_shape=[jax.ShapeDtypeStruct(dproj.shape, BF16), jax.ShapeDtypeStruct((1, LANES), F32)],
        scratch_shapes=[pltpu.VMEM((1, LANES), F32)], input_output_aliases={3: 0},
        compiler_params=_params("arbitrary"), name=name)(dF, fl, bf, dproj)


_NT = (((1,), (1,)), ((), ()))
_TN = (((0,), (0,)), ((), ()))


def _causal(T, transposed=False):
    r, c = lax.broadcasted_iota(jnp.int32, (T, T), 0), lax.broadcasted_iota(jnp.int32, (T, T), 1)
    return r <= c if transposed else c <= r


def _with_shift(q_tile, shift, lane):
    keep = jnp.where((lane >= L_SHIFT) & (lane < L_SHIFT + 3), 0.0, q_tile)
    return (keep + _put3(lane, L_SHIFT, _split3(-shift))).astype(BF16)


def _ride_along(xchg, n_in, n_out, grid):
    if xchg is None:
        return (lambda body: body), [], [], [], [], []
    arrs, scatter = xchg
    n = len(arrs)

    def wrap(body):
        def wrapped(*refs):
            own_in, x_in = refs[:n_in], refs[n_in:n_in + n]
            own_out, x_out = refs[n_in + n:n_in + n + n_out], refs[n_in + n + n_out:n_in + 2 * n + n_out]
            rest = refs[n_in + 2 * n + n_out:]
            own_scratch, sems = rest[:len(rest) - 3], rest[len(rest) - 3:]
            ids = [pl.program_id(d) for d in range(len(grid))]
            first = functools.reduce(jnp.logical_and, [i == 0 for i in ids])
            last = functools.reduce(jnp.logical_and, [i == g - 1 for i, g in zip(ids, grid)])

            @pl.when(first)
            def _():
                for cp in _xchg_copies(x_in, x_out, scatter, *sems):
                    cp.start()

            body(*own_in, *own_out, *own_scratch)

            @pl.when(last)
            def _():
                for cp in _xchg_copies(x_in, x_out, scatter, *sems):
                    cp.wait()

        return wrapped

    return wrap, [_HBM] * n, [_HBM] * n, _xchg_out_shapes(arrs, scatter), _xchg_sems(n), list(arrs)


def _attn_rowmax(q_aug, k_aug, xchg=None, *, name, T=1024):
    S = q_aug.shape[0]
    T = _tile(S, T)
    n = S // T
    wrap, x_in, x_out, x_shapes, x_sems, x_ops = _ride_along(xchg, 2, 1, (HEADS, n, n))

    def body(q_ref, k_ref, o_ref, m_s):
        i, j = pl.program_id(1), pl.program_id(2)

        @pl.when(j == 0)
        def _():
            m_s[...] = jnp.full_like(m_s, NEG)

        def step(diag):
            s = lax.dot_general(q_ref[...], k_ref[...], _NT, preferred_element_type=F32)
            if diag:
                s = jnp.where(_causal(T), s, NEG)
            m = m_s[...]
            for cb in range(T // LANES):
                m = jnp.maximum(m, s[:, cb * LANES:(cb + 1) * LANES])
            m_s[...] = m

        @pl.when(j < i)
        def _():
            step(False)

        @pl.when(j == i)
        def _():
            step(True)
            o_ref[...] = _with_shift(q_ref[...].astype(F32), jnp.max(m_s[...], axis=-1, keepdims=True), _lane())

    qrow = pl.BlockSpec((T, LANES), lambda h, i, j: (i, h))
    outs = pl.pallas_call(
        wrap(body), grid=(HEADS, n, n),
        in_specs=[qrow, pl.BlockSpec((T, LANES), lambda h, i, j: (jnp.minimum(j, i), h))] + x_in,
        out_specs=[qrow] + x_out, out_shape=[jax.ShapeDtypeStruct(q_aug.shape, BF16)] + x_shapes,
        scratch_shapes=[pltpu.VMEM((T, LANES), F32)] + x_sems,
        compiler_params=_params("arbitrary", "arbitrary", "arbitrary"), name=name)(q_aug, k_aug, *x_ops)
    return outs[0], outs[1:]


def _attn_fwd(q_max, k_aug, v_aug, *, name, T=1024):
    S = q_max.shape[0]
    T = _tile(S, T)
    n = S // T

    def body(q_ref, k_ref, v_ref, o_ref, qb_ref, acc_s):
        i, j = pl.program_id(1), pl.program_id(2)

        @pl.when(j == 0)
        def _():
            acc_s[...] = jnp.zeros_like(acc_s)

        def step(diag):
            s = lax.dot_general(q_ref[...], k_ref[...], _NT, preferred_element_type=F32)
            if diag:
                s = jnp.where(_causal(T), s, NEG)
            acc_s[...] += jnp.dot(jnp.exp(s).astype(BF16), v_ref[...], preferred_element_type=F32)

        @pl.when(j < i)
        def _():
            step(False)

        @pl.when(j == i)
        def _():
            step(True)
            lane = _lane()
            acc = acc_s[...]
            l = _lane_col(acc, lane, L_F)
            o_ref[...] = acc / l
            qf = q_ref[...].astype(F32)
            row_max = -jnp.sum(jnp.where((lane >= L_SHIFT) & (lane < L_SHIFT + 3), qf, 0.0), axis=-1, keepdims=True)
            qb_ref[...] = _with_shift(qf, row_max + jnp.log(l), lane)

    qrow = pl.BlockSpec((T, LANES), lambda h, i, j: (i, h))
    kv = pl.BlockSpec((T, LANES), lambda h, i, j: (jnp.minimum(j, i), h))
    return pl.pallas_call(
        body, grid=(HEADS, n, n), in_specs=[qrow, kv, kv], out_specs=[qrow, qrow],
        out_shape=[jax.ShapeDtypeStruct(q_max.shape, F32), jax.ShapeDtypeStruct(q_max.shape, BF16)],
        scratch_shapes=[pltpu.VMEM((T, LANES), F32)],
        compiler_params=_params("parallel", "parallel", "arbitrary"), name=name)(q_max, k_aug, v_aug)


def _attn_bwd(q_lse, k_aug, v_aug, do_aug, xchg=None, *, name, T=1024):
    S = q_lse.shape[0]
    T = _tile(S, T)
    n = S // T
    wrap, x_in, x_out, x_shapes, x_sems, x_ops = _ride_along(xchg, 4, 3, (HEADS, n, n))

    def body(q_ref, do_ref, k_ref, v_ref, dq_ref, dk_ref, dv_ref, dq_s, dk_s, dv_s):
        j, i = pl.program_id(1), pl.program_id(2)

        def step(diag):
            q, do, k, v = q_ref[...], do_ref[...], k_ref[...], v_ref[...]
            st = lax.dot_general(k, q, _NT, preferred_element_type=F32)
            if diag:
                st = jnp.where(_causal(T, transposed=True), st, NEG)
            pt = jnp.exp(st)
            dst = (pt * lax.dot_general(v, do, _NT, preferred_element_type=F32)).astype(BF16)
            dv_s[...] += jnp.dot(pt.astype(BF16), do, preferred_element_type=F32)
            dk_s[...] += jnp.dot(dst, q, preferred_element_type=F32)
            upd = lax.dot_general(dst, k, _TN, preferred_element_type=F32)

            @pl.when(j == 0)
            def _():
                dq_s[i] = upd

            @pl.when(j > 0)
            def _():
                dq_s[i] += upd

        @pl.when(i == j)
        def _():
            dk_s[...] = jnp.zeros_like(dk_s)
            dv_s[...] = jnp.zeros_like(dv_s)
            step(True)
            dq_ref[...] = dq_s[j]

        @pl.when(i > j)
        def _():
            step(False)

        @pl.when(i == n - 1)
        def _():
            dk_ref[...] = dk_s[...]
            dv_ref[...] = dv_s[...].astype(BF16)

    qrow = pl.BlockSpec((T, LANES), lambda h, j, i: (jnp.maximum(i, j), h))
    kv = pl.BlockSpec((T, LANES), lambda h, j, i: (j, h))
    outs = pl.pallas_call(
        wrap(body), grid=(HEADS, n, n), in_specs=[qrow, qrow, kv, kv] + x_in, out_specs=[kv, kv, kv] + x_out,
        out_shape=[jax.ShapeDtypeStruct(q_lse.shape, F32), jax.ShapeDtypeStruct(q_lse.shape, F32),
                   jax.ShapeDtypeStruct(q_lse.shape, BF16)] + x_shapes,
        scratch_shapes=[pltpu.VMEM((n, T, LANES), F32), pltpu.VMEM((T, LANES), F32), pltpu.VMEM((T, LANES), F32)] + x_sems,
        compiler_params=_params("arbitrary", "arbitrary", "arbitrary"), name=name)(q_lse, do_aug, k_aug, v_aug, *x_ops)
    return outs[0], outs[1], outs[2], outs[3:]


def _fox_gate_fwd(att_aug, proj, *, name, ts=256):
    S = att_aug.shape[0]
    D = HEADS * HEAD_DIM
    ts = _tile(S, ts)

    def body(a_ref, o_ref, att_ref, out_ref):
        lane = _lane()
        for cpair in range(HEAD_PAIRS):
            cols = pl.ds(cpair * LANES, LANES)
            pair = _pair_of_heads(a_ref[:, pl.ds(2 * cpair * LANES, LANES)], a_ref[:, pl.ds((2 * cpair + 1) * LANES, LANES)], lane)
            att_ref[:, cols] = pair
            out_ref[:, cols] = (pair * jax.nn.sigmoid(o_ref[:, cols].astype(F32))).astype(BF16)

    row = pl.BlockSpec((ts, D), lambda i: (i, 0))
    return pl.pallas_call(
        body, grid=(S // ts,),
        in_specs=[pl.BlockSpec((ts, HEADS * LANES), lambda i: (i, 0)), pl.BlockSpec((ts, D), lambda i: (i, 3))],
        out_specs=[row, row], out_shape=[jax.ShapeDtypeStruct((S, D), F32), jax.ShapeDtypeStruct((S, D), BF16)],
        compiler_params=_params("parallel"), name=name)(att_aug, proj)


def _fox_gate_bwd(da, att, proj, *, name, ts=256):
    S, D = att.shape
    ts = _tile(S, ts)

    def body(da_ref, a_ref, o_ref, do_ref, dog_ref):
        lane = _lane()
        for cpair in range(HEAD_PAIRS):
            cols = pl.ds(cpair * LANES, LANES)
            dav, av = da_ref[:, cols].astype(F32), a_ref[:, cols]
            sg = jax.nn.sigmoid(o_ref[:, cols].astype(F32))
            datt = (dav * sg).astype(BF16).astype(F32)
            dog_ref[:, cols] = (dav * av * sg * (1.0 - sg)).astype(BF16)
            prod = datt * av
            for e in range(2):
                in_head = (lane < HEAD_DIM) if e == 0 else (lane >= HEAD_DIM)
                delta = jnp.sum(jnp.where(in_head, prod, 0.0), axis=-1, keepdims=True)
                tile = _head_of_pair(datt, e, lane) + _put3(lane, L_F, _split3(-delta))
                do_ref[:, pl.ds((2 * cpair + e) * LANES, LANES)] = tile.astype(BF16)

    row = pl.BlockSpec((ts, D), lambda i: (i, 0))
    return pl.pallas_call(
        body, grid=(S // ts,), in_specs=[row, row, pl.BlockSpec((ts, D), lambda i: (i, 3))],
        out_specs=[pl.BlockSpec((ts, HEADS * LANES), lambda i: (i, 0)), row],
        out_shape=[jax.ShapeDtypeStruct((S, HEADS * LANES), BF16), jax.ShapeDtypeStruct((S, D), BF16)],
        compiler_params=_params("parallel"), name=name)(da, att, proj)


def _row_of(block, r):
    rows = lax.broadcasted_iota(jnp.int32, block.shape, 0)
    return jnp.sum(jnp.where(rows == r, block, 0.0), axis=0, keepdims=True)


def _shift_down(cur, tail, k):
    out = pltpu.roll(cur, k, 0)
    rows = lax.broadcasted_iota(jnp.int32, cur.shape, 0)
    for r in range(k):
        out = jnp.where(rows == r, _row_of(tail, tail.shape[0] - k + r), out)
    return out


def _shift_up(cur, head, k):
    n = cur.shape[0]
    out = pltpu.roll(cur, n - k, 0)
    rows = lax.broadcasted_iota(jnp.int32, cur.shape, 0)
    for r in range(k):
        out = jnp.where(rows == n - k + r, _row_of(head, r), out)
    return out


HALO = 16


def _conv_taps(a_ref, tail_ref, first):
    cur = a_ref[...].astype(F32)
    tail = jnp.where(first, 0.0, tail_ref[...].astype(F32))
    return cur, _shift_down(cur, tail, 1), _shift_down(cur, tail, 2)


def _conv_gate_fwd(a, cw, cb, *, name, ts=256, tc=1408):
    S, F2 = a.shape
    F = F2 // 2
    ts, tc = _tile(S, ts), _tile(F, tc)
    nc = F // tc
    sub = ts // HALO

    def body(g_ref, gt_ref, v_ref, vt_ref, wg_ref, wv_ref, bg_ref, bv_ref, o_ref):
        first = pl.program_id(1) == 0
        halves = []
        for a_ref, t_ref, w_ref, b_ref in ((g_ref, gt_ref, wg_ref, bg_ref), (v_ref, vt_ref, wv_ref, bv_ref)):
            a0, a1, a2 = _conv_taps(a_ref, t_ref, first)
            halves.append(a2 * w_ref[0:1, :] + a1 * w_ref[1:2, :] + a0 * w_ref[2:3, :] + b_ref[...])
        g, val = halves
        o_ref[...] = (g * jax.nn.sigmoid(g) * val).astype(BF16)

    cur = lambda off: pl.BlockSpec((ts, tc), lambda j, i: (i, off + j))
    tail = lambda off: pl.BlockSpec((HALO, tc), lambda j, i: (jnp.maximum(i * sub - 1, 0), off + j))
    wsp = lambda off: pl.BlockSpec((CONV_WIDTH, tc), lambda j, i: (0, off + j))
    bsp = lambda off: pl.BlockSpec((1, tc), lambda j, i: (0, off + j))
    return pl.pallas_call(
        body, grid=(nc, S // ts),
        in_specs=[cur(0), tail(0), cur(nc), tail(nc), wsp(0), wsp(nc), bsp(0), bsp(nc)],
        out_specs=pl.BlockSpec((ts, tc), lambda j, i: (i, j)),
        out_shape=jax.ShapeDtypeStruct((S, F), BF16),
        compiler_params=_params("parallel", "parallel"), name=name)(a, a, a, a, cw, cw, cb, cb)


def _conv_gate_bwd_pre(a, dact, cw, cb, *, name, ts=256, tc=1408):
    S, F2 = a.shape
    F = F2 // 2
    ts, tc = _tile(S, ts), _tile(F, tc)
    nc = F // tc
    sub = ts // HALO

    def body(g_ref, gt_ref, v_ref, vt_ref, d_ref, wg_ref, wv_ref, bg_ref, bv_ref, dg_ref, dv_ref, sg_ref, sv_ref):
        first = pl.program_id(1) == 0
        _acc_init(pl.program_id(1), sg_ref, sv_ref)
        taps, pre = [], []
        for a_ref, t_ref, w_ref, b_ref in ((g_ref, gt_ref, wg_ref, bg_ref), (v_ref, vt_ref, wv_ref, bv_ref)):
            a0, a1, a2 = _conv_taps(a_ref, t_ref, first)
            taps.append((a2, a1, a0))
            pre.append(a2 * w_ref[0:1, :] + a1 * w_ref[1:2, :] + a0 * w_ref[2:3, :] + b_ref[...])
        g, val = pre
        d = d_ref[...].astype(F32)
        sg = jax.nn.sigmoid(g)
        dg = d * val * (sg * (1.0 + g * (1.0 - sg)))
        dval = d * (g * sg)
        for dd, o_ref, s_ref, tp in ((dg, dg_ref, sg_ref, taps[0]), (dval, dv_ref, sv_ref, taps[1])):
            o_ref[...] = dd.astype(BF16)
            for r in range(CONV_WIDTH):
                s_ref[r:r + 1, :] += _colsum(dd * tp[r])
            s_ref[CONV_WIDTH:CONV_WIDTH + 1, :] += _colsum(dd)

    cur = lambda off: pl.BlockSpec((ts, tc), lambda j, i: (i, off + j))
    tail = lambda off: pl.BlockSpec((HALO, tc), lambda j, i: (jnp.maximum(i * sub - 1, 0), off + j))
    wsp = lambda off: pl.BlockSpec((CONV_WIDTH, tc), lambda j, i: (0, off + j))
    bsp = lambda off: pl.BlockSpec((1, tc), lambda j, i: (0, off + j))
    acc = lambda off: pl.BlockSpec((8, tc), lambda j, i: (0, off + j))
    dpre_g, dpre_v, sums_g, sums_v = pl.pallas_call(
        body, grid=(nc, S // ts),
        in_specs=[cur(0), tail(0), cur(nc), tail(nc), cur(0), wsp(0), wsp(nc), bsp(0), bsp(nc)],
        out_specs=[cur(0), cur(0), acc(0), acc(0)],
        out_shape=[jax.ShapeDtypeStruct((S, F), BF16), jax.ShapeDtypeStruct((S, F), BF16),
                   jax.ShapeDtypeStruct((8, F), F32), jax.ShapeDtypeStruct((8, F), F32)],
        compiler_params=_params("parallel", "arbitrary"), name=name)(a, a, a, a, dact, cw, cw, cb, cb)
    return dpre_g, dpre_v, sums_g, sums_v


def _conv_bwd_taps(dpre_g, dpre_v, cw, *, name, ts=256, tc=1408):
    S, F = dpre_g.shape
    ts, tc = _tile(S, ts), _tile(F, tc)
    nc = F // tc
    sub = ts // HALO
    last = S // ts - 1

    def one_half(dpre, w_off, call_name):
        def body(d_ref, h_ref, w_ref, o_ref):
            cur = d_ref[...].astype(F32)
            head = jnp.where(pl.program_id(1) == last, 0.0, h_ref[...].astype(F32))
            o_ref[...] = (cur * w_ref[2:3, :] + _shift_up(cur, head, 1) * w_ref[1:2, :]
                          + _shift_up(cur, head, 2) * w_ref[0:1, :]).astype(BF16)

        return pl.pallas_call(
            body, grid=(nc, S // ts),
            in_specs=[pl.BlockSpec((ts, tc), lambda j, i: (i, j)),
                      pl.BlockSpec((HALO, tc), lambda j, i: (jnp.minimum((i + 1) * sub, S // HALO - 1), j)),
                      pl.BlockSpec((CONV_WIDTH, tc), lambda j, i: (0, w_off + j))],
            out_specs=pl.BlockSpec((ts, tc), lambda j, i: (i, j)),
            out_shape=jax.ShapeDtypeStruct((S, F), BF16),
            compiler_params=_params("parallel", "parallel"), name=call_name)(dpre, dpre, cw)

    return one_half(dpre_g, 0, f"{name}_g"), one_half(dpre_v, nc, f"{name}_v")


def _gelu_parts(z):
    z2 = z * z
    t = jnp.tanh(GELU_C0 * (z + GELU_C1 * z * z2))
    val = 0.5 * z * (1.0 + t)
    grad = 0.5 * (1.0 + t) + 0.5 * z * (1.0 - t * t) * GELU_C0 * (1.0 + 3.0 * GELU_C1 * z2)
    return val, grad


def _sgu_fwd(pre, b_in, vgain, vbias, wm, bsb, *, name, ts=256):
    S, W2 = pre.shape
    W = W2 // 2
    gd = W // SGU_GROUPS
    ts = _tile(S, ts)

    def body(p_ref, b_ref, vg_ref, vb_ref, wm_ref, bs_ref, y_ref):
        u = _gelu_parts(p_ref[:, pl.ds(0, W)].astype(F32) + b_ref[:, pl.ds(0, W)])[0]
        v = _gelu_parts(p_ref[:, pl.ds(W, W)].astype(F32) + b_ref[:, pl.ds(W, W)])[0]
        mu = jnp.mean(v, axis=-1, keepdims=True)
        vc = v - mu
        rstd = lax.rsqrt(jnp.mean(vc * vc, axis=-1, keepdims=True) + EPS)
        vn = ((vc * rstd) * vg_ref[...] + vb_ref[...]).astype(BF16)
        for blk in range(ts // SGU_BLOCK):
            r0 = blk * SGU_BLOCK
            for g in range(SGU_GROUPS):
                c0 = g * gd
                mixed = jnp.dot(wm_ref[g], vn[r0:r0 + SGU_BLOCK, c0:c0 + gd], preferred_element_type=F32) + bs_ref[g]
                y_ref[pl.ds(r0, SGU_BLOCK), pl.ds(c0, gd)] = (u[r0:r0 + SGU_BLOCK, c0:c0 + gd] * mixed).astype(BF16)

    full = lambda shape: pl.BlockSpec(shape, lambda i: (0,) * len(shape))
    return pl.pallas_call(
        body, grid=(S // ts,),
        in_specs=[pl.BlockSpec((ts, W2), lambda i: (i, 0)), full((1, W2)), full((1, W)), full((1, W)),
                  full((SGU_GROUPS, SGU_BLOCK, SGU_BLOCK)), full((SGU_GROUPS, SGU_BLOCK, gd))],
        out_specs=pl.BlockSpec((ts, W), lambda i: (i, 0)), out_shape=jax.ShapeDtypeStruct((S, W), BF16),
        compiler_params=_params("parallel"), name=name)(pre, b_in, vgain, vbias, wm, bsb)


def _sgu_bwd(pre, dy, b_in, vgain, vbias, wm, wmt, bsb, *, name, ts=256):
    S, W2 = pre.shape
    W = W2 // 2
    gd = W // SGU_GROUPS
    ts = _tile(S, ts)
    last = S // ts - 1

    def body(p_ref, dy_ref, b_ref, vg_ref, vb_ref, wm_ref, wmt_ref, bs_ref,
             dp_ref, db_ref, dvg_ref, dvb_ref, dws_ref, dbs_ref, du_s, dvn_s, dbs_s):
        step = pl.program_id(0)
        _acc_init(step, db_ref, dvg_ref, dvb_ref, dws_ref, dbs_s)
        u, gu = _gelu_parts(p_ref[:, pl.ds(0, W)].astype(F32) + b_ref[:, pl.ds(0, W)])
        v, gv = _gelu_parts(p_ref[:, pl.ds(W, W)].astype(F32) + b_ref[:, pl.ds(W, W)])
        mu = jnp.mean(v, axis=-1, keepdims=True)
        vc = v - mu
        rstd = lax.rsqrt(jnp.mean(vc * vc, axis=-1, keepdims=True) + EPS)
        vhat = vc * rstd
        vn = (vhat * vg_ref[...] + vb_ref[...]).astype(BF16)
        dyv = dy_ref[...].astype(F32)
        for blk in range(ts // SGU_BLOCK):
            r0 = blk * SGU_BLOCK
            for g in range(SGU_GROUPS):
                c0 = g * gd
                vn_g = vn[r0:r0 + SGU_BLOCK, c0:c0 + gd]
                dy_g = dyv[r0:r0 + SGU_BLOCK, c0:c0 + gd]
                mixed = jnp.dot(wm_ref[g], vn_g, preferred_element_type=F32) + bs_ref[g]
                dmix = dy_g * u[r0:r0 + SGU_BLOCK, c0:c0 + gd]
                dmix_b = dmix.astype(BF16)
                du_s[pl.ds(r0, SGU_BLOCK), pl.ds(c0, gd)] = dy_g * mixed
                dvn_s[pl.ds(r0, SGU_BLOCK), pl.ds(c0, gd)] = jnp.dot(wmt_ref[g], dmix_b, preferred_element_type=F32)
                dws_ref[g] += lax.dot_general(dmix_b, vn_g, _NT, preferred_element_type=F32)
                dbs_s[g] += dmix
        dvn = dvn_s[...]
        dvg_ref[...] += _colsum(dvn * vhat)
        dvb_ref[...] += _colsum(dvn)
        dvh = dvn * vg_ref[...]
        dv = rstd * (dvh - jnp.mean(dvh, axis=-1, keepdims=True) - vhat * jnp.mean(dvh * vhat, axis=-1, keepdims=True))
        dpu = du_s[...] * gu
        dpv = dv * gv
        dp_ref[:, pl.ds(0, W)] = dpu.astype(BF16)
        dp_ref[:, pl.ds(W, W)] = dpv.astype(BF16)
        db_ref[:, pl.ds(0, W)] += _colsum(dpu)
        db_ref[:, pl.ds(W, W)] += _colsum(dpv)

        @pl.when(step == last)
        def _():
            for g in range(SGU_GROUPS):
                dbs_ref[g] = jnp.broadcast_to(jnp.sum(dbs_s[g], axis=-1, keepdims=True), (SGU_BLOCK, SGU_BLOCK))

    full = lambda shape: pl.BlockSpec(shape, lambda i: (0,) * len(shape))
    gsq = (SGU_GROUPS, SGU_BLOCK, SGU_BLOCK)
    return pl.pallas_call(
        body, grid=(S // ts,),
        in_specs=[pl.BlockSpec((ts, W2), lambda i: (i, 0)), pl.BlockSpec((ts, W), lambda i: (i, 0)),
                  full((1, W2)), full((1, W)), full((1, W)), full(gsq), full(gsq), full((SGU_GROUPS, SGU_BLOCK, gd))],
        out_specs=[pl.BlockSpec((ts, W2), lambda i: (i, 0)), full((1, W2)), full((1, W)), full((1, W)), full(gsq), full(gsq)],
        out_shape=[jax.ShapeDtypeStruct((S, W2), BF16), jax.ShapeDtypeStruct((1, W2), F32),
                   jax.ShapeDtypeStruct((1, W), F32), jax.ShapeDtypeStruct((1, W), F32),
                   jax.ShapeDtypeStruct(gsq, F32), jax.ShapeDtypeStruct(gsq, F32)],
        scratch_shapes=[pltpu.VMEM((ts, W), F32), pltpu.VMEM((ts, W), F32), pltpu.VMEM((SGU_GROUPS, SGU_BLOCK, gd), F32)],
        compiler_params=_params("arbitrary"), name=name)(pre, dy, b_in, vgain, vbias, wm, wmt, bsb)


def _ffn_fwd(x, mods, n2g, w_up, cw, cb, w_down, tag):
    sh, sc, gate = mods
    h = _norm_mod_fwd(x, n2g, sh, sc, name=f"{tag}_norm_fwd")
    a = _mm(h, w_up, out_dtype=BF16, tn=1408, name=f"{tag}_up")
    act = _conv_gate_fwd(a, cw, cb, name=f"{tag}_conv_fwd")
    x_out, y = _mm(act, w_down, tk=1408, tm=512, res=(x, gate), name=f"{tag}_down")
    return x_out, (x, h, a, act, y)


def _ffn_bwd(dy, saved, mods, n2g, w_up, cw, cb, w_down, dres, prev, tag):
    x, h, a, act, _ = saved
    sh, sc, gate = mods
    dact = _mm(dy, w_down, tb=True, out_dtype=BF16, tn=1408, name=f"{tag}_down_dx")
    dw_down = _mm(act, dy, ta=True, out_dtype=BF16, tn=1024, name=f"{tag}_down_dw")
    dpre_g, dpre_v, sums_g, sums_v = _conv_gate_bwd_pre(a, dact, cw, cb, name=f"{tag}_conv_bwd")
    da_g, da_v = _conv_bwd_taps(dpre_g, dpre_v, cw, name=f"{tag}_conv_taps")
    da = jnp.concatenate([da_g, da_v], axis=1)
    dh = _mm(da, w_up, tb=True, tk=1408, name=f"{tag}_up_dx")
    dw_up = _mm(h, da, ta=True, out_dtype=BF16, tn=1408, name=f"{tag}_up_dw")
    outs = _norm_mod_bwd(dh, x, n2g, sc, dres, prev, name=f"{tag}_norm_bwd")
    sums = jnp.concatenate([sums_g, sums_v], axis=1)
    return outs, dict(w_up=dw_up, w_down=dw_down, conv_w=sums[0:CONV_WIDTH], conv_b=sums[CONV_WIDTH])


def _local_step(x, target, w, mods, late=None, early=None):
    S, D = x.shape
    lane = jnp.arange(LANES)
    gmat = jnp.where((lane[:, None] // HEAD_DIM) == (lane[None, :] // HEAD_DIM), 1.0 / HEAD_DIM, 0.0).astype(BF16)
    qg2 = jnp.tile(w["fox_q_gain"].reshape(1, HEAD_DIM), (1, 2))
    kg2 = jnp.tile(w["fox_k_gain"].reshape(1, HEAD_DIM), (1, 2))
    bf_pad = jnp.pad(w["fox_b_f"].reshape(1, HEADS), ((0, 0), (0, LANES - HEADS)))
    w_in_pad = jnp.pad(w["fox_w_in"], ((0, 0), (0, 4 * D + LANES - w["fox_w_in"].shape[1])))
    w_qkvo, w_f = w_in_pad[:, :4 * D], w_in_pad[:, 4 * D:]
    tpos = jnp.arange(SGU_BLOCK)
    smask = (tpos[None, :] // SGU_CHUNK) <= (tpos[:, None] // SGU_CHUNK)
    wm32 = jnp.where(smask[None], w["sgu_w_s"], 0.0)
    wm, wmt = wm32.astype(BF16), jnp.swapaxes(wm32, 1, 2).astype(BF16)
    gd = w["sgu_v_gain"].shape[-1] // SGU_GROUPS
    bsb = jnp.broadcast_to(w["sgu_b_s"][:, :, None], (SGU_GROUPS, SGU_BLOCK, gd))
    vec = lambda v: v.reshape(1, -1)

    sh1, sc1, g1 = mods[0][0:3]
    h0 = _norm_mod_fwd(x, vec(w["norm1_g"][0]), sh1, sc1, name="fox_norm_fwd")
    proj = _mm(h0, w_qkvo, out_dtype=BF16, name="fox_proj")
    fl = _mm(h0, w_f, name="fox_forget_proj")
    fcum = _fox_decay_fwd(fl, bf_pad, name="fox_decay")
    q_aug, k_aug, v_aug = _fox_prep_fwd(proj, fcum, qg2, kg2, gmat, name="fox_qk_norm")
    xchg = None if late is None else (late[0], [False] * len(late[0]))
    q_max, gathered = _attn_rowmax(q_aug, k_aug, xchg, name="fox_attn_rowmax")
    if late is not None:
        w = {**w, **late[1](gathered)}
    att_aug, q_lse = _attn_fwd(q_max, k_aug, v_aug, name="fox_attn_fwd")
    att, ag = _fox_gate_fwd(att_aug, proj, name="fox_gate_fwd")
    x1, y_fox = _mm(ag, w["fox_w_out"], tm=512, res=(x, g1), name="fox_out")
    x2, ffn0 = _ffn_fwd(x1, mods[0][3:6], vec(w["norm2_g"][0]), w["ffn_w_up"][0], w["ffn_conv_w"][0],
                        vec(w["ffn_conv_b"][0]), w["ffn_w_down"][0], "ffn0")

    sh1b, sc1b, g1b = mods[1][0:3]
    h1 = _norm_mod_fwd(x2, vec(w["norm1_g"][1]), sh1b, sc1b, name="sgu_norm_fwd")
    pre = _mm(h1, w["sgu_w_in"], out_dtype=BF16, name="sgu_in")
    b_in, vg, vb = vec(w["sgu_b_in"]), vec(w["sgu_v_gain"]), vec(w["sgu_v_bias"])
    ys = _sgu_fwd(pre, b_in, vg, vb, wm, bsb, name="sgu_core_fwd")
    x3, y_sgu = _mm(ys, w["sgu_w_out"], tm=512, res=(x2, g1b), name="sgu_out")
    x4, ffn1 = _ffn_fwd(x3, mods[1][3:6], vec(w["norm2_g"][1]), w["ffn_w_up"][1], w["ffn_conv_w"][1],
                        vec(w["ffn_conv_b"][1]), w["ffn_w_down"][1], "ffn1")

    loss, d_final_g, dx4, dy_ffn1, dgate_ffn1 = _final_loss(x4, vec(w["final_g"]), target, ffn1[4], mods[1][5], name="final_loss")

    (dx3, dn2g_1, dsh2_1, dsc2_1, dy_sgu, dgate_sgu), g_ffn1 = _ffn_bwd(
        dy_ffn1, ffn1, mods[1][3:6], vec(w["norm2_g"][1]), w["ffn_w_up"][1], w["ffn_conv_w"][1], vec(w["ffn_conv_b"][1]),
        w["ffn_w_down"][1], dx4, (y_sgu, g1b), "ffn1")

    dys = _mm(dy_sgu, w["sgu_w_out"], tb=True, out_dtype=BF16, name="sgu_out_dx")
    dw_sgu_out = _mm(ys, dy_sgu, ta=True, out_dtype=BF16, name="sgu_out_dw")
    dpre, db_in, dvg, dvb, dws, dbs = _sgu_bwd(pre, dys, b_in, vg, vb, wm, wmt, bsb, name="sgu_core_bwd")
    dh1 = _mm(dpre, w["sgu_w_in"], tb=True, name="sgu_in_dx")
    dw_sgu_in = _mm(h1, dpre, ta=True, out_dtype=BF16, name="sgu_in_dw")
    dx2, dn1g_1, dsh1_1, dsc1_1, dy_ffn0, dgate_ffn0 = _norm_mod_bwd(
        dh1, x2, vec(w["norm1_g"][1]), sc1b, dx3, (ffn0[4], mods[0][5]), name="sgu_norm_bwd")

    (dx1, dn2g_0, dsh2_0, dsc2_0, dy_fox, dgate_fox), g_ffn0 = _ffn_bwd(
        dy_ffn0, ffn0, mods[0][3:6], vec(w["norm2_g"][0]), w["ffn_w_up"][0], w["ffn_conv_w"][0], vec(w["ffn_conv_b"][0]),
        w["ffn_w_down"][0], dx2, (y_fox, g1), "ffn0")

    dag = _mm(dy_fox, w["fox_w_out"], tb=True, out_dtype=BF16, name="fox_out_dx")
    dw_fox_out = _mm(ag, dy_fox, ta=True, out_dtype=BF16, name="fox_out_dw")
    do_aug, dog = _fox_gate_bwd(dag, att, proj, name="fox_gate_bwd")
    grads = dict(
        sgu_w_in=dw_sgu_in, sgu_b_in=db_in[0], sgu_v_gain=dvg[0], sgu_v_bias=dvb[0],
        sgu_w_s=jnp.where(smask[None], dws, 0.0), sgu_b_s=dbs[:, :, 0], sgu_w_out=dw_sgu_out,
        ffn_w_up=jnp.stack([g_ffn0["w_up"], g_ffn1["w_up"]]),
        ffn_conv_w=jnp.stack([g_ffn0["conv_w"], g_ffn1["conv_w"]]),
        ffn_conv_b=jnp.stack([g_ffn0["conv_b"], g_ffn1["conv_b"]]),
        ffn_w_down=jnp.stack([g_ffn0["w_down"], g_ffn1["w_down"]]),
        final_g=d_final_g[0],
    )
    xchg = None
    if early is not None:
        blocks = early(grads)
        xchg = (blocks, [True] * len(blocks))
    dq_aug, dk_aug, dv_aug, exchanged = _attn_bwd(q_lse, k_aug, v_aug, do_aug, xchg, name="fox_attn_bwd")
    dproj, dF, dqg, dkg = _fox_prep_bwd(proj, dq_aug, dk_aug, dv_aug, dog, qg2, kg2, gmat, name="fox_qk_norm_bwd")
    dproj, dbf = _fox_decay_bwd(dF, fl, bf_pad, dproj, name="fox_decay_bwd")
    dh0 = _mm(dproj, w_in_pad, tb=True, tk=1408, name="fox_proj_dx")
    dw_fox_in = _mm(h0, dproj, ta=True, out_dtype=BF16, tn=1408, name="fox_proj_dw")
    dx0, dn1g_0, dsh1_0, dsc1_0 = _norm_mod_bwd(dh0, x, vec(w["norm1_g"][0]), sc1, dx1, None, name="fox_norm_bwd")

    dmod0 = jnp.concatenate([dsh1_0, dsc1_0, dgate_fox, dsh2_0, dsc2_0, dgate_ffn0], axis=1)
    dmod1 = jnp.concatenate([dsh1_1, dsc1_1, dgate_sgu, dsh2_1, dsc2_1, dgate_ffn1], axis=1)
    grads.update(
        fox_w_in=dw_fox_in[:, :w["fox_w_in"].shape[1]],
        fox_b_f=dbf[0, :HEADS],
        fox_q_gain=dqg[0, :HEAD_DIM] + dqg[0, HEAD_DIM:],
        fox_k_gain=dkg[0, :HEAD_DIM] + dkg[0, HEAD_DIM:],
        fox_w_out=dw_fox_out,
        ada_b=jnp.concatenate([dmod0, dmod1], axis=0),
        norm1_g=jnp.concatenate([dn1g_0, dn1g_1], axis=0), norm2_g=jnp.concatenate([dn2g_0, dn2g_1], axis=0),
    )
    return loss[0, 0], dx0, grads, exchanged


_HBM = pl.BlockSpec(memory_space=pl.ANY)
N_PEER = N_DEV - 1


def _xchg_out_shapes(arrs, scatter):
    return [jax.ShapeDtypeStruct(a.shape if s else (N_DEV,) + a.shape, a.dtype) for a, s in zip(arrs, scatter)]


def _xchg_sems(n):
    return [pltpu.SemaphoreType.DMA((n * N_PEER,)), pltpu.SemaphoreType.DMA((n * N_PEER,)), pltpu.SemaphoreType.DMA((n,))]


def _xchg_copies(ins, outs, scatter, send, recv, loc):
    x, y, c = lax.axis_index("x"), lax.axis_index("y"), lax.axis_index("c")
    me = 4 * x + 2 * y + c
    copies = []
    for a in range(len(ins)):
        copies.append(pltpu.make_async_copy(ins[a].at[me] if scatter[a] else ins[a], outs[a].at[me], loc.at[a]))
        for k in range(1, N_DEV):
            px = 1 - x if k & 4 else x
            py = 1 - y if k & 2 else y
            pc = 1 - c if k & 1 else c
            copies.append(pltpu.make_async_remote_copy(
                src_ref=ins[a].at[4 * px + 2 * py + pc] if scatter[a] else ins[a], dst_ref=outs[a].at[me],
                send_sem=send.at[a * N_PEER + k - 1], recv_sem=recv.at[a * N_PEER + k - 1],
                device_id=(px, py, pc), device_id_type=MESH))
    return copies


def _exchange(arrs, scatter, *, name):
    n = len(arrs)

    def body(*refs):
        copies = _xchg_copies(refs[:n], refs[n:2 * n], scatter, *refs[2 * n:])
        for cp in copies:
            cp.start()
        for cp in copies:
            cp.wait()

    return pl.pallas_call(
        body, in_specs=[_HBM] * n, out_specs=[_HBM] * n, out_shape=_xchg_out_shapes(arrs, scatter),
        scratch_shapes=_xchg_sems(n),
        compiler_params=pltpu.CompilerParams(has_side_effects=True), name=name)(*arrs)


def _adamw(w, parts, m, v, *, name, tr=256):
    R, C = w.shape
    P = parts.shape[0]
    tr = next(t for t in range(min(R, tr), 0, -1) if R % t == 0 and (t % 16 == 0 or t == R))
    c1 = 1.0 - ADAM_B1 ** ADAM_STEP
    c2 = 1.0 - ADAM_B2 ** ADAM_STEP

    def body(w_ref, p_ref, m_ref, v_ref, g_ref, d_ref, mo_ref, vo_ref):
        g = p_ref[0].astype(F32)
        for p in range(1, P):
            g = g + p_ref[p].astype(F32)
        mn = ADAM_B1 * m_ref[...] + (1.0 - ADAM_B1) * g
        vn = ADAM_B2 * v_ref[...] + (1.0 - ADAM_B2) * (g * g)
        g_ref[...] = g
        mo_ref[...] = mn
        vo_ref[...] = vn
        d_ref[...] = -ADAM_LR * ((mn / c1) / (jnp.sqrt(vn / c2) + ADAM_EPS) + ADAM_WD * w_ref[...])

    row = pl.BlockSpec((tr, C), lambda i: (i, 0))
    return pl.pallas_call(
        body, grid=(R // tr,), in_specs=[row, pl.BlockSpec((P, tr, C), lambda i: (0, i, 0)), row, row],
        out_specs=[row] * 4, out_shape=[jax.ShapeDtypeStruct((R, C), F32)] * 4,
        compiler_params=_params("parallel"), name=name)(w, parts, m, v)


def _sum_parts(parts, *, name):
    P, R, C = parts.shape

    def body(p_ref, o_ref):
        g = p_ref[0]
        for p in range(1, P):
            g = g + p_ref[p]
        o_ref[...] = g

    return pl.pallas_call(body, out_shape=jax.ShapeDtypeStruct((R, C), F32), name=name)(parts)


WEIGHTS = ["fox_w_in", "fox_b_f", "fox_q_gain", "fox_k_gain", "fox_w_out", "sgu_w_in", "sgu_b_in", "sgu_v_gain",
           "sgu_v_bias", "sgu_w_s", "sgu_b_s", "sgu_w_out", "ffn_w_up", "ffn_conv_w", "ffn_conv_b", "ffn_w_down",
           "ada_w", "ada_b", "norm1_g", "norm2_g", "final_g"]
BIG_AXIS = dict(fox_w_in=1, fox_w_out=0, sgu_w_in=1, sgu_w_out=0, ffn_w_up=1, ffn_w_down=0, ada_w=1)
SMALL_SHARDED = ["sgu_b_in", "sgu_v_gain", "sgu_v_bias", "ffn_conv_w"]
SINGLE_LAYER = ("fox_", "sgu_")
BEFORE_ATTENTION = ["fox_w_in", "fox_w_out"]
AFTER_ATTENTION = ["sgu_w_in", "sgu_w_out", "ffn_w_up", "ffn_w_down"]


def _assemble(stacked, layers, axis):
    _, lr, cc = stacked.shape
    r = lr // layers
    s4 = stacked.reshape(N_DEV, layers, r, cc)
    if axis == 0:
        return s4.transpose(1, 0, 2, 3).reshape(layers, N_DEV * r, cc)
    return s4.transpose(1, 2, 0, 3).reshape(layers, r, N_DEV * cc)


def _disassemble(full, axis):
    layers, R, C = full.shape
    if axis == 0:
        r = R // N_DEV
        return full.reshape(layers, N_DEV, r, C).transpose(1, 0, 2, 3).reshape(N_DEV, layers * r, C)
    cc = C // N_DEV
    return full.reshape(layers, R, N_DEV, cc).transpose(2, 0, 1, 3).reshape(N_DEV, layers * R, cc)


def kernel(x, c, fox_w_in, fox_b_f, fox_q_gain, fox_k_gain, fox_w_out, sgu_w_in, sgu_b_in, sgu_v_gain, sgu_v_bias, sgu_w_s, sgu_b_s, sgu_w_out, ffn_w_up, ffn_conv_w, ffn_conv_b, ffn_w_down, ada_w, ada_b, norm1_g, norm2_g, final_g, loss_target, m_fox_w_in, m_fox_b_f, m_fox_q_gain, m_fox_k_gain, m_fox_w_out, m_sgu_w_in, m_sgu_b_in, m_sgu_v_gain, m_sgu_v_bias, m_sgu_w_s, m_sgu_b_s, m_sgu_w_out, m_ffn_w_up, m_ffn_conv_w, m_ffn_conv_b, m_ffn_w_down, m_ada_w, m_ada_b, m_norm1_g, m_norm2_g, m_final_g, v_fox_w_in, v_fox_b_f, v_fox_q_gain, v_fox_k_gain, v_fox_w_out, v_sgu_w_in, v_sgu_b_in, v_sgu_v_gain, v_sgu_v_bias, v_sgu_w_s, v_sgu_b_s, v_sgu_w_out, v_ffn_w_up, v_ffn_conv_w, v_ffn_conv_b, v_ffn_w_down, v_ada_w, v_ada_b, v_norm1_g, v_norm2_g, v_final_g):
    args = dict(locals())
    wts = {n: args[n] for n in WEIGHTS}
    ms = {n: args["m_" + n] for n in WEIGHTS}
    vs = {n: args["v_" + n] for n in WEIGHTS}
    me = 4 * lax.axis_index("x") + 2 * lax.axis_index("y") + lax.axis_index("c")

    shard2d = lambda n: wts[n].astype(BF16).reshape(-1, wts[n].shape[-1])

    def assemble_big(names, got):
        out = {}
        for n, g in zip(names, got):
            f = _assemble(g, wts[n].shape[0], BIG_AXIS[n])
            out[n] = f[0] if n.startswith(SINGLE_LAYER) else f
        return out

    def blocks_of(names, grads):
        return [_disassemble(grads[n] if grads[n].ndim == 3 else grads[n][None], BIG_AXIS[n]) for n in names]

    send = [c] + [shard2d(n) for n in BEFORE_ATTENTION] + [wts[n].reshape(-1, wts[n].shape[-1]) for n in SMALL_SHARDED]
    got = _exchange(send, [False] * len(send), name="gather_first")
    c_all = got[0].reshape(N_DEV, -1)
    full = assemble_big(BEFORE_ATTENTION, got[1:1 + len(BEFORE_ATTENTION)])
    for n, g in zip(SMALL_SHARDED, got[1 + len(BEFORE_ATTENTION):]):
        lead = wts[n].shape[:-1]
        f = jnp.moveaxis(g.reshape((N_DEV,) + wts[n].shape), 0, -2).reshape(lead + (-1,))
        full[n] = f[0] if n.startswith(SINGLE_LAYER) else f
    for n in WEIGHTS:
        if n not in full and n not in BIG_AXIS:
            full[n] = wts[n][0] if n.startswith(SINGLE_LAYER) else wts[n]

    ada_cols = wts["ada_w"].shape[-1]
    mod_rows = []
    for i in range(2):
        b_mine = lax.dynamic_slice_in_dim(wts["ada_b"][i], me * ada_cols, ada_cols).reshape(1, ada_cols)
        m, c_act = _ada_mod(c_all, wts["ada_w"][i].astype(BF16), b_mine, name=f"ada_mod_{i}")
        mod_rows.append(m)
    got = _exchange([jnp.concatenate(mod_rows, axis=1)[:, None, :]], [True], name="exchange_mods")[0]
    d_model = x.shape[-1]
    mods = []
    for i in range(2):
        mod = got[:, 0, i * ada_cols:(i + 1) * ada_cols].reshape(1, N_DEV * ada_cols)
        mods.append([mod[:, k * d_model:(k + 1) * d_model] for k in range(6)])

    late = ([shard2d(n) for n in AFTER_ATTENTION], lambda g: assemble_big(AFTER_ATTENTION, g))
    loss, grad_x, grads, got_late = _local_step(x[0], loss_target[0], full, mods, late,
                                                lambda gr: blocks_of(AFTER_ATTENTION, gr))

    small = [n for n in WEIGHTS if n not in BIG_AXIS]
    flat = jnp.concatenate([loss.reshape(1)] + [grads[n].reshape(-1).astype(F32) for n in small])
    n_flat = flat.shape[0]
    rows = -(-n_flat // (8 * LANES)) * 8
    flat = jnp.pad(flat, (0, rows * LANES - n_flat)).reshape(rows, LANES)
    got = _exchange(blocks_of(BEFORE_ATTENTION, grads) + [flat], [True] * len(BEFORE_ATTENTION) + [False], name="exchange_last")
    flat_all = got[-1]
    total = _sum_parts(flat_all, name="sum_small_grads").reshape(-1)
    loss_out = total[0]

    off_ada = 1 + sum(math.prod(grads[n].shape) for n in small[:small.index("ada_b")])
    dmod_all = flat_all.reshape(N_DEV, -1)[:, off_ada:off_ada + 2 * N_DEV * ada_cols].reshape(N_DEV, 2, N_DEV * ada_cols)
    dmod_mine = lax.dynamic_slice_in_dim(dmod_all, me * ada_cols, ada_cols, axis=2)
    d_ada = [_mm(c_act, jnp.pad(dmod_mine[:, i], ((0, c_act.shape[0] - N_DEV), (0, 0))).astype(BF16), ta=True,
                 name=f"ada_dw_{i}") for i in range(2)]

    out_g, out_d, out_m, out_v = {}, {}, {}, {}
    summands = dict(zip(BEFORE_ATTENTION, got))
    summands.update(zip(AFTER_ATTENTION, got_late))
    summands["ada_w"] = jnp.concatenate(d_ada, axis=0)[None]
    for n, p in summands.items():
        shp = wts[n].shape
        two_d = lambda a: a.reshape(-1, shp[-1])
        g, d, mn, vn = _adamw(two_d(wts[n]), p, two_d(ms[n]), two_d(vs[n]), name=f"adamw_{n}")
        out_g[n], out_d[n], out_m[n], out_v[n] = (a.reshape(shp) for a in (g, d, mn, vn))
    off = 1
    small_g = {}
    for n in small:
        full_shape = grads[n].shape
        size = math.prod(full_shape)
        g = total[off:off + size].reshape(full_shape)
        off += size
        if n in SMALL_SHARDED:
            blk = full_shape[-1] // N_DEV
            g = lax.dynamic_slice_in_dim(g, me * blk, blk, axis=g.ndim - 1)
        small_g[n] = g.reshape(wts[n].shape)
    cat = lambda d: jnp.concatenate([d[n].reshape(-1) for n in small])
    n_small = sum(math.prod(wts[n].shape) for n in small)
    rows2 = -(-n_small // (256 * LANES)) * 256
    pack = lambda d, fill: jnp.pad(cat(d), (0, rows2 * LANES - n_small), constant_values=fill).reshape(rows2, LANES)
    g, d, mn, vn = _adamw(pack(wts, 0.0), pack(small_g, 0.0)[None], pack(ms, 0.0), pack(vs, 1.0), name="adamw_small")
    off = 0
    for n in small:
        size = math.prod(wts[n].shape)
        for src, dst in ((g, out_g), (d, out_d), (mn, out_m), (vn, out_v)):
            dst[n] = src.reshape(-1)[off:off + size].reshape(wts[n].shape)
        off += size

    return (loss_out, grad_x[None], *[out_g[n] for n in WEIGHTS], *[out_d[n] for n in WEIGHTS],
            *[out_m[n] for n in WEIGHTS], *[out_v[n] for n in WEIGHTS])
```

```python
import functools
import math

import jax
import jax.numpy as jnp
from jax import lax
from jax.experimental import pallas as pl
from jax.experimental.pallas import tpu as pltpu

F32, BF16 = jnp.float32, jnp.bfloat16
N_DEV = 8
HEADS, HEAD_DIM = 16, 64
HEAD_PAIRS = HEADS // 2
LANES = 128
EPS = 1e-6
SGU_BLOCK, SGU_GROUPS, SGU_CHUNK = 128, 8, 64
CONV_WIDTH = 3
ADAM_LR, ADAM_B1, ADAM_B2, ADAM_EPS, ADAM_WD, ADAM_STEP = 0.001, 0.9, 0.999, 1e-08, 0.01, 10
NEG = -1e30
GELU_C0, GELU_C1 = math.sqrt(2.0 / math.pi), 0.044715
MESH = pl.DeviceIdType.MESH
VMEM_LIMIT = 56 * 1024 * 1024


def _tile(dim, pref):
    if dim <= pref:
        return dim
    t = (pref // LANES) * LANES
    while t >= LANES:
        if dim % t == 0:
            return t
        t -= LANES
    return dim


def _params(*sem):
    return pltpu.CompilerParams(dimension_semantics=sem, vmem_limit_bytes=VMEM_LIMIT)


def _mm(a, b, *, name, ta=False, tb=False, out_dtype=F32, tm=1024, tn=1024, tk=1024, res=None):
    M = a.shape[1] if ta else a.shape[0]
    K = a.shape[0] if ta else a.shape[1]
    N = b.shape[0] if tb else b.shape[1]
    tm, tn, tk = _tile(M, tm), _tile(N, tn), _tile(K, tk)
    nk = K // tk
    dims = (((0 if ta else 1,), (1 if tb else 0,)), ((), ()))
    a_spec = pl.BlockSpec((tk, tm), lambda i, j, k: (k, i)) if ta else pl.BlockSpec((tm, tk), lambda i, j, k: (i, k))
    b_spec = pl.BlockSpec((tn, tk), lambda i, j, k: (j, k)) if tb else pl.BlockSpec((tk, tn), lambda i, j, k: (k, j))
    o_spec = pl.BlockSpec((tm, tn), lambda i, j, k: (i, j))

    def accumulate(a_ref, b_ref, acc):
        @pl.when(pl.program_id(2) == 0)
        def _():
            acc[...] = jnp.zeros_like(acc)
        acc[...] += lax.dot_general(a_ref[...], b_ref[...], dims, preferred_element_type=F32)

    if res is None:
        def body(a_ref, b_ref, o_ref, acc):
            accumulate(a_ref, b_ref, acc)

            @pl.when(pl.program_id(2) == nk - 1)
            def _():
                o_ref[...] = acc[...].astype(o_ref.dtype)

        return pl.pallas_call(
            body, grid=(M // tm, N // tn, nk), in_specs=[a_spec, b_spec], out_specs=o_spec,
            out_shape=jax.ShapeDtypeStruct((M, N), out_dtype), scratch_shapes=[pltpu.VMEM((tm, tn), F32)],
            compiler_params=_params("parallel", "parallel", "arbitrary"), name=name)(a, b)

    x, gate = res

    def body_res(a_ref, b_ref, x_ref, g_ref, o_ref, y_ref, acc):
        accumulate(a_ref, b_ref, acc)

        @pl.when(pl.program_id(2) == nk - 1)
        def _():
            y = acc[...]
            o_ref[...] = x_ref[...] + g_ref[...] * y
            y_ref[...] = y.astype(BF16)

    return pl.pallas_call(
        body_res, grid=(M // tm, N // tn, nk),
        in_specs=[a_spec, b_spec, o_spec, pl.BlockSpec((1, tn), lambda i, j, k: (0, j))],
        out_specs=[o_spec, o_spec],
        out_shape=[jax.ShapeDtypeStruct((M, N), F32), jax.ShapeDtypeStruct((M, N), BF16)],
        scratch_shapes=[pltpu.VMEM((tm, tn), F32)],
        compiler_params=_params("parallel", "parallel", "arbitrary"), name=name)(a, b, x, gate)


def _ada_mod(c_rows, w, b, *, name):
    R, D = c_rows.shape
    N = w.shape[1]
    tn = _tile(N, 1536)
    rows = 16
    c_pad = jnp.pad(c_rows, ((0, rows - R), (0, 0)))

    def body(c_ref, w_ref, b_ref, o_ref, ca_ref):
        cv = c_ref[...]
        ca16 = (cv * jax.nn.sigmoid(cv)).astype(BF16)
        ca_ref[...] = ca16
        o_ref[...] = jnp.dot(ca16, w_ref[...], preferred_element_type=F32) + b_ref[...]

    out, ca = pl.pallas_call(
        body, grid=(N // tn,),
        in_specs=[pl.BlockSpec((rows, D), lambda j: (0, 0)), pl.BlockSpec((D, tn), lambda j: (0, j)),
                  pl.BlockSpec((1, tn), lambda j: (0, j))],
        out_specs=[pl.BlockSpec((rows, tn), lambda j: (0, j)), pl.BlockSpec((rows, D), lambda j: (0, 0))],
        out_shape=[jax.ShapeDtypeStruct((rows, N), F32), jax.ShapeDtypeStruct((rows, D), BF16)],
        compiler_params=_params("arbitrary"), name=name)(c_pad, w, b)
    return out[0:R], ca


def _norm_mod_fwd(x, g, shift, scale, *, name, ts=512):
    S, D = x.shape
    ts = _tile(S, ts)
    row = pl.BlockSpec((ts, D), lambda i: (i, 0))
    vec = pl.BlockSpec((1, D), lambda i: (0, 0))

    def body(x_ref, g_ref, sh_ref, sc_ref, h_ref):
        xv = x_ref[...]
        r = lax.rsqrt(jnp.mean(xv * xv, axis=-1, keepdims=True) + EPS)
        h_ref[...] = ((xv * r * g_ref[...]) * (1.0 + sc_ref[...]) + sh_ref[...]).astype(BF16)

    return pl.pallas_call(body, grid=(S // ts,), in_specs=[row, vec, vec, vec], out_specs=row,
                          out_shape=jax.ShapeDtypeStruct((S, D), BF16),
                          compiler_params=_params("parallel"), name=name)(x, g, shift, scale)


def _acc_init(step, *refs):
    @pl.when(step == 0)
    def _():
        for r in refs:
            r[...] = jnp.zeros_like(r)


def _colsum(v):
    return jnp.sum(v, axis=0, keepdims=True)


def _norm_mod_bwd(dh, x, g, scale, dres, prev=None, *, name, ts=512):
    S, D = x.shape
    ts = _tile(S, ts)
    row = pl.BlockSpec((ts, D), lambda i: (i, 0))
    vec = pl.BlockSpec((1, D), lambda i: (0, 0))
    has_prev = prev is not None

    def body(*refs):
        if has_prev:
            dh_ref, x_ref, g_ref, sc_ref, dres_ref, y_ref, gate_ref, dx_ref, dg_ref, dsh_ref, dsc_ref, dy_ref, dgate_ref = refs
            _acc_init(pl.program_id(0), dg_ref, dsh_ref, dsc_ref, dgate_ref)
        else:
            dh_ref, x_ref, g_ref, sc_ref, dres_ref, dx_ref, dg_ref, dsh_ref, dsc_ref = refs
            _acc_init(pl.program_id(0), dg_ref, dsh_ref, dsc_ref)
        xv, dhv, gv = x_ref[...], dh_ref[...], g_ref[...]
        r = lax.rsqrt(jnp.mean(xv * xv, axis=-1, keepdims=True) + EPS)
        xh = xv * r
        dsh_ref[...] += _colsum(dhv)
        dsc_ref[...] += _colsum(dhv * (xh * gv))
        dn = dhv * (1.0 + sc_ref[...])
        dg_ref[...] += _colsum(dn * xh)
        dxh = dn * gv
        dx = dres_ref[...] + r * (dxh - xh * jnp.mean(dxh * xh, axis=-1, keepdims=True))
        dx_ref[...] = dx
        if has_prev:
            dy_ref[...] = (gate_ref[...] * dx).astype(BF16)
            dgate_ref[...] += _colsum(dx * y_ref[...].astype(F32))

    ins, in_specs = [dh, x, g, scale, dres], [row, row, vec, vec, row]
    outs = [jax.ShapeDtypeStruct((S, D), F32)] + [jax.ShapeDtypeStruct((1, D), F32)] * 3
    out_specs = [row, vec, vec, vec]
    if has_prev:
        ins += list(prev)
        in_specs += [row, vec]
        outs += [jax.ShapeDtypeStruct((S, D), BF16), jax.ShapeDtypeStruct((1, D), F32)]
        out_specs += [row, vec]
    return pl.pallas_call(body, grid=(S // ts,), in_specs=in_specs, out_specs=out_specs, out_shape=outs,
                          compiler_params=_params("arbitrary"), name=name)(*ins)


def _final_loss(x, g, target, y, gate, *, name, ts=512):
    S, D = x.shape
    ts = _tile(S, ts)
    row = pl.BlockSpec((ts, D), lambda i: (i, 0))
    vec = pl.BlockSpec((1, D), lambda i: (0, 0))
    lvec = pl.BlockSpec((1, LANES), lambda i: (0, 0))

    def body(x_ref, g_ref, t_ref, y_ref, gate_ref, loss_ref, dg_ref, dx_ref, dy_ref, dgate_ref):
        _acc_init(pl.program_id(0), loss_ref, dg_ref, dgate_ref)
        xv, gv = x_ref[...], g_ref[...]
        r = lax.rsqrt(jnp.mean(xv * xv, axis=-1, keepdims=True) + EPS)
        xh = xv * r
        e = xh * gv - t_ref[...]
        loss_ref[...] += 0.5 * jnp.sum(jnp.mean(e * e, axis=-1, keepdims=True), axis=0, keepdims=True)
        dout = e * (1.0 / D)
        dg_ref[...] += _colsum(dout * xh)
        dxh = dout * gv
        dx = r * (dxh - xh * jnp.mean(dxh * xh, axis=-1, keepdims=True))
        dx_ref[...] = dx
        dy_ref[...] = (gate_ref[...] * dx).astype(BF16)
        dgate_ref[...] += _colsum(dx * y_ref[...].astype(F32))

    return pl.pallas_call(
        body, grid=(S // ts,), in_specs=[row, vec, row, row, vec], out_specs=[lvec, vec, row, row, vec],
        out_shape=[jax.ShapeDtypeStruct((1, LANES), F32), jax.ShapeDtypeStruct((1, D), F32),
                   jax.ShapeDtypeStruct((S, D), F32), jax.ShapeDtypeStruct((S, D), BF16),
                   jax.ShapeDtypeStruct((1, D), F32)],
        compiler_params=_params("arbitrary"), name=name)(x, g, target, y, gate)


def _head_mean(v, gmat):
    hi = v.astype(BF16)
    lo = (v - hi.astype(F32)).astype(BF16)
    return jnp.dot(hi, gmat, preferred_element_type=F32) + jnp.dot(lo, gmat, preferred_element_type=F32)


L_F, L_ONE, L_SHIFT = HEAD_DIM, HEAD_DIM + 3, HEAD_DIM + 6
SHIFT_FREE_LOGIT_BOUND = 60.0


def _lane():
    return lax.broadcasted_iota(jnp.int32, (1, LANES), 1)


def _split3(v):
    p1 = v.astype(BF16).astype(F32)
    r1 = v - p1
    p2 = r1.astype(BF16).astype(F32)
    p3 = (r1 - p2).astype(BF16).astype(F32)
    return p1, p2, p3


def _put3(lane, first, pieces):
    out = jnp.where(lane == first, pieces[0], 0.0)
    for k in (1, 2):
        out = out + jnp.where(lane == first + k, pieces[k], 0.0)
    return out


def _ones3(lane, first):
    return jnp.where((lane >= first) & (lane < first + 3), 1.0, 0.0)


def _lane_col(v, lane, idx):
    return jnp.sum(jnp.where(lane == idx, v, 0.0), axis=-1, keepdims=True)


def _head_of_pair(pair, e, lane):
    return jnp.where(lane < HEAD_DIM, pair if e == 0 else pltpu.roll(pair, HEAD_DIM, 1), 0.0)


def _pair_of_heads(even, odd, lane):
    return jnp.where(lane < HEAD_DIM, even, pltpu.roll(odd, HEAD_DIM, 1))


def _fox_prep_fwd(proj, fcum, qgain, kgain, gmat, *, name, ts=256):
    S = proj.shape[0]
    D = HEADS * HEAD_DIM
    ts = _tile(S, ts)
    scale = HEAD_DIM ** -0.5

    def body(p_ref, f_ref, qg_ref, kg_ref, gm_ref, q_ref, k_ref, v_ref):
        gm, lane, fc = gm_ref[...], _lane(), f_ref[...]
        for cpair in range(HEAD_PAIRS):
            qv = p_ref[:, pl.ds(cpair * LANES, LANES)].astype(F32)
            kv = p_ref[:, pl.ds(D + cpair * LANES, LANES)].astype(F32)
            vv = p_ref[:, pl.ds(2 * D + cpair * LANES, LANES)].astype(F32)
            qn = (qv * lax.rsqrt(_head_mean(qv * qv, gm) + EPS) * qg_ref[...]) * scale
            kn = kv * lax.rsqrt(_head_mean(kv * kv, gm) + EPS) * kg_ref[...]
            for e in range(2):
                h = 2 * cpair + e
                cols = pl.ds(h * LANES, LANES)
                f3 = _split3(_lane_col(fc, lane, h))
                q_ref[:, cols] = (_head_of_pair(qn, e, lane) + _put3(lane, L_F, f3) + _ones3(lane, L_ONE)).astype(BF16)
                k_ref[:, cols] = (_head_of_pair(kn, e, lane) + _ones3(lane, L_F)
                                  - _put3(lane, L_ONE, f3) + _ones3(lane, L_SHIFT)).astype(BF16)
                v_ref[:, cols] = (_head_of_pair(vv, e, lane) + _ones3(lane, L_F)).astype(BF16)

    vec = pl.BlockSpec((1, LANES), lambda i: (0, 0))
    wide = pl.BlockSpec((ts, HEADS * LANES), lambda i: (i, 0))
    return pl.pallas_call(
        body, grid=(S // ts,),
        in_specs=[pl.BlockSpec((ts, 3 * D), lambda i: (i, 0)), pl.BlockSpec((ts, LANES), lambda i: (i, 0)), vec, vec,
                  pl.BlockSpec((LANES, LANES), lambda i: (0, 0))],
        out_specs=[wide, wide, wide], out_shape=[jax.ShapeDtypeStruct((S, HEADS * LANES), BF16)] * 3,
        compiler_params=_params("parallel"), name=name)(proj, fcum, qgain, kgain, gmat)


def _fox_prep_bwd(proj, dq_aug, dk_aug, dv_aug, dog, qgain, kgain, gmat, *, name, ts=256):
    S = proj.shape[0]
    D = HEADS * HEAD_DIM
    ts = _tile(S, ts)
    scale = HEAD_DIM ** -0.5

    def body(p_ref, dq_ref, dk_ref, dv_ref, dog_ref, qg_ref, kg_ref, gm_ref, o_ref, df_ref, dqg_ref, dkg_ref):
        _acc_init(pl.program_id(0), dqg_ref, dkg_ref)
        gm, lane = gm_ref[...], _lane()
        df = jnp.zeros((ts, LANES), F32)
        for cpair in range(HEAD_PAIRS):
            tiles = []
            for e in range(2):
                h = 2 * cpair + e
                cols = pl.ds(h * LANES, LANES)
                tq, tk = dq_ref[:, cols], dk_ref[:, cols]
                df = jnp.where(lane == h, _lane_col(tq, lane, L_F) - _lane_col(tk, lane, L_ONE), df)
                tiles.append((tq, tk, dv_ref[:, cols].astype(F32)))
            pair = [_pair_of_heads(tiles[0][k], tiles[1][k], lane) for k in range(3)]
            for half, g_ref, dg_ref, mult in ((0, qg_ref, dqg_ref, scale), (1, kg_ref, dkg_ref, 1.0)):
                v = p_ref[:, pl.ds(half * D + cpair * LANES, LANES)].astype(F32)
                r = lax.rsqrt(_head_mean(v * v, gm) + EPS)
                xh = v * r
                dn = pair[half] * mult
                dg_ref[...] += _colsum(dn * xh)
                dxh = dn * g_ref[...]
                o_ref[:, pl.ds(half * D + cpair * LANES, LANES)] = (r * (dxh - xh * _head_mean(dxh * xh, gm))).astype(BF16)
            o_ref[:, pl.ds(2 * D + cpair * LANES, LANES)] = pair[2].astype(BF16)
        o_ref[:, pl.ds(3 * D, D)] = dog_ref[...]
        o_ref[:, pl.ds(4 * D, LANES)] = jnp.zeros((ts, LANES), BF16)
        df_ref[...] = df

    row = pl.BlockSpec((ts, D), lambda i: (i, 0))
    wide = pl.BlockSpec((ts, HEADS * LANES), lambda i: (i, 0))
    vec = pl.BlockSpec((1, LANES), lambda i: (0, 0))
    return pl.pallas_call(
        body, grid=(S // ts,),
        in_specs=[pl.BlockSpec((ts, 2 * D), lambda i: (i, 0)), wide, wide, wide, row, vec, vec,
                  pl.BlockSpec((LANES, LANES), lambda i: (0, 0))],
        out_specs=[pl.BlockSpec((ts, 4 * D + LANES), lambda i: (i, 0)), pl.BlockSpec((ts, LANES), lambda i: (i, 0)), vec, vec],
        out_shape=[jax.ShapeDtypeStruct((S, 4 * D + LANES), BF16), jax.ShapeDtypeStruct((S, LANES), F32),
                   jax.ShapeDtypeStruct((1, LANES), F32), jax.ShapeDtypeStruct((1, LANES), F32)],
        compiler_params=_params("arbitrary"), name=name)(proj, dq_aug, dk_aug, dv_aug, dog, qgain, kgain, gmat)


def _log_sigmoid(z):
    return jnp.minimum(z, 0.0) - jnp.log(1.0 + jnp.exp(-jnp.abs(z)))


def _fox_decay_fwd(fl, bf, *, name, tb=256):
    S = fl.shape[0]
    tb = _tile(S, tb)

    def body(fl_ref, b_ref, o_ref, carry):
        @pl.when(pl.program_id(0) == 0)
        def _():
            carry[...] = jnp.zeros_like(carry)
        logf = _log_sigmoid(fl_ref[...] + b_ref[...])
        tri = (lax.broadcasted_iota(jnp.int32, (tb, tb), 1) <= lax.broadcasted_iota(jnp.int32, (tb, tb), 0)).astype(F32)
        cs = jnp.dot(tri, logf, preferred_element_type=F32, precision=lax.Precision.HIGHEST) + carry[...]
        o_ref[...] = cs
        carry[...] = _row_of(cs, tb - 1)

    return pl.pallas_call(
        body, grid=(S // tb,),
        in_specs=[pl.BlockSpec((tb, LANES), lambda i: (i, 0)), pl.BlockSpec((1, LANES), lambda i: (0, 0))],
        out_specs=pl.BlockSpec((tb, LANES), lambda i: (i, 0)),
        out_shape=jax.ShapeDtypeStruct((S, LANES), F32), scratch_shapes=[pltpu.VMEM((1, LANES), F32)],
        compiler_params=_params("arbitrary"), name=name)(fl, bf)


def _fox_decay_bwd(dF, fl, bf, dproj, *, name, tb=256):
    S = fl.shape[0]
    tb = _tile(S, tb)
    n = S // tb
    last_col = dproj.shape[1] // LANES - 1

    def body(df_ref, fl_ref, b_ref, dproj_hbm, o_ref, db_ref, carry):
        del dproj_hbm
        @pl.when(pl.program_id(0) == 0)
        def _():
            carry[...] = jnp.zeros_like(carry)
            db_ref[...] = jnp.zeros_like(db_ref)
        tri = (lax.broadcasted_iota(jnp.int32, (tb, tb), 1) >= lax.broadcasted_iota(jnp.int32, (tb, tb), 0)).astype(F32)
        rc = jnp.dot(tri, df_ref[...], preferred_element_type=F32, precision=lax.Precision.HIGHEST) + carry[...]
        carry[...] = _row_of(rc, 0)
        dfl = rc * jax.nn.sigmoid(-(fl_ref[...] + b_ref[...]))
        o_ref[...] = dfl.astype(BF16)
        db_ref[...] += _colsum(dfl)

    rev = pl.BlockSpec((tb, LANES), lambda i: (n - 1 - i, 0))
    vec = pl.BlockSpec((1, LANES), lambda i: (0, 0))
    return pl.pallas_call(
        body, grid=(n,), in_specs=[rev, rev, vec, pl.BlockSpec(memory_space=pl.ANY)],
        out_specs=[pl.BlockSpec((tb, LANES), lambda i: (n - 1 - i, last_col)), vec],
        out_shape=[jax.ShapeDtypeStruct(dproj.shape, BF16), jax.ShapeDtypeStruct((1, LANES), F32)],
        scratch_shapes=[pltpu.VMEM((1, LANES), F32)], input_output_aliases={3: 0},
        compiler_params=_params("arbitrary"), name=name)(dF, fl, bf, dproj)


_NT = (((1,), (1,)), ((), ()))
_TN = (((0,), (0,)), ((), ()))


def _causal(T, transposed=False):
    r, c = lax.broadcasted_iota(jnp.int32, (T, T), 0), lax.broadcasted_iota(jnp.int32, (T, T), 1)
    return r <= c if transposed else c <= r


def _with_shift(q_tile, shift, lane):
    keep = jnp.where((lane >= L_SHIFT) & (lane < L_SHIFT + 3), 0.0, q_tile)
    return (keep + _put3(lane, L_SHIFT, _split3(-shift))).astype(BF16)


def _ride_along(xchg, n_in, n_out, grid):
    if xchg is None:
        return (lambda body: body), [], [], [], [], []
    arrs, scatter = xchg
    n = len(arrs)

    def wrap(body):
        def wrapped(*refs):
            own_in, x_in = refs[:n_in], refs[n_in:n_in + n]
            own_out, x_out = refs[n_in + n:n_in + n + n_out], refs[n_in + n + n_out:n_in + 2 * n + n_out]
            rest = refs[n_in + 2 * n + n_out:]
            own_scratch, sems = rest[:len(rest) - 3], rest[len(rest) - 3:]
            ids = [pl.program_id(d) for d in range(len(grid))]
            first = functools.reduce(jnp.logical_and, [i == 0 for i in ids])
            last = functools.reduce(jnp.logical_and, [i == g - 1 for i, g in zip(ids, grid)])

            @pl.when(first)
            def _():
                for cp in _xchg_copies(x_in, x_out, scatter, *sems):
                    cp.start()

            body(*own_in, *own_out, *own_scratch)

            @pl.when(last)
            def _():
                for cp in _xchg_copies(x_in, x_out, scatter, *sems):
                    cp.wait()

        return wrapped

    return wrap, [_HBM] * n, [_HBM] * n, _xchg_out_shapes(arrs, scatter), _xchg_sems(n), list(arrs)


def _attn_rowmax(q_aug, k_aug, *, name, T=1024):
    S = q_aug.shape[0]
    T = _tile(S, T)
    n = S // T

    def body(q_ref, k_ref, o_ref, m_s):
        i, j = pl.program_id(1), pl.program_id(2)

        @pl.when(j == 0)
        def _():
            m_s[...] = jnp.full_like(m_s, NEG)

        def step(diag):
            s = lax.dot_general(q_ref[...], k_ref[...], _NT, preferred_element_type=F32)
            if diag:
                s = jnp.where(_causal(T), s, NEG)
            m = m_s[...]
            for cb in range(T // LANES):
                m = jnp.maximum(m, s[:, cb * LANES:(cb + 1) * LANES])
            m_s[...] = m

        @pl.when(j < i)
        def _():
            step(False)

        @pl.when(j == i)
        def _():
            step(True)
            o_ref[...] = _with_shift(q_ref[...].astype(F32), jnp.max(m_s[...], axis=-1, keepdims=True), _lane())

    qrow = pl.BlockSpec((T, LANES), lambda h, i, j: (i, h))
    return pl.pallas_call(
        body, grid=(HEADS, n, n),
        in_specs=[qrow, pl.BlockSpec((T, LANES), lambda h, i, j: (jnp.minimum(j, i), h))],
        out_specs=qrow, out_shape=jax.ShapeDtypeStruct(q_aug.shape, BF16),
        scratch_shapes=[pltpu.VMEM((T, LANES), F32)],
        compiler_params=_params("parallel", "parallel", "arbitrary"), name=name)(q_aug, k_aug)


def _attn_fwd(q_max, k_aug, v_aug, xchg=None, *, name, T=1024):
    S = q_max.shape[0]
    T = _tile(S, T)
    n = S // T
    wrap, x_in, x_out, x_shapes, x_sems, x_ops = _ride_along(xchg, 3, 2, (HEADS, n, n))

    def body(q_ref, k_ref, v_ref, o_ref, qb_ref, acc_s):
        i, j = pl.program_id(1), pl.program_id(2)

        @pl.when(j == 0)
        def _():
            acc_s[...] = jnp.zeros_like(acc_s)

        def step(diag):
            s = lax.dot_general(q_ref[...], k_ref[...], _NT, preferred_element_type=F32)
            if diag:
                s = jnp.where(_causal(T), s, NEG)
            acc_s[...] += jnp.dot(jnp.exp(s).astype(BF16), v_ref[...], preferred_element_type=F32)

        @pl.when(j < i)
        def _():
            step(False)

        @pl.when(j == i)
        def _():
            step(True)
            lane = _lane()
            acc = acc_s[...]
            l = _lane_col(acc, lane, L_F)
            o_ref[...] = acc / l
            qf = q_ref[...].astype(F32)
            row_max = -jnp.sum(jnp.where((lane >= L_SHIFT) & (lane < L_SHIFT + 3), qf, 0.0), axis=-1, keepdims=True)
            qb_ref[...] = _with_shift(qf, row_max + jnp.log(l), lane)

    qrow = pl.BlockSpec((T, LANES), lambda h, i, j: (i, h))
    kv = pl.BlockSpec((T, LANES), lambda h, i, j: (jnp.minimum(j, i), h))
    outs = pl.pallas_call(
        wrap(body), grid=(HEADS, n, n), in_specs=[qrow, kv, kv] + x_in, out_specs=[qrow, qrow] + x_out,
        out_shape=[jax.ShapeDtypeStruct(q_max.shape, F32), jax.ShapeDtypeStruct(q_max.shape, BF16)] + x_shapes,
        scratch_shapes=[pltpu.VMEM((T, LANES), F32)] + x_sems,
        compiler_params=_params("arbitrary", "arbitrary", "arbitrary"), name=name)(q_max, k_aug, v_aug, *x_ops)
    return outs[0], outs[1], outs[2:]


def _attn_bwd(q_lse, k_aug, v_aug, do_aug, xchg=None, *, name, T=1024):
    S = q_lse.shape[0]
    T = _tile(S, T)
    n = S // T
    wrap, x_in, x_out, x_shapes, x_sems, x_ops = _ride_along(xchg, 4, 3, (HEADS, n, n))

    def body(q_ref, do_ref, k_ref, v_ref, dq_ref, dk_ref, dv_ref, dq_s, dk_s, dv_s):
        j, i = pl.program_id(1), pl.program_id(2)

        def step(diag):
            q, do, k, v = q_ref[...], do_ref[...], k_ref[...], v_ref[...]
            st = lax.dot_general(k, q, _NT, preferred_element_type=F32)
            if diag:
                st = jnp.where(_causal(T, transposed=True), st, NEG)
            pt = jnp.exp(st)
            dst = (pt * lax.dot_general(v, do, _NT, preferred_element_type=F32)).astype(BF16)
            dv_s[...] += jnp.dot(pt.astype(BF16), do, preferred_element_type=F32)
            dk_s[...] += jnp.dot(dst, q, preferred_element_type=F32)
            upd = lax.dot_general(dst, k, _TN, preferred_element_type=F32)

            @pl.when(j == 0)
            def _():
                dq_s[i] = upd

            @pl.when(j > 0)
            def _():
                dq_s[i] += upd

        @pl.when(i == j)
        def _():
            dk_s[...] = jnp.zeros_like(dk_s)
            dv_s[...] = jnp.zeros_like(dv_s)
            step(True)
            dq_ref[...] = dq_s[j]

        @pl.when(i > j)
        def _():
            step(False)

        @pl.when(i == n - 1)
        def _():
            dk_ref[...] = dk_s[...]
            dv_ref[...] = dv_s[...].astype(BF16)

    qrow = pl.BlockSpec((T, LANES), lambda h, j, i: (jnp.maximum(i, j), h))
    kv = pl.BlockSpec((T, LANES), lambda h, j, i: (j, h))
    outs = pl.pallas_call(
        wrap(body), grid=(HEADS, n, n), in_specs=[qrow, qrow, kv, kv] + x_in, out_specs=[kv, kv, kv] + x_out,
        out_shape=[jax.ShapeDtypeStruct(q_lse.shape, F32), jax.ShapeDtypeStruct(q_lse.shape, F32),
                   jax.ShapeDtypeStruct(q_lse.shape, BF16)] + x_shapes,
        scratch_shapes=[pltpu.VMEM((n, T, LANES), F32), pltpu.VMEM((T, LANES), F32), pltpu.VMEM((T, LANES), F32)] + x_sems,
        compiler_params=_params("arbitrary", "arbitrary", "arbitrary"), name=name)(q_lse, do_aug, k_aug, v_aug, *x_ops)
    return outs[0], outs[1], outs[2], outs[3:]


def _fox_gate_fwd(att_aug, proj, *, name, ts=256):
    S = att_aug.shape[0]
    D = HEADS * HEAD_DIM
    ts = _tile(S, ts)

    def body(a_ref, o_ref, att_ref, out_ref):
        lane = _lane()
        for cpair in range(HEAD_PAIRS):
            cols = pl.ds(cpair * LANES, LANES)
            pair = _pair_of_heads(a_ref[:, pl.ds(2 * cpair * LANES, LANES)], a_ref[:, pl.ds((2 * cpair + 1) * LANES, LANES)], lane)
            att_ref[:, cols] = pair
            out_ref[:, cols] = (pair * jax.nn.sigmoid(o_ref[:, cols].astype(F32))).astype(BF16)

    row = pl.BlockSpec((ts, D), lambda i: (i, 0))
    return pl.pallas_call(
        body, grid=(S // ts,),
        in_specs=[pl.BlockSpec((ts, HEADS * LANES), lambda i: (i, 0)), pl.BlockSpec((ts, D), lambda i: (i, 3))],
        out_specs=[row, row], out_shape=[jax.ShapeDtypeStruct((S, D), F32), jax.ShapeDtypeStruct((S, D), BF16)],
        compiler_params=_params("parallel"), name=name)(att_aug, proj)


def _fox_gate_bwd(da, att, proj, *, name, ts=256):
    S, D = att.shape
    ts = _tile(S, ts)

    def body(da_ref, a_ref, o_ref, do_ref, dog_ref):
        lane = _lane()
        for cpair in range(HEAD_PAIRS):
            cols = pl.ds(cpair * LANES, LANES)
            dav, av = da_ref[:, cols].astype(F32), a_ref[:, cols]
            sg = jax.nn.sigmoid(o_ref[:, cols].astype(F32))
            datt = (dav * sg).astype(BF16).astype(F32)
            dog_ref[:, cols] = (dav * av * sg * (1.0 - sg)).astype(BF16)
            prod = datt * av
            for e in range(2):
                in_head = (lane < HEAD_DIM) if e == 0 else (lane >= HEAD_DIM)
                delta = jnp.sum(jnp.where(in_head, prod, 0.0), axis=-1, keepdims=True)
                tile = _head_of_pair(datt, e, lane) + _put3(lane, L_F, _split3(-delta))
                do_ref[:, pl.ds((2 * cpair + e) * LANES, LANES)] = tile.astype(BF16)

    row = pl.BlockSpec((ts, D), lambda i: (i, 0))
    return pl.pallas_call(
        body, grid=(S // ts,), in_specs=[row, row, pl.BlockSpec((ts, D), lambda i: (i, 3))],
        out_specs=[pl.BlockSpec((ts, HEADS * LANES), lambda i: (i, 0)), row],
        out_shape=[jax.ShapeDtypeStruct((S, HEADS * LANES), BF16), jax.ShapeDtypeStruct((S, D), BF16)],
        compiler_params=_params("parallel"), name=name)(da, att, proj)


def _row_of(block, r):
    rows = lax.broadcasted_iota(jnp.int32, block.shape, 0)
    return jnp.sum(jnp.where(rows == r, block, 0.0), axis=0, keepdims=True)


def _shift_down(cur, tail, k):
    out = pltpu.roll(cur, k, 0)
    rows = lax.broadcasted_iota(jnp.int32, cur.shape, 0)
    for r in range(k):
        out = jnp.where(rows == r, _row_of(tail, tail.shape[0] - k + r), out)
    return out


def _shift_up(cur, head, k):
    n = cur.shape[0]
    out = pltpu.roll(cur, n - k, 0)
    rows = lax.broadcasted_iota(jnp.int32, cur.shape, 0)
    for r in range(k):
        out = jnp.where(rows == n - k + r, _row_of(head, r), out)
    return out


HALO = 16


CONV_TC = 256


def _pair_tiles(v):
    nc = v.shape[-1] // (2 * CONV_TC)
    return jnp.swapaxes(v.reshape(v.shape[:-1] + (2, nc, CONV_TC)), -3, -2).reshape(v.shape)


def _unpair_tiles(v):
    nc = v.shape[-1] // (2 * CONV_TC)
    return jnp.swapaxes(v.reshape(v.shape[:-1] + (nc, 2, CONV_TC)), -3, -2).reshape(v.shape)


def _conv_rows(cur, tail, w_ref, b_ref, cols):
    a1, a2 = _shift_down(cur, tail, 1), _shift_down(cur, tail, 2)
    return a2 * w_ref[0:1, cols] + a1 * w_ref[1:2, cols] + cur * w_ref[2:3, cols] + b_ref[:, cols], (a2, a1, cur)


def _conv_gate_fwd(a, cw, cb, *, name, ts=1024):
    S, F2 = a.shape
    tc = CONV_TC
    ts = _tile(S, ts)
    nc = F2 // (2 * tc)
    sub = ts // HALO
    halves = (pl.ds(0, tc), pl.ds(tc, tc))

    def body(a_ref, t_ref, w_ref, b_ref, o_ref):
        first = pl.program_id(1) == 0
        pre = []
        for cols in halves:
            tail = jnp.where(first, 0.0, t_ref[:, cols].astype(F32))
            pre.append(_conv_rows(a_ref[:, cols].astype(F32), tail, w_ref, b_ref, cols)[0])
        g, val = pre
        o_ref[...] = (g * jax.nn.sigmoid(g) * val).astype(BF16)

    return pl.pallas_call(
        body, grid=(nc, S // ts),
        in_specs=[pl.BlockSpec((ts, 2 * tc), lambda j, i: (i, j)),
                  pl.BlockSpec((HALO, 2 * tc), lambda j, i: (jnp.maximum(i * sub - 1, 0), j)),
                  pl.BlockSpec((CONV_WIDTH, 2 * tc), lambda j, i: (0, j)), pl.BlockSpec((1, 2 * tc), lambda j, i: (0, j))],
        out_specs=pl.BlockSpec((ts, tc), lambda j, i: (i, j)),
        out_shape=jax.ShapeDtypeStruct((S, F2 // 2), BF16),
        compiler_params=_params("parallel", "parallel"), name=name)(a, a, cw, cb)


def _conv_gate_bwd(a, dact, cw, cb, *, name, ts=1024):
    S, F2 = a.shape
    tc = CONV_TC
    ts = _tile(S, ts)
    nc = F2 // (2 * tc)
    sub = ts // HALO
    n_rows = S // ts
    halves = (pl.ds(0, tc), pl.ds(tc, tc))

    def body(a_ref, at_ref, ah_ref, d_ref, dh_ref, w_ref, b_ref, da_ref, s_ref):
        i = pl.program_id(1)
        _acc_init(i, s_ref)

        def dpre_of(rows, tails, d):
            (g, taps_g), (val, taps_v) = [_conv_rows(rows[h], tails[h], w_ref, b_ref, halves[h]) for h in range(2)]
            sg = jax.nn.sigmoid(g)
            return (d * val * (sg * (1.0 + g * (1.0 - sg))), d * (g * sg)), (taps_g, taps_v)

        cur = [a_ref[:, c].astype(F32) for c in halves]
        tail = [jnp.where(i == 0, 0.0, at_ref[:, c].astype(F32)) for c in halves]
        dpre, taps = dpre_of(cur, tail, d_ref[...].astype(F32))
        head, _ = dpre_of([ah_ref[:, c].astype(F32) for c in halves], [x[ts - HALO:, :] for x in cur], dh_ref[...].astype(F32))
        for h, cols in enumerate(halves):
            dd = dpre[h]
            nxt = jnp.where(i == n_rows - 1, 0.0, head[h])
            da_ref[:, cols] = (dd * w_ref[2:3, cols] + _shift_up(dd, nxt, 1) * w_ref[1:2, cols]
                               + _shift_up(dd, nxt, 2) * w_ref[0:1, cols]).astype(BF16)
            for r in range(CONV_WIDTH):
                s_ref[r:r + 1, cols] += _colsum(dd * taps[h][r])
            s_ref[CONV_WIDTH:CONV_WIDTH + 1, cols] += _colsum(dd)

    nxt_rows = lambda i: jnp.minimum((i + 1) * sub, S // HALO - 1)
    return pl.pallas_call(
        body, grid=(nc, n_rows),
        in_specs=[pl.BlockSpec((ts, 2 * tc), lambda j, i: (i, j)),
                  pl.BlockSpec((HALO, 2 * tc), lambda j, i: (jnp.maximum(i * sub - 1, 0), j)),
                  pl.BlockSpec((HALO, 2 * tc), lambda j, i: (nxt_rows(i), j)),
                  pl.BlockSpec((ts, tc), lambda j, i: (i, j)), pl.BlockSpec((HALO, tc), lambda j, i: (nxt_rows(i), j)),
                  pl.BlockSpec((CONV_WIDTH, 2 * tc), lambda j, i: (0, j)), pl.BlockSpec((1, 2 * tc), lambda j, i: (0, j))],
        out_specs=[pl.BlockSpec((ts, 2 * tc), lambda j, i: (i, j)), pl.BlockSpec((8, 2 * tc), lambda j, i: (0, j))],
        out_shape=[jax.ShapeDtypeStruct((S, F2), BF16), jax.ShapeDtypeStruct((8, F2), F32)],
        compiler_params=_params("parallel", "arbitrary"), name=name)(a, a, a, dact, dact, cw, cb)


def _gelu_parts(z):
    z2 = z * z
    t = jnp.tanh(GELU_C0 * (z + GELU_C1 * z * z2))
    val = 0.5 * z * (1.0 + t)
    grad = 0.5 * (1.0 + t) + 0.5 * z * (1.0 - t * t) * GELU_C0 * (1.0 + 3.0 * GELU_C1 * z2)
    return val, grad


def _sgu_fwd(pre, b_in, vgain, vbias, wm, bsb, *, name, ts=256):
    S, W2 = pre.shape
    W = W2 // 2
    gd = W // SGU_GROUPS
    ts = _tile(S, ts)

    def body(p_ref, b_ref, vg_ref, vb_ref, wm_ref, bs_ref, y_ref):
        u = _gelu_parts(p_ref[:, pl.ds(0, W)].astype(F32) + b_ref[:, pl.ds(0, W)])[0]
        v = _gelu_parts(p_ref[:, pl.ds(W, W)].astype(F32) + b_ref[:, pl.ds(W, W)])[0]
        mu = jnp.mean(v, axis=-1, keepdims=True)
        vc = v - mu
        rstd = lax.rsqrt(jnp.mean(vc * vc, axis=-1, keepdims=True) + EPS)
        vn = ((vc * rstd) * vg_ref[...] + vb_ref[...]).astype(BF16)
        for blk in range(ts // SGU_BLOCK):
            r0 = blk * SGU_BLOCK
            for g in range(SGU_GROUPS):
                c0 = g * gd
                mixed = jnp.dot(wm_ref[g], vn[r0:r0 + SGU_BLOCK, c0:c0 + gd], preferred_element_type=F32) + bs_ref[g]
                y_ref[pl.ds(r0, SGU_BLOCK), pl.ds(c0, gd)] = (u[r0:r0 + SGU_BLOCK, c0:c0 + gd] * mixed).astype(BF16)

    full = lambda shape: pl.BlockSpec(shape, lambda i: (0,) * len(shape))
    return pl.pallas_call(
        body, grid=(S // ts,),
        in_specs=[pl.BlockSpec((ts, W2), lambda i: (i, 0)), full((1, W2)), full((1, W)), full((1, W)),
                  full((SGU_GROUPS, SGU_BLOCK, SGU_BLOCK)), full((SGU_GROUPS, SGU_BLOCK, gd))],
        out_specs=pl.BlockSpec((ts, W), lambda i: (i, 0)), out_shape=jax.ShapeDtypeStruct((S, W), BF16),
        compiler_params=_params("parallel"), name=name)(pre, b_in, vgain, vbias, wm, bsb)


def _sgu_bwd(pre, dy, b_in, vgain, vbias, wm, wmt, bsb, *, name, ts=256):
    S, W2 = pre.shape
    W = W2 // 2
    gd = W // SGU_GROUPS
    ts = _tile(S, ts)
    last = S // ts - 1

    def body(p_ref, dy_ref, b_ref, vg_ref, vb_ref, wm_ref, wmt_ref, bs_ref,
             dp_ref, db_ref, dvg_ref, dvb_ref, dws_ref, dbs_ref, du_s, dvn_s, dbs_s):
        step = pl.program_id(0)
        _acc_init(step, db_ref, dvg_ref, dvb_ref, dws_ref, dbs_s)
        u, gu = _gelu_parts(p_ref[:, pl.ds(0, W)].astype(F32) + b_ref[:, pl.ds(0, W)])
        v, gv = _gelu_parts(p_ref[:, pl.ds(W, W)].astype(F32) + b_ref[:, pl.ds(W, W)])
        mu = jnp.mean(v, axis=-1, keepdims=True)
        vc = v - mu
        rstd = lax.rsqrt(jnp.mean(vc * vc, axis=-1, keepdims=True) + EPS)
        vhat = vc * rstd
        vn = (vhat * vg_ref[...] + vb_ref[...]).astype(BF16)
        dyv = dy_ref[...].astype(F32)
        for blk in range(ts // SGU_BLOCK):
            r0 = blk * SGU_BLOCK
            for g in range(SGU_GROUPS):
                c0 = g * gd
                vn_g = vn[r0:r0 + SGU_BLOCK, c0:c0 + gd]
                dy_g = dyv[r0:r0 + SGU_BLOCK, c0:c0 + gd]
                mixed = jnp.dot(wm_ref[g], vn_g, preferred_element_type=F32) + bs_ref[g]
                dmix = dy_g * u[r0:r0 + SGU_BLOCK, c0:c0 + gd]
                dmix_b = dmix.astype(BF16)
                du_s[pl.ds(r0, SGU_BLOCK), pl.ds(c0, gd)] = dy_g * mixed
                dvn_s[pl.ds(r0, SGU_BLOCK), pl.ds(c0, gd)] = jnp.dot(wmt_ref[g], dmix_b, preferred_element_type=F32)
                dws_ref[g] += lax.dot_general(dmix_b, vn_g, _NT, preferred_element_type=F32)
                dbs_s[g] += dmix
        dvn = dvn_s[...]
        dvg_ref[...] += _colsum(dvn * vhat)
        dvb_ref[...] += _colsum(dvn)
        dvh = dvn * vg_ref[...]
        dv = rstd * (dvh - jnp.mean(dvh, axis=-1, keepdims=True) - vhat * jnp.mean(dvh * vhat, axis=-1, keepdims=True))
        dpu = du_s[...] * gu
        dpv = dv * gv
        dp_ref[:, pl.ds(0, W)] = dpu.astype(BF16)
        dp_ref[:, pl.ds(W, W)] = dpv.astype(BF16)
        db_ref[:, pl.ds(0, W)] += _colsum(dpu)
        db_ref[:, pl.ds(W, W)] += _colsum(dpv)

        @pl.when(step == last)
        def _():
            for g in range(SGU_GROUPS):
                dbs_ref[g] = jnp.broadcast_to(jnp.sum(dbs_s[g], axis=-1, keepdims=True), (SGU_BLOCK, SGU_BLOCK))

    full = lambda shape: pl.BlockSpec(shape, lambda i: (0,) * len(shape))
    gsq = (SGU_GROUPS, SGU_BLOCK, SGU_BLOCK)
    return pl.pallas_call(
        body, grid=(S // ts,),
        in_specs=[pl.BlockSpec((ts, W2), lambda i: (i, 0)), pl.BlockSpec((ts, W), lambda i: (i, 0)),
                  full((1, W2)), full((1, W)), full((1, W)), full(gsq), full(gsq), full((SGU_GROUPS, SGU_BLOCK, gd))],
        out_specs=[pl.BlockSpec((ts, W2), lambda i: (i, 0)), full((1, W2)), full((1, W)), full((1, W)), full(gsq), full(gsq)],
        out_shape=[jax.ShapeDtypeStruct((S, W2), BF16), jax.ShapeDtypeStruct((1, W2), F32),
                   jax.ShapeDtypeStruct((1, W), F32), jax.ShapeDtypeStruct((1, W), F32),
                   jax.ShapeDtypeStruct(gsq, F32), jax.ShapeDtypeStruct(gsq, F32)],
        scratch_shapes=[pltpu.VMEM((ts, W), F32), pltpu.VMEM((ts, W), F32), pltpu.VMEM((SGU_GROUPS, SGU_BLOCK, gd), F32)],
        compiler_params=_params("arbitrary"), name=name)(pre, dy, b_in, vgain, vbias, wm, wmt, bsb)


def _ffn_fwd(x, mods, n2g, w_up, cw, cb, w_down, tag):
    sh, sc, gate = mods
    h = _norm_mod_fwd(x, n2g, sh, sc, name=f"{tag}_norm_fwd")
    a = _mm(h, w_up, out_dtype=BF16, tn=1408, name=f"{tag}_up")
    act = _conv_gate_fwd(a, cw, cb, name=f"{tag}_conv_fwd")
    x_out, y = _mm(act, w_down, tk=1408, tm=512, res=(x, gate), name=f"{tag}_down")
    return x_out, (x, h, a, act, y)


def _ffn_bwd(dy, saved, mods, n2g, w_up, cw, cb, w_down, dres, prev, tag):
    x, h, a, act, _ = saved
    sh, sc, gate = mods
    dact = _mm(dy, w_down, tb=True, out_dtype=BF16, tn=1408, name=f"{tag}_down_dx")
    dw_down = _mm(act, dy, ta=True, out_dtype=BF16, tn=1024, name=f"{tag}_down_dw")
    da, sums = _conv_gate_bwd(a, dact, cw, cb, name=f"{tag}_conv_bwd")
    dh = _mm(da, w_up, tb=True, tk=1408, name=f"{tag}_up_dx")
    dw_up = _mm(h, da, ta=True, out_dtype=BF16, tn=1408, name=f"{tag}_up_dw")
    outs = _norm_mod_bwd(dh, x, n2g, sc, dres, prev, name=f"{tag}_norm_bwd")
    sums = _unpair_tiles(sums)
    return outs, dict(w_up=_unpair_tiles(dw_up), w_down=dw_down, conv_w=sums[0:CONV_WIDTH], conv_b=sums[CONV_WIDTH])


def _local_step(x, target, w, mods, late=None, early=None):
    S, D = x.shape
    lane = jnp.arange(LANES)
    gmat = jnp.where((lane[:, None] // HEAD_DIM) == (lane[None, :] // HEAD_DIM), 1.0 / HEAD_DIM, 0.0).astype(BF16)
    qg2 = jnp.tile(w["fox_q_gain"].reshape(1, HEAD_DIM), (1, 2))
    kg2 = jnp.tile(w["fox_k_gain"].reshape(1, HEAD_DIM), (1, 2))
    bf_pad = jnp.pad(w["fox_b_f"].reshape(1, HEADS), ((0, 0), (0, LANES - HEADS)))
    w_in_pad = jnp.pad(w["fox_w_in"], ((0, 0), (0, 4 * D + LANES - w["fox_w_in"].shape[1])))
    w_qkvo, w_f = w_in_pad[:, :4 * D], w_in_pad[:, 4 * D:]
    tpos = jnp.arange(SGU_BLOCK)
    smask = (tpos[None, :] // SGU_CHUNK) <= (tpos[:, None] // SGU_CHUNK)
    wm32 = jnp.where(smask[None], w["sgu_w_s"], 0.0)
    wm, wmt = wm32.astype(BF16), jnp.swapaxes(wm32, 1, 2).astype(BF16)
    gd = w["sgu_v_gain"].shape[-1] // SGU_GROUPS
    bsb = jnp.broadcast_to(w["sgu_b_s"][:, :, None], (SGU_GROUPS, SGU_BLOCK, gd))
    vec = lambda v: v.reshape(1, -1)

    sh1, sc1, g1 = mods[0][0:3]
    h0 = _norm_mod_fwd(x, vec(w["norm1_g"][0]), sh1, sc1, name="fox_norm_fwd")
    proj = _mm(h0, w_qkvo, out_dtype=BF16, name="fox_proj")
    fl = _mm(h0, w_f, name="fox_forget_proj")
    fcum = _fox_decay_fwd(fl, bf_pad, name="fox_decay")
    q_aug, k_aug, v_aug = _fox_prep_fwd(proj, fcum, qg2, kg2, gmat, name="fox_qk_norm")
    logit_bound = 8.0 * jnp.max(jnp.abs(w["fox_q_gain"])) * jnp.max(jnp.abs(w["fox_k_gain"]))
    q_max = lax.cond(logit_bound <= SHIFT_FREE_LOGIT_BOUND, lambda: q_aug,
                     lambda: _attn_rowmax(q_aug, k_aug, name="fox_attn_rowmax"))
    xchg = None if late is None else (late[0], [False] * len(late[0]))
    att_aug, q_lse, gathered = _attn_fwd(q_max, k_aug, v_aug, xchg, name="fox_attn_fwd")
    if late is not None:
        w = {**w, **late[1](gathered)}
    w = dict(w, ffn_w_up=_pair_tiles(w["ffn_w_up"]), ffn_conv_w=_pair_tiles(w["ffn_conv_w"]),
             ffn_conv_b=_pair_tiles(w["ffn_conv_b"]))
    att, ag = _fox_gate_fwd(att_aug, proj, name="fox_gate_fwd")
    x1, y_fox = _mm(ag, w["fox_w_out"], tm=512, res=(x, g1), name="fox_out")
    x2, ffn0 = _ffn_fwd(x1, mods[0][3:6], vec(w["norm2_g"][0]), w["ffn_w_up"][0], w["ffn_conv_w"][0],
                        vec(w["ffn_conv_b"][0]), w["ffn_w_down"][0], "ffn0")

    sh1b, sc1b, g1b = mods[1][0:3]
    h1 = _norm_mod_fwd(x2, vec(w["norm1_g"][1]), sh1b, sc1b, name="sgu_norm_fwd")
    pre = _mm(h1, w["sgu_w_in"], out_dtype=BF16, name="sgu_in")
    b_in, vg, vb = vec(w["sgu_b_in"]), vec(w["sgu_v_gain"]), vec(w["sgu_v_bias"])
    ys = _sgu_fwd(pre, b_in, vg, vb, wm, bsb, name="sgu_core_fwd")
    x3, y_sgu = _mm(ys, w["sgu_w_out"], tm=512, res=(x2, g1b), name="sgu_out")
    x4, ffn1 = _ffn_fwd(x3, mods[1][3:6], vec(w["norm2_g"][1]), w["ffn_w_up"][1], w["ffn_conv_w"][1],
                        vec(w["ffn_conv_b"][1]), w["ffn_w_down"][1], "ffn1")

    loss, d_final_g, dx4, dy_ffn1, dgate_ffn1 = _final_loss(x4, vec(w["final_g"]), target, ffn1[4], mods[1][5], name="final_loss")

    (dx3, dn2g_1, dsh2_1, dsc2_1, dy_sgu, dgate_sgu), g_ffn1 = _ffn_bwd(
        dy_ffn1, ffn1, mods[1][3:6], vec(w["norm2_g"][1]), w["ffn_w_up"][1], w["ffn_conv_w"][1], vec(w["ffn_conv_b"][1]),
        w["ffn_w_down"][1], dx4, (y_sgu, g1b), "ffn1")

    dys = _mm(dy_sgu, w["sgu_w_out"], tb=True, out_dtype=BF16, name="sgu_out_dx")
    dw_sgu_out = _mm(ys, dy_sgu, ta=True, out_dtype=BF16, name="sgu_out_dw")
    dpre, db_in, dvg, dvb, dws, dbs = _sgu_bwd(pre, dys, b_in, vg, vb, wm, wmt, bsb, name="sgu_core_bwd")
    dh1 = _mm(dpre, w["sgu_w_in"], tb=True, name="sgu_in_dx")
    dw_sgu_in = _mm(h1, dpre, ta=True, out_dtype=BF16, name="sgu_in_dw")
    dx2, dn1g_1, dsh1_1, dsc1_1, dy_ffn0, dgate_ffn0 = _norm_mod_bwd(
        dh1, x2, vec(w["norm1_g"][1]), sc1b, dx3, (ffn0[4], mods[0][5]), name="sgu_norm_bwd")

    (dx1, dn2g_0, dsh2_0, dsc2_0, dy_fox, dgate_fox), g_ffn0 = _ffn_bwd(
        dy_ffn0, ffn0, mods[0][3:6], vec(w["norm2_g"][0]), w["ffn_w_up"][0], w["ffn_conv_w"][0], vec(w["ffn_conv_b"][0]),
        w["ffn_w_down"][0], dx2, (y_fox, g1), "ffn0")

    dag = _mm(dy_fox, w["fox_w_out"], tb=True, out_dtype=BF16, name="fox_out_dx")
    dw_fox_out = _mm(ag, dy_fox, ta=True, out_dtype=BF16, name="fox_out_dw")
    do_aug, dog = _fox_gate_bwd(dag, att, proj, name="fox_gate_bwd")
    grads = dict(
        sgu_w_in=dw_sgu_in, sgu_b_in=db_in[0], sgu_v_gain=dvg[0], sgu_v_bias=dvb[0],
        sgu_w_s=jnp.where(smask[None], dws, 0.0), sgu_b_s=dbs[:, :, 0], sgu_w_out=dw_sgu_out,
        ffn_w_up=jnp.stack([g_ffn0["w_up"], g_ffn1["w_up"]]),
        ffn_conv_w=jnp.stack([g_ffn0["conv_w"], g_ffn1["conv_w"]]),
        ffn_conv_b=jnp.stack([g_ffn0["conv_b"], g_ffn1["conv_b"]]),
        ffn_w_down=jnp.stack([g_ffn0["w_down"], g_ffn1["w_down"]]),
        final_g=d_final_g[0],
    )
    xchg = None
    if early is not None:
        blocks = early(grads)
        xchg = (blocks, [True] * len(blocks))
    dq_aug, dk_aug, dv_aug, exchanged = _attn_bwd(q_lse, k_aug, v_aug, do_aug, xchg, name="fox_attn_bwd")
    dproj, dF, dqg, dkg = _fox_prep_bwd(proj, dq_aug, dk_aug, dv_aug, dog, qg2, kg2, gmat, name="fox_qk_norm_bwd")
    dproj, dbf = _fox_decay_bwd(dF, fl, bf_pad, dproj, name="fox_decay_bwd")
    dh0 = _mm(dproj, w_in_pad, tb=True, tk=1408, name="fox_proj_dx")
    dw_fox_in = _mm(h0, dproj, ta=True, out_dtype=BF16, tn=1408, name="fox_proj_dw")
    dx0, dn1g_0, dsh1_0, dsc1_0 = _norm_mod_bwd(dh0, x, vec(w["norm1_g"][0]), sc1, dx1, None, name="fox_norm_bwd")

    dmod0 = jnp.concatenate([dsh1_0, dsc1_0, dgate_fox, dsh2_0, dsc2_0, dgate_ffn0], axis=1)
    dmod1 = jnp.concatenate([dsh1_1, dsc1_1, dgate_sgu, dsh2_1, dsc2_1, dgate_ffn1], axis=1)
    grads.update(
        fox_w_in=dw_fox_in[:, :w["fox_w_in"].shape[1]],
        fox_b_f=dbf[0, :HEADS],
        fox_q_gain=dqg[0, :HEAD_DIM] + dqg[0, HEAD_DIM:],
        fox_k_gain=dkg[0, :HEAD_DIM] + dkg[0, HEAD_DIM:],
        fox_w_out=dw_fox_out,
        ada_b=jnp.concatenate([dmod0, dmod1], axis=0),
        norm1_g=jnp.concatenate([dn1g_0, dn1g_1], axis=0), norm2_g=jnp.concatenate([dn2g_0, dn2g_1], axis=0),
    )
    return loss[0, 0], dx0, grads, exchanged


_HBM = pl.BlockSpec(memory_space=pl.ANY)
N_PEER = N_DEV - 1


def _xchg_out_shapes(arrs, scatter):
    return [jax.ShapeDtypeStruct(a.shape if s else (N_DEV,) + a.shape, a.dtype) for a, s in zip(arrs, scatter)]


def _xchg_sems(n):
    return [pltpu.SemaphoreType.DMA((n * N_PEER,)), pltpu.SemaphoreType.DMA((n * N_PEER,)), pltpu.SemaphoreType.DMA((n,))]


def _xchg_copies(ins, outs, scatter, send, recv, loc):
    x, y, c = lax.axis_index("x"), lax.axis_index("y"), lax.axis_index("c")
    me = 4 * x + 2 * y + c
    copies = []
    for a in range(len(ins)):
        copies.append(pltpu.make_async_copy(ins[a].at[me] if scatter[a] else ins[a], outs[a].at[me], loc.at[a]))
        for k in range(1, N_DEV):
            px = 1 - x if k & 4 else x
            py = 1 - y if k & 2 else y
            pc = 1 - c if k & 1 else c
            copies.append(pltpu.make_async_remote_copy(
                src_ref=ins[a].at[4 * px + 2 * py + pc] if scatter[a] else ins[a], dst_ref=outs[a].at[me],
                send_sem=send.at[a * N_PEER + k - 1], recv_sem=recv.at[a * N_PEER + k - 1],
                device_id=(px, py, pc), device_id_type=MESH))
    return copies


def _exchange(arrs, scatter, *, name):
    n = len(arrs)

    def body(*refs):
        copies = _xchg_copies(refs[:n], refs[n:2 * n], scatter, *refs[2 * n:])
        for cp in copies:
            cp.start()
        for cp in copies:
            cp.wait()

    return pl.pallas_call(
        body, in_specs=[_HBM] * n, out_specs=[_HBM] * n, out_shape=_xchg_out_shapes(arrs, scatter),
        scratch_shapes=_xchg_sems(n),
        compiler_params=pltpu.CompilerParams(has_side_effects=True), name=name)(*arrs)


def _adamw(w, parts, m, v, *, name, tr=256):
    R, C = w.shape
    P = parts.shape[0]
    tr = next(t for t in range(min(R, tr), 0, -1) if R % t == 0 and (t % 16 == 0 or t == R))
    c1 = 1.0 - ADAM_B1 ** ADAM_STEP
    c2 = 1.0 - ADAM_B2 ** ADAM_STEP

    def body(w_ref, p_ref, m_ref, v_ref, g_ref, d_ref, mo_ref, vo_ref):
        g = p_ref[0].astype(F32)
        for p in range(1, P):
            g = g + p_ref[p].astype(F32)
        mn = ADAM_B1 * m_ref[...] + (1.0 - ADAM_B1) * g
        vn = ADAM_B2 * v_ref[...] + (1.0 - ADAM_B2) * (g * g)
        g_ref[...] = g
        mo_ref[...] = mn
        vo_ref[...] = vn
        d_ref[...] = -ADAM_LR * ((mn / c1) / (jnp.sqrt(vn / c2) + ADAM_EPS) + ADAM_WD * w_ref[...])

    row = pl.BlockSpec((tr, C), lambda i: (i, 0))
    return pl.pallas_call(
        body, grid=(R // tr,), in_specs=[row, pl.BlockSpec((P, tr, C), lambda i: (0, i, 0)), row, row],
        out_specs=[row] * 4, out_shape=[jax.ShapeDtypeStruct((R, C), F32)] * 4,
        compiler_params=_params("parallel"), name=name)(w, parts, m, v)


def _sum_parts(parts, *, name):
    P, R, C = parts.shape

    def body(p_ref, o_ref):
        g = p_ref[0]
        for p in range(1, P):
            g = g + p_ref[p]
        o_ref[...] = g

    return pl.pallas_call(body, out_shape=jax.ShapeDtypeStruct((R, C), F32), name=name)(parts)


WEIGHTS = ["fox_w_in", "fox_b_f", "fox_q_gain", "fox_k_gain", "fox_w_out", "sgu_w_in", "sgu_b_in", "sgu_v_gain",
           "sgu_v_bias", "sgu_w_s", "sgu_b_s", "sgu_w_out", "ffn_w_up", "ffn_conv_w", "ffn_conv_b", "ffn_w_down",
           "ada_w", "ada_b", "norm1_g", "norm2_g", "final_g"]
BIG_AXIS = dict(fox_w_in=1, fox_w_out=0, sgu_w_in=1, sgu_w_out=0, ffn_w_up=1, ffn_w_down=0, ada_w=1)
SMALL_SHARDED = ["sgu_b_in", "sgu_v_gain", "sgu_v_bias", "ffn_conv_w"]
SINGLE_LAYER = ("fox_", "sgu_")
BEFORE_ATTENTION = ["fox_w_in", "fox_w_out"]
AFTER_ATTENTION = ["sgu_w_in", "sgu_w_out", "ffn_w_up", "ffn_w_down"]


def _assemble(stacked, layers, axis):
    _, lr, cc = stacked.shape
    r = lr // layers
    s4 = stacked.reshape(N_DEV, layers, r, cc)
    if axis == 0:
        return s4.transpose(1, 0, 2, 3).reshape(layers, N_DEV * r, cc)
    return s4.transpose(1, 2, 0, 3).reshape(layers, r, N_DEV * cc)


def _disassemble(full, axis):
    layers, R, C = full.shape
    if axis == 0:
        r = R // N_DEV
        return full.reshape(layers, N_DEV, r, C).transpose(1, 0, 2, 3).reshape(N_DEV, layers * r, C)
    cc = C // N_DEV
    return full.reshape(layers, R, N_DEV, cc).transpose(2, 0, 1, 3).reshape(N_DEV, layers * R, cc)


def kernel(x, c, fox_w_in, fox_b_f, fox_q_gain, fox_k_gain, fox_w_out, sgu_w_in, sgu_b_in, sgu_v_gain, sgu_v_bias, sgu_w_s, sgu_b_s, sgu_w_out, ffn_w_up, ffn_conv_w, ffn_conv_b, ffn_w_down, ada_w, ada_b, norm1_g, norm2_g, final_g, loss_target, m_fox_w_in, m_fox_b_f, m_fox_q_gain, m_fox_k_gain, m_fox_w_out, m_sgu_w_in, m_sgu_b_in, m_sgu_v_gain, m_sgu_v_bias, m_sgu_w_s, m_sgu_b_s, m_sgu_w_out, m_ffn_w_up, m_ffn_conv_w, m_ffn_conv_b, m_ffn_w_down, m_ada_w, m_ada_b, m_norm1_g, m_norm2_g, m_final_g, v_fox_w_in, v_fox_b_f, v_fox_q_gain, v_fox_k_gain, v_fox_w_out, v_sgu_w_in, v_sgu_b_in, v_sgu_v_gain, v_sgu_v_bias, v_sgu_w_s, v_sgu_b_s, v_sgu_w_out, v_ffn_w_up, v_ffn_conv_w, v_ffn_conv_b, v_ffn_w_down, v_ada_w, v_ada_b, v_norm1_g, v_norm2_g, v_final_g):
    args = dict(locals())
    wts = {n: args[n] for n in WEIGHTS}
    ms = {n: args["m_" + n] for n in WEIGHTS}
    vs = {n: args["v_" + n] for n in WEIGHTS}
    me = 4 * lax.axis_index("x") + 2 * lax.axis_index("y") + lax.axis_index("c")

    shard2d = lambda n: wts[n].astype(BF16).reshape(-1, wts[n].shape[-1])

    def assemble_big(names, got):
        out = {}
        for n, g in zip(names, got):
            f = _assemble(g, wts[n].shape[0], BIG_AXIS[n])
            out[n] = f[0] if n.startswith(SINGLE_LAYER) else f
        return out

    def blocks_of(names, grads):
        return [_disassemble(grads[n] if grads[n].ndim == 3 else grads[n][None], BIG_AXIS[n]) for n in names]

    send = [c] + [shard2d(n) for n in BEFORE_ATTENTION] + [wts[n].reshape(-1, wts[n].shape[-1]) for n in SMALL_SHARDED]
    got = _exchange(send, [False] * len(send), name="gather_first")
    c_all = got[0].reshape(N_DEV, -1)
    full = assemble_big(BEFORE_ATTENTION, got[1:1 + len(BEFORE_ATTENTION)])
    for n, g in zip(SMALL_SHARDED, got[1 + len(BEFORE_ATTENTION):]):
        lead = wts[n].shape[:-1]
        f = jnp.moveaxis(g.reshape((N_DEV,) + wts[n].shape), 0, -2).reshape(lead + (-1,))
        full[n] = f[0] if n.startswith(SINGLE_LAYER) else f
    for n in WEIGHTS:
        if n not in full and n not in BIG_AXIS:
            full[n] = wts[n][0] if n.startswith(SINGLE_LAYER) else wts[n]

    ada_cols = wts["ada_w"].shape[-1]
    mod_rows = []
    for i in range(2):
        b_mine = lax.dynamic_slice_in_dim(wts["ada_b"][i], me * ada_cols, ada_cols).reshape(1, ada_cols)
        m, c_act = _ada_mod(c_all, wts["ada_w"][i].astype(BF16), b_mine, name=f"ada_mod_{i}")
        mod_rows.append(m)
    got = _exchange([jnp.concatenate(mod_rows, axis=1)[:, None, :]], [True], name="exchange_mods")[0]
    d_model = x.shape[-1]
    mods = []
    for i in range(2):
        mod = got[:, 0, i * ada_cols:(i + 1) * ada_cols].reshape(1, N_DEV * ada_cols)
        mods.append([mod[:, k * d_model:(k + 1) * d_model] for k in range(6)])

    late = ([shard2d(n) for n in AFTER_ATTENTION], lambda g: assemble_big(AFTER_ATTENTION, g))
    loss, grad_x, grads, got_late = _local_step(x[0], loss_target[0], full, mods, late,
                                                lambda gr: blocks_of(AFTER_ATTENTION, gr))

    small = [n for n in WEIGHTS if n not in BIG_AXIS]
    flat = jnp.concatenate([loss.reshape(1)] + [grads[n].reshape(-1).astype(F32) for n in small])
    n_flat = flat.shape[0]
    rows = -(-n_flat // (8 * LANES)) * 8
    flat = jnp.pad(flat, (0, rows * LANES - n_flat)).reshape(rows, LANES)
    got = _exchange(blocks_of(BEFORE_ATTENTION, grads) + [flat], [True] * len(BEFORE_ATTENTION) + [False], name="exchange_last")
    flat_all = got[-1]
    total = _sum_parts(flat_all, name="sum_small_grads").reshape(-1)
    loss_out = total[0]

    off_ada = 1 + sum(math.prod(grads[n].shape) for n in small[:small.index("ada_b")])
    dmod_all = flat_all.reshape(N_DEV, -1)[:, off_ada:off_ada + 2 * N_DEV * ada_cols].reshape(N_DEV, 2, N_DEV * ada_cols)
    dmod_mine = lax.dynamic_slice_in_dim(dmod_all, me * ada_cols, ada_cols, axis=2)
    d_ada = [_mm(c_act, jnp.pad(dmod_mine[:, i], ((0, c_act.shape[0] - N_DEV), (0, 0))).astype(BF16), ta=True,
                 name=f"ada_dw_{i}") for i in range(2)]

    out_g, out_d, out_m, out_v = {}, {}, {}, {}
    summands = dict(zip(BEFORE_ATTENTION, got))
    summands.update(zip(AFTER_ATTENTION, got_late))
    summands["ada_w"] = jnp.concatenate(d_ada, axis=0)[None]
    for n, p in summands.items():
        shp = wts[n].shape
        two_d = lambda a: a.reshape(-1, shp[-1])
        g, d, mn, vn = _adamw(two_d(wts[n]), p, two_d(ms[n]), two_d(vs[n]), name=f"adamw_{n}")
        out_g[n], out_d[n], out_m[n], out_v[n] = (a.reshape(shp) for a in (g, d, mn, vn))
    off = 1
    small_g = {}
    for n in small:
        full_shape = grads[n].shape
        size = math.prod(full_shape)
        g = total[off:off + size].reshape(full_shape)
        off += size
        if n in SMALL_SHARDED:
            blk = full_shape[-1] // N_DEV
            g = lax.dynamic_slice_in_dim(g, me * blk, blk, axis=g.ndim - 1)
        small_g[n] = g.reshape(wts[n].shape)
    cat = lambda d: jnp.concatenate([d[n].reshape(-1) for n in small])
    n_small = sum(math.prod(wts[n].shape) for n in small)
    rows2 = -(-n_small // (256 * LANES)) * 256
    pack = lambda d, fill: jnp.pad(cat(d), (0, rows2 * LANES - n_small), constant_values=fill).reshape(rows2, LANES)
    g, d, mn, vn = _adamw(pack(wts, 0.0), pack(small_g, 0.0)[None], pack(ms, 0.0), pack(vs, 1.0), name="adamw_small")
    off = 0
    for n in small:
        size = math.prod(wts[n].shape)
        for src, dst in ((g, out_g), (d, out_d), (mn, out_m), (vn, out_v)):
            dst[n] = src.reshape(-1)[off:off + size].reshape(wts[n].shape)
        off += size

    return (loss_out, grad_x[None], *[out_g[n] for n in WEIGHTS], *[out_d[n] for n in WEIGHTS],
            *[out_m[n] for n in WEIGHTS], *[out_v[n] for n in WEIGHTS])
```

```python
import functools
import math

import jax
import jax.numpy as jnp
from jax import lax
from jax.experimental import pallas as pl
from jax.experimental.pallas import tpu as pltpu

F32, BF16 = jnp.float32, jnp.bfloat16
N_DEV = 8
HEADS, HEAD_DIM = 16, 64
HEAD_PAIRS = HEADS // 2
LANES = 128
EPS = 1e-6
SGU_BLOCK, SGU_GROUPS, SGU_CHUNK = 128, 8, 64
CONV_WIDTH = 3
ADAM_LR, ADAM_B1, ADAM_B2, ADAM_EPS, ADAM_WD, ADAM_STEP = 0.001, 0.9, 0.999, 1e-08, 0.01, 10
NEG = -1e30
GELU_C0, GELU_C1 = math.sqrt(2.0 / math.pi), 0.044715
MESH = pl.DeviceIdType.MESH
VMEM_LIMIT = 56 * 1024 * 1024


def _tile(dim, pref):
    if dim <= pref:
        return dim
    t = (pref // LANES) * LANES
    while t >= LANES:
        if dim % t == 0:
            return t
        t -= LANES
    return dim


def _params(*sem):
    return pltpu.CompilerParams(dimension_semantics=sem, vmem_limit_bytes=VMEM_LIMIT)


def _mm(a, b, *, name, ta=False, tb=False, out_dtype=F32, tm=1024, tn=1024, tk=1024, res=None):
    M = a.shape[1] if ta else a.shape[0]
    K = a.shape[0] if ta else a.shape[1]
    N = b.shape[0] if tb else b.shape[1]
    tm, tn, tk = _tile(M, tm), _tile(N, tn), _tile(K, tk)
    nk = K // tk
    dims = (((0 if ta else 1,), (1 if tb else 0,)), ((), ()))
    a_spec = pl.BlockSpec((tk, tm), lambda i, j, k: (k, i)) if ta else pl.BlockSpec((tm, tk), lambda i, j, k: (i, k))
    b_spec = pl.BlockSpec((tn, tk), lambda i, j, k: (j, k)) if tb else pl.BlockSpec((tk, tn), lambda i, j, k: (k, j))
    o_spec = pl.BlockSpec((tm, tn), lambda i, j, k: (i, j))

    def accumulate(a_ref, b_ref, acc):
        @pl.when(pl.program_id(2) == 0)
        def _():
            acc[...] = jnp.zeros_like(acc)
        acc[...] += lax.dot_general(a_ref[...], b_ref[...], dims, preferred_element_type=F32)

    if res is None:
        def body(a_ref, b_ref, o_ref, acc):
            accumulate(a_ref, b_ref, acc)

            @pl.when(pl.program_id(2) == nk - 1)
            def _():
                o_ref[...] = acc[...].astype(o_ref.dtype)

        return pl.pallas_call(
            body, grid=(M // tm, N // tn, nk), in_specs=[a_spec, b_spec], out_specs=o_spec,
            out_shape=jax.ShapeDtypeStruct((M, N), out_dtype), scratch_shapes=[pltpu.VMEM((tm, tn), F32)],
            compiler_params=_params("parallel", "parallel", "arbitrary"), name=name)(a, b)

    x, gate = res

    def body_res(a_ref, b_ref, x_ref, g_ref, o_ref, y_ref, acc):
        accumulate(a_ref, b_ref, acc)

        @pl.when(pl.program_id(2) == nk - 1)
        def _():
            y = acc[...]
            o_ref[...] = x_ref[...] + g_ref[...] * y
            y_ref[...] = y.astype(BF16)

    return pl.pallas_call(
        body_res, grid=(M // tm, N // tn, nk),
        in_specs=[a_spec, b_spec, o_spec, pl.BlockSpec((1, tn), lambda i, j, k: (0, j))],
        out_specs=[o_spec, o_spec],
        out_shape=[jax.ShapeDtypeStruct((M, N), F32), jax.ShapeDtypeStruct((M, N), BF16)],
        scratch_shapes=[pltpu.VMEM((tm, tn), F32)],
        compiler_params=_params("parallel", "parallel", "arbitrary"), name=name)(a, b, x, gate)


def _ada_mod(c_rows, w, b, *, name):
    R, D = c_rows.shape
    N = w.shape[1]
    tn = _tile(N, 1536)
    rows = 16
    c_pad = jnp.pad(c_rows, ((0, rows - R), (0, 0)))

    def body(c_ref, w_ref, b_ref, o_ref, ca_ref):
        cv = c_ref[...]
        ca16 = (cv * jax.nn.sigmoid(cv)).astype(BF16)
        ca_ref[...] = ca16
        o_ref[...] = jnp.dot(ca16, w_ref[...], preferred_element_type=F32) + b_ref[...]

    out, ca = pl.pallas_call(
        body, grid=(N // tn,),
        in_specs=[pl.BlockSpec((rows, D), lambda j: (0, 0)), pl.BlockSpec((D, tn), lambda j: (0, j)),
                  pl.BlockSpec((1, tn), lambda j: (0, j))],
        out_specs=[pl.BlockSpec((rows, tn), lambda j: (0, j)), pl.BlockSpec((rows, D), lambda j: (0, 0))],
        out_shape=[jax.ShapeDtypeStruct((rows, N), F32), jax.ShapeDtypeStruct((rows, D), BF16)],
        compiler_params=_params("arbitrary"), name=name)(c_pad, w, b)
    return out[0:R], ca


def _norm_mod_fwd(x, g, shift, scale, *, name, ts=512):
    S, D = x.shape
    ts = _tile(S, ts)
    row = pl.BlockSpec((ts, D), lambda i: (i, 0))
    vec = pl.BlockSpec((1, D), lambda i: (0, 0))

    def body(x_ref, g_ref, sh_ref, sc_ref, h_ref):
        xv = x_ref[...]
        r = lax.rsqrt(jnp.mean(xv * xv, axis=-1, keepdims=True) + EPS)
        h_ref[...] = ((xv * r * g_ref[...]) * (1.0 + sc_ref[...]) + sh_ref[...]).astype(BF16)

    return pl.pallas_call(body, grid=(S // ts,), in_specs=[row, vec, vec, vec], out_specs=row,
                          out_shape=jax.ShapeDtypeStruct((S, D), BF16),
                          compiler_params=_params("parallel"), name=name)(x, g, shift, scale)


def _acc_init(step, *refs):
    @pl.when(step == 0)
    def _():
        for r in refs:
            r[...] = jnp.zeros_like(r)


def _colsum(v):
    return jnp.sum(v, axis=0, keepdims=True)


def _norm_mod_bwd(dh, x, g, scale, dres, prev=None, *, name, ts=512):
    S, D = x.shape
    ts = _tile(S, ts)
    row = pl.BlockSpec((ts, D), lambda i: (i, 0))
    vec = pl.BlockSpec((1, D), lambda i: (0, 0))
    has_prev = prev is not None

    def body(*refs):
        if has_prev:
            dh_ref, x_ref, g_ref, sc_ref, dres_ref, y_ref, gate_ref, dx_ref, dg_ref, dsh_ref, dsc_ref, dy_ref, dgate_ref = refs
            _acc_init(pl.program_id(0), dg_ref, dsh_ref, dsc_ref, dgate_ref)
        else:
            dh_ref, x_ref, g_ref, sc_ref, dres_ref, dx_ref, dg_ref, dsh_ref, dsc_ref = refs
            _acc_init(pl.program_id(0), dg_ref, dsh_ref, dsc_ref)
        xv, dhv, gv = x_ref[...], dh_ref[...], g_ref[...]
        r = lax.rsqrt(jnp.mean(xv * xv, axis=-1, keepdims=True) + EPS)
        xh = xv * r
        dsh_ref[...] += _colsum(dhv)
        dsc_ref[...] += _colsum(dhv * (xh * gv))
        dn = dhv * (1.0 + sc_ref[...])
        dg_ref[...] += _colsum(dn * xh)
        dxh = dn * gv
        dx = dres_ref[...] + r * (dxh - xh * jnp.mean(dxh * xh, axis=-1, keepdims=True))
        dx_ref[...] = dx
        if has_prev:
            dy_ref[...] = (gate_ref[...] * dx).astype(BF16)
            dgate_ref[...] += _colsum(dx * y_ref[...].astype(F32))

    ins, in_specs = [dh, x, g, scale, dres], [row, row, vec, vec, row]
    outs = [jax.ShapeDtypeStruct((S, D), F32)] + [jax.ShapeDtypeStruct((1, D), F32)] * 3
    out_specs = [row, vec, vec, vec]
    if has_prev:
        ins += list(prev)
        in_specs += [row, vec]
        outs += [jax.ShapeDtypeStruct((S, D), BF16), jax.ShapeDtypeStruct((1, D), F32)]
        out_specs += [row, vec]
    return pl.pallas_call(body, grid=(S // ts,), in_specs=in_specs, out_specs=out_specs, out_shape=outs,
                          compiler_params=_params("arbitrary"), name=name)(*ins)


def _final_loss(x, g, target, y, gate, *, name, ts=512):
    S, D = x.shape
    ts = _tile(S, ts)
    row = pl.BlockSpec((ts, D), lambda i: (i, 0))
    vec = pl.BlockSpec((1, D), lambda i: (0, 0))
    lvec = pl.BlockSpec((1, LANES), lambda i: (0, 0))

    def body(x_ref, g_ref, t_ref, y_ref, gate_ref, loss_ref, dg_ref, dx_ref, dy_ref, dgate_ref):
        _acc_init(pl.program_id(0), loss_ref, dg_ref, dgate_ref)
        xv, gv = x_ref[...], g_ref[...]
        r = lax.rsqrt(jnp.mean(xv * xv, axis=-1, keepdims=True) + EPS)
        xh = xv * r
        e = xh * gv - t_ref[...]
        loss_ref[...] += 0.5 * jnp.sum(jnp.mean(e * e, axis=-1, keepdims=True), axis=0, keepdims=True)
        dout = e * (1.0 / D)
        dg_ref[...] += _colsum(dout * xh)
        dxh = dout * gv
        dx = r * (dxh - xh * jnp.mean(dxh * xh, axis=-1, keepdims=True))
        dx_ref[...] = dx
        dy_ref[...] = (gate_ref[...] * dx).astype(BF16)
        dgate_ref[...] += _colsum(dx * y_ref[...].astype(F32))

    return pl.pallas_call(
        body, grid=(S // ts,), in_specs=[row, vec, row, row, vec], out_specs=[lvec, vec, row, row, vec],
        out_shape=[jax.ShapeDtypeStruct((1, LANES), F32), jax.ShapeDtypeStruct((1, D), F32),
                   jax.ShapeDtypeStruct((S, D), F32), jax.ShapeDtypeStruct((S, D), BF16),
                   jax.ShapeDtypeStruct((1, D), F32)],
        compiler_params=_params("arbitrary"), name=name)(x, g, target, y, gate)


def _head_mean(v, gmat):
    hi = v.astype(BF16)
    lo = (v - hi.astype(F32)).astype(BF16)
    return jnp.dot(hi, gmat, preferred_element_type=F32) + jnp.dot(lo, gmat, preferred_element_type=F32)


L_F, L_ONE, L_SHIFT = HEAD_DIM, HEAD_DIM + 3, HEAD_DIM + 6
SHIFT_FREE_LOGIT_BOUND = 60.0


def _lane():
    return lax.broadcasted_iota(jnp.int32, (1, LANES), 1)


def _split3(v):
    p1 = v.astype(BF16).astype(F32)
    r1 = v - p1
    p2 = r1.astype(BF16).astype(F32)
    p3 = (r1 - p2).astype(BF16).astype(F32)
    return p1, p2, p3


def _put3(lane, first, pieces):
    out = jnp.where(lane == first, pieces[0], 0.0)
    for k in (1, 2):
        out = out + jnp.where(lane == first + k, pieces[k], 0.0)
    return out


def _ones3(lane, first):
    return jnp.where((lane >= first) & (lane < first + 3), 1.0, 0.0)


def _lane_col(v, lane, idx):
    return jnp.sum(jnp.where(lane == idx, v, 0.0), axis=-1, keepdims=True)


def _head_of_pair(pair, e, lane):
    return jnp.where(lane < HEAD_DIM, pair if e == 0 else pltpu.roll(pair, HEAD_DIM, 1), 0.0)


def _pair_of_heads(even, odd, lane):
    return jnp.where(lane < HEAD_DIM, even, pltpu.roll(odd, HEAD_DIM, 1))


def _fox_prep_fwd(proj, fcum, qgain, kgain, gmat, *, name, ts=256):
    S = proj.shape[0]
    D = HEADS * HEAD_DIM
    ts = _tile(S, ts)
    scale = HEAD_DIM ** -0.5

    def body(p_ref, f_ref, qg_ref, kg_ref, gm_ref, q_ref, k_ref, v_ref):
        gm, lane, fc = gm_ref[...], _lane(), f_ref[...]
        for cpair in range(HEAD_PAIRS):
            qv = p_ref[:, pl.ds(cpair * LANES, LANES)].astype(F32)
            kv = p_ref[:, pl.ds(D + cpair * LANES, LANES)].astype(F32)
            vv = p_ref[:, pl.ds(2 * D + cpair * LANES, LANES)].astype(F32)
            qn = (qv * lax.rsqrt(_head_mean(qv * qv, gm) + EPS) * qg_ref[...]) * scale
            kn = kv * lax.rsqrt(_head_mean(kv * kv, gm) + EPS) * kg_ref[...]
            for e in range(2):
                h = 2 * cpair + e
                cols = pl.ds(h * LANES, LANES)
                f3 = _split3(_lane_col(fc, lane, h))
                q_ref[:, cols] = (_head_of_pair(qn, e, lane) + _put3(lane, L_F, f3) + _ones3(lane, L_ONE)).astype(BF16)
                k_ref[:, cols] = (_head_of_pair(kn, e, lane) + _ones3(lane, L_F)
                                  - _put3(lane, L_ONE, f3) + _ones3(lane, L_SHIFT)).astype(BF16)
                v_ref[:, cols] = (_head_of_pair(vv, e, lane) + _ones3(lane, L_F)).astype(BF16)

    vec = pl.BlockSpec((1, LANES), lambda i: (0, 0))
    wide = pl.BlockSpec((ts, HEADS * LANES), lambda i: (i, 0))
    return pl.pallas_call(
        body, grid=(S // ts,),
        in_specs=[pl.BlockSpec((ts, 3 * D), lambda i: (i, 0)), pl.BlockSpec((ts, LANES), lambda i: (i, 0)), vec, vec,
                  pl.BlockSpec((LANES, LANES), lambda i: (0, 0))],
        out_specs=[wide, wide, wide], out_shape=[jax.ShapeDtypeStruct((S, HEADS * LANES), BF16)] * 3,
        compiler_params=_params("parallel"), name=name)(proj, fcum, qgain, kgain, gmat)


def _fox_prep_bwd(proj, dq_aug, dk_aug, dv_aug, dog, qgain, kgain, gmat, *, name, ts=256):
    S = proj.shape[0]
    D = HEADS * HEAD_DIM
    ts = _tile(S, ts)
    scale = HEAD_DIM ** -0.5

    def body(p_ref, dq_ref, dk_ref, dv_ref, dog_ref, qg_ref, kg_ref, gm_ref, o_ref, df_ref, dqg_ref, dkg_ref):
        _acc_init(pl.program_id(0), dqg_ref, dkg_ref)
        gm, lane = gm_ref[...], _lane()
        df = jnp.zeros((ts, LANES), F32)
        for cpair in range(HEAD_PAIRS):
            tiles = []
            for e in range(2):
                h = 2 * cpair + e
                cols = pl.ds(h * LANES, LANES)
                tq, tk = dq_ref[:, cols], dk_ref[:, cols]
                df = jnp.where(lane == h, _lane_col(tq, lane, L_F) - _lane_col(tk, lane, L_ONE), df)
                tiles.append((tq, tk, dv_ref[:, cols].astype(F32)))
            pair = [_pair_of_heads(tiles[0][k], tiles[1][k], lane) for k in range(3)]
            for half, g_ref, dg_ref, mult in ((0, qg_ref, dqg_ref, scale), (1, kg_ref, dkg_ref, 1.0)):
                v = p_ref[:, pl.ds(half * D + cpair * LANES, LANES)].astype(F32)
                r = lax.rsqrt(_head_mean(v * v, gm) + EPS)
                xh = v * r
                dn = pair[half] * mult
                dg_ref[...] += _colsum(dn * xh)
                dxh = dn * g_ref[...]
                o_ref[:, pl.ds(half * D + cpair * LANES, LANES)] = (r * (dxh - xh * _head_mean(dxh * xh, gm))).astype(BF16)
            o_ref[:, pl.ds(2 * D + cpair * LANES, LANES)] = pair[2].astype(BF16)
        o_ref[:, pl.ds(3 * D, D)] = dog_ref[...]
        o_ref[:, pl.ds(4 * D, LANES)] = jnp.zeros((ts, LANES), BF16)
        df_ref[...] = df

    row = pl.BlockSpec((ts, D), lambda i: (i, 0))
    wide = pl.BlockSpec((ts, HEADS * LANES), lambda i: (i, 0))
    vec = pl.BlockSpec((1, LANES), lambda i: (0, 0))
    return pl.pallas_call(
        body, grid=(S // ts,),
        in_specs=[pl.BlockSpec((ts, 2 * D), lambda i: (i, 0)), wide, wide, wide, row, vec, vec,
                  pl.BlockSpec((LANES, LANES), lambda i: (0, 0))],
        out_specs=[pl.BlockSpec((ts, 4 * D + LANES), lambda i: (i, 0)), pl.BlockSpec((ts, LANES), lambda i: (i, 0)), vec, vec],
        out_shape=[jax.ShapeDtypeStruct((S, 4 * D + LANES), BF16), jax.ShapeDtypeStruct((S, LANES), F32),
                   jax.ShapeDtypeStruct((1, LANES), F32), jax.ShapeDtypeStruct((1, LANES), F32)],
        compiler_params=_params("arbitrary"), name=name)(proj, dq_aug, dk_aug, dv_aug, dog, qgain, kgain, gmat)


def _log_sigmoid(z):
    return jnp.minimum(z, 0.0) - jnp.log(1.0 + jnp.exp(-jnp.abs(z)))


def _fox_decay_fwd(fl, bf, *, name, tb=256):
    S = fl.shape[0]
    tb = _tile(S, tb)

    def body(fl_ref, b_ref, o_ref, carry):
        @pl.when(pl.program_id(0) == 0)
        def _():
            carry[...] = jnp.zeros_like(carry)
        logf = _log_sigmoid(fl_ref[...] + b_ref[...])
        tri = (lax.broadcasted_iota(jnp.int32, (tb, tb), 1) <= lax.broadcasted_iota(jnp.int32, (tb, tb), 0)).astype(F32)
        cs = jnp.dot(tri, logf, preferred_element_type=F32, precision=lax.Precision.HIGHEST) + carry[...]
        o_ref[...] = cs
        carry[...] = _row_of(cs, tb - 1)

    return pl.pallas_call(
        body, grid=(S // tb,),
        in_specs=[pl.BlockSpec((tb, LANES), lambda i: (i, 0)), pl.BlockSpec((1, LANES), lambda i: (0, 0))],
        out_specs=pl.BlockSpec((tb, LANES), lambda i: (i, 0)),
        out_shape=jax.ShapeDtypeStruct((S, LANES), F32), scratch_shapes=[pltpu.VMEM((1, LANES), F32)],
        compiler_params=_params("arbitrary"), name=name)(fl, bf)


def _fox_decay_bwd(dF, fl, bf, dproj, *, name, tb=256):
    S = fl.shape[0]
    tb = _tile(S, tb)
    n = S // tb
    last_col = dproj.shape[1] // LANES - 1

    def body(df_ref, fl_ref, b_ref, dproj_hbm, o_ref, db_ref, carry):
        del dproj_hbm
        @pl.when(pl.program_id(0) == 0)
        def _():
            carry[...] = jnp.zeros_like(carry)
            db_ref[...] = jnp.zeros_like(db_ref)
        tri = (lax.broadcasted_iota(jnp.int32, (tb, tb), 1) >= lax.broadcasted_iota(jnp.int32, (tb, tb), 0)).astype(F32)
        rc = jnp.dot(tri, df_ref[...], preferred_element_type=F32, precision=lax.Precision.HIGHEST) + carry[...]
        carry[...] = _row_of(rc, 0)
        dfl = rc * jax.nn.sigmoid(-(fl_ref[...] + b_ref[...]))
        o_ref[...] = dfl.astype(BF16)
        db_ref[...] += _colsum(dfl)

    rev = pl.BlockSpec((tb, LANES), lambda i: (n - 1 - i, 0))
    vec = pl.BlockSpec((1, LANES), lambda i: (0, 0))
    return pl.pallas_call(
        body, grid=(n,), in_specs=[rev, rev, vec, pl.BlockSpec(memory_space=pl.ANY)],
        out_specs=[pl.BlockSpec((tb, LANES), lambda i: (n - 1 - i, last_col)), vec],
        out_shape=[jax.ShapeDtypeStruct(dproj.shape, BF16), jax.ShapeDtypeStruct((1, LANES), F32)],
        scratch_shapes=[pltpu.VMEM((1, LANES), F32)], input_output_aliases={3: 0},
        compiler_params=_params("arbitrary"), name=name)(dF, fl, bf, dproj)


_NT = (((1,), (1,)), ((), ()))
_TN = (((0,), (0,)), ((), ()))


def _causal(T, transposed=False):
    r, c = lax.broadcasted_iota(jnp.int32, (T, T), 0), lax.broadcasted_iota(jnp.int32, (T, T), 1)
    return r <= c if transposed else c <= r


def _with_shift(q_tile, shift, lane):
    keep = jnp.where((lane >= L_SHIFT) & (lane < L_SHIFT + 3), 0.0, q_tile)
    return (keep + _put3(lane, L_SHIFT, _split3(-shift))).astype(BF16)


def _ride_along(xchg, n_in, n_out, grid):
    if xchg is None:
        return (lambda body: body), [], [], [], [], []
    arrs, scatter = xchg
    n = len(arrs)

    def wrap(body):
        def wrapped(*refs):
            own_in, x_in = refs[:n_in], refs[n_in:n_in + n]
            own_out, x_out = refs[n_in + n:n_in + n + n_out], refs[n_in + n + n_out:n_in + 2 * n + n_out]
            rest = refs[n_in + 2 * n + n_out:]
            own_scratch, sems = rest[:len(rest) - 3], rest[len(rest) - 3:]
            ids = [pl.program_id(d) for d in range(len(grid))]
            first = functools.reduce(jnp.logical_and, [i == 0 for i in ids])
            last = functools.reduce(jnp.logical_and, [i == g - 1 for i, g in zip(ids, grid)])

            @pl.when(first)
            def _():
                for cp in _xchg_copies(x_in, x_out, scatter, *sems):
                    cp.start()

            body(*own_in, *own_out, *own_scratch)

            @pl.when(last)
            def _():
                for cp in _xchg_copies(x_in, x_out, scatter, *sems):
                    cp.wait()

        return wrapped

    return wrap, [_HBM] * n, [_HBM] * n, _xchg_out_shapes(arrs, scatter), _xchg_sems(n), list(arrs)


def _attn_rowmax(q_aug, k_aug, *, name, T=1024):
    S = q_aug.shape[0]
    T = _tile(S, T)
    n = S // T

    def body(q_ref, k_ref, o_ref, m_s):
        i, j = pl.program_id(1), pl.program_id(2)

        @pl.when(j == 0)
        def _():
            m_s[...] = jnp.full_like(m_s, NEG)

        def step(diag):
            s = lax.dot_general(q_ref[...], k_ref[...], _NT, preferred_element_type=F32)
            if diag:
                s = jnp.where(_causal(T), s, NEG)
            m = m_s[...]
            for cb in range(T // LANES):
                m = jnp.maximum(m, s[:, cb * LANES:(cb + 1) * LANES])
            m_s[...] = m

        @pl.when(j < i)
        def _():
            step(False)

        @pl.when(j == i)
        def _():
            step(True)
            o_ref[...] = _with_shift(q_ref[...].astype(F32), jnp.max(m_s[...], axis=-1, keepdims=True), _lane())

    qrow = pl.BlockSpec((T, LANES), lambda h, i, j: (i, h))
    return pl.pallas_call(
        body, grid=(HEADS, n, n),
        in_specs=[qrow, pl.BlockSpec((T, LANES), lambda h, i, j: (jnp.minimum(j, i), h))],
        out_specs=qrow, out_shape=jax.ShapeDtypeStruct(q_aug.shape, BF16),
        scratch_shapes=[pltpu.VMEM((T, LANES), F32)],
        compiler_params=_params("parallel", "parallel", "arbitrary"), name=name)(q_aug, k_aug)


def _attn_fwd(q_max, k_aug, v_aug, xchg=None, *, name, T=1024):
    S = q_max.shape[0]
    T = _tile(S, T)
    n = S // T
    wrap, x_in, x_out, x_shapes, x_sems, x_ops = _ride_along(xchg, 3, 2, (HEADS, n, n))

    def body(q_ref, k_ref, v_ref, o_ref, qb_ref, acc_s):
        i, j = pl.program_id(1), pl.program_id(2)

        @pl.when(j == 0)
        def _():
            acc_s[...] = jnp.zeros_like(acc_s)

        def step(diag):
            s = lax.dot_general(q_ref[...], k_ref[...], _NT, preferred_element_type=F32)
            if diag:
                s = jnp.where(_causal(T), s, NEG)
            acc_s[...] += jnp.dot(jnp.exp(s).astype(BF16), v_ref[...], preferred_element_type=F32)

        @pl.when(j < i)
        def _():
            step(False)

        @pl.when(j == i)
        def _():
            step(True)
            lane = _lane()
            acc = acc_s[...]
            l = _lane_col(acc, lane, L_F)
            o_ref[...] = acc / l
            qf = q_ref[...].astype(F32)
            row_max = -jnp.sum(jnp.where((lane >= L_SHIFT) & (lane < L_SHIFT + 3), qf, 0.0), axis=-1, keepdims=True)
            qb_ref[...] = _with_shift(qf, row_max + jnp.log(l), lane)

    qrow = pl.BlockSpec((T, LANES), lambda h, i, j: (i, h))
    kv = pl.BlockSpec((T, LANES), lambda h, i, j: (jnp.minimum(j, i), h))
    outs = pl.pallas_call(
        wrap(body), grid=(HEADS, n, n), in_specs=[qrow, kv, kv] + x_in, out_specs=[qrow, qrow] + x_out,
        out_shape=[jax.ShapeDtypeStruct(q_max.shape, F32), jax.ShapeDtypeStruct(q_max.shape, BF16)] + x_shapes,
        scratch_shapes=[pltpu.VMEM((T, LANES), F32)] + x_sems,
        compiler_params=_params("arbitrary", "arbitrary", "arbitrary"), name=name)(q_max, k_aug, v_aug, *x_ops)
    return outs[0], outs[1], outs[2:]


def _attn_bwd(q_lse, k_aug, v_aug, do_aug, xchg=None, *, name, T=1024):
    S = q_lse.shape[0]
    T = _tile(S, T)
    n = S // T
    wrap, x_in, x_out, x_shapes, x_sems, x_ops = _ride_along(xchg, 4, 3, (HEADS, n, n))

    def body(q_ref, do_ref, k_ref, v_ref, dq_ref, dk_ref, dv_ref, dq_s, dk_s, dv_s):
        j, i = pl.program_id(1), pl.program_id(2)

        def step(diag):
            q, do, k, v = q_ref[...], do_ref[...], k_ref[...], v_ref[...]
            st = lax.dot_general(k, q, _NT, preferred_element_type=F32)
            if diag:
                st = jnp.where(_causal(T, transposed=True), st, NEG)
            pt = jnp.exp(st)
            dst = (pt * lax.dot_general(v, do, _NT, preferred_element_type=F32)).astype(BF16)
            dv_s[...] += jnp.dot(pt.astype(BF16), do, preferred_element_type=F32)
            dk_s[...] += jnp.dot(dst, q, preferred_element_type=F32)
            upd = lax.dot_general(dst, k, _TN, preferred_element_type=F32)

            @pl.when(j == 0)
            def _():
                dq_s[i] = upd

            @pl.when(j > 0)
            def _():
                dq_s[i] += upd

        @pl.when(i == j)
        def _():
            dk_s[...] = jnp.zeros_like(dk_s)
            dv_s[...] = jnp.zeros_like(dv_s)
            step(True)
            dq_ref[...] = dq_s[j]

        @pl.when(i > j)
        def _():
            step(False)

        @pl.when(i == n - 1)
        def _():
            dk_ref[...] = dk_s[...]
            dv_ref[...] = dv_s[...].astype(BF16)

    qrow = pl.BlockSpec((T, LANES), lambda h, j, i: (jnp.maximum(i, j), h))
    kv = pl.BlockSpec((T, LANES), lambda h, j, i: (j, h))
    outs = pl.pallas_call(
        wrap(body), grid=(HEADS, n, n), in_specs=[qrow, qrow, kv, kv] + x_in, out_specs=[kv, kv, kv] + x_out,
        out_shape=[jax.ShapeDtypeStruct(q_lse.shape, F32), jax.ShapeDtypeStruct(q_lse.shape, F32),
                   jax.ShapeDtypeStruct(q_lse.shape, BF16)] + x_shapes,
        scratch_shapes=[pltpu.VMEM((n, T, LANES), F32), pltpu.VMEM((T, LANES), F32), pltpu.VMEM((T, LANES), F32)] + x_sems,
        compiler_params=_params("arbitrary", "arbitrary", "arbitrary"), name=name)(q_lse, do_aug, k_aug, v_aug, *x_ops)
    return outs[0], outs[1], outs[2], outs[3:]


def _fox_gate_fwd(att_aug, proj, *, name, ts=256):
    S = att_aug.shape[0]
    D = HEADS * HEAD_DIM
    ts = _tile(S, ts)

    def body(a_ref, o_ref, att_ref, out_ref):
        lane = _lane()
        for cpair in range(HEAD_PAIRS):
            cols = pl.ds(cpair * LANES, LANES)
            pair = _pair_of_heads(a_ref[:, pl.ds(2 * cpair * LANES, LANES)], a_ref[:, pl.ds((2 * cpair + 1) * LANES, LANES)], lane)
            att_ref[:, cols] = pair
            out_ref[:, cols] = (pair * jax.nn.sigmoid(o_ref[:, cols].astype(F32))).astype(BF16)

    row = pl.BlockSpec((ts, D), lambda i: (i, 0))
    return pl.pallas_call(
        body, grid=(S // ts,),
        in_specs=[pl.BlockSpec((ts, HEADS * LANES), lambda i: (i, 0)), pl.BlockSpec((ts, D), lambda i: (i, 3))],
        out_specs=[row, row], out_shape=[jax.ShapeDtypeStruct((S, D), F32), jax.ShapeDtypeStruct((S, D), BF16)],
        compiler_params=_params("parallel"), name=name)(att_aug, proj)


def _fox_gate_bwd(da, att, proj, *, name, ts=256):
    S, D = att.shape
    ts = _tile(S, ts)

    def body(da_ref, a_ref, o_ref, do_ref, dog_ref):
        lane = _lane()
        for cpair in range(HEAD_PAIRS):
            cols = pl.ds(cpair * LANES, LANES)
            dav, av = da_ref[:, cols].astype(F32), a_ref[:, cols]
            sg = jax.nn.sigmoid(o_ref[:, cols].astype(F32))
            datt = (dav * sg).astype(BF16).astype(F32)
            dog_ref[:, cols] = (dav * av * sg * (1.0 - sg)).astype(BF16)
            prod = datt * av
            for e in range(2):
                in_head = (lane < HEAD_DIM) if e == 0 else (lane >= HEAD_DIM)
                delta = jnp.sum(jnp.where(in_head, prod, 0.0), axis=-1, keepdims=True)
                tile = _head_of_pair(datt, e, lane) + _put3(lane, L_F, _split3(-delta))
                do_ref[:, pl.ds((2 * cpair + e) * LANES, LANES)] = tile.astype(BF16)

    row = pl.BlockSpec((ts, D), lambda i: (i, 0))
    return pl.pallas_call(
        body, grid=(S // ts,), in_specs=[row, row, pl.BlockSpec((ts, D), lambda i: (i, 3))],
        out_specs=[pl.BlockSpec((ts, HEADS * LANES), lambda i: (i, 0)), row],
        out_shape=[jax.ShapeDtypeStruct((S, HEADS * LANES), BF16), jax.ShapeDtypeStruct((S, D), BF16)],
        compiler_params=_params("parallel"), name=name)(da, att, proj)


def _row_of(block, r):
    rows = lax.broadcasted_iota(jnp.int32, block.shape, 0)
    return jnp.sum(jnp.where(rows == r, block, 0.0), axis=0, keepdims=True)


def _shift_down(cur, tail, k):
    out = pltpu.roll(cur, k, 0)
    rows = lax.broadcasted_iota(jnp.int32, cur.shape, 0)
    for r in range(k):
        out = jnp.where(rows == r, _row_of(tail, tail.shape[0] - k + r), out)
    return out


def _shift_up(cur, head, k):
    n = cur.shape[0]
    out = pltpu.roll(cur, n - k, 0)
    rows = lax.broadcasted_iota(jnp.int32, cur.shape, 0)
    for r in range(k):
        out = jnp.where(rows == n - k + r, _row_of(head, r), out)
    return out


HALO = 16


CONV_TC = 1408


def _pair_tiles(v):
    nc = v.shape[-1] // (2 * CONV_TC)
    return jnp.swapaxes(v.reshape(v.shape[:-1] + (2, nc, CONV_TC)), -3, -2).reshape(v.shape)


def _unpair_tiles(v):
    nc = v.shape[-1] // (2 * CONV_TC)
    return jnp.swapaxes(v.reshape(v.shape[:-1] + (nc, 2, CONV_TC)), -3, -2).reshape(v.shape)


def _conv_rows(cur, tail, w_ref, b_ref, cols):
    a1, a2 = _shift_down(cur, tail, 1), _shift_down(cur, tail, 2)
    return a2 * w_ref[0:1, cols] + a1 * w_ref[1:2, cols] + cur * w_ref[2:3, cols] + b_ref[:, cols], (a2, a1, cur)


def _conv_gate_fwd(a, cw, cb, *, name, ts=256):
    S, F2 = a.shape
    tc = CONV_TC
    ts = _tile(S, ts)
    nc = F2 // (2 * tc)
    sub = ts // HALO
    halves = (pl.ds(0, tc), pl.ds(tc, tc))

    def body(a_ref, t_ref, w_ref, b_ref, o_ref):
        first = pl.program_id(1) == 0
        pre = []
        for cols in halves:
            tail = jnp.where(first, 0.0, t_ref[:, cols].astype(F32))
            pre.append(_conv_rows(a_ref[:, cols].astype(F32), tail, w_ref, b_ref, cols)[0])
        g, val = pre
        o_ref[...] = (g * jax.nn.sigmoid(g) * val).astype(BF16)

    return pl.pallas_call(
        body, grid=(nc, S // ts),
        in_specs=[pl.BlockSpec((ts, 2 * tc), lambda j, i: (i, j)),
                  pl.BlockSpec((HALO, 2 * tc), lambda j, i: (jnp.maximum(i * sub - 1, 0), j)),
                  pl.BlockSpec((CONV_WIDTH, 2 * tc), lambda j, i: (0, j)), pl.BlockSpec((1, 2 * tc), lambda j, i: (0, j))],
        out_specs=pl.BlockSpec((ts, tc), lambda j, i: (i, j)),
        out_shape=jax.ShapeDtypeStruct((S, F2 // 2), BF16),
        compiler_params=_params("parallel", "parallel"), name=name)(a, a, cw, cb)


def _conv_gate_bwd(a, dact, cw, cb, *, name, ts=256):
    S, F2 = a.shape
    tc = CONV_TC
    ts = _tile(S, ts)
    nc = F2 // (2 * tc)
    sub = ts // HALO
    n_rows = S // ts
    halves = (pl.ds(0, tc), pl.ds(tc, tc))

    def body(a_ref, at_ref, ah_ref, d_ref, dh_ref, w_ref, b_ref, da_ref, s_ref):
        i = pl.program_id(1)
        _acc_init(i, s_ref)

        def dpre_of(rows, tails, d):
            (g, taps_g), (val, taps_v) = [_conv_rows(rows[h], tails[h], w_ref, b_ref, halves[h]) for h in range(2)]
            sg = jax.nn.sigmoid(g)
            return (d * val * (sg * (1.0 + g * (1.0 - sg))), d * (g * sg)), (taps_g, taps_v)

        cur = [a_ref[:, c].astype(F32) for c in halves]
        tail = [jnp.where(i == 0, 0.0, at_ref[:, c].astype(F32)) for c in halves]
        dpre, taps = dpre_of(cur, tail, d_ref[...].astype(F32))
        head, _ = dpre_of([ah_ref[:, c].astype(F32) for c in halves], [x[ts - HALO:, :] for x in cur], dh_ref[...].astype(F32))
        for h, cols in enumerate(halves):
            dd = dpre[h]
            nxt = jnp.where(i == n_rows - 1, 0.0, head[h])
            da_ref[:, cols] = (dd * w_ref[2:3, cols] + _shift_up(dd, nxt, 1) * w_ref[1:2, cols]
                               + _shift_up(dd, nxt, 2) * w_ref[0:1, cols]).astype(BF16)
            for r in range(CONV_WIDTH):
                s_ref[r:r + 1, cols] += _colsum(dd * taps[h][r])
            s_ref[CONV_WIDTH:CONV_WIDTH + 1, cols] += _colsum(dd)

    nxt_rows = lambda i: jnp.minimum((i + 1) * sub, S // HALO - 1)
    return pl.pallas_call(
        body, grid=(nc, n_rows),
        in_specs=[pl.BlockSpec((ts, 2 * tc), lambda j, i: (i, j)),
                  pl.BlockSpec((HALO, 2 * tc), lambda j, i: (jnp.maximum(i * sub - 1, 0), j)),
                  pl.BlockSpec((HALO, 2 * tc), lambda j, i: (nxt_rows(i), j)),
                  pl.BlockSpec((ts, tc), lambda j, i: (i, j)), pl.BlockSpec((HALO, tc), lambda j, i: (nxt_rows(i), j)),
                  pl.BlockSpec((CONV_WIDTH, 2 * tc), lambda j, i: (0, j)), pl.BlockSpec((1, 2 * tc), lambda j, i: (0, j))],
        out_specs=[pl.BlockSpec((ts, 2 * tc), lambda j, i: (i, j)), pl.BlockSpec((8, 2 * tc), lambda j, i: (0, j))],
        out_shape=[jax.ShapeDtypeStruct((S, F2), BF16), jax.ShapeDtypeStruct((8, F2), F32)],
        compiler_params=_params("parallel", "arbitrary"), name=name)(a, a, a, dact, dact, cw, cb)


def _gelu_parts(z):
    z2 = z * z
    t = jnp.tanh(GELU_C0 * (z + GELU_C1 * z * z2))
    val = 0.5 * z * (1.0 + t)
    grad = 0.5 * (1.0 + t) + 0.5 * z * (1.0 - t * t) * GELU_C0 * (1.0 + 3.0 * GELU_C1 * z2)
    return val, grad


def _sgu_fwd(pre, b_in, vgain, vbias, wm, bsb, *, name, ts=256):
    S, W2 = pre.shape
    W = W2 // 2
    gd = W // SGU_GROUPS
    ts = _tile(S, ts)

    def body(p_ref, b_ref, vg_ref, vb_ref, wm_ref, bs_ref, y_ref):
        u = _gelu_parts(p_ref[:, pl.ds(0, W)].astype(F32) + b_ref[:, pl.ds(0, W)])[0]
        v = _gelu_parts(p_ref[:, pl.ds(W, W)].astype(F32) + b_ref[:, pl.ds(W, W)])[0]
        mu = jnp.mean(v, axis=-1, keepdims=True)
        vc = v - mu
        rstd = lax.rsqrt(jnp.mean(vc * vc, axis=-1, keepdims=True) + EPS)
        vn = ((vc * rstd) * vg_ref[...] + vb_ref[...]).astype(BF16)
        for blk in range(ts // SGU_BLOCK):
            r0 = blk * SGU_BLOCK
            for g in range(SGU_GROUPS):
                c0 = g * gd
                mixed = jnp.dot(wm_ref[g], vn[r0:r0 + SGU_BLOCK, c0:c0 + gd], preferred_element_type=F32) + bs_ref[g]
                y_ref[pl.ds(r0, SGU_BLOCK), pl.ds(c0, gd)] = (u[r0:r0 + SGU_BLOCK, c0:c0 + gd] * mixed).astype(BF16)

    full = lambda shape: pl.BlockSpec(shape, lambda i: (0,) * len(shape))
    return pl.pallas_call(
        body, grid=(S // ts,),
        in_specs=[pl.BlockSpec((ts, W2), lambda i: (i, 0)), full((1, W2)), full((1, W)), full((1, W)),
                  full((SGU_GROUPS, SGU_BLOCK, SGU_BLOCK)), full((SGU_GROUPS, SGU_BLOCK, gd))],
        out_specs=pl.BlockSpec((ts, W), lambda i: (i, 0)), out_shape=jax.ShapeDtypeStruct((S, W), BF16),
        compiler_params=_params("parallel"), name=name)(pre, b_in, vgain, vbias, wm, bsb)


def _sgu_bwd(pre, dy, b_in, vgain, vbias, wm, wmt, bsb, *, name, ts=256):
    S, W2 = pre.shape
    W = W2 // 2
    gd = W // SGU_GROUPS
    ts = _tile(S, ts)
    last = S // ts - 1

    def body(p_ref, dy_ref, b_ref, vg_ref, vb_ref, wm_ref, wmt_ref, bs_ref,
             dp_ref, db_ref, dvg_ref, dvb_ref, dws_ref, dbs_ref, du_s, dvn_s, dbs_s):
        step = pl.program_id(0)
        _acc_init(step, db_ref, dvg_ref, dvb_ref, dws_ref, dbs_s)
        u, gu = _gelu_parts(p_ref[:, pl.ds(0, W)].astype(F32) + b_ref[:, pl.ds(0, W)])
        v, gv = _gelu_parts(p_ref[:, pl.ds(W, W)].astype(F32) + b_ref[:, pl.ds(W, W)])
        mu = jnp.mean(v, axis=-1, keepdims=True)
        vc = v - mu
        rstd = lax.rsqrt(jnp.mean(vc * vc, axis=-1, keepdims=True) + EPS)
        vhat = vc * rstd
        vn = (vhat * vg_ref[...] + vb_ref[...]).astype(BF16)
        dyv = dy_ref[...].astype(F32)
        for blk in range(ts // SGU_BLOCK):
            r0 = blk * SGU_BLOCK
            for g in range(SGU_GROUPS):
                c0 = g * gd
                vn_g = vn[r0:r0 + SGU_BLOCK, c0:c0 + gd]
                dy_g = dyv[r0:r0 + SGU_BLOCK, c0:c0 + gd]
                mixed = jnp.dot(wm_ref[g], vn_g, preferred_element_type=F32) + bs_ref[g]
                dmix = dy_g * u[r0:r0 + SGU_BLOCK, c0:c0 + gd]
                dmix_b = dmix.astype(BF16)
                du_s[pl.ds(r0, SGU_BLOCK), pl.ds(c0, gd)] = dy_g * mixed
                dvn_s[pl.ds(r0, SGU_BLOCK), pl.ds(c0, gd)] = jnp.dot(wmt_ref[g], dmix_b, preferred_element_type=F32)
                dws_ref[g] += lax.dot_general(dmix_b, vn_g, _NT, preferred_element_type=F32)
                dbs_s[g] += dmix
        dvn = dvn_s[...]
        dvg_ref[...] += _colsum(dvn * vhat)
        dvb_ref[...] += _colsum(dvn)
        dvh = dvn * vg_ref[...]
        dv = rstd * (dvh - jnp.mean(dvh, axis=-1, keepdims=True) - vhat * jnp.mean(dvh * vhat, axis=-1, keepdims=True))
        dpu = du_s[...] * gu
        dpv = dv * gv
        dp_ref[:, pl.ds(0, W)] = dpu.astype(BF16)
        dp_ref[:, pl.ds(W, W)] = dpv.astype(BF16)
        db_ref[:, pl.ds(0, W)] += _colsum(dpu)
        db_ref[:, pl.ds(W, W)] += _colsum(dpv)

        @pl.when(step == last)
        def _():
            for g in range(SGU_GROUPS):
                dbs_ref[g] = jnp.broadcast_to(jnp.sum(dbs_s[g], axis=-1, keepdims=True), (SGU_BLOCK, SGU_BLOCK))

    full = lambda shape: pl.BlockSpec(shape, lambda i: (0,) * len(shape))
    gsq = (SGU_GROUPS, SGU_BLOCK, SGU_BLOCK)
    return pl.pallas_call(
        body, grid=(S // ts,),
        in_specs=[pl.BlockSpec((ts, W2), lambda i: (i, 0)), pl.BlockSpec((ts, W), lambda i: (i, 0)),
                  full((1, W2)), full((1, W)), full((1, W)), full(gsq), full(gsq), full((SGU_GROUPS, SGU_BLOCK, gd))],
        out_specs=[pl.BlockSpec((ts, W2), lambda i: (i, 0)), full((1, W2)), full((1, W)), full((1, W)), full(gsq), full(gsq)],
        out_shape=[jax.ShapeDtypeStruct((S, W2), BF16), jax.ShapeDtypeStruct((1, W2), F32),
                   jax.ShapeDtypeStruct((1, W), F32), jax.ShapeDtypeStruct((1, W), F32),
                   jax.ShapeDtypeStruct(gsq, F32), jax.ShapeDtypeStruct(gsq, F32)],
        scratch_shapes=[pltpu.VMEM((ts, W), F32), pltpu.VMEM((ts, W), F32), pltpu.VMEM((SGU_GROUPS, SGU_BLOCK, gd), F32)],
        compiler_params=_params("arbitrary"), name=name)(pre, dy, b_in, vgain, vbias, wm, wmt, bsb)


def _ffn_fwd(x, mods, n2g, w_up, cw, cb, w_down, tag):
    sh, sc, gate = mods
    h = _norm_mod_fwd(x, n2g, sh, sc, name=f"{tag}_norm_fwd")
    a = _mm(h, w_up, out_dtype=BF16, tn=1408, name=f"{tag}_up")
    act = _conv_gate_fwd(a, cw, cb, name=f"{tag}_conv_fwd")
    x_out, y = _mm(act, w_down, tk=1408, res=(x, gate), name=f"{tag}_down")
    return x_out, (x, h, a, act, y)


def _ffn_bwd(dy, saved, mods, n2g, w_up, cw, cb, w_down, dres, prev, tag):
    x, h, a, act, _ = saved
    sh, sc, gate = mods
    dact = _mm(dy, w_down, tb=True, out_dtype=BF16, tn=1408, name=f"{tag}_down_dx")
    dw_down = _mm(act, dy, ta=True, out_dtype=BF16, tm=1408, name=f"{tag}_down_dw")
    da, sums = _conv_gate_bwd(a, dact, cw, cb, name=f"{tag}_conv_bwd")
    dh = _mm(da, w_up, tb=True, tk=1408, name=f"{tag}_up_dx")
    dw_up = _mm(h, da, ta=True, out_dtype=BF16, tn=1408, name=f"{tag}_up_dw")
    outs = _norm_mod_bwd(dh, x, n2g, sc, dres, prev, name=f"{tag}_norm_bwd")
    sums = _unpair_tiles(sums)
    return outs, dict(w_up=_unpair_tiles(dw_up), w_down=dw_down, conv_w=sums[0:CONV_WIDTH], conv_b=sums[CONV_WIDTH])


def _local_step(x, target, w, mods, late=None, early=None):
    S, D = x.shape
    lane = jnp.arange(LANES)
    gmat = jnp.where((lane[:, None] // HEAD_DIM) == (lane[None, :] // HEAD_DIM), 1.0 / HEAD_DIM, 0.0).astype(BF16)
    qg2 = jnp.tile(w["fox_q_gain"].reshape(1, HEAD_DIM), (1, 2))
    kg2 = jnp.tile(w["fox_k_gain"].reshape(1, HEAD_DIM), (1, 2))
    bf_pad = jnp.pad(w["fox_b_f"].reshape(1, HEADS), ((0, 0), (0, LANES - HEADS)))
    w_in_pad = jnp.pad(w["fox_w_in"], ((0, 0), (0, 4 * D + LANES - w["fox_w_in"].shape[1])))
    w_qkvo, w_f = w_in_pad[:, :4 * D], w_in_pad[:, 4 * D:]
    tpos = jnp.arange(SGU_BLOCK)
    smask = (tpos[None, :] // SGU_CHUNK) <= (tpos[:, None] // SGU_CHUNK)
    wm32 = jnp.where(smask[None], w["sgu_w_s"], 0.0)
    wm, wmt = wm32.astype(BF16), jnp.swapaxes(wm32, 1, 2).astype(BF16)
    gd = w["sgu_v_gain"].shape[-1] // SGU_GROUPS
    bsb = jnp.broadcast_to(w["sgu_b_s"][:, :, None], (SGU_GROUPS, SGU_BLOCK, gd))
    vec = lambda v: v.reshape(1, -1)

    sh1, sc1, g1 = mods[0][0:3]
    h0 = _norm_mod_fwd(x, vec(w["norm1_g"][0]), sh1, sc1, name="fox_norm_fwd")
    proj = _mm(h0, w_qkvo, out_dtype=BF16, name="fox_proj")
    fl = _mm(h0, w_f, name="fox_forget_proj")
    fcum = _fox_decay_fwd(fl, bf_pad, name="fox_decay")
    q_aug, k_aug, v_aug = _fox_prep_fwd(proj, fcum, qg2, kg2, gmat, name="fox_qk_norm")
    logit_bound = 8.0 * jnp.max(jnp.abs(w["fox_q_gain"])) * jnp.max(jnp.abs(w["fox_k_gain"]))
    q_max = lax.cond(logit_bound <= SHIFT_FREE_LOGIT_BOUND, lambda: q_aug,
                     lambda: _attn_rowmax(q_aug, k_aug, name="fox_attn_rowmax"))
    xchg = None if late is None else (late[0], [False] * len(late[0]))
    att_aug, q_lse, gathered = _attn_fwd(q_max, k_aug, v_aug, xchg, name="fox_attn_fwd")
    if late is not None:
        w = {**w, **late[1](gathered)}
    w = dict(w, ffn_w_up=_pair_tiles(w["ffn_w_up"]), ffn_conv_w=_pair_tiles(w["ffn_conv_w"]),
             ffn_conv_b=_pair_tiles(w["ffn_conv_b"]))
    att, ag = _fox_gate_fwd(att_aug, proj, name="fox_gate_fwd")
    x1, y_fox = _mm(ag, w["fox_w_out"], res=(x, g1), name="fox_out")
    x2, ffn0 = _ffn_fwd(x1, mods[0][3:6], vec(w["norm2_g"][0]), w["ffn_w_up"][0], w["ffn_conv_w"][0],
                        vec(w["ffn_conv_b"][0]), w["ffn_w_down"][0], "ffn0")

    sh1b, sc1b, g1b = mods[1][0:3]
    h1 = _norm_mod_fwd(x2, vec(w["norm1_g"][1]), sh1b, sc1b, name="sgu_norm_fwd")
    pre = _mm(h1, w["sgu_w_in"], out_dtype=BF16, name="sgu_in")
    b_in, vg, vb = vec(w["sgu_b_in"]), vec(w["sgu_v_gain"]), vec(w["sgu_v_bias"])
    ys = _sgu_fwd(pre, b_in, vg, vb, wm, bsb, name="sgu_core_fwd")
    x3, y_sgu = _mm(ys, w["sgu_w_out"], res=(x2, g1b), name="sgu_out")
    x4, ffn1 = _ffn_fwd(x3, mods[1][3:6], vec(w["norm2_g"][1]), w["ffn_w_up"][1], w["ffn_conv_w"][1],
                        vec(w["ffn_conv_b"][1]), w["ffn_w_down"][1], "ffn1")

    loss, d_final_g, dx4, dy_ffn1, dgate_ffn1 = _final_loss(x4, vec(w["final_g"]), target, ffn1[4], mods[1][5], name="final_loss")

    (dx3, dn2g_1, dsh2_1, dsc2_1, dy_sgu, dgate_sgu), g_ffn1 = _ffn_bwd(
        dy_ffn1, ffn1, mods[1][3:6], vec(w["norm2_g"][1]), w["ffn_w_up"][1], w["ffn_conv_w"][1], vec(w["ffn_conv_b"][1]),
        w["ffn_w_down"][1], dx4, (y_sgu, g1b), "ffn1")

    dys = _mm(dy_sgu, w["sgu_w_out"], tb=True, out_dtype=BF16, name="sgu_out_dx")
    dw_sgu_out = _mm(ys, dy_sgu, ta=True, out_dtype=BF16, name="sgu_out_dw")
    dpre, db_in, dvg, dvb, dws, dbs = _sgu_bwd(pre, dys, b_in, vg, vb, wm, wmt, bsb, name="sgu_core_bwd")
    dh1 = _mm(dpre, w["sgu_w_in"], tb=True, name="sgu_in_dx")
    dw_sgu_in = _mm(h1, dpre, ta=True, out_dtype=BF16, name="sgu_in_dw")
    dx2, dn1g_1, dsh1_1, dsc1_1, dy_ffn0, dgate_ffn0 = _norm_mod_bwd(
        dh1, x2, vec(w["norm1_g"][1]), sc1b, dx3, (ffn0[4], mods[0][5]), name="sgu_norm_bwd")

    (dx1, dn2g_0, dsh2_0, dsc2_0, dy_fox, dgate_fox), g_ffn0 = _ffn_bwd(
        dy_ffn0, ffn0, mods[0][3:6], vec(w["norm2_g"][0]), w["ffn_w_up"][0], w["ffn_conv_w"][0], vec(w["ffn_conv_b"][0]),
        w["ffn_w_down"][0], dx2, (y_fox, g1), "ffn0")

    dag = _mm(dy_fox, w["fox_w_out"], tb=True, out_dtype=BF16, name="fox_out_dx")
    dw_fox_out = _mm(ag, dy_fox, ta=True, out_dtype=BF16, name="fox_out_dw")
    do_aug, dog = _fox_gate_bwd(dag, att, proj, name="fox_gate_bwd")
    grads = dict(
        sgu_w_in=dw_sgu_in, sgu_b_in=db_in[0], sgu_v_gain=dvg[0], sgu_v_bias=dvb[0],
        sgu_w_s=jnp.where(smask[None], dws, 0.0), sgu_b_s=dbs[:, :, 0], sgu_w_out=dw_sgu_out,
        ffn_w_up=jnp.stack([g_ffn0["w_up"], g_ffn1["w_up"]]),
        ffn_conv_w=jnp.stack([g_ffn0["conv_w"], g_ffn1["conv_w"]]),
        ffn_conv_b=jnp.stack([g_ffn0["conv_b"], g_ffn1["conv_b"]]),
        ffn_w_down=jnp.stack([g_ffn0["w_down"], g_ffn1["w_down"]]),
        final_g=d_final_g[0], fox_w_out=dw_fox_out,
    )
    xchg = None
    if early is not None:
        blocks = early(grads)
        xchg = (blocks, [True] * len(blocks))
    dq_aug, dk_aug, dv_aug, exchanged = _attn_bwd(q_lse, k_aug, v_aug, do_aug, xchg, name="fox_attn_bwd")
    dproj, dF, dqg, dkg = _fox_prep_bwd(proj, dq_aug, dk_aug, dv_aug, dog, qg2, kg2, gmat, name="fox_qk_norm_bwd")
    dproj, dbf = _fox_decay_bwd(dF, fl, bf_pad, dproj, name="fox_decay_bwd")
    dh0 = _mm(dproj, w_in_pad, tb=True, tk=1408, name="fox_proj_dx")
    dw_fox_in = _mm(h0, dproj, ta=True, out_dtype=BF16, tn=1408, name="fox_proj_dw")
    dx0, dn1g_0, dsh1_0, dsc1_0 = _norm_mod_bwd(dh0, x, vec(w["norm1_g"][0]), sc1, dx1, None, name="fox_norm_bwd")

    dmod0 = jnp.concatenate([dsh1_0, dsc1_0, dgate_fox, dsh2_0, dsc2_0, dgate_ffn0], axis=1)
    dmod1 = jnp.concatenate([dsh1_1, dsc1_1, dgate_sgu, dsh2_1, dsc2_1, dgate_ffn1], axis=1)
    grads.update(
        fox_w_in=dw_fox_in[:, :w["fox_w_in"].shape[1]],
        fox_b_f=dbf[0, :HEADS],
        fox_q_gain=dqg[0, :HEAD_DIM] + dqg[0, HEAD_DIM:],
        fox_k_gain=dkg[0, :HEAD_DIM] + dkg[0, HEAD_DIM:],
        fox_w_out=dw_fox_out,
        ada_b=jnp.concatenate([dmod0, dmod1], axis=0),
        norm1_g=jnp.concatenate([dn1g_0, dn1g_1], axis=0), norm2_g=jnp.concatenate([dn2g_0, dn2g_1], axis=0),
    )
    return loss[0, 0], dx0, grads, exchanged


_HBM = pl.BlockSpec(memory_space=pl.ANY)
N_PEER = N_DEV - 1


def _xchg_out_shapes(arrs, scatter):
    return [jax.ShapeDtypeStruct(a.shape if s else (N_DEV,) + a.shape, a.dtype) for a, s in zip(arrs, scatter)]


def _xchg_sems(n):
    return [pltpu.SemaphoreType.DMA((n * N_PEER,)), pltpu.SemaphoreType.DMA((n * N_PEER,)), pltpu.SemaphoreType.DMA((n,))]


def _xchg_copies(ins, outs, scatter, send, recv, loc):
    x, y, c = lax.axis_index("x"), lax.axis_index("y"), lax.axis_index("c")
    me = 4 * x + 2 * y + c
    copies = []
    for a in range(len(ins)):
        copies.append(pltpu.make_async_copy(ins[a].at[me] if scatter[a] else ins[a], outs[a].at[me], loc.at[a]))
        for k in range(1, N_DEV):
            px = 1 - x if k & 4 else x
            py = 1 - y if k & 2 else y
            pc = 1 - c if k & 1 else c
            copies.append(pltpu.make_async_remote_copy(
                src_ref=ins[a].at[4 * px + 2 * py + pc] if scatter[a] else ins[a], dst_ref=outs[a].at[me],
                send_sem=send.at[a * N_PEER + k - 1], recv_sem=recv.at[a * N_PEER + k - 1],
                device_id=(px, py, pc), device_id_type=MESH))
    return copies


def _exchange(arrs, scatter, *, name):
    n = len(arrs)

    def body(*refs):
        copies = _xchg_copies(refs[:n], refs[n:2 * n], scatter, *refs[2 * n:])
        for cp in copies:
            cp.start()
        for cp in copies:
            cp.wait()

    return pl.pallas_call(
        body, in_specs=[_HBM] * n, out_specs=[_HBM] * n, out_shape=_xchg_out_shapes(arrs, scatter),
        scratch_shapes=_xchg_sems(n),
        compiler_params=pltpu.CompilerParams(has_side_effects=True), name=name)(*arrs)


def _adamw(w, parts, m, v, *, name, tr=256):
    R, C = w.shape
    P = parts.shape[0]
    tr = next(t for t in range(min(R, tr), 0, -1) if R % t == 0 and (t % 16 == 0 or t == R))
    c1 = 1.0 - ADAM_B1 ** ADAM_STEP
    c2 = 1.0 - ADAM_B2 ** ADAM_STEP

    def body(w_ref, p_ref, m_ref, v_ref, g_ref, d_ref, mo_ref, vo_ref):
        g = p_ref[0].astype(F32)
        for p in range(1, P):
            g = g + p_ref[p].astype(F32)
        mn = ADAM_B1 * m_ref[...] + (1.0 - ADAM_B1) * g
        vn = ADAM_B2 * v_ref[...] + (1.0 - ADAM_B2) * (g * g)
        g_ref[...] = g
        mo_ref[...] = mn
        vo_ref[...] = vn
        d_ref[...] = -ADAM_LR * ((mn / c1) / (jnp.sqrt(vn / c2) + ADAM_EPS) + ADAM_WD * w_ref[...])

    row = pl.BlockSpec((tr, C), lambda i: (i, 0))
    return pl.pallas_call(
        body, grid=(R // tr,), in_specs=[row, pl.BlockSpec((P, tr, C), lambda i: (0, i, 0)), row, row],
        out_specs=[row] * 4, out_shape=[jax.ShapeDtypeStruct((R, C), F32)] * 4,
        compiler_params=_params("parallel"), name=name)(w, parts, m, v)


def _sum_parts(parts, *, name):
    P, R, C = parts.shape

    def body(p_ref, o_ref):
        g = p_ref[0]
        for p in range(1, P):
            g = g + p_ref[p]
        o_ref[...] = g

    return pl.pallas_call(body, out_shape=jax.ShapeDtypeStruct((R, C), F32), name=name)(parts)


WEIGHTS = ["fox_w_in", "fox_b_f", "fox_q_gain", "fox_k_gain", "fox_w_out", "sgu_w_in", "sgu_b_in", "sgu_v_gain",
           "sgu_v_bias", "sgu_w_s", "sgu_b_s", "sgu_w_out", "ffn_w_up", "ffn_conv_w", "ffn_conv_b", "ffn_w_down",
           "ada_w", "ada_b", "norm1_g", "norm2_g", "final_g"]
BIG_AXIS = dict(fox_w_in=1, fox_w_out=0, sgu_w_in=1, sgu_w_out=0, ffn_w_up=1, ffn_w_down=0, ada_w=1)
SMALL_SHARDED = ["sgu_b_in", "sgu_v_gain", "sgu_v_bias", "ffn_conv_w"]
SINGLE_LAYER = ("fox_", "sgu_")
BEFORE_ATTENTION = ["fox_w_in"]
AFTER_ATTENTION = ["fox_w_out", "sgu_w_in", "sgu_w_out", "ffn_w_up", "ffn_w_down"]


def _assemble(stacked, layers, axis):
    _, lr, cc = stacked.shape
    r = lr // layers
    s4 = stacked.reshape(N_DEV, layers, r, cc)
    if axis == 0:
        return s4.transpose(1, 0, 2, 3).reshape(layers, N_DEV * r, cc)
    return s4.transpose(1, 2, 0, 3).reshape(layers, r, N_DEV * cc)


def _disassemble(full, axis):
    layers, R, C = full.shape
    if axis == 0:
        r = R // N_DEV
        return full.reshape(layers, N_DEV, r, C).transpose(1, 0, 2, 3).reshape(N_DEV, layers * r, C)
    cc = C // N_DEV
    return full.reshape(layers, R, N_DEV, cc).transpose(2, 0, 1, 3).reshape(N_DEV, layers * R, cc)


def kernel(x, c, fox_w_in, fox_b_f, fox_q_gain, fox_k_gain, fox_w_out, sgu_w_in, sgu_b_in, sgu_v_gain, sgu_v_bias, sgu_w_s, sgu_b_s, sgu_w_out, ffn_w_up, ffn_conv_w, ffn_conv_b, ffn_w_down, ada_w, ada_b, norm1_g, norm2_g, final_g, loss_target, m_fox_w_in, m_fox_b_f, m_fox_q_gain, m_fox_k_gain, m_fox_w_out, m_sgu_w_in, m_sgu_b_in, m_sgu_v_gain, m_sgu_v_bias, m_sgu_w_s, m_sgu_b_s, m_sgu_w_out, m_ffn_w_up, m_ffn_conv_w, m_ffn_conv_b, m_ffn_w_down, m_ada_w, m_ada_b, m_norm1_g, m_norm2_g, m_final_g, v_fox_w_in, v_fox_b_f, v_fox_q_gain, v_fox_k_gain, v_fox_w_out, v_sgu_w_in, v_sgu_b_in, v_sgu_v_gain, v_sgu_v_bias, v_sgu_w_s, v_sgu_b_s, v_sgu_w_out, v_ffn_w_up, v_ffn_conv_w, v_ffn_conv_b, v_ffn_w_down, v_ada_w, v_ada_b, v_norm1_g, v_norm2_g, v_final_g):
    args = dict(locals())
    wts = {n: args[n] for n in WEIGHTS}
    ms = {n: args["m_" + n] for n in WEIGHTS}
    vs = {n: args["v_" + n] for n in WEIGHTS}
    me = 4 * lax.axis_index("x") + 2 * lax.axis_index("y") + lax.axis_index("c")

    shard2d = lambda n: wts[n].astype(BF16).reshape(-1, wts[n].shape[-1])

    def assemble_big(names, got):
        out = {}
        for n, g in zip(names, got):
            f = _assemble(g, wts[n].shape[0], BIG_AXIS[n])
            out[n] = f[0] if n.startswith(SINGLE_LAYER) else f
        return out

    def blocks_of(names, grads):
        return [_disassemble(grads[n] if grads[n].ndim == 3 else grads[n][None], BIG_AXIS[n]) for n in names]

    send = [c] + [shard2d(n) for n in BEFORE_ATTENTION] + [wts[n].reshape(-1, wts[n].shape[-1]) for n in SMALL_SHARDED]
    got = _exchange(send, [False] * len(send), name="gather_first")
    c_all = got[0].reshape(N_DEV, -1)
    full = assemble_big(BEFORE_ATTENTION, got[1:1 + len(BEFORE_ATTENTION)])
    for n, g in zip(SMALL_SHARDED, got[1 + len(BEFORE_ATTENTION):]):
        lead = wts[n].shape[:-1]
        f = jnp.moveaxis(g.reshape((N_DEV,) + wts[n].shape), 0, -2).reshape(lead + (-1,))
        full[n] = f[0] if n.startswith(SINGLE_LAYER) else f
    for n in WEIGHTS:
        if n not in full and n not in BIG_AXIS:
            full[n] = wts[n][0] if n.startswith(SINGLE_LAYER) else wts[n]

    ada_cols = wts["ada_w"].shape[-1]
    mod_rows = []
    for i in range(2):
        b_mine = lax.dynamic_slice_in_dim(wts["ada_b"][i], me * ada_cols, ada_cols).reshape(1, ada_cols)
        m, c_act = _ada_mod(c_all, wts["ada_w"][i].astype(BF16), b_mine, name=f"ada_mod_{i}")
        mod_rows.append(m)
    got = _exchange([jnp.concatenate(mod_rows, axis=1)[:, None, :]], [True], name="exchange_mods")[0]
    d_model = x.shape[-1]
    mods = []
    for i in range(2):
        mod = got[:, 0, i * ada_cols:(i + 1) * ada_cols].reshape(1, N_DEV * ada_cols)
        mods.append([mod[:, k * d_model:(k + 1) * d_model] for k in range(6)])

    late = ([shard2d(n) for n in AFTER_ATTENTION], lambda g: assemble_big(AFTER_ATTENTION, g))
    loss, grad_x, grads, got_late = _local_step(x[0], loss_target[0], full, mods, late,
                                                lambda gr: blocks_of(AFTER_ATTENTION, gr))

    small = [n for n in WEIGHTS if n not in BIG_AXIS]
    flat = jnp.concatenate([loss.reshape(1)] + [grads[n].reshape(-1).astype(F32) for n in small])
    n_flat = flat.shape[0]
    rows = -(-n_flat // (8 * LANES)) * 8
    flat = jnp.pad(flat, (0, rows * LANES - n_flat)).reshape(rows, LANES)
    got = _exchange(blocks_of(BEFORE_ATTENTION, grads) + [flat], [True] * len(BEFORE_ATTENTION) + [False], name="exchange_last")
    flat_all = got[-1]
    total = _sum_parts(flat_all, name="sum_small_grads").reshape(-1)
    loss_out = total[0]

    off_ada = 1 + sum(math.prod(grads[n].shape) for n in small[:small.index("ada_b")])
    dmod_all = flat_all.reshape(N_DEV, -1)[:, off_ada:off_ada + 2 * N_DEV * ada_cols].reshape(N_DEV, 2, N_DEV * ada_cols)
    dmod_mine = lax.dynamic_slice_in_dim(dmod_all, me * ada_cols, ada_cols, axis=2)
    d_ada = [_mm(c_act, jnp.pad(dmod_mine[:, i], ((0, c_act.shape[0] - N_DEV), (0, 0))).astype(BF16), ta=True,
                 name=f"ada_dw_{i}") for i in range(2)]

    out_g, out_d, out_m, out_v = {}, {}, {}, {}
    summands = dict(zip(BEFORE_ATTENTION, got))
    summands.update(zip(AFTER_ATTENTION, got_late))
    summands["ada_w"] = jnp.concatenate(d_ada, axis=0)[None]
    for n, p in summands.items():
        shp = wts[n].shape
        two_d = lambda a: a.reshape(-1, shp[-1])
        g, d, mn, vn = _adamw(two_d(wts[n]), p, two_d(ms[n]), two_d(vs[n]), name=f"adamw_{n}")
        out_g[n], out_d[n], out_m[n], out_v[n] = (a.reshape(shp) for a in (g, d, mn, vn))
    off = 1
    small_g = {}
    for n in small:
        full_shape = grads[n].shape
        size = math.prod(full_shape)
        g = total[off:off + size].reshape(full_shape)
        off += size
        if n in SMALL_SHARDED:
            blk = full_shape[-1] // N_DEV
            g = lax.dynamic_slice_in_dim(g, me * blk, blk, axis=g.ndim - 1)
        small_g[n] = g.reshape(wts[n].shape)
    cat = lambda d: jnp.concatenate([d[n].reshape(-1) for n in small])
    n_small = sum(math.prod(wts[n].shape) for n in small)
    rows2 = -(-n_small // (256 * LANES)) * 256
    pack = lambda d, fill: jnp.pad(cat(d), (0, rows2 * LANES - n_small), constant_values=fill).reshape(rows2, LANES)
    g, d, mn, vn = _adamw(pack(wts, 0.0), pack(small_g, 0.0)[None], pack(ms, 0.0), pack(vs, 1.0), name="adamw_small")
    off = 0
    for n in small:
        size = math.prod(wts[n].shape)
        for src, dst in ((g, out_g), (d, out_d), (mn, out_m), (vn, out_v)):
            dst[n] = src.reshape(-1)[off:off + size].reshape(wts[n].shape)
        off += size

    return (loss_out, grad_x[None], *[out_g[n] for n in WEIGHTS], *[out_d[n] for n in WEIGHTS],
            *[out_m[n] for n in WEIGHTS], *[out_v[n] for n in WEIGHTS])
```

```python
import functools
import math

import jax
import jax.numpy as jnp
from jax import lax
from jax.experimental import pallas as pl
from jax.experimental.pallas import tpu as pltpu

F32, BF16 = jnp.float32, jnp.bfloat16
N_DEV = 8
HEADS, HEAD_DIM = 16, 64
HEAD_PAIRS = HEADS // 2
LANES = 128
EPS = 1e-6
SGU_BLOCK, SGU_GROUPS, SGU_CHUNK = 128, 8, 64
CONV_WIDTH = 3
ADAM_LR, ADAM_B1, ADAM_B2, ADAM_EPS, ADAM_WD, ADAM_STEP = 0.001, 0.9, 0.999, 1e-08, 0.01, 10
NEG = -1e30
GELU_C0, GELU_C1 = math.sqrt(2.0 / math.pi), 0.044715
MESH = pl.DeviceIdType.MESH
VMEM_LIMIT = 56 * 1024 * 1024


def _tile(dim, pref):
    if dim <= pref:
        return dim
    t = (pref // LANES) * LANES
    while t >= LANES:
        if dim % t == 0:
            return t
        t -= LANES
    return dim


def _params(*sem):
    return pltpu.CompilerParams(dimension_semantics=sem, vmem_limit_bytes=VMEM_LIMIT)


def _mm(a, b, *, name, ta=False, tb=False, out_dtype=F32, tm=1024, tn=1024, tk=1024, res=None, b_n=None, b_k=None, o_n=None):
    M = a.shape[1] if ta else a.shape[0]
    K = a.shape[0] if ta else a.shape[1]
    N = b.shape[0] if tb else b.shape[1]
    tm, tn, tk = _tile(M, tm), _tile(N, tn), _tile(K, tk)
    nk = K // tk
    dims = (((0 if ta else 1,), (1 if tb else 0,)), ((), ()))
    same = lambda idx: idx
    b_n, b_k, o_n = b_n or same, b_k or same, o_n or same
    a_spec = pl.BlockSpec((tk, tm), lambda i, j, k: (k, i)) if ta else pl.BlockSpec((tm, tk), lambda i, j, k: (i, k))
    b_spec = (pl.BlockSpec((tn, tk), lambda i, j, k: (b_n(j), b_k(k))) if tb
              else pl.BlockSpec((tk, tn), lambda i, j, k: (b_k(k), b_n(j))))
    o_spec = pl.BlockSpec((tm, tn), lambda i, j, k: (i, o_n(j)))

    def accumulate(a_ref, b_ref, acc):
        @pl.when(pl.program_id(2) == 0)
        def _():
            acc[...] = jnp.zeros_like(acc)
        acc[...] += lax.dot_general(a_ref[...], b_ref[...], dims, preferred_element_type=F32)

    if res is None:
        def body(a_ref, b_ref, o_ref, acc):
            accumulate(a_ref, b_ref, acc)

            @pl.when(pl.program_id(2) == nk - 1)
            def _():
                o_ref[...] = acc[...].astype(o_ref.dtype)

        return pl.pallas_call(
            body, grid=(M // tm, N // tn, nk), in_specs=[a_spec, b_spec], out_specs=o_spec,
            out_shape=jax.ShapeDtypeStruct((M, N), out_dtype), scratch_shapes=[pltpu.VMEM((tm, tn), F32)],
            compiler_params=_params("parallel", "parallel", "arbitrary"), name=name)(a, b)

    x, gate = res

    def body_res(a_ref, b_ref, x_ref, g_ref, o_ref, y_ref, acc):
        accumulate(a_ref, b_ref, acc)

        @pl.when(pl.program_id(2) == nk - 1)
        def _():
            y = acc[...]
            o_ref[...] = x_ref[...] + g_ref[...] * y
            y_ref[...] = y.astype(BF16)

    return pl.pallas_call(
        body_res, grid=(M // tm, N // tn, nk),
        in_specs=[a_spec, b_spec, o_spec, pl.BlockSpec((1, tn), lambda i, j, k: (0, j))],
        out_specs=[o_spec, o_spec],
        out_shape=[jax.ShapeDtypeStruct((M, N), F32), jax.ShapeDtypeStruct((M, N), BF16)],
        scratch_shapes=[pltpu.VMEM((tm, tn), F32)],
        compiler_params=_params("parallel", "parallel", "arbitrary"), name=name)(a, b, x, gate)


def _ada_mod(c_rows, w, b, *, name):
    R, D = c_rows.shape
    N = w.shape[1]
    tn = _tile(N, 1536)
    rows = 16
    c_pad = jnp.pad(c_rows, ((0, rows - R), (0, 0)))

    def body(c_ref, w_ref, b_ref, o_ref, ca_ref):
        cv = c_ref[...]
        ca16 = (cv * jax.nn.sigmoid(cv)).astype(BF16)
        ca_ref[...] = ca16
        o_ref[...] = jnp.dot(ca16, w_ref[...], preferred_element_type=F32) + b_ref[...]

    out, ca = pl.pallas_call(
        body, grid=(N // tn,),
        in_specs=[pl.BlockSpec((rows, D), lambda j: (0, 0)), pl.BlockSpec((D, tn), lambda j: (0, j)),
                  pl.BlockSpec((1, tn), lambda j: (0, j))],
        out_specs=[pl.BlockSpec((rows, tn), lambda j: (0, j)), pl.BlockSpec((rows, D), lambda j: (0, 0))],
        out_shape=[jax.ShapeDtypeStruct((rows, N), F32), jax.ShapeDtypeStruct((rows, D), BF16)],
        compiler_params=_params("arbitrary"), name=name)(c_pad, w, b)
    return out[0:R], ca


def _norm_mod_fwd(x, g, shift, scale, *, name, ts=512):
    S, D = x.shape
    ts = _tile(S, ts)
    row = pl.BlockSpec((ts, D), lambda i: (i, 0))
    vec = pl.BlockSpec((1, D), lambda i: (0, 0))

    def body(x_ref, g_ref, sh_ref, sc_ref, h_ref):
        xv = x_ref[...]
        r = lax.rsqrt(jnp.mean(xv * xv, axis=-1, keepdims=True) + EPS)
        h_ref[...] = ((xv * r * g_ref[...]) * (1.0 + sc_ref[...]) + sh_ref[...]).astype(BF16)

    return pl.pallas_call(body, grid=(S // ts,), in_specs=[row, vec, vec, vec], out_specs=row,
                          out_shape=jax.ShapeDtypeStruct((S, D), BF16),
                          compiler_params=_params("parallel"), name=name)(x, g, shift, scale)


def _acc_init(step, *refs):
    @pl.when(step == 0)
    def _():
        for r in refs:
            r[...] = jnp.zeros_like(r)


def _colsum(v):
    return jnp.sum(v, axis=0, keepdims=True)


def _norm_mod_bwd(dh, x, g, scale, dres, prev=None, *, name, ts=512):
    S, D = x.shape
    ts = _tile(S, ts)
    row = pl.BlockSpec((ts, D), lambda i: (i, 0))
    vec = pl.BlockSpec((1, D), lambda i: (0, 0))
    has_prev = prev is not None

    def body(*refs):
        if has_prev:
            dh_ref, x_ref, g_ref, sc_ref, dres_ref, y_ref, gate_ref, dx_ref, dg_ref, dsh_ref, dsc_ref, dy_ref, dgate_ref = refs
            _acc_init(pl.program_id(0), dg_ref, dsh_ref, dsc_ref, dgate_ref)
        else:
            dh_ref, x_ref, g_ref, sc_ref, dres_ref, dx_ref, dg_ref, dsh_ref, dsc_ref = refs
            _acc_init(pl.program_id(0), dg_ref, dsh_ref, dsc_ref)
        xv, dhv, gv = x_ref[...], dh_ref[...], g_ref[...]
        r = lax.rsqrt(jnp.mean(xv * xv, axis=-1, keepdims=True) + EPS)
        xh = xv * r
        dsh_ref[...] += _colsum(dhv)
        dsc_ref[...] += _colsum(dhv * (xh * gv))
        dn = dhv * (1.0 + sc_ref[...])
        dg_ref[...] += _colsum(dn * xh)
        dxh = dn * gv
        dx = dres_ref[...] + r * (dxh - xh * jnp.mean(dxh * xh, axis=-1, keepdims=True))
        dx_ref[...] = dx
        if has_prev:
            dy_ref[...] = (gate_ref[...] * dx).astype(BF16)
            dgate_ref[...] += _colsum(dx * y_ref[...].astype(F32))

    ins, in_specs = [dh, x, g, scale, dres], [row, row, vec, vec, row]
    outs = [jax.ShapeDtypeStruct((S, D), F32)] + [jax.ShapeDtypeStruct((1, D), F32)] * 3
    out_specs = [row, vec, vec, vec]
    if has_prev:
        ins += list(prev)
        in_specs += [row, vec]
        outs += [jax.ShapeDtypeStruct((S, D), BF16), jax.ShapeDtypeStruct((1, D), F32)]
        out_specs += [row, vec]
    return pl.pallas_call(body, grid=(S // ts,), in_specs=in_specs, out_specs=out_specs, out_shape=outs,
                          compiler_params=_params("arbitrary"), name=name)(*ins)


def _final_loss(x, g, target, y, gate, *, name, ts=512):
    S, D = x.shape
    ts = _tile(S, ts)
    row = pl.BlockSpec((ts, D), lambda i: (i, 0))
    vec = pl.BlockSpec((1, D), lambda i: (0, 0))
    lvec = pl.BlockSpec((1, LANES), lambda i: (0, 0))

    def body(x_ref, g_ref, t_ref, y_ref, gate_ref, loss_ref, dg_ref, dx_ref, dy_ref, dgate_ref):
        _acc_init(pl.program_id(0), loss_ref, dg_ref, dgate_ref)
        xv, gv = x_ref[...], g_ref[...]
        r = lax.rsqrt(jnp.mean(xv * xv, axis=-1, keepdims=True) + EPS)
        xh = xv * r
        e = xh * gv - t_ref[...]
        loss_ref[...] += 0.5 * jnp.sum(jnp.mean(e * e, axis=-1, keepdims=True), axis=0, keepdims=True)
        dout = e * (1.0 / D)
        dg_ref[...] += _colsum(dout * xh)
        dxh = dout * gv
        dx = r * (dxh - xh * jnp.mean(dxh * xh, axis=-1, keepdims=True))
        dx_ref[...] = dx
        dy_ref[...] = (gate_ref[...] * dx).astype(BF16)
        dgate_ref[...] += _colsum(dx * y_ref[...].astype(F32))

    return pl.pallas_call(
        body, grid=(S // ts,), in_specs=[row, vec, row, row, vec], out_specs=[lvec, vec, row, row, vec],
        out_shape=[jax.ShapeDtypeStruct((1, LANES), F32), jax.ShapeDtypeStruct((1, D), F32),
                   jax.ShapeDtypeStruct((S, D), F32), jax.ShapeDtypeStruct((S, D), BF16),
                   jax.ShapeDtypeStruct((1, D), F32)],
        compiler_params=_params("arbitrary"), name=name)(x, g, target, y, gate)


def _head_mean(v, gmat):
    hi = v.astype(BF16)
    lo = (v - hi.astype(F32)).astype(BF16)
    return jnp.dot(hi, gmat, preferred_element_type=F32) + jnp.dot(lo, gmat, preferred_element_type=F32)


L_F, L_ONE, L_SHIFT = HEAD_DIM, HEAD_DIM + 3, HEAD_DIM + 6
SHIFT_FREE_LOGIT_BOUND = 60.0


def _lane():
    return lax.broadcasted_iota(jnp.int32, (1, LANES), 1)


def _split3(v):
    p1 = v.astype(BF16).astype(F32)
    r1 = v - p1
    p2 = r1.astype(BF16).astype(F32)
    p3 = (r1 - p2).astype(BF16).astype(F32)
    return p1, p2, p3


def _put3(lane, first, pieces):
    out = jnp.where(lane == first, pieces[0], 0.0)
    for k in (1, 2):
        out = out + jnp.where(lane == first + k, pieces[k], 0.0)
    return out


def _ones3(lane, first):
    return jnp.where((lane >= first) & (lane < first + 3), 1.0, 0.0)


def _lane_col(v, lane, idx):
    return jnp.sum(jnp.where(lane == idx, v, 0.0), axis=-1, keepdims=True)


def _head_of_pair(pair, e, lane):
    return jnp.where(lane < HEAD_DIM, pair if e == 0 else pltpu.roll(pair, HEAD_DIM, 1), 0.0)


def _pair_of_heads(even, odd, lane):
    return jnp.where(lane < HEAD_DIM, even, pltpu.roll(odd, HEAD_DIM, 1))


def _fox_prep_fwd(proj, fcum, qgain, kgain, gmat, *, name, ts=256):
    S = proj.shape[0]
    D = HEADS * HEAD_DIM
    ts = _tile(S, ts)
    scale = HEAD_DIM ** -0.5

    def body(p_ref, f_ref, qg_ref, kg_ref, gm_ref, q_ref, k_ref, v_ref):
        gm, lane, fc = gm_ref[...], _lane(), f_ref[...]
        for cpair in range(HEAD_PAIRS):
            qv = p_ref[:, pl.ds(cpair * LANES, LANES)].astype(F32)
            kv = p_ref[:, pl.ds(D + cpair * LANES, LANES)].astype(F32)
            vv = p_ref[:, pl.ds(2 * D + cpair * LANES, LANES)].astype(F32)
            qn = (qv * lax.rsqrt(_head_mean(qv * qv, gm) + EPS) * qg_ref[...]) * scale
            kn = kv * lax.rsqrt(_head_mean(kv * kv, gm) + EPS) * kg_ref[...]
            for e in range(2):
                h = 2 * cpair + e
                cols = pl.ds(h * LANES, LANES)
                f3 = _split3(_lane_col(fc, lane, h))
                q_ref[:, cols] = (_head_of_pair(qn, e, lane) + _put3(lane, L_F, f3) + _ones3(lane, L_ONE)).astype(BF16)
                k_ref[:, cols] = (_head_of_pair(kn, e, lane) + _ones3(lane, L_F)
                                  - _put3(lane, L_ONE, f3) + _ones3(lane, L_SHIFT)).astype(BF16)
                v_ref[:, cols] = (_head_of_pair(vv, e, lane) + _ones3(lane, L_F)).astype(BF16)

    vec = pl.BlockSpec((1, LANES), lambda i: (0, 0))
    wide = pl.BlockSpec((ts, HEADS * LANES), lambda i: (i, 0))
    return pl.pallas_call(
        body, grid=(S // ts,),
        in_specs=[pl.BlockSpec((ts, 3 * D), lambda i: (i, 0)), pl.BlockSpec((ts, LANES), lambda i: (i, 0)), vec, vec,
                  pl.BlockSpec((LANES, LANES), lambda i: (0, 0))],
        out_specs=[wide, wide, wide], out_shape=[jax.ShapeDtypeStruct((S, HEADS * LANES), BF16)] * 3,
        compiler_params=_params("parallel"), name=name)(proj, fcum, qgain, kgain, gmat)


def _fox_prep_bwd(proj, dq_aug, dk_aug, dv_aug, dog, qgain, kgain, gmat, *, name, ts=256):
    S = proj.shape[0]
    D = HEADS * HEAD_DIM
    ts = _tile(S, ts)
    scale = HEAD_DIM ** -0.5

    def body(p_ref, dq_ref, dk_ref, dv_ref, dog_ref, qg_ref, kg_ref, gm_ref, o_ref, df_ref, dqg_ref, dkg_ref):
        _acc_init(pl.program_id(0), dqg_ref, dkg_ref)
        gm, lane = gm_ref[...], _lane()
        df = jnp.zeros((ts, LANES), F32)
        for cpair in range(HEAD_PAIRS):
            tiles = []
            for e in range(2):
                h = 2 * cpair + e
                cols = pl.ds(h * LANES, LANES)
                tq, tk = dq_ref[:, cols], dk_ref[:, cols]
                df = jnp.where(lane == h, _lane_col(tq, lane, L_F) - _lane_col(tk, lane, L_ONE), df)
                tiles.append((tq, tk, dv_ref[:, cols].astype(F32)))
            pair = [_pair_of_heads(tiles[0][k], tiles[1][k], lane) for k in range(3)]
            for half, g_ref, dg_ref, mult in ((0, qg_ref, dqg_ref, scale), (1, kg_ref, dkg_ref, 1.0)):
                v = p_ref[:, pl.ds(half * D + cpair * LANES, LANES)].astype(F32)
                r = lax.rsqrt(_head_mean(v * v, gm) + EPS)
                xh = v * r
                dn = pair[half] * mult
                dg_ref[...] += _colsum(dn * xh)
                dxh = dn * g_ref[...]
                o_ref[:, pl.ds(half * D + cpair * LANES, LANES)] = (r * (dxh - xh * _head_mean(dxh * xh, gm))).astype(BF16)
            o_ref[:, pl.ds(2 * D + cpair * LANES, LANES)] = pair[2].astype(BF16)
        o_ref[:, pl.ds(3 * D, D)] = dog_ref[...]
        o_ref[:, pl.ds(4 * D, LANES)] = jnp.zeros((ts, LANES), BF16)
        df_ref[...] = df

    row = pl.BlockSpec((ts, D), lambda i: (i, 0))
    wide = pl.BlockSpec((ts, HEADS * LANES), lambda i: (i, 0))
    vec = pl.BlockSpec((1, LANES), lambda i: (0, 0))
    return pl.pallas_call(
        body, grid=(S // ts,),
        in_specs=[pl.BlockSpec((ts, 2 * D), lambda i: (i, 0)), wide, wide, wide, row, vec, vec,
                  pl.BlockSpec((LANES, LANES), lambda i: (0, 0))],
        out_specs=[pl.BlockSpec((ts, 4 * D + LANES), lambda i: (i, 0)), pl.BlockSpec((ts, LANES), lambda i: (i, 0)), vec, vec],
        out_shape=[jax.ShapeDtypeStruct((S, 4 * D + LANES), BF16), jax.ShapeDtypeStruct((S, LANES), F32),
                   jax.ShapeDtypeStruct((1, LANES), F32), jax.ShapeDtypeStruct((1, LANES), F32)],
        compiler_params=_params("arbitrary"), name=name)(proj, dq_aug, dk_aug, dv_aug, dog, qgain, kgain, gmat)


def _log_sigmoid(z):
    return jnp.minimum(z, 0.0) - jnp.log(1.0 + jnp.exp(-jnp.abs(z)))


def _fox_decay_fwd(fl, bf, *, name, tb=256):
    S = fl.shape[0]
    tb = _tile(S, tb)

    def body(fl_ref, b_ref, o_ref, carry):
        @pl.when(pl.program_id(0) == 0)
        def _():
            carry[...] = jnp.zeros_like(carry)
        logf = _log_sigmoid(fl_ref[...] + b_ref[...])
        tri = (lax.broadcasted_iota(jnp.int32, (tb, tb), 1) <= lax.broadcasted_iota(jnp.int32, (tb, tb), 0)).astype(F32)
        cs = jnp.dot(tri, logf, preferred_element_type=F32, precision=lax.Precision.HIGHEST) + carry[...]
        o_ref[...] = cs
        carry[...] = _row_of(cs, tb - 1)

    return pl.pallas_call(
        body, grid=(S // tb,),
        in_specs=[pl.BlockSpec((tb, LANES), lambda i: (i, 0)), pl.BlockSpec((1, LANES), lambda i: (0, 0))],
        out_specs=pl.BlockSpec((tb, LANES), lambda i: (i, 0)),
        out_shape=jax.ShapeDtypeStruct((S, LANES), F32), scratch_shapes=[pltpu.VMEM((1, LANES), F32)],
        compiler_params=_params("arbitrary"), name=name)(fl, bf)


def _fox_decay_bwd(dF, fl, bf, dproj, *, name, tb=256):
    S = fl.shape[0]
    tb = _tile(S, tb)
    n = S // tb
    last_col = dproj.shape[1] // LANES - 1

    def body(df_ref, fl_ref, b_ref, dproj_hbm, o_ref, db_ref, carry):
        del dproj_hbm
        @pl.when(pl.program_id(0) == 0)
        def _():
            carry[...] = jnp.zeros_like(carry)
            db_ref[...] = jnp.zeros_like(db_ref)
        tri = (lax.broadcasted_iota(jnp.int32, (tb, tb), 1) >= lax.broadcasted_iota(jnp.int32, (tb, tb), 0)).astype(F32)
        rc = jnp.dot(tri, df_ref[...], preferred_element_type=F32, precision=lax.Precision.HIGHEST) + carry[...]
        carry[...] = _row_of(rc, 0)
        dfl = rc * jax.nn.sigmoid(-(fl_ref[...] + b_ref[...]))
        o_ref[...] = dfl.astype(BF16)
        db_ref[...] += _colsum(dfl)

    rev = pl.BlockSpec((tb, LANES), lambda i: (n - 1 - i, 0))
    vec = pl.BlockSpec((1, LANES), lambda i: (0, 0))
    return pl.pallas_call(
        body, grid=(n,), in_specs=[rev, rev, vec, pl.BlockSpec(memory_space=pl.ANY)],
        out_specs=[pl.BlockSpec((tb, LANES), lambda i: (n - 1 - i, last_col)), vec],
        out_shape=[jax.ShapeDtypeStruct(dproj.shape, BF16), jax.ShapeDtypeStruct((1, LANES), F32)],
        scratch_shapes=[pltpu.VMEM((1, LANES), F32)], input_output_aliases={3: 0},
        compiler_params=_params("arbitrary"), name=name)(dF, fl, bf, dproj)


_NT = (((1,), (1,)), ((), ()))
_TN = (((0,), (0,)), ((), ()))


def _causal(T, transposed=False):
    r, c = lax.broadcasted_iota(jnp.int32, (T, T), 0), lax.broadcasted_iota(jnp.int32, (T, T), 1)
    return r <= c if transposed else c <= r


def _with_shift(q_tile, shift, lane):
    keep = jnp.where((lane >= L_SHIFT) & (lane < L_SHIFT + 3), 0.0, q_tile)
    return (keep + _put3(lane, L_SHIFT, _split3(-shift))).astype(BF16)


def _ride_along(xchg, n_in, n_out, grid):
    if xchg is None:
        return (lambda body: body), [], [], [], [], []
    arrs, scatter = xchg
    n = len(arrs)

    def wrap(body):
        def wrapped(*refs):
            own_in, x_in = refs[:n_in], refs[n_in:n_in + n]
            own_out, x_out = refs[n_in + n:n_in + n + n_out], refs[n_in + n + n_out:n_in + 2 * n + n_out]
            rest = refs[n_in + 2 * n + n_out:]
            own_scratch, sems = rest[:len(rest) - 3], rest[len(rest) - 3:]
            ids = [pl.program_id(d) for d in range(len(grid))]
            first = functools.reduce(jnp.logical_and, [i == 0 for i in ids])
            last = functools.reduce(jnp.logical_and, [i == g - 1 for i, g in zip(ids, grid)])

            @pl.when(first)
            def _():
                for cp in _xchg_copies(x_in, x_out, scatter, *sems):
                    cp.start()

            body(*own_in, *own_out, *own_scratch)

            @pl.when(last)
            def _():
                for cp in _xchg_copies(x_in, x_out, scatter, *sems):
                    cp.wait()

        return wrapped

    return wrap, [_HBM] * n, [_HBM] * n, _xchg_out_shapes(arrs, scatter), _xchg_sems(n), list(arrs)


def _attn_rowmax(q_aug, k_aug, *, name, T=1024):
    S = q_aug.shape[0]
    T = _tile(S, T)
    n = S // T

    def body(q_ref, k_ref, o_ref, m_s):
        i, j = pl.program_id(1), pl.program_id(2)

        @pl.when(j == 0)
        def _():
            m_s[...] = jnp.full_like(m_s, NEG)

        def step(diag):
            s = lax.dot_general(q_ref[...], k_ref[...], _NT, preferred_element_type=F32)
            if diag:
                s = jnp.where(_causal(T), s, NEG)
            m = m_s[...]
            for cb in range(T // LANES):
                m = jnp.maximum(m, s[:, cb * LANES:(cb + 1) * LANES])
            m_s[...] = m

        @pl.when(j < i)
        def _():
            step(False)

        @pl.when(j == i)
        def _():
            step(True)
            o_ref[...] = _with_shift(q_ref[...].astype(F32), jnp.max(m_s[...], axis=-1, keepdims=True), _lane())

    qrow = pl.BlockSpec((T, LANES), lambda h, i, j: (i, h))
    return pl.pallas_call(
        body, grid=(HEADS, n, n),
        in_specs=[qrow, pl.BlockSpec((T, LANES), lambda h, i, j: (jnp.minimum(j, i), h))],
        out_specs=qrow, out_shape=jax.ShapeDtypeStruct(q_aug.shape, BF16),
        scratch_shapes=[pltpu.VMEM((T, LANES), F32)],
        compiler_params=_params("parallel", "parallel", "arbitrary"), name=name)(q_aug, k_aug)


def _attn_fwd(q_max, k_aug, v_aug, xchg=None, *, name, T=1024):
    S = q_max.shape[0]
    T = _tile(S, T)
    n = S // T
    wrap, x_in, x_out, x_shapes, x_sems, x_ops = _ride_along(xchg, 3, 2, (HEADS, n, n))

    def body(q_ref, k_ref, v_ref, o_ref, qb_ref, acc_s):
        i, j = pl.program_id(1), pl.program_id(2)

        @pl.when(j == 0)
        def _():
            acc_s[...] = jnp.zeros_like(acc_s)

        def block(rows, cols, mask):
            s = lax.dot_general(q_ref[rows, :], k_ref[cols, :], _NT, preferred_element_type=F32)
            if mask is not None:
                s = jnp.where(mask, s, NEG)
            return jnp.dot(jnp.exp(s).astype(BF16), v_ref[cols, :], preferred_element_type=F32)

        @pl.when(j < i)
        def _():
            acc_s[...] += block(pl.ds(0, T), pl.ds(0, T), None)

        @pl.when(j == i)
        def _():
            half = T // 2
            lo, hi = pl.ds(0, half), pl.ds(half, half)
            acc_s[lo, :] += block(lo, lo, _causal(half))
            acc_s[hi, :] += block(hi, lo, None) + block(hi, hi, _causal(half))
            lane = _lane()
            acc = acc_s[...]
            l = _lane_col(acc, lane, L_F)
            o_ref[...] = acc / l
            qf = q_ref[...].astype(F32)
            row_max = -jnp.sum(jnp.where((lane >= L_SHIFT) & (lane < L_SHIFT + 3), qf, 0.0), axis=-1, keepdims=True)
            qb_ref[...] = _with_shift(qf, row_max + jnp.log(l), lane)

    qrow = pl.BlockSpec((T, LANES), lambda h, i, j: (i, h))
    kv = pl.BlockSpec((T, LANES), lambda h, i, j: (jnp.minimum(j, i), h))
    outs = pl.pallas_call(
        wrap(body), grid=(HEADS, n, n), in_specs=[qrow, kv, kv] + x_in, out_specs=[qrow, qrow] + x_out,
        out_shape=[jax.ShapeDtypeStruct(q_max.shape, F32), jax.ShapeDtypeStruct(q_max.shape, BF16)] + x_shapes,
        scratch_shapes=[pltpu.VMEM((T, LANES), F32)] + x_sems,
        compiler_params=_params("arbitrary", "arbitrary", "arbitrary"), name=name)(q_max, k_aug, v_aug, *x_ops)
    return outs[0], outs[1], outs[2:]


def _attn_bwd(q_lse, k_aug, v_aug, do_aug, xchg=None, *, name, T=1024):
    S = q_lse.shape[0]
    T = _tile(S, T)
    n = S // T
    wrap, x_in, x_out, x_shapes, x_sems, x_ops = _ride_along(xchg, 4, 3, (HEADS, n, n))

    def body(q_ref, do_ref, k_ref, v_ref, dq_ref, dk_ref, dv_ref, dq_s, dk_s, dv_s):
        j, i = pl.program_id(1), pl.program_id(2)

        def block(keys, queries, mask):
            q, do, k, v = q_ref[queries, :], do_ref[queries, :], k_ref[keys, :], v_ref[keys, :]
            st = lax.dot_general(k, q, _NT, preferred_element_type=F32)
            if mask is not None:
                st = jnp.where(mask, st, NEG)
            pt = jnp.exp(st)
            dst = (pt * lax.dot_general(v, do, _NT, preferred_element_type=F32)).astype(BF16)
            dv_s[keys, :] += jnp.dot(pt.astype(BF16), do, preferred_element_type=F32)
            dk_s[keys, :] += jnp.dot(dst, q, preferred_element_type=F32)
            return lax.dot_general(dst, k, _TN, preferred_element_type=F32)

        @pl.when(i == j)
        def _():
            dk_s[...] = jnp.zeros_like(dk_s)
            dv_s[...] = jnp.zeros_like(dv_s)

            @pl.when(j == 0)
            def _():
                dq_s[i] = jnp.zeros((T, LANES), F32)

            half = T // 2
            lo, hi = pl.ds(0, half), pl.ds(half, half)
            dq_s[i, lo, :] += block(lo, lo, _causal(half, transposed=True))
            dq_s[i, hi, :] += block(lo, hi, None) + block(hi, hi, _causal(half, transposed=True))
            dq_ref[...] = dq_s[j]

        @pl.when(i > j)
        def _():
            upd = block(pl.ds(0, T), pl.ds(0, T), None)

            @pl.when(j == 0)
            def _():
                dq_s[i] = upd

            @pl.when(j > 0)
            def _():
                dq_s[i] += upd

        @pl.when(i == n - 1)
        def _():
            dk_ref[...] = dk_s[...]
            dv_ref[...] = dv_s[...].astype(BF16)

    qrow = pl.BlockSpec((T, LANES), lambda h, j, i: (jnp.maximum(i, j), h))
    kv = pl.BlockSpec((T, LANES), lambda h, j, i: (j, h))
    outs = pl.pallas_call(
        wrap(body), grid=(HEADS, n, n), in_specs=[qrow, qrow, kv, kv] + x_in, out_specs=[kv, kv, kv] + x_out,
        out_shape=[jax.ShapeDtypeStruct(q_lse.shape, F32), jax.ShapeDtypeStruct(q_lse.shape, F32),
                   jax.ShapeDtypeStruct(q_lse.shape, BF16)] + x_shapes,
        scratch_shapes=[pltpu.VMEM((n, T, LANES), F32), pltpu.VMEM((T, LANES), F32), pltpu.VMEM((T, LANES), F32)] + x_sems,
        compiler_params=_params("arbitrary", "arbitrary", "arbitrary"), name=name)(q_lse, do_aug, k_aug, v_aug, *x_ops)
    return outs[0], outs[1], outs[2], outs[3:]


def _fox_gate_fwd(att_aug, proj, *, name, ts=256):
    S = att_aug.shape[0]
    D = HEADS * HEAD_DIM
    ts = _tile(S, ts)

    def body(a_ref, o_ref, att_ref, out_ref):
        lane = _lane()
        for cpair in range(HEAD_PAIRS):
            cols = pl.ds(cpair * LANES, LANES)
            pair = _pair_of_heads(a_ref[:, pl.ds(2 * cpair * LANES, LANES)], a_ref[:, pl.ds((2 * cpair + 1) * LANES, LANES)], lane)
            att_ref[:, cols] = pair
            out_ref[:, cols] = (pair * jax.nn.sigmoid(o_ref[:, cols].astype(F32))).astype(BF16)

    row = pl.BlockSpec((ts, D), lambda i: (i, 0))
    return pl.pallas_call(
        body, grid=(S // ts,),
        in_specs=[pl.BlockSpec((ts, HEADS * LANES), lambda i: (i, 0)), pl.BlockSpec((ts, D), lambda i: (i, 3))],
        out_specs=[row, row], out_shape=[jax.ShapeDtypeStruct((S, D), F32), jax.ShapeDtypeStruct((S, D), BF16)],
        compiler_params=_params("parallel"), name=name)(att_aug, proj)


def _fox_gate_bwd(da, att, proj, *, name, ts=256):
    S, D = att.shape
    ts = _tile(S, ts)

    def body(da_ref, a_ref, o_ref, do_ref, dog_ref):
        lane = _lane()
        for cpair in range(HEAD_PAIRS):
            cols = pl.ds(cpair * LANES, LANES)
            dav, av = da_ref[:, cols].astype(F32), a_ref[:, cols]
            sg = jax.nn.sigmoid(o_ref[:, cols].astype(F32))
            datt = (dav * sg).astype(BF16).astype(F32)
            dog_ref[:, cols] = (dav * av * sg * (1.0 - sg)).astype(BF16)
            prod = datt * av
            for e in range(2):
                in_head = (lane < HEAD_DIM) if e == 0 else (lane >= HEAD_DIM)
                delta = jnp.sum(jnp.where(in_head, prod, 0.0), axis=-1, keepdims=True)
                tile = _head_of_pair(datt, e, lane) + _put3(lane, L_F, _split3(-delta))
                do_ref[:, pl.ds((2 * cpair + e) * LANES, LANES)] = tile.astype(BF16)

    row = pl.BlockSpec((ts, D), lambda i: (i, 0))
    return pl.pallas_call(
        body, grid=(S // ts,), in_specs=[row, row, pl.BlockSpec((ts, D), lambda i: (i, 3))],
        out_specs=[pl.BlockSpec((ts, HEADS * LANES), lambda i: (i, 0)), row],
        out_shape=[jax.ShapeDtypeStruct((S, HEADS * LANES), BF16), jax.ShapeDtypeStruct((S, D), BF16)],
        compiler_params=_params("parallel"), name=name)(da, att, proj)


def _row_of(block, r):
    rows = lax.broadcasted_iota(jnp.int32, block.shape, 0)
    return jnp.sum(jnp.where(rows == r, block, 0.0), axis=0, keepdims=True)


def _shift_down(cur, tail, k):
    out = pltpu.roll(cur, k, 0)
    rows = lax.broadcasted_iota(jnp.int32, cur.shape, 0)
    for r in range(k):
        out = jnp.where(rows == r, _row_of(tail, tail.shape[0] - k + r), out)
    return out


def _shift_up(cur, head, k):
    n = cur.shape[0]
    out = pltpu.roll(cur, n - k, 0)
    rows = lax.broadcasted_iota(jnp.int32, cur.shape, 0)
    for r in range(k):
        out = jnp.where(rows == n - k + r, _row_of(head, r), out)
    return out


HALO = 16


CONV_TC = 1408


def _pair_tiles(v):
    nc = v.shape[-1] // (2 * CONV_TC)
    return jnp.swapaxes(v.reshape(v.shape[:-1] + (2, nc, CONV_TC)), -3, -2).reshape(v.shape)


def _unpair_tiles(v):
    nc = v.shape[-1] // (2 * CONV_TC)
    return jnp.swapaxes(v.reshape(v.shape[:-1] + (nc, 2, CONV_TC)), -3, -2).reshape(v.shape)


def _conv_rows(cur, tail, w_ref, b_ref, cols):
    a1, a2 = _shift_down(cur, tail, 1), _shift_down(cur, tail, 2)
    return a2 * w_ref[0:1, cols] + a1 * w_ref[1:2, cols] + cur * w_ref[2:3, cols] + b_ref[:, cols], (a2, a1, cur)


def _conv_gate_fwd(a, cw, cb, *, name, ts=512):
    S, F2 = a.shape
    tc = CONV_TC
    ts = _tile(S, ts)
    nc = F2 // (2 * tc)
    sub = ts // HALO
    halves = (pl.ds(0, tc), pl.ds(tc, tc))

    def body(a_ref, t_ref, w_ref, b_ref, o_ref):
        first = pl.program_id(1) == 0
        pre = []
        for cols in halves:
            tail = jnp.where(first, 0.0, t_ref[:, cols].astype(F32))
            pre.append(_conv_rows(a_ref[:, cols].astype(F32), tail, w_ref, b_ref, cols)[0])
        g, val = pre
        o_ref[...] = (g * jax.nn.sigmoid(g) * val).astype(BF16)

    return pl.pallas_call(
        body, grid=(nc, S // ts),
        in_specs=[pl.BlockSpec((ts, 2 * tc), lambda j, i: (i, j)),
                  pl.BlockSpec((HALO, 2 * tc), lambda j, i: (jnp.maximum(i * sub - 1, 0), j)),
                  pl.BlockSpec((CONV_WIDTH, 2 * tc), lambda j, i: (0, j)), pl.BlockSpec((1, 2 * tc), lambda j, i: (0, j))],
        out_specs=pl.BlockSpec((ts, tc), lambda j, i: (i, j)),
        out_shape=jax.ShapeDtypeStruct((S, F2 // 2), BF16),
        compiler_params=_params("parallel", "parallel"), name=name)(a, a, cw, cb)


def _conv_gate_bwd(a, dact, cw, cb, *, name, ts=512):
    S, F2 = a.shape
    tc = CONV_TC
    ts = _tile(S, ts)
    nc = F2 // (2 * tc)
    sub = ts // HALO
    n_rows = S // ts
    halves = (pl.ds(0, tc), pl.ds(tc, tc))

    def body(a_ref, at_ref, ah_ref, d_ref, dh_ref, w_ref, b_ref, da_ref, s_ref):
        i = pl.program_id(1)
        _acc_init(i, s_ref)

        def dpre_of(rows, tails, d):
            (g, taps_g), (val, taps_v) = [_conv_rows(rows[h], tails[h], w_ref, b_ref, halves[h]) for h in range(2)]
            sg = jax.nn.sigmoid(g)
            return (d * val * (sg * (1.0 + g * (1.0 - sg))), d * (g * sg)), (taps_g, taps_v)

        cur = [a_ref[:, c].astype(F32) for c in halves]
        tail = [jnp.where(i == 0, 0.0, at_ref[:, c].astype(F32)) for c in halves]
        dpre, taps = dpre_of(cur, tail, d_ref[...].astype(F32))
        head, _ = dpre_of([ah_ref[:, c].astype(F32) for c in halves], [x[ts - HALO:, :] for x in cur], dh_ref[...].astype(F32))
        for h, cols in enumerate(halves):
            dd = dpre[h]
            nxt = jnp.where(i == n_rows - 1, 0.0, head[h])
            da_ref[:, cols] = (dd * w_ref[2:3, cols] + _shift_up(dd, nxt, 1) * w_ref[1:2, cols]
                               + _shift_up(dd, nxt, 2) * w_ref[0:1, cols]).astype(BF16)
            for r in range(CONV_WIDTH):
                s_ref[r:r + 1, cols] += _colsum(dd * taps[h][r])
            s_ref[CONV_WIDTH:CONV_WIDTH + 1, cols] += _colsum(dd)

    nxt_rows = lambda i: jnp.minimum((i + 1) * sub, S // HALO - 1)
    return pl.pallas_call(
        body, grid=(nc, n_rows),
        in_specs=[pl.BlockSpec((ts, 2 * tc), lambda j, i: (i, j)),
                  pl.BlockSpec((HALO, 2 * tc), lambda j, i: (jnp.maximum(i * sub - 1, 0), j)),
                  pl.BlockSpec((HALO, 2 * tc), lambda j, i: (nxt_rows(i), j)),
                  pl.BlockSpec((ts, tc), lambda j, i: (i, j)), pl.BlockSpec((HALO, tc), lambda j, i: (nxt_rows(i), j)),
                  pl.BlockSpec((CONV_WIDTH, 2 * tc), lambda j, i: (0, j)), pl.BlockSpec((1, 2 * tc), lambda j, i: (0, j))],
        out_specs=[pl.BlockSpec((ts, 2 * tc), lambda j, i: (i, j)), pl.BlockSpec((8, 2 * tc), lambda j, i: (0, j))],
        out_shape=[jax.ShapeDtypeStruct((S, F2), BF16), jax.ShapeDtypeStruct((8, F2), F32)],
        compiler_params=_params("parallel", "arbitrary"), name=name)(a, a, a, dact, dact, cw, cb)


def _gelu_parts(z):
    z2 = z * z
    t = jnp.tanh(GELU_C0 * (z + GELU_C1 * z * z2))
    val = 0.5 * z * (1.0 + t)
    grad = 0.5 * (1.0 + t) + 0.5 * z * (1.0 - t * t) * GELU_C0 * (1.0 + 3.0 * GELU_C1 * z2)
    return val, grad


def _sgu_fwd(pre, b_in, vgain, vbias, wm, bsb, *, name, ts=256):
    S, W2 = pre.shape
    W = W2 // 2
    gd = W // SGU_GROUPS
    ts = _tile(S, ts)

    def body(p_ref, b_ref, vg_ref, vb_ref, wm_ref, bs_ref, y_ref):
        u = _gelu_parts(p_ref[:, pl.ds(0, W)].astype(F32) + b_ref[:, pl.ds(0, W)])[0]
        v = _gelu_parts(p_ref[:, pl.ds(W, W)].astype(F32) + b_ref[:, pl.ds(W, W)])[0]
        mu = jnp.mean(v, axis=-1, keepdims=True)
        vc = v - mu
        rstd = lax.rsqrt(jnp.mean(vc * vc, axis=-1, keepdims=True) + EPS)
        vn = ((vc * rstd) * vg_ref[...] + vb_ref[...]).astype(BF16)
        for blk in range(ts // SGU_BLOCK):
            r0 = blk * SGU_BLOCK
            for g in range(SGU_GROUPS):
                c0 = g * gd
                mixed = jnp.dot(wm_ref[g], vn[r0:r0 + SGU_BLOCK, c0:c0 + gd], preferred_element_type=F32) + bs_ref[g]
                y_ref[pl.ds(r0, SGU_BLOCK), pl.ds(c0, gd)] = (u[r0:r0 + SGU_BLOCK, c0:c0 + gd] * mixed).astype(BF16)

    full = lambda shape: pl.BlockSpec(shape, lambda i: (0,) * len(shape))
    return pl.pallas_call(
        body, grid=(S // ts,),
        in_specs=[pl.BlockSpec((ts, W2), lambda i: (i, 0)), full((1, W2)), full((1, W)), full((1, W)),
                  full((SGU_GROUPS, SGU_BLOCK, SGU_BLOCK)), full((SGU_GROUPS, SGU_BLOCK, gd))],
        out_specs=pl.BlockSpec((ts, W), lambda i: (i, 0)), out_shape=jax.ShapeDtypeStruct((S, W), BF16),
        compiler_params=_params("parallel"), name=name)(pre, b_in, vgain, vbias, wm, bsb)


def _sgu_bwd(pre, dy, b_in, vgain, vbias, wm, wmt, bsb, *, name, ts=256):
    S, W2 = pre.shape
    W = W2 // 2
    gd = W // SGU_GROUPS
    ts = _tile(S, ts)
    last = S // ts - 1

    def body(p_ref, dy_ref, b_ref, vg_ref, vb_ref, wm_ref, wmt_ref, bs_ref,
             dp_ref, db_ref, dvg_ref, dvb_ref, dws_ref, dbs_ref, du_s, dvn_s, dbs_s):
        step = pl.program_id(0)
        _acc_init(step, db_ref, dvg_ref, dvb_ref, dws_ref, dbs_s)
        u, gu = _gelu_parts(p_ref[:, pl.ds(0, W)].astype(F32) + b_ref[:, pl.ds(0, W)])
        v, gv = _gelu_parts(p_ref[:, pl.ds(W, W)].astype(F32) + b_ref[:, pl.ds(W, W)])
        mu = jnp.mean(v, axis=-1, keepdims=True)
        vc = v - mu
        rstd = lax.rsqrt(jnp.mean(vc * vc, axis=-1, keepdims=True) + EPS)
        vhat = vc * rstd
        vn = (vhat * vg_ref[...] + vb_ref[...]).astype(BF16)
        dyv = dy_ref[...].astype(F32)
        for blk in range(ts // SGU_BLOCK):
            r0 = blk * SGU_BLOCK
            for g in range(SGU_GROUPS):
                c0 = g * gd
                vn_g = vn[r0:r0 + SGU_BLOCK, c0:c0 + gd]
                dy_g = dyv[r0:r0 + SGU_BLOCK, c0:c0 + gd]
                mixed = jnp.dot(wm_ref[g], vn_g, preferred_element_type=F32) + bs_ref[g]
                dmix = dy_g * u[r0:r0 + SGU_BLOCK, c0:c0 + gd]
                dmix_b = dmix.astype(BF16)
                du_s[pl.ds(r0, SGU_BLOCK), pl.ds(c0, gd)] = dy_g * mixed
                dvn_s[pl.ds(r0, SGU_BLOCK), pl.ds(c0, gd)] = jnp.dot(wmt_ref[g], dmix_b, preferred_element_type=F32)
                dws_ref[g] += lax.dot_general(dmix_b, vn_g, _NT, preferred_element_type=F32)
                dbs_s[g] += dmix
        dvn = dvn_s[...]
        dvg_ref[...] += _colsum(dvn * vhat)
        dvb_ref[...] += _colsum(dvn)
        dvh = dvn * vg_ref[...]
        dv = rstd * (dvh - jnp.mean(dvh, axis=-1, keepdims=True) - vhat * jnp.mean(dvh * vhat, axis=-1, keepdims=True))
        dpu = du_s[...] * gu
        dpv = dv * gv
        dp_ref[:, pl.ds(0, W)] = dpu.astype(BF16)
        dp_ref[:, pl.ds(W, W)] = dpv.astype(BF16)
        db_ref[:, pl.ds(0, W)] += _colsum(dpu)
        db_ref[:, pl.ds(W, W)] += _colsum(dpv)

        @pl.when(step == last)
        def _():
            for g in range(SGU_GROUPS):
                dbs_ref[g] = jnp.broadcast_to(jnp.sum(dbs_s[g], axis=-1, keepdims=True), (SGU_BLOCK, SGU_BLOCK))

    full = lambda shape: pl.BlockSpec(shape, lambda i: (0,) * len(shape))
    gsq = (SGU_GROUPS, SGU_BLOCK, SGU_BLOCK)
    return pl.pallas_call(
        body, grid=(S // ts,),
        in_specs=[pl.BlockSpec((ts, W2), lambda i: (i, 0)), pl.BlockSpec((ts, W), lambda i: (i, 0)),
                  full((1, W2)), full((1, W)), full((1, W)), full(gsq), full(gsq), full((SGU_GROUPS, SGU_BLOCK, gd))],
        out_specs=[pl.BlockSpec((ts, W2), lambda i: (i, 0)), full((1, W2)), full((1, W)), full((1, W)), full(gsq), full(gsq)],
        out_shape=[jax.ShapeDtypeStruct((S, W2), BF16), jax.ShapeDtypeStruct((1, W2), F32),
                   jax.ShapeDtypeStruct((1, W), F32), jax.ShapeDtypeStruct((1, W), F32),
                   jax.ShapeDtypeStruct(gsq, F32), jax.ShapeDtypeStruct(gsq, F32)],
        scratch_shapes=[pltpu.VMEM((ts, W), F32), pltpu.VMEM((ts, W), F32), pltpu.VMEM((SGU_GROUPS, SGU_BLOCK, gd), F32)],
        compiler_params=_params("arbitrary"), name=name)(pre, dy, b_in, vgain, vbias, wm, wmt, bsb)


def _paired_to_natural(w_up):
    nc = w_up.shape[1] // (2 * CONV_TC)
    return lambda q: (q % 2) * nc + q // 2


def _ffn_fwd(x, mods, n2g, w_up, cw, cb, w_down, tag):
    sh, sc, gate = mods
    h = _norm_mod_fwd(x, n2g, sh, sc, name=f"{tag}_norm_fwd")
    a = _mm(h, w_up, out_dtype=BF16, tn=CONV_TC, b_n=_paired_to_natural(w_up), name=f"{tag}_up")
    act = _conv_gate_fwd(a, cw, cb, name=f"{tag}_conv_fwd")
    x_out, y = _mm(act, w_down, tk=1408, res=(x, gate), name=f"{tag}_down")
    return x_out, (x, h, a, act, y)


def _ffn_bwd(dy, saved, mods, n2g, w_up, cw, cb, w_down, dres, prev, tag):
    x, h, a, act, _ = saved
    sh, sc, gate = mods
    dact = _mm(dy, w_down, tb=True, out_dtype=BF16, tn=1408, name=f"{tag}_down_dx")
    dw_down = _mm(act, dy, ta=True, out_dtype=BF16, tm=1408, name=f"{tag}_down_dw")
    da, sums = _conv_gate_bwd(a, dact, cw, cb, name=f"{tag}_conv_bwd")
    dh = _mm(da, w_up, tb=True, tk=CONV_TC, b_k=_paired_to_natural(w_up), name=f"{tag}_up_dx")
    dw_up = _mm(h, da, ta=True, out_dtype=BF16, tn=CONV_TC, o_n=_paired_to_natural(w_up), name=f"{tag}_up_dw")
    outs = _norm_mod_bwd(dh, x, n2g, sc, dres, prev, name=f"{tag}_norm_bwd")
    sums = _unpair_tiles(sums)
    return outs, dict(w_up=dw_up, w_down=dw_down, conv_w=sums[0:CONV_WIDTH], conv_b=sums[CONV_WIDTH])


def _local_step(x, target, w, mods, late=None, early=None):
    S, D = x.shape
    lane = jnp.arange(LANES)
    gmat = jnp.where((lane[:, None] // HEAD_DIM) == (lane[None, :] // HEAD_DIM), 1.0 / HEAD_DIM, 0.0).astype(BF16)
    qg2 = jnp.tile(w["fox_q_gain"].reshape(1, HEAD_DIM), (1, 2))
    kg2 = jnp.tile(w["fox_k_gain"].reshape(1, HEAD_DIM), (1, 2))
    bf_pad = jnp.pad(w["fox_b_f"].reshape(1, HEADS), ((0, 0), (0, LANES - HEADS)))
    w_in_pad = jnp.pad(w["fox_w_in"], ((0, 0), (0, 4 * D + LANES - w["fox_w_in"].shape[1])))
    w_qkvo, w_f = w_in_pad[:, :4 * D], w_in_pad[:, 4 * D:]
    tpos = jnp.arange(SGU_BLOCK)
    smask = (tpos[None, :] // SGU_CHUNK) <= (tpos[:, None] // SGU_CHUNK)
    wm32 = jnp.where(smask[None], w["sgu_w_s"], 0.0)
    wm, wmt = wm32.astype(BF16), jnp.swapaxes(wm32, 1, 2).astype(BF16)
    gd = w["sgu_v_gain"].shape[-1] // SGU_GROUPS
    bsb = jnp.broadcast_to(w["sgu_b_s"][:, :, None], (SGU_GROUPS, SGU_BLOCK, gd))
    vec = lambda v: v.reshape(1, -1)

    sh1, sc1, g1 = mods[0][0:3]
    h0 = _norm_mod_fwd(x, vec(w["norm1_g"][0]), sh1, sc1, name="fox_norm_fwd")
    proj = _mm(h0, w_qkvo, out_dtype=BF16, name="fox_proj")
    fl = _mm(h0, w_f, name="fox_forget_proj")
    fcum = _fox_decay_fwd(fl, bf_pad, name="fox_decay")
    q_aug, k_aug, v_aug = _fox_prep_fwd(proj, fcum, qg2, kg2, gmat, name="fox_qk_norm")
    logit_bound = 8.0 * jnp.max(jnp.abs(w["fox_q_gain"])) * jnp.max(jnp.abs(w["fox_k_gain"]))
    q_max = lax.cond(logit_bound <= SHIFT_FREE_LOGIT_BOUND, lambda: q_aug,
                     lambda: _attn_rowmax(q_aug, k_aug, name="fox_attn_rowmax"))
    xchg = None if late is None else (late[0], [False] * len(late[0]))
    att_aug, q_lse, gathered = _attn_fwd(q_max, k_aug, v_aug, xchg, name="fox_attn_fwd")
    if late is not None:
        w = {**w, **late[1](gathered)}
    w = dict(w, ffn_conv_w=_pair_tiles(w["ffn_conv_w"]), ffn_conv_b=_pair_tiles(w["ffn_conv_b"]))
    att, ag = _fox_gate_fwd(att_aug, proj, name="fox_gate_fwd")
    x1, y_fox = _mm(ag, w["fox_w_out"], res=(x, g1), name="fox_out")
    x2, ffn0 = _ffn_fwd(x1, mods[0][3:6], vec(w["norm2_g"][0]), w["ffn_w_up"][0], w["ffn_conv_w"][0],
                        vec(w["ffn_conv_b"][0]), w["ffn_w_down"][0], "ffn0")

    sh1b, sc1b, g1b = mods[1][0:3]
    h1 = _norm_mod_fwd(x2, vec(w["norm1_g"][1]), sh1b, sc1b, name="sgu_norm_fwd")
    pre = _mm(h1, w["sgu_w_in"], out_dtype=BF16, name="sgu_in")
    b_in, vg, vb = vec(w["sgu_b_in"]), vec(w["sgu_v_gain"]), vec(w["sgu_v_bias"])
    ys = _sgu_fwd(pre, b_in, vg, vb, wm, bsb, name="sgu_core_fwd")
    x3, y_sgu = _mm(ys, w["sgu_w_out"], res=(x2, g1b), name="sgu_out")
    x4, ffn1 = _ffn_fwd(x3, mods[1][3:6], vec(w["norm2_g"][1]), w["ffn_w_up"][1], w["ffn_conv_w"][1],
                        vec(w["ffn_conv_b"][1]), w["ffn_w_down"][1], "ffn1")

    loss, d_final_g, dx4, dy_ffn1, dgate_ffn1 = _final_loss(x4, vec(w["final_g"]), target, ffn1[4], mods[1][5], name="final_loss")

    (dx3, dn2g_1, dsh2_1, dsc2_1, dy_sgu, dgate_sgu), g_ffn1 = _ffn_bwd(
        dy_ffn1, ffn1, mods[1][3:6], vec(w["norm2_g"][1]), w["ffn_w_up"][1], w["ffn_conv_w"][1], vec(w["ffn_conv_b"][1]),
        w["ffn_w_down"][1], dx4, (y_sgu, g1b), "ffn1")

    dys = _mm(dy_sgu, w["sgu_w_out"], tb=True, out_dtype=BF16, name="sgu_out_dx")
    dw_sgu_out = _mm(ys, dy_sgu, ta=True, out_dtype=BF16, name="sgu_out_dw")
    dpre, db_in, dvg, dvb, dws, dbs = _sgu_bwd(pre, dys, b_in, vg, vb, wm, wmt, bsb, name="sgu_core_bwd")
    dh1 = _mm(dpre, w["sgu_w_in"], tb=True, name="sgu_in_dx")
    dw_sgu_in = _mm(h1, dpre, ta=True, out_dtype=BF16, name="sgu_in_dw")
    dx2, dn1g_1, dsh1_1, dsc1_1, dy_ffn0, dgate_ffn0 = _norm_mod_bwd(
        dh1, x2, vec(w["norm1_g"][1]), sc1b, dx3, (ffn0[4], mods[0][5]), name="sgu_norm_bwd")

    (dx1, dn2g_0, dsh2_0, dsc2_0, dy_fox, dgate_fox), g_ffn0 = _ffn_bwd(
        dy_ffn0, ffn0, mods[0][3:6], vec(w["norm2_g"][0]), w["ffn_w_up"][0], w["ffn_conv_w"][0], vec(w["ffn_conv_b"][0]),
        w["ffn_w_down"][0], dx2, (y_fox, g1), "ffn0")

    dag = _mm(dy_fox, w["fox_w_out"], tb=True, out_dtype=BF16, name="fox_out_dx")
    dw_fox_out = _mm(ag, dy_fox, ta=True, out_dtype=BF16, name="fox_out_dw")
    do_aug, dog = _fox_gate_bwd(dag, att, proj, name="fox_gate_bwd")
    grads = dict(
        sgu_w_in=dw_sgu_in, sgu_b_in=db_in[0], sgu_v_gain=dvg[0], sgu_v_bias=dvb[0],
        sgu_w_s=jnp.where(smask[None], dws, 0.0), sgu_b_s=dbs[:, :, 0], sgu_w_out=dw_sgu_out,
        ffn_w_up=jnp.stack([g_ffn0["w_up"], g_ffn1["w_up"]]),
        ffn_conv_w=jnp.stack([g_ffn0["conv_w"], g_ffn1["conv_w"]]),
        ffn_conv_b=jnp.stack([g_ffn0["conv_b"], g_ffn1["conv_b"]]),
        ffn_w_down=jnp.stack([g_ffn0["w_down"], g_ffn1["w_down"]]),
        final_g=d_final_g[0], fox_w_out=dw_fox_out,
    )
    xchg = None
    if early is not None:
        blocks = early(grads)
        xchg = (blocks, [True] * len(blocks))
    dq_aug, dk_aug, dv_aug, exchanged = _attn_bwd(q_lse, k_aug, v_aug, do_aug, xchg, name="fox_attn_bwd")
    dproj, dF, dqg, dkg = _fox_prep_bwd(proj, dq_aug, dk_aug, dv_aug, dog, qg2, kg2, gmat, name="fox_qk_norm_bwd")
    dproj, dbf = _fox_decay_bwd(dF, fl, bf_pad, dproj, name="fox_decay_bwd")
    dh0 = _mm(dproj, w_in_pad, tb=True, tk=1408, name="fox_proj_dx")
    dw_fox_in = _mm(h0, dproj, ta=True, out_dtype=BF16, tn=1408, name="fox_proj_dw")
    dx0, dn1g_0, dsh1_0, dsc1_0 = _norm_mod_bwd(dh0, x, vec(w["norm1_g"][0]), sc1, dx1, None, name="fox_norm_bwd")

    dmod0 = jnp.concatenate([dsh1_0, dsc1_0, dgate_fox, dsh2_0, dsc2_0, dgate_ffn0], axis=1)
    dmod1 = jnp.concatenate([dsh1_1, dsc1_1, dgate_sgu, dsh2_1, dsc2_1, dgate_ffn1], axis=1)
    grads.update(
        fox_w_in=dw_fox_in[:, :w["fox_w_in"].shape[1]],
        fox_b_f=dbf[0, :HEADS],
        fox_q_gain=dqg[0, :HEAD_DIM] + dqg[0, HEAD_DIM:],
        fox_k_gain=dkg[0, :HEAD_DIM] + dkg[0, HEAD_DIM:],
        fox_w_out=dw_fox_out,
        ada_b=jnp.concatenate([dmod0, dmod1], axis=0),
        norm1_g=jnp.concatenate([dn1g_0, dn1g_1], axis=0), norm2_g=jnp.concatenate([dn2g_0, dn2g_1], axis=0),
    )
    return loss[0, 0], dx0, grads, exchanged


_HBM = pl.BlockSpec(memory_space=pl.ANY)
N_PEER = N_DEV - 1


def _xchg_out_shapes(arrs, scatter):
    return [jax.ShapeDtypeStruct(a.shape if s else (N_DEV,) + a.shape, a.dtype) for a, s in zip(arrs, scatter)]


def _xchg_sems(n):
    return [pltpu.SemaphoreType.DMA((n * N_PEER,)), pltpu.SemaphoreType.DMA((n * N_PEER,)), pltpu.SemaphoreType.DMA((n,))]


def _xchg_copies(ins, outs, scatter, send, recv, loc):
    x, y, c = lax.axis_index("x"), lax.axis_index("y"), lax.axis_index("c")
    me = 4 * x + 2 * y + c
    copies = []
    for a in range(len(ins)):
        copies.append(pltpu.make_async_copy(ins[a].at[me] if scatter[a] else ins[a], outs[a].at[me], loc.at[a]))
        for k in range(1, N_DEV):
            px = 1 - x if k & 4 else x
            py = 1 - y if k & 2 else y
            pc = 1 - c if k & 1 else c
            copies.append(pltpu.make_async_remote_copy(
                src_ref=ins[a].at[4 * px + 2 * py + pc] if scatter[a] else ins[a], dst_ref=outs[a].at[me],
                send_sem=send.at[a * N_PEER + k - 1], recv_sem=recv.at[a * N_PEER + k - 1],
                device_id=(px, py, pc), device_id_type=MESH))
    return copies


def _exchange(arrs, scatter, *, name):
    n = len(arrs)

    def body(*refs):
        copies = _xchg_copies(refs[:n], refs[n:2 * n], scatter, *refs[2 * n:])
        for cp in copies:
            cp.start()
        for cp in copies:
            cp.wait()

    return pl.pallas_call(
        body, in_specs=[_HBM] * n, out_specs=[_HBM] * n, out_shape=_xchg_out_shapes(arrs, scatter),
        scratch_shapes=_xchg_sems(n),
        compiler_params=pltpu.CompilerParams(has_side_effects=True), name=name)(*arrs)


def _adamw(w, parts, m, v, *, name, tr=256):
    R, C = w.shape
    P = parts.shape[0]
    tr = next(t for t in range(min(R, tr), 0, -1) if R % t == 0 and (t % 16 == 0 or t == R))
    c1 = 1.0 - ADAM_B1 ** ADAM_STEP
    c2 = 1.0 - ADAM_B2 ** ADAM_STEP

    def body(w_ref, p_ref, m_ref, v_ref, g_ref, d_ref, mo_ref, vo_ref):
        g = p_ref[0].astype(F32)
        for p in range(1, P):
            g = g + p_ref[p].astype(F32)
        mn = ADAM_B1 * m_ref[...] + (1.0 - ADAM_B1) * g
        vn = ADAM_B2 * v_ref[...] + (1.0 - ADAM_B2) * (g * g)
        g_ref[...] = g
        mo_ref[...] = mn
        vo_ref[...] = vn
        d_ref[...] = -ADAM_LR * ((mn / c1) / (jnp.sqrt(vn / c2) + ADAM_EPS) + ADAM_WD * w_ref[...])

    row = pl.BlockSpec((tr, C), lambda i: (i, 0))
    return pl.pallas_call(
        body, grid=(R // tr,), in_specs=[row, pl.BlockSpec((P, tr, C), lambda i: (0, i, 0)), row, row],
        out_specs=[row] * 4, out_shape=[jax.ShapeDtypeStruct((R, C), F32)] * 4,
        compiler_params=_params("parallel"), name=name)(w, parts, m, v)


def _sum_parts(parts, *, name):
    P, R, C = parts.shape

    def body(p_ref, o_ref):
        g = p_ref[0]
        for p in range(1, P):
            g = g + p_ref[p]
        o_ref[...] = g

    return pl.pallas_call(body, out_shape=jax.ShapeDtypeStruct((R, C), F32), name=name)(parts)


WEIGHTS = ["fox_w_in", "fox_b_f", "fox_q_gain", "fox_k_gain", "fox_w_out", "sgu_w_in", "sgu_b_in", "sgu_v_gain",
           "sgu_v_bias", "sgu_w_s", "sgu_b_s", "sgu_w_out", "ffn_w_up", "ffn_conv_w", "ffn_conv_b", "ffn_w_down",
           "ada_w", "ada_b", "norm1_g", "norm2_g", "final_g"]
BIG_AXIS = dict(fox_w_in=1, fox_w_out=0, sgu_w_in=1, sgu_w_out=0, ffn_w_up=1, ffn_w_down=0, ada_w=1)
SMALL_SHARDED = ["sgu_b_in", "sgu_v_gain", "sgu_v_bias", "ffn_conv_w"]
SINGLE_LAYER = ("fox_", "sgu_")
BEFORE_ATTENTION = ["fox_w_in"]
AFTER_ATTENTION = ["fox_w_out", "sgu_w_in", "sgu_w_out", "ffn_w_up", "ffn_w_down"]


def _assemble(stacked, layers, axis):
    _, lr, cc = stacked.shape
    r = lr // layers
    s4 = stacked.reshape(N_DEV, layers, r, cc)
    if axis == 0:
        return s4.transpose(1, 0, 2, 3).reshape(layers, N_DEV * r, cc)
    return s4.transpose(1, 2, 0, 3).reshape(layers, r, N_DEV * cc)


def _disassemble(full, axis):
    layers, R, C = full.shape
    if axis == 0:
        r = R // N_DEV
        return full.reshape(layers, N_DEV, r, C).transpose(1, 0, 2, 3).reshape(N_DEV, layers * r, C)
    cc = C // N_DEV
    return full.reshape(layers, R, N_DEV, cc).transpose(2, 0, 1, 3).reshape(N_DEV, layers * R, cc)


def kernel(x, c, fox_w_in, fox_b_f, fox_q_gain, fox_k_gain, fox_w_out, sgu_w_in, sgu_b_in, sgu_v_gain, sgu_v_bias, sgu_w_s, sgu_b_s, sgu_w_out, ffn_w_up, ffn_conv_w, ffn_conv_b, ffn_w_down, ada_w, ada_b, norm1_g, norm2_g, final_g, loss_target, m_fox_w_in, m_fox_b_f, m_fox_q_gain, m_fox_k_gain, m_fox_w_out, m_sgu_w_in, m_sgu_b_in, m_sgu_v_gain, m_sgu_v_bias, m_sgu_w_s, m_sgu_b_s, m_sgu_w_out, m_ffn_w_up, m_ffn_conv_w, m_ffn_conv_b, m_ffn_w_down, m_ada_w, m_ada_b, m_norm1_g, m_norm2_g, m_final_g, v_fox_w_in, v_fox_b_f, v_fox_q_gain, v_fox_k_gain, v_fox_w_out, v_sgu_w_in, v_sgu_b_in, v_sgu_v_gain, v_sgu_v_bias, v_sgu_w_s, v_sgu_b_s, v_sgu_w_out, v_ffn_w_up, v_ffn_conv_w, v_ffn_conv_b, v_ffn_w_down, v_ada_w, v_ada_b, v_norm1_g, v_norm2_g, v_final_g):
    args = dict(locals())
    wts = {n: args[n] for n in WEIGHTS}
    ms = {n: args["m_" + n] for n in WEIGHTS}
    vs = {n: args["v_" + n] for n in WEIGHTS}
    me = 4 * lax.axis_index("x") + 2 * lax.axis_index("y") + lax.axis_index("c")

    shard2d = lambda n: wts[n].astype(BF16).reshape(-1, wts[n].shape[-1])

    def assemble_big(names, got):
        out = {}
        for n, g in zip(names, got):
            f = _assemble(g, wts[n].shape[0], BIG_AXIS[n])
            out[n] = f[0] if n.startswith(SINGLE_LAYER) else f
        return out

    def blocks_of(names, grads):
        return [_disassemble(grads[n] if grads[n].ndim == 3 else grads[n][None], BIG_AXIS[n]) for n in names]

    send = [c] + [shard2d(n) for n in BEFORE_ATTENTION] + [wts[n].reshape(-1, wts[n].shape[-1]) for n in SMALL_SHARDED]
    got = _exchange(send, [False] * len(send), name="gather_first")
    c_all = got[0].reshape(N_DEV, -1)
    full = assemble_big(BEFORE_ATTENTION, got[1:1 + len(BEFORE_ATTENTION)])
    for n, g in zip(SMALL_SHARDED, got[1 + len(BEFORE_ATTENTION):]):
        lead = wts[n].shape[:-1]
        f = jnp.moveaxis(g.reshape((N_DEV,) + wts[n].shape), 0, -2).reshape(lead + (-1,))
        full[n] = f[0] if n.startswith(SINGLE_LAYER) else f
    for n in WEIGHTS:
        if n not in full and n not in BIG_AXIS:
            full[n] = wts[n][0] if n.startswith(SINGLE_LAYER) else wts[n]

    ada_cols = wts["ada_w"].shape[-1]
    mod_rows = []
    for i in range(2):
        b_mine = lax.dynamic_slice_in_dim(wts["ada_b"][i], me * ada_cols, ada_cols).reshape(1, ada_cols)
        m, c_act = _ada_mod(c_all, wts["ada_w"][i].astype(BF16), b_mine, name=f"ada_mod_{i}")
        mod_rows.append(m)
    got = _exchange([jnp.concatenate(mod_rows, axis=1)[:, None, :]], [True], name="exchange_mods")[0]
    d_model = x.shape[-1]
    mods = []
    for i in range(2):
        mod = got[:, 0, i * ada_cols:(i + 1) * ada_cols].reshape(1, N_DEV * ada_cols)
        mods.append([mod[:, k * d_model:(k + 1) * d_model] for k in range(6)])

    late = ([shard2d(n) for n in AFTER_ATTENTION], lambda g: assemble_big(AFTER_ATTENTION, g))
    loss, grad_x, grads, got_late = _local_step(x[0], loss_target[0], full, mods, late,
                                                lambda gr: blocks_of(AFTER_ATTENTION, gr))

    small = [n for n in WEIGHTS if n not in BIG_AXIS]
    flat = jnp.concatenate([loss.reshape(1)] + [grads[n].reshape(-1).astype(F32) for n in small])
    n_flat = flat.shape[0]
    rows = -(-n_flat // (8 * LANES)) * 8
    flat = jnp.pad(flat, (0, rows * LANES - n_flat)).reshape(rows, LANES)
    got = _exchange(blocks_of(BEFORE_ATTENTION, grads) + [flat], [True] * len(BEFORE_ATTENTION) + [False], name="exchange_last")
    flat_all = got[-1]
    total = _sum_parts(flat_all, name="sum_small_grads").reshape(-1)
    loss_out = total[0]

    off_ada = 1 + sum(math.prod(grads[n].shape) for n in small[:small.index("ada_b")])
    dmod_all = flat_all.reshape(N_DEV, -1)[:, off_ada:off_ada + 2 * N_DEV * ada_cols].reshape(N_DEV, 2, N_DEV * ada_cols)
    dmod_mine = lax.dynamic_slice_in_dim(dmod_all, me * ada_cols, ada_cols, axis=2)
    d_ada = [_mm(c_act, jnp.pad(dmod_mine[:, i], ((0, c_act.shape[0] - N_DEV), (0, 0))).astype(BF16), ta=True,
                 name=f"ada_dw_{i}") for i in range(2)]

    out_g, out_d, out_m, out_v = {}, {}, {}, {}
    summands = dict(zip(BEFORE_ATTENTION, got))
    summands.update(zip(AFTER_ATTENTION, got_late))
    summands["ada_w"] = jnp.concatenate(d_ada, axis=0)[None]
    for n, p in summands.items():
        shp = wts[n].shape
        two_d = lambda a: a.reshape(-1, shp[-1])
        g, d, mn, vn = _adamw(two_d(wts[n]), p, two_d(ms[n]), two_d(vs[n]), name=f"adamw_{n}")
        out_g[n], out_d[n], out_m[n], out_v[n] = (a.reshape(shp) for a in (g, d, mn, vn))
    off = 1
    small_g = {}
    for n in small:
        full_shape = grads[n].shape
        size = math.prod(full_shape)
        g = total[off:off + size].reshape(full_shape)
        off += size
        if n in SMALL_SHARDED:
            blk = full_shape[-1] // N_DEV
            g = lax.dynamic_slice_in_dim(g, me * blk, blk, axis=g.ndim - 1)
        small_g[n] = g.reshape(wts[n].shape)
    cat = lambda d: jnp.concatenate([d[n].reshape(-1) for n in small])
    n_small = sum(math.prod(wts[n].shape) for n in small)
    rows2 = -(-n_small // (256 * LANES)) * 256
    pack = lambda d, fill: jnp.pad(cat(d), (0, rows2 * LANES - n_small), constant_values=fill).reshape(rows2, LANES)
    g, d, mn, vn = _adamw(pack(wts, 0.0), pack(small_g, 0.0)[None], pack(ms, 0.0), pack(vs, 1.0), name="adamw_small")
    off = 0
    for n in small:
        size = math.prod(wts[n].shape)
        for src, dst in ((g, out_g), (d, out_d), (mn, out_m), (vn, out_v)):
            dst[n] = src.reshape(-1)[off:off + size].reshape(wts[n].shape)
        off += size

    return (loss_out, grad_x[None], *[out_g[n] for n in WEIGHTS], *[out_d[n] for n in WEIGHTS],
            *[out_m[n] for n in WEIGHTS], *[out_v[n] for n in WEIGHTS])
```

```python
import functools
import math

import jax
import jax.numpy as jnp
from jax import lax
from jax.experimental import pallas as pl
from jax.experimental.pallas import tpu as pltpu

F32, BF16 = jnp.float32, jnp.bfloat16
N_DEV = 8
HEADS, HEAD_DIM = 16, 64
HEAD_PAIRS = HEADS // 2
LANES = 128
EPS = 1e-6
SGU_BLOCK, SGU_GROUPS, SGU_CHUNK = 128, 8, 64
CONV_WIDTH = 3
ADAM_LR, ADAM_B1, ADAM_B2, ADAM_EPS, ADAM_WD, ADAM_STEP = 0.001, 0.9, 0.999, 1e-08, 0.01, 10
NEG = -1e30
GELU_C0, GELU_C1 = math.sqrt(2.0 / math.pi), 0.044715
MESH = pl.DeviceIdType.MESH
VMEM_LIMIT = 56 * 1024 * 1024


def _tile(dim, pref):
    if dim <= pref:
        return dim
    t = (pref // LANES) * LANES
    while t >= LANES:
        if dim % t == 0:
            return t
        t -= LANES
    return dim


def _params(*sem):
    return pltpu.CompilerParams(dimension_semantics=sem, vmem_limit_bytes=VMEM_LIMIT)


def _mm(a, b, *, name, ta=False, tb=False, out_dtype=F32, tm=1024, tn=1024, tk=1024, res=None, b_n=None, b_k=None, o_n=None,
        xchg=None):
    M = a.shape[1] if ta else a.shape[0]
    K = a.shape[0] if ta else a.shape[1]
    N = b.shape[0] if tb else b.shape[1]
    tm, tn, tk = _tile(M, tm), _tile(N, tn), _tile(K, tk)
    nk = K // tk
    dims = (((0 if ta else 1,), (1 if tb else 0,)), ((), ()))
    same = lambda idx: idx
    b_n, b_k, o_n = b_n or same, b_k or same, o_n or same
    a_spec = pl.BlockSpec((tk, tm), lambda i, j, k: (k, i)) if ta else pl.BlockSpec((tm, tk), lambda i, j, k: (i, k))
    b_spec = (pl.BlockSpec((tn, tk), lambda i, j, k: (b_n(j), b_k(k))) if tb
              else pl.BlockSpec((tk, tn), lambda i, j, k: (b_k(k), b_n(j))))
    o_spec = pl.BlockSpec((tm, tn), lambda i, j, k: (i, o_n(j)))

    def accumulate(a_ref, b_ref, acc):
        @pl.when(pl.program_id(2) == 0)
        def _():
            acc[...] = jnp.zeros_like(acc)
        acc[...] += lax.dot_general(a_ref[...], b_ref[...], dims, preferred_element_type=F32)

    if res is None:
        def body(a_ref, b_ref, o_ref, acc):
            accumulate(a_ref, b_ref, acc)

            @pl.when(pl.program_id(2) == nk - 1)
            def _():
                o_ref[...] = acc[...].astype(o_ref.dtype)

        if xchg is None:
            return pl.pallas_call(
                body, grid=(M // tm, N // tn, nk), in_specs=[a_spec, b_spec], out_specs=o_spec,
                out_shape=jax.ShapeDtypeStruct((M, N), out_dtype), scratch_shapes=[pltpu.VMEM((tm, tn), F32)],
                compiler_params=_params("parallel", "parallel", "arbitrary"), name=name)(a, b)
        grid = (M // tm, N // tn, nk)
        wrap, x_in, x_out, x_shapes, x_sems, x_ops = _ride_along(xchg, 2, 1, grid)
        outs = pl.pallas_call(
            wrap(body), grid=grid, in_specs=[a_spec, b_spec] + x_in, out_specs=[o_spec] + x_out,
            out_shape=[jax.ShapeDtypeStruct((M, N), out_dtype)] + x_shapes,
            scratch_shapes=[pltpu.VMEM((tm, tn), F32)] + x_sems,
            compiler_params=_params("arbitrary", "arbitrary", "arbitrary"), name=name)(a, b, *x_ops)
        return outs[0], outs[1:]

    x, gate = res

    def body_res(a_ref, b_ref, x_ref, g_ref, o_ref, y_ref, acc):
        accumulate(a_ref, b_ref, acc)

        @pl.when(pl.program_id(2) == nk - 1)
        def _():
            y = acc[...]
            o_ref[...] = x_ref[...] + g_ref[...] * y
            y_ref[...] = y.astype(BF16)

    return pl.pallas_call(
        body_res, grid=(M // tm, N // tn, nk),
        in_specs=[a_spec, b_spec, o_spec, pl.BlockSpec((1, tn), lambda i, j, k: (0, j))],
        out_specs=[o_spec, o_spec],
        out_shape=[jax.ShapeDtypeStruct((M, N), F32), jax.ShapeDtypeStruct((M, N), BF16)],
        scratch_shapes=[pltpu.VMEM((tm, tn), F32)],
        compiler_params=_params("parallel", "parallel", "arbitrary"), name=name)(a, b, x, gate)


def _ada_mod(c_rows, w, b, *, name):
    R, D = c_rows.shape
    N = w.shape[1]
    tn = _tile(N, 1536)
    rows = 16
    c_pad = jnp.pad(c_rows, ((0, rows - R), (0, 0)))

    def body(c_ref, w_ref, b_ref, o_ref, ca_ref):
        cv = c_ref[...]
        ca16 = (cv * jax.nn.sigmoid(cv)).astype(BF16)
        ca_ref[...] = ca16
        o_ref[...] = jnp.dot(ca16, w_ref[...], preferred_element_type=F32) + b_ref[...]

    out, ca = pl.pallas_call(
        body, grid=(N // tn,),
        in_specs=[pl.BlockSpec((rows, D), lambda j: (0, 0)), pl.BlockSpec((D, tn), lambda j: (0, j)),
                  pl.BlockSpec((1, tn), lambda j: (0, j))],
        out_specs=[pl.BlockSpec((rows, tn), lambda j: (0, j)), pl.BlockSpec((rows, D), lambda j: (0, 0))],
        out_shape=[jax.ShapeDtypeStruct((rows, N), F32), jax.ShapeDtypeStruct((rows, D), BF16)],
        compiler_params=_params("arbitrary"), name=name)(c_pad, w, b)
    return out[0:R], ca


def _norm_mod_fwd(x, g, shift, scale, *, name, ts=512):
    S, D = x.shape
    ts = _tile(S, ts)
    row = pl.BlockSpec((ts, D), lambda i: (i, 0))
    vec = pl.BlockSpec((1, D), lambda i: (0, 0))

    def body(x_ref, g_ref, sh_ref, sc_ref, h_ref):
        xv = x_ref[...]
        r = lax.rsqrt(jnp.mean(xv * xv, axis=-1, keepdims=True) + EPS)
        h_ref[...] = ((xv * r * g_ref[...]) * (1.0 + sc_ref[...]) + sh_ref[...]).astype(BF16)

    return pl.pallas_call(body, grid=(S // ts,), in_specs=[row, vec, vec, vec], out_specs=row,
                          out_shape=jax.ShapeDtypeStruct((S, D), BF16),
                          compiler_params=_params("parallel"), name=name)(x, g, shift, scale)


def _acc_init(step, *refs):
    @pl.when(step == 0)
    def _():
        for r in refs:
            r[...] = jnp.zeros_like(r)


def _colsum(v):
    return jnp.sum(v, axis=0, keepdims=True)


def _norm_mod_bwd(dh, x, g, scale, dres, prev=None, *, name, ts=512):
    S, D = x.shape
    ts = _tile(S, ts)
    row = pl.BlockSpec((ts, D), lambda i: (i, 0))
    vec = pl.BlockSpec((1, D), lambda i: (0, 0))
    has_prev = prev is not None

    def body(*refs):
        if has_prev:
            dh_ref, x_ref, g_ref, sc_ref, dres_ref, y_ref, gate_ref, dx_ref, dg_ref, dsh_ref, dsc_ref, dy_ref, dgate_ref = refs
            _acc_init(pl.program_id(0), dg_ref, dsh_ref, dsc_ref, dgate_ref)
        else:
            dh_ref, x_ref, g_ref, sc_ref, dres_ref, dx_ref, dg_ref, dsh_ref, dsc_ref = refs
            _acc_init(pl.program_id(0), dg_ref, dsh_ref, dsc_ref)
        xv, dhv, gv = x_ref[...], dh_ref[...], g_ref[...]
        r = lax.rsqrt(jnp.mean(xv * xv, axis=-1, keepdims=True) + EPS)
        xh = xv * r
        dsh_ref[...] += _colsum(dhv)
        dsc_ref[...] += _colsum(dhv * (xh * gv))
        dn = dhv * (1.0 + sc_ref[...])
        dg_ref[...] += _colsum(dn * xh)
        dxh = dn * gv
        dx = dres_ref[...] + r * (dxh - xh * jnp.mean(dxh * xh, axis=-1, keepdims=True))
        dx_ref[...] = dx
        if has_prev:
            dy_ref[...] = (gate_ref[...] * dx).astype(BF16)
            dgate_ref[...] += _colsum(dx * y_ref[...].astype(F32))

    ins, in_specs = [dh, x, g, scale, dres], [row, row, vec, vec, row]
    outs = [jax.ShapeDtypeStruct((S, D), F32)] + [jax.ShapeDtypeStruct((1, D), F32)] * 3
    out_specs = [row, vec, vec, vec]
    if has_prev:
        ins += list(prev)
        in_specs += [row, vec]
        outs += [jax.ShapeDtypeStruct((S, D), BF16), jax.ShapeDtypeStruct((1, D), F32)]
        out_specs += [row, vec]
    return pl.pallas_call(body, grid=(S // ts,), in_specs=in_specs, out_specs=out_specs, out_shape=outs,
                          compiler_params=_params("arbitrary"), name=name)(*ins)


def _final_loss(x, g, target, y, gate, *, name, ts=512):
    S, D = x.shape
    ts = _tile(S, ts)
    row = pl.BlockSpec((ts, D), lambda i: (i, 0))
    vec = pl.BlockSpec((1, D), lambda i: (0, 0))
    lvec = pl.BlockSpec((1, LANES), lambda i: (0, 0))

    def body(x_ref, g_ref, t_ref, y_ref, gate_ref, loss_ref, dg_ref, dx_ref, dy_ref, dgate_ref):
        _acc_init(pl.program_id(0), loss_ref, dg_ref, dgate_ref)
        xv, gv = x_ref[...], g_ref[...]
        r = lax.rsqrt(jnp.mean(xv * xv, axis=-1, keepdims=True) + EPS)
        xh = xv * r
        e = xh * gv - t_ref[...]
        loss_ref[...] += 0.5 * jnp.sum(jnp.mean(e * e, axis=-1, keepdims=True), axis=0, keepdims=True)
        dout = e * (1.0 / D)
        dg_ref[...] += _colsum(dout * xh)
        dxh = dout * gv
        dx = r * (dxh - xh * jnp.mean(dxh * xh, axis=-1, keepdims=True))
        dx_ref[...] = dx
        dy_ref[...] = (gate_ref[...] * dx).astype(BF16)
        dgate_ref[...] += _colsum(dx * y_ref[...].astype(F32))

    return pl.pallas_call(
        body, grid=(S // ts,), in_specs=[row, vec, row, row, vec], out_specs=[lvec, vec, row, row, vec],
        out_shape=[jax.ShapeDtypeStruct((1, LANES), F32), jax.ShapeDtypeStruct((1, D), F32),
                   jax.ShapeDtypeStruct((S, D), F32), jax.ShapeDtypeStruct((S, D), BF16),
                   jax.ShapeDtypeStruct((1, D), F32)],
        compiler_params=_params("arbitrary"), name=name)(x, g, target, y, gate)


def _head_mean(v, gmat):
    hi = v.astype(BF16)
    lo = (v - hi.astype(F32)).astype(BF16)
    return jnp.dot(hi, gmat, preferred_element_type=F32) + jnp.dot(lo, gmat, preferred_element_type=F32)


L_F, L_ONE, L_SHIFT = HEAD_DIM, HEAD_DIM + 3, HEAD_DIM + 6
SHIFT_FREE_LOGIT_BOUND = 60.0


def _lane():
    return lax.broadcasted_iota(jnp.int32, (1, LANES), 1)


def _split3(v):
    p1 = v.astype(BF16).astype(F32)
    r1 = v - p1
    p2 = r1.astype(BF16).astype(F32)
    p3 = (r1 - p2).astype(BF16).astype(F32)
    return p1, p2, p3


def _put3(lane, first, pieces):
    out = jnp.where(lane == first, pieces[0], 0.0)
    for k in (1, 2):
        out = out + jnp.where(lane == first + k, pieces[k], 0.0)
    return out


def _ones3(lane, first):
    return jnp.where((lane >= first) & (lane < first + 3), 1.0, 0.0)


def _lane_col(v, lane, idx):
    return jnp.sum(jnp.where(lane == idx, v, 0.0), axis=-1, keepdims=True)


def _head_of_pair(pair, e, lane):
    return jnp.where(lane < HEAD_DIM, pair if e == 0 else pltpu.roll(pair, HEAD_DIM, 1), 0.0)


def _pair_of_heads(even, odd, lane):
    return jnp.where(lane < HEAD_DIM, even, pltpu.roll(odd, HEAD_DIM, 1))


def _fox_prep_fwd(proj, fcum, qgain, kgain, gmat, *, name, ts=256):
    S = proj.shape[0]
    D = HEADS * HEAD_DIM
    ts = _tile(S, ts)
    scale = HEAD_DIM ** -0.5

    def body(p_ref, f_ref, qg_ref, kg_ref, gm_ref, q_ref, k_ref, v_ref):
        gm, lane, fc = gm_ref[...], _lane(), f_ref[...]
        for cpair in range(HEAD_PAIRS):
            qv = p_ref[:, pl.ds(cpair * LANES, LANES)].astype(F32)
            kv = p_ref[:, pl.ds(D + cpair * LANES, LANES)].astype(F32)
            vv = p_ref[:, pl.ds(2 * D + cpair * LANES, LANES)].astype(F32)
            qn = (qv * lax.rsqrt(_head_mean(qv * qv, gm) + EPS) * qg_ref[...]) * scale
            kn = kv * lax.rsqrt(_head_mean(kv * kv, gm) + EPS) * kg_ref[...]
            for e in range(2):
                h = 2 * cpair + e
                cols = pl.ds(h * LANES, LANES)
                f3 = _split3(_lane_col(fc, lane, h))
                q_ref[:, cols] = (_head_of_pair(qn, e, lane) + _put3(lane, L_F, f3) + _ones3(lane, L_ONE)).astype(BF16)
                k_ref[:, cols] = (_head_of_pair(kn, e, lane) + _ones3(lane, L_F)
                                  - _put3(lane, L_ONE, f3) + _ones3(lane, L_SHIFT)).astype(BF16)
                v_ref[:, cols] = (_head_of_pair(vv, e, lane) + _ones3(lane, L_F)).astype(BF16)

    vec = pl.BlockSpec((1, LANES), lambda i: (0, 0))
    wide = pl.BlockSpec((ts, HEADS * LANES), lambda i: (i, 0))
    return pl.pallas_call(
        body, grid=(S // ts,),
        in_specs=[pl.BlockSpec((ts, 3 * D), lambda i: (i, 0)), pl.BlockSpec((ts, LANES), lambda i: (i, 0)), vec, vec,
                  pl.BlockSpec((LANES, LANES), lambda i: (0, 0))],
        out_specs=[wide, wide, wide], out_shape=[jax.ShapeDtypeStruct((S, HEADS * LANES), BF16)] * 3,
        compiler_params=_params("parallel"), name=name)(proj, fcum, qgain, kgain, gmat)


def _fox_prep_bwd(proj, dq_aug, dk_aug, dv_aug, dog, qgain, kgain, gmat, *, name, ts=256):
    S = proj.shape[0]
    D = HEADS * HEAD_DIM
    ts = _tile(S, ts)
    scale = HEAD_DIM ** -0.5

    def body(p_ref, dq_ref, dk_ref, dv_ref, dog_ref, qg_ref, kg_ref, gm_ref, o_ref, df_ref, dqg_ref, dkg_ref):
        _acc_init(pl.program_id(0), dqg_ref, dkg_ref)
        gm, lane = gm_ref[...], _lane()
        df = jnp.zeros((ts, LANES), F32)
        for cpair in range(HEAD_PAIRS):
            tiles = []
            for e in range(2):
                h = 2 * cpair + e
                cols = pl.ds(h * LANES, LANES)
                tq, tk = dq_ref[:, cols], dk_ref[:, cols]
                df = jnp.where(lane == h, _lane_col(tq, lane, L_F) - _lane_col(tk, lane, L_ONE), df)
                tiles.append((tq, tk, dv_ref[:, cols].astype(F32)))
            pair = [_pair_of_heads(tiles[0][k], tiles[1][k], lane) for k in range(3)]
            for half, g_ref, dg_ref, mult in ((0, qg_ref, dqg_ref, scale), (1, kg_ref, dkg_ref, 1.0)):
                v = p_ref[:, pl.ds(half * D + cpair * LANES, LANES)].astype(F32)
                r = lax.rsqrt(_head_mean(v * v, gm) + EPS)
                xh = v * r
                dn = pair[half] * mult
                dg_ref[...] += _colsum(dn * xh)
                dxh = dn * g_ref[...]
                o_ref[:, pl.ds(half * D + cpair * LANES, LANES)] = (r * (dxh - xh * _head_mean(dxh * xh, gm))).astype(BF16)
            o_ref[:, pl.ds(2 * D + cpair * LANES, LANES)] = pair[2].astype(BF16)
        o_ref[:, pl.ds(3 * D, D)] = dog_ref[...]
        o_ref[:, pl.ds(4 * D, LANES)] = jnp.zeros((ts, LANES), BF16)
        df_ref[...] = df

    row = pl.BlockSpec((ts, D), lambda i: (i, 0))
    wide = pl.BlockSpec((ts, HEADS * LANES), lambda i: (i, 0))
    vec = pl.BlockSpec((1, LANES), lambda i: (0, 0))
    return pl.pallas_call(
        body, grid=(S // ts,),
        in_specs=[pl.BlockSpec((ts, 2 * D), lambda i: (i, 0)), wide, wide, wide, row, vec, vec,
                  pl.BlockSpec((LANES, LANES), lambda i: (0, 0))],
        out_specs=[pl.BlockSpec((ts, 4 * D + LANES), lambda i: (i, 0)), pl.BlockSpec((ts, LANES), lambda i: (i, 0)), vec, vec],
        out_shape=[jax.ShapeDtypeStruct((S, 4 * D + LANES), BF16), jax.ShapeDtypeStruct((S, LANES), F32),
                   jax.ShapeDtypeStruct((1, LANES), F32), jax.ShapeDtypeStruct((1, LANES), F32)],
        compiler_params=_params("arbitrary"), name=name)(proj, dq_aug, dk_aug, dv_aug, dog, qgain, kgain, gmat)


def _log_sigmoid(z):
    return jnp.minimum(z, 0.0) - jnp.log(1.0 + jnp.exp(-jnp.abs(z)))


def _fox_decay_fwd(fl, bf, *, name, tb=256):
    S = fl.shape[0]
    tb = _tile(S, tb)

    def body(fl_ref, b_ref, o_ref, carry):
        @pl.when(pl.program_id(0) == 0)
        def _():
            carry[...] = jnp.zeros_like(carry)
        logf = _log_sigmoid(fl_ref[...] + b_ref[...])
        tri = (lax.broadcasted_iota(jnp.int32, (tb, tb), 1) <= lax.broadcasted_iota(jnp.int32, (tb, tb), 0)).astype(F32)
        cs = jnp.dot(tri, logf, preferred_element_type=F32, precision=lax.Precision.HIGHEST) + carry[...]
        o_ref[...] = cs
        carry[...] = _row_of(cs, tb - 1)

    return pl.pallas_call(
        body, grid=(S // tb,),
        in_specs=[pl.BlockSpec((tb, LANES), lambda i: (i, 0)), pl.BlockSpec((1, LANES), lambda i: (0, 0))],
        out_specs=pl.BlockSpec((tb, LANES), lambda i: (i, 0)),
        out_shape=jax.ShapeDtypeStruct((S, LANES), F32), scratch_shapes=[pltpu.VMEM((1, LANES), F32)],
        compiler_params=_params("arbitrary"), name=name)(fl, bf)


def _fox_decay_bwd(dF, fl, bf, dproj, *, name, tb=256):
    S = fl.shape[0]
    tb = _tile(S, tb)
    n = S // tb
    last_col = dproj.shape[1] // LANES - 1

    def body(df_ref, fl_ref, b_ref, dproj_hbm, o_ref, db_ref, carry):
        del dproj_hbm
        @pl.when(pl.program_id(0) == 0)
        def _():
            carry[...] = jnp.zeros_like(carry)
            db_ref[...] = jnp.zeros_like(db_ref)
        tri = (lax.broadcasted_iota(jnp.int32, (tb, tb), 1) >= lax.broadcasted_iota(jnp.int32, (tb, tb), 0)).astype(F32)
        rc = jnp.dot(tri, df_ref[...], preferred_element_type=F32, precision=lax.Precision.HIGHEST) + carry[...]
        carry[...] = _row_of(rc, 0)
        dfl = rc * jax.nn.sigmoid(-(fl_ref[...] + b_ref[...]))
        o_ref[...] = dfl.astype(BF16)
        db_ref[...] += _colsum(dfl)

    rev = pl.BlockSpec((tb, LANES), lambda i: (n - 1 - i, 0))
    vec = pl.BlockSpec((1, LANES), lambda i: (0, 0))
    return pl.pallas_call(
        body, grid=(n,), in_specs=[rev, rev, vec, pl.BlockSpec(memory_space=pl.ANY)],
        out_specs=[pl.BlockSpec((tb, LANES), lambda i: (n - 1 - i, last_col)), vec],
        out_shape=[jax.ShapeDtypeStruct(dproj.shape, BF16), jax.ShapeDtypeStruct((1, LANES), F32)],
        scratch_shapes=[pltpu.VMEM((1, LANES), F32)], input_output_aliases={3: 0},
        compiler_params=_params("arbitrary"), name=name)(dF, fl, bf, dproj)


_NT = (((1,), (1,)), ((), ()))
_TN = (((0,), (0,)), ((), ()))


def _causal(T, transposed=False):
    r, c = lax.broadcasted_iota(jnp.int32, (T, T), 0), lax.broadcasted_iota(jnp.int32, (T, T), 1)
    return r <= c if transposed else c <= r


def _with_shift(q_tile, shift, lane):
    keep = jnp.where((lane >= L_SHIFT) & (lane < L_SHIFT + 3), 0.0, q_tile)
    return (keep + _put3(lane, L_SHIFT, _split3(-shift))).astype(BF16)


def _ride_along(xchg, n_in, n_out, grid):
    if xchg is None:
        return (lambda body: body), [], [], [], [], []
    arrs, scatter = xchg
    n = len(arrs)

    def wrap(body):
        def wrapped(*refs):
            own_in, x_in = refs[:n_in], refs[n_in:n_in + n]
            own_out, x_out = refs[n_in + n:n_in + n + n_out], refs[n_in + n + n_out:n_in + 2 * n + n_out]
            rest = refs[n_in + 2 * n + n_out:]
            own_scratch, sems = rest[:len(rest) - 3], rest[len(rest) - 3:]
            ids = [pl.program_id(d) for d in range(len(grid))]
            first = functools.reduce(jnp.logical_and, [i == 0 for i in ids])
            last = functools.reduce(jnp.logical_and, [i == g - 1 for i, g in zip(ids, grid)])

            @pl.when(first)
            def _():
                for cp in _xchg_copies(x_in, x_out, scatter, *sems):
                    cp.start()

            body(*own_in, *own_out, *own_scratch)

            @pl.when(last)
            def _():
                for cp in _xchg_copies(x_in, x_out, scatter, *sems):
                    cp.wait()

        return wrapped

    return wrap, [_HBM] * n, [_HBM] * n, _xchg_out_shapes(arrs, scatter), _xchg_sems(n), list(arrs)


def _attn_rowmax(q_aug, k_aug, *, name, T=1024):
    S = q_aug.shape[0]
    T = _tile(S, T)
    n = S // T

    def body(q_ref, k_ref, o_ref, m_s):
        i, j = pl.program_id(1), pl.program_id(2)

        @pl.when(j == 0)
        def _():
            m_s[...] = jnp.full_like(m_s, NEG)

        def step(diag):
            s = lax.dot_general(q_ref[...], k_ref[...], _NT, preferred_element_type=F32)
            if diag:
                s = jnp.where(_causal(T), s, NEG)
            m = m_s[...]
            for cb in range(T // LANES):
                m = jnp.maximum(m, s[:, cb * LANES:(cb + 1) * LANES])
            m_s[...] = m

        @pl.when(j < i)
        def _():
            step(False)

        @pl.when(j == i)
        def _():
            step(True)
            o_ref[...] = _with_shift(q_ref[...].astype(F32), jnp.max(m_s[...], axis=-1, keepdims=True), _lane())

    qrow = pl.BlockSpec((T, LANES), lambda h, i, j: (i, h))
    return pl.pallas_call(
        body, grid=(HEADS, n, n),
        in_specs=[qrow, pl.BlockSpec((T, LANES), lambda h, i, j: (jnp.minimum(j, i), h))],
        out_specs=qrow, out_shape=jax.ShapeDtypeStruct(q_aug.shape, BF16),
        scratch_shapes=[pltpu.VMEM((T, LANES), F32)],
        compiler_params=_params("parallel", "parallel", "arbitrary"), name=name)(q_aug, k_aug)


def _attn_fwd(q_max, k_aug, v_aug, xchg=None, *, name, T=1024):
    S = q_max.shape[0]
    T = _tile(S, T)
    n = S // T
    wrap, x_in, x_out, x_shapes, x_sems, x_ops = _ride_along(xchg, 3, 2, (HEADS, n, n))

    def body(q_ref, k_ref, v_ref, o_ref, qb_ref, acc_s):
        i, j = pl.program_id(1), pl.program_id(2)

        @pl.when(j == 0)
        def _():
            acc_s[...] = jnp.zeros_like(acc_s)

        def block(rows, cols, mask):
            s = lax.dot_general(q_ref[rows, :], k_ref[cols, :], _NT, preferred_element_type=F32)
            if mask is not None:
                s = jnp.where(mask, s, NEG)
            return jnp.dot(jnp.exp(s).astype(BF16), v_ref[cols, :], preferred_element_type=F32)

        @pl.when(j < i)
        def _():
            acc_s[...] += block(pl.ds(0, T), pl.ds(0, T), None)

        @pl.when(j == i)
        def _():
            half = T // 2
            lo, hi = pl.ds(0, half), pl.ds(half, half)
            acc_s[lo, :] += block(lo, lo, _causal(half))
            acc_s[hi, :] += block(hi, lo, None) + block(hi, hi, _causal(half))
            lane = _lane()
            acc = acc_s[...]
            l = _lane_col(acc, lane, L_F)
            o_ref[...] = acc / l
            qf = q_ref[...].astype(F32)
            row_max = -jnp.sum(jnp.where((lane >= L_SHIFT) & (lane < L_SHIFT + 3), qf, 0.0), axis=-1, keepdims=True)
            qb_ref[...] = _with_shift(qf, row_max + jnp.log(l), lane)

    qrow = pl.BlockSpec((T, LANES), lambda h, i, j: (i, h))
    kv = pl.BlockSpec((T, LANES), lambda h, i, j: (jnp.minimum(j, i), h))
    outs = pl.pallas_call(
        wrap(body), grid=(HEADS, n, n), in_specs=[qrow, kv, kv] + x_in, out_specs=[qrow, qrow] + x_out,
        out_shape=[jax.ShapeDtypeStruct(q_max.shape, F32), jax.ShapeDtypeStruct(q_max.shape, BF16)] + x_shapes,
        scratch_shapes=[pltpu.VMEM((T, LANES), F32)] + x_sems,
        compiler_params=_params("arbitrary", "arbitrary", "arbitrary"), name=name)(q_max, k_aug, v_aug, *x_ops)
    return outs[0], outs[1], outs[2:]


def _attn_bwd(q_lse, k_aug, v_aug, do_aug, xchg=None, *, name, T=1024):
    S = q_lse.shape[0]
    T = _tile(S, T)
    n = S // T
    wrap, x_in, x_out, x_shapes, x_sems, x_ops = _ride_along(xchg, 4, 3, (HEADS, n, n))

    def body(q_ref, do_ref, k_ref, v_ref, dq_ref, dk_ref, dv_ref, dq_s, dk_s, dv_s):
        j, i = pl.program_id(1), pl.program_id(2)

        def block(keys, queries, mask):
            q, do, k, v = q_ref[queries, :], do_ref[queries, :], k_ref[keys, :], v_ref[keys, :]
            st = lax.dot_general(k, q, _NT, preferred_element_type=F32)
            if mask is not None:
                st = jnp.where(mask, st, NEG)
            pt = jnp.exp(st)
            dst = (pt * lax.dot_general(v, do, _NT, preferred_element_type=F32)).astype(BF16)
            dv_s[keys, :] += jnp.dot(pt.astype(BF16), do, preferred_element_type=F32)
            dk_s[keys, :] += jnp.dot(dst, q, preferred_element_type=F32)
            return lax.dot_general(dst, k, _TN, preferred_element_type=F32)

        @pl.when(i == j)
        def _():
            dk_s[...] = jnp.zeros_like(dk_s)
            dv_s[...] = jnp.zeros_like(dv_s)

            @pl.when(j == 0)
            def _():
                dq_s[i] = jnp.zeros((T, LANES), F32)

            half = T // 2
            lo, hi = pl.ds(0, half), pl.ds(half, half)
            dq_s[i, lo, :] += block(lo, lo, _causal(half, transposed=True))
            dq_s[i, hi, :] += block(lo, hi, None) + block(hi, hi, _causal(half, transposed=True))
            dq_ref[...] = dq_s[j]

        @pl.when(i > j)
        def _():
            upd = block(pl.ds(0, T), pl.ds(0, T), None)

            @pl.when(j == 0)
            def _():
                dq_s[i] = upd

            @pl.when(j > 0)
            def _():
                dq_s[i] += upd

        @pl.when(i == n - 1)
        def _():
            dk_ref[...] = dk_s[...]
            dv_ref[...] = dv_s[...].astype(BF16)

    qrow = pl.BlockSpec((T, LANES), lambda h, j, i: (jnp.maximum(i, j), h))
    kv = pl.BlockSpec((T, LANES), lambda h, j, i: (j, h))
    outs = pl.pallas_call(
        wrap(body), grid=(HEADS, n, n), in_specs=[qrow, qrow, kv, kv] + x_in, out_specs=[kv, kv, kv] + x_out,
        out_shape=[jax.ShapeDtypeStruct(q_lse.shape, F32), jax.ShapeDtypeStruct(q_lse.shape, F32),
                   jax.ShapeDtypeStruct(q_lse.shape, BF16)] + x_shapes,
        scratch_shapes=[pltpu.VMEM((n, T, LANES), F32), pltpu.VMEM((T, LANES), F32), pltpu.VMEM((T, LANES), F32)] + x_sems,
        compiler_params=_params("arbitrary", "arbitrary", "arbitrary"), name=name)(q_lse, do_aug, k_aug, v_aug, *x_ops)
    return outs[0], outs[1], outs[2], outs[3:]


def _fox_gate_fwd(att_aug, proj, *, name, ts=256):
    S = att_aug.shape[0]
    D = HEADS * HEAD_DIM
    ts = _tile(S, ts)

    def body(a_ref, o_ref, att_ref, out_ref):
        lane = _lane()
        for cpair in range(HEAD_PAIRS):
            cols = pl.ds(cpair * LANES, LANES)
            pair = _pair_of_heads(a_ref[:, pl.ds(2 * cpair * LANES, LANES)], a_ref[:, pl.ds((2 * cpair + 1) * LANES, LANES)], lane)
            att_ref[:, cols] = pair
            out_ref[:, cols] = (pair * jax.nn.sigmoid(o_ref[:, cols].astype(F32))).astype(BF16)

    row = pl.BlockSpec((ts, D), lambda i: (i, 0))
    return pl.pallas_call(
        body, grid=(S // ts,),
        in_specs=[pl.BlockSpec((ts, HEADS * LANES), lambda i: (i, 0)), pl.BlockSpec((ts, D), lambda i: (i, 3))],
        out_specs=[row, row], out_shape=[jax.ShapeDtypeStruct((S, D), F32), jax.ShapeDtypeStruct((S, D), BF16)],
        compiler_params=_params("parallel"), name=name)(att_aug, proj)


def _fox_gate_bwd(da, att, proj, *, name, ts=256):
    S, D = att.shape
    ts = _tile(S, ts)

    def body(da_ref, a_ref, o_ref, do_ref, dog_ref):
        lane = _lane()
        for cpair in range(HEAD_PAIRS):
            cols = pl.ds(cpair * LANES, LANES)
            dav, av = da_ref[:, cols].astype(F32), a_ref[:, cols]
            sg = jax.nn.sigmoid(o_ref[:, cols].astype(F32))
            datt = (dav * sg).astype(BF16).astype(F32)
            dog_ref[:, cols] = (dav * av * sg * (1.0 - sg)).astype(BF16)
            prod = datt * av
            for e in range(2):
                in_head = (lane < HEAD_DIM) if e == 0 else (lane >= HEAD_DIM)
                delta = jnp.sum(jnp.where(in_head, prod, 0.0), axis=-1, keepdims=True)
                tile = _head_of_pair(datt, e, lane) + _put3(lane, L_F, _split3(-delta))
                do_ref[:, pl.ds((2 * cpair + e) * LANES, LANES)] = tile.astype(BF16)

    row = pl.BlockSpec((ts, D), lambda i: (i, 0))
    return pl.pallas_call(
        body, grid=(S // ts,), in_specs=[row, row, pl.BlockSpec((ts, D), lambda i: (i, 3))],
        out_specs=[pl.BlockSpec((ts, HEADS * LANES), lambda i: (i, 0)), row],
        out_shape=[jax.ShapeDtypeStruct((S, HEADS * LANES), BF16), jax.ShapeDtypeStruct((S, D), BF16)],
        compiler_params=_params("parallel"), name=name)(da, att, proj)


def _row_of(block, r):
    rows = lax.broadcasted_iota(jnp.int32, block.shape, 0)
    return jnp.sum(jnp.where(rows == r, block, 0.0), axis=0, keepdims=True)


def _shift_down(cur, tail, k):
    out = pltpu.roll(cur, k, 0)
    rows = lax.broadcasted_iota(jnp.int32, cur.shape, 0)
    for r in range(k):
        out = jnp.where(rows == r, _row_of(tail, tail.shape[0] - k + r), out)
    return out


def _shift_up(cur, head, k):
    n = cur.shape[0]
    out = pltpu.roll(cur, n - k, 0)
    rows = lax.broadcasted_iota(jnp.int32, cur.shape, 0)
    for r in range(k):
        out = jnp.where(rows == n - k + r, _row_of(head, r), out)
    return out


HALO = 16


CONV_TC = 1408


def _pair_tiles(v):
    nc = v.shape[-1] // (2 * CONV_TC)
    return jnp.swapaxes(v.reshape(v.shape[:-1] + (2, nc, CONV_TC)), -3, -2).reshape(v.shape)


def _unpair_tiles(v):
    nc = v.shape[-1] // (2 * CONV_TC)
    return jnp.swapaxes(v.reshape(v.shape[:-1] + (nc, 2, CONV_TC)), -3, -2).reshape(v.shape)


def _conv_rows(cur, tail, w_ref, b_ref, cols):
    a1, a2 = _shift_down(cur, tail, 1), _shift_down(cur, tail, 2)
    return a2 * w_ref[0:1, cols] + a1 * w_ref[1:2, cols] + cur * w_ref[2:3, cols] + b_ref[:, cols], (a2, a1, cur)


def _conv_gate_fwd(a, cw, cb, *, name, ts=512):
    S, F2 = a.shape
    tc = CONV_TC
    ts = _tile(S, ts)
    nc = F2 // (2 * tc)
    sub = ts // HALO
    halves = (pl.ds(0, tc), pl.ds(tc, tc))

    def body(a_ref, t_ref, w_ref, b_ref, o_ref):
        first = pl.program_id(1) == 0
        pre = []
        for cols in halves:
            tail = jnp.where(first, 0.0, t_ref[:, cols].astype(F32))
            pre.append(_conv_rows(a_ref[:, cols].astype(F32), tail, w_ref, b_ref, cols)[0])
        g, val = pre
        o_ref[...] = (g * jax.nn.sigmoid(g) * val).astype(BF16)

    return pl.pallas_call(
        body, grid=(nc, S // ts),
        in_specs=[pl.BlockSpec((ts, 2 * tc), lambda j, i: (i, j)),
                  pl.BlockSpec((HALO, 2 * tc), lambda j, i: (jnp.maximum(i * sub - 1, 0), j)),
                  pl.BlockSpec((CONV_WIDTH, 2 * tc), lambda j, i: (0, j)), pl.BlockSpec((1, 2 * tc), lambda j, i: (0, j))],
        out_specs=pl.BlockSpec((ts, tc), lambda j, i: (i, j)),
        out_shape=jax.ShapeDtypeStruct((S, F2 // 2), BF16),
        compiler_params=_params("parallel", "parallel"), name=name)(a, a, cw, cb)


def _conv_gate_bwd(a, dact, cw, cb, *, name, ts=512):
    S, F2 = a.shape
    tc = CONV_TC
    ts = _tile(S, ts)
    nc = F2 // (2 * tc)
    sub = ts // HALO
    n_rows = S // ts
    halves = (pl.ds(0, tc), pl.ds(tc, tc))

    def body(a_ref, at_ref, ah_ref, d_ref, dh_ref, w_ref, b_ref, da_ref, s_ref):
        i = pl.program_id(1)
        _acc_init(i, s_ref)

        def dpre_of(rows, tails, d):
            (g, taps_g), (val, taps_v) = [_conv_rows(rows[h], tails[h], w_ref, b_ref, halves[h]) for h in range(2)]
            sg = jax.nn.sigmoid(g)
            return (d * val * (sg * (1.0 + g * (1.0 - sg))), d * (g * sg)), (taps_g, taps_v)

        cur = [a_ref[:, c].astype(F32) for c in halves]
        tail = [jnp.where(i == 0, 0.0, at_ref[:, c].astype(F32)) for c in halves]
        dpre, taps = dpre_of(cur, tail, d_ref[...].astype(F32))
        head, _ = dpre_of([ah_ref[:, c].astype(F32) for c in halves], [x[ts - HALO:, :] for x in cur], dh_ref[...].astype(F32))
        for h, cols in enumerate(halves):
            dd = dpre[h]
            nxt = jnp.where(i == n_rows - 1, 0.0, head[h])
            da_ref[:, cols] = (dd * w_ref[2:3, cols] + _shift_up(dd, nxt, 1) * w_ref[1:2, cols]
                               + _shift_up(dd, nxt, 2) * w_ref[0:1, cols]).astype(BF16)
            for r in range(CONV_WIDTH):
                s_ref[r:r + 1, cols] += _colsum(dd * taps[h][r])
            s_ref[CONV_WIDTH:CONV_WIDTH + 1, cols] += _colsum(dd)

    nxt_rows = lambda i: jnp.minimum((i + 1) * sub, S // HALO - 1)
    return pl.pallas_call(
        body, grid=(nc, n_rows),
        in_specs=[pl.BlockSpec((ts, 2 * tc), lambda j, i: (i, j)),
                  pl.BlockSpec((HALO, 2 * tc), lambda j, i: (jnp.maximum(i * sub - 1, 0), j)),
                  pl.BlockSpec((HALO, 2 * tc), lambda j, i: (nxt_rows(i), j)),
                  pl.BlockSpec((ts, tc), lambda j, i: (i, j)), pl.BlockSpec((HALO, tc), lambda j, i: (nxt_rows(i), j)),
                  pl.BlockSpec((CONV_WIDTH, 2 * tc), lambda j, i: (0, j)), pl.BlockSpec((1, 2 * tc), lambda j, i: (0, j))],
        out_specs=[pl.BlockSpec((ts, 2 * tc), lambda j, i: (i, j)), pl.BlockSpec((8, 2 * tc), lambda j, i: (0, j))],
        out_shape=[jax.ShapeDtypeStruct((S, F2), BF16), jax.ShapeDtypeStruct((8, F2), F32)],
        compiler_params=_params("parallel", "arbitrary"), name=name)(a, a, a, dact, dact, cw, cb)


def _gelu_parts(z):
    z2 = z * z
    t = jnp.tanh(GELU_C0 * (z + GELU_C1 * z * z2))
    val = 0.5 * z * (1.0 + t)
    grad = 0.5 * (1.0 + t) + 0.5 * z * (1.0 - t * t) * GELU_C0 * (1.0 + 3.0 * GELU_C1 * z2)
    return val, grad


def _sgu_fwd(pre, b_in, vgain, vbias, wm, bsb, *, name, ts=256):
    S, W2 = pre.shape
    W = W2 // 2
    gd = W // SGU_GROUPS
    ts = _tile(S, ts)

    def body(p_ref, b_ref, vg_ref, vb_ref, wm_ref, bs_ref, y_ref):
        u = _gelu_parts(p_ref[:, pl.ds(0, W)].astype(F32) + b_ref[:, pl.ds(0, W)])[0]
        v = _gelu_parts(p_ref[:, pl.ds(W, W)].astype(F32) + b_ref[:, pl.ds(W, W)])[0]
        mu = jnp.mean(v, axis=-1, keepdims=True)
        vc = v - mu
        rstd = lax.rsqrt(jnp.mean(vc * vc, axis=-1, keepdims=True) + EPS)
        vn = ((vc * rstd) * vg_ref[...] + vb_ref[...]).astype(BF16)
        for blk in range(ts // SGU_BLOCK):
            r0 = blk * SGU_BLOCK
            for g in range(SGU_GROUPS):
                c0 = g * gd
                mixed = jnp.dot(wm_ref[g], vn[r0:r0 + SGU_BLOCK, c0:c0 + gd], preferred_element_type=F32) + bs_ref[g]
                y_ref[pl.ds(r0, SGU_BLOCK), pl.ds(c0, gd)] = (u[r0:r0 + SGU_BLOCK, c0:c0 + gd] * mixed).astype(BF16)

    full = lambda shape: pl.BlockSpec(shape, lambda i: (0,) * len(shape))
    return pl.pallas_call(
        body, grid=(S // ts,),
        in_specs=[pl.BlockSpec((ts, W2), lambda i: (i, 0)), full((1, W2)), full((1, W)), full((1, W)),
                  full((SGU_GROUPS, SGU_BLOCK, SGU_BLOCK)), full((SGU_GROUPS, SGU_BLOCK, gd))],
        out_specs=pl.BlockSpec((ts, W), lambda i: (i, 0)), out_shape=jax.ShapeDtypeStruct((S, W), BF16),
        compiler_params=_params("parallel"), name=name)(pre, b_in, vgain, vbias, wm, bsb)


def _sgu_bwd(pre, dy, b_in, vgain, vbias, wm, wmt, bsb, *, name, ts=256):
    S, W2 = pre.shape
    W = W2 // 2
    gd = W // SGU_GROUPS
    ts = _tile(S, ts)
    last = S // ts - 1

    def body(p_ref, dy_ref, b_ref, vg_ref, vb_ref, wm_ref, wmt_ref, bs_ref,
             dp_ref, db_ref, dvg_ref, dvb_ref, dws_ref, dbs_ref, du_s, dvn_s, dbs_s):
        step = pl.program_id(0)
        _acc_init(step, db_ref, dvg_ref, dvb_ref, dws_ref, dbs_s)
        u, gu = _gelu_parts(p_ref[:, pl.ds(0, W)].astype(F32) + b_ref[:, pl.ds(0, W)])
        v, gv = _gelu_parts(p_ref[:, pl.ds(W, W)].astype(F32) + b_ref[:, pl.ds(W, W)])
        mu = jnp.mean(v, axis=-1, keepdims=True)
        vc = v - mu
        rstd = lax.rsqrt(jnp.mean(vc * vc, axis=-1, keepdims=True) + EPS)
        vhat = vc * rstd
        vn = (vhat * vg_ref[...] + vb_ref[...]).astype(BF16)
        dyv = dy_ref[...].astype(F32)
        for blk in range(ts // SGU_BLOCK):
            r0 = blk * SGU_BLOCK
            for g in range(SGU_GROUPS):
                c0 = g * gd
                vn_g = vn[r0:r0 + SGU_BLOCK, c0:c0 + gd]
                dy_g = dyv[r0:r0 + SGU_BLOCK, c0:c0 + gd]
                mixed = jnp.dot(wm_ref[g], vn_g, preferred_element_type=F32) + bs_ref[g]
                dmix = dy_g * u[r0:r0 + SGU_BLOCK, c0:c0 + gd]
                dmix_b = dmix.astype(BF16)
                du_s[pl.ds(r0, SGU_BLOCK), pl.ds(c0, gd)] = dy_g * mixed
                dvn_s[pl.ds(r0, SGU_BLOCK), pl.ds(c0, gd)] = jnp.dot(wmt_ref[g], dmix_b, preferred_element_type=F32)
                dws_ref[g] += lax.dot_general(dmix_b, vn_g, _NT, preferred_element_type=F32)
                dbs_s[g] += dmix
        dvn = dvn_s[...]
        dvg_ref[...] += _colsum(dvn * vhat)
        dvb_ref[...] += _colsum(dvn)
        dvh = dvn * vg_ref[...]
        dv = rstd * (dvh - jnp.mean(dvh, axis=-1, keepdims=True) - vhat * jnp.mean(dvh * vhat, axis=-1, keepdims=True))
        dpu = du_s[...] * gu
        dpv = dv * gv
        dp_ref[:, pl.ds(0, W)] = dpu.astype(BF16)
        dp_ref[:, pl.ds(W, W)] = dpv.astype(BF16)
        db_ref[:, pl.ds(0, W)] += _colsum(dpu)
        db_ref[:, pl.ds(W, W)] += _colsum(dpv)

        @pl.when(step == last)
        def _():
            for g in range(SGU_GROUPS):
                dbs_ref[g] = jnp.broadcast_to(jnp.sum(dbs_s[g], axis=-1, keepdims=True), (SGU_BLOCK, SGU_BLOCK))

    full = lambda shape: pl.BlockSpec(shape, lambda i: (0,) * len(shape))
    gsq = (SGU_GROUPS, SGU_BLOCK, SGU_BLOCK)
    return pl.pallas_call(
        body, grid=(S // ts,),
        in_specs=[pl.BlockSpec((ts, W2), lambda i: (i, 0)), pl.BlockSpec((ts, W), lambda i: (i, 0)),
                  full((1, W2)), full((1, W)), full((1, W)), full(gsq), full(gsq), full((SGU_GROUPS, SGU_BLOCK, gd))],
        out_specs=[pl.BlockSpec((ts, W2), lambda i: (i, 0)), full((1, W2)), full((1, W)), full((1, W)), full(gsq), full(gsq)],
        out_shape=[jax.ShapeDtypeStruct((S, W2), BF16), jax.ShapeDtypeStruct((1, W2), F32),
                   jax.ShapeDtypeStruct((1, W), F32), jax.ShapeDtypeStruct((1, W), F32),
                   jax.ShapeDtypeStruct(gsq, F32), jax.ShapeDtypeStruct(gsq, F32)],
        scratch_shapes=[pltpu.VMEM((ts, W), F32), pltpu.VMEM((ts, W), F32), pltpu.VMEM((SGU_GROUPS, SGU_BLOCK, gd), F32)],
        compiler_params=_params("arbitrary"), name=name)(pre, dy, b_in, vgain, vbias, wm, wmt, bsb)


def _paired_to_natural(w_up):
    nc = w_up.shape[1] // (2 * CONV_TC)
    return lambda q: (q % 2) * nc + q // 2


def _ffn_fwd(x, mods, n2g, w_up, cw, cb, w_down, tag):
    sh, sc, gate = mods
    h = _norm_mod_fwd(x, n2g, sh, sc, name=f"{tag}_norm_fwd")
    a = _mm(h, w_up, out_dtype=BF16, tn=CONV_TC, b_n=_paired_to_natural(w_up), name=f"{tag}_up")
    act = _conv_gate_fwd(a, cw, cb, name=f"{tag}_conv_fwd")
    x_out, y = _mm(act, w_down, tk=1408, res=(x, gate), name=f"{tag}_down")
    return x_out, (x, h, a, act, y)


def _ffn_bwd(dy, saved, mods, n2g, w_up, cw, cb, w_down, dres, prev, tag):
    x, h, a, act, _ = saved
    sh, sc, gate = mods
    dact = _mm(dy, w_down, tb=True, out_dtype=BF16, tn=1408, name=f"{tag}_down_dx")
    dw_down = _mm(act, dy, ta=True, out_dtype=BF16, tm=1408, name=f"{tag}_down_dw")
    da, sums = _conv_gate_bwd(a, dact, cw, cb, name=f"{tag}_conv_bwd")
    dh = _mm(da, w_up, tb=True, tk=CONV_TC, b_k=_paired_to_natural(w_up), name=f"{tag}_up_dx")
    dw_up = _mm(h, da, ta=True, out_dtype=BF16, tn=CONV_TC, o_n=_paired_to_natural(w_up), name=f"{tag}_up_dw")
    outs = _norm_mod_bwd(dh, x, n2g, sc, dres, prev, name=f"{tag}_norm_bwd")
    sums = _unpair_tiles(sums)
    return outs, dict(w_up=dw_up, w_down=dw_down, conv_w=sums[0:CONV_WIDTH], conv_b=sums[CONV_WIDTH])


def _local_step(x, target, w, mods, late=None, early=None, last=None):
    S, D = x.shape
    lane = jnp.arange(LANES)
    gmat = jnp.where((lane[:, None] // HEAD_DIM) == (lane[None, :] // HEAD_DIM), 1.0 / HEAD_DIM, 0.0).astype(BF16)
    qg2 = jnp.tile(w["fox_q_gain"].reshape(1, HEAD_DIM), (1, 2))
    kg2 = jnp.tile(w["fox_k_gain"].reshape(1, HEAD_DIM), (1, 2))
    bf_pad = jnp.pad(w["fox_b_f"].reshape(1, HEADS), ((0, 0), (0, LANES - HEADS)))
    w_in_pad = jnp.pad(w["fox_w_in"], ((0, 0), (0, 4 * D + LANES - w["fox_w_in"].shape[1])))
    w_qkvo, w_f = w_in_pad[:, :4 * D], w_in_pad[:, 4 * D:]
    tpos = jnp.arange(SGU_BLOCK)
    smask = (tpos[None, :] // SGU_CHUNK) <= (tpos[:, None] // SGU_CHUNK)
    wm32 = jnp.where(smask[None], w["sgu_w_s"], 0.0)
    wm, wmt = wm32.astype(BF16), jnp.swapaxes(wm32, 1, 2).astype(BF16)
    gd = w["sgu_v_gain"].shape[-1] // SGU_GROUPS
    bsb = jnp.broadcast_to(w["sgu_b_s"][:, :, None], (SGU_GROUPS, SGU_BLOCK, gd))
    vec = lambda v: v.reshape(1, -1)

    sh1, sc1, g1 = mods[0][0:3]
    h0 = _norm_mod_fwd(x, vec(w["norm1_g"][0]), sh1, sc1, name="fox_norm_fwd")
    proj = _mm(h0, w_qkvo, out_dtype=BF16, name="fox_proj")
    fl = _mm(h0, w_f, name="fox_forget_proj")
    fcum = _fox_decay_fwd(fl, bf_pad, name="fox_decay")
    q_aug, k_aug, v_aug = _fox_prep_fwd(proj, fcum, qg2, kg2, gmat, name="fox_qk_norm")
    logit_bound = 8.0 * jnp.max(jnp.abs(w["fox_q_gain"])) * jnp.max(jnp.abs(w["fox_k_gain"]))
    q_max = lax.cond(logit_bound <= SHIFT_FREE_LOGIT_BOUND, lambda: q_aug,
                     lambda: _attn_rowmax(q_aug, k_aug, name="fox_attn_rowmax"))
    xchg = None if late is None else (late[0], [False] * len(late[0]))
    att_aug, q_lse, gathered = _attn_fwd(q_max, k_aug, v_aug, xchg, name="fox_attn_fwd")
    if late is not None:
        w = {**w, **late[1](gathered)}
    w = dict(w, ffn_conv_w=_pair_tiles(w["ffn_conv_w"]), ffn_conv_b=_pair_tiles(w["ffn_conv_b"]))
    att, ag = _fox_gate_fwd(att_aug, proj, name="fox_gate_fwd")
    x1, y_fox = _mm(ag, w["fox_w_out"], res=(x, g1), name="fox_out")
    x2, ffn0 = _ffn_fwd(x1, mods[0][3:6], vec(w["norm2_g"][0]), w["ffn_w_up"][0], w["ffn_conv_w"][0],
                        vec(w["ffn_conv_b"][0]), w["ffn_w_down"][0], "ffn0")

    sh1b, sc1b, g1b = mods[1][0:3]
    h1 = _norm_mod_fwd(x2, vec(w["norm1_g"][1]), sh1b, sc1b, name="sgu_norm_fwd")
    pre = _mm(h1, w["sgu_w_in"], out_dtype=BF16, name="sgu_in")
    b_in, vg, vb = vec(w["sgu_b_in"]), vec(w["sgu_v_gain"]), vec(w["sgu_v_bias"])
    ys = _sgu_fwd(pre, b_in, vg, vb, wm, bsb, name="sgu_core_fwd")
    x3, y_sgu = _mm(ys, w["sgu_w_out"], res=(x2, g1b), name="sgu_out")
    x4, ffn1 = _ffn_fwd(x3, mods[1][3:6], vec(w["norm2_g"][1]), w["ffn_w_up"][1], w["ffn_conv_w"][1],
                        vec(w["ffn_conv_b"][1]), w["ffn_w_down"][1], "ffn1")

    loss, d_final_g, dx4, dy_ffn1, dgate_ffn1 = _final_loss(x4, vec(w["final_g"]), target, ffn1[4], mods[1][5], name="final_loss")

    (dx3, dn2g_1, dsh2_1, dsc2_1, dy_sgu, dgate_sgu), g_ffn1 = _ffn_bwd(
        dy_ffn1, ffn1, mods[1][3:6], vec(w["norm2_g"][1]), w["ffn_w_up"][1], w["ffn_conv_w"][1], vec(w["ffn_conv_b"][1]),
        w["ffn_w_down"][1], dx4, (y_sgu, g1b), "ffn1")

    dys = _mm(dy_sgu, w["sgu_w_out"], tb=True, out_dtype=BF16, name="sgu_out_dx")
    dw_sgu_out = _mm(ys, dy_sgu, ta=True, out_dtype=BF16, name="sgu_out_dw")
    dpre, db_in, dvg, dvb, dws, dbs = _sgu_bwd(pre, dys, b_in, vg, vb, wm, wmt, bsb, name="sgu_core_bwd")
    dh1 = _mm(dpre, w["sgu_w_in"], tb=True, name="sgu_in_dx")
    dw_sgu_in = _mm(h1, dpre, ta=True, out_dtype=BF16, name="sgu_in_dw")
    dx2, dn1g_1, dsh1_1, dsc1_1, dy_ffn0, dgate_ffn0 = _norm_mod_bwd(
        dh1, x2, vec(w["norm1_g"][1]), sc1b, dx3, (ffn0[4], mods[0][5]), name="sgu_norm_bwd")

    (dx1, dn2g_0, dsh2_0, dsc2_0, dy_fox, dgate_fox), g_ffn0 = _ffn_bwd(
        dy_ffn0, ffn0, mods[0][3:6], vec(w["norm2_g"][0]), w["ffn_w_up"][0], w["ffn_conv_w"][0], vec(w["ffn_conv_b"][0]),
        w["ffn_w_down"][0], dx2, (y_fox, g1), "ffn0")

    dag = _mm(dy_fox, w["fox_w_out"], tb=True, out_dtype=BF16, name="fox_out_dx")
    dw_fox_out = _mm(ag, dy_fox, ta=True, out_dtype=BF16, name="fox_out_dw")
    do_aug, dog = _fox_gate_bwd(dag, att, proj, name="fox_gate_bwd")
    grads = dict(
        sgu_w_in=dw_sgu_in, sgu_b_in=db_in[0], sgu_v_gain=dvg[0], sgu_v_bias=dvb[0],
        sgu_w_s=jnp.where(smask[None], dws, 0.0), sgu_b_s=dbs[:, :, 0], sgu_w_out=dw_sgu_out,
        ffn_w_up=jnp.stack([g_ffn0["w_up"], g_ffn1["w_up"]]),
        ffn_conv_w=jnp.stack([g_ffn0["conv_w"], g_ffn1["conv_w"]]),
        ffn_conv_b=jnp.stack([g_ffn0["conv_b"], g_ffn1["conv_b"]]),
        ffn_w_down=jnp.stack([g_ffn0["w_down"], g_ffn1["w_down"]]),
        final_g=d_final_g[0], fox_w_out=dw_fox_out,
    )
    xchg = None
    if early is not None:
        blocks = early(grads)
        xchg = (blocks, [True] * len(blocks))
    dq_aug, dk_aug, dv_aug, exchanged = _attn_bwd(q_lse, k_aug, v_aug, do_aug, xchg, name="fox_attn_bwd")
    dproj, dF, dqg, dkg = _fox_prep_bwd(proj, dq_aug, dk_aug, dv_aug, dog, qg2, kg2, gmat, name="fox_qk_norm_bwd")
    dproj, dbf = _fox_decay_bwd(dF, fl, bf_pad, dproj, name="fox_decay_bwd")
    dw_fox_in = _mm(h0, dproj, ta=True, out_dtype=BF16, tn=1408, name="fox_proj_dw")[:, :w["fox_w_in"].shape[1]]
    xchg = None
    if last is not None:
        blocks = last(dict(fox_w_in=dw_fox_in))
        xchg = (blocks, [True] * len(blocks))
    dh0 = _mm(dproj, w_in_pad, tb=True, tk=1408, xchg=xchg, name="fox_proj_dx")
    dh0, exchanged_last = dh0 if last is not None else (dh0, [])
    dx0, dn1g_0, dsh1_0, dsc1_0 = _norm_mod_bwd(dh0, x, vec(w["norm1_g"][0]), sc1, dx1, None, name="fox_norm_bwd")

    dmod0 = jnp.concatenate([dsh1_0, dsc1_0, dgate_fox, dsh2_0, dsc2_0, dgate_ffn0], axis=1)
    dmod1 = jnp.concatenate([dsh1_1, dsc1_1, dgate_sgu, dsh2_1, dsc2_1, dgate_ffn1], axis=1)
    grads.update(
        fox_w_in=dw_fox_in,
        fox_b_f=dbf[0, :HEADS],
        fox_q_gain=dqg[0, :HEAD_DIM] + dqg[0, HEAD_DIM:],
        fox_k_gain=dkg[0, :HEAD_DIM] + dkg[0, HEAD_DIM:],
        fox_w_out=dw_fox_out,
        ada_b=jnp.concatenate([dmod0, dmod1], axis=0),
        norm1_g=jnp.concatenate([dn1g_0, dn1g_1], axis=0), norm2_g=jnp.concatenate([dn2g_0, dn2g_1], axis=0),
    )
    return loss[0, 0], dx0, grads, exchanged, exchanged_last


_HBM = pl.BlockSpec(memory_space=pl.ANY)
N_PEER = N_DEV - 1


def _xchg_out_shapes(arrs, scatter):
    return [jax.ShapeDtypeStruct(a.shape if s else (N_DEV,) + a.shape, a.dtype) for a, s in zip(arrs, scatter)]


def _xchg_sems(n):
    return [pltpu.SemaphoreType.DMA((n * N_PEER,)), pltpu.SemaphoreType.DMA((n * N_PEER,)), pltpu.SemaphoreType.DMA((n,))]


def _xchg_copies(ins, outs, scatter, send, recv, loc):
    x, y, c = lax.axis_index("x"), lax.axis_index("y"), lax.axis_index("c")
    me = 4 * x + 2 * y + c
    copies = []
    for a in range(len(ins)):
        copies.append(pltpu.make_async_copy(ins[a].at[me] if scatter[a] else ins[a], outs[a].at[me], loc.at[a]))
        for k in range(1, N_DEV):
            px = 1 - x if k & 4 else x
            py = 1 - y if k & 2 else y
            pc = 1 - c if k & 1 else c
            copies.append(pltpu.make_async_remote_copy(
                src_ref=ins[a].at[4 * px + 2 * py + pc] if scatter[a] else ins[a], dst_ref=outs[a].at[me],
                send_sem=send.at[a * N_PEER + k - 1], recv_sem=recv.at[a * N_PEER + k - 1],
                device_id=(px, py, pc), device_id_type=MESH))
    return copies


def _exchange(arrs, scatter, *, name):
    n = len(arrs)

    def body(*refs):
        copies = _xchg_copies(refs[:n], refs[n:2 * n], scatter, *refs[2 * n:])
        for cp in copies:
            cp.start()
        for cp in copies:
            cp.wait()

    return pl.pallas_call(
        body, in_specs=[_HBM] * n, out_specs=[_HBM] * n, out_shape=_xchg_out_shapes(arrs, scatter),
        scratch_shapes=_xchg_sems(n),
        compiler_params=pltpu.CompilerParams(has_side_effects=True), name=name)(*arrs)


def _adamw(w, parts, m, v, *, name, tr=256):
    L, R, C = w.shape
    P = parts.shape[0]
    tr = next(t for t in range(min(R, tr), 0, -1) if R % t == 0 and (t % 16 == 0 or t == R))
    nr = R // tr
    c1 = 1.0 - ADAM_B1 ** ADAM_STEP
    c2 = 1.0 - ADAM_B2 ** ADAM_STEP

    def body(w_ref, p_ref, m_ref, v_ref, g_ref, d_ref, mo_ref, vo_ref):
        g = p_ref[0].astype(F32)
        for p in range(1, P):
            g = g + p_ref[p].astype(F32)
        mn = ADAM_B1 * m_ref[0] + (1.0 - ADAM_B1) * g
        vn = ADAM_B2 * v_ref[0] + (1.0 - ADAM_B2) * (g * g)
        g_ref[0] = g
        mo_ref[0] = mn
        vo_ref[0] = vn
        d_ref[0] = -ADAM_LR * ((mn / c1) / (jnp.sqrt(vn / c2) + ADAM_EPS) + ADAM_WD * w_ref[0])

    row = pl.BlockSpec((1, tr, C), lambda l, i: (l, i, 0))
    return pl.pallas_call(
        body, grid=(L, nr), in_specs=[row, pl.BlockSpec((P, tr, C), lambda l, i: (0, l * nr + i, 0)), row, row],
        out_specs=[row] * 4, out_shape=[jax.ShapeDtypeStruct((L, R, C), F32)] * 4,
        compiler_params=_params("parallel", "parallel"), name=name)(w, parts, m, v)


def _sum_parts(parts, *, name):
    P, R, C = parts.shape

    def body(p_ref, o_ref):
        g = p_ref[0]
        for p in range(1, P):
            g = g + p_ref[p]
        o_ref[...] = g

    return pl.pallas_call(body, out_shape=jax.ShapeDtypeStruct((R, C), F32), name=name)(parts)


WEIGHTS = ["fox_w_in", "fox_b_f", "fox_q_gain", "fox_k_gain", "fox_w_out", "sgu_w_in", "sgu_b_in", "sgu_v_gain",
           "sgu_v_bias", "sgu_w_s", "sgu_b_s", "sgu_w_out", "ffn_w_up", "ffn_conv_w", "ffn_conv_b", "ffn_w_down",
           "ada_w", "ada_b", "norm1_g", "norm2_g", "final_g"]
BIG_AXIS = dict(fox_w_in=1, fox_w_out=0, sgu_w_in=1, sgu_w_out=0, ffn_w_up=1, ffn_w_down=0, ada_w=1)
SMALL_SHARDED = ["sgu_b_in", "sgu_v_gain", "sgu_v_bias", "ffn_conv_w"]
SINGLE_LAYER = ("fox_", "sgu_")
BEFORE_ATTENTION = ["fox_w_in"]
AFTER_ATTENTION = ["fox_w_out", "sgu_w_in", "sgu_w_out", "ffn_w_up", "ffn_w_down"]


def _assemble(stacked, layers, axis):
    _, lr, cc = stacked.shape
    r = lr // layers
    s4 = stacked.reshape(N_DEV, layers, r, cc)
    if axis == 0:
        return s4.transpose(1, 0, 2, 3).reshape(layers, N_DEV * r, cc)
    return s4.transpose(1, 2, 0, 3).reshape(layers, r, N_DEV * cc)


def _disassemble(full, axis):
    layers, R, C = full.shape
    if axis == 0:
        r = R // N_DEV
        return full.reshape(layers, N_DEV, r, C).transpose(1, 0, 2, 3).reshape(N_DEV, layers * r, C)
    cc = C // N_DEV
    return full.reshape(layers, R, N_DEV, cc).transpose(2, 0, 1, 3).reshape(N_DEV, layers * R, cc)


def kernel(x, c, fox_w_in, fox_b_f, fox_q_gain, fox_k_gain, fox_w_out, sgu_w_in, sgu_b_in, sgu_v_gain, sgu_v_bias, sgu_w_s, sgu_b_s, sgu_w_out, ffn_w_up, ffn_conv_w, ffn_conv_b, ffn_w_down, ada_w, ada_b, norm1_g, norm2_g, final_g, loss_target, m_fox_w_in, m_fox_b_f, m_fox_q_gain, m_fox_k_gain, m_fox_w_out, m_sgu_w_in, m_sgu_b_in, m_sgu_v_gain, m_sgu_v_bias, m_sgu_w_s, m_sgu_b_s, m_sgu_w_out, m_ffn_w_up, m_ffn_conv_w, m_ffn_conv_b, m_ffn_w_down, m_ada_w, m_ada_b, m_norm1_g, m_norm2_g, m_final_g, v_fox_w_in, v_fox_b_f, v_fox_q_gain, v_fox_k_gain, v_fox_w_out, v_sgu_w_in, v_sgu_b_in, v_sgu_v_gain, v_sgu_v_bias, v_sgu_w_s, v_sgu_b_s, v_sgu_w_out, v_ffn_w_up, v_ffn_conv_w, v_ffn_conv_b, v_ffn_w_down, v_ada_w, v_ada_b, v_norm1_g, v_norm2_g, v_final_g):
    args = dict(locals())
    wts = {n: args[n] for n in WEIGHTS}
    ms = {n: args["m_" + n] for n in WEIGHTS}
    vs = {n: args["v_" + n] for n in WEIGHTS}
    me = 4 * lax.axis_index("x") + 2 * lax.axis_index("y") + lax.axis_index("c")

    shard2d = lambda n: wts[n].astype(BF16).reshape(-1, wts[n].shape[-1])

    def assemble_big(names, got):
        out = {}
        for n, g in zip(names, got):
            f = _assemble(g, wts[n].shape[0], BIG_AXIS[n])
            out[n] = f[0] if n.startswith(SINGLE_LAYER) else f
        return out

    def blocks_of(names, grads):
        return [_disassemble(grads[n] if grads[n].ndim == 3 else grads[n][None], BIG_AXIS[n]) for n in names]

    send = [c] + [shard2d(n) for n in BEFORE_ATTENTION] + [wts[n].reshape(-1, wts[n].shape[-1]) for n in SMALL_SHARDED]
    got = _exchange(send, [False] * len(send), name="gather_first")
    c_all = got[0].reshape(N_DEV, -1)
    full = assemble_big(BEFORE_ATTENTION, got[1:1 + len(BEFORE_ATTENTION)])
    for n, g in zip(SMALL_SHARDED, got[1 + len(BEFORE_ATTENTION):]):
        lead = wts[n].shape[:-1]
        f = jnp.moveaxis(g.reshape((N_DEV,) + wts[n].shape), 0, -2).reshape(lead + (-1,))
        full[n] = f[0] if n.startswith(SINGLE_LAYER) else f
    for n in WEIGHTS:
        if n not in full and n not in BIG_AXIS:
            full[n] = wts[n][0] if n.startswith(SINGLE_LAYER) else wts[n]

    ada_cols = wts["ada_w"].shape[-1]
    mod_rows = []
    for i in range(2):
        b_mine = lax.dynamic_slice_in_dim(wts["ada_b"][i], me * ada_cols, ada_cols).reshape(1, ada_cols)
        m, c_act = _ada_mod(c_all, wts["ada_w"][i].astype(BF16), b_mine, name=f"ada_mod_{i}")
        mod_rows.append(m)
    got = _exchange([jnp.concatenate(mod_rows, axis=1)[:, None, :]], [True], name="exchange_mods")[0]
    d_model = x.shape[-1]
    mods = []
    for i in range(2):
        mod = got[:, 0, i * ada_cols:(i + 1) * ada_cols].reshape(1, N_DEV * ada_cols)
        mods.append([mod[:, k * d_model:(k + 1) * d_model] for k in range(6)])

    late = ([shard2d(n) for n in AFTER_ATTENTION], lambda g: assemble_big(AFTER_ATTENTION, g))
    loss, grad_x, grads, got_late, got_last = _local_step(
        x[0], loss_target[0], full, mods, late, lambda gr: blocks_of(AFTER_ATTENTION, gr),
        lambda gr: blocks_of(BEFORE_ATTENTION, gr))

    small = [n for n in WEIGHTS if n not in BIG_AXIS]
    flat = jnp.concatenate([loss.reshape(1)] + [grads[n].reshape(-1).astype(F32) for n in small])
    n_flat = flat.shape[0]
    rows = -(-n_flat // (8 * LANES)) * 8
    flat = jnp.pad(flat, (0, rows * LANES - n_flat)).reshape(rows, LANES)
    flat_all = _exchange([flat], [False], name="gather_small_grads")[0]
    total = _sum_parts(flat_all, name="sum_small_grads").reshape(-1)
    loss_out = total[0]

    off_ada = 1 + sum(math.prod(grads[n].shape) for n in small[:small.index("ada_b")])
    dmod_all = flat_all.reshape(N_DEV, -1)[:, off_ada:off_ada + 2 * N_DEV * ada_cols].reshape(N_DEV, 2, N_DEV * ada_cols)
    dmod_mine = lax.dynamic_slice_in_dim(dmod_all, me * ada_cols, ada_cols, axis=2)
    d_ada = [_mm(c_act, jnp.pad(dmod_mine[:, i], ((0, c_act.shape[0] - N_DEV), (0, 0))).astype(BF16), ta=True,
                 name=f"ada_dw_{i}") for i in range(2)]

    out_g, out_d, out_m, out_v = {}, {}, {}, {}
    summands = dict(zip(BEFORE_ATTENTION, got_last))
    summands.update(zip(AFTER_ATTENTION, got_late))
    summands["ada_w"] = jnp.concatenate(d_ada, axis=0)[None]
    for n, p in summands.items():
        out_g[n], out_d[n], out_m[n], out_v[n] = _adamw(wts[n], p, ms[n], vs[n], name=f"adamw_{n}")
    off = 1
    small_g = {}
    for n in small:
        full_shape = grads[n].shape
        size = math.prod(full_shape)
        g = total[off:off + size].reshape(full_shape)
        off += size
        if n in SMALL_SHARDED:
            blk = full_shape[-1] // N_DEV
            g = lax.dynamic_slice_in_dim(g, me * blk, blk, axis=g.ndim - 1)
        small_g[n] = g.reshape(wts[n].shape)
    cat = lambda d: jnp.concatenate([d[n].reshape(-1) for n in small])
    n_small = sum(math.prod(wts[n].shape) for n in small)
    rows2 = -(-n_small // (256 * LANES)) * 256
    pack = lambda d, fill: jnp.pad(cat(d), (0, rows2 * LANES - n_small), constant_values=fill).reshape(1, rows2, LANES)
    g, d, mn, vn = _adamw(pack(wts, 0.0), pack(small_g, 0.0), pack(ms, 0.0), pack(vs, 1.0), name="adamw_small")
    off = 0
    for n in small:
        size = math.prod(wts[n].shape)
        for src, dst in ((g, out_g), (d, out_d), (mn, out_m), (vn, out_v)):
            dst[n] = src.reshape(-1)[off:off + size].reshape(wts[n].shape)
        off += size

    return (loss_out, grad_x[None], *[out_g[n] for n in WEIGHTS], *[out_d[n] for n in WEIGHTS],
            *[out_m[n] for n in WEIGHTS], *[out_v[n] for n in WEIGHTS])
```

```python
import functools
import math

import jax
import jax.numpy as jnp
from jax import lax
from jax.experimental import pallas as pl
from jax.experimental.pallas import tpu as pltpu

F32, BF16 = jnp.float32, jnp.bfloat16
N_DEV = 8
HEADS, HEAD_DIM = 16, 64
HEAD_PAIRS = HEADS // 2
LANES = 128
SUBLANES = 8
EPS = 1e-6
SGU_BLOCK, SGU_GROUPS, SGU_CHUNK = 128, 8, 64
CONV_WIDTH = 3
ADAM_LR, ADAM_B1, ADAM_B2, ADAM_EPS, ADAM_WD, ADAM_STEP = 0.001, 0.9, 0.999, 1e-08, 0.01, 10
NEG = -1e30
GELU_C0, GELU_C1 = math.sqrt(2.0 / math.pi), 0.044715
MESH = pl.DeviceIdType.MESH
VMEM_LIMIT = 56 * 1024 * 1024


def _tile(dim, pref):
    if dim <= pref:
        return dim
    t = (pref // LANES) * LANES
    while t >= LANES:
        if dim % t == 0:
            return t
        t -= LANES
    return dim


def _params(*sem):
    return pltpu.CompilerParams(dimension_semantics=sem, vmem_limit_bytes=VMEM_LIMIT)


def _mm(a, b, *, name, ta=False, tb=False, out_dtype=F32, tm=1024, tn=1024, tk=1024, res=None, b_n=None, b_k=None, o_n=None,
        xchg=None):
    M = a.shape[1] if ta else a.shape[0]
    K = a.shape[0] if ta else a.shape[1]
    N = b.shape[0] if tb else b.shape[1]
    tm, tn, tk = _tile(M, tm), _tile(N, tn), _tile(K, tk)
    nk = K // tk
    dims = (((0 if ta else 1,), (1 if tb else 0,)), ((), ()))
    same = lambda idx: idx
    b_n, b_k, o_n = b_n or same, b_k or same, o_n or same
    a_spec = pl.BlockSpec((tk, tm), lambda i, j, k: (k, i)) if ta else pl.BlockSpec((tm, tk), lambda i, j, k: (i, k))
    b_spec = (pl.BlockSpec((tn, tk), lambda i, j, k: (b_n(j), b_k(k))) if tb
              else pl.BlockSpec((tk, tn), lambda i, j, k: (b_k(k), b_n(j))))
    o_spec = pl.BlockSpec((tm, tn), lambda i, j, k: (i, o_n(j)))

    def accumulate(a_ref, b_ref, acc):
        @pl.when(pl.program_id(2) == 0)
        def _():
            acc[...] = jnp.zeros_like(acc)
        acc[...] += lax.dot_general(a_ref[...], b_ref[...], dims, preferred_element_type=F32)

    if res is None:
        def body(a_ref, b_ref, o_ref, acc):
            accumulate(a_ref, b_ref, acc)

            @pl.when(pl.program_id(2) == nk - 1)
            def _():
                o_ref[...] = acc[...].astype(o_ref.dtype)

        if xchg is None:
            return pl.pallas_call(
                body, grid=(M // tm, N // tn, nk), in_specs=[a_spec, b_spec], out_specs=o_spec,
                out_shape=jax.ShapeDtypeStruct((M, N), out_dtype), scratch_shapes=[pltpu.VMEM((tm, tn), F32)],
                compiler_params=_params("parallel", "parallel", "arbitrary"), name=name)(a, b)
        grid = (M // tm, N // tn, nk)
        wrap, x_in, x_out, x_shapes, x_sems, x_ops = _ride_along(xchg, 2, 1, grid)
        outs = pl.pallas_call(
            wrap(body), grid=grid, in_specs=[a_spec, b_spec] + x_in, out_specs=[o_spec] + x_out,
            out_shape=[jax.ShapeDtypeStruct((M, N), out_dtype)] + x_shapes,
            scratch_shapes=[pltpu.VMEM((tm, tn), F32)] + x_sems,
            compiler_params=_params("arbitrary", "arbitrary", "arbitrary"), name=name)(a, b, *x_ops)
        return outs[0], outs[1:]

    x, gate = res

    def body_res(a_ref, b_ref, x_ref, g_ref, o_ref, y_ref, acc):
        accumulate(a_ref, b_ref, acc)

        @pl.when(pl.program_id(2) == nk - 1)
        def _():
            y = acc[...]
            o_ref[...] = x_ref[...] + g_ref[...] * y
            y_ref[...] = y.astype(BF16)

    return pl.pallas_call(
        body_res, grid=(M // tm, N // tn, nk),
        in_specs=[a_spec, b_spec, o_spec, pl.BlockSpec((1, tn), lambda i, j, k: (0, j))],
        out_specs=[o_spec, o_spec],
        out_shape=[jax.ShapeDtypeStruct((M, N), F32), jax.ShapeDtypeStruct((M, N), BF16)],
        scratch_shapes=[pltpu.VMEM((tm, tn), F32)],
        compiler_params=_params("parallel", "parallel", "arbitrary"), name=name)(a, b, x, gate)


def _ada_mod(c_rows, w, b, *, name):
    R, D = c_rows.shape
    N = w.shape[1]
    tn = _tile(N, 1536)
    rows = 16
    c_pad = jnp.pad(c_rows, ((0, rows - R), (0, 0)))

    def body(c_ref, w_ref, b_ref, o_ref, ca_ref):
        cv = c_ref[...]
        ca16 = (cv * jax.nn.sigmoid(cv)).astype(BF16)
        ca_ref[...] = ca16
        o_ref[...] = jnp.dot(ca16, w_ref[...], preferred_element_type=F32) + b_ref[...]

    out, ca = pl.pallas_call(
        body, grid=(N // tn,),
        in_specs=[pl.BlockSpec((rows, D), lambda j: (0, 0)), pl.BlockSpec((D, tn), lambda j: (0, j)),
                  pl.BlockSpec((1, tn), lambda j: (0, j))],
        out_specs=[pl.BlockSpec((rows, tn), lambda j: (0, j)), pl.BlockSpec((rows, D), lambda j: (0, 0))],
        out_shape=[jax.ShapeDtypeStruct((rows, N), F32), jax.ShapeDtypeStruct((rows, D), BF16)],
        compiler_params=_params("arbitrary"), name=name)(c_pad, w, b)
    return out[0:R], ca


def _norm_mod_fwd(x, g, shift, scale, *, name, ts=512):
    S, D = x.shape
    ts = _tile(S, ts)
    row = pl.BlockSpec((ts, D), lambda i: (i, 0))
    vec = pl.BlockSpec((1, D), lambda i: (0, 0))

    def body(x_ref, g_ref, sh_ref, sc_ref, h_ref):
        xv = x_ref[...]
        r = lax.rsqrt(jnp.mean(xv * xv, axis=-1, keepdims=True) + EPS)
        h_ref[...] = ((xv * r * g_ref[...]) * (1.0 + sc_ref[...]) + sh_ref[...]).astype(BF16)

    return pl.pallas_call(body, grid=(S // ts,), in_specs=[row, vec, vec, vec], out_specs=row,
                          out_shape=jax.ShapeDtypeStruct((S, D), BF16),
                          compiler_params=_params("parallel"), name=name)(x, g, shift, scale)


def _acc_init(step, *refs):
    @pl.when(step == 0)
    def _():
        for r in refs:
            r[...] = jnp.zeros_like(r)


def _colsum(v):
    return jnp.sum(v, axis=0, keepdims=True)


def _norm_mod_bwd(dh, x, g, scale, dres, prev=None, *, name, ts=512):
    S, D = x.shape
    ts = _tile(S, ts)
    row = pl.BlockSpec((ts, D), lambda i: (i, 0))
    vec = pl.BlockSpec((1, D), lambda i: (0, 0))
    has_prev = prev is not None

    def body(*refs):
        if has_prev:
            dh_ref, x_ref, g_ref, sc_ref, dres_ref, y_ref, gate_ref, dx_ref, dg_ref, dsh_ref, dsc_ref, dy_ref, dgate_ref = refs
            _acc_init(pl.program_id(0), dg_ref, dsh_ref, dsc_ref, dgate_ref)
        else:
            dh_ref, x_ref, g_ref, sc_ref, dres_ref, dx_ref, dg_ref, dsh_ref, dsc_ref = refs
            _acc_init(pl.program_id(0), dg_ref, dsh_ref, dsc_ref)
        xv, dhv, gv = x_ref[...], dh_ref[...], g_ref[...]
        r = lax.rsqrt(jnp.mean(xv * xv, axis=-1, keepdims=True) + EPS)
        xh = xv * r
        dsh_ref[...] += _colsum(dhv)
        dsc_ref[...] += _colsum(dhv * (xh * gv))
        dn = dhv * (1.0 + sc_ref[...])
        dg_ref[...] += _colsum(dn * xh)
        dxh = dn * gv
        dx = dres_ref[...] + r * (dxh - xh * jnp.mean(dxh * xh, axis=-1, keepdims=True))
        dx_ref[...] = dx
        if has_prev:
            dy_ref[...] = (gate_ref[...] * dx).astype(BF16)
            dgate_ref[...] += _colsum(dx * y_ref[...].astype(F32))

    ins, in_specs = [dh, x, g, scale, dres], [row, row, vec, vec, row]
    outs = [jax.ShapeDtypeStruct((S, D), F32)] + [jax.ShapeDtypeStruct((1, D), F32)] * 3
    out_specs = [row, vec, vec, vec]
    if has_prev:
        ins += list(prev)
        in_specs += [row, vec]
        outs += [jax.ShapeDtypeStruct((S, D), BF16), jax.ShapeDtypeStruct((1, D), F32)]
        out_specs += [row, vec]
    return pl.pallas_call(body, grid=(S // ts,), in_specs=in_specs, out_specs=out_specs, out_shape=outs,
                          compiler_params=_params("arbitrary"), name=name)(*ins)


def _final_loss(x, g, target, y, gate, *, name, ts=512):
    S, D = x.shape
    ts = _tile(S, ts)
    row = pl.BlockSpec((ts, D), lambda i: (i, 0))
    vec = pl.BlockSpec((1, D), lambda i: (0, 0))
    lvec = pl.BlockSpec((1, LANES), lambda i: (0, 0))

    def body(x_ref, g_ref, t_ref, y_ref, gate_ref, loss_ref, dg_ref, dx_ref, dy_ref, dgate_ref):
        _acc_init(pl.program_id(0), loss_ref, dg_ref, dgate_ref)
        xv, gv = x_ref[...], g_ref[...]
        r = lax.rsqrt(jnp.mean(xv * xv, axis=-1, keepdims=True) + EPS)
        xh = xv * r
        e = xh * gv - t_ref[...]
        loss_ref[...] += 0.5 * jnp.sum(jnp.mean(e * e, axis=-1, keepdims=True), axis=0, keepdims=True)
        dout = e * (1.0 / D)
        dg_ref[...] += _colsum(dout * xh)
        dxh = dout * gv
        dx = r * (dxh - xh * jnp.mean(dxh * xh, axis=-1, keepdims=True))
        dx_ref[...] = dx
        dy_ref[...] = (gate_ref[...] * dx).astype(BF16)
        dgate_ref[...] += _colsum(dx * y_ref[...].astype(F32))

    return pl.pallas_call(
        body, grid=(S // ts,), in_specs=[row, vec, row, row, vec], out_specs=[lvec, vec, row, row, vec],
        out_shape=[jax.ShapeDtypeStruct((1, LANES), F32), jax.ShapeDtypeStruct((1, D), F32),
                   jax.ShapeDtypeStruct((S, D), F32), jax.ShapeDtypeStruct((S, D), BF16),
                   jax.ShapeDtypeStruct((1, D), F32)],
        compiler_params=_params("arbitrary"), name=name)(x, g, target, y, gate)


def _head_mean(v, gmat):
    hi = v.astype(BF16)
    lo = (v - hi.astype(F32)).astype(BF16)
    return jnp.dot(hi, gmat, preferred_element_type=F32) + jnp.dot(lo, gmat, preferred_element_type=F32)


L_F, L_ONE, L_SHIFT = HEAD_DIM, HEAD_DIM + 3, HEAD_DIM + 6
SHIFT_FREE_LOGIT_BOUND = 60.0


def _lane():
    return lax.broadcasted_iota(jnp.int32, (1, LANES), 1)


def _split3(v):
    p1 = v.astype(BF16).astype(F32)
    r1 = v - p1
    p2 = r1.astype(BF16).astype(F32)
    p3 = (r1 - p2).astype(BF16).astype(F32)
    return p1, p2, p3


def _put3(lane, first, pieces):
    out = jnp.where(lane == first, pieces[0], 0.0)
    for k in (1, 2):
        out = out + jnp.where(lane == first + k, pieces[k], 0.0)
    return out


def _ones3(lane, first):
    return jnp.where((lane >= first) & (lane < first + 3), 1.0, 0.0)


def _lane_col(v, lane, idx):
    return jnp.sum(jnp.where(lane == idx, v, 0.0), axis=-1, keepdims=True)


def _head_of_pair(pair, e, lane):
    return jnp.where(lane < HEAD_DIM, pair if e == 0 else pltpu.roll(pair, HEAD_DIM, 1), 0.0)


def _pair_of_heads(even, odd, lane):
    return jnp.where(lane < HEAD_DIM, even, pltpu.roll(odd, HEAD_DIM, 1))


def _fox_prep_fwd(proj, fcum, qgain, kgain, gmat, *, name, ts=256):
    S = proj.shape[0]
    D = HEADS * HEAD_DIM
    ts = _tile(S, ts)
    scale = HEAD_DIM ** -0.5

    def body(p_ref, f_ref, qg_ref, kg_ref, gm_ref, q_ref, k_ref, v_ref):
        gm, lane, fc = gm_ref[...], _lane(), f_ref[...]
        for cpair in range(HEAD_PAIRS):
            qv = p_ref[:, pl.ds(cpair * LANES, LANES)].astype(F32)
            kv = p_ref[:, pl.ds(D + cpair * LANES, LANES)].astype(F32)
            vv = p_ref[:, pl.ds(2 * D + cpair * LANES, LANES)].astype(F32)
            qn = (qv * lax.rsqrt(_head_mean(qv * qv, gm) + EPS) * qg_ref[...]) * scale
            kn = kv * lax.rsqrt(_head_mean(kv * kv, gm) + EPS) * kg_ref[...]
            for e in range(2):
                h = 2 * cpair + e
                cols = pl.ds(h * LANES, LANES)
                f3 = _split3(_lane_col(fc, lane, h))
                q_ref[:, cols] = (_head_of_pair(qn, e, lane) + _put3(lane, L_F, f3) + _ones3(lane, L_ONE)).astype(BF16)
                k_ref[:, cols] = (_head_of_pair(kn, e, lane) + _ones3(lane, L_F)
                                  - _put3(lane, L_ONE, f3) + _ones3(lane, L_SHIFT)).astype(BF16)
                v_ref[:, cols] = (_head_of_pair(vv, e, lane) + _ones3(lane, L_F)).astype(BF16)

    vec = pl.BlockSpec((1, LANES), lambda i: (0, 0))
    wide = pl.BlockSpec((ts, HEADS * LANES), lambda i: (i, 0))
    return pl.pallas_call(
        body, grid=(S // ts,),
        in_specs=[pl.BlockSpec((ts, 3 * D), lambda i: (i, 0)), pl.BlockSpec((ts, LANES), lambda i: (i, 0)), vec, vec,
                  pl.BlockSpec((LANES, LANES), lambda i: (0, 0))],
        out_specs=[wide, wide, wide], out_shape=[jax.ShapeDtypeStruct((S, HEADS * LANES), BF16)] * 3,
        compiler_params=_params("parallel"), name=name)(proj, fcum, qgain, kgain, gmat)


def _fox_prep_bwd(proj, dq_aug, dk_aug, dv_aug, dog, qgain, kgain, gmat, *, name, ts=256):
    S = proj.shape[0]
    D = HEADS * HEAD_DIM
    ts = _tile(S, ts)
    scale = HEAD_DIM ** -0.5

    def body(p_ref, dq_ref, dk_ref, dv_ref, dog_ref, qg_ref, kg_ref, gm_ref, o_ref, df_ref, dqg_ref, dkg_ref):
        _acc_init(pl.program_id(0), dqg_ref, dkg_ref)
        gm, lane = gm_ref[...], _lane()
        df = jnp.zeros((ts, LANES), F32)
        for cpair in range(HEAD_PAIRS):
            tiles = []
            for e in range(2):
                h = 2 * cpair + e
                cols = pl.ds(h * LANES, LANES)
                tq, tk = dq_ref[:, cols], dk_ref[:, cols]
                df = jnp.where(lane == h, _lane_col(tq, lane, L_F) - _lane_col(tk, lane, L_ONE), df)
                tiles.append((tq, tk, dv_ref[:, cols].astype(F32)))
            pair = [_pair_of_heads(tiles[0][k], tiles[1][k], lane) for k in range(3)]
            for half, g_ref, dg_ref, mult in ((0, qg_ref, dqg_ref, scale), (1, kg_ref, dkg_ref, 1.0)):
                v = p_ref[:, pl.ds(half * D + cpair * LANES, LANES)].astype(F32)
                r = lax.rsqrt(_head_mean(v * v, gm) + EPS)
                xh = v * r
                dn = pair[half] * mult
                dg_ref[...] += _colsum(dn * xh)
                dxh = dn * g_ref[...]
                o_ref[:, pl.ds(half * D + cpair * LANES, LANES)] = (r * (dxh - xh * _head_mean(dxh * xh, gm))).astype(BF16)
            o_ref[:, pl.ds(2 * D + cpair * LANES, LANES)] = pair[2].astype(BF16)
        o_ref[:, pl.ds(3 * D, D)] = dog_ref[...]
        o_ref[:, pl.ds(4 * D, LANES)] = jnp.zeros((ts, LANES), BF16)
        df_ref[...] = df

    row = pl.BlockSpec((ts, D), lambda i: (i, 0))
    wide = pl.BlockSpec((ts, HEADS * LANES), lambda i: (i, 0))
    vec = pl.BlockSpec((1, LANES), lambda i: (0, 0))
    return pl.pallas_call(
        body, grid=(S // ts,),
        in_specs=[pl.BlockSpec((ts, 2 * D), lambda i: (i, 0)), wide, wide, wide, row, vec, vec,
                  pl.BlockSpec((LANES, LANES), lambda i: (0, 0))],
        out_specs=[pl.BlockSpec((ts, 4 * D + LANES), lambda i: (i, 0)), pl.BlockSpec((ts, LANES), lambda i: (i, 0)), vec, vec],
        out_shape=[jax.ShapeDtypeStruct((S, 4 * D + LANES), BF16), jax.ShapeDtypeStruct((S, LANES), F32),
                   jax.ShapeDtypeStruct((1, LANES), F32), jax.ShapeDtypeStruct((1, LANES), F32)],
        compiler_params=_params("arbitrary"), name=name)(proj, dq_aug, dk_aug, dv_aug, dog, qgain, kgain, gmat)


def _log_sigmoid(z):
    return jnp.minimum(z, 0.0) - jnp.log(1.0 + jnp.exp(-jnp.abs(z)))


def _fox_decay_fwd(fl, bf, *, name, tb=256):
    S = fl.shape[0]
    tb = _tile(S, tb)

    def body(fl_ref, b_ref, o_ref, carry):
        @pl.when(pl.program_id(0) == 0)
        def _():
            carry[...] = jnp.zeros_like(carry)
        logf = _log_sigmoid(fl_ref[...] + b_ref[...])
        tri = (lax.broadcasted_iota(jnp.int32, (tb, tb), 1) <= lax.broadcasted_iota(jnp.int32, (tb, tb), 0)).astype(F32)
        cs = jnp.dot(tri, logf, preferred_element_type=F32, precision=lax.Precision.HIGHEST) + carry[...]
        o_ref[...] = cs
        carry[...] = _row_of(cs, tb - 1)

    return pl.pallas_call(
        body, grid=(S // tb,),
        in_specs=[pl.BlockSpec((tb, LANES), lambda i: (i, 0)), pl.BlockSpec((1, LANES), lambda i: (0, 0))],
        out_specs=pl.BlockSpec((tb, LANES), lambda i: (i, 0)),
        out_shape=jax.ShapeDtypeStruct((S, LANES), F32), scratch_shapes=[pltpu.VMEM((1, LANES), F32)],
        compiler_params=_params("arbitrary"), name=name)(fl, bf)


def _fox_decay_bwd(dF, fl, bf, dproj, *, name, tb=256):
    S = fl.shape[0]
    tb = _tile(S, tb)
    n = S // tb
    last_col = dproj.shape[1] // LANES - 1

    def body(df_ref, fl_ref, b_ref, dproj_hbm, o_ref, db_ref, carry):
        del dproj_hbm
        @pl.when(pl.program_id(0) == 0)
        def _():
            carry[...] = jnp.zeros_like(carry)
            db_ref[...] = jnp.zeros_like(db_ref)
        tri = (lax.broadcasted_iota(jnp.int32, (tb, tb), 1) >= lax.broadcasted_iota(jnp.int32, (tb, tb), 0)).astype(F32)
        rc = jnp.dot(tri, df_ref[...], preferred_element_type=F32, precision=lax.Precision.HIGHEST) + carry[...]
        carry[...] = _row_of(rc, 0)
        dfl = rc * jax.nn.sigmoid(-(fl_ref[...] + b_ref[...]))
        o_ref[...] = dfl.astype(BF16)
        db_ref[...] += _colsum(dfl)

    rev = pl.BlockSpec((tb, LANES), lambda i: (n - 1 - i, 0))
    vec = pl.BlockSpec((1, LANES), lambda i: (0, 0))
    return pl.pallas_call(
        body, grid=(n,), in_specs=[rev, rev, vec, pl.BlockSpec(memory_space=pl.ANY)],
        out_specs=[pl.BlockSpec((tb, LANES), lambda i: (n - 1 - i, last_col)), vec],
        out_shape=[jax.ShapeDtypeStruct(dproj.shape, BF16), jax.ShapeDtypeStruct((1, LANES), F32)],
        scratch_shapes=[pltpu.VMEM((1, LANES), F32)], input_output_aliases={3: 0},
        compiler_params=_params("arbitrary"), name=name)(dF, fl, bf, dproj)


_NT = (((1,), (1,)), ((), ()))
_TN = (((0,), (0,)), ((), ()))


def _causal(T, transposed=False):
    r, c = lax.broadcasted_iota(jnp.int32, (T, T), 0), lax.broadcasted_iota(jnp.int32, (T, T), 1)
    return r <= c if transposed else c <= r


def _with_shift(q_tile, shift, lane):
    keep = jnp.where((lane >= L_SHIFT) & (lane < L_SHIFT + 3), 0.0, q_tile)
    return (keep + _put3(lane, L_SHIFT, _split3(-shift))).astype(BF16)


def _ride_along(xchg, n_in, n_out, grid):
    if xchg is None:
        return (lambda body: body), [], [], [], [], []
    arrs, scatter = xchg
    n = len(arrs)

    def wrap(body):
        def wrapped(*refs):
            own_in, x_in = refs[:n_in], refs[n_in:n_in + n]
            own_out, x_out = refs[n_in + n:n_in + n + n_out], refs[n_in + n + n_out:n_in + 2 * n + n_out]
            rest = refs[n_in + 2 * n + n_out:]
            own_scratch, sems = rest[:len(rest) - 3], rest[len(rest) - 3:]
            ids = [pl.program_id(d) for d in range(len(grid))]
            first = functools.reduce(jnp.logical_and, [i == 0 for i in ids])
            last = functools.reduce(jnp.logical_and, [i == g - 1 for i, g in zip(ids, grid)])

            @pl.when(first)
            def _():
                for cp in _xchg_copies(x_in, x_out, scatter, *sems):
                    cp.start()

            body(*own_in, *own_out, *own_scratch)

            @pl.when(last)
            def _():
                for cp in _xchg_copies(x_in, x_out, scatter, *sems):
                    cp.wait()

        return wrapped

    return wrap, [_HBM] * n, [_HBM] * n, _xchg_out_shapes(arrs, scatter), _xchg_sems(n), list(arrs)


def _attn_rowmax(q_aug, k_aug, *, name, T=1024):
    S = q_aug.shape[0]
    T = _tile(S, T)
    n = S // T

    def body(q_ref, k_ref, o_ref, m_s):
        i, j = pl.program_id(1), pl.program_id(2)

        @pl.when(j == 0)
        def _():
            m_s[...] = jnp.full_like(m_s, NEG)

        def step(diag):
            s = lax.dot_general(q_ref[...], k_ref[...], _NT, preferred_element_type=F32)
            if diag:
                s = jnp.where(_causal(T), s, NEG)
            m = m_s[...]
            for cb in range(T // LANES):
                m = jnp.maximum(m, s[:, cb * LANES:(cb + 1) * LANES])
            m_s[...] = m

        @pl.when(j < i)
        def _():
            step(False)

        @pl.when(j == i)
        def _():
            step(True)
            o_ref[...] = _with_shift(q_ref[...].astype(F32), jnp.max(m_s[...], axis=-1, keepdims=True), _lane())

    qrow = pl.BlockSpec((T, LANES), lambda h, i, j: (i, h))
    return pl.pallas_call(
        body, grid=(HEADS, n, n),
        in_specs=[qrow, pl.BlockSpec((T, LANES), lambda h, i, j: (jnp.minimum(j, i), h))],
        out_specs=qrow, out_shape=jax.ShapeDtypeStruct(q_aug.shape, BF16),
        scratch_shapes=[pltpu.VMEM((T, LANES), F32)],
        compiler_params=_params("parallel", "parallel", "arbitrary"), name=name)(q_aug, k_aug)


def _attn_fwd(q_max, k_aug, v_aug, xchg=None, *, name, T=1024):
    S = q_max.shape[0]
    T = _tile(S, T)
    n = S // T
    wrap, x_in, x_out, x_shapes, x_sems, x_ops = _ride_along(xchg, 3, 2, (HEADS, n, n))

    def body(q_ref, k_ref, v_ref, o_ref, qb_ref, acc_s):
        i, j = pl.program_id(1), pl.program_id(2)

        @pl.when(j == 0)
        def _():
            acc_s[...] = jnp.zeros_like(acc_s)

        def block(rows, cols, mask):
            s = lax.dot_general(q_ref[rows, :], k_ref[cols, :], _NT, preferred_element_type=F32)
            if mask is not None:
                s = jnp.where(mask, s, NEG)
            return jnp.dot(jnp.exp(s).astype(BF16), v_ref[cols, :], preferred_element_type=F32)

        @pl.when(j < i)
        def _():
            acc_s[...] += block(pl.ds(0, T), pl.ds(0, T), None)

        @pl.when(j == i)
        def _():
            half = T // 2
            lo, hi = pl.ds(0, half), pl.ds(half, half)
            acc_s[lo, :] += block(lo, lo, _causal(half))
            acc_s[hi, :] += block(hi, lo, None) + block(hi, hi, _causal(half))
            lane = _lane()
            acc = acc_s[...]
            l = _lane_col(acc, lane, L_F)
            o_ref[...] = acc / l
            qf = q_ref[...].astype(F32)
            row_max = -jnp.sum(jnp.where((lane >= L_SHIFT) & (lane < L_SHIFT + 3), qf, 0.0), axis=-1, keepdims=True)
            qb_ref[...] = _with_shift(qf, row_max + jnp.log(l), lane)

    qrow = pl.BlockSpec((T, LANES), lambda h, i, j: (i, h))
    kv = pl.BlockSpec((T, LANES), lambda h, i, j: (jnp.minimum(j, i), h))
    outs = pl.pallas_call(
        wrap(body), grid=(HEADS, n, n), in_specs=[qrow, kv, kv] + x_in, out_specs=[qrow, qrow] + x_out,
        out_shape=[jax.ShapeDtypeStruct(q_max.shape, F32), jax.ShapeDtypeStruct(q_max.shape, BF16)] + x_shapes,
        scratch_shapes=[pltpu.VMEM((T, LANES), F32)] + x_sems,
        compiler_params=_params("arbitrary", "arbitrary", "arbitrary"), name=name)(q_max, k_aug, v_aug, *x_ops)
    return outs[0], outs[1], outs[2:]


def _attn_bwd(q_lse, k_aug, v_aug, do_aug, xchg=None, *, name, T=1024):
    S = q_lse.shape[0]
    T = _tile(S, T)
    n = S // T
    wrap, x_in, x_out, x_shapes, x_sems, x_ops = _ride_along(xchg, 4, 3, (HEADS, n, n))

    def body(q_ref, do_ref, k_ref, v_ref, dq_ref, dk_ref, dv_ref, dq_s, dk_s, dv_s):
        j, i = pl.program_id(1), pl.program_id(2)

        def block(keys, queries, mask):
            q, do, k, v = q_ref[queries, :], do_ref[queries, :], k_ref[keys, :], v_ref[keys, :]
            st = lax.dot_general(k, q, _NT, preferred_element_type=F32)
            if mask is not None:
                st = jnp.where(mask, st, NEG)
            pt = jnp.exp(st)
            dst = (pt * lax.dot_general(v, do, _NT, preferred_element_type=F32)).astype(BF16)
            dv_s[keys, :] += jnp.dot(pt.astype(BF16), do, preferred_element_type=F32)
            dk_s[keys, :] += jnp.dot(dst, q, preferred_element_type=F32)
            return lax.dot_general(dst, k, _TN, preferred_element_type=F32)

        @pl.when(i == j)
        def _():
            dk_s[...] = jnp.zeros_like(dk_s)
            dv_s[...] = jnp.zeros_like(dv_s)

            @pl.when(j == 0)
            def _():
                dq_s[i] = jnp.zeros((T, LANES), F32)

            half = T // 2
            lo, hi = pl.ds(0, half), pl.ds(half, half)
            dq_s[i, lo, :] += block(lo, lo, _causal(half, transposed=True))
            dq_s[i, hi, :] += block(lo, hi, None) + block(hi, hi, _causal(half, transposed=True))
            dq_ref[...] = dq_s[j]

        @pl.when(i > j)
        def _():
            upd = block(pl.ds(0, T), pl.ds(0, T), None)

            @pl.when(j == 0)
            def _():
                dq_s[i] = upd

            @pl.when(j > 0)
            def _():
                dq_s[i] += upd

        @pl.when(i == n - 1)
        def _():
            dk_ref[...] = dk_s[...]
            dv_ref[...] = dv_s[...].astype(BF16)

    qrow = pl.BlockSpec((T, LANES), lambda h, j, i: (jnp.maximum(i, j), h))
    kv = pl.BlockSpec((T, LANES), lambda h, j, i: (j, h))
    outs = pl.pallas_call(
        wrap(body), grid=(HEADS, n, n), in_specs=[qrow, qrow, kv, kv] + x_in, out_specs=[kv, kv, kv] + x_out,
        out_shape=[jax.ShapeDtypeStruct(q_lse.shape, F32), jax.ShapeDtypeStruct(q_lse.shape, F32),
                   jax.ShapeDtypeStruct(q_lse.shape, BF16)] + x_shapes,
        scratch_shapes=[pltpu.VMEM((n, T, LANES), F32), pltpu.VMEM((T, LANES), F32), pltpu.VMEM((T, LANES), F32)] + x_sems,
        compiler_params=_params("arbitrary", "arbitrary", "arbitrary"), name=name)(q_lse, do_aug, k_aug, v_aug, *x_ops)
    return outs[0], outs[1], outs[2], outs[3:]


def _fox_gate_fwd(att_aug, proj, *, name, ts=256):
    S = att_aug.shape[0]
    D = HEADS * HEAD_DIM
    ts = _tile(S, ts)

    def body(a_ref, o_ref, att_ref, out_ref):
        lane = _lane()
        for cpair in range(HEAD_PAIRS):
            cols = pl.ds(cpair * LANES, LANES)
            pair = _pair_of_heads(a_ref[:, pl.ds(2 * cpair * LANES, LANES)], a_ref[:, pl.ds((2 * cpair + 1) * LANES, LANES)], lane)
            att_ref[:, cols] = pair
            out_ref[:, cols] = (pair * jax.nn.sigmoid(o_ref[:, cols].astype(F32))).astype(BF16)

    row = pl.BlockSpec((ts, D), lambda i: (i, 0))
    return pl.pallas_call(
        body, grid=(S // ts,),
        in_specs=[pl.BlockSpec((ts, HEADS * LANES), lambda i: (i, 0)), pl.BlockSpec((ts, D), lambda i: (i, 3))],
        out_specs=[row, row], out_shape=[jax.ShapeDtypeStruct((S, D), F32), jax.ShapeDtypeStruct((S, D), BF16)],
        compiler_params=_params("parallel"), name=name)(att_aug, proj)


def _fox_gate_bwd(da, att, proj, *, name, ts=256):
    S, D = att.shape
    ts = _tile(S, ts)

    def body(da_ref, a_ref, o_ref, do_ref, dog_ref):
        lane = _lane()
        for cpair in range(HEAD_PAIRS):
            cols = pl.ds(cpair * LANES, LANES)
            dav, av = da_ref[:, cols].astype(F32), a_ref[:, cols]
            sg = jax.nn.sigmoid(o_ref[:, cols].astype(F32))
            datt = (dav * sg).astype(BF16).astype(F32)
            dog_ref[:, cols] = (dav * av * sg * (1.0 - sg)).astype(BF16)
            prod = datt * av
            for e in range(2):
                in_head = (lane < HEAD_DIM) if e == 0 else (lane >= HEAD_DIM)
                delta = jnp.sum(jnp.where(in_head, prod, 0.0), axis=-1, keepdims=True)
                tile = _head_of_pair(datt, e, lane) + _put3(lane, L_F, _split3(-delta))
                do_ref[:, pl.ds((2 * cpair + e) * LANES, LANES)] = tile.astype(BF16)

    row = pl.BlockSpec((ts, D), lambda i: (i, 0))
    return pl.pallas_call(
        body, grid=(S // ts,), in_specs=[row, row, pl.BlockSpec((ts, D), lambda i: (i, 3))],
        out_specs=[pl.BlockSpec((ts, HEADS * LANES), lambda i: (i, 0)), row],
        out_shape=[jax.ShapeDtypeStruct((S, HEADS * LANES), BF16), jax.ShapeDtypeStruct((S, D), BF16)],
        compiler_params=_params("parallel"), name=name)(da, att, proj)


def _row_of(block, r):
    rows = lax.broadcasted_iota(jnp.int32, block.shape, 0)
    return jnp.sum(jnp.where(rows == r, block, 0.0), axis=0, keepdims=True)


def _shift_down(cur, tail, k):
    out = pltpu.roll(cur, k, 0)
    top = out[:SUBLANES]
    rows = lax.broadcasted_iota(jnp.int32, top.shape, 0)
    for r in range(k):
        top = jnp.where(rows == r, _row_of(tail, tail.shape[0] - k + r), top)
    return jnp.concatenate([top, out[SUBLANES:]], axis=0)


def _shift_up(cur, head, k):
    n = cur.shape[0]
    out = pltpu.roll(cur, n - k, 0)
    bottom = out[n - SUBLANES:]
    rows = lax.broadcasted_iota(jnp.int32, bottom.shape, 0)
    for r in range(k):
        bottom = jnp.where(rows == SUBLANES - k + r, _row_of(head, r), bottom)
    return jnp.concatenate([out[:n - SUBLANES], bottom], axis=0)


HALO = 16


CONV_TC = 1408


def _pair_tiles(v):
    nc = v.shape[-1] // (2 * CONV_TC)
    return jnp.swapaxes(v.reshape(v.shape[:-1] + (2, nc, CONV_TC)), -3, -2).reshape(v.shape)


def _unpair_tiles(v):
    nc = v.shape[-1] // (2 * CONV_TC)
    return jnp.swapaxes(v.reshape(v.shape[:-1] + (nc, 2, CONV_TC)), -3, -2).reshape(v.shape)


def _conv_rows(cur, tail, w_ref, b_ref, cols):
    a1, a2 = _shift_down(cur, tail, 1), _shift_down(cur, tail, 2)
    return a2 * w_ref[0:1, cols] + a1 * w_ref[1:2, cols] + cur * w_ref[2:3, cols] + b_ref[:, cols], (a2, a1, cur)


def _conv_gate_fwd(a, cw, cb, *, name, ts=512):
    S, F2 = a.shape
    tc = CONV_TC
    ts = _tile(S, ts)
    nc = F2 // (2 * tc)
    sub = ts // HALO
    halves = (pl.ds(0, tc), pl.ds(tc, tc))

    def body(a_ref, t_ref, w_ref, b_ref, o_ref):
        first = pl.program_id(1) == 0
        pre = []
        for cols in halves:
            tail = jnp.where(first, 0.0, t_ref[:, cols].astype(F32))
            pre.append(_conv_rows(a_ref[:, cols].astype(F32), tail, w_ref, b_ref, cols)[0])
        g, val = pre
        o_ref[...] = (g * jax.nn.sigmoid(g) * val).astype(BF16)

    return pl.pallas_call(
        body, grid=(nc, S // ts),
        in_specs=[pl.BlockSpec((ts, 2 * tc), lambda j, i: (i, j)),
                  pl.BlockSpec((HALO, 2 * tc), lambda j, i: (jnp.maximum(i * sub - 1, 0), j)),
                  pl.BlockSpec((CONV_WIDTH, 2 * tc), lambda j, i: (0, j)), pl.BlockSpec((1, 2 * tc), lambda j, i: (0, j))],
        out_specs=pl.BlockSpec((ts, tc), lambda j, i: (i, j)),
        out_shape=jax.ShapeDtypeStruct((S, F2 // 2), BF16),
        compiler_params=_params("parallel", "parallel"), name=name)(a, a, cw, cb)


def _conv_gate_bwd(a, dact, cw, cb, *, name, ts=512):
    S, F2 = a.shape
    tc = CONV_TC
    ts = _tile(S, ts)
    nc = F2 // (2 * tc)
    sub = ts // HALO
    n_rows = S // ts
    halves = (pl.ds(0, tc), pl.ds(tc, tc))

    def body(a_ref, at_ref, ah_ref, d_ref, dh_ref, w_ref, b_ref, da_ref, s_ref):
        i = pl.program_id(1)
        _acc_init(i, s_ref)

        def dpre_of(rows, tails, d):
            (g, taps_g), (val, taps_v) = [_conv_rows(rows[h], tails[h], w_ref, b_ref, halves[h]) for h in range(2)]
            sg = jax.nn.sigmoid(g)
            return (d * val * (sg * (1.0 + g * (1.0 - sg))), d * (g * sg)), (taps_g, taps_v)

        cur = [a_ref[:, c].astype(F32) for c in halves]
        tail = [jnp.where(i == 0, 0.0, at_ref[:, c].astype(F32)) for c in halves]
        dpre, taps = dpre_of(cur, tail, d_ref[...].astype(F32))
        head, _ = dpre_of([ah_ref[:, c].astype(F32) for c in halves], [x[ts - HALO:, :] for x in cur], dh_ref[...].astype(F32))
        for h, cols in enumerate(halves):
            dd = dpre[h]
            nxt = jnp.where(i == n_rows - 1, 0.0, head[h])
            da_ref[:, cols] = (dd * w_ref[2:3, cols] + _shift_up(dd, nxt, 1) * w_ref[1:2, cols]
                               + _shift_up(dd, nxt, 2) * w_ref[0:1, cols]).astype(BF16)
            for r in range(CONV_WIDTH):
                s_ref[r:r + 1, cols] += _colsum(dd * taps[h][r])
            s_ref[CONV_WIDTH:CONV_WIDTH + 1, cols] += _colsum(dd)

    nxt_rows = lambda i: jnp.minimum((i + 1) * sub, S // HALO - 1)
    return pl.pallas_call(
        body, grid=(nc, n_rows),
        in_specs=[pl.BlockSpec((ts, 2 * tc), lambda j, i: (i, j)),
                  pl.BlockSpec((HALO, 2 * tc), lambda j, i: (jnp.maximum(i * sub - 1, 0), j)),
                  pl.BlockSpec((HALO, 2 * tc), lambda j, i: (nxt_rows(i), j)),
                  pl.BlockSpec((ts, tc), lambda j, i: (i, j)), pl.BlockSpec((HALO, tc), lambda j, i: (nxt_rows(i), j)),
                  pl.BlockSpec((CONV_WIDTH, 2 * tc), lambda j, i: (0, j)), pl.BlockSpec((1, 2 * tc), lambda j, i: (0, j))],
        out_specs=[pl.BlockSpec((ts, 2 * tc), lambda j, i: (i, j)), pl.BlockSpec((8, 2 * tc), lambda j, i: (0, j))],
        out_shape=[jax.ShapeDtypeStruct((S, F2), BF16), jax.ShapeDtypeStruct((8, F2), F32)],
        compiler_params=_params("parallel", "arbitrary"), name=name)(a, a, a, dact, dact, cw, cb)


def _gelu_parts(z):
    z2 = z * z
    t = jnp.tanh(GELU_C0 * (z + GELU_C1 * z * z2))
    val = 0.5 * z * (1.0 + t)
    grad = 0.5 * (1.0 + t) + 0.5 * z * (1.0 - t * t) * GELU_C0 * (1.0 + 3.0 * GELU_C1 * z2)
    return val, grad


def _sgu_fwd(pre, b_in, vgain, vbias, wm, bsb, *, name, ts=256):
    S, W2 = pre.shape
    W = W2 // 2
    gd = W // SGU_GROUPS
    ts = _tile(S, ts)

    def body(p_ref, b_ref, vg_ref, vb_ref, wm_ref, bs_ref, y_ref):
        u = _gelu_parts(p_ref[:, pl.ds(0, W)].astype(F32) + b_ref[:, pl.ds(0, W)])[0]
        v = _gelu_parts(p_ref[:, pl.ds(W, W)].astype(F32) + b_ref[:, pl.ds(W, W)])[0]
        mu = jnp.mean(v, axis=-1, keepdims=True)
        vc = v - mu
        rstd = lax.rsqrt(jnp.mean(vc * vc, axis=-1, keepdims=True) + EPS)
        vn = ((vc * rstd) * vg_ref[...] + vb_ref[...]).astype(BF16)
        for blk in range(ts // SGU_BLOCK):
            r0 = blk * SGU_BLOCK
            for g in range(SGU_GROUPS):
                c0 = g * gd
                mixed = jnp.dot(wm_ref[g], vn[r0:r0 + SGU_BLOCK, c0:c0 + gd], preferred_element_type=F32) + bs_ref[g]
                y_ref[pl.ds(r0, SGU_BLOCK), pl.ds(c0, gd)] = (u[r0:r0 + SGU_BLOCK, c0:c0 + gd] * mixed).astype(BF16)

    full = lambda shape: pl.BlockSpec(shape, lambda i: (0,) * len(shape))
    return pl.pallas_call(
        body, grid=(S // ts,),
        in_specs=[pl.BlockSpec((ts, W2), lambda i: (i, 0)), full((1, W2)), full((1, W)), full((1, W)),
                  full((SGU_GROUPS, SGU_BLOCK, SGU_BLOCK)), full((SGU_GROUPS, SGU_BLOCK, gd))],
        out_specs=pl.BlockSpec((ts, W), lambda i: (i, 0)), out_shape=jax.ShapeDtypeStruct((S, W), BF16),
        compiler_params=_params("parallel"), name=name)(pre, b_in, vgain, vbias, wm, bsb)


def _sgu_bwd(pre, dy, b_in, vgain, vbias, wm, wmt, bsb, *, name, ts=256):
    S, W2 = pre.shape
    W = W2 // 2
    gd = W // SGU_GROUPS
    ts = _tile(S, ts)
    last = S // ts - 1

    def body(p_ref, dy_ref, b_ref, vg_ref, vb_ref, wm_ref, wmt_ref, bs_ref,
             dp_ref, db_ref, dvg_ref, dvb_ref, dws_ref, dbs_ref, du_s, dvn_s, dbs_s):
        step = pl.program_id(0)
        _acc_init(step, db_ref, dvg_ref, dvb_ref, dws_ref, dbs_s)
        u, gu = _gelu_parts(p_ref[:, pl.ds(0, W)].astype(F32) + b_ref[:, pl.ds(0, W)])
        v, gv = _gelu_parts(p_ref[:, pl.ds(W, W)].astype(F32) + b_ref[:, pl.ds(W, W)])
        mu = jnp.mean(v, axis=-1, keepdims=True)
        vc = v - mu
        rstd = lax.rsqrt(jnp.mean(vc * vc, axis=-1, keepdims=True) + EPS)
        vhat = vc * rstd
        vn = (vhat * vg_ref[...] + vb_ref[...]).astype(BF16)
        dyv = dy_ref[...].astype(F32)
        for blk in range(ts // SGU_BLOCK):
            r0 = blk * SGU_BLOCK
            for g in range(SGU_GROUPS):
                c0 = g * gd
                vn_g = vn[r0:r0 + SGU_BLOCK, c0:c0 + gd]
                dy_g = dyv[r0:r0 + SGU_BLOCK, c0:c0 + gd]
                mixed = jnp.dot(wm_ref[g], vn_g, preferred_element_type=F32) + bs_ref[g]
                dmix = dy_g * u[r0:r0 + SGU_BLOCK, c0:c0 + gd]
                dmix_b = dmix.astype(BF16)
                du_s[pl.ds(r0, SGU_BLOCK), pl.ds(c0, gd)] = dy_g * mixed
                dvn_s[pl.ds(r0, SGU_BLOCK), pl.ds(c0, gd)] = jnp.dot(wmt_ref[g], dmix_b, preferred_element_type=F32)
                dws_ref[g] += lax.dot_general(dmix_b, vn_g, _NT, preferred_element_type=F32)
                dbs_s[g] += dmix
        dvn = dvn_s[...]
        dvg_ref[...] += _colsum(dvn * vhat)
        dvb_ref[...] += _colsum(dvn)
        dvh = dvn * vg_ref[...]
        dv = rstd * (dvh - jnp.mean(dvh, axis=-1, keepdims=True) - vhat * jnp.mean(dvh * vhat, axis=-1, keepdims=True))
        dpu = du_s[...] * gu
        dpv = dv * gv
        dp_ref[:, pl.ds(0, W)] = dpu.astype(BF16)
        dp_ref[:, pl.ds(W, W)] = dpv.astype(BF16)
        db_ref[:, pl.ds(0, W)] += _colsum(dpu)
        db_ref[:, pl.ds(W, W)] += _colsum(dpv)

        @pl.when(step == last)
        def _():
            for g in range(SGU_GROUPS):
                dbs_ref[g] = jnp.broadcast_to(jnp.sum(dbs_s[g], axis=-1, keepdims=True), (SGU_BLOCK, SGU_BLOCK))

    full = lambda shape: pl.BlockSpec(shape, lambda i: (0,) * len(shape))
    gsq = (SGU_GROUPS, SGU_BLOCK, SGU_BLOCK)
    return pl.pallas_call(
        body, grid=(S // ts,),
        in_specs=[pl.BlockSpec((ts, W2), lambda i: (i, 0)), pl.BlockSpec((ts, W), lambda i: (i, 0)),
                  full((1, W2)), full((1, W)), full((1, W)), full(gsq), full(gsq), full((SGU_GROUPS, SGU_BLOCK, gd))],
        out_specs=[pl.BlockSpec((ts, W2), lambda i: (i, 0)), full((1, W2)), full((1, W)), full((1, W)), full(gsq), full(gsq)],
        out_shape=[jax.ShapeDtypeStruct((S, W2), BF16), jax.ShapeDtypeStruct((1, W2), F32),
                   jax.ShapeDtypeStruct((1, W), F32), jax.ShapeDtypeStruct((1, W), F32),
                   jax.ShapeDtypeStruct(gsq, F32), jax.ShapeDtypeStruct(gsq, F32)],
        scratch_shapes=[pltpu.VMEM((ts, W), F32), pltpu.VMEM((ts, W), F32), pltpu.VMEM((SGU_GROUPS, SGU_BLOCK, gd), F32)],
        compiler_params=_params("arbitrary"), name=name)(pre, dy, b_in, vgain, vbias, wm, wmt, bsb)


def _paired_to_natural(w_up):
    nc = w_up.shape[1] // (2 * CONV_TC)
    return lambda q: (q % 2) * nc + q // 2


def _ffn_fwd(x, mods, n2g, w_up, cw, cb, w_down, tag):
    sh, sc, gate = mods
    h = _norm_mod_fwd(x, n2g, sh, sc, name=f"{tag}_norm_fwd")
    a = _mm(h, w_up, out_dtype=BF16, tn=CONV_TC, b_n=_paired_to_natural(w_up), name=f"{tag}_up")
    act = _conv_gate_fwd(a, cw, cb, name=f"{tag}_conv_fwd")
    x_out, y = _mm(act, w_down, tk=1408, res=(x, gate), name=f"{tag}_down")
    return x_out, (x, h, a, act, y)


def _ffn_bwd(dy, saved, mods, n2g, w_up, cw, cb, w_down, dres, prev, tag):
    x, h, a, act, _ = saved
    sh, sc, gate = mods
    dact = _mm(dy, w_down, tb=True, out_dtype=BF16, tn=1408, name=f"{tag}_down_dx")
    dw_down = _mm(act, dy, ta=True, out_dtype=BF16, tm=1408, name=f"{tag}_down_dw")
    da, sums = _conv_gate_bwd(a, dact, cw, cb, name=f"{tag}_conv_bwd")
    dh = _mm(da, w_up, tb=True, tk=CONV_TC, b_k=_paired_to_natural(w_up), name=f"{tag}_up_dx")
    dw_up = _mm(h, da, ta=True, out_dtype=BF16, tn=CONV_TC, o_n=_paired_to_natural(w_up), name=f"{tag}_up_dw")
    outs = _norm_mod_bwd(dh, x, n2g, sc, dres, prev, name=f"{tag}_norm_bwd")
    sums = _unpair_tiles(sums)
    return outs, dict(w_up=dw_up, w_down=dw_down, conv_w=sums[0:CONV_WIDTH], conv_b=sums[CONV_WIDTH])


def _local_step(x, target, w, mods, late=None, early=None, last=None):
    S, D = x.shape
    lane = jnp.arange(LANES)
    gmat = jnp.where((lane[:, None] // HEAD_DIM) == (lane[None, :] // HEAD_DIM), 1.0 / HEAD_DIM, 0.0).astype(BF16)
    qg2 = jnp.tile(w["fox_q_gain"].reshape(1, HEAD_DIM), (1, 2))
    kg2 = jnp.tile(w["fox_k_gain"].reshape(1, HEAD_DIM), (1, 2))
    bf_pad = jnp.pad(w["fox_b_f"].reshape(1, HEADS), ((0, 0), (0, LANES - HEADS)))
    w_in_pad = jnp.pad(w["fox_w_in"], ((0, 0), (0, 4 * D + LANES - w["fox_w_in"].shape[1])))
    w_qkvo, w_f = w_in_pad[:, :4 * D], w_in_pad[:, 4 * D:]
    tpos = jnp.arange(SGU_BLOCK)
    smask = (tpos[None, :] // SGU_CHUNK) <= (tpos[:, None] // SGU_CHUNK)
    wm32 = jnp.where(smask[None], w["sgu_w_s"], 0.0)
    wm, wmt = wm32.astype(BF16), jnp.swapaxes(wm32, 1, 2).astype(BF16)
    gd = w["sgu_v_gain"].shape[-1] // SGU_GROUPS
    bsb = jnp.broadcast_to(w["sgu_b_s"][:, :, None], (SGU_GROUPS, SGU_BLOCK, gd))
    vec = lambda v: v.reshape(1, -1)

    sh1, sc1, g1 = mods[0][0:3]
    h0 = _norm_mod_fwd(x, vec(w["norm1_g"][0]), sh1, sc1, name="fox_norm_fwd")
    proj = _mm(h0, w_qkvo, out_dtype=BF16, name="fox_proj")
    fl = _mm(h0, w_f, name="fox_forget_proj")
    fcum = _fox_decay_fwd(fl, bf_pad, name="fox_decay")
    q_aug, k_aug, v_aug = _fox_prep_fwd(proj, fcum, qg2, kg2, gmat, name="fox_qk_norm")
    logit_bound = 8.0 * jnp.max(jnp.abs(w["fox_q_gain"])) * jnp.max(jnp.abs(w["fox_k_gain"]))
    q_max = lax.cond(logit_bound <= SHIFT_FREE_LOGIT_BOUND, lambda: q_aug,
                     lambda: _attn_rowmax(q_aug, k_aug, name="fox_attn_rowmax"))
    xchg = None if late is None else (late[0], [False] * len(late[0]))
    att_aug, q_lse, gathered = _attn_fwd(q_max, k_aug, v_aug, xchg, name="fox_attn_fwd")
    if late is not None:
        w = {**w, **late[1](gathered)}
    w = dict(w, ffn_conv_w=_pair_tiles(w["ffn_conv_w"]), ffn_conv_b=_pair_tiles(w["ffn_conv_b"]))
    att, ag = _fox_gate_fwd(att_aug, proj, name="fox_gate_fwd")
    x1, y_fox = _mm(ag, w["fox_w_out"], res=(x, g1), name="fox_out")
    x2, ffn0 = _ffn_fwd(x1, mods[0][3:6], vec(w["norm2_g"][0]), w["ffn_w_up"][0], w["ffn_conv_w"][0],
                        vec(w["ffn_conv_b"][0]), w["ffn_w_down"][0], "ffn0")

    sh1b, sc1b, g1b = mods[1][0:3]
    h1 = _norm_mod_fwd(x2, vec(w["norm1_g"][1]), sh1b, sc1b, name="sgu_norm_fwd")
    pre = _mm(h1, w["sgu_w_in"], out_dtype=BF16, name="sgu_in")
    b_in, vg, vb = vec(w["sgu_b_in"]), vec(w["sgu_v_gain"]), vec(w["sgu_v_bias"])
    ys = _sgu_fwd(pre, b_in, vg, vb, wm, bsb, name="sgu_core_fwd")
    x3, y_sgu = _mm(ys, w["sgu_w_out"], res=(x2, g1b), name="sgu_out")
    x4, ffn1 = _ffn_fwd(x3, mods[1][3:6], vec(w["norm2_g"][1]), w["ffn_w_up"][1], w["ffn_conv_w"][1],
                        vec(w["ffn_conv_b"][1]), w["ffn_w_down"][1], "ffn1")

    loss, d_final_g, dx4, dy_ffn1, dgate_ffn1 = _final_loss(x4, vec(w["final_g"]), target, ffn1[4], mods[1][5], name="final_loss")

    (dx3, dn2g_1, dsh2_1, dsc2_1, dy_sgu, dgate_sgu), g_ffn1 = _ffn_bwd(
        dy_ffn1, ffn1, mods[1][3:6], vec(w["norm2_g"][1]), w["ffn_w_up"][1], w["ffn_conv_w"][1], vec(w["ffn_conv_b"][1]),
        w["ffn_w_down"][1], dx4, (y_sgu, g1b), "ffn1")

    dys = _mm(dy_sgu, w["sgu_w_out"], tb=True, out_dtype=BF16, name="sgu_out_dx")
    dw_sgu_out = _mm(ys, dy_sgu, ta=True, out_dtype=BF16, name="sgu_out_dw")
    dpre, db_in, dvg, dvb, dws, dbs = _sgu_bwd(pre, dys, b_in, vg, vb, wm, wmt, bsb, name="sgu_core_bwd")
    dh1 = _mm(dpre, w["sgu_w_in"], tb=True, name="sgu_in_dx")
    dw_sgu_in = _mm(h1, dpre, ta=True, out_dtype=BF16, name="sgu_in_dw")
    dx2, dn1g_1, dsh1_1, dsc1_1, dy_ffn0, dgate_ffn0 = _norm_mod_bwd(
        dh1, x2, vec(w["norm1_g"][1]), sc1b, dx3, (ffn0[4], mods[0][5]), name="sgu_norm_bwd")

    (dx1, dn2g_0, dsh2_0, dsc2_0, dy_fox, dgate_fox), g_ffn0 = _ffn_bwd(
        dy_ffn0, ffn0, mods[0][3:6], vec(w["norm2_g"][0]), w["ffn_w_up"][0], w["ffn_conv_w"][0], vec(w["ffn_conv_b"][0]),
        w["ffn_w_down"][0], dx2, (y_fox, g1), "ffn0")

    dag = _mm(dy_fox, w["fox_w_out"], tb=True, out_dtype=BF16, name="fox_out_dx")
    dw_fox_out = _mm(ag, dy_fox, ta=True, out_dtype=BF16, name="fox_out_dw")
    do_aug, dog = _fox_gate_bwd(dag, att, proj, name="fox_gate_bwd")
    grads = dict(
        sgu_w_in=dw_sgu_in, sgu_b_in=db_in[0], sgu_v_gain=dvg[0], sgu_v_bias=dvb[0],
        sgu_w_s=jnp.where(smask[None], dws, 0.0), sgu_b_s=dbs[:, :, 0], sgu_w_out=dw_sgu_out,
        ffn_w_up=jnp.stack([g_ffn0["w_up"], g_ffn1["w_up"]]),
        ffn_conv_w=jnp.stack([g_ffn0["conv_w"], g_ffn1["conv_w"]]),
        ffn_conv_b=jnp.stack([g_ffn0["conv_b"], g_ffn1["conv_b"]]),
        ffn_w_down=jnp.stack([g_ffn0["w_down"], g_ffn1["w_down"]]),
        final_g=d_final_g[0], fox_w_out=dw_fox_out, norm2_g=jnp.concatenate([dn2g_0, dn2g_1], axis=0),
    )
    xchg = None if early is None else early(grads)
    dq_aug, dk_aug, dv_aug, exchanged = _attn_bwd(q_lse, k_aug, v_aug, do_aug, xchg, name="fox_attn_bwd")
    dproj, dF, dqg, dkg = _fox_prep_bwd(proj, dq_aug, dk_aug, dv_aug, dog, qg2, kg2, gmat, name="fox_qk_norm_bwd")
    dproj, dbf = _fox_decay_bwd(dF, fl, bf_pad, dproj, name="fox_decay_bwd")
    dw_fox_in = _mm(h0, dproj, ta=True, out_dtype=BF16, tn=1408, name="fox_proj_dw")[:, :w["fox_w_in"].shape[1]]
    xchg = None if last is None else last(dict(fox_w_in=dw_fox_in))
    dh0 = _mm(dproj, w_in_pad, tb=True, tk=1408, xchg=xchg, name="fox_proj_dx")
    dh0, exchanged_last = dh0 if last is not None else (dh0, [])
    dx0, dn1g_0, dsh1_0, dsc1_0 = _norm_mod_bwd(dh0, x, vec(w["norm1_g"][0]), sc1, dx1, None, name="fox_norm_bwd")

    dmod0 = jnp.concatenate([dsh1_0, dsc1_0, dgate_fox, dsh2_0, dsc2_0, dgate_ffn0], axis=1)
    dmod1 = jnp.concatenate([dsh1_1, dsc1_1, dgate_sgu, dsh2_1, dsc2_1, dgate_ffn1], axis=1)
    grads.update(
        fox_w_in=dw_fox_in,
        fox_b_f=dbf[0, :HEADS],
        fox_q_gain=dqg[0, :HEAD_DIM] + dqg[0, HEAD_DIM:],
        fox_k_gain=dkg[0, :HEAD_DIM] + dkg[0, HEAD_DIM:],
        fox_w_out=dw_fox_out,
        ada_b=jnp.concatenate([dmod0, dmod1], axis=0),
        norm1_g=jnp.concatenate([dn1g_0, dn1g_1], axis=0),
    )
    return loss[0, 0], dx0, grads, exchanged, exchanged_last


_HBM = pl.BlockSpec(memory_space=pl.ANY)
N_PEER = N_DEV - 1


def _xchg_out_shapes(arrs, scatter):
    return [jax.ShapeDtypeStruct(a.shape if s else (N_DEV,) + a.shape, a.dtype) for a, s in zip(arrs, scatter)]


def _xchg_sems(n):
    return [pltpu.SemaphoreType.DMA((n * N_PEER,)), pltpu.SemaphoreType.DMA((n * N_PEER,)), pltpu.SemaphoreType.DMA((n,))]


def _xchg_copies(ins, outs, scatter, send, recv, loc):
    x, y, c = lax.axis_index("x"), lax.axis_index("y"), lax.axis_index("c")
    me = 4 * x + 2 * y + c
    copies = []
    for a in range(len(ins)):
        copies.append(pltpu.make_async_copy(ins[a].at[me] if scatter[a] else ins[a], outs[a].at[me], loc.at[a]))
        for k in range(1, N_DEV):
            px = 1 - x if k & 4 else x
            py = 1 - y if k & 2 else y
            pc = 1 - c if k & 1 else c
            copies.append(pltpu.make_async_remote_copy(
                src_ref=ins[a].at[4 * px + 2 * py + pc] if scatter[a] else ins[a], dst_ref=outs[a].at[me],
                send_sem=send.at[a * N_PEER + k - 1], recv_sem=recv.at[a * N_PEER + k - 1],
                device_id=(px, py, pc), device_id_type=MESH))
    return copies


def _exchange(arrs, scatter, *, name):
    n = len(arrs)

    def body(*refs):
        copies = _xchg_copies(refs[:n], refs[n:2 * n], scatter, *refs[2 * n:])
        for cp in copies:
            cp.start()
        for cp in copies:
            cp.wait()

    return pl.pallas_call(
        body, in_specs=[_HBM] * n, out_specs=[_HBM] * n, out_shape=_xchg_out_shapes(arrs, scatter),
        scratch_shapes=_xchg_sems(n),
        compiler_params=pltpu.CompilerParams(has_side_effects=True), name=name)(*arrs)


def _adamw(w, parts, m, v, *, name, tr=256):
    L, R, C = w.shape
    P = parts.shape[0]
    tr = next(t for t in range(min(R, tr), 0, -1) if R % t == 0 and (t % 16 == 0 or t == R))
    nr = R // tr
    c1 = 1.0 - ADAM_B1 ** ADAM_STEP
    c2 = 1.0 - ADAM_B2 ** ADAM_STEP

    def body(w_ref, p_ref, m_ref, v_ref, g_ref, d_ref, mo_ref, vo_ref):
        g = p_ref[0].astype(F32)
        for p in range(1, P):
            g = g + p_ref[p].astype(F32)
        mn = ADAM_B1 * m_ref[0] + (1.0 - ADAM_B1) * g
        vn = ADAM_B2 * v_ref[0] + (1.0 - ADAM_B2) * (g * g)
        g_ref[0] = g
        mo_ref[0] = mn
        vo_ref[0] = vn
        d_ref[0] = -ADAM_LR * ((mn / c1) / (jnp.sqrt(vn / c2) + ADAM_EPS) + ADAM_WD * w_ref[0])

    row = pl.BlockSpec((1, tr, C), lambda l, i: (l, i, 0))
    return pl.pallas_call(
        body, grid=(L, nr), in_specs=[row, pl.BlockSpec((P, tr, C), lambda l, i: (0, l * nr + i, 0)), row, row],
        out_specs=[row] * 4, out_shape=[jax.ShapeDtypeStruct((L, R, C), F32)] * 4,
        compiler_params=_params("parallel", "parallel"), name=name)(w, parts, m, v)


def _sum_parts(parts, *, name):
    P, R, C = parts.shape

    def body(p_ref, o_ref):
        g = p_ref[0]
        for p in range(1, P):
            g = g + p_ref[p]
        o_ref[...] = g

    return pl.pallas_call(body, out_shape=jax.ShapeDtypeStruct((R, C), F32), name=name)(parts)


WEIGHTS = ["fox_w_in", "fox_b_f", "fox_q_gain", "fox_k_gain", "fox_w_out", "sgu_w_in", "sgu_b_in", "sgu_v_gain",
           "sgu_v_bias", "sgu_w_s", "sgu_b_s", "sgu_w_out", "ffn_w_up", "ffn_conv_w", "ffn_conv_b", "ffn_w_down",
           "ada_w", "ada_b", "norm1_g", "norm2_g", "final_g"]
BIG_AXIS = dict(fox_w_in=1, fox_w_out=0, sgu_w_in=1, sgu_w_out=0, ffn_w_up=1, ffn_w_down=0, ada_w=1)
SMALL_SHARDED = ["sgu_b_in", "sgu_v_gain", "sgu_v_bias", "ffn_conv_w"]
SINGLE_LAYER = ("fox_", "sgu_")
BEFORE_ATTENTION = ["fox_w_in"]
AFTER_ATTENTION = ["fox_w_out", "sgu_w_in", "sgu_w_out", "ffn_w_up", "ffn_w_down"]
SMALL_EARLY = ["sgu_b_in", "sgu_v_gain", "sgu_v_bias", "sgu_w_s", "sgu_b_s", "ffn_conv_w", "ffn_conv_b", "norm2_g", "final_g"]


def _assemble(stacked, layers, axis):
    _, lr, cc = stacked.shape
    r = lr // layers
    s4 = stacked.reshape(N_DEV, layers, r, cc)
    if axis == 0:
        return s4.transpose(1, 0, 2, 3).reshape(layers, N_DEV * r, cc)
    return s4.transpose(1, 2, 0, 3).reshape(layers, r, N_DEV * cc)


def _disassemble(full, axis):
    layers, R, C = full.shape
    if axis == 0:
        r = R // N_DEV
        return full.reshape(layers, N_DEV, r, C).transpose(1, 0, 2, 3).reshape(N_DEV, layers * r, C)
    cc = C // N_DEV
    return full.reshape(layers, R, N_DEV, cc).transpose(2, 0, 1, 3).reshape(N_DEV, layers * R, cc)


def kernel(x, c, fox_w_in, fox_b_f, fox_q_gain, fox_k_gain, fox_w_out, sgu_w_in, sgu_b_in, sgu_v_gain, sgu_v_bias, sgu_w_s, sgu_b_s, sgu_w_out, ffn_w_up, ffn_conv_w, ffn_conv_b, ffn_w_down, ada_w, ada_b, norm1_g, norm2_g, final_g, loss_target, m_fox_w_in, m_fox_b_f, m_fox_q_gain, m_fox_k_gain, m_fox_w_out, m_sgu_w_in, m_sgu_b_in, m_sgu_v_gain, m_sgu_v_bias, m_sgu_w_s, m_sgu_b_s, m_sgu_w_out, m_ffn_w_up, m_ffn_conv_w, m_ffn_conv_b, m_ffn_w_down, m_ada_w, m_ada_b, m_norm1_g, m_norm2_g, m_final_g, v_fox_w_in, v_fox_b_f, v_fox_q_gain, v_fox_k_gain, v_fox_w_out, v_sgu_w_in, v_sgu_b_in, v_sgu_v_gain, v_sgu_v_bias, v_sgu_w_s, v_sgu_b_s, v_sgu_w_out, v_ffn_w_up, v_ffn_conv_w, v_ffn_conv_b, v_ffn_w_down, v_ada_w, v_ada_b, v_norm1_g, v_norm2_g, v_final_g):
    args = dict(locals())
    wts = {n: args[n] for n in WEIGHTS}
    ms = {n: args["m_" + n] for n in WEIGHTS}
    vs = {n: args["v_" + n] for n in WEIGHTS}
    me = 4 * lax.axis_index("x") + 2 * lax.axis_index("y") + lax.axis_index("c")

    shard2d = lambda n: wts[n].astype(BF16).reshape(-1, wts[n].shape[-1])

    def assemble_big(names, got):
        out = {}
        for n, g in zip(names, got):
            f = _assemble(g, wts[n].shape[0], BIG_AXIS[n])
            out[n] = f[0] if n.startswith(SINGLE_LAYER) else f
        return out

    def blocks_of(names, grads):
        return [_disassemble(grads[n] if grads[n].ndim == 3 else grads[n][None], BIG_AXIS[n]) for n in names]

    send = [c] + [shard2d(n) for n in BEFORE_ATTENTION] + [wts[n].reshape(-1, wts[n].shape[-1]) for n in SMALL_SHARDED]
    got = _exchange(send, [False] * len(send), name="gather_first")
    c_all = got[0].reshape(N_DEV, -1)
    full = assemble_big(BEFORE_ATTENTION, got[1:1 + len(BEFORE_ATTENTION)])
    for n, g in zip(SMALL_SHARDED, got[1 + len(BEFORE_ATTENTION):]):
        lead = wts[n].shape[:-1]
        f = jnp.moveaxis(g.reshape((N_DEV,) + wts[n].shape), 0, -2).reshape(lead + (-1,))
        full[n] = f[0] if n.startswith(SINGLE_LAYER) else f
    for n in WEIGHTS:
        if n not in full and n not in BIG_AXIS:
            full[n] = wts[n][0] if n.startswith(SINGLE_LAYER) else wts[n]

    ada_cols = wts["ada_w"].shape[-1]
    mod_rows = []
    for i in range(2):
        b_mine = lax.dynamic_slice_in_dim(wts["ada_b"][i], me * ada_cols, ada_cols).reshape(1, ada_cols)
        m, c_act = _ada_mod(c_all, wts["ada_w"][i].astype(BF16), b_mine, name=f"ada_mod_{i}")
        mod_rows.append(m)
    got = _exchange([jnp.concatenate(mod_rows, axis=1)[:, None, :]], [True], name="exchange_mods")[0]
    d_model = x.shape[-1]
    mods = []
    for i in range(2):
        mod = got[:, 0, i * ada_cols:(i + 1) * ada_cols].reshape(1, N_DEV * ada_cols)
        mods.append([mod[:, k * d_model:(k + 1) * d_model] for k in range(6)])

    small = [n for n in WEIGHTS if n not in BIG_AXIS]
    small_late = [n for n in small if n not in SMALL_EARLY]

    def pack_flat(arrays):
        flat = jnp.concatenate([a.reshape(-1).astype(F32) for a in arrays])
        rows = -(-flat.shape[0] // (8 * LANES)) * 8
        return jnp.pad(flat, (0, rows * LANES - flat.shape[0])).reshape(rows, LANES)

    late = ([shard2d(n) for n in AFTER_ATTENTION], lambda g: assemble_big(AFTER_ATTENTION, g))
    loss, grad_x, grads, got_late, got_last = _local_step(
        x[0], loss_target[0], full, mods, late,
        lambda gr: (blocks_of(AFTER_ATTENTION, gr) + [pack_flat([gr[n] for n in SMALL_EARLY])],
                    [True] * len(AFTER_ATTENTION) + [False]),
        lambda gr: (blocks_of(BEFORE_ATTENTION, gr), [True] * len(BEFORE_ATTENTION)))

    flat_late_all = _exchange([pack_flat([loss] + [grads[n] for n in small_late])], [False], name="gather_small_grads")[0]
    total_late = _sum_parts(flat_late_all, name="sum_small_grads_late").reshape(-1)
    total_early = _sum_parts(got_late[len(AFTER_ATTENTION)], name="sum_small_grads_early").reshape(-1)
    loss_out = total_late[0]
    summed, offs = {}, {}
    for vec, names, off in ((total_early, SMALL_EARLY, 0), (total_late, small_late, 1)):
        for n in names:
            size = math.prod(grads[n].shape)
            summed[n], offs[n] = vec[off:off + size].reshape(grads[n].shape), off
            off += size

    off_ada = offs["ada_b"]
    dmod_all = flat_late_all.reshape(N_DEV, -1)[:, off_ada:off_ada + 2 * N_DEV * ada_cols].reshape(N_DEV, 2, N_DEV * ada_cols)
    dmod_mine = lax.dynamic_slice_in_dim(dmod_all, me * ada_cols, ada_cols, axis=2)
    d_ada = [_mm(c_act, jnp.pad(dmod_mine[:, i], ((0, c_act.shape[0] - N_DEV), (0, 0))).astype(BF16), ta=True,
                 name=f"ada_dw_{i}") for i in range(2)]

    out_g, out_d, out_m, out_v = {}, {}, {}, {}
    summands = dict(zip(BEFORE_ATTENTION, got_last))
    summands.update(zip(AFTER_ATTENTION, got_late))
    summands["ada_w"] = jnp.concatenate(d_ada, axis=0)[None]
    for n, p in summands.items():
        out_g[n], out_d[n], out_m[n], out_v[n] = _adamw(wts[n], p, ms[n], vs[n], name=f"adamw_{n}")
    small_g = {}
    for n in small:
        g = summed[n]
        if n in SMALL_SHARDED:
            blk = g.shape[-1] // N_DEV
            g = lax.dynamic_slice_in_dim(g, me * blk, blk, axis=g.ndim - 1)
        small_g[n] = g.reshape(wts[n].shape)
    cat = lambda d: jnp.concatenate([d[n].reshape(-1) for n in small])
    n_small = sum(math.prod(wts[n].shape) for n in small)
    rows2 = -(-n_small // (256 * LANES)) * 256
    pack = lambda d, fill: jnp.pad(cat(d), (0, rows2 * LANES - n_small), constant_values=fill).reshape(1, rows2, LANES)
    g, d, mn, vn = _adamw(pack(wts, 0.0), pack(small_g, 0.0), pack(ms, 0.0), pack(vs, 1.0), name="adamw_small")
    off = 0
    for n in small:
        size = math.prod(wts[n].shape)
        for src, dst in ((g, out_g), (d, out_d), (mn, out_m), (vn, out_v)):
            dst[n] = src.reshape(-1)[off:off + size].reshape(wts[n].shape)
        off += size

    return (loss_out, grad_x[None], *[out_g[n] for n in WEIGHTS], *[out_d[n] for n in WEIGHTS],
            *[out_m[n] for n in WEIGHTS], *[out_v[n] for n in WEIGHTS])
```

```python
import functools
import math

import jax
import jax.numpy as jnp
from jax import lax
from jax.experimental import pallas as pl
from jax.experimental.pallas import tpu as pltpu

F32, BF16 = jnp.float32, jnp.bfloat16
N_DEV = 8
HEADS, HEAD_DIM = 16, 64
HEAD_PAIRS = HEADS // 2
LANES = 128
SUBLANES = 8
EPS = 1e-6
SGU_BLOCK, SGU_GROUPS, SGU_CHUNK = 128, 8, 64
CONV_WIDTH = 3
ADAM_LR, ADAM_B1, ADAM_B2, ADAM_EPS, ADAM_WD, ADAM_STEP = 0.001, 0.9, 0.999, 1e-08, 0.01, 10
NEG = -1e30
GELU_C0, GELU_C1 = math.sqrt(2.0 / math.pi), 0.044715
MESH = pl.DeviceIdType.MESH
VMEM_LIMIT = 56 * 1024 * 1024


def _tile(dim, pref):
    if dim <= pref:
        return dim
    t = (pref // LANES) * LANES
    while t >= LANES:
        if dim % t == 0:
            return t
        t -= LANES
    return dim


def _params(*sem):
    return pltpu.CompilerParams(dimension_semantics=sem, vmem_limit_bytes=VMEM_LIMIT)


def _mm(a, b, *, name, ta=False, tb=False, out_dtype=F32, tm=1024, tn=1024, tk=1024, res=None, b_n=None, b_k=None, o_n=None,
        xchg=None):
    M = a.shape[1] if ta else a.shape[0]
    K = a.shape[0] if ta else a.shape[1]
    N = b.shape[0] if tb else b.shape[1]
    tm, tn, tk = _tile(M, tm), _tile(N, tn), _tile(K, tk)
    nk = K // tk
    dims = (((0 if ta else 1,), (1 if tb else 0,)), ((), ()))
    same = lambda idx: idx
    b_n, b_k, o_n = b_n or same, b_k or same, o_n or same
    a_spec = pl.BlockSpec((tk, tm), lambda i, j, k: (k, i)) if ta else pl.BlockSpec((tm, tk), lambda i, j, k: (i, k))
    b_spec = (pl.BlockSpec((tn, tk), lambda i, j, k: (b_n(j), b_k(k))) if tb
              else pl.BlockSpec((tk, tn), lambda i, j, k: (b_k(k), b_n(j))))
    o_spec = pl.BlockSpec((tm, tn), lambda i, j, k: (i, o_n(j)))

    def accumulate(a_ref, b_ref, acc):
        @pl.when(pl.program_id(2) == 0)
        def _():
            acc[...] = jnp.zeros_like(acc)
        acc[...] += lax.dot_general(a_ref[...], b_ref[...], dims, preferred_element_type=F32)

    if res is None:
        def body(a_ref, b_ref, o_ref, acc):
            accumulate(a_ref, b_ref, acc)

            @pl.when(pl.program_id(2) == nk - 1)
            def _():
                o_ref[...] = acc[...].astype(o_ref.dtype)

        if xchg is None:
            return pl.pallas_call(
                body, grid=(M // tm, N // tn, nk), in_specs=[a_spec, b_spec], out_specs=o_spec,
                out_shape=jax.ShapeDtypeStruct((M, N), out_dtype), scratch_shapes=[pltpu.VMEM((tm, tn), F32)],
                compiler_params=_params("parallel", "parallel", "arbitrary"), name=name)(a, b)
        grid = (M // tm, N // tn, nk)
        wrap, x_in, x_out, x_shapes, x_sems, x_ops = _ride_along(xchg, 2, 1, grid)
        outs = pl.pallas_call(
            wrap(body), grid=grid, in_specs=[a_spec, b_spec] + x_in, out_specs=[o_spec] + x_out,
            out_shape=[jax.ShapeDtypeStruct((M, N), out_dtype)] + x_shapes,
            scratch_shapes=[pltpu.VMEM((tm, tn), F32)] + x_sems,
            compiler_params=_params("arbitrary", "arbitrary", "arbitrary"), name=name)(a, b, *x_ops)
        return outs[0], outs[1:]

    x, gate = res

    def body_res(a_ref, b_ref, x_ref, g_ref, o_ref, y_ref, acc):
        accumulate(a_ref, b_ref, acc)

        @pl.when(pl.program_id(2) == nk - 1)
        def _():
            y = acc[...]
            o_ref[...] = x_ref[...] + g_ref[...] * y
            y_ref[...] = y.astype(BF16)

    return pl.pallas_call(
        body_res, grid=(M // tm, N // tn, nk),
        in_specs=[a_spec, b_spec, o_spec, pl.BlockSpec((1, tn), lambda i, j, k: (0, j))],
        out_specs=[o_spec, o_spec],
        out_shape=[jax.ShapeDtypeStruct((M, N), F32), jax.ShapeDtypeStruct((M, N), BF16)],
        scratch_shapes=[pltpu.VMEM((tm, tn), F32)],
        compiler_params=_params("parallel", "parallel", "arbitrary"), name=name)(a, b, x, gate)


def _ada_mod(c_rows, w, b, *, name):
    R, D = c_rows.shape
    N = w.shape[1]
    tn = _tile(N, 1536)
    rows = 16
    c_pad = jnp.pad(c_rows, ((0, rows - R), (0, 0)))

    def body(c_ref, w_ref, b_ref, o_ref, ca_ref):
        cv = c_ref[...]
        ca16 = (cv * jax.nn.sigmoid(cv)).astype(BF16)
        ca_ref[...] = ca16
        o_ref[...] = jnp.dot(ca16, w_ref[...], preferred_element_type=F32) + b_ref[...]

    out, ca = pl.pallas_call(
        body, grid=(N // tn,),
        in_specs=[pl.BlockSpec((rows, D), lambda j: (0, 0)), pl.BlockSpec((D, tn), lambda j: (0, j)),
                  pl.BlockSpec((1, tn), lambda j: (0, j))],
        out_specs=[pl.BlockSpec((rows, tn), lambda j: (0, j)), pl.BlockSpec((rows, D), lambda j: (0, 0))],
        out_shape=[jax.ShapeDtypeStruct((rows, N), F32), jax.ShapeDtypeStruct((rows, D), BF16)],
        compiler_params=_params("arbitrary"), name=name)(c_pad, w, b)
    return out[0:R], ca


def _norm_mod_fwd(x, g, shift, scale, *, name, ts=512):
    S, D = x.shape
    ts = _tile(S, ts)
    row = pl.BlockSpec((ts, D), lambda i: (i, 0))
    vec = pl.BlockSpec((1, D), lambda i: (0, 0))

    def body(x_ref, g_ref, sh_ref, sc_ref, h_ref):
        xv = x_ref[...]
        r = lax.rsqrt(jnp.mean(xv * xv, axis=-1, keepdims=True) + EPS)
        h_ref[...] = ((xv * r * g_ref[...]) * (1.0 + sc_ref[...]) + sh_ref[...]).astype(BF16)

    return pl.pallas_call(body, grid=(S // ts,), in_specs=[row, vec, vec, vec], out_specs=row,
                          out_shape=jax.ShapeDtypeStruct((S, D), BF16),
                          compiler_params=_params("parallel"), name=name)(x, g, shift, scale)


def _acc_init(step, *refs):
    @pl.when(step == 0)
    def _():
        for r in refs:
            r[...] = jnp.zeros_like(r)


def _colsum(v):
    return jnp.sum(v, axis=0, keepdims=True)


def _norm_mod_bwd(dh, x, g, scale, dres, prev=None, *, name, ts=512):
    S, D = x.shape
    ts = _tile(S, ts)
    row = pl.BlockSpec((ts, D), lambda i: (i, 0))
    vec = pl.BlockSpec((1, D), lambda i: (0, 0))
    has_prev = prev is not None

    def body(*refs):
        if has_prev:
            dh_ref, x_ref, g_ref, sc_ref, dres_ref, y_ref, gate_ref, dx_ref, dg_ref, dsh_ref, dsc_ref, dy_ref, dgate_ref = refs
            _acc_init(pl.program_id(0), dg_ref, dsh_ref, dsc_ref, dgate_ref)
        else:
            dh_ref, x_ref, g_ref, sc_ref, dres_ref, dx_ref, dg_ref, dsh_ref, dsc_ref = refs
            _acc_init(pl.program_id(0), dg_ref, dsh_ref, dsc_ref)
        xv, dhv, gv = x_ref[...], dh_ref[...], g_ref[...]
        r = lax.rsqrt(jnp.mean(xv * xv, axis=-1, keepdims=True) + EPS)
        xh = xv * r
        dsh_ref[...] += _colsum(dhv)
        dsc_ref[...] += _colsum(dhv * (xh * gv))
        dn = dhv * (1.0 + sc_ref[...])
        dg_ref[...] += _colsum(dn * xh)
        dxh = dn * gv
        dx = dres_ref[...] + r * (dxh - xh * jnp.mean(dxh * xh, axis=-1, keepdims=True))
        dx_ref[...] = dx
        if has_prev:
            dy_ref[...] = (gate_ref[...] * dx).astype(BF16)
            dgate_ref[...] += _colsum(dx * y_ref[...].astype(F32))

    ins, in_specs = [dh, x, g, scale, dres], [row, row, vec, vec, row]
    outs = [jax.ShapeDtypeStruct((S, D), F32)] + [jax.ShapeDtypeStruct((1, D), F32)] * 3
    out_specs = [row, vec, vec, vec]
    if has_prev:
        ins += list(prev)
        in_specs += [row, vec]
        outs += [jax.ShapeDtypeStruct((S, D), BF16), jax.ShapeDtypeStruct((1, D), F32)]
        out_specs += [row, vec]
    return pl.pallas_call(body, grid=(S // ts,), in_specs=in_specs, out_specs=out_specs, out_shape=outs,
                          compiler_params=_params("arbitrary"), name=name)(*ins)


def _final_loss(x, g, target, y, gate, *, name, ts=512):
    S, D = x.shape
    ts = _tile(S, ts)
    row = pl.BlockSpec((ts, D), lambda i: (i, 0))
    vec = pl.BlockSpec((1, D), lambda i: (0, 0))
    lvec = pl.BlockSpec((1, LANES), lambda i: (0, 0))

    def body(x_ref, g_ref, t_ref, y_ref, gate_ref, loss_ref, dg_ref, dx_ref, dy_ref, dgate_ref):
        _acc_init(pl.program_id(0), loss_ref, dg_ref, dgate_ref)
        xv, gv = x_ref[...], g_ref[...]
        r = lax.rsqrt(jnp.mean(xv * xv, axis=-1, keepdims=True) + EPS)
        xh = xv * r
        e = xh * gv - t_ref[...]
        loss_ref[...] += 0.5 * jnp.sum(jnp.mean(e * e, axis=-1, keepdims=True), axis=0, keepdims=True)
        dout = e * (1.0 / D)
        dg_ref[...] += _colsum(dout * xh)
        dxh = dout * gv
        dx = r * (dxh - xh * jnp.mean(dxh * xh, axis=-1, keepdims=True))
        dx_ref[...] = dx
        dy_ref[...] = (gate_ref[...] * dx).astype(BF16)
        dgate_ref[...] += _colsum(dx * y_ref[...].astype(F32))

    return pl.pallas_call(
        body, grid=(S // ts,), in_specs=[row, vec, row, row, vec], out_specs=[lvec, vec, row, row, vec],
        out_shape=[jax.ShapeDtypeStruct((1, LANES), F32), jax.ShapeDtypeStruct((1, D), F32),
                   jax.ShapeDtypeStruct((S, D), F32), jax.ShapeDtypeStruct((S, D), BF16),
                   jax.ShapeDtypeStruct((1, D), F32)],
        compiler_params=_params("arbitrary"), name=name)(x, g, target, y, gate)


def _head_mean(v, gmat):
    hi = v.astype(BF16)
    lo = (v - hi.astype(F32)).astype(BF16)
    return jnp.dot(hi, gmat, preferred_element_type=F32) + jnp.dot(lo, gmat, preferred_element_type=F32)


L_F, L_ONE, L_SHIFT = HEAD_DIM, HEAD_DIM + 3, HEAD_DIM + 6
KEY_CHUNKS = 2
SHIFT_FREE_LOGIT_BOUND = 60.0


def _lane():
    return lax.broadcasted_iota(jnp.int32, (1, LANES), 1)


def _split3(v):
    p1 = v.astype(BF16).astype(F32)
    r1 = v - p1
    p2 = r1.astype(BF16).astype(F32)
    p3 = (r1 - p2).astype(BF16).astype(F32)
    return p1, p2, p3


def _put3(lane, first, pieces):
    out = jnp.where(lane == first, pieces[0], 0.0)
    for k in (1, 2):
        out = out + jnp.where(lane == first + k, pieces[k], 0.0)
    return out


def _ones3(lane, first):
    return jnp.where((lane >= first) & (lane < first + 3), 1.0, 0.0)


def _lane_col(v, lane, idx):
    return jnp.sum(jnp.where(lane == idx, v, 0.0), axis=-1, keepdims=True)


def _head_of_pair(pair, e, lane):
    return jnp.where(lane < HEAD_DIM, pair if e == 0 else pltpu.roll(pair, HEAD_DIM, 1), 0.0)


def _pair_of_heads(even, odd, lane):
    return jnp.where(lane < HEAD_DIM, even, pltpu.roll(odd, HEAD_DIM, 1))


def _fox_prep_fwd(proj, fcum, qgain, kgain, gmat, *, name, ts=256):
    S = proj.shape[0]
    D = HEADS * HEAD_DIM
    ts = _tile(S, ts)
    scale = HEAD_DIM ** -0.5

    def body(p_ref, f_ref, qg_ref, kg_ref, gm_ref, q_ref, k_ref, v_ref):
        gm, lane, fc = gm_ref[...], _lane(), f_ref[...]
        for cpair in range(HEAD_PAIRS):
            qv = p_ref[:, pl.ds(cpair * LANES, LANES)].astype(F32)
            kv = p_ref[:, pl.ds(D + cpair * LANES, LANES)].astype(F32)
            vv = p_ref[:, pl.ds(2 * D + cpair * LANES, LANES)].astype(F32)
            qn = (qv * lax.rsqrt(_head_mean(qv * qv, gm) + EPS) * qg_ref[...]) * scale
            kn = kv * lax.rsqrt(_head_mean(kv * kv, gm) + EPS) * kg_ref[...]
            for e in range(2):
                h = 2 * cpair + e
                cols = pl.ds(h * LANES, LANES)
                f3 = _split3(_lane_col(fc, lane, h))
                q_ref[:, cols] = (_head_of_pair(qn, e, lane) + _put3(lane, L_F, f3) + _ones3(lane, L_ONE)).astype(BF16)
                k_ref[:, cols] = (_head_of_pair(kn, e, lane) + _ones3(lane, L_F)
                                  - _put3(lane, L_ONE, f3) + _ones3(lane, L_SHIFT)).astype(BF16)
                v_ref[:, cols] = (_head_of_pair(vv, e, lane) + _ones3(lane, L_F)).astype(BF16)

    vec = pl.BlockSpec((1, LANES), lambda i: (0, 0))
    wide = pl.BlockSpec((ts, HEADS * LANES), lambda i: (i, 0))
    return pl.pallas_call(
        body, grid=(S // ts,),
        in_specs=[pl.BlockSpec((ts, 3 * D), lambda i: (i, 0)), pl.BlockSpec((ts, LANES), lambda i: (i, 0)), vec, vec,
                  pl.BlockSpec((LANES, LANES), lambda i: (0, 0))],
        out_specs=[wide, wide, wide], out_shape=[jax.ShapeDtypeStruct((S, HEADS * LANES), BF16)] * 3,
        compiler_params=_params("parallel"), name=name)(proj, fcum, qgain, kgain, gmat)


def _fox_prep_bwd(proj, dq_aug, dk_aug, dv_aug, dog, qgain, kgain, gmat, *, name, ts=256):
    S = proj.shape[0]
    D = HEADS * HEAD_DIM
    ts = _tile(S, ts)
    scale = HEAD_DIM ** -0.5

    def body(p_ref, dq_ref, dk_ref, dv_ref, dog_ref, qg_ref, kg_ref, gm_ref, o_ref, df_ref, dqg_ref, dkg_ref):
        _acc_init(pl.program_id(0), dqg_ref, dkg_ref)
        gm, lane = gm_ref[...], _lane()
        df = jnp.zeros((ts, LANES), F32)
        for cpair in range(HEAD_PAIRS):
            tiles = []
            for e in range(2):
                h = 2 * cpair + e
                cols = pl.ds(h * LANES, LANES)
                tq, tk = dq_ref[:, cols], dk_ref[:, cols]
                df = jnp.where(lane == h, _lane_col(tq, lane, L_F) - _lane_col(tk, lane, L_ONE), df)
                tiles.append((tq, tk, dv_ref[:, cols].astype(F32)))
            pair = [_pair_of_heads(tiles[0][k], tiles[1][k], lane) for k in range(3)]
            for half, g_ref, dg_ref, mult in ((0, qg_ref, dqg_ref, scale), (1, kg_ref, dkg_ref, 1.0)):
                v = p_ref[:, pl.ds(half * D + cpair * LANES, LANES)].astype(F32)
                r = lax.rsqrt(_head_mean(v * v, gm) + EPS)
                xh = v * r
                dn = pair[half] * mult
                dg_ref[...] += _colsum(dn * xh)
                dxh = dn * g_ref[...]
                o_ref[:, pl.ds(half * D + cpair * LANES, LANES)] = (r * (dxh - xh * _head_mean(dxh * xh, gm))).astype(BF16)
            o_ref[:, pl.ds(2 * D + cpair * LANES, LANES)] = pair[2].astype(BF16)
        o_ref[:, pl.ds(3 * D, D)] = dog_ref[...]
        o_ref[:, pl.ds(4 * D, LANES)] = jnp.zeros((ts, LANES), BF16)
        df_ref[...] = df

    row = pl.BlockSpec((ts, D), lambda i: (i, 0))
    wide = pl.BlockSpec((ts, HEADS * LANES), lambda i: (i, 0))
    vec = pl.BlockSpec((1, LANES), lambda i: (0, 0))
    return pl.pallas_call(
        body, grid=(S // ts,),
        in_specs=[pl.BlockSpec((ts, 2 * D), lambda i: (i, 0)), wide, wide, wide, row, vec, vec,
                  pl.BlockSpec((LANES, LANES), lambda i: (0, 0))],
        out_specs=[pl.BlockSpec((ts, 4 * D + LANES), lambda i: (i, 0)), pl.BlockSpec((ts, LANES), lambda i: (i, 0)), vec, vec],
        out_shape=[jax.ShapeDtypeStruct((S, 4 * D + LANES), BF16), jax.ShapeDtypeStruct((S, LANES), F32),
                   jax.ShapeDtypeStruct((1, LANES), F32), jax.ShapeDtypeStruct((1, LANES), F32)],
        compiler_params=_params("arbitrary"), name=name)(proj, dq_aug, dk_aug, dv_aug, dog, qgain, kgain, gmat)


def _log_sigmoid(z):
    return jnp.minimum(z, 0.0) - jnp.log(1.0 + jnp.exp(-jnp.abs(z)))


def _fox_decay_fwd(fl, bf, *, name, tb=256):
    S = fl.shape[0]
    tb = _tile(S, tb)

    def body(fl_ref, b_ref, o_ref, carry):
        @pl.when(pl.program_id(0) == 0)
        def _():
            carry[...] = jnp.zeros_like(carry)
        logf = _log_sigmoid(fl_ref[...] + b_ref[...])
        tri = (lax.broadcasted_iota(jnp.int32, (tb, tb), 1) <= lax.broadcasted_iota(jnp.int32, (tb, tb), 0)).astype(F32)
        cs = jnp.dot(tri, logf, preferred_element_type=F32, precision=lax.Precision.HIGHEST) + carry[...]
        o_ref[...] = cs
        carry[...] = _row_of(cs, tb - 1)

    return pl.pallas_call(
        body, grid=(S // tb,),
        in_specs=[pl.BlockSpec((tb, LANES), lambda i: (i, 0)), pl.BlockSpec((1, LANES), lambda i: (0, 0))],
        out_specs=pl.BlockSpec((tb, LANES), lambda i: (i, 0)),
        out_shape=jax.ShapeDtypeStruct((S, LANES), F32), scratch_shapes=[pltpu.VMEM((1, LANES), F32)],
        compiler_params=_params("arbitrary"), name=name)(fl, bf)


def _fox_decay_bwd(dF, fl, bf, dproj, *, name, tb=256):
    S = fl.shape[0]
    tb = _tile(S, tb)
    n = S // tb
    last_col = dproj.shape[1] // LANES - 1

    def body(df_ref, fl_ref, b_ref, dproj_hbm, o_ref, db_ref, carry):
        del dproj_hbm
        @pl.when(pl.program_id(0) == 0)
        def _():
            carry[...] = jnp.zeros_like(carry)
            db_ref[...] = jnp.zeros_like(db_ref)
        tri = (lax.broadcasted_iota(jnp.int32, (tb, tb), 1) >= lax.broadcasted_iota(jnp.int32, (tb, tb), 0)).astype(F32)
        rc = jnp.dot(tri, df_ref[...], preferred_element_type=F32, precision=lax.Precision.HIGHEST) + carry[...]
        carry[...] = _row_of(rc, 0)
        dfl = rc * jax.nn.sigmoid(-(fl_ref[...] + b_ref[...]))
        o_ref[...] = dfl.astype(BF16)
        db_ref[...] += _colsum(dfl)

    rev = pl.BlockSpec((tb, LANES), lambda i: (n - 1 - i, 0))
    vec = pl.BlockSpec((1, LANES), lambda i: (0, 0))
    return pl.pallas_call(
        body, grid=(n,), in_specs=[rev, rev, vec, pl.BlockSpec(memory_space=pl.ANY)],
        out_specs=[pl.BlockSpec((tb, LANES), lambda i: (n - 1 - i, last_col)), vec],
        out_shape=[jax.ShapeDtypeStruct(dproj.shape, BF16), jax.ShapeDtypeStruct((1, LANES), F32)],
        scratch_shapes=[pltpu.VMEM((1, LANES), F32)], input_output_aliases={3: 0},
        compiler_params=_params("arbitrary"), name=name)(dF, fl, bf, dproj)


_NT = (((1,), (1,)), ((), ()))
_TN = (((0,), (0,)), ((), ()))


def _causal(T, transposed=False):
    r, c = lax.broadcasted_iota(jnp.int32, (T, T), 0), lax.broadcasted_iota(jnp.int32, (T, T), 1)
    return r <= c if transposed else c <= r


def _with_shift(q_tile, shift, lane):
    keep = jnp.where((lane >= L_SHIFT) & (lane < L_SHIFT + 3), 0.0, q_tile)
    return (keep + _put3(lane, L_SHIFT, _split3(-shift))).astype(BF16)


def _ride_along(xchg, n_in, n_out, grid):
    if xchg is None:
        return (lambda body: body), [], [], [], [], []
    arrs, scatter = xchg
    n = len(arrs)

    def wrap(body):
        def wrapped(*refs):
            own_in, x_in = refs[:n_in], refs[n_in:n_in + n]
            own_out, x_out = refs[n_in + n:n_in + n + n_out], refs[n_in + n + n_out:n_in + 2 * n + n_out]
            rest = refs[n_in + 2 * n + n_out:]
            own_scratch, sems = rest[:len(rest) - 3], rest[len(rest) - 3:]
            ids = [pl.program_id(d) for d in range(len(grid))]
            first = functools.reduce(jnp.logical_and, [i == 0 for i in ids])
            last = functools.reduce(jnp.logical_and, [i == g - 1 for i, g in zip(ids, grid)])

            @pl.when(first)
            def _():
                for cp in _xchg_copies(x_in, x_out, scatter, *sems):
                    cp.start()

            body(*own_in, *own_out, *own_scratch)

            @pl.when(last)
            def _():
                for cp in _xchg_copies(x_in, x_out, scatter, *sems):
                    cp.wait()

        return wrapped

    return wrap, [_HBM] * n, [_HBM] * n, _xchg_out_shapes(arrs, scatter), _xchg_sems(n), list(arrs)


def _attn_rowmax(q_aug, k_aug, *, name, T=1024):
    S = q_aug.shape[0]
    T = _tile(S, T)
    n = S // T

    def body(q_ref, k_ref, o_ref, m_s):
        i, j = pl.program_id(1), pl.program_id(2)

        @pl.when(j == 0)
        def _():
            m_s[...] = jnp.full_like(m_s, NEG)

        def step(diag):
            s = lax.dot_general(q_ref[...], k_ref[...], _NT, preferred_element_type=F32)
            if diag:
                s = jnp.where(_causal(T), s, NEG)
            m = m_s[...]
            for cb in range(T // LANES):
                m = jnp.maximum(m, s[:, cb * LANES:(cb + 1) * LANES])
            m_s[...] = m

        @pl.when(j < i)
        def _():
            step(False)

        @pl.when(j == i)
        def _():
            step(True)
            o_ref[...] = _with_shift(q_ref[...].astype(F32), jnp.max(m_s[...], axis=-1, keepdims=True), _lane())

    qrow = pl.BlockSpec((T, LANES), lambda h, i, j: (i, h))
    return pl.pallas_call(
        body, grid=(HEADS, n, n),
        in_specs=[qrow, pl.BlockSpec((T, LANES), lambda h, i, j: (jnp.minimum(j, i), h))],
        out_specs=qrow, out_shape=jax.ShapeDtypeStruct(q_aug.shape, BF16),
        scratch_shapes=[pltpu.VMEM((T, LANES), F32)],
        compiler_params=_params("parallel", "parallel", "arbitrary"), name=name)(q_aug, k_aug)


def _attn_fwd(q_max, k_aug, v_aug, xchg=None, *, name, T=1024):
    S = q_max.shape[0]
    T = _tile(S, T)
    n = S // T
    wrap, x_in, x_out, x_shapes, x_sems, x_ops = _ride_along(xchg, 3, 2, (HEADS, n, n))

    def body(q_ref, k_ref, v_ref, o_ref, qb_ref, acc_s):
        i, j = pl.program_id(1), pl.program_id(2)

        @pl.when(j == 0)
        def _():
            acc_s[...] = jnp.zeros_like(acc_s)

        def block(rows, cols, mask):
            s = lax.dot_general(q_ref[rows, :], k_ref[cols, :], _NT, preferred_element_type=F32)
            if mask is not None:
                s = jnp.where(mask, s, NEG)
            return jnp.dot(jnp.exp(s).astype(BF16), v_ref[cols, :], preferred_element_type=F32)

        @pl.when(j < i)
        def _():
            chunk = T // KEY_CHUNKS
            upd = block(pl.ds(0, T), pl.ds(0, chunk), None)
            for c in range(1, KEY_CHUNKS):
                upd = upd + block(pl.ds(0, T), pl.ds(c * chunk, chunk), None)
            acc_s[...] += upd

        @pl.when(j == i)
        def _():
            half = T // 2
            lo, hi = pl.ds(0, half), pl.ds(half, half)
            acc_s[lo, :] += block(lo, lo, _causal(half))
            acc_s[hi, :] += block(hi, lo, None) + block(hi, hi, _causal(half))
            lane = _lane()
            acc = acc_s[...]
            l = _lane_col(acc, lane, L_F)
            o_ref[...] = acc / l
            qf = q_ref[...].astype(F32)
            row_max = -jnp.sum(jnp.where((lane >= L_SHIFT) & (lane < L_SHIFT + 3), qf, 0.0), axis=-1, keepdims=True)
            qb_ref[...] = _with_shift(qf, row_max + jnp.log(l), lane)

    qrow = pl.BlockSpec((T, LANES), lambda h, i, j: (i, h))
    kv = pl.BlockSpec((T, LANES), lambda h, i, j: (jnp.minimum(j, i), h))
    outs = pl.pallas_call(
        wrap(body), grid=(HEADS, n, n), in_specs=[qrow, kv, kv] + x_in, out_specs=[qrow, qrow] + x_out,
        out_shape=[jax.ShapeDtypeStruct(q_max.shape, F32), jax.ShapeDtypeStruct(q_max.shape, BF16)] + x_shapes,
        scratch_shapes=[pltpu.VMEM((T, LANES), F32)] + x_sems,
        compiler_params=_params("arbitrary", "arbitrary", "arbitrary"), name=name)(q_max, k_aug, v_aug, *x_ops)
    return outs[0], outs[1], outs[2:]


def _attn_bwd(q_lse, k_aug, v_aug, do_aug, xchg=None, *, name, T=1024):
    S = q_lse.shape[0]
    T = _tile(S, T)
    n = S // T
    wrap, x_in, x_out, x_shapes, x_sems, x_ops = _ride_along(xchg, 4, 3, (HEADS, n, n))

    def body(q_ref, do_ref, k_ref, v_ref, dq_ref, dk_ref, dv_ref, dq_s, dk_s, dv_s):
        j, i = pl.program_id(1), pl.program_id(2)

        def block(keys, queries, mask):
            q, do, k, v = q_ref[queries, :], do_ref[queries, :], k_ref[keys, :], v_ref[keys, :]
            st = lax.dot_general(k, q, _NT, preferred_element_type=F32)
            if mask is not None:
                st = jnp.where(mask, st, NEG)
            pt = jnp.exp(st)
            dst = (pt * lax.dot_general(v, do, _NT, preferred_element_type=F32)).astype(BF16)
            dv_s[keys, :] += jnp.dot(pt.astype(BF16), do, preferred_element_type=F32)
            dk_s[keys, :] += jnp.dot(dst, q, preferred_element_type=F32)
            return lax.dot_general(dst, k, _TN, preferred_element_type=F32)

        @pl.when(i == j)
        def _():
            dk_s[...] = jnp.zeros_like(dk_s)
            dv_s[...] = jnp.zeros_like(dv_s)

            @pl.when(j == 0)
            def _():
                dq_s[i] = jnp.zeros((T, LANES), F32)

            half = T // 2
            lo, hi = pl.ds(0, half), pl.ds(half, half)
            dq_s[i, lo, :] += block(lo, lo, _causal(half, transposed=True))
            dq_s[i, hi, :] += block(lo, hi, None) + block(hi, hi, _causal(half, transposed=True))
            dq_ref[...] = dq_s[j]

        @pl.when(i > j)
        def _():
            chunk = T // KEY_CHUNKS
            upd = block(pl.ds(0, chunk), pl.ds(0, T), None)
            for c in range(1, KEY_CHUNKS):
                upd = upd + block(pl.ds(c * chunk, chunk), pl.ds(0, T), None)

            @pl.when(j == 0)
            def _():
                dq_s[i] = upd

            @pl.when(j > 0)
            def _():
                dq_s[i] += upd

        @pl.when(i == n - 1)
        def _():
            dk_ref[...] = dk_s[...]
            dv_ref[...] = dv_s[...].astype(BF16)

    qrow = pl.BlockSpec((T, LANES), lambda h, j, i: (jnp.maximum(i, j), h))
    kv = pl.BlockSpec((T, LANES), lambda h, j, i: (j, h))
    outs = pl.pallas_call(
        wrap(body), grid=(HEADS, n, n), in_specs=[qrow, qrow, kv, kv] + x_in, out_specs=[kv, kv, kv] + x_out,
        out_shape=[jax.ShapeDtypeStruct(q_lse.shape, F32), jax.ShapeDtypeStruct(q_lse.shape, F32),
                   jax.ShapeDtypeStruct(q_lse.shape, BF16)] + x_shapes,
        scratch_shapes=[pltpu.VMEM((n, T, LANES), F32), pltpu.VMEM((T, LANES), F32), pltpu.VMEM((T, LANES), F32)] + x_sems,
        compiler_params=_params("arbitrary", "arbitrary", "arbitrary"), name=name)(q_lse, do_aug, k_aug, v_aug, *x_ops)
    return outs[0], outs[1], outs[2], outs[3:]


def _fox_gate_fwd(att_aug, proj, *, name, ts=256):
    S = att_aug.shape[0]
    D = HEADS * HEAD_DIM
    ts = _tile(S, ts)

    def body(a_ref, o_ref, att_ref, out_ref):
        lane = _lane()
        for cpair in range(HEAD_PAIRS):
            cols = pl.ds(cpair * LANES, LANES)
            pair = _pair_of_heads(a_ref[:, pl.ds(2 * cpair * LANES, LANES)], a_ref[:, pl.ds((2 * cpair + 1) * LANES, LANES)], lane)
            att_ref[:, cols] = pair
            out_ref[:, cols] = (pair * jax.nn.sigmoid(o_ref[:, cols].astype(F32))).astype(BF16)

    row = pl.BlockSpec((ts, D), lambda i: (i, 0))
    return pl.pallas_call(
        body, grid=(S // ts,),
        in_specs=[pl.BlockSpec((ts, HEADS * LANES), lambda i: (i, 0)), pl.BlockSpec((ts, D), lambda i: (i, 3))],
        out_specs=[row, row], out_shape=[jax.ShapeDtypeStruct((S, D), F32), jax.ShapeDtypeStruct((S, D), BF16)],
        compiler_params=_params("parallel"), name=name)(att_aug, proj)


def _fox_gate_bwd(da, att, proj, *, name, ts=256):
    S, D = att.shape
    ts = _tile(S, ts)

    def body(da_ref, a_ref, o_ref, do_ref, dog_ref):
        lane = _lane()
        for cpair in range(HEAD_PAIRS):
            cols = pl.ds(cpair * LANES, LANES)
            dav, av = da_ref[:, cols].astype(F32), a_ref[:, cols]
            sg = jax.nn.sigmoid(o_ref[:, cols].astype(F32))
            datt = (dav * sg).astype(BF16).astype(F32)
            dog_ref[:, cols] = (dav * av * sg * (1.0 - sg)).astype(BF16)
            prod = datt * av
            for e in range(2):
                in_head = (lane < HEAD_DIM) if e == 0 else (lane >= HEAD_DIM)
                delta = jnp.sum(jnp.where(in_head, prod, 0.0), axis=-1, keepdims=True)
                tile = _head_of_pair(datt, e, lane) + _put3(lane, L_F, _split3(-delta))
                do_ref[:, pl.ds((2 * cpair + e) * LANES, LANES)] = tile.astype(BF16)

    row = pl.BlockSpec((ts, D), lambda i: (i, 0))
    return pl.pallas_call(
        body, grid=(S // ts,), in_specs=[row, row, pl.BlockSpec((ts, D), lambda i: (i, 3))],
        out_specs=[pl.BlockSpec((ts, HEADS * LANES), lambda i: (i, 0)), row],
        out_shape=[jax.ShapeDtypeStruct((S, HEADS * LANES), BF16), jax.ShapeDtypeStruct((S, D), BF16)],
        compiler_params=_params("parallel"), name=name)(da, att, proj)


def _row_of(block, r):
    rows = lax.broadcasted_iota(jnp.int32, block.shape, 0)
    return jnp.sum(jnp.where(rows == r, block, 0.0), axis=0, keepdims=True)


def _shift_down(cur, tail, k):
    out = pltpu.roll(cur, k, 0)
    top = out[:SUBLANES]
    rows = lax.broadcasted_iota(jnp.int32, top.shape, 0)
    for r in range(k):
        top = jnp.where(rows == r, _row_of(tail, tail.shape[0] - k + r), top)
    return jnp.concatenate([top, out[SUBLANES:]], axis=0)


def _shift_up(cur, head, k):
    n = cur.shape[0]
    out = pltpu.roll(cur, n - k, 0)
    bottom = out[n - SUBLANES:]
    rows = lax.broadcasted_iota(jnp.int32, bottom.shape, 0)
    for r in range(k):
        bottom = jnp.where(rows == SUBLANES - k + r, _row_of(head, r), bottom)
    return jnp.concatenate([out[:n - SUBLANES], bottom], axis=0)


HALO = 16


CONV_TC = 1408


def _pair_tiles(v):
    nc = v.shape[-1] // (2 * CONV_TC)
    return jnp.swapaxes(v.reshape(v.shape[:-1] + (2, nc, CONV_TC)), -3, -2).reshape(v.shape)


def _unpair_tiles(v):
    nc = v.shape[-1] // (2 * CONV_TC)
    return jnp.swapaxes(v.reshape(v.shape[:-1] + (nc, 2, CONV_TC)), -3, -2).reshape(v.shape)


def _conv_rows(cur, tail, w_ref, b_ref, cols):
    a1, a2 = _shift_down(cur, tail, 1), _shift_down(cur, tail, 2)
    return a2 * w_ref[0:1, cols] + a1 * w_ref[1:2, cols] + cur * w_ref[2:3, cols] + b_ref[:, cols], (a2, a1, cur)


def _conv_gate_fwd(a, cw, cb, *, name, ts=512):
    S, F2 = a.shape
    tc = CONV_TC
    ts = _tile(S, ts)
    nc = F2 // (2 * tc)
    sub = ts // HALO
    halves = (pl.ds(0, tc), pl.ds(tc, tc))

    def body(a_ref, t_ref, w_ref, b_ref, o_ref):
        first = pl.program_id(1) == 0
        pre = []
        for cols in halves:
            tail = jnp.where(first, 0.0, t_ref[:, cols].astype(F32))
            pre.append(_conv_rows(a_ref[:, cols].astype(F32), tail, w_ref, b_ref, cols)[0])
        g, val = pre
        o_ref[...] = (g * jax.nn.sigmoid(g) * val).astype(BF16)

    return pl.pallas_call(
        body, grid=(nc, S // ts),
        in_specs=[pl.BlockSpec((ts, 2 * tc), lambda j, i: (i, j)),
                  pl.BlockSpec((HALO, 2 * tc), lambda j, i: (jnp.maximum(i * sub - 1, 0), j)),
                  pl.BlockSpec((CONV_WIDTH, 2 * tc), lambda j, i: (0, j)), pl.BlockSpec((1, 2 * tc), lambda j, i: (0, j))],
        out_specs=pl.BlockSpec((ts, tc), lambda j, i: (i, j)),
        out_shape=jax.ShapeDtypeStruct((S, F2 // 2), BF16),
        compiler_params=_params("parallel", "parallel"), name=name)(a, a, cw, cb)


def _conv_gate_bwd(a, dact, cw, cb, *, name, ts=512):
    S, F2 = a.shape
    tc = CONV_TC
    ts = _tile(S, ts)
    nc = F2 // (2 * tc)
    sub = ts // HALO
    n_rows = S // ts
    halves = (pl.ds(0, tc), pl.ds(tc, tc))

    def body(a_ref, at_ref, ah_ref, d_ref, dh_ref, w_ref, b_ref, da_ref, s_ref):
        i = pl.program_id(1)
        _acc_init(i, s_ref)

        def dpre_of(rows, tails, d):
            (g, taps_g), (val, taps_v) = [_conv_rows(rows[h], tails[h], w_ref, b_ref, halves[h]) for h in range(2)]
            sg = jax.nn.sigmoid(g)
            return (d * val * (sg * (1.0 + g * (1.0 - sg))), d * (g * sg)), (taps_g, taps_v)

        cur = [a_ref[:, c].astype(F32) for c in halves]
        tail = [jnp.where(i == 0, 0.0, at_ref[:, c].astype(F32)) for c in halves]
        dpre, taps = dpre_of(cur, tail, d_ref[...].astype(F32))
        head, _ = dpre_of([ah_ref[:, c].astype(F32) for c in halves], [x[ts - HALO:, :] for x in cur], dh_ref[...].astype(F32))
        for h, cols in enumerate(halves):
            dd = dpre[h]
            nxt = jnp.where(i == n_rows - 1, 0.0, head[h])
            da_ref[:, cols] = (dd * w_ref[2:3, cols] + _shift_up(dd, nxt, 1) * w_ref[1:2, cols]
                               + _shift_up(dd, nxt, 2) * w_ref[0:1, cols]).astype(BF16)
            for r in range(CONV_WIDTH):
                s_ref[r:r + 1, cols] += _colsum(dd * taps[h][r])
            s_ref[CONV_WIDTH:CONV_WIDTH + 1, cols] += _colsum(dd)

    nxt_rows = lambda i: jnp.minimum((i + 1) * sub, S // HALO - 1)
    return pl.pallas_call(
        body, grid=(nc, n_rows),
        in_specs=[pl.BlockSpec((ts, 2 * tc), lambda j, i: (i, j)),
                  pl.BlockSpec((HALO, 2 * tc), lambda j, i: (jnp.maximum(i * sub - 1, 0), j)),
                  pl.BlockSpec((HALO, 2 * tc), lambda j, i: (nxt_rows(i), j)),
                  pl.BlockSpec((ts, tc), lambda j, i: (i, j)), pl.BlockSpec((HALO, tc), lambda j, i: (nxt_rows(i), j)),
                  pl.BlockSpec((CONV_WIDTH, 2 * tc), lambda j, i: (0, j)), pl.BlockSpec((1, 2 * tc), lambda j, i: (0, j))],
        out_specs=[pl.BlockSpec((ts, 2 * tc), lambda j, i: (i, j)), pl.BlockSpec((8, 2 * tc), lambda j, i: (0, j))],
        out_shape=[jax.ShapeDtypeStruct((S, F2), BF16), jax.ShapeDtypeStruct((8, F2), F32)],
        compiler_params=_params("parallel", "arbitrary"), name=name)(a, a, a, dact, dact, cw, cb)


def _gelu_parts(z):
    z2 = z * z
    t = jnp.tanh(GELU_C0 * (z + GELU_C1 * z * z2))
    val = 0.5 * z * (1.0 + t)
    grad = 0.5 * (1.0 + t) + 0.5 * z * (1.0 - t * t) * GELU_C0 * (1.0 + 3.0 * GELU_C1 * z2)
    return val, grad


def _sgu_fwd(pre, b_in, vgain, vbias, wm, bsb, *, name, ts=256):
    S, W2 = pre.shape
    W = W2 // 2
    gd = W // SGU_GROUPS
    ts = _tile(S, ts)

    def body(p_ref, b_ref, vg_ref, vb_ref, wm_ref, bs_ref, y_ref):
        u = _gelu_parts(p_ref[:, pl.ds(0, W)].astype(F32) + b_ref[:, pl.ds(0, W)])[0]
        v = _gelu_parts(p_ref[:, pl.ds(W, W)].astype(F32) + b_ref[:, pl.ds(W, W)])[0]
        mu = jnp.mean(v, axis=-1, keepdims=True)
        vc = v - mu
        rstd = lax.rsqrt(jnp.mean(vc * vc, axis=-1, keepdims=True) + EPS)
        vn = ((vc * rstd) * vg_ref[...] + vb_ref[...]).astype(BF16)
        for blk in range(ts // SGU_BLOCK):
            r0 = blk * SGU_BLOCK
            for g in range(SGU_GROUPS):
                c0 = g * gd
                mixed = jnp.dot(wm_ref[g], vn[r0:r0 + SGU_BLOCK, c0:c0 + gd], preferred_element_type=F32) + bs_ref[g]
                y_ref[pl.ds(r0, SGU_BLOCK), pl.ds(c0, gd)] = (u[r0:r0 + SGU_BLOCK, c0:c0 + gd] * mixed).astype(BF16)

    full = lambda shape: pl.BlockSpec(shape, lambda i: (0,) * len(shape))
    return pl.pallas_call(
        body, grid=(S // ts,),
        in_specs=[pl.BlockSpec((ts, W2), lambda i: (i, 0)), full((1, W2)), full((1, W)), full((1, W)),
                  full((SGU_GROUPS, SGU_BLOCK, SGU_BLOCK)), full((SGU_GROUPS, SGU_BLOCK, gd))],
        out_specs=pl.BlockSpec((ts, W), lambda i: (i, 0)), out_shape=jax.ShapeDtypeStruct((S, W), BF16),
        compiler_params=_params("parallel"), name=name)(pre, b_in, vgain, vbias, wm, bsb)


def _sgu_bwd(pre, dy, b_in, vgain, vbias, wm, wmt, bsb, *, name, ts=256):
    S, W2 = pre.shape
    W = W2 // 2
    gd = W // SGU_GROUPS
    ts = _tile(S, ts)
    last = S // ts - 1

    def body(p_ref, dy_ref, b_ref, vg_ref, vb_ref, wm_ref, wmt_ref, bs_ref,
             dp_ref, db_ref, dvg_ref, dvb_ref, dws_ref, dbs_ref, du_s, dvn_s, dbs_s):
        step = pl.program_id(0)
        _acc_init(step, db_ref, dvg_ref, dvb_ref, dws_ref, dbs_s)
        u, gu = _gelu_parts(p_ref[:, pl.ds(0, W)].astype(F32) + b_ref[:, pl.ds(0, W)])
        v, gv = _gelu_parts(p_ref[:, pl.ds(W, W)].astype(F32) + b_ref[:, pl.ds(W, W)])
        mu = jnp.mean(v, axis=-1, keepdims=True)
        vc = v - mu
        rstd = lax.rsqrt(jnp.mean(vc * vc, axis=-1, keepdims=True) + EPS)
        vhat = vc * rstd
        vn = (vhat * vg_ref[...] + vb_ref[...]).astype(BF16)
        dyv = dy_ref[...].astype(F32)
        for blk in range(ts // SGU_BLOCK):
            r0 = blk * SGU_BLOCK
            for g in range(SGU_GROUPS):
                c0 = g * gd
                vn_g = vn[r0:r0 + SGU_BLOCK, c0:c0 + gd]
                dy_g = dyv[r0:r0 + SGU_BLOCK, c0:c0 + gd]
                mixed = jnp.dot(wm_ref[g], vn_g, preferred_element_type=F32) + bs_ref[g]
                dmix = dy_g * u[r0:r0 + SGU_BLOCK, c0:c0 + gd]
                dmix_b = dmix.astype(BF16)
                du_s[pl.ds(r0, SGU_BLOCK), pl.ds(c0, gd)] = dy_g * mixed
                dvn_s[pl.ds(r0, SGU_BLOCK), pl.ds(c0, gd)] = jnp.dot(wmt_ref[g], dmix_b, preferred_element_type=F32)
                dws_ref[g] += lax.dot_general(dmix_b, vn_g, _NT, preferred_element_type=F32)
                dbs_s[g] += dmix
        dvn = dvn_s[...]
        dvg_ref[...] += _colsum(dvn * vhat)
        dvb_ref[...] += _colsum(dvn)
        dvh = dvn * vg_ref[...]
        dv = rstd * (dvh - jnp.mean(dvh, axis=-1, keepdims=True) - vhat * jnp.mean(dvh * vhat, axis=-1, keepdims=True))
        dpu = du_s[...] * gu
        dpv = dv * gv
        dp_ref[:, pl.ds(0, W)] = dpu.astype(BF16)
        dp_ref[:, pl.ds(W, W)] = dpv.astype(BF16)
        db_ref[:, pl.ds(0, W)] += _colsum(dpu)
        db_ref[:, pl.ds(W, W)] += _colsum(dpv)

        @pl.when(step == last)
        def _():
            for g in range(SGU_GROUPS):
                dbs_ref[g] = jnp.broadcast_to(jnp.sum(dbs_s[g], axis=-1, keepdims=True), (SGU_BLOCK, SGU_BLOCK))

    full = lambda shape: pl.BlockSpec(shape, lambda i: (0,) * len(shape))
    gsq = (SGU_GROUPS, SGU_BLOCK, SGU_BLOCK)
    return pl.pallas_call(
        body, grid=(S // ts,),
        in_specs=[pl.BlockSpec((ts, W2), lambda i: (i, 0)), pl.BlockSpec((ts, W), lambda i: (i, 0)),
                  full((1, W2)), full((1, W)), full((1, W)), full(gsq), full(gsq), full((SGU_GROUPS, SGU_BLOCK, gd))],
        out_specs=[pl.BlockSpec((ts, W2), lambda i: (i, 0)), full((1, W2)), full((1, W)), full((1, W)), full(gsq), full(gsq)],
        out_shape=[jax.ShapeDtypeStruct((S, W2), BF16), jax.ShapeDtypeStruct((1, W2), F32),
                   jax.ShapeDtypeStruct((1, W), F32), jax.ShapeDtypeStruct((1, W), F32),
                   jax.ShapeDtypeStruct(gsq, F32), jax.ShapeDtypeStruct(gsq, F32)],
        scratch_shapes=[pltpu.VMEM((ts, W), F32), pltpu.VMEM((ts, W), F32), pltpu.VMEM((SGU_GROUPS, SGU_BLOCK, gd), F32)],
        compiler_params=_params("arbitrary"), name=name)(pre, dy, b_in, vgain, vbias, wm, wmt, bsb)


def _paired_to_natural(w_up):
    nc = w_up.shape[1] // (2 * CONV_TC)
    return lambda q: (q % 2) * nc + q // 2


def _ffn_fwd(x, mods, n2g, w_up, cw, cb, w_down, tag):
    sh, sc, gate = mods
    h = _norm_mod_fwd(x, n2g, sh, sc, name=f"{tag}_norm_fwd")
    a = _mm(h, w_up, out_dtype=BF16, tn=CONV_TC, b_n=_paired_to_natural(w_up), name=f"{tag}_up")
    act = _conv_gate_fwd(a, cw, cb, name=f"{tag}_conv_fwd")
    x_out, y = _mm(act, w_down, tk=1408, res=(x, gate), name=f"{tag}_down")
    return x_out, (x, h, a, act, y)


def _ffn_bwd(dy, saved, mods, n2g, w_up, cw, cb, w_down, dres, prev, tag):
    x, h, a, act, _ = saved
    sh, sc, gate = mods
    dact = _mm(dy, w_down, tb=True, out_dtype=BF16, tn=1408, name=f"{tag}_down_dx")
    dw_down = _mm(act, dy, ta=True, out_dtype=BF16, tm=1408, name=f"{tag}_down_dw")
    da, sums = _conv_gate_bwd(a, dact, cw, cb, name=f"{tag}_conv_bwd")
    dh = _mm(da, w_up, tb=True, tk=CONV_TC, b_k=_paired_to_natural(w_up), name=f"{tag}_up_dx")
    dw_up = _mm(h, da, ta=True, out_dtype=BF16, tn=CONV_TC, o_n=_paired_to_natural(w_up), name=f"{tag}_up_dw")
    outs = _norm_mod_bwd(dh, x, n2g, sc, dres, prev, name=f"{tag}_norm_bwd")
    sums = _unpair_tiles(sums)
    return outs, dict(w_up=dw_up, w_down=dw_down, conv_w=sums[0:CONV_WIDTH], conv_b=sums[CONV_WIDTH])


def _local_step(x, target, w, mods, late=None, early=None, last=None):
    S, D = x.shape
    lane = jnp.arange(LANES)
    gmat = jnp.where((lane[:, None] // HEAD_DIM) == (lane[None, :] // HEAD_DIM), 1.0 / HEAD_DIM, 0.0).astype(BF16)
    qg2 = jnp.tile(w["fox_q_gain"].reshape(1, HEAD_DIM), (1, 2))
    kg2 = jnp.tile(w["fox_k_gain"].reshape(1, HEAD_DIM), (1, 2))
    bf_pad = jnp.pad(w["fox_b_f"].reshape(1, HEADS), ((0, 0), (0, LANES - HEADS)))
    w_in_pad = jnp.pad(w["fox_w_in"], ((0, 0), (0, 4 * D + LANES - w["fox_w_in"].shape[1])))
    w_qkvo, w_f = w_in_pad[:, :4 * D], w_in_pad[:, 4 * D:]
    tpos = jnp.arange(SGU_BLOCK)
    smask = (tpos[None, :] // SGU_CHUNK) <= (tpos[:, None] // SGU_CHUNK)
    wm32 = jnp.where(smask[None], w["sgu_w_s"], 0.0)
    wm, wmt = wm32.astype(BF16), jnp.swapaxes(wm32, 1, 2).astype(BF16)
    gd = w["sgu_v_gain"].shape[-1] // SGU_GROUPS
    bsb = jnp.broadcast_to(w["sgu_b_s"][:, :, None], (SGU_GROUPS, SGU_BLOCK, gd))
    vec = lambda v: v.reshape(1, -1)

    sh1, sc1, g1 = mods[0][0:3]
    h0 = _norm_mod_fwd(x, vec(w["norm1_g"][0]), sh1, sc1, name="fox_norm_fwd")
    proj = _mm(h0, w_qkvo, out_dtype=BF16, name="fox_proj")
    fl = _mm(h0, w_f, name="fox_forget_proj")
    fcum = _fox_decay_fwd(fl, bf_pad, name="fox_decay")
    q_aug, k_aug, v_aug = _fox_prep_fwd(proj, fcum, qg2, kg2, gmat, name="fox_qk_norm")
    logit_bound = 8.0 * jnp.max(jnp.abs(w["fox_q_gain"])) * jnp.max(jnp.abs(w["fox_k_gain"]))
    q_max = lax.cond(logit_bound <= SHIFT_FREE_LOGIT_BOUND, lambda: q_aug,
                     lambda: _attn_rowmax(q_aug, k_aug, name="fox_attn_rowmax"))
    xchg = None if late is None else (late[0], [False] * len(late[0]))
    att_aug, q_lse, gathered = _attn_fwd(q_max, k_aug, v_aug, xchg, name="fox_attn_fwd")
    if late is not None:
        w = {**w, **late[1](gathered)}
    w = dict(w, ffn_conv_w=_pair_tiles(w["ffn_conv_w"]), ffn_conv_b=_pair_tiles(w["ffn_conv_b"]))
    att, ag = _fox_gate_fwd(att_aug, proj, name="fox_gate_fwd")
    x1, y_fox = _mm(ag, w["fox_w_out"], res=(x, g1), name="fox_out")
    x2, ffn0 = _ffn_fwd(x1, mods[0][3:6], vec(w["norm2_g"][0]), w["ffn_w_up"][0], w["ffn_conv_w"][0],
                        vec(w["ffn_conv_b"][0]), w["ffn_w_down"][0], "ffn0")

    sh1b, sc1b, g1b = mods[1][0:3]
    h1 = _norm_mod_fwd(x2, vec(w["norm1_g"][1]), sh1b, sc1b, name="sgu_norm_fwd")
    pre = _mm(h1, w["sgu_w_in"], out_dtype=BF16, name="sgu_in")
    b_in, vg, vb = vec(w["sgu_b_in"]), vec(w["sgu_v_gain"]), vec(w["sgu_v_bias"])
    ys = _sgu_fwd(pre, b_in, vg, vb, wm, bsb, name="sgu_core_fwd")
    x3, y_sgu = _mm(ys, w["sgu_w_out"], res=(x2, g1b), name="sgu_out")
    x4, ffn1 = _ffn_fwd(x3, mods[1][3:6], vec(w["norm2_g"][1]), w["ffn_w_up"][1], w["ffn_conv_w"][1],
                        vec(w["ffn_conv_b"][1]), w["ffn_w_down"][1], "ffn1")

    loss, d_final_g, dx4, dy_ffn1, dgate_ffn1 = _final_loss(x4, vec(w["final_g"]), target, ffn1[4], mods[1][5], name="final_loss")

    (dx3, dn2g_1, dsh2_1, dsc2_1, dy_sgu, dgate_sgu), g_ffn1 = _ffn_bwd(
        dy_ffn1, ffn1, mods[1][3:6], vec(w["norm2_g"][1]), w["ffn_w_up"][1], w["ffn_conv_w"][1], vec(w["ffn_conv_b"][1]),
        w["ffn_w_down"][1], dx4, (y_sgu, g1b), "ffn1")

    dys = _mm(dy_sgu, w["sgu_w_out"], tb=True, out_dtype=BF16, name="sgu_out_dx")
    dw_sgu_out = _mm(ys, dy_sgu, ta=True, out_dtype=BF16, name="sgu_out_dw")
    dpre, db_in, dvg, dvb, dws, dbs = _sgu_bwd(pre, dys, b_in, vg, vb, wm, wmt, bsb, name="sgu_core_bwd")
    dh1 = _mm(dpre, w["sgu_w_in"], tb=True, name="sgu_in_dx")
    dw_sgu_in = _mm(h1, dpre, ta=True, out_dtype=BF16, name="sgu_in_dw")
    dx2, dn1g_1, dsh1_1, dsc1_1, dy_ffn0, dgate_ffn0 = _norm_mod_bwd(
        dh1, x2, vec(w["norm1_g"][1]), sc1b, dx3, (ffn0[4], mods[0][5]), name="sgu_norm_bwd")

    (dx1, dn2g_0, dsh2_0, dsc2_0, dy_fox, dgate_fox), g_ffn0 = _ffn_bwd(
        dy_ffn0, ffn0, mods[0][3:6], vec(w["norm2_g"][0]), w["ffn_w_up"][0], w["ffn_conv_w"][0], vec(w["ffn_conv_b"][0]),
        w["ffn_w_down"][0], dx2, (y_fox, g1), "ffn0")

    dag = _mm(dy_fox, w["fox_w_out"], tb=True, out_dtype=BF16, name="fox_out_dx")
    dw_fox_out = _mm(ag, dy_fox, ta=True, out_dtype=BF16, name="fox_out_dw")
    do_aug, dog = _fox_gate_bwd(dag, att, proj, name="fox_gate_bwd")
    grads = dict(
        sgu_w_in=dw_sgu_in, sgu_b_in=db_in[0], sgu_v_gain=dvg[0], sgu_v_bias=dvb[0],
        sgu_w_s=jnp.where(smask[None], dws, 0.0), sgu_b_s=dbs[:, :, 0], sgu_w_out=dw_sgu_out,
        ffn_w_up=jnp.stack([g_ffn0["w_up"], g_ffn1["w_up"]]),
        ffn_conv_w=jnp.stack([g_ffn0["conv_w"], g_ffn1["conv_w"]]),
        ffn_conv_b=jnp.stack([g_ffn0["conv_b"], g_ffn1["conv_b"]]),
        ffn_w_down=jnp.stack([g_ffn0["w_down"], g_ffn1["w_down"]]),
        final_g=d_final_g[0], fox_w_out=dw_fox_out, norm2_g=jnp.concatenate([dn2g_0, dn2g_1], axis=0),
    )
    xchg = None if early is None else early(grads)
    dq_aug, dk_aug, dv_aug, exchanged = _attn_bwd(q_lse, k_aug, v_aug, do_aug, xchg, name="fox_attn_bwd")
    dproj, dF, dqg, dkg = _fox_prep_bwd(proj, dq_aug, dk_aug, dv_aug, dog, qg2, kg2, gmat, name="fox_qk_norm_bwd")
    dproj, dbf = _fox_decay_bwd(dF, fl, bf_pad, dproj, name="fox_decay_bwd")
    dw_fox_in = _mm(h0, dproj, ta=True, out_dtype=BF16, tn=1408, name="fox_proj_dw")[:, :w["fox_w_in"].shape[1]]
    xchg = None if last is None else last(dict(fox_w_in=dw_fox_in))
    dh0 = _mm(dproj, w_in_pad, tb=True, tk=1408, xchg=xchg, name="fox_proj_dx")
    dh0, exchanged_last = dh0 if last is not None else (dh0, [])
    dx0, dn1g_0, dsh1_0, dsc1_0 = _norm_mod_bwd(dh0, x, vec(w["norm1_g"][0]), sc1, dx1, None, name="fox_norm_bwd")

    dmod0 = jnp.concatenate([dsh1_0, dsc1_0, dgate_fox, dsh2_0, dsc2_0, dgate_ffn0], axis=1)
    dmod1 = jnp.concatenate([dsh1_1, dsc1_1, dgate_sgu, dsh2_1, dsc2_1, dgate_ffn1], axis=1)
    grads.update(
        fox_w_in=dw_fox_in,
        fox_b_f=dbf[0, :HEADS],
        fox_q_gain=dqg[0, :HEAD_DIM] + dqg[0, HEAD_DIM:],
        fox_k_gain=dkg[0, :HEAD_DIM] + dkg[0, HEAD_DIM:],
        fox_w_out=dw_fox_out,
        ada_b=jnp.concatenate([dmod0, dmod1], axis=0),
        norm1_g=jnp.concatenate([dn1g_0, dn1g_1], axis=0),
    )
    return loss[0, 0], dx0, grads, exchanged, exchanged_last


_HBM = pl.BlockSpec(memory_space=pl.ANY)
N_PEER = N_DEV - 1


def _xchg_out_shapes(arrs, scatter):
    return [jax.ShapeDtypeStruct(a.shape if s else (N_DEV,) + a.shape, a.dtype) for a, s in zip(arrs, scatter)]


def _xchg_sems(n):
    return [pltpu.SemaphoreType.DMA((n * N_PEER,)), pltpu.SemaphoreType.DMA((n * N_PEER,)), pltpu.SemaphoreType.DMA((n,))]


def _xchg_copies(ins, outs, scatter, send, recv, loc):
    x, y, c = lax.axis_index("x"), lax.axis_index("y"), lax.axis_index("c")
    me = 4 * x + 2 * y + c
    copies = []
    for a in range(len(ins)):
        copies.append(pltpu.make_async_copy(ins[a].at[me] if scatter[a] else ins[a], outs[a].at[me], loc.at[a]))
        for k in range(1, N_DEV):
            px = 1 - x if k & 4 else x
            py = 1 - y if k & 2 else y
            pc = 1 - c if k & 1 else c
            copies.append(pltpu.make_async_remote_copy(
                src_ref=ins[a].at[4 * px + 2 * py + pc] if scatter[a] else ins[a], dst_ref=outs[a].at[me],
                send_sem=send.at[a * N_PEER + k - 1], recv_sem=recv.at[a * N_PEER + k - 1],
                device_id=(px, py, pc), device_id_type=MESH))
    return copies


def _exchange(arrs, scatter, *, name):
    n = len(arrs)

    def body(*refs):
        copies = _xchg_copies(refs[:n], refs[n:2 * n], scatter, *refs[2 * n:])
        for cp in copies:
            cp.start()
        for cp in copies:
            cp.wait()

    return pl.pallas_call(
        body, in_specs=[_HBM] * n, out_specs=[_HBM] * n, out_shape=_xchg_out_shapes(arrs, scatter),
        scratch_shapes=_xchg_sems(n),
        compiler_params=pltpu.CompilerParams(has_side_effects=True), name=name)(*arrs)


def _adamw(w, parts, m, v, *, name, tr=256):
    L, R, C = w.shape
    P = parts.shape[0]
    tr = next(t for t in range(min(R, tr), 0, -1) if R % t == 0 and (t % 16 == 0 or t == R))
    nr = R // tr
    c1 = 1.0 - ADAM_B1 ** ADAM_STEP
    c2 = 1.0 - ADAM_B2 ** ADAM_STEP

    def body(w_ref, p_ref, m_ref, v_ref, g_ref, d_ref, mo_ref, vo_ref):
        g = p_ref[0].astype(F32)
        for p in range(1, P):
            g = g + p_ref[p].astype(F32)
        mn = ADAM_B1 * m_ref[0] + (1.0 - ADAM_B1) * g
        vn = ADAM_B2 * v_ref[0] + (1.0 - ADAM_B2) * (g * g)
        g_ref[0] = g
        mo_ref[0] = mn
        vo_ref[0] = vn
        d_ref[0] = -ADAM_LR * ((mn / c1) / (jnp.sqrt(vn / c2) + ADAM_EPS) + ADAM_WD * w_ref[0])

    row = pl.BlockSpec((1, tr, C), lambda l, i: (l, i, 0))
    return pl.pallas_call(
        body, grid=(L, nr), in_specs=[row, pl.BlockSpec((P, tr, C), lambda l, i: (0, l * nr + i, 0)), row, row],
        out_specs=[row] * 4, out_shape=[jax.ShapeDtypeStruct((L, R, C), F32)] * 4,
        compiler_params=_params("parallel", "parallel"), name=name)(w, parts, m, v)


def _sum_parts(parts, *, name):
    P, R, C = parts.shape

    def body(p_ref, o_ref):
        g = p_ref[0]
        for p in range(1, P):
            g = g + p_ref[p]
        o_ref[...] = g

    return pl.pallas_call(body, out_shape=jax.ShapeDtypeStruct((R, C), F32), name=name)(parts)


WEIGHTS = ["fox_w_in", "fox_b_f", "fox_q_gain", "fox_k_gain", "fox_w_out", "sgu_w_in", "sgu_b_in", "sgu_v_gain",
           "sgu_v_bias", "sgu_w_s", "sgu_b_s", "sgu_w_out", "ffn_w_up", "ffn_conv_w", "ffn_conv_b", "ffn_w_down",
           "ada_w", "ada_b", "norm1_g", "norm2_g", "final_g"]
BIG_AXIS = dict(fox_w_in=1, fox_w_out=0, sgu_w_in=1, sgu_w_out=0, ffn_w_up=1, ffn_w_down=0, ada_w=1)
SMALL_SHARDED = ["sgu_b_in", "sgu_v_gain", "sgu_v_bias", "ffn_conv_w"]
SINGLE_LAYER = ("fox_", "sgu_")
BEFORE_ATTENTION = ["fox_w_in"]
AFTER_ATTENTION = ["fox_w_out", "sgu_w_in", "sgu_w_out", "ffn_w_up", "ffn_w_down"]
SMALL_EARLY = ["sgu_b_in", "sgu_v_gain", "sgu_v_bias", "sgu_w_s", "sgu_b_s", "ffn_conv_w", "ffn_conv_b", "norm2_g", "final_g"]


def _assemble(stacked, layers, axis):
    _, lr, cc = stacked.shape
    r = lr // layers
    s4 = stacked.reshape(N_DEV, layers, r, cc)
    if axis == 0:
        return s4.transpose(1, 0, 2, 3).reshape(layers, N_DEV * r, cc)
    return s4.transpose(1, 2, 0, 3).reshape(layers, r, N_DEV * cc)


def _disassemble(full, axis):
    layers, R, C = full.shape
    if axis == 0:
        r = R // N_DEV
        return full.reshape(layers, N_DEV, r, C).transpose(1, 0, 2, 3).reshape(N_DEV, layers * r, C)
    cc = C // N_DEV
    return full.reshape(layers, R, N_DEV, cc).transpose(2, 0, 1, 3).reshape(N_DEV, layers * R, cc)


def kernel(x, c, fox_w_in, fox_b_f, fox_q_gain, fox_k_gain, fox_w_out, sgu_w_in, sgu_b_in, sgu_v_gain, sgu_v_bias, sgu_w_s, sgu_b_s, sgu_w_out, ffn_w_up, ffn_conv_w, ffn_conv_b, ffn_w_down, ada_w, ada_b, norm1_g, norm2_g, final_g, loss_target, m_fox_w_in, m_fox_b_f, m_fox_q_gain, m_fox_k_gain, m_fox_w_out, m_sgu_w_in, m_sgu_b_in, m_sgu_v_gain, m_sgu_v_bias, m_sgu_w_s, m_sgu_b_s, m_sgu_w_out, m_ffn_w_up, m_ffn_conv_w, m_ffn_conv_b, m_ffn_w_down, m_ada_w, m_ada_b, m_norm1_g, m_norm2_g, m_final_g, v_fox_w_in, v_fox_b_f, v_fox_q_gain, v_fox_k_gain, v_fox_w_out, v_sgu_w_in, v_sgu_b_in, v_sgu_v_gain, v_sgu_v_bias, v_sgu_w_s, v_sgu_b_s, v_sgu_w_out, v_ffn_w_up, v_ffn_conv_w, v_ffn_conv_b, v_ffn_w_down, v_ada_w, v_ada_b, v_norm1_g, v_norm2_g, v_final_g):
    args = dict(locals())
    wts = {n: args[n] for n in WEIGHTS}
    ms = {n: args["m_" + n] for n in WEIGHTS}
    vs = {n: args["v_" + n] for n in WEIGHTS}
    me = 4 * lax.axis_index("x") + 2 * lax.axis_index("y") + lax.axis_index("c")

    shard2d = lambda n: wts[n].astype(BF16).reshape(-1, wts[n].shape[-1])

    def assemble_big(names, got):
        out = {}
        for n, g in zip(names, got):
            f = _assemble(g, wts[n].shape[0], BIG_AXIS[n])
            out[n] = f[0] if n.startswith(SINGLE_LAYER) else f
        return out

    def blocks_of(names, grads):
        return [_disassemble(grads[n] if grads[n].ndim == 3 else grads[n][None], BIG_AXIS[n]) for n in names]

    send = [c] + [shard2d(n) for n in BEFORE_ATTENTION] + [wts[n].reshape(-1, wts[n].shape[-1]) for n in SMALL_SHARDED]
    got = _exchange(send, [False] * len(send), name="gather_first")
    c_all = got[0].reshape(N_DEV, -1)
    full = assemble_big(BEFORE_ATTENTION, got[1:1 + len(BEFORE_ATTENTION)])
    for n, g in zip(SMALL_SHARDED, got[1 + len(BEFORE_ATTENTION):]):
        lead = wts[n].shape[:-1]
        f = jnp.moveaxis(g.reshape((N_DEV,) + wts[n].shape), 0, -2).reshape(lead + (-1,))
        full[n] = f[0] if n.startswith(SINGLE_LAYER) else f
    for n in WEIGHTS:
        if n not in full and n not in BIG_AXIS:
            full[n] = wts[n][0] if n.startswith(SINGLE_LAYER) else wts[n]

    ada_cols = wts["ada_w"].shape[-1]
    mod_rows = []
    for i in range(2):
        b_mine = lax.dynamic_slice_in_dim(wts["ada_b"][i], me * ada_cols, ada_cols).reshape(1, ada_cols)
        m, c_act = _ada_mod(c_all, wts["ada_w"][i].astype(BF16), b_mine, name=f"ada_mod_{i}")
        mod_rows.append(m)
    got = _exchange([jnp.concatenate(mod_rows, axis=1)[:, None, :]], [True], name="exchange_mods")[0]
    d_model = x.shape[-1]
    mods = []
    for i in range(2):
        mod = got[:, 0, i * ada_cols:(i + 1) * ada_cols].reshape(1, N_DEV * ada_cols)
        mods.append([mod[:, k * d_model:(k + 1) * d_model] for k in range(6)])

    small = [n for n in WEIGHTS if n not in BIG_AXIS]
    small_late = [n for n in small if n not in SMALL_EARLY]

    def pack_flat(arrays):
        flat = jnp.concatenate([a.reshape(-1).astype(F32) for a in arrays])
        rows = -(-flat.shape[0] // (8 * LANES)) * 8
        return jnp.pad(flat, (0, rows * LANES - flat.shape[0])).reshape(rows, LANES)

    late = ([shard2d(n) for n in AFTER_ATTENTION], lambda g: assemble_big(AFTER_ATTENTION, g))
    loss, grad_x, grads, got_late, got_last = _local_step(
        x[0], loss_target[0], full, mods, late,
        lambda gr: (blocks_of(AFTER_ATTENTION, gr) + [pack_flat([gr[n] for n in SMALL_EARLY])],
                    [True] * len(AFTER_ATTENTION) + [False]),
        lambda gr: (blocks_of(BEFORE_ATTENTION, gr), [True] * len(BEFORE_ATTENTION)))

    flat_late_all = _exchange([pack_flat([loss] + [grads[n] for n in small_late])], [False], name="gather_small_grads")[0]
    total_late = _sum_parts(flat_late_all, name="sum_small_grads_late").reshape(-1)
    total_early = _sum_parts(got_late[len(AFTER_ATTENTION)], name="sum_small_grads_early").reshape(-1)
    loss_out = total_late[0]
    summed, offs = {}, {}
    for vec, names, off in ((total_early, SMALL_EARLY, 0), (total_late, small_late, 1)):
        for n in names:
            size = math.prod(grads[n].shape)
            summed[n], offs[n] = vec[off:off + size].reshape(grads[n].shape), off
            off += size

    off_ada = offs["ada_b"]
    dmod_all = flat_late_all.reshape(N_DEV, -1)[:, off_ada:off_ada + 2 * N_DEV * ada_cols].reshape(N_DEV, 2, N_DEV * ada_cols)
    dmod_mine = lax.dynamic_slice_in_dim(dmod_all, me * ada_cols, ada_cols, axis=2)
    d_ada = [_mm(c_act, jnp.pad(dmod_mine[:, i], ((0, c_act.shape[0] - N_DEV), (0, 0))).astype(BF16), ta=True,
                 name=f"ada_dw_{i}") for i in range(2)]

    out_g, out_d, out_m, out_v = {}, {}, {}, {}
    summands = dict(zip(BEFORE_ATTENTION, got_last))
    summands.update(zip(AFTER_ATTENTION, got_late))
    summands["ada_w"] = jnp.concatenate(d_ada, axis=0)[None]
    for n, p in summands.items():
        out_g[n], out_d[n], out_m[n], out_v[n] = _adamw(wts[n], p, ms[n], vs[n], name=f"adamw_{n}")
    small_g = {}
    for n in small:
        g = summed[n]
        if n in SMALL_SHARDED:
            blk = g.shape[-1] // N_DEV
            g = lax.dynamic_slice_in_dim(g, me * blk, blk, axis=g.ndim - 1)
        small_g[n] = g.reshape(wts[n].shape)
    cat = lambda d: jnp.concatenate([d[n].reshape(-1) for n in small])
    n_small = sum(math.prod(wts[n].shape) for n in small)
    rows2 = -(-n_small // (256 * LANES)) * 256
    pack = lambda d, fill: jnp.pad(cat(d), (0, rows2 * LANES - n_small), constant_values=fill).reshape(1, rows2, LANES)
    g, d, mn, vn = _adamw(pack(wts, 0.0), pack(small_g, 0.0), pack(ms, 0.0), pack(vs, 1.0), name="adamw_small")
    off = 0
    for n in small:
        size = math.prod(wts[n].shape)
        for src, dst in ((g, out_g), (d, out_d), (mn, out_m), (vn, out_v)):
            dst[n] = src.reshape(-1)[off:off + size].reshape(wts[n].shape)
        off += size

    return (loss_out, grad_x[None], *[out_g[n] for n in WEIGHTS], *[out_d[n] for n in WEIGHTS],
            *[out_m[n] for n in WEIGHTS], *[out_v[n] for n in WEIGHTS])
```

```python
import functools
import math

import jax
import jax.numpy as jnp
from jax import lax
from jax.experimental import pallas as pl
from jax.experimental.pallas import tpu as pltpu

F32, BF16 = jnp.float32, jnp.bfloat16
N_DEV = 8
HEADS, HEAD_DIM = 16, 64
HEAD_PAIRS = HEADS // 2
LANES = 128
SUBLANES = 8
EPS = 1e-6
SGU_BLOCK, SGU_GROUPS, SGU_CHUNK = 128, 8, 64
CONV_WIDTH = 3
ADAM_LR, ADAM_B1, ADAM_B2, ADAM_EPS, ADAM_WD, ADAM_STEP = 0.001, 0.9, 0.999, 1e-08, 0.01, 10
NEG = -1e30
GELU_C0, GELU_C1 = math.sqrt(2.0 / math.pi), 0.044715
MESH = pl.DeviceIdType.MESH
VMEM_LIMIT = 56 * 1024 * 1024


def _tile(dim, pref):
    if dim <= pref:
        return dim
    t = (pref // LANES) * LANES
    while t >= LANES:
        if dim % t == 0:
            return t
        t -= LANES
    return dim


def _params(*sem):
    return pltpu.CompilerParams(dimension_semantics=sem, vmem_limit_bytes=VMEM_LIMIT)


def _mm(a, b, *, name, ta=False, tb=False, out_dtype=F32, tm=1024, tn=1024, tk=1024, res=None, b_n=None, b_k=None, o_n=None,
        xchg=None):
    M = a.shape[1] if ta else a.shape[0]
    K = a.shape[0] if ta else a.shape[1]
    N = b.shape[0] if tb else b.shape[1]
    tm, tn, tk = _tile(M, tm), _tile(N, tn), _tile(K, tk)
    nk = K // tk
    dims = (((0 if ta else 1,), (1 if tb else 0,)), ((), ()))
    same = lambda idx: idx
    b_n, b_k, o_n = b_n or same, b_k or same, o_n or same
    a_spec = pl.BlockSpec((tk, tm), lambda i, j, k: (k, i)) if ta else pl.BlockSpec((tm, tk), lambda i, j, k: (i, k))
    b_spec = (pl.BlockSpec((tn, tk), lambda i, j, k: (b_n(j), b_k(k))) if tb
              else pl.BlockSpec((tk, tn), lambda i, j, k: (b_k(k), b_n(j))))
    o_spec = pl.BlockSpec((tm, tn), lambda i, j, k: (i, o_n(j)))

    def accumulate(a_ref, b_ref, acc):
        @pl.when(pl.program_id(2) == 0)
        def _():
            acc[...] = jnp.zeros_like(acc)
        acc[...] += lax.dot_general(a_ref[...], b_ref[...], dims, preferred_element_type=F32)

    if res is None:
        def body(a_ref, b_ref, o_ref, acc):
            accumulate(a_ref, b_ref, acc)

            @pl.when(pl.program_id(2) == nk - 1)
            def _():
                o_ref[...] = acc[...].astype(o_ref.dtype)

        if xchg is None:
            return pl.pallas_call(
                body, grid=(M // tm, N // tn, nk), in_specs=[a_spec, b_spec], out_specs=o_spec,
                out_shape=jax.ShapeDtypeStruct((M, N), out_dtype), scratch_shapes=[pltpu.VMEM((tm, tn), F32)],
                compiler_params=_params("parallel", "parallel", "arbitrary"), name=name)(a, b)
        grid = (M // tm, N // tn, nk)
        wrap, x_in, x_out, x_shapes, x_sems, x_ops = _ride_along(xchg, 2, 1, grid)
        outs = pl.pallas_call(
            wrap(body), grid=grid, in_specs=[a_spec, b_spec] + x_in, out_specs=[o_spec] + x_out,
            out_shape=[jax.ShapeDtypeStruct((M, N), out_dtype)] + x_shapes,
            scratch_shapes=[pltpu.VMEM((tm, tn), F32)] + x_sems,
            compiler_params=_params("arbitrary", "arbitrary", "arbitrary"), name=name)(a, b, *x_ops)
        return outs[0], outs[1:]

    x, gate = res

    def body_res(a_ref, b_ref, x_ref, g_ref, o_ref, y_ref, acc):
        accumulate(a_ref, b_ref, acc)

        @pl.when(pl.program_id(2) == nk - 1)
        def _():
            y = acc[...]
            o_ref[...] = x_ref[...] + g_ref[...] * y
            y_ref[...] = y.astype(BF16)

    return pl.pallas_call(
        body_res, grid=(M // tm, N // tn, nk),
        in_specs=[a_spec, b_spec, o_spec, pl.BlockSpec((1, tn), lambda i, j, k: (0, j))],
        out_specs=[o_spec, o_spec],
        out_shape=[jax.ShapeDtypeStruct((M, N), F32), jax.ShapeDtypeStruct((M, N), BF16)],
        scratch_shapes=[pltpu.VMEM((tm, tn), F32)],
        compiler_params=_params("parallel", "parallel", "arbitrary"), name=name)(a, b, x, gate)


def _ada_mod(c_rows, w, b, *, name):
    R, D = c_rows.shape
    N = w.shape[1]
    tn = _tile(N, 1536)
    rows = 16
    c_pad = jnp.pad(c_rows, ((0, rows - R), (0, 0)))

    def body(c_ref, w_ref, b_ref, o_ref, ca_ref):
        cv = c_ref[...]
        ca16 = (cv * jax.nn.sigmoid(cv)).astype(BF16)
        ca_ref[...] = ca16
        o_ref[...] = jnp.dot(ca16, w_ref[...], preferred_element_type=F32) + b_ref[...]

    out, ca = pl.pallas_call(
        body, grid=(N // tn,),
        in_specs=[pl.BlockSpec((rows, D), lambda j: (0, 0)), pl.BlockSpec((D, tn), lambda j: (0, j)),
                  pl.BlockSpec((1, tn), lambda j: (0, j))],
        out_specs=[pl.BlockSpec((rows, tn), lambda j: (0, j)), pl.BlockSpec((rows, D), lambda j: (0, 0))],
        out_shape=[jax.ShapeDtypeStruct((rows, N), F32), jax.ShapeDtypeStruct((rows, D), BF16)],
        compiler_params=_params("arbitrary"), name=name)(c_pad, w, b)
    return out[0:R], ca


def _norm_mod_fwd(x, g, shift, scale, *, name, ts=512):
    S, D = x.shape
    ts = _tile(S, ts)
    row = pl.BlockSpec((ts, D), lambda i: (i, 0))
    vec = pl.BlockSpec((1, D), lambda i: (0, 0))

    def body(x_ref, g_ref, sh_ref, sc_ref, h_ref):
        xv = x_ref[...]
        r = lax.rsqrt(jnp.mean(xv * xv, axis=-1, keepdims=True) + EPS)
        h_ref[...] = ((xv * r * g_ref[...]) * (1.0 + sc_ref[...]) + sh_ref[...]).astype(BF16)

    return pl.pallas_call(body, grid=(S // ts,), in_specs=[row, vec, vec, vec], out_specs=row,
                          out_shape=jax.ShapeDtypeStruct((S, D), BF16),
                          compiler_params=_params("parallel"), name=name)(x, g, shift, scale)


def _acc_init(step, *refs):
    @pl.when(step == 0)
    def _():
        for r in refs:
            r[...] = jnp.zeros_like(r)


def _colsum(v):
    return jnp.sum(v, axis=0, keepdims=True)


def _norm_mod_bwd(dh, x, g, scale, dres, prev=None, *, name, ts=512):
    S, D = x.shape
    ts = _tile(S, ts)
    row = pl.BlockSpec((ts, D), lambda i: (i, 0))
    vec = pl.BlockSpec((1, D), lambda i: (0, 0))
    has_prev = prev is not None

    def body(*refs):
        if has_prev:
            dh_ref, x_ref, g_ref, sc_ref, dres_ref, y_ref, gate_ref, dx_ref, dg_ref, dsh_ref, dsc_ref, dy_ref, dgate_ref = refs
            _acc_init(pl.program_id(0), dg_ref, dsh_ref, dsc_ref, dgate_ref)
        else:
            dh_ref, x_ref, g_ref, sc_ref, dres_ref, dx_ref, dg_ref, dsh_ref, dsc_ref = refs
            _acc_init(pl.program_id(0), dg_ref, dsh_ref, dsc_ref)
        xv, dhv, gv = x_ref[...], dh_ref[...], g_ref[...]
        r = lax.rsqrt(jnp.mean(xv * xv, axis=-1, keepdims=True) + EPS)
        xh = xv * r
        dsh_ref[...] += _colsum(dhv)
        dsc_ref[...] += _colsum(dhv * (xh * gv))
        dn = dhv * (1.0 + sc_ref[...])
        dg_ref[...] += _colsum(dn * xh)
        dxh = dn * gv
        dx = dres_ref[...] + r * (dxh - xh * jnp.mean(dxh * xh, axis=-1, keepdims=True))
        dx_ref[...] = dx
        if has_prev:
            dy_ref[...] = (gate_ref[...] * dx).astype(BF16)
            dgate_ref[...] += _colsum(dx * y_ref[...].astype(F32))

    ins, in_specs = [dh, x, g, scale, dres], [row, row, vec, vec, row]
    outs = [jax.ShapeDtypeStruct((S, D), F32)] + [jax.ShapeDtypeStruct((1, D), F32)] * 3
    out_specs = [row, vec, vec, vec]
    if has_prev:
        ins += list(prev)
        in_specs += [row, vec]
        outs += [jax.ShapeDtypeStruct((S, D), BF16), jax.ShapeDtypeStruct((1, D), F32)]
        out_specs += [row, vec]
    return pl.pallas_call(body, grid=(S // ts,), in_specs=in_specs, out_specs=out_specs, out_shape=outs,
                          compiler_params=_params("arbitrary"), name=name)(*ins)


def _final_loss(x, g, target, y, gate, *, name, ts=512):
    S, D = x.shape
    ts = _tile(S, ts)
    row = pl.BlockSpec((ts, D), lambda i: (i, 0))
    vec = pl.BlockSpec((1, D), lambda i: (0, 0))
    lvec = pl.BlockSpec((1, LANES), lambda i: (0, 0))

    def body(x_ref, g_ref, t_ref, y_ref, gate_ref, loss_ref, dg_ref, dx_ref, dy_ref, dgate_ref):
        _acc_init(pl.program_id(0), loss_ref, dg_ref, dgate_ref)
        xv, gv = x_ref[...], g_ref[...]
        r = lax.rsqrt(jnp.mean(xv * xv, axis=-1, keepdims=True) + EPS)
        xh = xv * r
        e = xh * gv - t_ref[...]
        loss_ref[...] += 0.5 * jnp.sum(jnp.mean(e * e, axis=-1, keepdims=True), axis=0, keepdims=True)
        dout = e * (1.0 / D)
        dg_ref[...] += _colsum(dout * xh)
        dxh = dout * gv
        dx = r * (dxh - xh * jnp.mean(dxh * xh, axis=-1, keepdims=True))
        dx_ref[...] = dx
        dy_ref[...] = (gate_ref[...] * dx).astype(BF16)
        dgate_ref[...] += _colsum(dx * y_ref[...].astype(F32))

    return pl.pallas_call(
        body, grid=(S // ts,), in_specs=[row, vec, row, row, vec], out_specs=[lvec, vec, row, row, vec],
        out_shape=[jax.ShapeDtypeStruct((1, LANES), F32), jax.ShapeDtypeStruct((1, D), F32),
                   jax.ShapeDtypeStruct((S, D), F32), jax.ShapeDtypeStruct((S, D), BF16),
                   jax.ShapeDtypeStruct((1, D), F32)],
        compiler_params=_params("arbitrary"), name=name)(x, g, target, y, gate)


def _head_mean(v, gmat):
    hi = v.astype(BF16)
    lo = (v - hi.astype(F32)).astype(BF16)
    return jnp.dot(hi, gmat, preferred_element_type=F32) + jnp.dot(lo, gmat, preferred_element_type=F32)


L_F, L_ONE, L_SHIFT = HEAD_DIM, HEAD_DIM + 3, HEAD_DIM + 6
KEY_CHUNKS = 2
SHIFT_FREE_LOGIT_BOUND = 60.0


def _lane():
    return lax.broadcasted_iota(jnp.int32, (1, LANES), 1)


def _split3(v):
    p1 = v.astype(BF16).astype(F32)
    r1 = v - p1
    p2 = r1.astype(BF16).astype(F32)
    p3 = (r1 - p2).astype(BF16).astype(F32)
    return p1, p2, p3


def _put3(lane, first, pieces):
    out = jnp.where(lane == first, pieces[0], 0.0)
    for k in (1, 2):
        out = out + jnp.where(lane == first + k, pieces[k], 0.0)
    return out


def _ones3(lane, first):
    return jnp.where((lane >= first) & (lane < first + 3), 1.0, 0.0)


def _lane_col(v, lane, idx):
    return jnp.sum(jnp.where(lane == idx, v, 0.0), axis=-1, keepdims=True)


def _head_of_pair(pair, e, lane):
    return jnp.where(lane < HEAD_DIM, pair if e == 0 else pltpu.roll(pair, HEAD_DIM, 1), 0.0)


def _pair_of_heads(even, odd, lane):
    return jnp.where(lane < HEAD_DIM, even, pltpu.roll(odd, HEAD_DIM, 1))


def _fox_prep_fwd(proj, fcum, qgain, kgain, gmat, *, name, ts=256):
    S = proj.shape[0]
    D = HEADS * HEAD_DIM
    ts = _tile(S, ts)
    scale = HEAD_DIM ** -0.5

    def body(p_ref, f_ref, qg_ref, kg_ref, gm_ref, q_ref, k_ref, v_ref):
        gm, lane, fc = gm_ref[...], _lane(), f_ref[...]
        for cpair in range(HEAD_PAIRS):
            qv = p_ref[:, pl.ds(cpair * LANES, LANES)].astype(F32)
            kv = p_ref[:, pl.ds(D + cpair * LANES, LANES)].astype(F32)
            vv = p_ref[:, pl.ds(2 * D + cpair * LANES, LANES)].astype(F32)
            qn = (qv * lax.rsqrt(_head_mean(qv * qv, gm) + EPS) * qg_ref[...]) * scale
            kn = kv * lax.rsqrt(_head_mean(kv * kv, gm) + EPS) * kg_ref[...]
            for e in range(2):
                h = 2 * cpair + e
                f3 = _split3(_lane_col(fc, lane, h))
                q_ref[h] = (_head_of_pair(qn, e, lane) + _put3(lane, L_F, f3) + _ones3(lane, L_ONE)).astype(BF16)
                k_ref[h] = (_head_of_pair(kn, e, lane) + _ones3(lane, L_F)
                            - _put3(lane, L_ONE, f3) + _ones3(lane, L_SHIFT)).astype(BF16)
                v_ref[h] = (_head_of_pair(vv, e, lane) + _ones3(lane, L_F)).astype(BF16)

    vec = pl.BlockSpec((1, LANES), lambda i: (0, 0))
    wide = pl.BlockSpec((HEADS, ts, LANES), lambda i: (0, i, 0))
    return pl.pallas_call(
        body, grid=(S // ts,),
        in_specs=[pl.BlockSpec((ts, 3 * D), lambda i: (i, 0)), pl.BlockSpec((ts, LANES), lambda i: (i, 0)), vec, vec,
                  pl.BlockSpec((LANES, LANES), lambda i: (0, 0))],
        out_specs=[wide, wide, wide], out_shape=[jax.ShapeDtypeStruct((HEADS, S, LANES), BF16)] * 3,
        compiler_params=_params("parallel"), name=name)(proj, fcum, qgain, kgain, gmat)


def _fox_prep_bwd(proj, dq_aug, dk_aug, dv_aug, dog, qgain, kgain, gmat, *, name, ts=256):
    S = proj.shape[0]
    D = HEADS * HEAD_DIM
    ts = _tile(S, ts)
    scale = HEAD_DIM ** -0.5

    def body(p_ref, dq_ref, dk_ref, dv_ref, dog_ref, qg_ref, kg_ref, gm_ref, o_ref, df_ref, dqg_ref, dkg_ref):
        _acc_init(pl.program_id(0), dqg_ref, dkg_ref)
        gm, lane = gm_ref[...], _lane()
        df = jnp.zeros((ts, LANES), F32)
        for cpair in range(HEAD_PAIRS):
            tiles = []
            for e in range(2):
                h = 2 * cpair + e
                tq, tk = dq_ref[h], dk_ref[h]
                df = jnp.where(lane == h, _lane_col(tq, lane, L_F) - _lane_col(tk, lane, L_ONE), df)
                tiles.append((tq, tk, dv_ref[h].astype(F32)))
            pair = [_pair_of_heads(tiles[0][k], tiles[1][k], lane) for k in range(3)]
            for half, g_ref, dg_ref, mult in ((0, qg_ref, dqg_ref, scale), (1, kg_ref, dkg_ref, 1.0)):
                v = p_ref[:, pl.ds(half * D + cpair * LANES, LANES)].astype(F32)
                r = lax.rsqrt(_head_mean(v * v, gm) + EPS)
                xh = v * r
                dn = pair[half] * mult
                dg_ref[...] += _colsum(dn * xh)
                dxh = dn * g_ref[...]
                o_ref[:, pl.ds(half * D + cpair * LANES, LANES)] = (r * (dxh - xh * _head_mean(dxh * xh, gm))).astype(BF16)
            o_ref[:, pl.ds(2 * D + cpair * LANES, LANES)] = pair[2].astype(BF16)
        o_ref[:, pl.ds(3 * D, D)] = dog_ref[...]
        o_ref[:, pl.ds(4 * D, LANES)] = jnp.zeros((ts, LANES), BF16)
        df_ref[...] = df

    row = pl.BlockSpec((ts, D), lambda i: (i, 0))
    wide = pl.BlockSpec((HEADS, ts, LANES), lambda i: (0, i, 0))
    vec = pl.BlockSpec((1, LANES), lambda i: (0, 0))
    return pl.pallas_call(
        body, grid=(S // ts,),
        in_specs=[pl.BlockSpec((ts, 2 * D), lambda i: (i, 0)), wide, wide, wide, row, vec, vec,
                  pl.BlockSpec((LANES, LANES), lambda i: (0, 0))],
        out_specs=[pl.BlockSpec((ts, 4 * D + LANES), lambda i: (i, 0)), pl.BlockSpec((ts, LANES), lambda i: (i, 0)), vec, vec],
        out_shape=[jax.ShapeDtypeStruct((S, 4 * D + LANES), BF16), jax.ShapeDtypeStruct((S, LANES), F32),
                   jax.ShapeDtypeStruct((1, LANES), F32), jax.ShapeDtypeStruct((1, LANES), F32)],
        compiler_params=_params("arbitrary"), name=name)(proj, dq_aug, dk_aug, dv_aug, dog, qgain, kgain, gmat)


def _log_sigmoid(z):
    return jnp.minimum(z, 0.0) - jnp.log(1.0 + jnp.exp(-jnp.abs(z)))


def _fox_decay_fwd(fl, bf, *, name, tb=256):
    S = fl.shape[0]
    tb = _tile(S, tb)

    def body(fl_ref, b_ref, o_ref, carry):
        @pl.when(pl.program_id(0) == 0)
        def _():
            carry[...] = jnp.zeros_like(carry)
        logf = _log_sigmoid(fl_ref[...] + b_ref[...])
        tri = (lax.broadcasted_iota(jnp.int32, (tb, tb), 1) <= lax.broadcasted_iota(jnp.int32, (tb, tb), 0)).astype(F32)
        cs = jnp.dot(tri, logf, preferred_element_type=F32, precision=lax.Precision.HIGHEST) + carry[...]
        o_ref[...] = cs
        carry[...] = _row_of(cs, tb - 1)

    return pl.pallas_call(
        body, grid=(S // tb,),
        in_specs=[pl.BlockSpec((tb, LANES), lambda i: (i, 0)), pl.BlockSpec((1, LANES), lambda i: (0, 0))],
        out_specs=pl.BlockSpec((tb, LANES), lambda i: (i, 0)),
        out_shape=jax.ShapeDtypeStruct((S, LANES), F32), scratch_shapes=[pltpu.VMEM((1, LANES), F32)],
        compiler_params=_params("arbitrary"), name=name)(fl, bf)


def _fox_decay_bwd(dF, fl, bf, dproj, *, name, tb=256):
    S = fl.shape[0]
    tb = _tile(S, tb)
    n = S // tb
    last_col = dproj.shape[1] // LANES - 1

    def body(df_ref, fl_ref, b_ref, dproj_hbm, o_ref, db_ref, carry):
        del dproj_hbm
        @pl.when(pl.program_id(0) == 0)
        def _():
            carry[...] = jnp.zeros_like(carry)
            db_ref[...] = jnp.zeros_like(db_ref)
        tri = (lax.broadcasted_iota(jnp.int32, (tb, tb), 1) >= lax.broadcasted_iota(jnp.int32, (tb, tb), 0)).astype(F32)
        rc = jnp.dot(tri, df_ref[...], preferred_element_type=F32, precision=lax.Precision.HIGHEST) + carry[...]
        carry[...] = _row_of(rc, 0)
        dfl = rc * jax.nn.sigmoid(-(fl_ref[...] + b_ref[...]))
        o_ref[...] = dfl.astype(BF16)
        db_ref[...] += _colsum(dfl)

    rev = pl.BlockSpec((tb, LANES), lambda i: (n - 1 - i, 0))
    vec = pl.BlockSpec((1, LANES), lambda i: (0, 0))
    return pl.pallas_call(
        body, grid=(n,), in_specs=[rev, rev, vec, pl.BlockSpec(memory_space=pl.ANY)],
        out_specs=[pl.BlockSpec((tb, LANES), lambda i: (n - 1 - i, last_col)), vec],
        out_shape=[jax.ShapeDtypeStruct(dproj.shape, BF16), jax.ShapeDtypeStruct((1, LANES), F32)],
        scratch_shapes=[pltpu.VMEM((1, LANES), F32)], input_output_aliases={3: 0},
        compiler_params=_params("arbitrary"), name=name)(dF, fl, bf, dproj)


_NT = (((1,), (1,)), ((), ()))
_TN = (((0,), (0,)), ((), ()))


def _causal(T, transposed=False):
    r, c = lax.broadcasted_iota(jnp.int32, (T, T), 0), lax.broadcasted_iota(jnp.int32, (T, T), 1)
    return r <= c if transposed else c <= r


def _with_shift(q_tile, shift, lane):
    keep = jnp.where((lane >= L_SHIFT) & (lane < L_SHIFT + 3), 0.0, q_tile)
    return (keep + _put3(lane, L_SHIFT, _split3(-shift))).astype(BF16)


def _ride_along(xchg, n_in, n_out, grid):
    if xchg is None:
        return (lambda body: body), [], [], [], [], []
    arrs, scatter = xchg
    n = len(arrs)

    def wrap(body):
        def wrapped(*refs):
            own_in, x_in = refs[:n_in], refs[n_in:n_in + n]
            own_out, x_out = refs[n_in + n:n_in + n + n_out], refs[n_in + n + n_out:n_in + 2 * n + n_out]
            rest = refs[n_in + 2 * n + n_out:]
            own_scratch, sems = rest[:len(rest) - 3], rest[len(rest) - 3:]
            ids = [pl.program_id(d) for d in range(len(grid))]
            first = functools.reduce(jnp.logical_and, [i == 0 for i in ids])
            last = functools.reduce(jnp.logical_and, [i == g - 1 for i, g in zip(ids, grid)])

            @pl.when(first)
            def _():
                for cp in _xchg_copies(x_in, x_out, scatter, *sems):
                    cp.start()

            body(*own_in, *own_out, *own_scratch)

            @pl.when(last)
            def _():
                for cp in _xchg_copies(x_in, x_out, scatter, *sems):
                    cp.wait()

        return wrapped

    return wrap, [_HBM] * n, [_HBM] * n, _xchg_out_shapes(arrs, scatter), _xchg_sems(n), list(arrs)


def _attn_rowmax(q_aug, k_aug, *, name, T=1024):
    S = q_aug.shape[1]
    T = _tile(S, T)
    n = S // T

    def body(q_ref, k_ref, o_ref, m_s):
        i, j = pl.program_id(1), pl.program_id(2)

        @pl.when(j == 0)
        def _():
            m_s[...] = jnp.full_like(m_s, NEG)

        def step(diag):
            s = lax.dot_general(q_ref[...], k_ref[...], _NT, preferred_element_type=F32)
            if diag:
                s = jnp.where(_causal(T), s, NEG)
            m = m_s[...]
            for cb in range(T // LANES):
                m = jnp.maximum(m, s[:, cb * LANES:(cb + 1) * LANES])
            m_s[...] = m

        @pl.when(j < i)
        def _():
            step(False)

        @pl.when(j == i)
        def _():
            step(True)
            o_ref[...] = _with_shift(q_ref[...].astype(F32), jnp.max(m_s[...], axis=-1, keepdims=True), _lane())

    qrow = pl.BlockSpec((None, T, LANES), lambda h, i, j: (h, i, 0))
    return pl.pallas_call(
        body, grid=(HEADS, n, n),
        in_specs=[qrow, pl.BlockSpec((None, T, LANES), lambda h, i, j: (h, jnp.minimum(j, i), 0))],
        out_specs=qrow, out_shape=jax.ShapeDtypeStruct(q_aug.shape, BF16),
        scratch_shapes=[pltpu.VMEM((T, LANES), F32)],
        compiler_params=_params("parallel", "parallel", "arbitrary"), name=name)(q_aug, k_aug)


def _attn_fwd(q_max, k_aug, v_aug, xchg=None, *, name, T=1024):
    S = q_max.shape[1]
    T = _tile(S, T)
    n = S // T
    wrap, x_in, x_out, x_shapes, x_sems, x_ops = _ride_along(xchg, 3, 2, (HEADS, n, n))

    def body(q_ref, k_ref, v_ref, o_ref, qb_ref, acc_s):
        i, j = pl.program_id(1), pl.program_id(2)

        @pl.when(j == 0)
        def _():
            acc_s[...] = jnp.zeros_like(acc_s)

        def block(rows, cols, mask):
            s = lax.dot_general(q_ref[rows, :], k_ref[cols, :], _NT, preferred_element_type=F32)
            if mask is not None:
                s = jnp.where(mask, s, NEG)
            return jnp.dot(jnp.exp(s).astype(BF16), v_ref[cols, :], preferred_element_type=F32)

        @pl.when(j < i)
        def _():
            chunk = T // KEY_CHUNKS
            upd = block(pl.ds(0, T), pl.ds(0, chunk), None)
            for c in range(1, KEY_CHUNKS):
                upd = upd + block(pl.ds(0, T), pl.ds(c * chunk, chunk), None)
            acc_s[...] += upd

        @pl.when(j == i)
        def _():
            half = T // 2
            lo, hi = pl.ds(0, half), pl.ds(half, half)
            acc_s[lo, :] += block(lo, lo, _causal(half))
            acc_s[hi, :] += block(hi, lo, None) + block(hi, hi, _causal(half))
            lane = _lane()
            acc = acc_s[...]
            l = _lane_col(acc, lane, L_F)
            o_ref[...] = acc / l
            qf = q_ref[...].astype(F32)
            row_max = -jnp.sum(jnp.where((lane >= L_SHIFT) & (lane < L_SHIFT + 3), qf, 0.0), axis=-1, keepdims=True)
            qb_ref[...] = _with_shift(qf, row_max + jnp.log(l), lane)

    qrow = pl.BlockSpec((None, T, LANES), lambda h, i, j: (h, i, 0))
    kv = pl.BlockSpec((None, T, LANES), lambda h, i, j: (h, jnp.minimum(j, i), 0))
    outs = pl.pallas_call(
        wrap(body), grid=(HEADS, n, n), in_specs=[qrow, kv, kv] + x_in, out_specs=[qrow, qrow] + x_out,
        out_shape=[jax.ShapeDtypeStruct(q_max.shape, F32), jax.ShapeDtypeStruct(q_max.shape, BF16)] + x_shapes,
        scratch_shapes=[pltpu.VMEM((T, LANES), F32)] + x_sems,
        compiler_params=_params("arbitrary", "arbitrary", "arbitrary"), name=name)(q_max, k_aug, v_aug, *x_ops)
    return outs[0], outs[1], outs[2:]


def _attn_bwd(q_lse, k_aug, v_aug, do_aug, xchg=None, *, name, T=1024):
    S = q_lse.shape[1]
    T = _tile(S, T)
    n = S // T
    wrap, x_in, x_out, x_shapes, x_sems, x_ops = _ride_along(xchg, 4, 3, (HEADS, n, n))

    def body(q_ref, do_ref, k_ref, v_ref, dq_ref, dk_ref, dv_ref, dq_s, dk_s, dv_s):
        j, i = pl.program_id(1), pl.program_id(2)

        def block(keys, queries, mask):
            q, do, k, v = q_ref[queries, :], do_ref[queries, :], k_ref[keys, :], v_ref[keys, :]
            st = lax.dot_general(k, q, _NT, preferred_element_type=F32)
            if mask is not None:
                st = jnp.where(mask, st, NEG)
            pt = jnp.exp(st)
            dst = (pt * lax.dot_general(v, do, _NT, preferred_element_type=F32)).astype(BF16)
            dv_s[keys, :] += jnp.dot(pt.astype(BF16), do, preferred_element_type=F32)
            dk_s[keys, :] += jnp.dot(dst, q, preferred_element_type=F32)
            return lax.dot_general(dst, k, _TN, preferred_element_type=F32)

        @pl.when(i == j)
        def _():
            dk_s[...] = jnp.zeros_like(dk_s)
            dv_s[...] = jnp.zeros_like(dv_s)

            @pl.when(j == 0)
            def _():
                dq_s[i] = jnp.zeros((T, LANES), F32)

            half = T // 2
            lo, hi = pl.ds(0, half), pl.ds(half, half)
            dq_s[i, lo, :] += block(lo, lo, _causal(half, transposed=True))
            dq_s[i, hi, :] += block(lo, hi, None) + block(hi, hi, _causal(half, transposed=True))
            dq_ref[...] = dq_s[j]

        @pl.when(i > j)
        def _():
            chunk = T // KEY_CHUNKS
            upd = block(pl.ds(0, chunk), pl.ds(0, T), None)
            for c in range(1, KEY_CHUNKS):
                upd = upd + block(pl.ds(c * chunk, chunk), pl.ds(0, T), None)

            @pl.when(j == 0)
            def _():
                dq_s[i] = upd

            @pl.when(j > 0)
            def _():
                dq_s[i] += upd

        @pl.when(i == n - 1)
        def _():
            dk_ref[...] = dk_s[...]
            dv_ref[...] = dv_s[...].astype(BF16)

    qrow = pl.BlockSpec((None, T, LANES), lambda h, j, i: (h, jnp.maximum(i, j), 0))
    kv = pl.BlockSpec((None, T, LANES), lambda h, j, i: (h, j, 0))
    outs = pl.pallas_call(
        wrap(body), grid=(HEADS, n, n), in_specs=[qrow, qrow, kv, kv] + x_in, out_specs=[kv, kv, kv] + x_out,
        out_shape=[jax.ShapeDtypeStruct(q_lse.shape, F32), jax.ShapeDtypeStruct(q_lse.shape, F32),
                   jax.ShapeDtypeStruct(q_lse.shape, BF16)] + x_shapes,
        scratch_shapes=[pltpu.VMEM((n, T, LANES), F32), pltpu.VMEM((T, LANES), F32), pltpu.VMEM((T, LANES), F32)] + x_sems,
        compiler_params=_params("arbitrary", "arbitrary", "arbitrary"), name=name)(q_lse, do_aug, k_aug, v_aug, *x_ops)
    return outs[0], outs[1], outs[2], outs[3:]


def _fox_gate_fwd(att_aug, proj, *, name, ts=256):
    S = att_aug.shape[1]
    D = HEADS * HEAD_DIM
    ts = _tile(S, ts)

    def body(a_ref, o_ref, att_ref, out_ref):
        lane = _lane()
        for cpair in range(HEAD_PAIRS):
            cols = pl.ds(cpair * LANES, LANES)
            pair = _pair_of_heads(a_ref[2 * cpair], a_ref[2 * cpair + 1], lane)
            att_ref[:, cols] = pair
            out_ref[:, cols] = (pair * jax.nn.sigmoid(o_ref[:, cols].astype(F32))).astype(BF16)

    row = pl.BlockSpec((ts, D), lambda i: (i, 0))
    return pl.pallas_call(
        body, grid=(S // ts,),
        in_specs=[pl.BlockSpec((HEADS, ts, LANES), lambda i: (0, i, 0)), pl.BlockSpec((ts, D), lambda i: (i, 3))],
        out_specs=[row, row], out_shape=[jax.ShapeDtypeStruct((S, D), F32), jax.ShapeDtypeStruct((S, D), BF16)],
        compiler_params=_params("parallel"), name=name)(att_aug, proj)


def _fox_gate_bwd(da, att, proj, *, name, ts=256):
    S, D = att.shape
    ts = _tile(S, ts)

    def body(da_ref, a_ref, o_ref, do_ref, dog_ref):
        lane = _lane()
        for cpair in range(HEAD_PAIRS):
            cols = pl.ds(cpair * LANES, LANES)
            dav, av = da_ref[:, cols].astype(F32), a_ref[:, cols]
            sg = jax.nn.sigmoid(o_ref[:, cols].astype(F32))
            datt = (dav * sg).astype(BF16).astype(F32)
            dog_ref[:, cols] = (dav * av * sg * (1.0 - sg)).astype(BF16)
            prod = datt * av
            for e in range(2):
                in_head = (lane < HEAD_DIM) if e == 0 else (lane >= HEAD_DIM)
                delta = jnp.sum(jnp.where(in_head, prod, 0.0), axis=-1, keepdims=True)
                tile = _head_of_pair(datt, e, lane) + _put3(lane, L_F, _split3(-delta))
                do_ref[2 * cpair + e] = tile.astype(BF16)

    row = pl.BlockSpec((ts, D), lambda i: (i, 0))
    return pl.pallas_call(
        body, grid=(S // ts,), in_specs=[row, row, pl.BlockSpec((ts, D), lambda i: (i, 3))],
        out_specs=[pl.BlockSpec((HEADS, ts, LANES), lambda i: (0, i, 0)), row],
        out_shape=[jax.ShapeDtypeStruct((HEADS, S, LANES), BF16), jax.ShapeDtypeStruct((S, D), BF16)],
        compiler_params=_params("parallel"), name=name)(da, att, proj)


def _row_of(block, r):
    rows = lax.broadcasted_iota(jnp.int32, block.shape, 0)
    return jnp.sum(jnp.where(rows == r, block, 0.0), axis=0, keepdims=True)


def _shift_down(cur, tail, k):
    out = pltpu.roll(cur, k, 0)
    top = out[:SUBLANES]
    rows = lax.broadcasted_iota(jnp.int32, top.shape, 0)
    for r in range(k):
        top = jnp.where(rows == r, _row_of(tail, tail.shape[0] - k + r), top)
    return jnp.concatenate([top, out[SUBLANES:]], axis=0)


def _shift_up(cur, head, k):
    n = cur.shape[0]
    out = pltpu.roll(cur, n - k, 0)
    bottom = out[n - SUBLANES:]
    rows = lax.broadcasted_iota(jnp.int32, bottom.shape, 0)
    for r in range(k):
        bottom = jnp.where(rows == SUBLANES - k + r, _row_of(head, r), bottom)
    return jnp.concatenate([out[:n - SUBLANES], bottom], axis=0)


HALO = 16


CONV_TC = 1408


def _pair_tiles(v):
    nc = v.shape[-1] // (2 * CONV_TC)
    return jnp.swapaxes(v.reshape(v.shape[:-1] + (2, nc, CONV_TC)), -3, -2).reshape(v.shape)


def _unpair_tiles(v):
    nc = v.shape[-1] // (2 * CONV_TC)
    return jnp.swapaxes(v.reshape(v.shape[:-1] + (nc, 2, CONV_TC)), -3, -2).reshape(v.shape)


def _conv_rows(cur, tail, w_ref, b_ref, cols):
    a1, a2 = _shift_down(cur, tail, 1), _shift_down(cur, tail, 2)
    return a2 * w_ref[0:1, cols] + a1 * w_ref[1:2, cols] + cur * w_ref[2:3, cols] + b_ref[:, cols], (a2, a1, cur)


def _conv_gate_fwd(a, cw, cb, *, name, ts=512):
    S, F2 = a.shape
    tc = CONV_TC
    ts = _tile(S, ts)
    nc = F2 // (2 * tc)
    sub = ts // HALO
    halves = (pl.ds(0, tc), pl.ds(tc, tc))

    def body(a_ref, t_ref, w_ref, b_ref, o_ref):
        first = pl.program_id(1) == 0
        pre = []
        for cols in halves:
            tail = jnp.where(first, 0.0, t_ref[:, cols].astype(F32))
            pre.append(_conv_rows(a_ref[:, cols].astype(F32), tail, w_ref, b_ref, cols)[0])
        g, val = pre
        o_ref[...] = (g * jax.nn.sigmoid(g) * val).astype(BF16)

    return pl.pallas_call(
        body, grid=(nc, S // ts),
        in_specs=[pl.BlockSpec((ts, 2 * tc), lambda j, i: (i, j)),
                  pl.BlockSpec((HALO, 2 * tc), lambda j, i: (jnp.maximum(i * sub - 1, 0), j)),
                  pl.BlockSpec((CONV_WIDTH, 2 * tc), lambda j, i: (0, j)), pl.BlockSpec((1, 2 * tc), lambda j, i: (0, j))],
        out_specs=pl.BlockSpec((ts, tc), lambda j, i: (i, j)),
        out_shape=jax.ShapeDtypeStruct((S, F2 // 2), BF16),
        compiler_params=_params("parallel", "parallel"), name=name)(a, a, cw, cb)


def _conv_gate_bwd(a, dact, cw, cb, *, name, ts=512):
    S, F2 = a.shape
    tc = CONV_TC
    ts = _tile(S, ts)
    nc = F2 // (2 * tc)
    sub = ts // HALO
    n_rows = S // ts
    halves = (pl.ds(0, tc), pl.ds(tc, tc))

    def body(a_ref, at_ref, ah_ref, d_ref, dh_ref, w_ref, b_ref, da_ref, s_ref):
        i = pl.program_id(1)
        _acc_init(i, s_ref)

        def dpre_of(rows, tails, d):
            (g, taps_g), (val, taps_v) = [_conv_rows(rows[h], tails[h], w_ref, b_ref, halves[h]) for h in range(2)]
            sg = jax.nn.sigmoid(g)
            return (d * val * (sg * (1.0 + g * (1.0 - sg))), d * (g * sg)), (taps_g, taps_v)

        cur = [a_ref[:, c].astype(F32) for c in halves]
        tail = [jnp.where(i == 0, 0.0, at_ref[:, c].astype(F32)) for c in halves]
        dpre, taps = dpre_of(cur, tail, d_ref[...].astype(F32))
        head, _ = dpre_of([ah_ref[:, c].astype(F32) for c in halves], [x[ts - HALO:, :] for x in cur], dh_ref[...].astype(F32))
        for h, cols in enumerate(halves):
            dd = dpre[h]
            nxt = jnp.where(i == n_rows - 1, 0.0, head[h])
            da_ref[:, cols] = (dd * w_ref[2:3, cols] + _shift_up(dd, nxt, 1) * w_ref[1:2, cols]
                               + _shift_up(dd, nxt, 2) * w_ref[0:1, cols]).astype(BF16)
            for r in range(CONV_WIDTH):
                s_ref[r:r + 1, cols] += _colsum(dd * taps[h][r])
            s_ref[CONV_WIDTH:CONV_WIDTH + 1, cols] += _colsum(dd)

    nxt_rows = lambda i: jnp.minimum((i + 1) * sub, S // HALO - 1)
    return pl.pallas_call(
        body, grid=(nc, n_rows),
        in_specs=[pl.BlockSpec((ts, 2 * tc), lambda j, i: (i, j)),
                  pl.BlockSpec((HALO, 2 * tc), lambda j, i: (jnp.maximum(i * sub - 1, 0), j)),
                  pl.BlockSpec((HALO, 2 * tc), lambda j, i: (nxt_rows(i), j)),
                  pl.BlockSpec((ts, tc), lambda j, i: (i, j)), pl.BlockSpec((HALO, tc), lambda j, i: (nxt_rows(i), j)),
                  pl.BlockSpec((CONV_WIDTH, 2 * tc), lambda j, i: (0, j)), pl.BlockSpec((1, 2 * tc), lambda j, i: (0, j))],
        out_specs=[pl.BlockSpec((ts, 2 * tc), lambda j, i: (i, j)), pl.BlockSpec((8, 2 * tc), lambda j, i: (0, j))],
        out_shape=[jax.ShapeDtypeStruct((S, F2), BF16), jax.ShapeDtypeStruct((8, F2), F32)],
        compiler_params=_params("parallel", "arbitrary"), name=name)(a, a, a, dact, dact, cw, cb)


def _gelu_parts(z):
    z2 = z * z
    t = jnp.tanh(GELU_C0 * (z + GELU_C1 * z * z2))
    val = 0.5 * z * (1.0 + t)
    grad = 0.5 * (1.0 + t) + 0.5 * z * (1.0 - t * t) * GELU_C0 * (1.0 + 3.0 * GELU_C1 * z2)
    return val, grad


def _sgu_fwd(pre, b_in, vgain, vbias, wm, bsb, *, name, ts=256):
    S, W2 = pre.shape
    W = W2 // 2
    gd = W // SGU_GROUPS
    ts = _tile(S, ts)

    def body(p_ref, b_ref, vg_ref, vb_ref, wm_ref, bs_ref, y_ref):
        u = _gelu_parts(p_ref[:, pl.ds(0, W)].astype(F32) + b_ref[:, pl.ds(0, W)])[0]
        v = _gelu_parts(p_ref[:, pl.ds(W, W)].astype(F32) + b_ref[:, pl.ds(W, W)])[0]
        mu = jnp.mean(v, axis=-1, keepdims=True)
        vc = v - mu
        rstd = lax.rsqrt(jnp.mean(vc * vc, axis=-1, keepdims=True) + EPS)
        vn = ((vc * rstd) * vg_ref[...] + vb_ref[...]).astype(BF16)
        for blk in range(ts // SGU_BLOCK):
            r0 = blk * SGU_BLOCK
            for g in range(SGU_GROUPS):
                c0 = g * gd
                mixed = jnp.dot(wm_ref[g], vn[r0:r0 + SGU_BLOCK, c0:c0 + gd], preferred_element_type=F32) + bs_ref[g]
                y_ref[pl.ds(r0, SGU_BLOCK), pl.ds(c0, gd)] = (u[r0:r0 + SGU_BLOCK, c0:c0 + gd] * mixed).astype(BF16)

    full = lambda shape: pl.BlockSpec(shape, lambda i: (0,) * len(shape))
    return pl.pallas_call(
        body, grid=(S // ts,),
        in_specs=[pl.BlockSpec((ts, W2), lambda i: (i, 0)), full((1, W2)), full((1, W)), full((1, W)),
                  full((SGU_GROUPS, SGU_BLOCK, SGU_BLOCK)), full((SGU_GROUPS, SGU_BLOCK, gd))],
        out_specs=pl.BlockSpec((ts, W), lambda i: (i, 0)), out_shape=jax.ShapeDtypeStruct((S, W), BF16),
        compiler_params=_params("parallel"), name=name)(pre, b_in, vgain, vbias, wm, bsb)


def _sgu_bwd(pre, dy, b_in, vgain, vbias, wm, wmt, bsb, *, name, ts=256):
    S, W2 = pre.shape
    W = W2 // 2
    gd = W // SGU_GROUPS
    ts = _tile(S, ts)
    last = S // ts - 1

    def body(p_ref, dy_ref, b_ref, vg_ref, vb_ref, wm_ref, wmt_ref, bs_ref,
             dp_ref, db_ref, dvg_ref, dvb_ref, dws_ref, dbs_ref, du_s, dvn_s, dbs_s):
        step = pl.program_id(0)
        _acc_init(step, db_ref, dvg_ref, dvb_ref, dws_ref, dbs_s)
        u, gu = _gelu_parts(p_ref[:, pl.ds(0, W)].astype(F32) + b_ref[:, pl.ds(0, W)])
        v, gv = _gelu_parts(p_ref[:, pl.ds(W, W)].astype(F32) + b_ref[:, pl.ds(W, W)])
        mu = jnp.mean(v, axis=-1, keepdims=True)
        vc = v - mu
        rstd = lax.rsqrt(jnp.mean(vc * vc, axis=-1, keepdims=True) + EPS)
        vhat = vc * rstd
        vn = (vhat * vg_ref[...] + vb_ref[...]).astype(BF16)
        dyv = dy_ref[...].astype(F32)
        for blk in range(ts // SGU_BLOCK):
            r0 = blk * SGU_BLOCK
            for g in range(SGU_GROUPS):
                c0 = g * gd
                vn_g = vn[r0:r0 + SGU_BLOCK, c0:c0 + gd]
                dy_g = dyv[r0:r0 + SGU_BLOCK, c0:c0 + gd]
                mixed = jnp.dot(wm_ref[g], vn_g, preferred_element_type=F32) + bs_ref[g]
                dmix = dy_g * u[r0:r0 + SGU_BLOCK, c0:c0 + gd]
                dmix_b = dmix.astype(BF16)
                du_s[pl.ds(r0, SGU_BLOCK), pl.ds(c0, gd)] = dy_g * mixed
                dvn_s[pl.ds(r0, SGU_BLOCK), pl.ds(c0, gd)] = jnp.dot(wmt_ref[g], dmix_b, preferred_element_type=F32)
                dws_ref[g] += lax.dot_general(dmix_b, vn_g, _NT, preferred_element_type=F32)
                dbs_s[g] += dmix
        dvn = dvn_s[...]
        dvg_ref[...] += _colsum(dvn * vhat)
        dvb_ref[...] += _colsum(dvn)
        dvh = dvn * vg_ref[...]
        dv = rstd * (dvh - jnp.mean(dvh, axis=-1, keepdims=True) - vhat * jnp.mean(dvh * vhat, axis=-1, keepdims=True))
        dpu = du_s[...] * gu
        dpv = dv * gv
        dp_ref[:, pl.ds(0, W)] = dpu.astype(BF16)
        dp_ref[:, pl.ds(W, W)] = dpv.astype(BF16)
        db_ref[:, pl.ds(0, W)] += _colsum(dpu)
        db_ref[:, pl.ds(W, W)] += _colsum(dpv)

        @pl.when(step == last)
        def _():
            for g in range(SGU_GROUPS):
                dbs_ref[g] = jnp.broadcast_to(jnp.sum(dbs_s[g], axis=-1, keepdims=True), (SGU_BLOCK, SGU_BLOCK))

    full = lambda shape: pl.BlockSpec(shape, lambda i: (0,) * len(shape))
    gsq = (SGU_GROUPS, SGU_BLOCK, SGU_BLOCK)
    return pl.pallas_call(
        body, grid=(S // ts,),
        in_specs=[pl.BlockSpec((ts, W2), lambda i: (i, 0)), pl.BlockSpec((ts, W), lambda i: (i, 0)),
                  full((1, W2)), full((1, W)), full((1, W)), full(gsq), full(gsq), full((SGU_GROUPS, SGU_BLOCK, gd))],
        out_specs=[pl.BlockSpec((ts, W2), lambda i: (i, 0)), full((1, W2)), full((1, W)), full((1, W)), full(gsq), full(gsq)],
        out_shape=[jax.ShapeDtypeStruct((S, W2), BF16), jax.ShapeDtypeStruct((1, W2), F32),
                   jax.ShapeDtypeStruct((1, W), F32), jax.ShapeDtypeStruct((1, W), F32),
                   jax.ShapeDtypeStruct(gsq, F32), jax.ShapeDtypeStruct(gsq, F32)],
        scratch_shapes=[pltpu.VMEM((ts, W), F32), pltpu.VMEM((ts, W), F32), pltpu.VMEM((SGU_GROUPS, SGU_BLOCK, gd), F32)],
        compiler_params=_params("arbitrary"), name=name)(pre, dy, b_in, vgain, vbias, wm, wmt, bsb)


def _paired_to_natural(w_up):
    nc = w_up.shape[1] // (2 * CONV_TC)
    return lambda q: (q % 2) * nc + q // 2


def _ffn_fwd(x, mods, n2g, w_up, cw, cb, w_down, tag):
    sh, sc, gate = mods
    h = _norm_mod_fwd(x, n2g, sh, sc, name=f"{tag}_norm_fwd")
    a = _mm(h, w_up, out_dtype=BF16, tn=CONV_TC, b_n=_paired_to_natural(w_up), name=f"{tag}_up")
    act = _conv_gate_fwd(a, cw, cb, name=f"{tag}_conv_fwd")
    x_out, y = _mm(act, w_down, tk=1408, res=(x, gate), name=f"{tag}_down")
    return x_out, (x, h, a, act, y)


def _ffn_bwd(dy, saved, mods, n2g, w_up, cw, cb, w_down, dres, prev, tag):
    x, h, a, act, _ = saved
    sh, sc, gate = mods
    dact = _mm(dy, w_down, tb=True, out_dtype=BF16, tn=1408, name=f"{tag}_down_dx")
    dw_down = _mm(act, dy, ta=True, out_dtype=BF16, tm=1408, name=f"{tag}_down_dw")
    da, sums = _conv_gate_bwd(a, dact, cw, cb, name=f"{tag}_conv_bwd")
    dh = _mm(da, w_up, tb=True, tk=CONV_TC, b_k=_paired_to_natural(w_up), name=f"{tag}_up_dx")
    dw_up = _mm(h, da, ta=True, out_dtype=BF16, tn=CONV_TC, o_n=_paired_to_natural(w_up), name=f"{tag}_up_dw")
    outs = _norm_mod_bwd(dh, x, n2g, sc, dres, prev, name=f"{tag}_norm_bwd")
    sums = _unpair_tiles(sums)
    return outs, dict(w_up=dw_up, w_down=dw_down, conv_w=sums[0:CONV_WIDTH], conv_b=sums[CONV_WIDTH])


def _local_step(x, target, w, mods, late=None, early=None, last=None):
    S, D = x.shape
    lane = jnp.arange(LANES)
    gmat = jnp.where((lane[:, None] // HEAD_DIM) == (lane[None, :] // HEAD_DIM), 1.0 / HEAD_DIM, 0.0).astype(BF16)
    qg2 = jnp.tile(w["fox_q_gain"].reshape(1, HEAD_DIM), (1, 2))
    kg2 = jnp.tile(w["fox_k_gain"].reshape(1, HEAD_DIM), (1, 2))
    bf_pad = jnp.pad(w["fox_b_f"].reshape(1, HEADS), ((0, 0), (0, LANES - HEADS)))
    w_in_pad = jnp.pad(w["fox_w_in"], ((0, 0), (0, 4 * D + LANES - w["fox_w_in"].shape[1])))
    w_qkvo, w_f = w_in_pad[:, :4 * D], w_in_pad[:, 4 * D:]
    tpos = jnp.arange(SGU_BLOCK)
    smask = (tpos[None, :] // SGU_CHUNK) <= (tpos[:, None] // SGU_CHUNK)
    wm32 = jnp.where(smask[None], w["sgu_w_s"], 0.0)
    wm, wmt = wm32.astype(BF16), jnp.swapaxes(wm32, 1, 2).astype(BF16)
    gd = w["sgu_v_gain"].shape[-1] // SGU_GROUPS
    bsb = jnp.broadcast_to(w["sgu_b_s"][:, :, None], (SGU_GROUPS, SGU_BLOCK, gd))
    vec = lambda v: v.reshape(1, -1)

    sh1, sc1, g1 = mods[0][0:3]
    h0 = _norm_mod_fwd(x, vec(w["norm1_g"][0]), sh1, sc1, name="fox_norm_fwd")
    proj = _mm(h0, w_qkvo, out_dtype=BF16, name="fox_proj")
    fl = _mm(h0, w_f, name="fox_forget_proj")
    fcum = _fox_decay_fwd(fl, bf_pad, name="fox_decay")
    q_aug, k_aug, v_aug = _fox_prep_fwd(proj, fcum, qg2, kg2, gmat, name="fox_qk_norm")
    logit_bound = 8.0 * jnp.max(jnp.abs(w["fox_q_gain"])) * jnp.max(jnp.abs(w["fox_k_gain"]))
    q_max = lax.cond(logit_bound <= SHIFT_FREE_LOGIT_BOUND, lambda: q_aug,
                     lambda: _attn_rowmax(q_aug, k_aug, name="fox_attn_rowmax"))
    xchg = None if late is None else (late[0], [False] * len(late[0]))
    att_aug, q_lse, gathered = _attn_fwd(q_max, k_aug, v_aug, xchg, name="fox_attn_fwd")
    if late is not None:
        w = {**w, **late[1](gathered)}
    w = dict(w, ffn_conv_w=_pair_tiles(w["ffn_conv_w"]), ffn_conv_b=_pair_tiles(w["ffn_conv_b"]))
    att, ag = _fox_gate_fwd(att_aug, proj, name="fox_gate_fwd")
    x1, y_fox = _mm(ag, w["fox_w_out"], res=(x, g1), name="fox_out")
    x2, ffn0 = _ffn_fwd(x1, mods[0][3:6], vec(w["norm2_g"][0]), w["ffn_w_up"][0], w["ffn_conv_w"][0],
                        vec(w["ffn_conv_b"][0]), w["ffn_w_down"][0], "ffn0")

    sh1b, sc1b, g1b = mods[1][0:3]
    h1 = _norm_mod_fwd(x2, vec(w["norm1_g"][1]), sh1b, sc1b, name="sgu_norm_fwd")
    pre = _mm(h1, w["sgu_w_in"], out_dtype=BF16, name="sgu_in")
    b_in, vg, vb = vec(w["sgu_b_in"]), vec(w["sgu_v_gain"]), vec(w["sgu_v_bias"])
    ys = _sgu_fwd(pre, b_in, vg, vb, wm, bsb, name="sgu_core_fwd")
    x3, y_sgu = _mm(ys, w["sgu_w_out"], res=(x2, g1b), name="sgu_out")
    x4, ffn1 = _ffn_fwd(x3, mods[1][3:6], vec(w["norm2_g"][1]), w["ffn_w_up"][1], w["ffn_conv_w"][1],
                        vec(w["ffn_conv_b"][1]), w["ffn_w_down"][1], "ffn1")

    loss, d_final_g, dx4, dy_ffn1, dgate_ffn1 = _final_loss(x4, vec(w["final_g"]), target, ffn1[4], mods[1][5], name="final_loss")

    (dx3, dn2g_1, dsh2_1, dsc2_1, dy_sgu, dgate_sgu), g_ffn1 = _ffn_bwd(
        dy_ffn1, ffn1, mods[1][3:6], vec(w["norm2_g"][1]), w["ffn_w_up"][1], w["ffn_conv_w"][1], vec(w["ffn_conv_b"][1]),
        w["ffn_w_down"][1], dx4, (y_sgu, g1b), "ffn1")

    dys = _mm(dy_sgu, w["sgu_w_out"], tb=True, out_dtype=BF16, name="sgu_out_dx")
    dw_sgu_out = _mm(ys, dy_sgu, ta=True, out_dtype=BF16, name="sgu_out_dw")
    dpre, db_in, dvg, dvb, dws, dbs = _sgu_bwd(pre, dys, b_in, vg, vb, wm, wmt, bsb, name="sgu_core_bwd")
    dh1 = _mm(dpre, w["sgu_w_in"], tb=True, name="sgu_in_dx")
    dw_sgu_in = _mm(h1, dpre, ta=True, out_dtype=BF16, name="sgu_in_dw")
    dx2, dn1g_1, dsh1_1, dsc1_1, dy_ffn0, dgate_ffn0 = _norm_mod_bwd(
        dh1, x2, vec(w["norm1_g"][1]), sc1b, dx3, (ffn0[4], mods[0][5]), name="sgu_norm_bwd")

    (dx1, dn2g_0, dsh2_0, dsc2_0, dy_fox, dgate_fox), g_ffn0 = _ffn_bwd(
        dy_ffn0, ffn0, mods[0][3:6], vec(w["norm2_g"][0]), w["ffn_w_up"][0], w["ffn_conv_w"][0], vec(w["ffn_conv_b"][0]),
        w["ffn_w_down"][0], dx2, (y_fox, g1), "ffn0")

    dag = _mm(dy_fox, w["fox_w_out"], tb=True, out_dtype=BF16, name="fox_out_dx")
    dw_fox_out = _mm(ag, dy_fox, ta=True, out_dtype=BF16, name="fox_out_dw")
    do_aug, dog = _fox_gate_bwd(dag, att, proj, name="fox_gate_bwd")
    grads = dict(
        sgu_w_in=dw_sgu_in, sgu_b_in=db_in[0], sgu_v_gain=dvg[0], sgu_v_bias=dvb[0],
        sgu_w_s=jnp.where(smask[None], dws, 0.0), sgu_b_s=dbs[:, :, 0], sgu_w_out=dw_sgu_out,
        ffn_w_up=jnp.stack([g_ffn0["w_up"], g_ffn1["w_up"]]),
        ffn_conv_w=jnp.stack([g_ffn0["conv_w"], g_ffn1["conv_w"]]),
        ffn_conv_b=jnp.stack([g_ffn0["conv_b"], g_ffn1["conv_b"]]),
        ffn_w_down=jnp.stack([g_ffn0["w_down"], g_ffn1["w_down"]]),
        final_g=d_final_g[0], fox_w_out=dw_fox_out, norm2_g=jnp.concatenate([dn2g_0, dn2g_1], axis=0),
    )
    xchg = None if early is None else early(grads)
    dq_aug, dk_aug, dv_aug, exchanged = _attn_bwd(q_lse, k_aug, v_aug, do_aug, xchg, name="fox_attn_bwd")
    dproj, dF, dqg, dkg = _fox_prep_bwd(proj, dq_aug, dk_aug, dv_aug, dog, qg2, kg2, gmat, name="fox_qk_norm_bwd")
    dproj, dbf = _fox_decay_bwd(dF, fl, bf_pad, dproj, name="fox_decay_bwd")
    dw_fox_in = _mm(h0, dproj, ta=True, out_dtype=BF16, tn=1408, name="fox_proj_dw")[:, :w["fox_w_in"].shape[1]]
    xchg = None if last is None else last(dict(fox_w_in=dw_fox_in))
    dh0 = _mm(dproj, w_in_pad, tb=True, tk=1408, xchg=xchg, name="fox_proj_dx")
    dh0, exchanged_last = dh0 if last is not None else (dh0, [])
    dx0, dn1g_0, dsh1_0, dsc1_0 = _norm_mod_bwd(dh0, x, vec(w["norm1_g"][0]), sc1, dx1, None, name="fox_norm_bwd")

    dmod0 = jnp.concatenate([dsh1_0, dsc1_0, dgate_fox, dsh2_0, dsc2_0, dgate_ffn0], axis=1)
    dmod1 = jnp.concatenate([dsh1_1, dsc1_1, dgate_sgu, dsh2_1, dsc2_1, dgate_ffn1], axis=1)
    grads.update(
        fox_w_in=dw_fox_in,
        fox_b_f=dbf[0, :HEADS],
        fox_q_gain=dqg[0, :HEAD_DIM] + dqg[0, HEAD_DIM:],
        fox_k_gain=dkg[0, :HEAD_DIM] + dkg[0, HEAD_DIM:],
        fox_w_out=dw_fox_out,
        ada_b=jnp.concatenate([dmod0, dmod1], axis=0),
        norm1_g=jnp.concatenate([dn1g_0, dn1g_1], axis=0),
    )
    return loss[0, 0], dx0, grads, exchanged, exchanged_last


_HBM = pl.BlockSpec(memory_space=pl.ANY)
N_PEER = N_DEV - 1


def _xchg_out_shapes(arrs, scatter):
    return [jax.ShapeDtypeStruct(a.shape if s else (N_DEV,) + a.shape, a.dtype) for a, s in zip(arrs, scatter)]


def _xchg_sems(n):
    return [pltpu.SemaphoreType.DMA((n * N_PEER,)), pltpu.SemaphoreType.DMA((n * N_PEER,)), pltpu.SemaphoreType.DMA((n,))]


def _xchg_copies(ins, outs, scatter, send, recv, loc):
    x, y, c = lax.axis_index("x"), lax.axis_index("y"), lax.axis_index("c")
    me = 4 * x + 2 * y + c
    copies = []
    for a in range(len(ins)):
        copies.append(pltpu.make_async_copy(ins[a].at[me] if scatter[a] else ins[a], outs[a].at[me], loc.at[a]))
        for k in range(1, N_DEV):
            px = 1 - x if k & 4 else x
            py = 1 - y if k & 2 else y
            pc = 1 - c if k & 1 else c
            copies.append(pltpu.make_async_remote_copy(
                src_ref=ins[a].at[4 * px + 2 * py + pc] if scatter[a] else ins[a], dst_ref=outs[a].at[me],
                send_sem=send.at[a * N_PEER + k - 1], recv_sem=recv.at[a * N_PEER + k - 1],
                device_id=(px, py, pc), device_id_type=MESH))
    return copies


def _exchange(arrs, scatter, *, name):
    n = len(arrs)

    def body(*refs):
        copies = _xchg_copies(refs[:n], refs[n:2 * n], scatter, *refs[2 * n:])
        for cp in copies:
            cp.start()
        for cp in copies:
            cp.wait()

    return pl.pallas_call(
        body, in_specs=[_HBM] * n, out_specs=[_HBM] * n, out_shape=_xchg_out_shapes(arrs, scatter),
        scratch_shapes=_xchg_sems(n),
        compiler_params=pltpu.CompilerParams(has_side_effects=True), name=name)(*arrs)


def _adamw(w, parts, m, v, *, name, tr=256):
    L, R, C = w.shape
    P = parts.shape[0]
    tr = next(t for t in range(min(R, tr), 0, -1) if R % t == 0 and (t % 16 == 0 or t == R))
    nr = R // tr
    c1 = 1.0 - ADAM_B1 ** ADAM_STEP
    c2 = 1.0 - ADAM_B2 ** ADAM_STEP

    def body(w_ref, p_ref, m_ref, v_ref, g_ref, d_ref, mo_ref, vo_ref):
        g = p_ref[0].astype(F32)
        for p in range(1, P):
            g = g + p_ref[p].astype(F32)
        mn = ADAM_B1 * m_ref[0] + (1.0 - ADAM_B1) * g
        vn = ADAM_B2 * v_ref[0] + (1.0 - ADAM_B2) * (g * g)
        g_ref[0] = g
        mo_ref[0] = mn
        vo_ref[0] = vn
        d_ref[0] = -ADAM_LR * ((mn / c1) / (jnp.sqrt(vn / c2) + ADAM_EPS) + ADAM_WD * w_ref[0])

    row = pl.BlockSpec((1, tr, C), lambda l, i: (l, i, 0))
    return pl.pallas_call(
        body, grid=(L, nr), in_specs=[row, pl.BlockSpec((P, tr, C), lambda l, i: (0, l * nr + i, 0)), row, row],
        out_specs=[row] * 4, out_shape=[jax.ShapeDtypeStruct((L, R, C), F32)] * 4,
        compiler_params=_params("parallel", "parallel"), name=name)(w, parts, m, v)


def _sum_parts(parts, *, name):
    P, R, C = parts.shape

    def body(p_ref, o_ref):
        g = p_ref[0]
        for p in range(1, P):
            g = g + p_ref[p]
        o_ref[...] = g

    return pl.pallas_call(body, out_shape=jax.ShapeDtypeStruct((R, C), F32), name=name)(parts)


WEIGHTS = ["fox_w_in", "fox_b_f", "fox_q_gain", "fox_k_gain", "fox_w_out", "sgu_w_in", "sgu_b_in", "sgu_v_gain",
           "sgu_v_bias", "sgu_w_s", "sgu_b_s", "sgu_w_out", "ffn_w_up", "ffn_conv_w", "ffn_conv_b", "ffn_w_down",
           "ada_w", "ada_b", "norm1_g", "norm2_g", "final_g"]
BIG_AXIS = dict(fox_w_in=1, fox_w_out=0, sgu_w_in=1, sgu_w_out=0, ffn_w_up=1, ffn_w_down=0, ada_w=1)
SMALL_SHARDED = ["sgu_b_in", "sgu_v_gain", "sgu_v_bias", "ffn_conv_w"]
SINGLE_LAYER = ("fox_", "sgu_")
BEFORE_ATTENTION = ["fox_w_in"]
AFTER_ATTENTION = ["fox_w_out", "sgu_w_in", "sgu_w_out", "ffn_w_up", "ffn_w_down"]
SMALL_EARLY = ["sgu_b_in", "sgu_v_gain", "sgu_v_bias", "sgu_w_s", "sgu_b_s", "ffn_conv_w", "ffn_conv_b", "norm2_g", "final_g"]


def _assemble(stacked, layers, axis):
    _, lr, cc = stacked.shape
    r = lr // layers
    s4 = stacked.reshape(N_DEV, layers, r, cc)
    if axis == 0:
        return s4.transpose(1, 0, 2, 3).reshape(layers, N_DEV * r, cc)
    return s4.transpose(1, 2, 0, 3).reshape(layers, r, N_DEV * cc)


def _disassemble(full, axis):
    layers, R, C = full.shape
    if axis == 0:
        r = R // N_DEV
        return full.reshape(layers, N_DEV, r, C).transpose(1, 0, 2, 3).reshape(N_DEV, layers * r, C)
    cc = C // N_DEV
    return full.reshape(layers, R, N_DEV, cc).transpose(2, 0, 1, 3).reshape(N_DEV, layers * R, cc)


def kernel(x, c, fox_w_in, fox_b_f, fox_q_gain, fox_k_gain, fox_w_out, sgu_w_in, sgu_b_in, sgu_v_gain, sgu_v_bias, sgu_w_s, sgu_b_s, sgu_w_out, ffn_w_up, ffn_conv_w, ffn_conv_b, ffn_w_down, ada_w, ada_b, norm1_g, norm2_g, final_g, loss_target, m_fox_w_in, m_fox_b_f, m_fox_q_gain, m_fox_k_gain, m_fox_w_out, m_sgu_w_in, m_sgu_b_in, m_sgu_v_gain, m_sgu_v_bias, m_sgu_w_s, m_sgu_b_s, m_sgu_w_out, m_ffn_w_up, m_ffn_conv_w, m_ffn_conv_b, m_ffn_w_down, m_ada_w, m_ada_b, m_norm1_g, m_norm2_g, m_final_g, v_fox_w_in, v_fox_b_f, v_fox_q_gain, v_fox_k_gain, v_fox_w_out, v_sgu_w_in, v_sgu_b_in, v_sgu_v_gain, v_sgu_v_bias, v_sgu_w_s, v_sgu_b_s, v_sgu_w_out, v_ffn_w_up, v_ffn_conv_w, v_ffn_conv_b, v_ffn_w_down, v_ada_w, v_ada_b, v_norm1_g, v_norm2_g, v_final_g):
    args = dict(locals())
    wts = {n: args[n] for n in WEIGHTS}
    ms = {n: args["m_" + n] for n in WEIGHTS}
    vs = {n: args["v_" + n] for n in WEIGHTS}
    me = 4 * lax.axis_index("x") + 2 * lax.axis_index("y") + lax.axis_index("c")

    shard2d = lambda n: wts[n].astype(BF16).reshape(-1, wts[n].shape[-1])

    def assemble_big(names, got):
        out = {}
        for n, g in zip(names, got):
            f = _assemble(g, wts[n].shape[0], BIG_AXIS[n])
            out[n] = f[0] if n.startswith(SINGLE_LAYER) else f
        return out

    def blocks_of(names, grads):
        return [_disassemble(grads[n] if grads[n].ndim == 3 else grads[n][None], BIG_AXIS[n]) for n in names]

    send = [c] + [shard2d(n) for n in BEFORE_ATTENTION] + [wts[n].reshape(-1, wts[n].shape[-1]) for n in SMALL_SHARDED]
    got = _exchange(send, [False] * len(send), name="gather_first")
    c_all = got[0].reshape(N_DEV, -1)
    full = assemble_big(BEFORE_ATTENTION, got[1:1 + len(BEFORE_ATTENTION)])
    for n, g in zip(SMALL_SHARDED, got[1 + len(BEFORE_ATTENTION):]):
        lead = wts[n].shape[:-1]
        f = jnp.moveaxis(g.reshape((N_DEV,) + wts[n].shape), 0, -2).reshape(lead + (-1,))
        full[n] = f[0] if n.startswith(SINGLE_LAYER) else f
    for n in WEIGHTS:
        if n not in full and n not in BIG_AXIS:
            full[n] = wts[n][0] if n.startswith(SINGLE_LAYER) else wts[n]

    ada_cols = wts["ada_w"].shape[-1]
    mod_rows = []
    for i in range(2):
        b_mine = lax.dynamic_slice_in_dim(wts["ada_b"][i], me * ada_cols, ada_cols).reshape(1, ada_cols)
        m, c_act = _ada_mod(c_all, wts["ada_w"][i].astype(BF16), b_mine, name=f"ada_mod_{i}")
        mod_rows.append(m)
    got = _exchange([jnp.concatenate(mod_rows, axis=1)[:, None, :]], [True], name="exchange_mods")[0]
    d_model = x.shape[-1]
    mods = []
    for i in range(2):
        mod = got[:, 0, i * ada_cols:(i + 1) * ada_cols].reshape(1, N_DEV * ada_cols)
        mods.append([mod[:, k * d_model:(k + 1) * d_model] for k in range(6)])

    small = [n for n in WEIGHTS if n not in BIG_AXIS]
    small_late = [n for n in small if n not in SMALL_EARLY]

    def pack_flat(arrays):
        flat = jnp.concatenate([a.reshape(-1).astype(F32) for a in arrays])
        rows = -(-flat.shape[0] // (8 * LANES)) * 8
        return jnp.pad(flat, (0, rows * LANES - flat.shape[0])).reshape(rows, LANES)

    late = ([shard2d(n) for n in AFTER_ATTENTION], lambda g: assemble_big(AFTER_ATTENTION, g))
    loss, grad_x, grads, got_late, got_last = _local_step(
        x[0], loss_target[0], full, mods, late,
        lambda gr: (blocks_of(AFTER_ATTENTION, gr) + [pack_flat([gr[n] for n in SMALL_EARLY])],
                    [True] * len(AFTER_ATTENTION) + [False]),
        lambda gr: (blocks_of(BEFORE_ATTENTION, gr), [True] * len(BEFORE_ATTENTION)))

    flat_late_all = _exchange([pack_flat([loss] + [grads[n] for n in small_late])], [False], name="gather_small_grads")[0]
    total_late = _sum_parts(flat_late_all, name="sum_small_grads_late").reshape(-1)
    total_early = _sum_parts(got_late[len(AFTER_ATTENTION)], name="sum_small_grads_early").reshape(-1)
    loss_out = total_late[0]
    summed, offs = {}, {}
    for vec, names, off in ((total_early, SMALL_EARLY, 0), (total_late, small_late, 1)):
        for n in names:
            size = math.prod(grads[n].shape)
            summed[n], offs[n] = vec[off:off + size].reshape(grads[n].shape), off
            off += size

    off_ada = offs["ada_b"]
    dmod_all = flat_late_all.reshape(N_DEV, -1)[:, off_ada:off_ada + 2 * N_DEV * ada_cols].reshape(N_DEV, 2, N_DEV * ada_cols)
    dmod_mine = lax.dynamic_slice_in_dim(dmod_all, me * ada_cols, ada_cols, axis=2)
    d_ada = [_mm(c_act, jnp.pad(dmod_mine[:, i], ((0, c_act.shape[0] - N_DEV), (0, 0))).astype(BF16), ta=True,
                 name=f"ada_dw_{i}") for i in range(2)]

    out_g, out_d, out_m, out_v = {}, {}, {}, {}
    summands = dict(zip(BEFORE_ATTENTION, got_last))
    summands.update(zip(AFTER_ATTENTION, got_late))
    summands["ada_w"] = jnp.concatenate(d_ada, axis=0)[None]
    for n, p in summands.items():
        out_g[n], out_d[n], out_m[n], out_v[n] = _adamw(wts[n], p, ms[n], vs[n], name=f"adamw_{n}")
    small_g = {}
    for n in small:
        g = summed[n]
        if n in SMALL_SHARDED:
            blk = g.shape[-1] // N_DEV
            g = lax.dynamic_slice_in_dim(g, me * blk, blk, axis=g.ndim - 1)
        small_g[n] = g.reshape(wts[n].shape)
    cat = lambda d: jnp.concatenate([d[n].reshape(-1) for n in small])
    n_small = sum(math.prod(wts[n].shape) for n in small)
    rows2 = -(-n_small // (256 * LANES)) * 256
    pack = lambda d, fill: jnp.pad(cat(d), (0, rows2 * LANES - n_small), constant_values=fill).reshape(1, rows2, LANES)
    g, d, mn, vn = _adamw(pack(wts, 0.0), pack(small_g, 0.0), pack(ms, 0.0), pack(vs, 1.0), name="adamw_small")
    off = 0
    for n in small:
        size = math.prod(wts[n].shape)
        for src, dst in ((g, out_g), (d, out_d), (mn, out_m), (vn, out_v)):
            dst[n] = src.reshape(-1)[off:off + size].reshape(wts[n].shape)
        off += size

    return (loss_out, grad_x[None], *[out_g[n] for n in WEIGHTS], *[out_d[n] for n in WEIGHTS],
            *[out_m[n] for n in WEIGHTS], *[out_v[n] for n in WEIGHTS])
```

```python
import functools
import math

import jax
import jax.numpy as jnp
from jax import lax
from jax.experimental import pallas as pl
from jax.experimental.pallas import tpu as pltpu

F32, BF16 = jnp.float32, jnp.bfloat16
N_DEV = 8
HEADS, HEAD_DIM = 16, 64
HEAD_PAIRS = HEADS // 2
LANES = 128
SUBLANES = 8
EPS = 1e-6
SGU_BLOCK, SGU_GROUPS, SGU_CHUNK = 128, 8, 64
CONV_WIDTH = 3
ADAM_LR, ADAM_B1, ADAM_B2, ADAM_EPS, ADAM_WD, ADAM_STEP = 0.001, 0.9, 0.999, 1e-08, 0.01, 10
NEG = -1e30
GELU_C0, GELU_C1 = math.sqrt(2.0 / math.pi), 0.044715
MESH = pl.DeviceIdType.MESH
VMEM_LIMIT = 56 * 1024 * 1024


def _tile(dim, pref):
    if dim <= pref:
        return dim
    t = (pref // LANES) * LANES
    while t >= LANES:
        if dim % t == 0:
            return t
        t -= LANES
    return dim


def _params(*sem):
    return pltpu.CompilerParams(dimension_semantics=sem, vmem_limit_bytes=VMEM_LIMIT)


def _mm(a, b, *, name, ta=False, tb=False, out_dtype=F32, tm=1024, tn=1024, tk=1024, res=None, b_n=None, b_k=None, o_n=None,
        xchg=None):
    M = a.shape[1] if ta else a.shape[0]
    K = a.shape[0] if ta else a.shape[1]
    N = b.shape[0] if tb else b.shape[1]
    tm, tn, tk = _tile(M, tm), _tile(N, tn), _tile(K, tk)
    nk = K // tk
    dims = (((0 if ta else 1,), (1 if tb else 0,)), ((), ()))
    same = lambda idx: idx
    b_n, b_k, o_n = b_n or same, b_k or same, o_n or same
    a_spec = pl.BlockSpec((tk, tm), lambda i, j, k: (k, i)) if ta else pl.BlockSpec((tm, tk), lambda i, j, k: (i, k))
    b_spec = (pl.BlockSpec((tn, tk), lambda i, j, k: (b_n(j), b_k(k))) if tb
              else pl.BlockSpec((tk, tn), lambda i, j, k: (b_k(k), b_n(j))))
    o_spec = pl.BlockSpec((tm, tn), lambda i, j, k: (i, o_n(j)))

    def accumulate(a_ref, b_ref, acc):
        @pl.when(pl.program_id(2) == 0)
        def _():
            acc[...] = jnp.zeros_like(acc)
        acc[...] += lax.dot_general(a_ref[...], b_ref[...], dims, preferred_element_type=F32)

    if res is None:
        def body(a_ref, b_ref, o_ref, acc):
            accumulate(a_ref, b_ref, acc)

            @pl.when(pl.program_id(2) == nk - 1)
            def _():
                o_ref[...] = acc[...].astype(o_ref.dtype)

        if xchg is None:
            return pl.pallas_call(
                body, grid=(M // tm, N // tn, nk), in_specs=[a_spec, b_spec], out_specs=o_spec,
                out_shape=jax.ShapeDtypeStruct((M, N), out_dtype), scratch_shapes=[pltpu.VMEM((tm, tn), F32)],
                compiler_params=_params("parallel", "parallel", "arbitrary"), name=name)(a, b)
        grid = (M // tm, N // tn, nk)
        wrap, x_in, x_out, x_shapes, x_sems, x_ops = _ride_along(xchg, 2, 1, grid)
        outs = pl.pallas_call(
            wrap(body), grid=grid, in_specs=[a_spec, b_spec] + x_in, out_specs=[o_spec] + x_out,
            out_shape=[jax.ShapeDtypeStruct((M, N), out_dtype)] + x_shapes,
            scratch_shapes=[pltpu.VMEM((tm, tn), F32)] + x_sems,
            compiler_params=_params("arbitrary", "arbitrary", "arbitrary"), name=name)(a, b, *x_ops)
        return outs[0], outs[1:]

    x, gate = res

    def body_res(a_ref, b_ref, x_ref, g_ref, o_ref, y_ref, acc):
        accumulate(a_ref, b_ref, acc)

        @pl.when(pl.program_id(2) == nk - 1)
        def _():
            y = acc[...]
            o_ref[...] = x_ref[...] + g_ref[...] * y
            y_ref[...] = y.astype(BF16)

    return pl.pallas_call(
        body_res, grid=(M // tm, N // tn, nk),
        in_specs=[a_spec, b_spec, o_spec, pl.BlockSpec((1, tn), lambda i, j, k: (0, j))],
        out_specs=[o_spec, o_spec],
        out_shape=[jax.ShapeDtypeStruct((M, N), F32), jax.ShapeDtypeStruct((M, N), BF16)],
        scratch_shapes=[pltpu.VMEM((tm, tn), F32)],
        compiler_params=_params("parallel", "parallel", "arbitrary"), name=name)(a, b, x, gate)


def _ada_mod(c_rows, w, b, *, name):
    R, D = c_rows.shape
    N = w.shape[1]
    tn = _tile(N, 1536)
    rows = 16
    c_pad = jnp.pad(c_rows, ((0, rows - R), (0, 0)))

    def body(c_ref, w_ref, b_ref, o_ref, ca_ref):
        cv = c_ref[...]
        ca16 = (cv * jax.nn.sigmoid(cv)).astype(BF16)
        ca_ref[...] = ca16
        o_ref[...] = jnp.dot(ca16, w_ref[...], preferred_element_type=F32) + b_ref[...]

    out, ca = pl.pallas_call(
        body, grid=(N // tn,),
        in_specs=[pl.BlockSpec((rows, D), lambda j: (0, 0)), pl.BlockSpec((D, tn), lambda j: (0, j)),
                  pl.BlockSpec((1, tn), lambda j: (0, j))],
        out_specs=[pl.BlockSpec((rows, tn), lambda j: (0, j)), pl.BlockSpec((rows, D), lambda j: (0, 0))],
        out_shape=[jax.ShapeDtypeStruct((rows, N), F32), jax.ShapeDtypeStruct((rows, D), BF16)],
        compiler_params=_params("arbitrary"), name=name)(c_pad, w, b)
    return out[0:R], ca


def _norm_mod_fwd(x, g, shift, scale, *, name, ts=512):
    S, D = x.shape
    ts = _tile(S, ts)
    row = pl.BlockSpec((ts, D), lambda i: (i, 0))
    vec = pl.BlockSpec((1, D), lambda i: (0, 0))

    def body(x_ref, g_ref, sh_ref, sc_ref, h_ref):
        xv = x_ref[...]
        r = lax.rsqrt(jnp.mean(xv * xv, axis=-1, keepdims=True) + EPS)
        h_ref[...] = ((xv * r * g_ref[...]) * (1.0 + sc_ref[...]) + sh_ref[...]).astype(BF16)

    return pl.pallas_call(body, grid=(S // ts,), in_specs=[row, vec, vec, vec], out_specs=row,
                          out_shape=jax.ShapeDtypeStruct((S, D), BF16),
                          compiler_params=_params("parallel"), name=name)(x, g, shift, scale)


def _acc_init(step, *refs):
    @pl.when(step == 0)
    def _():
        for r in refs:
            r[...] = jnp.zeros_like(r)


def _colsum(v):
    return jnp.sum(v, axis=0, keepdims=True)


def _norm_mod_bwd(dh, x, g, scale, dres, prev=None, *, name, ts=512):
    S, D = x.shape
    ts = _tile(S, ts)
    row = pl.BlockSpec((ts, D), lambda i: (i, 0))
    vec = pl.BlockSpec((1, D), lambda i: (0, 0))
    has_prev = prev is not None

    def body(*refs):
        if has_prev:
            dh_ref, x_ref, g_ref, sc_ref, dres_ref, y_ref, gate_ref, dx_ref, dg_ref, dsh_ref, dsc_ref, dy_ref, dgate_ref = refs
            _acc_init(pl.program_id(0), dg_ref, dsh_ref, dsc_ref, dgate_ref)
        else:
            dh_ref, x_ref, g_ref, sc_ref, dres_ref, dx_ref, dg_ref, dsh_ref, dsc_ref = refs
            _acc_init(pl.program_id(0), dg_ref, dsh_ref, dsc_ref)
        xv, dhv, gv = x_ref[...], dh_ref[...], g_ref[...]
        r = lax.rsqrt(jnp.mean(xv * xv, axis=-1, keepdims=True) + EPS)
        xh = xv * r
        dsh_ref[...] += _colsum(dhv)
        dsc_ref[...] += _colsum(dhv * (xh * gv))
        dn = dhv * (1.0 + sc_ref[...])
        dg_ref[...] += _colsum(dn * xh)
        dxh = dn * gv
        dx = dres_ref[...] + r * (dxh - xh * jnp.mean(dxh * xh, axis=-1, keepdims=True))
        dx_ref[...] = dx
        if has_prev:
            dy_ref[...] = (gate_ref[...] * dx).astype(BF16)
            dgate_ref[...] += _colsum(dx * y_ref[...].astype(F32))

    ins, in_specs = [dh, x, g, scale, dres], [row, row, vec, vec, row]
    outs = [jax.ShapeDtypeStruct((S, D), F32)] + [jax.ShapeDtypeStruct((1, D), F32)] * 3
    out_specs = [row, vec, vec, vec]
    if has_prev:
        ins += list(prev)
        in_specs += [row, vec]
        outs += [jax.ShapeDtypeStruct((S, D), BF16), jax.ShapeDtypeStruct((1, D), F32)]
        out_specs += [row, vec]
    return pl.pallas_call(body, grid=(S // ts,), in_specs=in_specs, out_specs=out_specs, out_shape=outs,
                          compiler_params=_params("arbitrary"), name=name)(*ins)


def _final_loss(x, g, target, y, gate, *, name, ts=512):
    S, D = x.shape
    ts = _tile(S, ts)
    row = pl.BlockSpec((ts, D), lambda i: (i, 0))
    vec = pl.BlockSpec((1, D), lambda i: (0, 0))
    lvec = pl.BlockSpec((1, LANES), lambda i: (0, 0))

    def body(x_ref, g_ref, t_ref, y_ref, gate_ref, loss_ref, dg_ref, dx_ref, dy_ref, dgate_ref):
        _acc_init(pl.program_id(0), loss_ref, dg_ref, dgate_ref)
        xv, gv = x_ref[...], g_ref[...]
        r = lax.rsqrt(jnp.mean(xv * xv, axis=-1, keepdims=True) + EPS)
        xh = xv * r
        e = xh * gv - t_ref[...]
        loss_ref[...] += 0.5 * jnp.sum(jnp.mean(e * e, axis=-1, keepdims=True), axis=0, keepdims=True)
        dout = e * (1.0 / D)
        dg_ref[...] += _colsum(dout * xh)
        dxh = dout * gv
        dx = r * (dxh - xh * jnp.mean(dxh * xh, axis=-1, keepdims=True))
        dx_ref[...] = dx
        dy_ref[...] = (gate_ref[...] * dx).astype(BF16)
        dgate_ref[...] += _colsum(dx * y_ref[...].astype(F32))

    return pl.pallas_call(
        body, grid=(S // ts,), in_specs=[row, vec, row, row, vec], out_specs=[lvec, vec, row, row, vec],
        out_shape=[jax.ShapeDtypeStruct((1, LANES), F32), jax.ShapeDtypeStruct((1, D), F32),
                   jax.ShapeDtypeStruct((S, D), F32), jax.ShapeDtypeStruct((S, D), BF16),
                   jax.ShapeDtypeStruct((1, D), F32)],
        compiler_params=_params("arbitrary"), name=name)(x, g, target, y, gate)


def _head_mean(v, gmat):
    hi = v.astype(BF16)
    lo = (v - hi.astype(F32)).astype(BF16)
    return jnp.dot(hi, gmat, preferred_element_type=F32) + jnp.dot(lo, gmat, preferred_element_type=F32)


L_F, L_ONE, L_SHIFT = HEAD_DIM, HEAD_DIM + 3, HEAD_DIM + 6
KEY_CHUNKS = 2
SHIFT_FREE_LOGIT_BOUND = 60.0


def _lane():
    return lax.broadcasted_iota(jnp.int32, (1, LANES), 1)


def _split3(v):
    p1 = v.astype(BF16).astype(F32)
    r1 = v - p1
    p2 = r1.astype(BF16).astype(F32)
    p3 = (r1 - p2).astype(BF16).astype(F32)
    return p1, p2, p3


def _put3(lane, first, pieces):
    out = jnp.where(lane == first, pieces[0], 0.0)
    for k in (1, 2):
        out = out + jnp.where(lane == first + k, pieces[k], 0.0)
    return out


def _ones3(lane, first):
    return jnp.where((lane >= first) & (lane < first + 3), 1.0, 0.0)


def _lane_col(v, lane, idx):
    return jnp.sum(jnp.where(lane == idx, v, 0.0), axis=-1, keepdims=True)


def _head_of_pair(pair, e, lane):
    return jnp.where(lane < HEAD_DIM, pair if e == 0 else pltpu.roll(pair, HEAD_DIM, 1), 0.0)


def _pair_of_heads(even, odd, lane):
    return jnp.where(lane < HEAD_DIM, even, pltpu.roll(odd, HEAD_DIM, 1))


def _fox_prep_fwd(proj, fcum, qgain, kgain, gmat, *, name, ts=256):
    S = proj.shape[0]
    D = HEADS * HEAD_DIM
    ts = _tile(S, ts)
    scale = HEAD_DIM ** -0.5

    def body(p_ref, f_ref, qg_ref, kg_ref, gm_ref, q_ref, k_ref, v_ref):
        gm, lane, fc = gm_ref[...], _lane(), f_ref[...]
        for cpair in range(HEAD_PAIRS):
            qv = p_ref[:, pl.ds(cpair * LANES, LANES)].astype(F32)
            kv = p_ref[:, pl.ds(D + cpair * LANES, LANES)].astype(F32)
            vv = p_ref[:, pl.ds(2 * D + cpair * LANES, LANES)].astype(F32)
            qn = (qv * lax.rsqrt(_head_mean(qv * qv, gm) + EPS) * qg_ref[...]) * scale
            kn = kv * lax.rsqrt(_head_mean(kv * kv, gm) + EPS) * kg_ref[...]
            for e in range(2):
                h = 2 * cpair + e
                f3 = _split3(_lane_col(fc, lane, h))
                q_ref[h] = (_head_of_pair(qn, e, lane) + _put3(lane, L_F, f3) + _ones3(lane, L_ONE)).astype(BF16)
                k_ref[h] = (_head_of_pair(kn, e, lane) + _ones3(lane, L_F)
                            - _put3(lane, L_ONE, f3) + _ones3(lane, L_SHIFT)).astype(BF16)
                v_ref[h] = (_head_of_pair(vv, e, lane) + _ones3(lane, L_F)).astype(BF16)

    vec = pl.BlockSpec((1, LANES), lambda i: (0, 0))
    wide = pl.BlockSpec((HEADS, ts, LANES), lambda i: (0, i, 0))
    return pl.pallas_call(
        body, grid=(S // ts,),
        in_specs=[pl.BlockSpec((ts, 3 * D), lambda i: (i, 0)), pl.BlockSpec((ts, LANES), lambda i: (i, 0)), vec, vec,
                  pl.BlockSpec((LANES, LANES), lambda i: (0, 0))],
        out_specs=[wide, wide, wide], out_shape=[jax.ShapeDtypeStruct((HEADS, S, LANES), BF16)] * 3,
        compiler_params=_params("parallel"), name=name)(proj, fcum, qgain, kgain, gmat)


def _fox_prep_bwd(proj, dq_aug, dk_aug, dv_aug, dog, qgain, kgain, gmat, *, name, ts=256):
    S = proj.shape[0]
    D = HEADS * HEAD_DIM
    ts = _tile(S, ts)
    scale = HEAD_DIM ** -0.5

    def body(p_ref, dq_ref, dk_ref, dv_ref, dog_ref, qg_ref, kg_ref, gm_ref, o_ref, df_ref, dqg_ref, dkg_ref):
        _acc_init(pl.program_id(0), dqg_ref, dkg_ref)
        gm, lane = gm_ref[...], _lane()
        df = jnp.zeros((ts, LANES), F32)
        for cpair in range(HEAD_PAIRS):
            tiles = []
            for e in range(2):
                h = 2 * cpair + e
                tq, tk = dq_ref[h], dk_ref[h]
                df = jnp.where(lane == h, _lane_col(tq, lane, L_F) - _lane_col(tk, lane, L_ONE), df)
                tiles.append((tq, tk, dv_ref[h].astype(F32)))
            pair = [_pair_of_heads(tiles[0][k], tiles[1][k], lane) for k in range(3)]
            for half, g_ref, dg_ref, mult in ((0, qg_ref, dqg_ref, scale), (1, kg_ref, dkg_ref, 1.0)):
                v = p_ref[:, pl.ds(half * D + cpair * LANES, LANES)].astype(F32)
                r = lax.rsqrt(_head_mean(v * v, gm) + EPS)
                xh = v * r
                dn = pair[half] * mult
                dg_ref[...] += _colsum(dn * xh)
                dxh = dn * g_ref[...]
                o_ref[:, pl.ds(half * D + cpair * LANES, LANES)] = (r * (dxh - xh * _head_mean(dxh * xh, gm))).astype(BF16)
            o_ref[:, pl.ds(2 * D + cpair * LANES, LANES)] = pair[2].astype(BF16)
        o_ref[:, pl.ds(3 * D, D)] = dog_ref[...]
        o_ref[:, pl.ds(4 * D, LANES)] = jnp.zeros((ts, LANES), BF16)
        df_ref[...] = df

    row = pl.BlockSpec((ts, D), lambda i: (i, 0))
    wide = pl.BlockSpec((HEADS, ts, LANES), lambda i: (0, i, 0))
    vec = pl.BlockSpec((1, LANES), lambda i: (0, 0))
    return pl.pallas_call(
        body, grid=(S // ts,),
        in_specs=[pl.BlockSpec((ts, 2 * D), lambda i: (i, 0)), wide, wide, wide, row, vec, vec,
                  pl.BlockSpec((LANES, LANES), lambda i: (0, 0))],
        out_specs=[pl.BlockSpec((ts, 4 * D + LANES), lambda i: (i, 0)), pl.BlockSpec((ts, LANES), lambda i: (i, 0)), vec, vec],
        out_shape=[jax.ShapeDtypeStruct((S, 4 * D + LANES), BF16), jax.ShapeDtypeStruct((S, LANES), F32),
                   jax.ShapeDtypeStruct((1, LANES), F32), jax.ShapeDtypeStruct((1, LANES), F32)],
        compiler_params=_params("arbitrary"), name=name)(proj, dq_aug, dk_aug, dv_aug, dog, qgain, kgain, gmat)


def _log_sigmoid(z):
    return jnp.minimum(z, 0.0) - jnp.log(1.0 + jnp.exp(-jnp.abs(z)))


def _fox_decay_fwd(fl, bf, *, name, tb=256):
    S = fl.shape[0]
    tb = _tile(S, tb)

    def body(fl_ref, b_ref, o_ref, carry):
        @pl.when(pl.program_id(0) == 0)
        def _():
            carry[...] = jnp.zeros_like(carry)
        logf = _log_sigmoid(fl_ref[...] + b_ref[...])
        tri = (lax.broadcasted_iota(jnp.int32, (tb, tb), 1) <= lax.broadcasted_iota(jnp.int32, (tb, tb), 0)).astype(F32)
        cs = jnp.dot(tri, logf, preferred_element_type=F32, precision=lax.Precision.HIGHEST) + carry[...]
        o_ref[...] = cs
        carry[...] = _row_of(cs, tb - 1)

    return pl.pallas_call(
        body, grid=(S // tb,),
        in_specs=[pl.BlockSpec((tb, LANES), lambda i: (i, 0)), pl.BlockSpec((1, LANES), lambda i: (0, 0))],
        out_specs=pl.BlockSpec((tb, LANES), lambda i: (i, 0)),
        out_shape=jax.ShapeDtypeStruct((S, LANES), F32), scratch_shapes=[pltpu.VMEM((1, LANES), F32)],
        compiler_params=_params("arbitrary"), name=name)(fl, bf)


def _fox_decay_bwd(dF, fl, bf, dproj, *, name, tb=256):
    S = fl.shape[0]
    tb = _tile(S, tb)
    n = S // tb
    last_col = dproj.shape[1] // LANES - 1

    def body(df_ref, fl_ref, b_ref, dproj_hbm, o_ref, db_ref, carry):
        del dproj_hbm
        @pl.when(pl.program_id(0) == 0)
        def _():
            carry[...] = jnp.zeros_like(carry)
            db_ref[...] = jnp.zeros_like(db_ref)
        tri = (lax.broadcasted_iota(jnp.int32, (tb, tb), 1) >= lax.broadcasted_iota(jnp.int32, (tb, tb), 0)).astype(F32)
        rc = jnp.dot(tri, df_ref[...], preferred_element_type=F32, precision=lax.Precision.HIGHEST) + carry[...]
        carry[...] = _row_of(rc, 0)
        dfl = rc * jax.nn.sigmoid(-(fl_ref[...] + b_ref[...]))
        o_ref[...] = dfl.astype(BF16)
        db_ref[...] += _colsum(dfl)

    rev = pl.BlockSpec((tb, LANES), lambda i: (n - 1 - i, 0))
    vec = pl.BlockSpec((1, LANES), lambda i: (0, 0))
    return pl.pallas_call(
        body, grid=(n,), in_specs=[rev, rev, vec, pl.BlockSpec(memory_space=pl.ANY)],
        out_specs=[pl.BlockSpec((tb, LANES), lambda i: (n - 1 - i, last_col)), vec],
        out_shape=[jax.ShapeDtypeStruct(dproj.shape, BF16), jax.ShapeDtypeStruct((1, LANES), F32)],
        scratch_shapes=[pltpu.VMEM((1, LANES), F32)], input_output_aliases={3: 0},
        compiler_params=_params("arbitrary"), name=name)(dF, fl, bf, dproj)


_NT = (((1,), (1,)), ((), ()))
_TN = (((0,), (0,)), ((), ()))


def _causal(T, transposed=False):
    r, c = lax.broadcasted_iota(jnp.int32, (T, T), 0), lax.broadcasted_iota(jnp.int32, (T, T), 1)
    return r <= c if transposed else c <= r


def _with_shift(q_tile, shift, lane):
    keep = jnp.where((lane >= L_SHIFT) & (lane < L_SHIFT + 3), 0.0, q_tile)
    return (keep + _put3(lane, L_SHIFT, _split3(-shift))).astype(BF16)


def _ride_along(xchg, n_in, n_out, grid):
    if xchg is None:
        return (lambda body: body), [], [], [], [], []
    arrs, scatter = xchg
    n = len(arrs)

    def wrap(body):
        def wrapped(*refs):
            own_in, x_in = refs[:n_in], refs[n_in:n_in + n]
            own_out, x_out = refs[n_in + n:n_in + n + n_out], refs[n_in + n + n_out:n_in + 2 * n + n_out]
            rest = refs[n_in + 2 * n + n_out:]
            own_scratch, sems = rest[:len(rest) - 3], rest[len(rest) - 3:]
            ids = [pl.program_id(d) for d in range(len(grid))]
            first = functools.reduce(jnp.logical_and, [i == 0 for i in ids])
            last = functools.reduce(jnp.logical_and, [i == g - 1 for i, g in zip(ids, grid)])

            @pl.when(first)
            def _():
                for cp in _xchg_copies(x_in, x_out, scatter, *sems):
                    cp.start()

            body(*own_in, *own_out, *own_scratch)

            @pl.when(last)
            def _():
                for cp in _xchg_copies(x_in, x_out, scatter, *sems):
                    cp.wait()

        return wrapped

    return wrap, [_HBM] * n, [_HBM] * n, _xchg_out_shapes(arrs, scatter), _xchg_sems(n), list(arrs)


def _attn_rowmax(q_aug, k_aug, *, name, T=1024):
    S = q_aug.shape[1]
    T = _tile(S, T)
    n = S // T

    def body(q_ref, k_ref, o_ref, m_s):
        i, j = pl.program_id(1), pl.program_id(2)

        @pl.when(j == 0)
        def _():
            m_s[...] = jnp.full_like(m_s, NEG)

        def step(diag):
            s = lax.dot_general(q_ref[...], k_ref[...], _NT, preferred_element_type=F32)
            if diag:
                s = jnp.where(_causal(T), s, NEG)
            m = m_s[...]
            for cb in range(T // LANES):
                m = jnp.maximum(m, s[:, cb * LANES:(cb + 1) * LANES])
            m_s[...] = m

        @pl.when(j < i)
        def _():
            step(False)

        @pl.when(j == i)
        def _():
            step(True)
            o_ref[...] = _with_shift(q_ref[...].astype(F32), jnp.max(m_s[...], axis=-1, keepdims=True), _lane())

    qrow = pl.BlockSpec((None, T, LANES), lambda h, i, j: (h, i, 0))
    return pl.pallas_call(
        body, grid=(HEADS, n, n),
        in_specs=[qrow, pl.BlockSpec((None, T, LANES), lambda h, i, j: (h, jnp.minimum(j, i), 0))],
        out_specs=qrow, out_shape=jax.ShapeDtypeStruct(q_aug.shape, BF16),
        scratch_shapes=[pltpu.VMEM((T, LANES), F32)],
        compiler_params=_params("parallel", "parallel", "arbitrary"), name=name)(q_aug, k_aug)


def _attn_fwd(q_max, k_aug, v_aug, xchg=None, *, name, T=1024):
    S = q_max.shape[1]
    T = _tile(S, T)
    n = S // T
    wrap, x_in, x_out, x_shapes, x_sems, x_ops = _ride_along(xchg, 3, 2, (HEADS, n, n))

    def body(q_ref, k_ref, v_ref, o_ref, qb_ref, acc_s):
        i, j = pl.program_id(1), pl.program_id(2)

        @pl.when(j == 0)
        def _():
            acc_s[...] = jnp.zeros_like(acc_s)

        def block(rows, cols, mask):
            s = lax.dot_general(q_ref[rows, :], k_ref[cols, :], _NT, preferred_element_type=F32)
            if mask is not None:
                s = jnp.where(mask, s, NEG)
            return jnp.dot(jnp.exp(s).astype(BF16), v_ref[cols, :], preferred_element_type=F32)

        @pl.when(j < i)
        def _():
            chunk = T // KEY_CHUNKS
            upd = block(pl.ds(0, T), pl.ds(0, chunk), None)
            for c in range(1, KEY_CHUNKS):
                upd = upd + block(pl.ds(0, T), pl.ds(c * chunk, chunk), None)
            acc_s[...] += upd

        @pl.when(j == i)
        def _():
            half = T // 2
            lo, hi = pl.ds(0, half), pl.ds(half, half)
            acc_s[lo, :] += block(lo, lo, _causal(half))
            acc_s[hi, :] += block(hi, lo, None) + block(hi, hi, _causal(half))
            lane = _lane()
            acc = acc_s[...]
            l = _lane_col(acc, lane, L_F)
            o_ref[...] = acc / l
            qf = q_ref[...].astype(F32)
            row_max = -jnp.sum(jnp.where((lane >= L_SHIFT) & (lane < L_SHIFT + 3), qf, 0.0), axis=-1, keepdims=True)
            qb_ref[...] = _with_shift(qf, row_max + jnp.log(l), lane)

    qrow = pl.BlockSpec((None, T, LANES), lambda h, i, j: (h, i, 0))
    kv = pl.BlockSpec((None, T, LANES), lambda h, i, j: (h, jnp.minimum(j, i), 0))
    outs = pl.pallas_call(
        wrap(body), grid=(HEADS, n, n), in_specs=[qrow, kv, kv] + x_in, out_specs=[qrow, qrow] + x_out,
        out_shape=[jax.ShapeDtypeStruct(q_max.shape, F32), jax.ShapeDtypeStruct(q_max.shape, BF16)] + x_shapes,
        scratch_shapes=[pltpu.VMEM((T, LANES), F32)] + x_sems,
        compiler_params=_params("arbitrary", "arbitrary", "arbitrary"), name=name)(q_max, k_aug, v_aug, *x_ops)
    return outs[0], outs[1], outs[2:]


def _attn_bwd(q_lse, k_aug, v_aug, do_aug, xchg=None, *, name, T=1024):
    S = q_lse.shape[1]
    T = _tile(S, T)
    n = S // T
    wrap, x_in, x_out, x_shapes, x_sems, x_ops = _ride_along(xchg, 4, 3, (HEADS, n, n))

    def body(q_ref, do_ref, k_ref, v_ref, dq_ref, dk_ref, dv_ref, dq_s, dk_s, dv_s):
        j, i = pl.program_id(1), pl.program_id(2)

        def block(keys, queries, mask):
            q, do, k, v = q_ref[queries, :], do_ref[queries, :], k_ref[keys, :], v_ref[keys, :]
            st = lax.dot_general(k, q, _NT, preferred_element_type=F32)
            if mask is not None:
                st = jnp.where(mask, st, NEG)
            pt = jnp.exp(st)
            dst = (pt * lax.dot_general(v, do, _NT, preferred_element_type=F32)).astype(BF16)
            dv_s[keys, :] += jnp.dot(pt.astype(BF16), do, preferred_element_type=F32)
            dk_s[keys, :] += jnp.dot(dst, q, preferred_element_type=F32)
            return lax.dot_general(dst, k, _TN, preferred_element_type=F32)

        @pl.when(i == j)
        def _():
            dk_s[...] = jnp.zeros_like(dk_s)
            dv_s[...] = jnp.zeros_like(dv_s)

            @pl.when(j == 0)
            def _():
                dq_s[i] = jnp.zeros((T, LANES), F32)

            half = T // 2
            lo, hi = pl.ds(0, half), pl.ds(half, half)
            dq_s[i, lo, :] += block(lo, lo, _causal(half, transposed=True))
            dq_s[i, hi, :] += block(lo, hi, None) + block(hi, hi, _causal(half, transposed=True))
            dq_ref[...] = dq_s[j]

        @pl.when(i > j)
        def _():
            chunk = T // KEY_CHUNKS
            upd = block(pl.ds(0, chunk), pl.ds(0, T), None)
            for c in range(1, KEY_CHUNKS):
                upd = upd + block(pl.ds(c * chunk, chunk), pl.ds(0, T), None)

            @pl.when(j == 0)
            def _():
                dq_s[i] = upd

            @pl.when(j > 0)
            def _():
                dq_s[i] += upd

        @pl.when(i == n - 1)
        def _():
            dk_ref[...] = dk_s[...]
            dv_ref[...] = dv_s[...].astype(BF16)

    qrow = pl.BlockSpec((None, T, LANES), lambda h, j, i: (h, jnp.maximum(i, j), 0))
    kv = pl.BlockSpec((None, T, LANES), lambda h, j, i: (h, j, 0))
    outs = pl.pallas_call(
        wrap(body), grid=(HEADS, n, n), in_specs=[qrow, qrow, kv, kv] + x_in, out_specs=[kv, kv, kv] + x_out,
        out_shape=[jax.ShapeDtypeStruct(q_lse.shape, F32), jax.ShapeDtypeStruct(q_lse.shape, F32),
                   jax.ShapeDtypeStruct(q_lse.shape, BF16)] + x_shapes,
        scratch_shapes=[pltpu.VMEM((n, T, LANES), F32), pltpu.VMEM((T, LANES), F32), pltpu.VMEM((T, LANES), F32)] + x_sems,
        compiler_params=_params("arbitrary", "arbitrary", "arbitrary"), name=name)(q_lse, do_aug, k_aug, v_aug, *x_ops)
    return outs[0], outs[1], outs[2], outs[3:]


def _fox_gate_fwd(att_aug, proj, *, name, ts=256):
    S = att_aug.shape[1]
    D = HEADS * HEAD_DIM
    ts = _tile(S, ts)

    def body(a_ref, o_ref, att_ref, out_ref):
        lane = _lane()
        for cpair in range(HEAD_PAIRS):
            cols = pl.ds(cpair * LANES, LANES)
            pair = _pair_of_heads(a_ref[2 * cpair], a_ref[2 * cpair + 1], lane)
            att_ref[:, cols] = pair
            out_ref[:, cols] = (pair * jax.nn.sigmoid(o_ref[:, cols].astype(F32))).astype(BF16)

    row = pl.BlockSpec((ts, D), lambda i: (i, 0))
    return pl.pallas_call(
        body, grid=(S // ts,),
        in_specs=[pl.BlockSpec((HEADS, ts, LANES), lambda i: (0, i, 0)), pl.BlockSpec((ts, D), lambda i: (i, 3))],
        out_specs=[row, row], out_shape=[jax.ShapeDtypeStruct((S, D), F32), jax.ShapeDtypeStruct((S, D), BF16)],
        compiler_params=_params("parallel"), name=name)(att_aug, proj)


def _fox_gate_bwd(da, att, proj, *, name, ts=256):
    S, D = att.shape
    ts = _tile(S, ts)

    def body(da_ref, a_ref, o_ref, do_ref, dog_ref):
        lane = _lane()
        for cpair in range(HEAD_PAIRS):
            cols = pl.ds(cpair * LANES, LANES)
            dav, av = da_ref[:, cols].astype(F32), a_ref[:, cols]
            sg = jax.nn.sigmoid(o_ref[:, cols].astype(F32))
            datt = (dav * sg).astype(BF16).astype(F32)
            dog_ref[:, cols] = (dav * av * sg * (1.0 - sg)).astype(BF16)
            prod = datt * av
            for e in range(2):
                in_head = (lane < HEAD_DIM) if e == 0 else (lane >= HEAD_DIM)
                delta = jnp.sum(jnp.where(in_head, prod, 0.0), axis=-1, keepdims=True)
                tile = _head_of_pair(datt, e, lane) + _put3(lane, L_F, _split3(-delta))
                do_ref[2 * cpair + e] = tile.astype(BF16)

    row = pl.BlockSpec((ts, D), lambda i: (i, 0))
    return pl.pallas_call(
        body, grid=(S // ts,), in_specs=[row, row, pl.BlockSpec((ts, D), lambda i: (i, 3))],
        out_specs=[pl.BlockSpec((HEADS, ts, LANES), lambda i: (0, i, 0)), row],
        out_shape=[jax.ShapeDtypeStruct((HEADS, S, LANES), BF16), jax.ShapeDtypeStruct((S, D), BF16)],
        compiler_params=_params("parallel"), name=name)(da, att, proj)


def _row_of(block, r):
    rows = lax.broadcasted_iota(jnp.int32, block.shape, 0)
    return jnp.sum(jnp.where(rows == r, block, 0.0), axis=0, keepdims=True)


def _shift_down(cur, tail, k):
    out = pltpu.roll(cur, k, 0)
    top = out[:SUBLANES]
    rows = lax.broadcasted_iota(jnp.int32, top.shape, 0)
    for r in range(k):
        top = jnp.where(rows == r, _row_of(tail, tail.shape[0] - k + r), top)
    return jnp.concatenate([top, out[SUBLANES:]], axis=0)


def _shift_up(cur, head, k):
    n = cur.shape[0]
    out = pltpu.roll(cur, n - k, 0)
    bottom = out[n - SUBLANES:]
    rows = lax.broadcasted_iota(jnp.int32, bottom.shape, 0)
    for r in range(k):
        bottom = jnp.where(rows == SUBLANES - k + r, _row_of(head, r), bottom)
    return jnp.concatenate([out[:n - SUBLANES], bottom], axis=0)


HALO = 16


CONV_TC = 1408


def _pair_tiles(v):
    nc = v.shape[-1] // (2 * CONV_TC)
    return jnp.swapaxes(v.reshape(v.shape[:-1] + (2, nc, CONV_TC)), -3, -2).reshape(v.shape)


def _unpair_tiles(v):
    nc = v.shape[-1] // (2 * CONV_TC)
    return jnp.swapaxes(v.reshape(v.shape[:-1] + (nc, 2, CONV_TC)), -3, -2).reshape(v.shape)


def _conv_rows(cur, tail, w_ref, b_ref, cols):
    a1, a2 = _shift_down(cur, tail, 1), _shift_down(cur, tail, 2)
    return a2 * w_ref[0:1, cols] + a1 * w_ref[1:2, cols] + cur * w_ref[2:3, cols] + b_ref[:, cols], (a2, a1, cur)


def _conv_gate_fwd(a, cw, cb, *, name, ts=512):
    S, F2 = a.shape
    tc = CONV_TC
    ts = _tile(S, ts)
    nc = F2 // (2 * tc)
    sub = ts // HALO
    halves = (pl.ds(0, tc), pl.ds(tc, tc))

    def body(a_ref, t_ref, w_ref, b_ref, o_ref):
        first = pl.program_id(1) == 0
        pre = []
        for cols in halves:
            tail = jnp.where(first, 0.0, t_ref[:, cols].astype(F32))
            pre.append(_conv_rows(a_ref[:, cols].astype(F32), tail, w_ref, b_ref, cols)[0])
        g, val = pre
        o_ref[...] = (g * jax.nn.sigmoid(g) * val).astype(BF16)

    return pl.pallas_call(
        body, grid=(nc, S // ts),
        in_specs=[pl.BlockSpec((ts, 2 * tc), lambda j, i: (i, j)),
                  pl.BlockSpec((HALO, 2 * tc), lambda j, i: (jnp.maximum(i * sub - 1, 0), j)),
                  pl.BlockSpec((CONV_WIDTH, 2 * tc), lambda j, i: (0, j)), pl.BlockSpec((1, 2 * tc), lambda j, i: (0, j))],
        out_specs=pl.BlockSpec((ts, tc), lambda j, i: (i, j)),
        out_shape=jax.ShapeDtypeStruct((S, F2 // 2), BF16),
        compiler_params=_params("parallel", "parallel"), name=name)(a, a, cw, cb)


def _conv_gate_bwd(a, dact, cw, cb, *, name, ts=512):
    S, F2 = a.shape
    tc = CONV_TC
    ts = _tile(S, ts)
    nc = F2 // (2 * tc)
    sub = ts // HALO
    n_rows = S // ts
    halves = (pl.ds(0, tc), pl.ds(tc, tc))

    def body(a_ref, at_ref, ah_ref, d_ref, dh_ref, w_ref, b_ref, da_ref, s_ref):
        i = pl.program_id(1)
        _acc_init(i, s_ref)

        def dpre_of(rows, tails, d):
            (g, taps_g), (val, taps_v) = [_conv_rows(rows[h], tails[h], w_ref, b_ref, halves[h]) for h in range(2)]
            sg = jax.nn.sigmoid(g)
            return (d * val * (sg * (1.0 + g * (1.0 - sg))), d * (g * sg)), (taps_g, taps_v)

        cur = [a_ref[:, c].astype(F32) for c in halves]
        tail = [jnp.where(i == 0, 0.0, at_ref[:, c].astype(F32)) for c in halves]
        dpre, taps = dpre_of(cur, tail, d_ref[...].astype(F32))
        head, _ = dpre_of([ah_ref[:, c].astype(F32) for c in halves], [x[ts - HALO:, :] for x in cur], dh_ref[...].astype(F32))
        for h, cols in enumerate(halves):
            dd = dpre[h]
            nxt = jnp.where(i == n_rows - 1, 0.0, head[h])
            da_ref[:, cols] = (dd * w_ref[2:3, cols] + _shift_up(dd, nxt, 1) * w_ref[1:2, cols]
                               + _shift_up(dd, nxt, 2) * w_ref[0:1, cols]).astype(BF16)
            for r in range(CONV_WIDTH):
                s_ref[r:r + 1, cols] += _colsum(dd * taps[h][r])
            s_ref[CONV_WIDTH:CONV_WIDTH + 1, cols] += _colsum(dd)

    nxt_rows = lambda i: jnp.minimum((i + 1) * sub, S // HALO - 1)
    return pl.pallas_call(
        body, grid=(nc, n_rows),
        in_specs=[pl.BlockSpec((ts, 2 * tc), lambda j, i: (i, j)),
                  pl.BlockSpec((HALO, 2 * tc), lambda j, i: (jnp.maximum(i * sub - 1, 0), j)),
                  pl.BlockSpec((HALO, 2 * tc), lambda j, i: (nxt_rows(i), j)),
                  pl.BlockSpec((ts, tc), lambda j, i: (i, j)), pl.BlockSpec((HALO, tc), lambda j, i: (nxt_rows(i), j)),
                  pl.BlockSpec((CONV_WIDTH, 2 * tc), lambda j, i: (0, j)), pl.BlockSpec((1, 2 * tc), lambda j, i: (0, j))],
        out_specs=[pl.BlockSpec((ts, 2 * tc), lambda j, i: (i, j)), pl.BlockSpec((8, 2 * tc), lambda j, i: (0, j))],
        out_shape=[jax.ShapeDtypeStruct((S, F2), BF16), jax.ShapeDtypeStruct((8, F2), F32)],
        compiler_params=_params("parallel", "arbitrary"), name=name)(a, a, a, dact, dact, cw, cb)


def _gelu_parts(z):
    z2 = z * z
    t = jnp.tanh(GELU_C0 * (z + GELU_C1 * z * z2))
    val = 0.5 * z * (1.0 + t)
    grad = 0.5 * (1.0 + t) + 0.5 * z * (1.0 - t * t) * GELU_C0 * (1.0 + 3.0 * GELU_C1 * z2)
    return val, grad


def _sgu_fwd(pre, b_in, vgain, vbias, wm, bsb, *, name, ts=256):
    S, W2 = pre.shape
    W = W2 // 2
    gd = W // SGU_GROUPS
    ts = _tile(S, ts)

    def body(p_ref, b_ref, vg_ref, vb_ref, wm_ref, bs_ref, y_ref):
        u = _gelu_parts(p_ref[:, pl.ds(0, W)].astype(F32) + b_ref[:, pl.ds(0, W)])[0]
        v = _gelu_parts(p_ref[:, pl.ds(W, W)].astype(F32) + b_ref[:, pl.ds(W, W)])[0]
        mu = jnp.mean(v, axis=-1, keepdims=True)
        vc = v - mu
        rstd = lax.rsqrt(jnp.mean(vc * vc, axis=-1, keepdims=True) + EPS)
        vn = ((vc * rstd) * vg_ref[...] + vb_ref[...]).astype(BF16)
        for blk in range(ts // SGU_BLOCK):
            r0 = blk * SGU_BLOCK
            for g in range(SGU_GROUPS):
                c0 = g * gd
                mixed = jnp.dot(wm_ref[g], vn[r0:r0 + SGU_BLOCK, c0:c0 + gd], preferred_element_type=F32) + bs_ref[g]
                y_ref[pl.ds(r0, SGU_BLOCK), pl.ds(c0, gd)] = (u[r0:r0 + SGU_BLOCK, c0:c0 + gd] * mixed).astype(BF16)

    full = lambda shape: pl.BlockSpec(shape, lambda i: (0,) * len(shape))
    return pl.pallas_call(
        body, grid=(S // ts,),
        in_specs=[pl.BlockSpec((ts, W2), lambda i: (i, 0)), full((1, W2)), full((1, W)), full((1, W)),
                  full((SGU_GROUPS, SGU_BLOCK, SGU_BLOCK)), full((SGU_GROUPS, SGU_BLOCK, gd))],
        out_specs=pl.BlockSpec((ts, W), lambda i: (i, 0)), out_shape=jax.ShapeDtypeStruct((S, W), BF16),
        compiler_params=_params("parallel"), name=name)(pre, b_in, vgain, vbias, wm, bsb)


def _sgu_bwd(pre, dy, b_in, vgain, vbias, wm, wmt, bsb, *, name, ts=256):
    S, W2 = pre.shape
    W = W2 // 2
    gd = W // SGU_GROUPS
    ts = _tile(S, ts)
    last = S // ts - 1

    def body(p_ref, dy_ref, b_ref, vg_ref, vb_ref, wm_ref, wmt_ref, bs_ref,
             dp_ref, db_ref, dvg_ref, dvb_ref, dws_ref, dbs_ref, du_s, dvn_s, dbs_s):
        step = pl.program_id(0)
        _acc_init(step, db_ref, dvg_ref, dvb_ref, dws_ref, dbs_s)
        u, gu = _gelu_parts(p_ref[:, pl.ds(0, W)].astype(F32) + b_ref[:, pl.ds(0, W)])
        v, gv = _gelu_parts(p_ref[:, pl.ds(W, W)].astype(F32) + b_ref[:, pl.ds(W, W)])
        mu = jnp.mean(v, axis=-1, keepdims=True)
        vc = v - mu
        rstd = lax.rsqrt(jnp.mean(vc * vc, axis=-1, keepdims=True) + EPS)
        vhat = vc * rstd
        vn = (vhat * vg_ref[...] + vb_ref[...]).astype(BF16)
        dyv = dy_ref[...].astype(F32)
        for blk in range(ts // SGU_BLOCK):
            r0 = blk * SGU_BLOCK
            for g in range(SGU_GROUPS):
                c0 = g * gd
                vn_g = vn[r0:r0 + SGU_BLOCK, c0:c0 + gd]
                dy_g = dyv[r0:r0 + SGU_BLOCK, c0:c0 + gd]
                mixed = jnp.dot(wm_ref[g], vn_g, preferred_element_type=F32) + bs_ref[g]
                dmix = dy_g * u[r0:r0 + SGU_BLOCK, c0:c0 + gd]
                dmix_b = dmix.astype(BF16)
                du_s[pl.ds(r0, SGU_BLOCK), pl.ds(c0, gd)] = dy_g * mixed
                dvn_s[pl.ds(r0, SGU_BLOCK), pl.ds(c0, gd)] = jnp.dot(wmt_ref[g], dmix_b, preferred_element_type=F32)
                dws_ref[g] += lax.dot_general(dmix_b, vn_g, _NT, preferred_element_type=F32)
                dbs_s[g] += dmix
        dvn = dvn_s[...]
        dvg_ref[...] += _colsum(dvn * vhat)
        dvb_ref[...] += _colsum(dvn)
        dvh = dvn * vg_ref[...]
        dv = rstd * (dvh - jnp.mean(dvh, axis=-1, keepdims=True) - vhat * jnp.mean(dvh * vhat, axis=-1, keepdims=True))
        dpu = du_s[...] * gu
        dpv = dv * gv
        dp_ref[:, pl.ds(0, W)] = dpu.astype(BF16)
        dp_ref[:, pl.ds(W, W)] = dpv.astype(BF16)
        db_ref[:, pl.ds(0, W)] += _colsum(dpu)
        db_ref[:, pl.ds(W, W)] += _colsum(dpv)

        @pl.when(step == last)
        def _():
            for g in range(SGU_GROUPS):
                dbs_ref[g] = jnp.broadcast_to(jnp.sum(dbs_s[g], axis=-1, keepdims=True), (SGU_BLOCK, SGU_BLOCK))

    full = lambda shape: pl.BlockSpec(shape, lambda i: (0,) * len(shape))
    gsq = (SGU_GROUPS, SGU_BLOCK, SGU_BLOCK)
    return pl.pallas_call(
        body, grid=(S // ts,),
        in_specs=[pl.BlockSpec((ts, W2), lambda i: (i, 0)), pl.BlockSpec((ts, W), lambda i: (i, 0)),
                  full((1, W2)), full((1, W)), full((1, W)), full(gsq), full(gsq), full((SGU_GROUPS, SGU_BLOCK, gd))],
        out_specs=[pl.BlockSpec((ts, W2), lambda i: (i, 0)), full((1, W2)), full((1, W)), full((1, W)), full(gsq), full(gsq)],
        out_shape=[jax.ShapeDtypeStruct((S, W2), BF16), jax.ShapeDtypeStruct((1, W2), F32),
                   jax.ShapeDtypeStruct((1, W), F32), jax.ShapeDtypeStruct((1, W), F32),
                   jax.ShapeDtypeStruct(gsq, F32), jax.ShapeDtypeStruct(gsq, F32)],
        scratch_shapes=[pltpu.VMEM((ts, W), F32), pltpu.VMEM((ts, W), F32), pltpu.VMEM((SGU_GROUPS, SGU_BLOCK, gd), F32)],
        compiler_params=_params("arbitrary"), name=name)(pre, dy, b_in, vgain, vbias, wm, wmt, bsb)


def _paired_to_natural(w_up):
    nc = w_up.shape[1] // (2 * CONV_TC)
    return lambda q: (q % 2) * nc + q // 2


def _ffn_fwd(x, mods, n2g, w_up, cw, cb, w_down, tag):
    sh, sc, gate = mods
    h = _norm_mod_fwd(x, n2g, sh, sc, name=f"{tag}_norm_fwd")
    a = _mm(h, w_up, out_dtype=BF16, tn=CONV_TC, b_n=_paired_to_natural(w_up), name=f"{tag}_up")
    act = _conv_gate_fwd(a, cw, cb, name=f"{tag}_conv_fwd")
    x_out, y = _mm(act, w_down, tk=1408, res=(x, gate), name=f"{tag}_down")
    return x_out, (x, h, a, act, y)


def _ffn_bwd(dy, saved, mods, n2g, w_up, cw, cb, w_down, dres, prev, tag):
    x, h, a, act, _ = saved
    sh, sc, gate = mods
    dact = _mm(dy, w_down, tb=True, out_dtype=BF16, tn=1408, name=f"{tag}_down_dx")
    dw_down = _mm(act, dy, ta=True, out_dtype=BF16, tm=1408, name=f"{tag}_down_dw")
    da, sums = _conv_gate_bwd(a, dact, cw, cb, name=f"{tag}_conv_bwd")
    dh = _mm(da, w_up, tb=True, tk=CONV_TC, b_k=_paired_to_natural(w_up), name=f"{tag}_up_dx")
    dw_up = _mm(h, da, ta=True, out_dtype=BF16, tn=CONV_TC, o_n=_paired_to_natural(w_up), name=f"{tag}_up_dw")
    outs = _norm_mod_bwd(dh, x, n2g, sc, dres, prev, name=f"{tag}_norm_bwd")
    sums = _unpair_tiles(sums)
    return outs, dict(w_up=dw_up, w_down=dw_down, conv_w=sums[0:CONV_WIDTH], conv_b=sums[CONV_WIDTH])


def _local_step(x, target, w, mods, late=None, early=None, last=None):
    S, D = x.shape
    lane = jnp.arange(LANES)
    gmat = jnp.where((lane[:, None] // HEAD_DIM) == (lane[None, :] // HEAD_DIM), 1.0 / HEAD_DIM, 0.0).astype(BF16)
    qg2 = jnp.tile(w["fox_q_gain"].reshape(1, HEAD_DIM), (1, 2))
    kg2 = jnp.tile(w["fox_k_gain"].reshape(1, HEAD_DIM), (1, 2))
    bf_pad = jnp.pad(w["fox_b_f"].reshape(1, HEADS), ((0, 0), (0, LANES - HEADS)))
    w_in_pad = jnp.pad(w["fox_w_in"], ((0, 0), (0, 4 * D + LANES - w["fox_w_in"].shape[1])))
    w_qkvo, w_f = w_in_pad[:, :4 * D], w_in_pad[:, 4 * D:]
    tpos = jnp.arange(SGU_BLOCK)
    smask = (tpos[None, :] // SGU_CHUNK) <= (tpos[:, None] // SGU_CHUNK)
    wm32 = jnp.where(smask[None], w["sgu_w_s"], 0.0)
    wm, wmt = wm32.astype(BF16), jnp.swapaxes(wm32, 1, 2).astype(BF16)
    gd = w["sgu_v_gain"].shape[-1] // SGU_GROUPS
    bsb = jnp.broadcast_to(w["sgu_b_s"][:, :, None], (SGU_GROUPS, SGU_BLOCK, gd))
    vec = lambda v: v.reshape(1, -1)

    sh1, sc1, g1 = mods[0][0:3]
    h0 = _norm_mod_fwd(x, vec(w["norm1_g"][0]), sh1, sc1, name="fox_norm_fwd")
    proj = _mm(h0, w_qkvo, out_dtype=BF16, name="fox_proj")
    fl = _mm(h0, w_f, name="fox_forget_proj")
    fcum = _fox_decay_fwd(fl, bf_pad, name="fox_decay")
    q_aug, k_aug, v_aug = _fox_prep_fwd(proj, fcum, qg2, kg2, gmat, name="fox_qk_norm")
    logit_bound = 8.0 * jnp.max(jnp.abs(w["fox_q_gain"])) * jnp.max(jnp.abs(w["fox_k_gain"]))
    q_max = lax.cond(logit_bound <= SHIFT_FREE_LOGIT_BOUND, lambda: q_aug,
                     lambda: _attn_rowmax(q_aug, k_aug, name="fox_attn_rowmax"))
    xchg = None if late is None else (late[0], [False] * len(late[0]))
    att_aug, q_lse, gathered = _attn_fwd(q_max, k_aug, v_aug, xchg, name="fox_attn_fwd", T=2048)
    if late is not None:
        w = {**w, **late[1](gathered)}
    w = dict(w, ffn_conv_w=_pair_tiles(w["ffn_conv_w"]), ffn_conv_b=_pair_tiles(w["ffn_conv_b"]))
    att, ag = _fox_gate_fwd(att_aug, proj, name="fox_gate_fwd")
    x1, y_fox = _mm(ag, w["fox_w_out"], res=(x, g1), name="fox_out")
    x2, ffn0 = _ffn_fwd(x1, mods[0][3:6], vec(w["norm2_g"][0]), w["ffn_w_up"][0], w["ffn_conv_w"][0],
                        vec(w["ffn_conv_b"][0]), w["ffn_w_down"][0], "ffn0")

    sh1b, sc1b, g1b = mods[1][0:3]
    h1 = _norm_mod_fwd(x2, vec(w["norm1_g"][1]), sh1b, sc1b, name="sgu_norm_fwd")
    pre = _mm(h1, w["sgu_w_in"], out_dtype=BF16, name="sgu_in")
    b_in, vg, vb = vec(w["sgu_b_in"]), vec(w["sgu_v_gain"]), vec(w["sgu_v_bias"])
    ys = _sgu_fwd(pre, b_in, vg, vb, wm, bsb, name="sgu_core_fwd")
    x3, y_sgu = _mm(ys, w["sgu_w_out"], res=(x2, g1b), name="sgu_out")
    x4, ffn1 = _ffn_fwd(x3, mods[1][3:6], vec(w["norm2_g"][1]), w["ffn_w_up"][1], w["ffn_conv_w"][1],
                        vec(w["ffn_conv_b"][1]), w["ffn_w_down"][1], "ffn1")

    loss, d_final_g, dx4, dy_ffn1, dgate_ffn1 = _final_loss(x4, vec(w["final_g"]), target, ffn1[4], mods[1][5], name="final_loss")

    (dx3, dn2g_1, dsh2_1, dsc2_1, dy_sgu, dgate_sgu), g_ffn1 = _ffn_bwd(
        dy_ffn1, ffn1, mods[1][3:6], vec(w["norm2_g"][1]), w["ffn_w_up"][1], w["ffn_conv_w"][1], vec(w["ffn_conv_b"][1]),
        w["ffn_w_down"][1], dx4, (y_sgu, g1b), "ffn1")

    dys = _mm(dy_sgu, w["sgu_w_out"], tb=True, out_dtype=BF16, name="sgu_out_dx")
    dw_sgu_out = _mm(ys, dy_sgu, ta=True, out_dtype=BF16, name="sgu_out_dw")
    dpre, db_in, dvg, dvb, dws, dbs = _sgu_bwd(pre, dys, b_in, vg, vb, wm, wmt, bsb, name="sgu_core_bwd")
    dh1 = _mm(dpre, w["sgu_w_in"], tb=True, name="sgu_in_dx")
    dw_sgu_in = _mm(h1, dpre, ta=True, out_dtype=BF16, name="sgu_in_dw")
    dx2, dn1g_1, dsh1_1, dsc1_1, dy_ffn0, dgate_ffn0 = _norm_mod_bwd(
        dh1, x2, vec(w["norm1_g"][1]), sc1b, dx3, (ffn0[4], mods[0][5]), name="sgu_norm_bwd")

    (dx1, dn2g_0, dsh2_0, dsc2_0, dy_fox, dgate_fox), g_ffn0 = _ffn_bwd(
        dy_ffn0, ffn0, mods[0][3:6], vec(w["norm2_g"][0]), w["ffn_w_up"][0], w["ffn_conv_w"][0], vec(w["ffn_conv_b"][0]),
        w["ffn_w_down"][0], dx2, (y_fox, g1), "ffn0")

    dag = _mm(dy_fox, w["fox_w_out"], tb=True, out_dtype=BF16, name="fox_out_dx")
    dw_fox_out = _mm(ag, dy_fox, ta=True, out_dtype=BF16, name="fox_out_dw")
    do_aug, dog = _fox_gate_bwd(dag, att, proj, name="fox_gate_bwd")
    grads = dict(
        sgu_w_in=dw_sgu_in, sgu_b_in=db_in[0], sgu_v_gain=dvg[0], sgu_v_bias=dvb[0],
        sgu_w_s=jnp.where(smask[None], dws, 0.0), sgu_b_s=dbs[:, :, 0], sgu_w_out=dw_sgu_out,
        ffn_w_up=jnp.stack([g_ffn0["w_up"], g_ffn1["w_up"]]),
        ffn_conv_w=jnp.stack([g_ffn0["conv_w"], g_ffn1["conv_w"]]),
        ffn_conv_b=jnp.stack([g_ffn0["conv_b"], g_ffn1["conv_b"]]),
        ffn_w_down=jnp.stack([g_ffn0["w_down"], g_ffn1["w_down"]]),
        final_g=d_final_g[0], fox_w_out=dw_fox_out, norm2_g=jnp.concatenate([dn2g_0, dn2g_1], axis=0),
    )
    xchg = None if early is None else early(grads)
    dq_aug, dk_aug, dv_aug, exchanged = _attn_bwd(q_lse, k_aug, v_aug, do_aug, xchg, name="fox_attn_bwd")
    dproj, dF, dqg, dkg = _fox_prep_bwd(proj, dq_aug, dk_aug, dv_aug, dog, qg2, kg2, gmat, name="fox_qk_norm_bwd")
    dproj, dbf = _fox_decay_bwd(dF, fl, bf_pad, dproj, name="fox_decay_bwd")
    dw_fox_in = _mm(h0, dproj, ta=True, out_dtype=BF16, tn=1408, name="fox_proj_dw")[:, :w["fox_w_in"].shape[1]]
    xchg = None if last is None else last(dict(fox_w_in=dw_fox_in))
    dh0 = _mm(dproj, w_in_pad, tb=True, tk=1408, xchg=xchg, name="fox_proj_dx")
    dh0, exchanged_last = dh0 if last is not None else (dh0, [])
    dx0, dn1g_0, dsh1_0, dsc1_0 = _norm_mod_bwd(dh0, x, vec(w["norm1_g"][0]), sc1, dx1, None, name="fox_norm_bwd")

    dmod0 = jnp.concatenate([dsh1_0, dsc1_0, dgate_fox, dsh2_0, dsc2_0, dgate_ffn0], axis=1)
    dmod1 = jnp.concatenate([dsh1_1, dsc1_1, dgate_sgu, dsh2_1, dsc2_1, dgate_ffn1], axis=1)
    grads.update(
        fox_w_in=dw_fox_in,
        fox_b_f=dbf[0, :HEADS],
        fox_q_gain=dqg[0, :HEAD_DIM] + dqg[0, HEAD_DIM:],
        fox_k_gain=dkg[0, :HEAD_DIM] + dkg[0, HEAD_DIM:],
        fox_w_out=dw_fox_out,
        ada_b=jnp.concatenate([dmod0, dmod1], axis=0),
        norm1_g=jnp.concatenate([dn1g_0, dn1g_1], axis=0),
    )
    return loss[0, 0], dx0, grads, exchanged, exchanged_last


_HBM = pl.BlockSpec(memory_space=pl.ANY)
N_PEER = N_DEV - 1


def _xchg_out_shapes(arrs, scatter):
    return [jax.ShapeDtypeStruct(a.shape if s else (N_DEV,) + a.shape, a.dtype) for a, s in zip(arrs, scatter)]


def _xchg_sems(n):
    return [pltpu.SemaphoreType.DMA((n * N_PEER,)), pltpu.SemaphoreType.DMA((n * N_PEER,)), pltpu.SemaphoreType.DMA((n,))]


def _xchg_copies(ins, outs, scatter, send, recv, loc):
    x, y, c = lax.axis_index("x"), lax.axis_index("y"), lax.axis_index("c")
    me = 4 * x + 2 * y + c
    copies = []
    for a in range(len(ins)):
        copies.append(pltpu.make_async_copy(ins[a].at[me] if scatter[a] else ins[a], outs[a].at[me], loc.at[a]))
        for k in range(1, N_DEV):
            px = 1 - x if k & 4 else x
            py = 1 - y if k & 2 else y
            pc = 1 - c if k & 1 else c
            copies.append(pltpu.make_async_remote_copy(
                src_ref=ins[a].at[4 * px + 2 * py + pc] if scatter[a] else ins[a], dst_ref=outs[a].at[me],
                send_sem=send.at[a * N_PEER + k - 1], recv_sem=recv.at[a * N_PEER + k - 1],
                device_id=(px, py, pc), device_id_type=MESH))
    return copies


def _exchange(arrs, scatter, *, name):
    n = len(arrs)

    def body(*refs):
        copies = _xchg_copies(refs[:n], refs[n:2 * n], scatter, *refs[2 * n:])
        for cp in copies:
            cp.start()
        for cp in copies:
            cp.wait()

    return pl.pallas_call(
        body, in_specs=[_HBM] * n, out_specs=[_HBM] * n, out_shape=_xchg_out_shapes(arrs, scatter),
        scratch_shapes=_xchg_sems(n),
        compiler_params=pltpu.CompilerParams(has_side_effects=True), name=name)(*arrs)


def _adamw(w, parts, m, v, *, name, tr=256):
    L, R, C = w.shape
    P = parts.shape[0]
    tr = next(t for t in range(min(R, tr), 0, -1) if R % t == 0 and (t % 16 == 0 or t == R))
    nr = R // tr
    c1 = 1.0 - ADAM_B1 ** ADAM_STEP
    c2 = 1.0 - ADAM_B2 ** ADAM_STEP

    def body(w_ref, p_ref, m_ref, v_ref, g_ref, d_ref, mo_ref, vo_ref):
        g = p_ref[0].astype(F32)
        for p in range(1, P):
            g = g + p_ref[p].astype(F32)
        mn = ADAM_B1 * m_ref[0] + (1.0 - ADAM_B1) * g
        vn = ADAM_B2 * v_ref[0] + (1.0 - ADAM_B2) * (g * g)
        g_ref[0] = g
        mo_ref[0] = mn
        vo_ref[0] = vn
        d_ref[0] = -ADAM_LR * ((mn / c1) / (jnp.sqrt(vn / c2) + ADAM_EPS) + ADAM_WD * w_ref[0])

    row = pl.BlockSpec((1, tr, C), lambda l, i: (l, i, 0))
    return pl.pallas_call(
        body, grid=(L, nr), in_specs=[row, pl.BlockSpec((P, tr, C), lambda l, i: (0, l * nr + i, 0)), row, row],
        out_specs=[row] * 4, out_shape=[jax.ShapeDtypeStruct((L, R, C), F32)] * 4,
        compiler_params=_params("parallel", "parallel"), name=name)(w, parts, m, v)


def _sum_parts(parts, *, name):
    P, R, C = parts.shape

    def body(p_ref, o_ref):
        g = p_ref[0]
        for p in range(1, P):
            g = g + p_ref[p]
        o_ref[...] = g

    return pl.pallas_call(body, out_shape=jax.ShapeDtypeStruct((R, C), F32), name=name)(parts)


WEIGHTS = ["fox_w_in", "fox_b_f", "fox_q_gain", "fox_k_gain", "fox_w_out", "sgu_w_in", "sgu_b_in", "sgu_v_gain",
           "sgu_v_bias", "sgu_w_s", "sgu_b_s", "sgu_w_out", "ffn_w_up", "ffn_conv_w", "ffn_conv_b", "ffn_w_down",
           "ada_w", "ada_b", "norm1_g", "norm2_g", "final_g"]
BIG_AXIS = dict(fox_w_in=1, fox_w_out=0, sgu_w_in=1, sgu_w_out=0, ffn_w_up=1, ffn_w_down=0, ada_w=1)
SMALL_SHARDED = ["sgu_b_in", "sgu_v_gain", "sgu_v_bias", "ffn_conv_w"]
SINGLE_LAYER = ("fox_", "sgu_")
BEFORE_ATTENTION = ["fox_w_in"]
AFTER_ATTENTION = ["fox_w_out", "sgu_w_in", "sgu_w_out", "ffn_w_up", "ffn_w_down"]
SMALL_EARLY = ["sgu_b_in", "sgu_v_gain", "sgu_v_bias", "sgu_w_s", "sgu_b_s", "ffn_conv_w", "ffn_conv_b", "norm2_g", "final_g"]


def _assemble(stacked, layers, axis):
    _, lr, cc = stacked.shape
    r = lr // layers
    s4 = stacked.reshape(N_DEV, layers, r, cc)
    if axis == 0:
        return s4.transpose(1, 0, 2, 3).reshape(layers, N_DEV * r, cc)
    return s4.transpose(1, 2, 0, 3).reshape(layers, r, N_DEV * cc)


def _disassemble(full, axis):
    layers, R, C = full.shape
    if axis == 0:
        r = R // N_DEV
        return full.reshape(layers, N_DEV, r, C).transpose(1, 0, 2, 3).reshape(N_DEV, layers * r, C)
    cc = C // N_DEV
    return full.reshape(layers, R, N_DEV, cc).transpose(2, 0, 1, 3).reshape(N_DEV, layers * R, cc)


def kernel(x, c, fox_w_in, fox_b_f, fox_q_gain, fox_k_gain, fox_w_out, sgu_w_in, sgu_b_in, sgu_v_gain, sgu_v_bias, sgu_w_s, sgu_b_s, sgu_w_out, ffn_w_up, ffn_conv_w, ffn_conv_b, ffn_w_down, ada_w, ada_b, norm1_g, norm2_g, final_g, loss_target, m_fox_w_in, m_fox_b_f, m_fox_q_gain, m_fox_k_gain, m_fox_w_out, m_sgu_w_in, m_sgu_b_in, m_sgu_v_gain, m_sgu_v_bias, m_sgu_w_s, m_sgu_b_s, m_sgu_w_out, m_ffn_w_up, m_ffn_conv_w, m_ffn_conv_b, m_ffn_w_down, m_ada_w, m_ada_b, m_norm1_g, m_norm2_g, m_final_g, v_fox_w_in, v_fox_b_f, v_fox_q_gain, v_fox_k_gain, v_fox_w_out, v_sgu_w_in, v_sgu_b_in, v_sgu_v_gain, v_sgu_v_bias, v_sgu_w_s, v_sgu_b_s, v_sgu_w_out, v_ffn_w_up, v_ffn_conv_w, v_ffn_conv_b, v_ffn_w_down, v_ada_w, v_ada_b, v_norm1_g, v_norm2_g, v_final_g):
    args = dict(locals())
    wts = {n: args[n] for n in WEIGHTS}
    ms = {n: args["m_" + n] for n in WEIGHTS}
    vs = {n: args["v_" + n] for n in WEIGHTS}
    me = 4 * lax.axis_index("x") + 2 * lax.axis_index("y") + lax.axis_index("c")

    shard2d = lambda n: wts[n].astype(BF16).reshape(-1, wts[n].shape[-1])

    def assemble_big(names, got):
        out = {}
        for n, g in zip(names, got):
            f = _assemble(g, wts[n].shape[0], BIG_AXIS[n])
            out[n] = f[0] if n.startswith(SINGLE_LAYER) else f
        return out

    def blocks_of(names, grads):
        return [_disassemble(grads[n] if grads[n].ndim == 3 else grads[n][None], BIG_AXIS[n]) for n in names]

    send = [c] + [shard2d(n) for n in BEFORE_ATTENTION] + [wts[n].reshape(-1, wts[n].shape[-1]) for n in SMALL_SHARDED]
    got = _exchange(send, [False] * len(send), name="gather_first")
    c_all = got[0].reshape(N_DEV, -1)
    full = assemble_big(BEFORE_ATTENTION, got[1:1 + len(BEFORE_ATTENTION)])
    for n, g in zip(SMALL_SHARDED, got[1 + len(BEFORE_ATTENTION):]):
        lead = wts[n].shape[:-1]
        f = jnp.moveaxis(g.reshape((N_DEV,) + wts[n].shape), 0, -2).reshape(lead + (-1,))
        full[n] = f[0] if n.startswith(SINGLE_LAYER) else f
    for n in WEIGHTS:
        if n not in full and n not in BIG_AXIS:
            full[n] = wts[n][0] if n.startswith(SINGLE_LAYER) else wts[n]

    ada_cols = wts["ada_w"].shape[-1]
    mod_rows = []
    for i in range(2):
        b_mine = lax.dynamic_slice_in_dim(wts["ada_b"][i], me * ada_cols, ada_cols).reshape(1, ada_cols)
        m, c_act = _ada_mod(c_all, wts["ada_w"][i].astype(BF16), b_mine, name=f"ada_mod_{i}")
        mod_rows.append(m)
    got = _exchange([jnp.concatenate(mod_rows, axis=1)[:, None, :]], [True], name="exchange_mods")[0]
    d_model = x.shape[-1]
    mods = []
    for i in range(2):
        mod = got[:, 0, i * ada_cols:(i + 1) * ada_cols].reshape(1, N_DEV * ada_cols)
        mods.append([mod[:, k * d_model:(k + 1) * d_model] for k in range(6)])

    small = [n for n in WEIGHTS if n not in BIG_AXIS]
    small_late = [n for n in small if n not in SMALL_EARLY]

    def pack_flat(arrays):
        flat = jnp.concatenate([a.reshape(-1).astype(F32) for a in arrays])
        rows = -(-flat.shape[0] // (8 * LANES)) * 8
        return jnp.pad(flat, (0, rows * LANES - flat.shape[0])).reshape(rows, LANES)

    late = ([shard2d(n) for n in AFTER_ATTENTION], lambda g: assemble_big(AFTER_ATTENTION, g))
    loss, grad_x, grads, got_late, got_last = _local_step(
        x[0], loss_target[0], full, mods, late,
        lambda gr: (blocks_of(AFTER_ATTENTION, gr) + [pack_flat([gr[n] for n in SMALL_EARLY])],
                    [True] * len(AFTER_ATTENTION) + [False]),
        lambda gr: (blocks_of(BEFORE_ATTENTION, gr), [True] * len(BEFORE_ATTENTION)))

    flat_late_all = _exchange([pack_flat([loss] + [grads[n] for n in small_late])], [False], name="gather_small_grads")[0]
    total_late = _sum_parts(flat_late_all, name="sum_small_grads_late").reshape(-1)
    total_early = _sum_parts(got_late[len(AFTER_ATTENTION)], name="sum_small_grads_early").reshape(-1)
    loss_out = total_late[0]
    summed, offs = {}, {}
    for vec, names, off in ((total_early, SMALL_EARLY, 0), (total_late, small_late, 1)):
        for n in names:
            size = math.prod(grads[n].shape)
            summed[n], offs[n] = vec[off:off + size].reshape(grads[n].shape), off
            off += size

    off_ada = offs["ada_b"]
    dmod_all = flat_late_all.reshape(N_DEV, -1)[:, off_ada:off_ada + 2 * N_DEV * ada_cols].reshape(N_DEV, 2, N_DEV * ada_cols)
    dmod_mine = lax.dynamic_slice_in_dim(dmod_all, me * ada_cols, ada_cols, axis=2)
    d_ada = [_mm(c_act, jnp.pad(dmod_mine[:, i], ((0, c_act.shape[0] - N_DEV), (0, 0))).astype(BF16), ta=True,
                 name=f"ada_dw_{i}") for i in range(2)]

    out_g, out_d, out_m, out_v = {}, {}, {}, {}
    summands = dict(zip(BEFORE_ATTENTION, got_last))
    summands.update(zip(AFTER_ATTENTION, got_late))
    summands["ada_w"] = jnp.concatenate(d_ada, axis=0)[None]
    for n, p in summands.items():
        out_g[n], out_d[n], out_m[n], out_v[n] = _adamw(wts[n], p, ms[n], vs[n], name=f"adamw_{n}")
    small_g = {}
    for n in small:
        g = summed[n]
        if n in SMALL_SHARDED:
            blk = g.shape[-1] // N_DEV
            g = lax.dynamic_slice_in_dim(g, me * blk, blk, axis=g.ndim - 1)
        small_g[n] = g.reshape(wts[n].shape)
    cat = lambda d: jnp.concatenate([d[n].reshape(-1) for n in small])
    n_small = sum(math.prod(wts[n].shape) for n in small)
    rows2 = -(-n_small // (256 * LANES)) * 256
    pack = lambda d, fill: jnp.pad(cat(d), (0, rows2 * LANES - n_small), constant_values=fill).reshape(1, rows2, LANES)
    g, d, mn, vn = _adamw(pack(wts, 0.0), pack(small_g, 0.0), pack(ms, 0.0), pack(vs, 1.0), name="adamw_small")
    off = 0
    for n in small:
        size = math.prod(wts[n].shape)
        for src, dst in ((g, out_g), (d, out_d), (mn, out_m), (vn, out_v)):
            dst[n] = src.reshape(-1)[off:off + size].reshape(wts[n].shape)
        off += size

    return (loss_out, grad_x[None], *[out_g[n] for n in WEIGHTS], *[out_d[n] for n in WEIGHTS],
            *[out_m[n] for n in WEIGHTS], *[out_v[n] for n in WEIGHTS])
```

```python
import functools
import math

import jax
import jax.numpy as jnp
from jax import lax
from jax.experimental import pallas as pl
from jax.experimental.pallas import tpu as pltpu

F32, BF16 = jnp.float32, jnp.bfloat16
N_DEV = 8
HEADS, HEAD_DIM = 16, 64
HEAD_PAIRS = HEADS // 2
LANES = 128
SUBLANES = 8
EPS = 1e-6
SGU_BLOCK, SGU_GROUPS, SGU_CHUNK = 128, 8, 64
CONV_WIDTH = 3
ADAM_LR, ADAM_B1, ADAM_B2, ADAM_EPS, ADAM_WD, ADAM_STEP = 0.001, 0.9, 0.999, 1e-08, 0.01, 10
NEG = -1e30
GELU_C0, GELU_C1 = math.sqrt(2.0 / math.pi), 0.044715
MESH = pl.DeviceIdType.MESH
VMEM_LIMIT = 56 * 1024 * 1024


def _tile(dim, pref):
    if dim <= pref:
        return dim
    t = (pref // LANES) * LANES
    while t >= LANES:
        if dim % t == 0:
            return t
        t -= LANES
    return dim


def _params(*sem):
    return pltpu.CompilerParams(dimension_semantics=sem, vmem_limit_bytes=VMEM_LIMIT)


def _mm(a, b, *, name, ta=False, tb=False, out_dtype=F32, tm=1024, tn=1024, tk=1024, res=None, b_n=None, b_k=None, o_n=None,
        xchg=None):
    M = a.shape[1] if ta else a.shape[0]
    K = a.shape[0] if ta else a.shape[1]
    N = b.shape[0] if tb else b.shape[1]
    tm, tn, tk = _tile(M, tm), _tile(N, tn), _tile(K, tk)
    nk = K // tk
    dims = (((0 if ta else 1,), (1 if tb else 0,)), ((), ()))
    same = lambda idx: idx
    b_n, b_k, o_n = b_n or same, b_k or same, o_n or same
    a_spec = pl.BlockSpec((tk, tm), lambda i, j, k: (k, i)) if ta else pl.BlockSpec((tm, tk), lambda i, j, k: (i, k))
    b_spec = (pl.BlockSpec((tn, tk), lambda i, j, k: (b_n(j), b_k(k))) if tb
              else pl.BlockSpec((tk, tn), lambda i, j, k: (b_k(k), b_n(j))))
    o_spec = pl.BlockSpec((tm, tn), lambda i, j, k: (i, o_n(j)))

    def accumulate(a_ref, b_ref, acc):
        @pl.when(pl.program_id(2) == 0)
        def _():
            acc[...] = jnp.zeros_like(acc)
        acc[...] += lax.dot_general(a_ref[...], b_ref[...], dims, preferred_element_type=F32)

    if res is None:
        def body(a_ref, b_ref, o_ref, acc):
            accumulate(a_ref, b_ref, acc)

            @pl.when(pl.program_id(2) == nk - 1)
            def _():
                o_ref[...] = acc[...].astype(o_ref.dtype)

        if xchg is None:
            return pl.pallas_call(
                body, grid=(M // tm, N // tn, nk), in_specs=[a_spec, b_spec], out_specs=o_spec,
                out_shape=jax.ShapeDtypeStruct((M, N), out_dtype), scratch_shapes=[pltpu.VMEM((tm, tn), F32)],
                compiler_params=_params("parallel", "parallel", "arbitrary"), name=name)(a, b)
        grid = (M // tm, N // tn, nk)
        wrap, x_in, x_out, x_shapes, x_sems, x_ops = _ride_along(xchg, 2, 1, grid)
        outs = pl.pallas_call(
            wrap(body), grid=grid, in_specs=[a_spec, b_spec] + x_in, out_specs=[o_spec] + x_out,
            out_shape=[jax.ShapeDtypeStruct((M, N), out_dtype)] + x_shapes,
            scratch_shapes=[pltpu.VMEM((tm, tn), F32)] + x_sems,
            compiler_params=_params("arbitrary", "arbitrary", "arbitrary"), name=name)(a, b, *x_ops)
        return outs[0], outs[1:]

    x, gate = res

    def body_res(a_ref, b_ref, x_ref, g_ref, o_ref, y_ref, acc):
        accumulate(a_ref, b_ref, acc)

        @pl.when(pl.program_id(2) == nk - 1)
        def _():
            y = acc[...]
            o_ref[...] = x_ref[...] + g_ref[...] * y
            y_ref[...] = y.astype(BF16)

    return pl.pallas_call(
        body_res, grid=(M // tm, N // tn, nk),
        in_specs=[a_spec, b_spec, o_spec, pl.BlockSpec((1, tn), lambda i, j, k: (0, j))],
        out_specs=[o_spec, o_spec],
        out_shape=[jax.ShapeDtypeStruct((M, N), F32), jax.ShapeDtypeStruct((M, N), BF16)],
        scratch_shapes=[pltpu.VMEM((tm, tn), F32)],
        compiler_params=_params("parallel", "parallel", "arbitrary"), name=name)(a, b, x, gate)


def _ada_mod(c_rows, w, b, *, name):
    R, D = c_rows.shape
    N = w.shape[1]
    tn = _tile(N, 1536)
    rows = 16
    c_pad = jnp.pad(c_rows, ((0, rows - R), (0, 0)))

    def body(c_ref, w_ref, b_ref, o_ref, ca_ref):
        cv = c_ref[...]
        ca16 = (cv * jax.nn.sigmoid(cv)).astype(BF16)
        ca_ref[...] = ca16
        o_ref[...] = jnp.dot(ca16, w_ref[...], preferred_element_type=F32) + b_ref[...]

    out, ca = pl.pallas_call(
        body, grid=(N // tn,),
        in_specs=[pl.BlockSpec((rows, D), lambda j: (0, 0)), pl.BlockSpec((D, tn), lambda j: (0, j)),
                  pl.BlockSpec((1, tn), lambda j: (0, j))],
        out_specs=[pl.BlockSpec((rows, tn), lambda j: (0, j)), pl.BlockSpec((rows, D), lambda j: (0, 0))],
        out_shape=[jax.ShapeDtypeStruct((rows, N), F32), jax.ShapeDtypeStruct((rows, D), BF16)],
        compiler_params=_params("arbitrary"), name=name)(c_pad, w, b)
    return out[0:R], ca


def _norm_mod_fwd(x, g, shift, scale, *, name, ts=512):
    S, D = x.shape
    ts = _tile(S, ts)
    row = pl.BlockSpec((ts, D), lambda i: (i, 0))
    vec = pl.BlockSpec((1, D), lambda i: (0, 0))

    def body(x_ref, g_ref, sh_ref, sc_ref, h_ref):
        xv = x_ref[...]
        r = lax.rsqrt(jnp.mean(xv * xv, axis=-1, keepdims=True) + EPS)
        h_ref[...] = ((xv * r * g_ref[...]) * (1.0 + sc_ref[...]) + sh_ref[...]).astype(BF16)

    return pl.pallas_call(body, grid=(S // ts,), in_specs=[row, vec, vec, vec], out_specs=row,
                          out_shape=jax.ShapeDtypeStruct((S, D), BF16),
                          compiler_params=_params("parallel"), name=name)(x, g, shift, scale)


def _acc_init(step, *refs):
    @pl.when(step == 0)
    def _():
        for r in refs:
            r[...] = jnp.zeros_like(r)


def _colsum(v):
    return jnp.sum(v, axis=0, keepdims=True)


def _norm_mod_bwd(dh, x, g, scale, dres, prev=None, *, name, ts=512):
    S, D = x.shape
    ts = _tile(S, ts)
    row = pl.BlockSpec((ts, D), lambda i: (i, 0))
    vec = pl.BlockSpec((1, D), lambda i: (0, 0))
    has_prev = prev is not None

    def body(*refs):
        if has_prev:
            dh_ref, x_ref, g_ref, sc_ref, dres_ref, y_ref, gate_ref, dx_ref, dg_ref, dsh_ref, dsc_ref, dy_ref, dgate_ref = refs
            _acc_init(pl.program_id(0), dg_ref, dsh_ref, dsc_ref, dgate_ref)
        else:
            dh_ref, x_ref, g_ref, sc_ref, dres_ref, dx_ref, dg_ref, dsh_ref, dsc_ref = refs
            _acc_init(pl.program_id(0), dg_ref, dsh_ref, dsc_ref)
        xv, dhv, gv = x_ref[...], dh_ref[...], g_ref[...]
        r = lax.rsqrt(jnp.mean(xv * xv, axis=-1, keepdims=True) + EPS)
        xh = xv * r
        dsh_ref[...] += _colsum(dhv)
        dsc_ref[...] += _colsum(dhv * (xh * gv))
        dn = dhv * (1.0 + sc_ref[...])
        dg_ref[...] += _colsum(dn * xh)
        dxh = dn * gv
        dx = dres_ref[...] + r * (dxh - xh * jnp.mean(dxh * xh, axis=-1, keepdims=True))
        dx_ref[...] = dx
        if has_prev:
            dy_ref[...] = (gate_ref[...] * dx).astype(BF16)
            dgate_ref[...] += _colsum(dx * y_ref[...].astype(F32))

    ins, in_specs = [dh, x, g, scale, dres], [row, row, vec, vec, row]
    outs = [jax.ShapeDtypeStruct((S, D), F32)] + [jax.ShapeDtypeStruct((1, D), F32)] * 3
    out_specs = [row, vec, vec, vec]
    if has_prev:
        ins += list(prev)
        in_specs += [row, vec]
        outs += [jax.ShapeDtypeStruct((S, D), BF16), jax.ShapeDtypeStruct((1, D), F32)]
        out_specs += [row, vec]
    return pl.pallas_call(body, grid=(S // ts,), in_specs=in_specs, out_specs=out_specs, out_shape=outs,
                          compiler_params=_params("arbitrary"), name=name)(*ins)


def _final_loss(x, g, target, y, gate, *, name, ts=512):
    S, D = x.shape
    ts = _tile(S, ts)
    row = pl.BlockSpec((ts, D), lambda i: (i, 0))
    vec = pl.BlockSpec((1, D), lambda i: (0, 0))
    lvec = pl.BlockSpec((1, LANES), lambda i: (0, 0))

    def body(x_ref, g_ref, t_ref, y_ref, gate_ref, loss_ref, dg_ref, dx_ref, dy_ref, dgate_ref):
        _acc_init(pl.program_id(0), loss_ref, dg_ref, dgate_ref)
        xv, gv = x_ref[...], g_ref[...]
        r = lax.rsqrt(jnp.mean(xv * xv, axis=-1, keepdims=True) + EPS)
        xh = xv * r
        e = xh * gv - t_ref[...]
        loss_ref[...] += 0.5 * jnp.sum(jnp.mean(e * e, axis=-1, keepdims=True), axis=0, keepdims=True)
        dout = e * (1.0 / D)
        dg_ref[...] += _colsum(dout * xh)
        dxh = dout * gv
        dx = r * (dxh - xh * jnp.mean(dxh * xh, axis=-1, keepdims=True))
        dx_ref[...] = dx
        dy_ref[...] = (gate_ref[...] * dx).astype(BF16)
        dgate_ref[...] += _colsum(dx * y_ref[...].astype(F32))

    return pl.pallas_call(
        body, grid=(S // ts,), in_specs=[row, vec, row, row, vec], out_specs=[lvec, vec, row, row, vec],
        out_shape=[jax.ShapeDtypeStruct((1, LANES), F32), jax.ShapeDtypeStruct((1, D), F32),
                   jax.ShapeDtypeStruct((S, D), F32), jax.ShapeDtypeStruct((S, D), BF16),
                   jax.ShapeDtypeStruct((1, D), F32)],
        compiler_params=_params("arbitrary"), name=name)(x, g, target, y, gate)


def _head_mean(v, gmat):
    hi = v.astype(BF16)
    lo = (v - hi.astype(F32)).astype(BF16)
    return jnp.dot(hi, gmat, preferred_element_type=F32) + jnp.dot(lo, gmat, preferred_element_type=F32)


L_F, L_ONE, L_SHIFT = HEAD_DIM, HEAD_DIM + 3, HEAD_DIM + 6
KEY_CHUNKS = 2
SHIFT_FREE_LOGIT_BOUND = 60.0


def _lane():
    return lax.broadcasted_iota(jnp.int32, (1, LANES), 1)


def _split3(v):
    p1 = v.astype(BF16).astype(F32)
    r1 = v - p1
    p2 = r1.astype(BF16).astype(F32)
    p3 = (r1 - p2).astype(BF16).astype(F32)
    return p1, p2, p3


def _put3(lane, first, pieces):
    out = jnp.where(lane == first, pieces[0], 0.0)
    for k in (1, 2):
        out = out + jnp.where(lane == first + k, pieces[k], 0.0)
    return out


def _ones3(lane, first):
    return jnp.where((lane >= first) & (lane < first + 3), 1.0, 0.0)


def _lane_col(v, lane, idx):
    return jnp.sum(jnp.where(lane == idx, v, 0.0), axis=-1, keepdims=True)


def _head_of_pair(pair, e, lane):
    return jnp.where(lane < HEAD_DIM, pair if e == 0 else pltpu.roll(pair, HEAD_DIM, 1), 0.0)


def _pair_of_heads(even, odd, lane):
    return jnp.where(lane < HEAD_DIM, even, pltpu.roll(odd, HEAD_DIM, 1))


def _fox_prep_fwd(proj, fcum, qgain, kgain, gmat, *, name, ts=256):
    S = proj.shape[0]
    D = HEADS * HEAD_DIM
    ts = _tile(S, ts)
    scale = HEAD_DIM ** -0.5

    def body(p_ref, f_ref, qg_ref, kg_ref, gm_ref, q_ref, k_ref, v_ref):
        gm, lane, fc = gm_ref[...], _lane(), f_ref[...]
        for cpair in range(HEAD_PAIRS):
            qv = p_ref[:, pl.ds(cpair * LANES, LANES)].astype(F32)
            kv = p_ref[:, pl.ds(D + cpair * LANES, LANES)].astype(F32)
            vv = p_ref[:, pl.ds(2 * D + cpair * LANES, LANES)].astype(F32)
            qn = (qv * lax.rsqrt(_head_mean(qv * qv, gm) + EPS) * qg_ref[...]) * scale
            kn = kv * lax.rsqrt(_head_mean(kv * kv, gm) + EPS) * kg_ref[...]
            for e in range(2):
                h = 2 * cpair + e
                f3 = _split3(_lane_col(fc, lane, h))
                q_ref[h] = (_head_of_pair(qn, e, lane) + _put3(lane, L_F, f3) + _ones3(lane, L_ONE)).astype(BF16)
                k_ref[h] = (_head_of_pair(kn, e, lane) + _ones3(lane, L_F)
                            - _put3(lane, L_ONE, f3) + _ones3(lane, L_SHIFT)).astype(BF16)
                v_ref[h] = (_head_of_pair(vv, e, lane) + _ones3(lane, L_F)).astype(BF16)

    vec = pl.BlockSpec((1, LANES), lambda i: (0, 0))
    wide = pl.BlockSpec((HEADS, ts, LANES), lambda i: (0, i, 0))
    return pl.pallas_call(
        body, grid=(S // ts,),
        in_specs=[pl.BlockSpec((ts, 3 * D), lambda i: (i, 0)), pl.BlockSpec((ts, LANES), lambda i: (i, 0)), vec, vec,
                  pl.BlockSpec((LANES, LANES), lambda i: (0, 0))],
        out_specs=[wide, wide, wide], out_shape=[jax.ShapeDtypeStruct((HEADS, S, LANES), BF16)] * 3,
        compiler_params=_params("parallel"), name=name)(proj, fcum, qgain, kgain, gmat)


def _fox_prep_bwd(proj, dq_aug, dk_aug, dv_aug, dog, qgain, kgain, gmat, *, name, ts=256):
    S = proj.shape[0]
    D = HEADS * HEAD_DIM
    ts = _tile(S, ts)
    scale = HEAD_DIM ** -0.5

    def body(p_ref, dq_ref, dk_ref, dv_ref, dog_ref, qg_ref, kg_ref, gm_ref, o_ref, df_ref, dqg_ref, dkg_ref):
        _acc_init(pl.program_id(0), dqg_ref, dkg_ref)
        gm, lane = gm_ref[...], _lane()
        df = jnp.zeros((ts, LANES), F32)
        for cpair in range(HEAD_PAIRS):
            tiles = []
            for e in range(2):
                h = 2 * cpair + e
                tq, tk = dq_ref[h], dk_ref[h]
                df = jnp.where(lane == h, _lane_col(tq, lane, L_F) - _lane_col(tk, lane, L_ONE), df)
                tiles.append((tq, tk, dv_ref[h].astype(F32)))
            pair = [_pair_of_heads(tiles[0][k], tiles[1][k], lane) for k in range(3)]
            for half, g_ref, dg_ref, mult in ((0, qg_ref, dqg_ref, scale), (1, kg_ref, dkg_ref, 1.0)):
                v = p_ref[:, pl.ds(half * D + cpair * LANES, LANES)].astype(F32)
                r = lax.rsqrt(_head_mean(v * v, gm) + EPS)
                xh = v * r
                dn = pair[half] * mult
                dg_ref[...] += _colsum(dn * xh)
                dxh = dn * g_ref[...]
                o_ref[:, pl.ds(half * D + cpair * LANES, LANES)] = (r * (dxh - xh * _head_mean(dxh * xh, gm))).astype(BF16)
            o_ref[:, pl.ds(2 * D + cpair * LANES, LANES)] = pair[2].astype(BF16)
        o_ref[:, pl.ds(3 * D, D)] = dog_ref[...]
        o_ref[:, pl.ds(4 * D, LANES)] = jnp.zeros((ts, LANES), BF16)
        df_ref[...] = df

    row = pl.BlockSpec((ts, D), lambda i: (i, 0))
    wide = pl.BlockSpec((HEADS, ts, LANES), lambda i: (0, i, 0))
    vec = pl.BlockSpec((1, LANES), lambda i: (0, 0))
    return pl.pallas_call(
        body, grid=(S // ts,),
        in_specs=[pl.BlockSpec((ts, 2 * D), lambda i: (i, 0)), wide, wide, wide, row, vec, vec,
                  pl.BlockSpec((LANES, LANES), lambda i: (0, 0))],
        out_specs=[pl.BlockSpec((ts, 4 * D + LANES), lambda i: (i, 0)), pl.BlockSpec((ts, LANES), lambda i: (i, 0)), vec, vec],
        out_shape=[jax.ShapeDtypeStruct((S, 4 * D + LANES), BF16), jax.ShapeDtypeStruct((S, LANES), F32),
                   jax.ShapeDtypeStruct((1, LANES), F32), jax.ShapeDtypeStruct((1, LANES), F32)],
        compiler_params=_params("arbitrary"), name=name)(proj, dq_aug, dk_aug, dv_aug, dog, qgain, kgain, gmat)


def _log_sigmoid(z):
    return jnp.minimum(z, 0.0) - jnp.log(1.0 + jnp.exp(-jnp.abs(z)))


def _fox_decay_fwd(fl, bf, *, name, tb=256):
    S = fl.shape[0]
    tb = _tile(S, tb)

    def body(fl_ref, b_ref, o_ref, carry):
        @pl.when(pl.program_id(0) == 0)
        def _():
            carry[...] = jnp.zeros_like(carry)
        logf = _log_sigmoid(fl_ref[...] + b_ref[...])
        tri = (lax.broadcasted_iota(jnp.int32, (tb, tb), 1) <= lax.broadcasted_iota(jnp.int32, (tb, tb), 0)).astype(F32)
        cs = jnp.dot(tri, logf, preferred_element_type=F32, precision=lax.Precision.HIGHEST) + carry[...]
        o_ref[...] = cs
        carry[...] = _row_of(cs, tb - 1)

    return pl.pallas_call(
        body, grid=(S // tb,),
        in_specs=[pl.BlockSpec((tb, LANES), lambda i: (i, 0)), pl.BlockSpec((1, LANES), lambda i: (0, 0))],
        out_specs=pl.BlockSpec((tb, LANES), lambda i: (i, 0)),
        out_shape=jax.ShapeDtypeStruct((S, LANES), F32), scratch_shapes=[pltpu.VMEM((1, LANES), F32)],
        compiler_params=_params("arbitrary"), name=name)(fl, bf)


def _fox_decay_bwd(dF, fl, bf, dproj, *, name, tb=256):
    S = fl.shape[0]
    tb = _tile(S, tb)
    n = S // tb
    last_col = dproj.shape[1] // LANES - 1

    def body(df_ref, fl_ref, b_ref, dproj_hbm, o_ref, db_ref, carry):
        del dproj_hbm
        @pl.when(pl.program_id(0) == 0)
        def _():
            carry[...] = jnp.zeros_like(carry)
            db_ref[...] = jnp.zeros_like(db_ref)
        tri = (lax.broadcasted_iota(jnp.int32, (tb, tb), 1) >= lax.broadcasted_iota(jnp.int32, (tb, tb), 0)).astype(F32)
        rc = jnp.dot(tri, df_ref[...], preferred_element_type=F32, precision=lax.Precision.HIGHEST) + carry[...]
        carry[...] = _row_of(rc, 0)
        dfl = rc * jax.nn.sigmoid(-(fl_ref[...] + b_ref[...]))
        o_ref[...] = dfl.astype(BF16)
        db_ref[...] += _colsum(dfl)

    rev = pl.BlockSpec((tb, LANES), lambda i: (n - 1 - i, 0))
    vec = pl.BlockSpec((1, LANES), lambda i: (0, 0))
    return pl.pallas_call(
        body, grid=(n,), in_specs=[rev, rev, vec, pl.BlockSpec(memory_space=pl.ANY)],
        out_specs=[pl.BlockSpec((tb, LANES), lambda i: (n - 1 - i, last_col)), vec],
        out_shape=[jax.ShapeDtypeStruct(dproj.shape, BF16), jax.ShapeDtypeStruct((1, LANES), F32)],
        scratch_shapes=[pltpu.VMEM((1, LANES), F32)], input_output_aliases={3: 0},
        compiler_params=_params("arbitrary"), name=name)(dF, fl, bf, dproj)


_NT = (((1,), (1,)), ((), ()))
_TN = (((0,), (0,)), ((), ()))


def _causal(T, transposed=False):
    r, c = lax.broadcasted_iota(jnp.int32, (T, T), 0), lax.broadcasted_iota(jnp.int32, (T, T), 1)
    return r <= c if transposed else c <= r


def _with_shift(q_tile, shift, lane):
    keep = jnp.where((lane >= L_SHIFT) & (lane < L_SHIFT + 3), 0.0, q_tile)
    return (keep + _put3(lane, L_SHIFT, _split3(-shift))).astype(BF16)


def _ride_along(xchg, n_in, n_out, grid):
    if xchg is None:
        return (lambda body: body), [], [], [], [], []
    arrs, scatter = xchg
    n = len(arrs)

    def wrap(body):
        def wrapped(*refs):
            own_in, x_in = refs[:n_in], refs[n_in:n_in + n]
            own_out, x_out = refs[n_in + n:n_in + n + n_out], refs[n_in + n + n_out:n_in + 2 * n + n_out]
            rest = refs[n_in + 2 * n + n_out:]
            own_scratch, sems = rest[:len(rest) - 3], rest[len(rest) - 3:]
            ids = [pl.program_id(d) for d in range(len(grid))]
            first = functools.reduce(jnp.logical_and, [i == 0 for i in ids])
            last = functools.reduce(jnp.logical_and, [i == g - 1 for i, g in zip(ids, grid)])

            @pl.when(first)
            def _():
                for cp in _xchg_copies(x_in, x_out, scatter, *sems):
                    cp.start()

            body(*own_in, *own_out, *own_scratch)

            @pl.when(last)
            def _():
                for cp in _xchg_copies(x_in, x_out, scatter, *sems):
                    cp.wait()

        return wrapped

    return wrap, [_HBM] * n, [_HBM] * n, _xchg_out_shapes(arrs, scatter), _xchg_sems(n), list(arrs)


def _attn_rowmax(q_aug, k_aug, *, name, T=1024):
    S = q_aug.shape[1]
    T = _tile(S, T)
    n = S // T

    def body(q_ref, k_ref, o_ref, m_s):
        i, j = pl.program_id(1), pl.program_id(2)

        @pl.when(j == 0)
        def _():
            m_s[...] = jnp.full_like(m_s, NEG)

        def step(diag):
            s = lax.dot_general(q_ref[...], k_ref[...], _NT, preferred_element_type=F32)
            if diag:
                s = jnp.where(_causal(T), s, NEG)
            m = m_s[...]
            for cb in range(T // LANES):
                m = jnp.maximum(m, s[:, cb * LANES:(cb + 1) * LANES])
            m_s[...] = m

        @pl.when(j < i)
        def _():
            step(False)

        @pl.when(j == i)
        def _():
            step(True)
            o_ref[...] = _with_shift(q_ref[...].astype(F32), jnp.max(m_s[...], axis=-1, keepdims=True), _lane())

    qrow = pl.BlockSpec((None, T, LANES), lambda h, i, j: (h, i, 0))
    return pl.pallas_call(
        body, grid=(HEADS, n, n),
        in_specs=[qrow, pl.BlockSpec((None, T, LANES), lambda h, i, j: (h, jnp.minimum(j, i), 0))],
        out_specs=qrow, out_shape=jax.ShapeDtypeStruct(q_aug.shape, BF16),
        scratch_shapes=[pltpu.VMEM((T, LANES), F32)],
        compiler_params=_params("parallel", "parallel", "arbitrary"), name=name)(q_aug, k_aug)


def _attn_fwd(q_max, k_aug, v_aug, xchg=None, *, name, T=1024):
    S = q_max.shape[1]
    T = _tile(S, T)
    n = S // T
    wrap, x_in, x_out, x_shapes, x_sems, x_ops = _ride_along(xchg, 3, 2, (HEADS, n, n))

    def body(q_ref, k_ref, v_ref, o_ref, qb_ref, acc_s):
        i, j = pl.program_id(1), pl.program_id(2)

        @pl.when(j == 0)
        def _():
            acc_s[...] = jnp.zeros_like(acc_s)

        def block(rows, cols, mask):
            s = lax.dot_general(q_ref[rows, :], k_ref[cols, :], _NT, preferred_element_type=F32)
            if mask is not None:
                s = jnp.where(mask, s, NEG)
            return jnp.dot(jnp.exp(s).astype(BF16), v_ref[cols, :], preferred_element_type=F32)

        @pl.when(j < i)
        def _():
            chunk = T // KEY_CHUNKS
            upd = block(pl.ds(0, T), pl.ds(0, chunk), None)
            for c in range(1, KEY_CHUNKS):
                upd = upd + block(pl.ds(0, T), pl.ds(c * chunk, chunk), None)
            acc_s[...] += upd

        @pl.when(j == i)
        def _():
            half = T // 2
            lo, hi = pl.ds(0, half), pl.ds(half, half)
            acc_s[lo, :] += block(lo, lo, _causal(half))
            acc_s[hi, :] += block(hi, lo, None) + block(hi, hi, _causal(half))
            lane = _lane()
            acc = acc_s[...]
            l = _lane_col(acc, lane, L_F)
            o_ref[...] = acc / l
            qf = q_ref[...].astype(F32)
            row_max = -jnp.sum(jnp.where((lane >= L_SHIFT) & (lane < L_SHIFT + 3), qf, 0.0), axis=-1, keepdims=True)
            qb_ref[...] = _with_shift(qf, row_max + jnp.log(l), lane)

    qrow = pl.BlockSpec((None, T, LANES), lambda h, i, j: (h, i, 0))
    kv = pl.BlockSpec((None, T, LANES), lambda h, i, j: (h, jnp.minimum(j, i), 0))
    outs = pl.pallas_call(
        wrap(body), grid=(HEADS, n, n), in_specs=[qrow, kv, kv] + x_in, out_specs=[qrow, qrow] + x_out,
        out_shape=[jax.ShapeDtypeStruct(q_max.shape, F32), jax.ShapeDtypeStruct(q_max.shape, BF16)] + x_shapes,
        scratch_shapes=[pltpu.VMEM((T, LANES), F32)] + x_sems,
        compiler_params=_params("arbitrary", "arbitrary", "arbitrary"), name=name)(q_max, k_aug, v_aug, *x_ops)
    return outs[0], outs[1], outs[2:]


def _attn_bwd(q_lse, k_aug, v_aug, do_aug, xchg=None, *, name, T=1024):
    S = q_lse.shape[1]
    T = _tile(S, T)
    n = S // T
    wrap, x_in, x_out, x_shapes, x_sems, x_ops = _ride_along(xchg, 4, 3, (HEADS, n, n))

    def body(q_ref, do_ref, k_ref, v_ref, dq_ref, dk_ref, dv_ref, dq_s, dk_s, dv_s):
        j, i = pl.program_id(1), pl.program_id(2)

        def block(keys, queries, mask):
            q, do, k, v = q_ref[queries, :], do_ref[queries, :], k_ref[keys, :], v_ref[keys, :]
            st = lax.dot_general(k, q, _NT, preferred_element_type=F32)
            if mask is not None:
                st = jnp.where(mask, st, NEG)
            pt = jnp.exp(st)
            dst = (pt * lax.dot_general(v, do, _NT, preferred_element_type=F32)).astype(BF16)
            dv_s[keys, :] += jnp.dot(pt.astype(BF16), do, preferred_element_type=F32)
            dk_s[keys, :] += jnp.dot(dst, q, preferred_element_type=F32)
            return lax.dot_general(dst, k, _TN, preferred_element_type=F32)

        @pl.when(i == j)
        def _():
            dk_s[...] = jnp.zeros_like(dk_s)
            dv_s[...] = jnp.zeros_like(dv_s)

            @pl.when(j == 0)
            def _():
                dq_s[i] = jnp.zeros((T, LANES), F32)

            half = T // 2
            lo, hi = pl.ds(0, half), pl.ds(half, half)
            dq_s[i, lo, :] += block(lo, lo, _causal(half, transposed=True))
            dq_s[i, hi, :] += block(lo, hi, None) + block(hi, hi, _causal(half, transposed=True))
            dq_ref[...] = dq_s[j]

        @pl.when(i > j)
        def _():
            chunk = T // KEY_CHUNKS
            upd = block(pl.ds(0, chunk), pl.ds(0, T), None)
            for c in range(1, KEY_CHUNKS):
                upd = upd + block(pl.ds(c * chunk, chunk), pl.ds(0, T), None)

            @pl.when(j == 0)
            def _():
                dq_s[i] = upd

            @pl.when(j > 0)
            def _():
                dq_s[i] += upd

        @pl.when(i == n - 1)
        def _():
            dk_ref[...] = dk_s[...]
            dv_ref[...] = dv_s[...].astype(BF16)

    qrow = pl.BlockSpec((None, T, LANES), lambda h, j, i: (h, jnp.maximum(i, j), 0))
    kv = pl.BlockSpec((None, T, LANES), lambda h, j, i: (h, j, 0))
    outs = pl.pallas_call(
        wrap(body), grid=(HEADS, n, n), in_specs=[qrow, qrow, kv, kv] + x_in, out_specs=[kv, kv, kv] + x_out,
        out_shape=[jax.ShapeDtypeStruct(q_lse.shape, F32), jax.ShapeDtypeStruct(q_lse.shape, F32),
                   jax.ShapeDtypeStruct(q_lse.shape, BF16)] + x_shapes,
        scratch_shapes=[pltpu.VMEM((n, T, LANES), F32), pltpu.VMEM((T, LANES), F32), pltpu.VMEM((T, LANES), F32)] + x_sems,
        compiler_params=_params("arbitrary", "arbitrary", "arbitrary"), name=name)(q_lse, do_aug, k_aug, v_aug, *x_ops)
    return outs[0], outs[1], outs[2], outs[3:]


def _fox_gate_fwd(att_aug, proj, *, name, ts=256):
    S = att_aug.shape[1]
    D = HEADS * HEAD_DIM
    ts = _tile(S, ts)

    def body(a_ref, o_ref, att_ref, out_ref):
        lane = _lane()
        for cpair in range(HEAD_PAIRS):
            cols = pl.ds(cpair * LANES, LANES)
            pair = _pair_of_heads(a_ref[2 * cpair], a_ref[2 * cpair + 1], lane)
            att_ref[:, cols] = pair
            out_ref[:, cols] = (pair * jax.nn.sigmoid(o_ref[:, cols].astype(F32))).astype(BF16)

    row = pl.BlockSpec((ts, D), lambda i: (i, 0))
    return pl.pallas_call(
        body, grid=(S // ts,),
        in_specs=[pl.BlockSpec((HEADS, ts, LANES), lambda i: (0, i, 0)), pl.BlockSpec((ts, D), lambda i: (i, 3))],
        out_specs=[row, row], out_shape=[jax.ShapeDtypeStruct((S, D), F32), jax.ShapeDtypeStruct((S, D), BF16)],
        compiler_params=_params("parallel"), name=name)(att_aug, proj)


def _fox_gate_bwd(da, att, proj, *, name, ts=256):
    S, D = att.shape
    ts = _tile(S, ts)

    def body(da_ref, a_ref, o_ref, do_ref, dog_ref):
        lane = _lane()
        for cpair in range(HEAD_PAIRS):
            cols = pl.ds(cpair * LANES, LANES)
            dav, av = da_ref[:, cols].astype(F32), a_ref[:, cols]
            sg = jax.nn.sigmoid(o_ref[:, cols].astype(F32))
            datt = (dav * sg).astype(BF16).astype(F32)
            dog_ref[:, cols] = (dav * av * sg * (1.0 - sg)).astype(BF16)
            prod = datt * av
            for e in range(2):
                in_head = (lane < HEAD_DIM) if e == 0 else (lane >= HEAD_DIM)
                delta = jnp.sum(jnp.where(in_head, prod, 0.0), axis=-1, keepdims=True)
                tile = _head_of_pair(datt, e, lane) + _put3(lane, L_F, _split3(-delta))
                do_ref[2 * cpair + e] = tile.astype(BF16)

    row = pl.BlockSpec((ts, D), lambda i: (i, 0))
    return pl.pallas_call(
        body, grid=(S // ts,), in_specs=[row, row, pl.BlockSpec((ts, D), lambda i: (i, 3))],
        out_specs=[pl.BlockSpec((HEADS, ts, LANES), lambda i: (0, i, 0)), row],
        out_shape=[jax.ShapeDtypeStruct((HEADS, S, LANES), BF16), jax.ShapeDtypeStruct((S, D), BF16)],
        compiler_params=_params("parallel"), name=name)(da, att, proj)


def _row_of(block, r):
    rows = lax.broadcasted_iota(jnp.int32, block.shape, 0)
    return jnp.sum(jnp.where(rows == r, block, 0.0), axis=0, keepdims=True)


def _shift_down(cur, tail, k):
    out = pltpu.roll(cur, k, 0)
    top = out[:SUBLANES]
    rows = lax.broadcasted_iota(jnp.int32, top.shape, 0)
    for r in range(k):
        top = jnp.where(rows == r, _row_of(tail, tail.shape[0] - k + r), top)
    return jnp.concatenate([top, out[SUBLANES:]], axis=0)


def _shift_up(cur, head, k):
    n = cur.shape[0]
    out = pltpu.roll(cur, n - k, 0)
    bottom = out[n - SUBLANES:]
    rows = lax.broadcasted_iota(jnp.int32, bottom.shape, 0)
    for r in range(k):
        bottom = jnp.where(rows == SUBLANES - k + r, _row_of(head, r), bottom)
    return jnp.concatenate([out[:n - SUBLANES], bottom], axis=0)


HALO = 16


CONV_TC = 1408


def _pair_tiles(v):
    nc = v.shape[-1] // (2 * CONV_TC)
    return jnp.swapaxes(v.reshape(v.shape[:-1] + (2, nc, CONV_TC)), -3, -2).reshape(v.shape)


def _unpair_tiles(v):
    nc = v.shape[-1] // (2 * CONV_TC)
    return jnp.swapaxes(v.reshape(v.shape[:-1] + (nc, 2, CONV_TC)), -3, -2).reshape(v.shape)


def _conv_rows(cur, tail, w_ref, b_ref, cols):
    a1, a2 = _shift_down(cur, tail, 1), _shift_down(cur, tail, 2)
    return a2 * w_ref[0:1, cols] + a1 * w_ref[1:2, cols] + cur * w_ref[2:3, cols] + b_ref[:, cols], (a2, a1, cur)


def _conv_gate_fwd(a, cw, cb, *, name, ts=512):
    S, F2 = a.shape
    tc = CONV_TC
    ts = _tile(S, ts)
    nc = F2 // (2 * tc)
    sub = ts // HALO
    halves = (pl.ds(0, tc), pl.ds(tc, tc))

    def body(a_ref, t_ref, w_ref, b_ref, o_ref):
        first = pl.program_id(1) == 0
        pre = []
        for cols in halves:
            tail = jnp.where(first, 0.0, t_ref[:, cols].astype(F32))
            pre.append(_conv_rows(a_ref[:, cols].astype(F32), tail, w_ref, b_ref, cols)[0])
        g, val = pre
        o_ref[...] = (g * jax.nn.sigmoid(g) * val).astype(BF16)

    return pl.pallas_call(
        body, grid=(nc, S // ts),
        in_specs=[pl.BlockSpec((ts, 2 * tc), lambda j, i: (i, j)),
                  pl.BlockSpec((HALO, 2 * tc), lambda j, i: (jnp.maximum(i * sub - 1, 0), j)),
                  pl.BlockSpec((CONV_WIDTH, 2 * tc), lambda j, i: (0, j)), pl.BlockSpec((1, 2 * tc), lambda j, i: (0, j))],
        out_specs=pl.BlockSpec((ts, tc), lambda j, i: (i, j)),
        out_shape=jax.ShapeDtypeStruct((S, F2 // 2), BF16),
        compiler_params=_params("parallel", "parallel"), name=name)(a, a, cw, cb)


def _conv_gate_bwd(a, dact, cw, cb, *, name, ts=512):
    S, F2 = a.shape
    tc = CONV_TC
    ts = _tile(S, ts)
    nc = F2 // (2 * tc)
    sub = ts // HALO
    n_rows = S // ts
    halves = (pl.ds(0, tc), pl.ds(tc, tc))

    def body(a_ref, at_ref, ah_ref, d_ref, dh_ref, w_ref, b_ref, da_ref, s_ref):
        i = pl.program_id(1)
        _acc_init(i, s_ref)

        def dpre_of(rows, tails, d):
            (g, taps_g), (val, taps_v) = [_conv_rows(rows[h], tails[h], w_ref, b_ref, halves[h]) for h in range(2)]
            sg = jax.nn.sigmoid(g)
            return (d * val * (sg * (1.0 + g * (1.0 - sg))), d * (g * sg)), (taps_g, taps_v)

        cur = [a_ref[:, c].astype(F32) for c in halves]
        tail = [jnp.where(i == 0, 0.0, at_ref[:, c].astype(F32)) for c in halves]
        dpre, taps = dpre_of(cur, tail, d_ref[...].astype(F32))
        head, _ = dpre_of([ah_ref[:, c].astype(F32) for c in halves], [x[ts - HALO:, :] for x in cur], dh_ref[...].astype(F32))
        for h, cols in enumerate(halves):
            dd = dpre[h]
            nxt = jnp.where(i == n_rows - 1, 0.0, head[h])
            da_ref[:, cols] = (dd * w_ref[2:3, cols] + _shift_up(dd, nxt, 1) * w_ref[1:2, cols]
                               + _shift_up(dd, nxt, 2) * w_ref[0:1, cols]).astype(BF16)
            for r in range(CONV_WIDTH):
                s_ref[r:r + 1, cols] += _colsum(dd * taps[h][r])
            s_ref[CONV_WIDTH:CONV_WIDTH + 1, cols] += _colsum(dd)

    nxt_rows = lambda i: jnp.minimum((i + 1) * sub, S // HALO - 1)
    return pl.pallas_call(
        body, grid=(nc, n_rows),
        in_specs=[pl.BlockSpec((ts, 2 * tc), lambda j, i: (i, j)),
                  pl.BlockSpec((HALO, 2 * tc), lambda j, i: (jnp.maximum(i * sub - 1, 0), j)),
                  pl.BlockSpec((HALO, 2 * tc), lambda j, i: (nxt_rows(i), j)),
                  pl.BlockSpec((ts, tc), lambda j, i: (i, j)), pl.BlockSpec((HALO, tc), lambda j, i: (nxt_rows(i), j)),
                  pl.BlockSpec((CONV_WIDTH, 2 * tc), lambda j, i: (0, j)), pl.BlockSpec((1, 2 * tc), lambda j, i: (0, j))],
        out_specs=[pl.BlockSpec((ts, 2 * tc), lambda j, i: (i, j)), pl.BlockSpec((8, 2 * tc), lambda j, i: (0, j))],
        out_shape=[jax.ShapeDtypeStruct((S, F2), BF16), jax.ShapeDtypeStruct((8, F2), F32)],
        compiler_params=_params("parallel", "arbitrary"), name=name)(a, a, a, dact, dact, cw, cb)


def _gelu_parts(z):
    z2 = z * z
    t = jnp.tanh(GELU_C0 * (z + GELU_C1 * z * z2))
    val = 0.5 * z * (1.0 + t)
    grad = 0.5 * (1.0 + t) + 0.5 * z * (1.0 - t * t) * GELU_C0 * (1.0 + 3.0 * GELU_C1 * z2)
    return val, grad


def _sgu_fwd(pre, b_in, vgain, vbias, wm, bsb, *, name, ts=256):
    S, W2 = pre.shape
    W = W2 // 2
    gd = W // SGU_GROUPS
    ts = _tile(S, ts)

    def body(p_ref, b_ref, vg_ref, vb_ref, wm_ref, bs_ref, y_ref):
        u = _gelu_parts(p_ref[:, pl.ds(0, W)].astype(F32) + b_ref[:, pl.ds(0, W)])[0]
        v = _gelu_parts(p_ref[:, pl.ds(W, W)].astype(F32) + b_ref[:, pl.ds(W, W)])[0]
        mu = jnp.mean(v, axis=-1, keepdims=True)
        vc = v - mu
        rstd = lax.rsqrt(jnp.mean(vc * vc, axis=-1, keepdims=True) + EPS)
        vn = ((vc * rstd) * vg_ref[...] + vb_ref[...]).astype(BF16)
        for blk in range(ts // SGU_BLOCK):
            r0 = blk * SGU_BLOCK
            for g in range(SGU_GROUPS):
                c0 = g * gd
                mixed = jnp.dot(wm_ref[g], vn[r0:r0 + SGU_BLOCK, c0:c0 + gd], preferred_element_type=F32) + bs_ref[g]
                y_ref[pl.ds(r0, SGU_BLOCK), pl.ds(c0, gd)] = (u[r0:r0 + SGU_BLOCK, c0:c0 + gd] * mixed).astype(BF16)

    full = lambda shape: pl.BlockSpec(shape, lambda i: (0,) * len(shape))
    return pl.pallas_call(
        body, grid=(S // ts,),
        in_specs=[pl.BlockSpec((ts, W2), lambda i: (i, 0)), full((1, W2)), full((1, W)), full((1, W)),
                  full((SGU_GROUPS, SGU_BLOCK, SGU_BLOCK)), full((SGU_GROUPS, SGU_BLOCK, gd))],
        out_specs=pl.BlockSpec((ts, W), lambda i: (i, 0)), out_shape=jax.ShapeDtypeStruct((S, W), BF16),
        compiler_params=_params("parallel"), name=name)(pre, b_in, vgain, vbias, wm, bsb)


def _sgu_bwd(pre, dy, b_in, vgain, vbias, wm, wmt, bsb, *, name, ts=256):
    S, W2 = pre.shape
    W = W2 // 2
    gd = W // SGU_GROUPS
    ts = _tile(S, ts)
    last = S // ts - 1

    def body(p_ref, dy_ref, b_ref, vg_ref, vb_ref, wm_ref, wmt_ref, bs_ref,
             dp_ref, db_ref, dvg_ref, dvb_ref, dws_ref, dbs_ref, du_s, dvn_s, dbs_s):
        step = pl.program_id(0)
        _acc_init(step, db_ref, dvg_ref, dvb_ref, dws_ref, dbs_s)
        u, gu = _gelu_parts(p_ref[:, pl.ds(0, W)].astype(F32) + b_ref[:, pl.ds(0, W)])
        v, gv = _gelu_parts(p_ref[:, pl.ds(W, W)].astype(F32) + b_ref[:, pl.ds(W, W)])
        mu = jnp.mean(v, axis=-1, keepdims=True)
        vc = v - mu
        rstd = lax.rsqrt(jnp.mean(vc * vc, axis=-1, keepdims=True) + EPS)
        vhat = vc * rstd
        vn = (vhat * vg_ref[...] + vb_ref[...]).astype(BF16)
        dyv = dy_ref[...].astype(F32)
        for blk in range(ts // SGU_BLOCK):
            r0 = blk * SGU_BLOCK
            for g in range(SGU_GROUPS):
                c0 = g * gd
                vn_g = vn[r0:r0 + SGU_BLOCK, c0:c0 + gd]
                dy_g = dyv[r0:r0 + SGU_BLOCK, c0:c0 + gd]
                mixed = jnp.dot(wm_ref[g], vn_g, preferred_element_type=F32) + bs_ref[g]
                dmix = dy_g * u[r0:r0 + SGU_BLOCK, c0:c0 + gd]
                dmix_b = dmix.astype(BF16)
                du_s[pl.ds(r0, SGU_BLOCK), pl.ds(c0, gd)] = dy_g * mixed
                dvn_s[pl.ds(r0, SGU_BLOCK), pl.ds(c0, gd)] = jnp.dot(wmt_ref[g], dmix_b, preferred_element_type=F32)
                dws_ref[g] += lax.dot_general(dmix_b, vn_g, _NT, preferred_element_type=F32)
                dbs_s[g] += dmix
        dvn = dvn_s[...]
        dvg_ref[...] += _colsum(dvn * vhat)
        dvb_ref[...] += _colsum(dvn)
        dvh = dvn * vg_ref[...]
        dv = rstd * (dvh - jnp.mean(dvh, axis=-1, keepdims=True) - vhat * jnp.mean(dvh * vhat, axis=-1, keepdims=True))
        dpu = du_s[...] * gu
        dpv = dv * gv
        dp_ref[:, pl.ds(0, W)] = dpu.astype(BF16)
        dp_ref[:, pl.ds(W, W)] = dpv.astype(BF16)
        db_ref[:, pl.ds(0, W)] += _colsum(dpu)
        db_ref[:, pl.ds(W, W)] += _colsum(dpv)

        @pl.when(step == last)
        def _():
            for g in range(SGU_GROUPS):
                dbs_ref[g] = jnp.broadcast_to(jnp.sum(dbs_s[g], axis=-1, keepdims=True), (SGU_BLOCK, SGU_BLOCK))

    full = lambda shape: pl.BlockSpec(shape, lambda i: (0,) * len(shape))
    gsq = (SGU_GROUPS, SGU_BLOCK, SGU_BLOCK)
    return pl.pallas_call(
        body, grid=(S // ts,),
        in_specs=[pl.BlockSpec((ts, W2), lambda i: (i, 0)), pl.BlockSpec((ts, W), lambda i: (i, 0)),
                  full((1, W2)), full((1, W)), full((1, W)), full(gsq), full(gsq), full((SGU_GROUPS, SGU_BLOCK, gd))],
        out_specs=[pl.BlockSpec((ts, W2), lambda i: (i, 0)), full((1, W2)), full((1, W)), full((1, W)), full(gsq), full(gsq)],
        out_shape=[jax.ShapeDtypeStruct((S, W2), BF16), jax.ShapeDtypeStruct((1, W2), F32),
                   jax.ShapeDtypeStruct((1, W), F32), jax.ShapeDtypeStruct((1, W), F32),
                   jax.ShapeDtypeStruct(gsq, F32), jax.ShapeDtypeStruct(gsq, F32)],
        scratch_shapes=[pltpu.VMEM((ts, W), F32), pltpu.VMEM((ts, W), F32), pltpu.VMEM((SGU_GROUPS, SGU_BLOCK, gd), F32)],
        compiler_params=_params("arbitrary"), name=name)(pre, dy, b_in, vgain, vbias, wm, wmt, bsb)


def _paired_to_natural(w_up):
    nc = w_up.shape[1] // (2 * CONV_TC)
    return lambda q: (q % 2) * nc + q // 2


def _ffn_fwd(x, mods, n2g, w_up, cw, cb, w_down, tag):
    sh, sc, gate = mods
    h = _norm_mod_fwd(x, n2g, sh, sc, name=f"{tag}_norm_fwd")
    a = _mm(h, w_up, out_dtype=BF16, tn=CONV_TC, b_n=_paired_to_natural(w_up), name=f"{tag}_up")
    act = _conv_gate_fwd(a, cw, cb, name=f"{tag}_conv_fwd")
    x_out, y = _mm(act, w_down, tk=1408, res=(x, gate), name=f"{tag}_down")
    return x_out, (x, h, a, act, y)


def _ffn_bwd(dy, saved, mods, n2g, w_up, cw, cb, w_down, dres, prev, tag):
    x, h, a, act, _ = saved
    sh, sc, gate = mods
    dact = _mm(dy, w_down, tb=True, out_dtype=BF16, tn=1408, name=f"{tag}_down_dx")
    dw_down = _mm(act, dy, ta=True, out_dtype=BF16, tm=1408, name=f"{tag}_down_dw")
    da, sums = _conv_gate_bwd(a, dact, cw, cb, name=f"{tag}_conv_bwd")
    dh = _mm(da, w_up, tb=True, tk=CONV_TC, b_k=_paired_to_natural(w_up), name=f"{tag}_up_dx")
    dw_up = _mm(h, da, ta=True, out_dtype=BF16, tn=CONV_TC, o_n=_paired_to_natural(w_up), name=f"{tag}_up_dw")
    outs = _norm_mod_bwd(dh, x, n2g, sc, dres, prev, name=f"{tag}_norm_bwd")
    sums = _unpair_tiles(sums)
    return outs, dict(w_up=dw_up, w_down=dw_down, conv_w=sums[0:CONV_WIDTH], conv_b=sums[CONV_WIDTH])


def _local_step(x, target, w, mods, late=None, early=None, last=None):
    S, D = x.shape
    lane = jnp.arange(LANES)
    gmat = jnp.where((lane[:, None] // HEAD_DIM) == (lane[None, :] // HEAD_DIM), 1.0 / HEAD_DIM, 0.0).astype(BF16)
    qg2 = jnp.tile(w["fox_q_gain"].reshape(1, HEAD_DIM), (1, 2))
    kg2 = jnp.tile(w["fox_k_gain"].reshape(1, HEAD_DIM), (1, 2))
    bf_pad = jnp.pad(w["fox_b_f"].reshape(1, HEADS), ((0, 0), (0, LANES - HEADS)))
    w_in_pad = jnp.pad(w["fox_w_in"], ((0, 0), (0, 4 * D + LANES - w["fox_w_in"].shape[1])))
    w_qkvo, w_f = w_in_pad[:, :4 * D], w_in_pad[:, 4 * D:]
    tpos = jnp.arange(SGU_BLOCK)
    smask = (tpos[None, :] // SGU_CHUNK) <= (tpos[:, None] // SGU_CHUNK)
    wm32 = jnp.where(smask[None], w["sgu_w_s"], 0.0)
    wm, wmt = wm32.astype(BF16), jnp.swapaxes(wm32, 1, 2).astype(BF16)
    gd = w["sgu_v_gain"].shape[-1] // SGU_GROUPS
    bsb = jnp.broadcast_to(w["sgu_b_s"][:, :, None], (SGU_GROUPS, SGU_BLOCK, gd))
    vec = lambda v: v.reshape(1, -1)

    sh1, sc1, g1 = mods[0][0:3]
    h0 = _norm_mod_fwd(x, vec(w["norm1_g"][0]), sh1, sc1, name="fox_norm_fwd")
    proj = _mm(h0, w_qkvo, out_dtype=BF16, name="fox_proj")
    fl = _mm(h0, w_f, name="fox_forget_proj")
    fcum = _fox_decay_fwd(fl, bf_pad, name="fox_decay")
    q_aug, k_aug, v_aug = _fox_prep_fwd(proj, fcum, qg2, kg2, gmat, name="fox_qk_norm")
    logit_bound = 8.0 * jnp.max(jnp.abs(w["fox_q_gain"])) * jnp.max(jnp.abs(w["fox_k_gain"]))
    q_max = lax.cond(logit_bound <= SHIFT_FREE_LOGIT_BOUND, lambda: q_aug,
                     lambda: _attn_rowmax(q_aug, k_aug, name="fox_attn_rowmax"))
    xchg = None if late is None else (late[0], [False] * len(late[0]))
    att_aug, q_lse, gathered = _attn_fwd(q_max, k_aug, v_aug, xchg, name="fox_attn_fwd", T=2048)
    if late is not None:
        w = {**w, **late[1](gathered)}
    w = dict(w, ffn_conv_w=_pair_tiles(w["ffn_conv_w"]), ffn_conv_b=_pair_tiles(w["ffn_conv_b"]))
    att, ag = _fox_gate_fwd(att_aug, proj, name="fox_gate_fwd")
    x1, y_fox = _mm(ag, w["fox_w_out"], res=(x, g1), name="fox_out")
    x2, ffn0 = _ffn_fwd(x1, mods[0][3:6], vec(w["norm2_g"][0]), w["ffn_w_up"][0], w["ffn_conv_w"][0],
                        vec(w["ffn_conv_b"][0]), w["ffn_w_down"][0], "ffn0")

    sh1b, sc1b, g1b = mods[1][0:3]
    h1 = _norm_mod_fwd(x2, vec(w["norm1_g"][1]), sh1b, sc1b, name="sgu_norm_fwd")
    pre = _mm(h1, w["sgu_w_in"], out_dtype=BF16, name="sgu_in")
    b_in, vg, vb = vec(w["sgu_b_in"]), vec(w["sgu_v_gain"]), vec(w["sgu_v_bias"])
    ys = _sgu_fwd(pre, b_in, vg, vb, wm, bsb, name="sgu_core_fwd")
    x3, y_sgu = _mm(ys, w["sgu_w_out"], res=(x2, g1b), name="sgu_out")
    x4, ffn1 = _ffn_fwd(x3, mods[1][3:6], vec(w["norm2_g"][1]), w["ffn_w_up"][1], w["ffn_conv_w"][1],
                        vec(w["ffn_conv_b"][1]), w["ffn_w_down"][1], "ffn1")

    loss, d_final_g, dx4, dy_ffn1, dgate_ffn1 = _final_loss(x4, vec(w["final_g"]), target, ffn1[4], mods[1][5], name="final_loss")

    (dx3, dn2g_1, dsh2_1, dsc2_1, dy_sgu, dgate_sgu), g_ffn1 = _ffn_bwd(
        dy_ffn1, ffn1, mods[1][3:6], vec(w["norm2_g"][1]), w["ffn_w_up"][1], w["ffn_conv_w"][1], vec(w["ffn_conv_b"][1]),
        w["ffn_w_down"][1], dx4, (y_sgu, g1b), "ffn1")

    dys = _mm(dy_sgu, w["sgu_w_out"], tb=True, out_dtype=BF16, name="sgu_out_dx")
    dw_sgu_out = _mm(ys, dy_sgu, ta=True, out_dtype=BF16, name="sgu_out_dw")
    dpre, db_in, dvg, dvb, dws, dbs = _sgu_bwd(pre, dys, b_in, vg, vb, wm, wmt, bsb, name="sgu_core_bwd")
    dh1 = _mm(dpre, w["sgu_w_in"], tb=True, name="sgu_in_dx")
    dw_sgu_in = _mm(h1, dpre, ta=True, out_dtype=BF16, name="sgu_in_dw")
    dx2, dn1g_1, dsh1_1, dsc1_1, dy_ffn0, dgate_ffn0 = _norm_mod_bwd(
        dh1, x2, vec(w["norm1_g"][1]), sc1b, dx3, (ffn0[4], mods[0][5]), name="sgu_norm_bwd")

    (dx1, dn2g_0, dsh2_0, dsc2_0, dy_fox, dgate_fox), g_ffn0 = _ffn_bwd(
        dy_ffn0, ffn0, mods[0][3:6], vec(w["norm2_g"][0]), w["ffn_w_up"][0], w["ffn_conv_w"][0], vec(w["ffn_conv_b"][0]),
        w["ffn_w_down"][0], dx2, (y_fox, g1), "ffn0")

    dag = _mm(dy_fox, w["fox_w_out"], tb=True, out_dtype=BF16, name="fox_out_dx")
    dw_fox_out = _mm(ag, dy_fox, ta=True, out_dtype=BF16, name="fox_out_dw")
    do_aug, dog = _fox_gate_bwd(dag, att, proj, name="fox_gate_bwd")
    grads = dict(
        sgu_w_in=dw_sgu_in, sgu_b_in=db_in[0], sgu_v_gain=dvg[0], sgu_v_bias=dvb[0],
        sgu_w_s=jnp.where(smask[None], dws, 0.0), sgu_b_s=dbs[:, :, 0], sgu_w_out=dw_sgu_out,
        ffn_w_up=jnp.stack([g_ffn0["w_up"], g_ffn1["w_up"]]),
        ffn_conv_w=jnp.stack([g_ffn0["conv_w"], g_ffn1["conv_w"]]),
        ffn_conv_b=jnp.stack([g_ffn0["conv_b"], g_ffn1["conv_b"]]),
        ffn_w_down=jnp.stack([g_ffn0["w_down"], g_ffn1["w_down"]]),
        final_g=d_final_g[0], fox_w_out=dw_fox_out, norm2_g=jnp.concatenate([dn2g_0, dn2g_1], axis=0),
    )
    xchg = None if early is None else early(grads)
    dq_aug, dk_aug, dv_aug, exchanged = _attn_bwd(q_lse, k_aug, v_aug, do_aug, xchg, name="fox_attn_bwd", T=2048)
    dproj, dF, dqg, dkg = _fox_prep_bwd(proj, dq_aug, dk_aug, dv_aug, dog, qg2, kg2, gmat, name="fox_qk_norm_bwd")
    dproj, dbf = _fox_decay_bwd(dF, fl, bf_pad, dproj, name="fox_decay_bwd")
    dw_fox_in = _mm(h0, dproj, ta=True, out_dtype=BF16, tn=1408, name="fox_proj_dw")[:, :w["fox_w_in"].shape[1]]
    xchg = None if last is None else last(dict(fox_w_in=dw_fox_in))
    dh0 = _mm(dproj, w_in_pad, tb=True, tk=1408, xchg=xchg, name="fox_proj_dx")
    dh0, exchanged_last = dh0 if last is not None else (dh0, [])
    dx0, dn1g_0, dsh1_0, dsc1_0 = _norm_mod_bwd(dh0, x, vec(w["norm1_g"][0]), sc1, dx1, None, name="fox_norm_bwd")

    dmod0 = jnp.concatenate([dsh1_0, dsc1_0, dgate_fox, dsh2_0, dsc2_0, dgate_ffn0], axis=1)
    dmod1 = jnp.concatenate([dsh1_1, dsc1_1, dgate_sgu, dsh2_1, dsc2_1, dgate_ffn1], axis=1)
    grads.update(
        fox_w_in=dw_fox_in,
        fox_b_f=dbf[0, :HEADS],
        fox_q_gain=dqg[0, :HEAD_DIM] + dqg[0, HEAD_DIM:],
        fox_k_gain=dkg[0, :HEAD_DIM] + dkg[0, HEAD_DIM:],
        fox_w_out=dw_fox_out,
        ada_b=jnp.concatenate([dmod0, dmod1], axis=0),
        norm1_g=jnp.concatenate([dn1g_0, dn1g_1], axis=0),
    )
    return loss[0, 0], dx0, grads, exchanged, exchanged_last


_HBM = pl.BlockSpec(memory_space=pl.ANY)
N_PEER = N_DEV - 1


def _xchg_out_shapes(arrs, scatter):
    return [jax.ShapeDtypeStruct(a.shape if s else (N_DEV,) + a.shape, a.dtype) for a, s in zip(arrs, scatter)]


def _xchg_sems(n):
    return [pltpu.SemaphoreType.DMA((n * N_PEER,)), pltpu.SemaphoreType.DMA((n * N_PEER,)), pltpu.SemaphoreType.DMA((n,))]


def _xchg_copies(ins, outs, scatter, send, recv, loc):
    x, y, c = lax.axis_index("x"), lax.axis_index("y"), lax.axis_index("c")
    me = 4 * x + 2 * y + c
    copies = []
    for a in range(len(ins)):
        copies.append(pltpu.make_async_copy(ins[a].at[me] if scatter[a] else ins[a], outs[a].at[me], loc.at[a]))
        for k in range(1, N_DEV):
            px = 1 - x if k & 4 else x
            py = 1 - y if k & 2 else y
            pc = 1 - c if k & 1 else c
            copies.append(pltpu.make_async_remote_copy(
                src_ref=ins[a].at[4 * px + 2 * py + pc] if scatter[a] else ins[a], dst_ref=outs[a].at[me],
                send_sem=send.at[a * N_PEER + k - 1], recv_sem=recv.at[a * N_PEER + k - 1],
                device_id=(px, py, pc), device_id_type=MESH))
    return copies


def _exchange(arrs, scatter, *, name):
    n = len(arrs)

    def body(*refs):
        copies = _xchg_copies(refs[:n], refs[n:2 * n], scatter, *refs[2 * n:])
        for cp in copies:
            cp.start()
        for cp in copies:
            cp.wait()

    return pl.pallas_call(
        body, in_specs=[_HBM] * n, out_specs=[_HBM] * n, out_shape=_xchg_out_shapes(arrs, scatter),
        scratch_shapes=_xchg_sems(n),
        compiler_params=pltpu.CompilerParams(has_side_effects=True), name=name)(*arrs)


def _adamw(w, parts, m, v, *, name, tr=256):
    L, R, C = w.shape
    P = parts.shape[0]
    tr = next(t for t in range(min(R, tr), 0, -1) if R % t == 0 and (t % 16 == 0 or t == R))
    nr = R // tr
    c1 = 1.0 - ADAM_B1 ** ADAM_STEP
    c2 = 1.0 - ADAM_B2 ** ADAM_STEP

    def body(w_ref, p_ref, m_ref, v_ref, g_ref, d_ref, mo_ref, vo_ref):
        g = p_ref[0].astype(F32)
        for p in range(1, P):
            g = g + p_ref[p].astype(F32)
        mn = ADAM_B1 * m_ref[0] + (1.0 - ADAM_B1) * g
        vn = ADAM_B2 * v_ref[0] + (1.0 - ADAM_B2) * (g * g)
        g_ref[0] = g
        mo_ref[0] = mn
        vo_ref[0] = vn
        d_ref[0] = -ADAM_LR * ((mn / c1) / (jnp.sqrt(vn / c2) + ADAM_EPS) + ADAM_WD * w_ref[0])

    row = pl.BlockSpec((1, tr, C), lambda l, i: (l, i, 0))
    return pl.pallas_call(
        body, grid=(L, nr), in_specs=[row, pl.BlockSpec((P, tr, C), lambda l, i: (0, l * nr + i, 0)), row, row],
        out_specs=[row] * 4, out_shape=[jax.ShapeDtypeStruct((L, R, C), F32)] * 4,
        compiler_params=_params("parallel", "parallel"), name=name)(w, parts, m, v)


def _sum_parts(parts, *, name):
    P, R, C = parts.shape

    def body(p_ref, o_ref):
        g = p_ref[0]
        for p in range(1, P):
            g = g + p_ref[p]
        o_ref[...] = g

    return pl.pallas_call(body, out_shape=jax.ShapeDtypeStruct((R, C), F32), name=name)(parts)


WEIGHTS = ["fox_w_in", "fox_b_f", "fox_q_gain", "fox_k_gain", "fox_w_out", "sgu_w_in", "sgu_b_in", "sgu_v_gain",
           "sgu_v_bias", "sgu_w_s", "sgu_b_s", "sgu_w_out", "ffn_w_up", "ffn_conv_w", "ffn_conv_b", "ffn_w_down",
           "ada_w", "ada_b", "norm1_g", "norm2_g", "final_g"]
BIG_AXIS = dict(fox_w_in=1, fox_w_out=0, sgu_w_in=1, sgu_w_out=0, ffn_w_up=1, ffn_w_down=0, ada_w=1)
SMALL_SHARDED = ["sgu_b_in", "sgu_v_gain", "sgu_v_bias", "ffn_conv_w"]
SINGLE_LAYER = ("fox_", "sgu_")
BEFORE_ATTENTION = ["fox_w_in"]
AFTER_ATTENTION = ["fox_w_out", "sgu_w_in", "sgu_w_out", "ffn_w_up", "ffn_w_down"]
SMALL_EARLY = ["sgu_b_in", "sgu_v_gain", "sgu_v_bias", "sgu_w_s", "sgu_b_s", "ffn_conv_w", "ffn_conv_b", "norm2_g", "final_g"]


def _assemble(stacked, layers, axis):
    _, lr, cc = stacked.shape
    r = lr // layers
    s4 = stacked.reshape(N_DEV, layers, r, cc)
    if axis == 0:
        return s4.transpose(1, 0, 2, 3).reshape(layers, N_DEV * r, cc)
    return s4.transpose(1, 2, 0, 3).reshape(layers, r, N_DEV * cc)


def _disassemble(full, axis):
    layers, R, C = full.shape
    if axis == 0:
        r = R // N_DEV
        return full.reshape(layers, N_DEV, r, C).transpose(1, 0, 2, 3).reshape(N_DEV, layers * r, C)
    cc = C // N_DEV
    return full.reshape(layers, R, N_DEV, cc).transpose(2, 0, 1, 3).reshape(N_DEV, layers * R, cc)


def kernel(x, c, fox_w_in, fox_b_f, fox_q_gain, fox_k_gain, fox_w_out, sgu_w_in, sgu_b_in, sgu_v_gain, sgu_v_bias, sgu_w_s, sgu_b_s, sgu_w_out, ffn_w_up, ffn_conv_w, ffn_conv_b, ffn_w_down, ada_w, ada_b, norm1_g, norm2_g, final_g, loss_target, m_fox_w_in, m_fox_b_f, m_fox_q_gain, m_fox_k_gain, m_fox_w_out, m_sgu_w_in, m_sgu_b_in, m_sgu_v_gain, m_sgu_v_bias, m_sgu_w_s, m_sgu_b_s, m_sgu_w_out, m_ffn_w_up, m_ffn_conv_w, m_ffn_conv_b, m_ffn_w_down, m_ada_w, m_ada_b, m_norm1_g, m_norm2_g, m_final_g, v_fox_w_in, v_fox_b_f, v_fox_q_gain, v_fox_k_gain, v_fox_w_out, v_sgu_w_in, v_sgu_b_in, v_sgu_v_gain, v_sgu_v_bias, v_sgu_w_s, v_sgu_b_s, v_sgu_w_out, v_ffn_w_up, v_ffn_conv_w, v_ffn_conv_b, v_ffn_w_down, v_ada_w, v_ada_b, v_norm1_g, v_norm2_g, v_final_g):
    args = dict(locals())
    wts = {n: args[n] for n in WEIGHTS}
    ms = {n: args["m_" + n] for n in WEIGHTS}
    vs = {n: args["v_" + n] for n in WEIGHTS}
    me = 4 * lax.axis_index("x") + 2 * lax.axis_index("y") + lax.axis_index("c")

    shard2d = lambda n: wts[n].astype(BF16).reshape(-1, wts[n].shape[-1])

    def assemble_big(names, got):
        out = {}
        for n, g in zip(names, got):
            f = _assemble(g, wts[n].shape[0], BIG_AXIS[n])
            out[n] = f[0] if n.startswith(SINGLE_LAYER) else f
        return out

    def blocks_of(names, grads):
        return [_disassemble(grads[n] if grads[n].ndim == 3 else grads[n][None], BIG_AXIS[n]) for n in names]

    send = [c] + [shard2d(n) for n in BEFORE_ATTENTION] + [wts[n].reshape(-1, wts[n].shape[-1]) for n in SMALL_SHARDED]
    got = _exchange(send, [False] * len(send), name="gather_first")
    c_all = got[0].reshape(N_DEV, -1)
    full = assemble_big(BEFORE_ATTENTION, got[1:1 + len(BEFORE_ATTENTION)])
    for n, g in zip(SMALL_SHARDED, got[1 + len(BEFORE_ATTENTION):]):
        lead = wts[n].shape[:-1]
        f = jnp.moveaxis(g.reshape((N_DEV,) + wts[n].shape), 0, -2).reshape(lead + (-1,))
        full[n] = f[0] if n.startswith(SINGLE_LAYER) else f
    for n in WEIGHTS:
        if n not in full and n not in BIG_AXIS:
            full[n] = wts[n][0] if n.startswith(SINGLE_LAYER) else wts[n]

    ada_cols = wts["ada_w"].shape[-1]
    mod_rows = []
    for i in range(2):
        b_mine = lax.dynamic_slice_in_dim(wts["ada_b"][i], me * ada_cols, ada_cols).reshape(1, ada_cols)
        m, c_act = _ada_mod(c_all, wts["ada_w"][i].astype(BF16), b_mine, name=f"ada_mod_{i}")
        mod_rows.append(m)
    got = _exchange([jnp.concatenate(mod_rows, axis=1)[:, None, :]], [True], name="exchange_mods")[0]
    d_model = x.shape[-1]
    mods = []
    for i in range(2):
        mod = got[:, 0, i * ada_cols:(i + 1) * ada_cols].reshape(1, N_DEV * ada_cols)
        mods.append([mod[:, k * d_model:(k + 1) * d_model] for k in range(6)])

    small = [n for n in WEIGHTS if n not in BIG_AXIS]
    small_late = [n for n in small if n not in SMALL_EARLY]

    def pack_flat(arrays):
        flat = jnp.concatenate([a.reshape(-1).astype(F32) for a in arrays])
        rows = -(-flat.shape[0] // (8 * LANES)) * 8
        return jnp.pad(flat, (0, rows * LANES - flat.shape[0])).reshape(rows, LANES)

    late = ([shard2d(n) for n in AFTER_ATTENTION], lambda g: assemble_big(AFTER_ATTENTION, g))
    loss, grad_x, grads, got_late, got_last = _local_step(
        x[0], loss_target[0], full, mods, late,
        lambda gr: (blocks_of(AFTER_ATTENTION, gr) + [pack_flat([gr[n] for n in SMALL_EARLY])],
                    [True] * len(AFTER_ATTENTION) + [False]),
        lambda gr: (blocks_of(BEFORE_ATTENTION, gr), [True] * len(BEFORE_ATTENTION)))

    flat_late_all = _exchange([pack_flat([loss] + [grads[n] for n in small_late])], [False], name="gather_small_grads")[0]
    total_late = _sum_parts(flat_late_all, name="sum_small_grads_late").reshape(-1)
    total_early = _sum_parts(got_late[len(AFTER_ATTENTION)], name="sum_small_grads_early").reshape(-1)
    loss_out = total_late[0]
    summed, offs = {}, {}
    for vec, names, off in ((total_early, SMALL_EARLY, 0), (total_late, small_late, 1)):
        for n in names:
            size = math.prod(grads[n].shape)
            summed[n], offs[n] = vec[off:off + size].reshape(grads[n].shape), off
            off += size

    off_ada = offs["ada_b"]
    dmod_all = flat_late_all.reshape(N_DEV, -1)[:, off_ada:off_ada + 2 * N_DEV * ada_cols].reshape(N_DEV, 2, N_DEV * ada_cols)
    dmod_mine = lax.dynamic_slice_in_dim(dmod_all, me * ada_cols, ada_cols, axis=2)
    d_ada = [_mm(c_act, jnp.pad(dmod_mine[:, i], ((0, c_act.shape[0] - N_DEV), (0, 0))).astype(BF16), ta=True,
                 name=f"ada_dw_{i}") for i in range(2)]

    out_g, out_d, out_m, out_v = {}, {}, {}, {}
    summands = dict(zip(BEFORE_ATTENTION, got_last))
    summands.update(zip(AFTER_ATTENTION, got_late))
    summands["ada_w"] = jnp.concatenate(d_ada, axis=0)[None]
    for n, p in summands.items():
        out_g[n], out_d[n], out_m[n], out_v[n] = _adamw(wts[n], p, ms[n], vs[n], name=f"adamw_{n}")
    small_g = {}
    for n in small:
        g = summed[n]
        if n in SMALL_SHARDED:
            blk = g.shape[-1] // N_DEV
            g = lax.dynamic_slice_in_dim(g, me * blk, blk, axis=g.ndim - 1)
        small_g[n] = g.reshape(wts[n].shape)
    cat = lambda d: jnp.concatenate([d[n].reshape(-1) for n in small])
    n_small = sum(math.prod(wts[n].shape) for n in small)
    rows2 = -(-n_small // (256 * LANES)) * 256
    pack = lambda d, fill: jnp.pad(cat(d), (0, rows2 * LANES - n_small), constant_values=fill).reshape(1, rows2, LANES)
    g, d, mn, vn = _adamw(pack(wts, 0.0), pack(small_g, 0.0), pack(ms, 0.0), pack(vs, 1.0), name="adamw_small")
    off = 0
    for n in small:
        size = math.prod(wts[n].shape)
        for src, dst in ((g, out_g), (d, out_d), (mn, out_m), (vn, out_v)):
            dst[n] = src.reshape(-1)[off:off + size].reshape(wts[n].shape)
        off += size

    return (loss_out, grad_x[None], *[out_g[n] for n in WEIGHTS], *[out_d[n] for n in WEIGHTS],
            *[out_m[n] for n in WEIGHTS], *[out_v[n] for n in WEIGHTS])
```

```python
import functools
import math

import jax
import jax.numpy as jnp
from jax import lax
from jax.experimental import pallas as pl
from jax.experimental.pallas import tpu as pltpu

F32, BF16 = jnp.float32, jnp.bfloat16
N_DEV = 8
HEADS, HEAD_DIM = 16, 64
HEAD_PAIRS = HEADS // 2
LANES = 128
SUBLANES = 8
EPS = 1e-6
SGU_BLOCK, SGU_GROUPS, SGU_CHUNK = 128, 8, 64
CONV_WIDTH = 3
ADAM_LR, ADAM_B1, ADAM_B2, ADAM_EPS, ADAM_WD, ADAM_STEP = 0.001, 0.9, 0.999, 1e-08, 0.01, 10
NEG = -1e30
GELU_C0, GELU_C1 = math.sqrt(2.0 / math.pi), 0.044715
MESH = pl.DeviceIdType.MESH
VMEM_LIMIT = 56 * 1024 * 1024


def _tile(dim, pref):
    if dim <= pref:
        return dim
    t = (pref // LANES) * LANES
    while t >= LANES:
        if dim % t == 0:
            return t
        t -= LANES
    return dim


def _params(*sem):
    return pltpu.CompilerParams(dimension_semantics=sem, vmem_limit_bytes=VMEM_LIMIT)


def _mm(a, b, *, name, ta=False, tb=False, out_dtype=F32, tm=1024, tn=1024, tk=1024, res=None, b_n=None, b_k=None, o_n=None,
        xchg=None):
    M = a.shape[1] if ta else a.shape[0]
    K = a.shape[0] if ta else a.shape[1]
    N = b.shape[0] if tb else b.shape[1]
    tm, tn, tk = _tile(M, tm), _tile(N, tn), _tile(K, tk)
    nk = K // tk
    dims = (((0 if ta else 1,), (1 if tb else 0,)), ((), ()))
    same = lambda idx: idx
    b_n, b_k, o_n = b_n or same, b_k or same, o_n or same
    a_spec = pl.BlockSpec((tk, tm), lambda i, j, k: (k, i)) if ta else pl.BlockSpec((tm, tk), lambda i, j, k: (i, k))
    b_spec = (pl.BlockSpec((tn, tk), lambda i, j, k: (b_n(j), b_k(k))) if tb
              else pl.BlockSpec((tk, tn), lambda i, j, k: (b_k(k), b_n(j))))
    o_spec = pl.BlockSpec((tm, tn), lambda i, j, k: (i, o_n(j)))

    def accumulate(a_ref, b_ref, acc):
        @pl.when(pl.program_id(2) == 0)
        def _():
            acc[...] = jnp.zeros_like(acc)
        acc[...] += lax.dot_general(a_ref[...], b_ref[...], dims, preferred_element_type=F32)

    if res is None:
        def body(a_ref, b_ref, o_ref, acc):
            accumulate(a_ref, b_ref, acc)

            @pl.when(pl.program_id(2) == nk - 1)
            def _():
                o_ref[...] = acc[...].astype(o_ref.dtype)

        if xchg is None:
            return pl.pallas_call(
                body, grid=(M // tm, N // tn, nk), in_specs=[a_spec, b_spec], out_specs=o_spec,
                out_shape=jax.ShapeDtypeStruct((M, N), out_dtype), scratch_shapes=[pltpu.VMEM((tm, tn), F32)],
                compiler_params=_params("parallel", "parallel", "arbitrary"), name=name)(a, b)
        grid = (M // tm, N // tn, nk)
        wrap, x_in, x_out, x_shapes, x_sems, x_ops = _ride_along(xchg, 2, 1, grid)
        outs = pl.pallas_call(
            wrap(body), grid=grid, in_specs=[a_spec, b_spec] + x_in, out_specs=[o_spec] + x_out,
            out_shape=[jax.ShapeDtypeStruct((M, N), out_dtype)] + x_shapes,
            scratch_shapes=[pltpu.VMEM((tm, tn), F32)] + x_sems,
            compiler_params=_params("arbitrary", "arbitrary", "arbitrary"), name=name)(a, b, *x_ops)
        return outs[0], outs[1:]

    x, gate = res

    def body_res(a_ref, b_ref, x_ref, g_ref, o_ref, y_ref, acc):
        accumulate(a_ref, b_ref, acc)

        @pl.when(pl.program_id(2) == nk - 1)
        def _():
            y = acc[...]
            o_ref[...] = x_ref[...] + g_ref[...] * y
            y_ref[...] = y.astype(BF16)

    return pl.pallas_call(
        body_res, grid=(M // tm, N // tn, nk),
        in_specs=[a_spec, b_spec, o_spec, pl.BlockSpec((1, tn), lambda i, j, k: (0, j))],
        out_specs=[o_spec, o_spec],
        out_shape=[jax.ShapeDtypeStruct((M, N), F32), jax.ShapeDtypeStruct((M, N), BF16)],
        scratch_shapes=[pltpu.VMEM((tm, tn), F32)],
        compiler_params=_params("parallel", "parallel", "arbitrary"), name=name)(a, b, x, gate)


def _ada_mod(c_rows, w, b, *, name):
    R, D = c_rows.shape
    N = w.shape[1]
    tn = _tile(N, 1536)
    rows = 16
    c_pad = jnp.pad(c_rows, ((0, rows - R), (0, 0)))

    def body(c_ref, w_ref, b_ref, o_ref, ca_ref):
        cv = c_ref[...]
        ca16 = (cv * jax.nn.sigmoid(cv)).astype(BF16)
        ca_ref[...] = ca16
        o_ref[...] = jnp.dot(ca16, w_ref[...], preferred_element_type=F32) + b_ref[...]

    out, ca = pl.pallas_call(
        body, grid=(N // tn,),
        in_specs=[pl.BlockSpec((rows, D), lambda j: (0, 0)), pl.BlockSpec((D, tn), lambda j: (0, j)),
                  pl.BlockSpec((1, tn), lambda j: (0, j))],
        out_specs=[pl.BlockSpec((rows, tn), lambda j: (0, j)), pl.BlockSpec((rows, D), lambda j: (0, 0))],
        out_shape=[jax.ShapeDtypeStruct((rows, N), F32), jax.ShapeDtypeStruct((rows, D), BF16)],
        compiler_params=_params("arbitrary"), name=name)(c_pad, w, b)
    return out[0:R], ca


def _norm_mod_fwd(x, g, shift, scale, *, name, ts=512):
    S, D = x.shape
    ts = _tile(S, ts)
    row = pl.BlockSpec((ts, D), lambda i: (i, 0))
    vec = pl.BlockSpec((1, D), lambda i: (0, 0))

    def body(x_ref, g_ref, sh_ref, sc_ref, h_ref):
        xv = x_ref[...]
        r = lax.rsqrt(jnp.mean(xv * xv, axis=-1, keepdims=True) + EPS)
        h_ref[...] = ((xv * r * g_ref[...]) * (1.0 + sc_ref[...]) + sh_ref[...]).astype(BF16)

    return pl.pallas_call(body, grid=(S // ts,), in_specs=[row, vec, vec, vec], out_specs=row,
                          out_shape=jax.ShapeDtypeStruct((S, D), BF16),
                          compiler_params=_params("parallel"), name=name)(x, g, shift, scale)


def _acc_init(step, *refs):
    @pl.when(step == 0)
    def _():
        for r in refs:
            r[...] = jnp.zeros_like(r)


def _colsum(v):
    return jnp.sum(v, axis=0, keepdims=True)


def _norm_mod_bwd(dh, x, g, scale, dres, prev=None, *, name, ts=512):
    S, D = x.shape
    ts = _tile(S, ts)
    row = pl.BlockSpec((ts, D), lambda i: (i, 0))
    vec = pl.BlockSpec((1, D), lambda i: (0, 0))
    has_prev = prev is not None

    def body(*refs):
        if has_prev:
            dh_ref, x_ref, g_ref, sc_ref, dres_ref, y_ref, gate_ref, dx_ref, dg_ref, dsh_ref, dsc_ref, dy_ref, dgate_ref = refs
            _acc_init(pl.program_id(0), dg_ref, dsh_ref, dsc_ref, dgate_ref)
        else:
            dh_ref, x_ref, g_ref, sc_ref, dres_ref, dx_ref, dg_ref, dsh_ref, dsc_ref = refs
            _acc_init(pl.program_id(0), dg_ref, dsh_ref, dsc_ref)
        xv, dhv, gv = x_ref[...], dh_ref[...], g_ref[...]
        r = lax.rsqrt(jnp.mean(xv * xv, axis=-1, keepdims=True) + EPS)
        xh = xv * r
        dsh_ref[...] += _colsum(dhv)
        dsc_ref[...] += _colsum(dhv * (xh * gv))
        dn = dhv * (1.0 + sc_ref[...])
        dg_ref[...] += _colsum(dn * xh)
        dxh = dn * gv
        dx = dres_ref[...] + r * (dxh - xh * jnp.mean(dxh * xh, axis=-1, keepdims=True))
        dx_ref[...] = dx
        if has_prev:
            dy_ref[...] = (gate_ref[...] * dx).astype(BF16)
            dgate_ref[...] += _colsum(dx * y_ref[...].astype(F32))

    ins, in_specs = [dh, x, g, scale, dres], [row, row, vec, vec, row]
    outs = [jax.ShapeDtypeStruct((S, D), F32)] + [jax.ShapeDtypeStruct((1, D), F32)] * 3
    out_specs = [row, vec, vec, vec]
    if has_prev:
        ins += list(prev)
        in_specs += [row, vec]
        outs += [jax.ShapeDtypeStruct((S, D), BF16), jax.ShapeDtypeStruct((1, D), F32)]
        out_specs += [row, vec]
    return pl.pallas_call(body, grid=(S // ts,), in_specs=in_specs, out_specs=out_specs, out_shape=outs,
                          compiler_params=_params("arbitrary"), name=name)(*ins)


def _final_loss(x, g, target, y, gate, *, name, ts=512):
    S, D = x.shape
    ts = _tile(S, ts)
    row = pl.BlockSpec((ts, D), lambda i: (i, 0))
    vec = pl.BlockSpec((1, D), lambda i: (0, 0))
    lvec = pl.BlockSpec((1, LANES), lambda i: (0, 0))

    def body(x_ref, g_ref, t_ref, y_ref, gate_ref, loss_ref, dg_ref, dx_ref, dy_ref, dgate_ref):
        _acc_init(pl.program_id(0), loss_ref, dg_ref, dgate_ref)
        xv, gv = x_ref[...], g_ref[...]
        r = lax.rsqrt(jnp.mean(xv * xv, axis=-1, keepdims=True) + EPS)
        xh = xv * r
        e = xh * gv - t_ref[...]
        loss_ref[...] += 0.5 * jnp.sum(jnp.mean(e * e, axis=-1, keepdims=True), axis=0, keepdims=True)
        dout = e * (1.0 / D)
        dg_ref[...] += _colsum(dout * xh)
        dxh = dout * gv
        dx = r * (dxh - xh * jnp.mean(dxh * xh, axis=-1, keepdims=True))
        dx_ref[...] = dx
        dy_ref[...] = (gate_ref[...] * dx).astype(BF16)
        dgate_ref[...] += _colsum(dx * y_ref[...].astype(F32))

    return pl.pallas_call(
        body, grid=(S // ts,), in_specs=[row, vec, row, row, vec], out_specs=[lvec, vec, row, row, vec],
        out_shape=[jax.ShapeDtypeStruct((1, LANES), F32), jax.ShapeDtypeStruct((1, D), F32),
                   jax.ShapeDtypeStruct((S, D), F32), jax.ShapeDtypeStruct((S, D), BF16),
                   jax.ShapeDtypeStruct((1, D), F32)],
        compiler_params=_params("arbitrary"), name=name)(x, g, target, y, gate)


def _head_mean(v, gmat):
    hi = v.astype(BF16)
    lo = (v - hi.astype(F32)).astype(BF16)
    return jnp.dot(hi, gmat, preferred_element_type=F32) + jnp.dot(lo, gmat, preferred_element_type=F32)


L_F, L_ONE, L_SHIFT = HEAD_DIM, HEAD_DIM + 3, HEAD_DIM + 6
KEY_CHUNKS = 2
DIAG_SPLIT = 4
SHIFT_FREE_LOGIT_BOUND = 60.0


def _lane():
    return lax.broadcasted_iota(jnp.int32, (1, LANES), 1)


def _split3(v):
    p1 = v.astype(BF16).astype(F32)
    r1 = v - p1
    p2 = r1.astype(BF16).astype(F32)
    p3 = (r1 - p2).astype(BF16).astype(F32)
    return p1, p2, p3


def _put3(lane, first, pieces):
    out = jnp.where(lane == first, pieces[0], 0.0)
    for k in (1, 2):
        out = out + jnp.where(lane == first + k, pieces[k], 0.0)
    return out


def _ones3(lane, first):
    return jnp.where((lane >= first) & (lane < first + 3), 1.0, 0.0)


def _lane_col(v, lane, idx):
    return jnp.sum(jnp.where(lane == idx, v, 0.0), axis=-1, keepdims=True)


def _head_of_pair(pair, e, lane):
    return jnp.where(lane < HEAD_DIM, pair if e == 0 else pltpu.roll(pair, HEAD_DIM, 1), 0.0)


def _pair_of_heads(even, odd, lane):
    return jnp.where(lane < HEAD_DIM, even, pltpu.roll(odd, HEAD_DIM, 1))


def _fox_prep_fwd(proj, fcum, qgain, kgain, gmat, *, name, ts=256):
    S = proj.shape[0]
    D = HEADS * HEAD_DIM
    ts = _tile(S, ts)
    scale = HEAD_DIM ** -0.5

    def body(p_ref, f_ref, qg_ref, kg_ref, gm_ref, q_ref, k_ref, v_ref):
        gm, lane, fc = gm_ref[...], _lane(), f_ref[...]
        for cpair in range(HEAD_PAIRS):
            qv = p_ref[:, pl.ds(cpair * LANES, LANES)].astype(F32)
            kv = p_ref[:, pl.ds(D + cpair * LANES, LANES)].astype(F32)
            vv = p_ref[:, pl.ds(2 * D + cpair * LANES, LANES)].astype(F32)
            qn = (qv * lax.rsqrt(_head_mean(qv * qv, gm) + EPS) * qg_ref[...]) * scale
            kn = kv * lax.rsqrt(_head_mean(kv * kv, gm) + EPS) * kg_ref[...]
            for e in range(2):
                h = 2 * cpair + e
                f3 = _split3(_lane_col(fc, lane, h))
                q_ref[h] = (_head_of_pair(qn, e, lane) + _put3(lane, L_F, f3) + _ones3(lane, L_ONE)).astype(BF16)
                k_ref[h] = (_head_of_pair(kn, e, lane) + _ones3(lane, L_F)
                            - _put3(lane, L_ONE, f3) + _ones3(lane, L_SHIFT)).astype(BF16)
                v_ref[h] = (_head_of_pair(vv, e, lane) + _ones3(lane, L_F)).astype(BF16)

    vec = pl.BlockSpec((1, LANES), lambda i: (0, 0))
    wide = pl.BlockSpec((HEADS, ts, LANES), lambda i: (0, i, 0))
    return pl.pallas_call(
        body, grid=(S // ts,),
        in_specs=[pl.BlockSpec((ts, 3 * D), lambda i: (i, 0)), pl.BlockSpec((ts, LANES), lambda i: (i, 0)), vec, vec,
                  pl.BlockSpec((LANES, LANES), lambda i: (0, 0))],
        out_specs=[wide, wide, wide], out_shape=[jax.ShapeDtypeStruct((HEADS, S, LANES), BF16)] * 3,
        compiler_params=_params("parallel"), name=name)(proj, fcum, qgain, kgain, gmat)


def _fox_prep_bwd(proj, dq_aug, dk_aug, dv_aug, dog, qgain, kgain, gmat, *, name, ts=256):
    S = proj.shape[0]
    D = HEADS * HEAD_DIM
    ts = _tile(S, ts)
    scale = HEAD_DIM ** -0.5

    def body(p_ref, dq_ref, dk_ref, dv_ref, dog_ref, qg_ref, kg_ref, gm_ref, o_ref, df_ref, dqg_ref, dkg_ref):
        _acc_init(pl.program_id(0), dqg_ref, dkg_ref)
        gm, lane = gm_ref[...], _lane()
        df = jnp.zeros((ts, LANES), F32)
        for cpair in range(HEAD_PAIRS):
            tiles = []
            for e in range(2):
                h = 2 * cpair + e
                tq, tk = dq_ref[h], dk_ref[h]
                df = jnp.where(lane == h, _lane_col(tq, lane, L_F) - _lane_col(tk, lane, L_ONE), df)
                tiles.append((tq, tk, dv_ref[h].astype(F32)))
            pair = [_pair_of_heads(tiles[0][k], tiles[1][k], lane) for k in range(3)]
            for half, g_ref, dg_ref, mult in ((0, qg_ref, dqg_ref, scale), (1, kg_ref, dkg_ref, 1.0)):
                v = p_ref[:, pl.ds(half * D + cpair * LANES, LANES)].astype(F32)
                r = lax.rsqrt(_head_mean(v * v, gm) + EPS)
                xh = v * r
                dn = pair[half] * mult
                dg_ref[...] += _colsum(dn * xh)
                dxh = dn * g_ref[...]
                o_ref[:, pl.ds(half * D + cpair * LANES, LANES)] = (r * (dxh - xh * _head_mean(dxh * xh, gm))).astype(BF16)
            o_ref[:, pl.ds(2 * D + cpair * LANES, LANES)] = pair[2].astype(BF16)
        o_ref[:, pl.ds(3 * D, D)] = dog_ref[...]
        o_ref[:, pl.ds(4 * D, LANES)] = jnp.zeros((ts, LANES), BF16)
        df_ref[...] = df

    row = pl.BlockSpec((ts, D), lambda i: (i, 0))
    wide = pl.BlockSpec((HEADS, ts, LANES), lambda i: (0, i, 0))
    vec = pl.BlockSpec((1, LANES), lambda i: (0, 0))
    return pl.pallas_call(
        body, grid=(S // ts,),
        in_specs=[pl.BlockSpec((ts, 2 * D), lambda i: (i, 0)), wide, wide, wide, row, vec, vec,
                  pl.BlockSpec((LANES, LANES), lambda i: (0, 0))],
        out_specs=[pl.BlockSpec((ts, 4 * D + LANES), lambda i: (i, 0)), pl.BlockSpec((ts, LANES), lambda i: (i, 0)), vec, vec],
        out_shape=[jax.ShapeDtypeStruct((S, 4 * D + LANES), BF16), jax.ShapeDtypeStruct((S, LANES), F32),
                   jax.ShapeDtypeStruct((1, LANES), F32), jax.ShapeDtypeStruct((1, LANES), F32)],
        compiler_params=_params("arbitrary"), name=name)(proj, dq_aug, dk_aug, dv_aug, dog, qgain, kgain, gmat)


def _log_sigmoid(z):
    return jnp.minimum(z, 0.0) - jnp.log(1.0 + jnp.exp(-jnp.abs(z)))


def _fox_decay_fwd(fl, bf, *, name, tb=256):
    S = fl.shape[0]
    tb = _tile(S, tb)

    def body(fl_ref, b_ref, o_ref, carry):
        @pl.when(pl.program_id(0) == 0)
        def _():
            carry[...] = jnp.zeros_like(carry)
        logf = _log_sigmoid(fl_ref[...] + b_ref[...])
        tri = (lax.broadcasted_iota(jnp.int32, (tb, tb), 1) <= lax.broadcasted_iota(jnp.int32, (tb, tb), 0)).astype(F32)
        cs = jnp.dot(tri, logf, preferred_element_type=F32, precision=lax.Precision.HIGHEST) + carry[...]
        o_ref[...] = cs
        carry[...] = _row_of(cs, tb - 1)

    return pl.pallas_call(
        body, grid=(S // tb,),
        in_specs=[pl.BlockSpec((tb, LANES), lambda i: (i, 0)), pl.BlockSpec((1, LANES), lambda i: (0, 0))],
        out_specs=pl.BlockSpec((tb, LANES), lambda i: (i, 0)),
        out_shape=jax.ShapeDtypeStruct((S, LANES), F32), scratch_shapes=[pltpu.VMEM((1, LANES), F32)],
        compiler_params=_params("arbitrary"), name=name)(fl, bf)


def _fox_decay_bwd(dF, fl, bf, dproj, *, name, tb=256):
    S = fl.shape[0]
    tb = _tile(S, tb)
    n = S // tb
    last_col = dproj.shape[1] // LANES - 1

    def body(df_ref, fl_ref, b_ref, dproj_hbm, o_ref, db_ref, carry):
        del dproj_hbm
        @pl.when(pl.program_id(0) == 0)
        def _():
            carry[...] = jnp.zeros_like(carry)
            db_ref[...] = jnp.zeros_like(db_ref)
        tri = (lax.broadcasted_iota(jnp.int32, (tb, tb), 1) >= lax.broadcasted_iota(jnp.int32, (tb, tb), 0)).astype(F32)
        rc = jnp.dot(tri, df_ref[...], preferred_element_type=F32, precision=lax.Precision.HIGHEST) + carry[...]
        carry[...] = _row_of(rc, 0)
        dfl = rc * jax.nn.sigmoid(-(fl_ref[...] + b_ref[...]))
        o_ref[...] = dfl.astype(BF16)
        db_ref[...] += _colsum(dfl)

    rev = pl.BlockSpec((tb, LANES), lambda i: (n - 1 - i, 0))
    vec = pl.BlockSpec((1, LANES), lambda i: (0, 0))
    return pl.pallas_call(
        body, grid=(n,), in_specs=[rev, rev, vec, pl.BlockSpec(memory_space=pl.ANY)],
        out_specs=[pl.BlockSpec((tb, LANES), lambda i: (n - 1 - i, last_col)), vec],
        out_shape=[jax.ShapeDtypeStruct(dproj.shape, BF16), jax.ShapeDtypeStruct((1, LANES), F32)],
        scratch_shapes=[pltpu.VMEM((1, LANES), F32)], input_output_aliases={3: 0},
        compiler_params=_params("arbitrary"), name=name)(dF, fl, bf, dproj)


_NT = (((1,), (1,)), ((), ()))
_TN = (((0,), (0,)), ((), ()))


def _causal(T, transposed=False):
    r, c = lax.broadcasted_iota(jnp.int32, (T, T), 0), lax.broadcasted_iota(jnp.int32, (T, T), 1)
    return r <= c if transposed else c <= r


def _with_shift(q_tile, shift, lane):
    keep = jnp.where((lane >= L_SHIFT) & (lane < L_SHIFT + 3), 0.0, q_tile)
    return (keep + _put3(lane, L_SHIFT, _split3(-shift))).astype(BF16)


def _ride_along(xchg, n_in, n_out, grid):
    if xchg is None:
        return (lambda body: body), [], [], [], [], []
    arrs, scatter = xchg
    n = len(arrs)

    def wrap(body):
        def wrapped(*refs):
            own_in, x_in = refs[:n_in], refs[n_in:n_in + n]
            own_out, x_out = refs[n_in + n:n_in + n + n_out], refs[n_in + n + n_out:n_in + 2 * n + n_out]
            rest = refs[n_in + 2 * n + n_out:]
            own_scratch, sems = rest[:len(rest) - 3], rest[len(rest) - 3:]
            ids = [pl.program_id(d) for d in range(len(grid))]
            first = functools.reduce(jnp.logical_and, [i == 0 for i in ids])
            last = functools.reduce(jnp.logical_and, [i == g - 1 for i, g in zip(ids, grid)])

            @pl.when(first)
            def _():
                for cp in _xchg_copies(x_in, x_out, scatter, *sems):
                    cp.start()

            body(*own_in, *own_out, *own_scratch)

            @pl.when(last)
            def _():
                for cp in _xchg_copies(x_in, x_out, scatter, *sems):
                    cp.wait()

        return wrapped

    return wrap, [_HBM] * n, [_HBM] * n, _xchg_out_shapes(arrs, scatter), _xchg_sems(n), list(arrs)


def _attn_rowmax(q_aug, k_aug, *, name, T=1024):
    S = q_aug.shape[1]
    T = _tile(S, T)
    n = S // T

    def body(q_ref, k_ref, o_ref, m_s):
        i, j = pl.program_id(1), pl.program_id(2)

        @pl.when(j == 0)
        def _():
            m_s[...] = jnp.full_like(m_s, NEG)

        def step(diag):
            s = lax.dot_general(q_ref[...], k_ref[...], _NT, preferred_element_type=F32)
            if diag:
                s = jnp.where(_causal(T), s, NEG)
            m = m_s[...]
            for cb in range(T // LANES):
                m = jnp.maximum(m, s[:, cb * LANES:(cb + 1) * LANES])
            m_s[...] = m

        @pl.when(j < i)
        def _():
            step(False)

        @pl.when(j == i)
        def _():
            step(True)
            o_ref[...] = _with_shift(q_ref[...].astype(F32), jnp.max(m_s[...], axis=-1, keepdims=True), _lane())

    qrow = pl.BlockSpec((None, T, LANES), lambda h, i, j: (h, i, 0))
    return pl.pallas_call(
        body, grid=(HEADS, n, n),
        in_specs=[qrow, pl.BlockSpec((None, T, LANES), lambda h, i, j: (h, jnp.minimum(j, i), 0))],
        out_specs=qrow, out_shape=jax.ShapeDtypeStruct(q_aug.shape, BF16),
        scratch_shapes=[pltpu.VMEM((T, LANES), F32)],
        compiler_params=_params("parallel", "parallel", "arbitrary"), name=name)(q_aug, k_aug)


def _attn_fwd(q_max, k_aug, v_aug, xchg=None, *, name, T=1024):
    S = q_max.shape[1]
    T = _tile(S, T)
    n = S // T
    wrap, x_in, x_out, x_shapes, x_sems, x_ops = _ride_along(xchg, 3, 2, (HEADS, n, n))

    def body(q_ref, k_ref, v_ref, o_ref, qb_ref, acc_s):
        i, j = pl.program_id(1), pl.program_id(2)

        @pl.when(j == 0)
        def _():
            acc_s[...] = jnp.zeros_like(acc_s)

        def block(rows, cols, mask):
            s = lax.dot_general(q_ref[rows, :], k_ref[cols, :], _NT, preferred_element_type=F32)
            if mask is not None:
                s = jnp.where(mask, s, NEG)
            return jnp.dot(jnp.exp(s).astype(BF16), v_ref[cols, :], preferred_element_type=F32)

        @pl.when(j < i)
        def _():
            chunk = T // KEY_CHUNKS
            upd = block(pl.ds(0, T), pl.ds(0, chunk), None)
            for c in range(1, KEY_CHUNKS):
                upd = upd + block(pl.ds(0, T), pl.ds(c * chunk, chunk), None)
            acc_s[...] += upd

        @pl.when(j == i)
        def _():
            sub = T // DIAG_SPLIT
            for r in range(DIAG_SPLIT):
                rows = pl.ds(r * sub, sub)
                upd = block(rows, rows, _causal(sub))
                if r:
                    upd = upd + block(rows, pl.ds(0, r * sub), None)
                acc_s[rows, :] += upd
            lane = _lane()
            acc = acc_s[...]
            l = _lane_col(acc, lane, L_F)
            o_ref[...] = acc / l
            qf = q_ref[...].astype(F32)
            row_max = -jnp.sum(jnp.where((lane >= L_SHIFT) & (lane < L_SHIFT + 3), qf, 0.0), axis=-1, keepdims=True)
            qb_ref[...] = _with_shift(qf, row_max + jnp.log(l), lane)

    qrow = pl.BlockSpec((None, T, LANES), lambda h, i, j: (h, i, 0))
    kv = pl.BlockSpec((None, T, LANES), lambda h, i, j: (h, jnp.minimum(j, i), 0))
    outs = pl.pallas_call(
        wrap(body), grid=(HEADS, n, n), in_specs=[qrow, kv, kv] + x_in, out_specs=[qrow, qrow] + x_out,
        out_shape=[jax.ShapeDtypeStruct(q_max.shape, F32), jax.ShapeDtypeStruct(q_max.shape, BF16)] + x_shapes,
        scratch_shapes=[pltpu.VMEM((T, LANES), F32)] + x_sems,
        compiler_params=_params("arbitrary", "arbitrary", "arbitrary"), name=name)(q_max, k_aug, v_aug, *x_ops)
    return outs[0], outs[1], outs[2:]


def _attn_bwd(q_lse, k_aug, v_aug, do_aug, xchg=None, *, name, T=1024):
    S = q_lse.shape[1]
    T = _tile(S, T)
    n = S // T
    wrap, x_in, x_out, x_shapes, x_sems, x_ops = _ride_along(xchg, 4, 3, (HEADS, n, n))

    def body(q_ref, do_ref, k_ref, v_ref, dq_ref, dk_ref, dv_ref, dq_s, dk_s, dv_s):
        j, i = pl.program_id(1), pl.program_id(2)

        def block(keys, queries, mask):
            q, do, k, v = q_ref[queries, :], do_ref[queries, :], k_ref[keys, :], v_ref[keys, :]
            st = lax.dot_general(k, q, _NT, preferred_element_type=F32)
            if mask is not None:
                st = jnp.where(mask, st, NEG)
            pt = jnp.exp(st)
            dst = (pt * lax.dot_general(v, do, _NT, preferred_element_type=F32)).astype(BF16)
            dv_s[keys, :] += jnp.dot(pt.astype(BF16), do, preferred_element_type=F32)
            dk_s[keys, :] += jnp.dot(dst, q, preferred_element_type=F32)
            return lax.dot_general(dst, k, _TN, preferred_element_type=F32)

        @pl.when(i == j)
        def _():
            dk_s[...] = jnp.zeros_like(dk_s)
            dv_s[...] = jnp.zeros_like(dv_s)

            @pl.when(j == 0)
            def _():
                dq_s[i] = jnp.zeros((T, LANES), F32)

            sub = T // DIAG_SPLIT
            for c in range(DIAG_SPLIT):
                keys = pl.ds(c * sub, sub)
                dq_s[i, keys, :] += block(keys, keys, _causal(sub, transposed=True))
                if c < DIAG_SPLIT - 1:
                    later = pl.ds((c + 1) * sub, (DIAG_SPLIT - 1 - c) * sub)
                    dq_s[i, later, :] += block(keys, later, None)
            dq_ref[...] = dq_s[j]

        @pl.when(i > j)
        def _():
            chunk = T // KEY_CHUNKS
            upd = block(pl.ds(0, chunk), pl.ds(0, T), None)
            for c in range(1, KEY_CHUNKS):
                upd = upd + block(pl.ds(c * chunk, chunk), pl.ds(0, T), None)

            @pl.when(j == 0)
            def _():
                dq_s[i] = upd

            @pl.when(j > 0)
            def _():
                dq_s[i] += upd

        @pl.when(i == n - 1)
        def _():
            dk_ref[...] = dk_s[...]
            dv_ref[...] = dv_s[...].astype(BF16)

    qrow = pl.BlockSpec((None, T, LANES), lambda h, j, i: (h, jnp.maximum(i, j), 0))
    kv = pl.BlockSpec((None, T, LANES), lambda h, j, i: (h, j, 0))
    outs = pl.pallas_call(
        wrap(body), grid=(HEADS, n, n), in_specs=[qrow, qrow, kv, kv] + x_in, out_specs=[kv, kv, kv] + x_out,
        out_shape=[jax.ShapeDtypeStruct(q_lse.shape, F32), jax.ShapeDtypeStruct(q_lse.shape, F32),
                   jax.ShapeDtypeStruct(q_lse.shape, BF16)] + x_shapes,
        scratch_shapes=[pltpu.VMEM((n, T, LANES), F32), pltpu.VMEM((T, LANES), F32), pltpu.VMEM((T, LANES), F32)] + x_sems,
        compiler_params=_params("arbitrary", "arbitrary", "arbitrary"), name=name)(q_lse, do_aug, k_aug, v_aug, *x_ops)
    return outs[0], outs[1], outs[2], outs[3:]


def _fox_gate_fwd(att_aug, proj, *, name, ts=256):
    S = att_aug.shape[1]
    D = HEADS * HEAD_DIM
    ts = _tile(S, ts)

    def body(a_ref, o_ref, att_ref, out_ref):
        lane = _lane()
        for cpair in range(HEAD_PAIRS):
            cols = pl.ds(cpair * LANES, LANES)
            pair = _pair_of_heads(a_ref[2 * cpair], a_ref[2 * cpair + 1], lane)
            att_ref[:, cols] = pair
            out_ref[:, cols] = (pair * jax.nn.sigmoid(o_ref[:, cols].astype(F32))).astype(BF16)

    row = pl.BlockSpec((ts, D), lambda i: (i, 0))
    return pl.pallas_call(
        body, grid=(S // ts,),
        in_specs=[pl.BlockSpec((HEADS, ts, LANES), lambda i: (0, i, 0)), pl.BlockSpec((ts, D), lambda i: (i, 3))],
        out_specs=[row, row], out_shape=[jax.ShapeDtypeStruct((S, D), F32), jax.ShapeDtypeStruct((S, D), BF16)],
        compiler_params=_params("parallel"), name=name)(att_aug, proj)


def _fox_gate_bwd(da, att, proj, *, name, ts=256):
    S, D = att.shape
    ts = _tile(S, ts)

    def body(da_ref, a_ref, o_ref, do_ref, dog_ref):
        lane = _lane()
        for cpair in range(HEAD_PAIRS):
            cols = pl.ds(cpair * LANES, LANES)
            dav, av = da_ref[:, cols].astype(F32), a_ref[:, cols]
            sg = jax.nn.sigmoid(o_ref[:, cols].astype(F32))
            datt = (dav * sg).astype(BF16).astype(F32)
            dog_ref[:, cols] = (dav * av * sg * (1.0 - sg)).astype(BF16)
            prod = datt * av
            for e in range(2):
                in_head = (lane < HEAD_DIM) if e == 0 else (lane >= HEAD_DIM)
                delta = jnp.sum(jnp.where(in_head, prod, 0.0), axis=-1, keepdims=True)
                tile = _head_of_pair(datt, e, lane) + _put3(lane, L_F, _split3(-delta))
                do_ref[2 * cpair + e] = tile.astype(BF16)

    row = pl.BlockSpec((ts, D), lambda i: (i, 0))
    return pl.pallas_call(
        body, grid=(S // ts,), in_specs=[row, row, pl.BlockSpec((ts, D), lambda i: (i, 3))],
        out_specs=[pl.BlockSpec((HEADS, ts, LANES), lambda i: (0, i, 0)), row],
        out_shape=[jax.ShapeDtypeStruct((HEADS, S, LANES), BF16), jax.ShapeDtypeStruct((S, D), BF16)],
        compiler_params=_params("parallel"), name=name)(da, att, proj)


def _row_of(block, r):
    rows = lax.broadcasted_iota(jnp.int32, block.shape, 0)
    return jnp.sum(jnp.where(rows == r, block, 0.0), axis=0, keepdims=True)


def _shift_down(cur, tail, k):
    out = pltpu.roll(cur, k, 0)
    top = out[:SUBLANES]
    rows = lax.broadcasted_iota(jnp.int32, top.shape, 0)
    for r in range(k):
        top = jnp.where(rows == r, _row_of(tail, tail.shape[0] - k + r), top)
    return jnp.concatenate([top, out[SUBLANES:]], axis=0)


def _shift_up(cur, head, k):
    n = cur.shape[0]
    out = pltpu.roll(cur, n - k, 0)
    bottom = out[n - SUBLANES:]
    rows = lax.broadcasted_iota(jnp.int32, bottom.shape, 0)
    for r in range(k):
        bottom = jnp.where(rows == SUBLANES - k + r, _row_of(head, r), bottom)
    return jnp.concatenate([out[:n - SUBLANES], bottom], axis=0)


HALO = 16


CONV_TC = 1408


def _pair_tiles(v):
    nc = v.shape[-1] // (2 * CONV_TC)
    return jnp.swapaxes(v.reshape(v.shape[:-1] + (2, nc, CONV_TC)), -3, -2).reshape(v.shape)


def _unpair_tiles(v):
    nc = v.shape[-1] // (2 * CONV_TC)
    return jnp.swapaxes(v.reshape(v.shape[:-1] + (nc, 2, CONV_TC)), -3, -2).reshape(v.shape)


def _conv_rows(cur, tail, w_ref, b_ref, cols):
    a1, a2 = _shift_down(cur, tail, 1), _shift_down(cur, tail, 2)
    return a2 * w_ref[0:1, cols] + a1 * w_ref[1:2, cols] + cur * w_ref[2:3, cols] + b_ref[:, cols], (a2, a1, cur)


def _conv_gate_fwd(a, cw, cb, *, name, ts=512):
    S, F2 = a.shape
    tc = CONV_TC
    ts = _tile(S, ts)
    nc = F2 // (2 * tc)
    sub = ts // HALO
    halves = (pl.ds(0, tc), pl.ds(tc, tc))

    def body(a_ref, t_ref, w_ref, b_ref, o_ref):
        first = pl.program_id(1) == 0
        pre = []
        for cols in halves:
            tail = jnp.where(first, 0.0, t_ref[:, cols].astype(F32))
            pre.append(_conv_rows(a_ref[:, cols].astype(F32), tail, w_ref, b_ref, cols)[0])
        g, val = pre
        o_ref[...] = (g * jax.nn.sigmoid(g) * val).astype(BF16)

    return pl.pallas_call(
        body, grid=(nc, S // ts),
        in_specs=[pl.BlockSpec((ts, 2 * tc), lambda j, i: (i, j)),
                  pl.BlockSpec((HALO, 2 * tc), lambda j, i: (jnp.maximum(i * sub - 1, 0), j)),
                  pl.BlockSpec((CONV_WIDTH, 2 * tc), lambda j, i: (0, j)), pl.BlockSpec((1, 2 * tc), lambda j, i: (0, j))],
        out_specs=pl.BlockSpec((ts, tc), lambda j, i: (i, j)),
        out_shape=jax.ShapeDtypeStruct((S, F2 // 2), BF16),
        compiler_params=_params("parallel", "parallel"), name=name)(a, a, cw, cb)


def _conv_gate_bwd(a, dact, cw, cb, *, name, ts=512):
    S, F2 = a.shape
    tc = CONV_TC
    ts = _tile(S, ts)
    nc = F2 // (2 * tc)
    sub = ts // HALO
    n_rows = S // ts
    halves = (pl.ds(0, tc), pl.ds(tc, tc))

    def body(a_ref, at_ref, ah_ref, d_ref, dh_ref, w_ref, b_ref, da_ref, s_ref):
        i = pl.program_id(1)
        _acc_init(i, s_ref)

        def dpre_of(rows, tails, d):
            (g, taps_g), (val, taps_v) = [_conv_rows(rows[h], tails[h], w_ref, b_ref, halves[h]) for h in range(2)]
            sg = jax.nn.sigmoid(g)
            return (d * val * (sg * (1.0 + g * (1.0 - sg))), d * (g * sg)), (taps_g, taps_v)

        cur = [a_ref[:, c].astype(F32) for c in halves]
        tail = [jnp.where(i == 0, 0.0, at_ref[:, c].astype(F32)) for c in halves]
        dpre, taps = dpre_of(cur, tail, d_ref[...].astype(F32))
        head, _ = dpre_of([ah_ref[:, c].astype(F32) for c in halves], [x[ts - HALO:, :] for x in cur], dh_ref[...].astype(F32))
        for h, cols in enumerate(halves):
            dd = dpre[h]
            nxt = jnp.where(i == n_rows - 1, 0.0, head[h])
            da_ref[:, cols] = (dd * w_ref[2:3, cols] + _shift_up(dd, nxt, 1) * w_ref[1:2, cols]
                               + _shift_up(dd, nxt, 2) * w_ref[0:1, cols]).astype(BF16)
            for r in range(CONV_WIDTH):
                s_ref[r:r + 1, cols] += _colsum(dd * taps[h][r])
            s_ref[CONV_WIDTH:CONV_WIDTH + 1, cols] += _colsum(dd)

    nxt_rows = lambda i: jnp.minimum((i + 1) * sub, S // HALO - 1)
    return pl.pallas_call(
        body, grid=(nc, n_rows),
        in_specs=[pl.BlockSpec((ts, 2 * tc), lambda j, i: (i, j)),
                  pl.BlockSpec((HALO, 2 * tc), lambda j, i: (jnp.maximum(i * sub - 1, 0), j)),
                  pl.BlockSpec((HALO, 2 * tc), lambda j, i: (nxt_rows(i), j)),
                  pl.BlockSpec((ts, tc), lambda j, i: (i, j)), pl.BlockSpec((HALO, tc), lambda j, i: (nxt_rows(i), j)),
                  pl.BlockSpec((CONV_WIDTH, 2 * tc), lambda j, i: (0, j)), pl.BlockSpec((1, 2 * tc), lambda j, i: (0, j))],
        out_specs=[pl.BlockSpec((ts, 2 * tc), lambda j, i: (i, j)), pl.BlockSpec((8, 2 * tc), lambda j, i: (0, j))],
        out_shape=[jax.ShapeDtypeStruct((S, F2), BF16), jax.ShapeDtypeStruct((8, F2), F32)],
        compiler_params=_params("parallel", "arbitrary"), name=name)(a, a, a, dact, dact, cw, cb)


def _gelu_parts(z):
    z2 = z * z
    t = jnp.tanh(GELU_C0 * (z + GELU_C1 * z * z2))
    val = 0.5 * z * (1.0 + t)
    grad = 0.5 * (1.0 + t) + 0.5 * z * (1.0 - t * t) * GELU_C0 * (1.0 + 3.0 * GELU_C1 * z2)
    return val, grad


def _sgu_fwd(pre, b_in, vgain, vbias, wm, bsb, *, name, ts=256):
    S, W2 = pre.shape
    W = W2 // 2
    gd = W // SGU_GROUPS
    ts = _tile(S, ts)

    def body(p_ref, b_ref, vg_ref, vb_ref, wm_ref, bs_ref, y_ref):
        u = _gelu_parts(p_ref[:, pl.ds(0, W)].astype(F32) + b_ref[:, pl.ds(0, W)])[0]
        v = _gelu_parts(p_ref[:, pl.ds(W, W)].astype(F32) + b_ref[:, pl.ds(W, W)])[0]
        mu = jnp.mean(v, axis=-1, keepdims=True)
        vc = v - mu
        rstd = lax.rsqrt(jnp.mean(vc * vc, axis=-1, keepdims=True) + EPS)
        vn = ((vc * rstd) * vg_ref[...] + vb_ref[...]).astype(BF16)
        for blk in range(ts // SGU_BLOCK):
            r0 = blk * SGU_BLOCK
            for g in range(SGU_GROUPS):
                c0 = g * gd
                mixed = jnp.dot(wm_ref[g], vn[r0:r0 + SGU_BLOCK, c0:c0 + gd], preferred_element_type=F32) + bs_ref[g]
                y_ref[pl.ds(r0, SGU_BLOCK), pl.ds(c0, gd)] = (u[r0:r0 + SGU_BLOCK, c0:c0 + gd] * mixed).astype(BF16)

    full = lambda shape: pl.BlockSpec(shape, lambda i: (0,) * len(shape))
    return pl.pallas_call(
        body, grid=(S // ts,),
        in_specs=[pl.BlockSpec((ts, W2), lambda i: (i, 0)), full((1, W2)), full((1, W)), full((1, W)),
                  full((SGU_GROUPS, SGU_BLOCK, SGU_BLOCK)), full((SGU_GROUPS, SGU_BLOCK, gd))],
        out_specs=pl.BlockSpec((ts, W), lambda i: (i, 0)), out_shape=jax.ShapeDtypeStruct((S, W), BF16),
        compiler_params=_params("parallel"), name=name)(pre, b_in, vgain, vbias, wm, bsb)


def _sgu_bwd(pre, dy, b_in, vgain, vbias, wm, wmt, bsb, *, name, ts=256):
    S, W2 = pre.shape
    W = W2 // 2
    gd = W // SGU_GROUPS
    ts = _tile(S, ts)
    last = S // ts - 1

    def body(p_ref, dy_ref, b_ref, vg_ref, vb_ref, wm_ref, wmt_ref, bs_ref,
             dp_ref, db_ref, dvg_ref, dvb_ref, dws_ref, dbs_ref, du_s, dvn_s, dbs_s):
        step = pl.program_id(0)
        _acc_init(step, db_ref, dvg_ref, dvb_ref, dws_ref, dbs_s)
        u, gu = _gelu_parts(p_ref[:, pl.ds(0, W)].astype(F32) + b_ref[:, pl.ds(0, W)])
        v, gv = _gelu_parts(p_ref[:, pl.ds(W, W)].astype(F32) + b_ref[:, pl.ds(W, W)])
        mu = jnp.mean(v, axis=-1, keepdims=True)
        vc = v - mu
        rstd = lax.rsqrt(jnp.mean(vc * vc, axis=-1, keepdims=True) + EPS)
        vhat = vc * rstd
        vn = (vhat * vg_ref[...] + vb_ref[...]).astype(BF16)
        dyv = dy_ref[...].astype(F32)
        for blk in range(ts // SGU_BLOCK):
            r0 = blk * SGU_BLOCK
            for g in range(SGU_GROUPS):
                c0 = g * gd
                vn_g = vn[r0:r0 + SGU_BLOCK, c0:c0 + gd]
                dy_g = dyv[r0:r0 + SGU_BLOCK, c0:c0 + gd]
                mixed = jnp.dot(wm_ref[g], vn_g, preferred_element_type=F32) + bs_ref[g]
                dmix = dy_g * u[r0:r0 + SGU_BLOCK, c0:c0 + gd]
                dmix_b = dmix.astype(BF16)
                du_s[pl.ds(r0, SGU_BLOCK), pl.ds(c0, gd)] = dy_g * mixed
                dvn_s[pl.ds(r0, SGU_BLOCK), pl.ds(c0, gd)] = jnp.dot(wmt_ref[g], dmix_b, preferred_element_type=F32)
                dws_ref[g] += lax.dot_general(dmix_b, vn_g, _NT, preferred_element_type=F32)
                dbs_s[g] += dmix
        dvn = dvn_s[...]
        dvg_ref[...] += _colsum(dvn * vhat)
        dvb_ref[...] += _colsum(dvn)
        dvh = dvn * vg_ref[...]
        dv = rstd * (dvh - jnp.mean(dvh, axis=-1, keepdims=True) - vhat * jnp.mean(dvh * vhat, axis=-1, keepdims=True))
        dpu = du_s[...] * gu
        dpv = dv * gv
        dp_ref[:, pl.ds(0, W)] = dpu.astype(BF16)
        dp_ref[:, pl.ds(W, W)] = dpv.astype(BF16)
        db_ref[:, pl.ds(0, W)] += _colsum(dpu)
        db_ref[:, pl.ds(W, W)] += _colsum(dpv)

        @pl.when(step == last)
        def _():
            for g in range(SGU_GROUPS):
                dbs_ref[g] = jnp.broadcast_to(jnp.sum(dbs_s[g], axis=-1, keepdims=True), (SGU_BLOCK, SGU_BLOCK))

    full = lambda shape: pl.BlockSpec(shape, lambda i: (0,) * len(shape))
    gsq = (SGU_GROUPS, SGU_BLOCK, SGU_BLOCK)
    return pl.pallas_call(
        body, grid=(S // ts,),
        in_specs=[pl.BlockSpec((ts, W2), lambda i: (i, 0)), pl.BlockSpec((ts, W), lambda i: (i, 0)),
                  full((1, W2)), full((1, W)), full((1, W)), full(gsq), full(gsq), full((SGU_GROUPS, SGU_BLOCK, gd))],
        out_specs=[pl.BlockSpec((ts, W2), lambda i: (i, 0)), full((1, W2)), full((1, W)), full((1, W)), full(gsq), full(gsq)],
        out_shape=[jax.ShapeDtypeStruct((S, W2), BF16), jax.ShapeDtypeStruct((1, W2), F32),
                   jax.ShapeDtypeStruct((1, W), F32), jax.ShapeDtypeStruct((1, W), F32),
                   jax.ShapeDtypeStruct(gsq, F32), jax.ShapeDtypeStruct(gsq, F32)],
        scratch_shapes=[pltpu.VMEM((ts, W), F32), pltpu.VMEM((ts, W), F32), pltpu.VMEM((SGU_GROUPS, SGU_BLOCK, gd), F32)],
        compiler_params=_params("arbitrary"), name=name)(pre, dy, b_in, vgain, vbias, wm, wmt, bsb)


def _paired_to_natural(w_up):
    nc = w_up.shape[1] // (2 * CONV_TC)
    return lambda q: (q % 2) * nc + q // 2


def _ffn_fwd(x, mods, n2g, w_up, cw, cb, w_down, tag):
    sh, sc, gate = mods
    h = _norm_mod_fwd(x, n2g, sh, sc, name=f"{tag}_norm_fwd")
    a = _mm(h, w_up, out_dtype=BF16, tn=CONV_TC, b_n=_paired_to_natural(w_up), name=f"{tag}_up")
    act = _conv_gate_fwd(a, cw, cb, name=f"{tag}_conv_fwd")
    x_out, y = _mm(act, w_down, tk=1408, res=(x, gate), name=f"{tag}_down")
    return x_out, (x, h, a, act, y)


def _ffn_bwd(dy, saved, mods, n2g, w_up, cw, cb, w_down, dres, prev, tag):
    x, h, a, act, _ = saved
    sh, sc, gate = mods
    dact = _mm(dy, w_down, tb=True, out_dtype=BF16, tn=1408, name=f"{tag}_down_dx")
    dw_down = _mm(act, dy, ta=True, out_dtype=BF16, tm=1408, name=f"{tag}_down_dw")
    da, sums = _conv_gate_bwd(a, dact, cw, cb, name=f"{tag}_conv_bwd")
    dh = _mm(da, w_up, tb=True, tk=CONV_TC, b_k=_paired_to_natural(w_up), name=f"{tag}_up_dx")
    dw_up = _mm(h, da, ta=True, out_dtype=BF16, tn=CONV_TC, o_n=_paired_to_natural(w_up), name=f"{tag}_up_dw")
    outs = _norm_mod_bwd(dh, x, n2g, sc, dres, prev, name=f"{tag}_norm_bwd")
    sums = _unpair_tiles(sums)
    return outs, dict(w_up=dw_up, w_down=dw_down, conv_w=sums[0:CONV_WIDTH], conv_b=sums[CONV_WIDTH])


def _local_step(x, target, w, mods, late=None, early=None, last=None):
    S, D = x.shape
    lane = jnp.arange(LANES)
    gmat = jnp.where((lane[:, None] // HEAD_DIM) == (lane[None, :] // HEAD_DIM), 1.0 / HEAD_DIM, 0.0).astype(BF16)
    qg2 = jnp.tile(w["fox_q_gain"].reshape(1, HEAD_DIM), (1, 2))
    kg2 = jnp.tile(w["fox_k_gain"].reshape(1, HEAD_DIM), (1, 2))
    bf_pad = jnp.pad(w["fox_b_f"].reshape(1, HEADS), ((0, 0), (0, LANES - HEADS)))
    w_in_pad = jnp.pad(w["fox_w_in"], ((0, 0), (0, 4 * D + LANES - w["fox_w_in"].shape[1])))
    w_qkvo, w_f = w_in_pad[:, :4 * D], w_in_pad[:, 4 * D:]
    tpos = jnp.arange(SGU_BLOCK)
    smask = (tpos[None, :] // SGU_CHUNK) <= (tpos[:, None] // SGU_CHUNK)
    wm32 = jnp.where(smask[None], w["sgu_w_s"], 0.0)
    wm, wmt = wm32.astype(BF16), jnp.swapaxes(wm32, 1, 2).astype(BF16)
    gd = w["sgu_v_gain"].shape[-1] // SGU_GROUPS
    bsb = jnp.broadcast_to(w["sgu_b_s"][:, :, None], (SGU_GROUPS, SGU_BLOCK, gd))
    vec = lambda v: v.reshape(1, -1)

    sh1, sc1, g1 = mods[0][0:3]
    h0 = _norm_mod_fwd(x, vec(w["norm1_g"][0]), sh1, sc1, name="fox_norm_fwd")
    proj = _mm(h0, w_qkvo, out_dtype=BF16, name="fox_proj")
    fl = _mm(h0, w_f, name="fox_forget_proj")
    fcum = _fox_decay_fwd(fl, bf_pad, name="fox_decay")
    q_aug, k_aug, v_aug = _fox_prep_fwd(proj, fcum, qg2, kg2, gmat, name="fox_qk_norm")
    logit_bound = 8.0 * jnp.max(jnp.abs(w["fox_q_gain"])) * jnp.max(jnp.abs(w["fox_k_gain"]))
    q_max = lax.cond(logit_bound <= SHIFT_FREE_LOGIT_BOUND, lambda: q_aug,
                     lambda: _attn_rowmax(q_aug, k_aug, name="fox_attn_rowmax"))
    xchg = None if late is None else (late[0], [False] * len(late[0]))
    att_aug, q_lse, gathered = _attn_fwd(q_max, k_aug, v_aug, xchg, name="fox_attn_fwd", T=2048)
    if late is not None:
        w = {**w, **late[1](gathered)}
    w = dict(w, ffn_conv_w=_pair_tiles(w["ffn_conv_w"]), ffn_conv_b=_pair_tiles(w["ffn_conv_b"]))
    att, ag = _fox_gate_fwd(att_aug, proj, name="fox_gate_fwd")
    x1, y_fox = _mm(ag, w["fox_w_out"], res=(x, g1), name="fox_out")
    x2, ffn0 = _ffn_fwd(x1, mods[0][3:6], vec(w["norm2_g"][0]), w["ffn_w_up"][0], w["ffn_conv_w"][0],
                        vec(w["ffn_conv_b"][0]), w["ffn_w_down"][0], "ffn0")

    sh1b, sc1b, g1b = mods[1][0:3]
    h1 = _norm_mod_fwd(x2, vec(w["norm1_g"][1]), sh1b, sc1b, name="sgu_norm_fwd")
    pre = _mm(h1, w["sgu_w_in"], out_dtype=BF16, name="sgu_in")
    b_in, vg, vb = vec(w["sgu_b_in"]), vec(w["sgu_v_gain"]), vec(w["sgu_v_bias"])
    ys = _sgu_fwd(pre, b_in, vg, vb, wm, bsb, name="sgu_core_fwd")
    x3, y_sgu = _mm(ys, w["sgu_w_out"], res=(x2, g1b), name="sgu_out")
    x4, ffn1 = _ffn_fwd(x3, mods[1][3:6], vec(w["norm2_g"][1]), w["ffn_w_up"][1], w["ffn_conv_w"][1],
                        vec(w["ffn_conv_b"][1]), w["ffn_w_down"][1], "ffn1")

    loss, d_final_g, dx4, dy_ffn1, dgate_ffn1 = _final_loss(x4, vec(w["final_g"]), target, ffn1[4], mods[1][5], name="final_loss")

    (dx3, dn2g_1, dsh2_1, dsc2_1, dy_sgu, dgate_sgu), g_ffn1 = _ffn_bwd(
        dy_ffn1, ffn1, mods[1][3:6], vec(w["norm2_g"][1]), w["ffn_w_up"][1], w["ffn_conv_w"][1], vec(w["ffn_conv_b"][1]),
        w["ffn_w_down"][1], dx4, (y_sgu, g1b), "ffn1")

    dys = _mm(dy_sgu, w["sgu_w_out"], tb=True, out_dtype=BF16, name="sgu_out_dx")
    dw_sgu_out = _mm(ys, dy_sgu, ta=True, out_dtype=BF16, name="sgu_out_dw")
    dpre, db_in, dvg, dvb, dws, dbs = _sgu_bwd(pre, dys, b_in, vg, vb, wm, wmt, bsb, name="sgu_core_bwd")
    dh1 = _mm(dpre, w["sgu_w_in"], tb=True, name="sgu_in_dx")
    dw_sgu_in = _mm(h1, dpre, ta=True, out_dtype=BF16, name="sgu_in_dw")
    dx2, dn1g_1, dsh1_1, dsc1_1, dy_ffn0, dgate_ffn0 = _norm_mod_bwd(
        dh1, x2, vec(w["norm1_g"][1]), sc1b, dx3, (ffn0[4], mods[0][5]), name="sgu_norm_bwd")

    (dx1, dn2g_0, dsh2_0, dsc2_0, dy_fox, dgate_fox), g_ffn0 = _ffn_bwd(
        dy_ffn0, ffn0, mods[0][3:6], vec(w["norm2_g"][0]), w["ffn_w_up"][0], w["ffn_conv_w"][0], vec(w["ffn_conv_b"][0]),
        w["ffn_w_down"][0], dx2, (y_fox, g1), "ffn0")

    dag = _mm(dy_fox, w["fox_w_out"], tb=True, out_dtype=BF16, name="fox_out_dx")
    dw_fox_out = _mm(ag, dy_fox, ta=True, out_dtype=BF16, name="fox_out_dw")
    do_aug, dog = _fox_gate_bwd(dag, att, proj, name="fox_gate_bwd")
    grads = dict(
        sgu_w_in=dw_sgu_in, sgu_b_in=db_in[0], sgu_v_gain=dvg[0], sgu_v_bias=dvb[0],
        sgu_w_s=jnp.where(smask[None], dws, 0.0), sgu_b_s=dbs[:, :, 0], sgu_w_out=dw_sgu_out,
        ffn_w_up=jnp.stack([g_ffn0["w_up"], g_ffn1["w_up"]]),
        ffn_conv_w=jnp.stack([g_ffn0["conv_w"], g_ffn1["conv_w"]]),
        ffn_conv_b=jnp.stack([g_ffn0["conv_b"], g_ffn1["conv_b"]]),
        ffn_w_down=jnp.stack([g_ffn0["w_down"], g_ffn1["w_down"]]),
        final_g=d_final_g[0], fox_w_out=dw_fox_out, norm2_g=jnp.concatenate([dn2g_0, dn2g_1], axis=0),
    )
    xchg = None if early is None else early(grads)
    dq_aug, dk_aug, dv_aug, exchanged = _attn_bwd(q_lse, k_aug, v_aug, do_aug, xchg, name="fox_attn_bwd", T=2048)
    dproj, dF, dqg, dkg = _fox_prep_bwd(proj, dq_aug, dk_aug, dv_aug, dog, qg2, kg2, gmat, name="fox_qk_norm_bwd")
    dproj, dbf = _fox_decay_bwd(dF, fl, bf_pad, dproj, name="fox_decay_bwd")
    dw_fox_in = _mm(h0, dproj, ta=True, out_dtype=BF16, tn=1408, name="fox_proj_dw")[:, :w["fox_w_in"].shape[1]]
    xchg = None if last is None else last(dict(fox_w_in=dw_fox_in))
    dh0 = _mm(dproj, w_in_pad, tb=True, tk=1408, xchg=xchg, name="fox_proj_dx")
    dh0, exchanged_last = dh0 if last is not None else (dh0, [])
    dx0, dn1g_0, dsh1_0, dsc1_0 = _norm_mod_bwd(dh0, x, vec(w["norm1_g"][0]), sc1, dx1, None, name="fox_norm_bwd")

    dmod0 = jnp.concatenate([dsh1_0, dsc1_0, dgate_fox, dsh2_0, dsc2_0, dgate_ffn0], axis=1)
    dmod1 = jnp.concatenate([dsh1_1, dsc1_1, dgate_sgu, dsh2_1, dsc2_1, dgate_ffn1], axis=1)
    grads.update(
        fox_w_in=dw_fox_in,
        fox_b_f=dbf[0, :HEADS],
        fox_q_gain=dqg[0, :HEAD_DIM] + dqg[0, HEAD_DIM:],
        fox_k_gain=dkg[0, :HEAD_DIM] + dkg[0, HEAD_DIM:],
        fox_w_out=dw_fox_out,
        ada_b=jnp.concatenate([dmod0, dmod1], axis=0),
        norm1_g=jnp.concatenate([dn1g_0, dn1g_1], axis=0),
    )
    return loss[0, 0], dx0, grads, exchanged, exchanged_last


_HBM = pl.BlockSpec(memory_space=pl.ANY)
N_PEER = N_DEV - 1


def _xchg_out_shapes(arrs, scatter):
    return [jax.ShapeDtypeStruct(a.shape if s else (N_DEV,) + a.shape, a.dtype) for a, s in zip(arrs, scatter)]


def _xchg_sems(n):
    return [pltpu.SemaphoreType.DMA((n * N_PEER,)), pltpu.SemaphoreType.DMA((n * N_PEER,)), pltpu.SemaphoreType.DMA((n,))]


def _xchg_copies(ins, outs, scatter, send, recv, loc):
    x, y, c = lax.axis_index("x"), lax.axis_index("y"), lax.axis_index("c")
    me = 4 * x + 2 * y + c
    copies = []
    for a in range(len(ins)):
        copies.append(pltpu.make_async_copy(ins[a].at[me] if scatter[a] else ins[a], outs[a].at[me], loc.at[a]))
        for k in range(1, N_DEV):
            px = 1 - x if k & 4 else x
            py = 1 - y if k & 2 else y
            pc = 1 - c if k & 1 else c
            copies.append(pltpu.make_async_remote_copy(
                src_ref=ins[a].at[4 * px + 2 * py + pc] if scatter[a] else ins[a], dst_ref=outs[a].at[me],
                send_sem=send.at[a * N_PEER + k - 1], recv_sem=recv.at[a * N_PEER + k - 1],
                device_id=(px, py, pc), device_id_type=MESH))
    return copies


def _exchange(arrs, scatter, *, name):
    n = len(arrs)

    def body(*refs):
        copies = _xchg_copies(refs[:n], refs[n:2 * n], scatter, *refs[2 * n:])
        for cp in copies:
            cp.start()
        for cp in copies:
            cp.wait()

    return pl.pallas_call(
        body, in_specs=[_HBM] * n, out_specs=[_HBM] * n, out_shape=_xchg_out_shapes(arrs, scatter),
        scratch_shapes=_xchg_sems(n),
        compiler_params=pltpu.CompilerParams(has_side_effects=True), name=name)(*arrs)


def _adamw(w, parts, m, v, *, name, tr=256):
    L, R, C = w.shape
    P = parts.shape[0]
    tr = next(t for t in range(min(R, tr), 0, -1) if R % t == 0 and (t % 16 == 0 or t == R))
    nr = R // tr
    c1 = 1.0 - ADAM_B1 ** ADAM_STEP
    c2 = 1.0 - ADAM_B2 ** ADAM_STEP

    def body(w_ref, p_ref, m_ref, v_ref, g_ref, d_ref, mo_ref, vo_ref):
        g = p_ref[0].astype(F32)
        for p in range(1, P):
            g = g + p_ref[p].astype(F32)
        mn = ADAM_B1 * m_ref[0] + (1.0 - ADAM_B1) * g
        vn = ADAM_B2 * v_ref[0] + (1.0 - ADAM_B2) * (g * g)
        g_ref[0] = g
        mo_ref[0] = mn
        vo_ref[0] = vn
        d_ref[0] = -ADAM_LR * ((mn / c1) / (jnp.sqrt(vn / c2) + ADAM_EPS) + ADAM_WD * w_ref[0])

    row = pl.BlockSpec((1, tr, C), lambda l, i: (l, i, 0))
    return pl.pallas_call(
        body, grid=(L, nr), in_specs=[row, pl.BlockSpec((P, tr, C), lambda l, i: (0, l * nr + i, 0)), row, row],
        out_specs=[row] * 4, out_shape=[jax.ShapeDtypeStruct((L, R, C), F32)] * 4,
        compiler_params=_params("parallel", "parallel"), name=name)(w, parts, m, v)


def _sum_parts(parts, *, name):
    P, R, C = parts.shape

    def body(p_ref, o_ref):
        g = p_ref[0]
        for p in range(1, P):
            g = g + p_ref[p]
        o_ref[...] = g

    return pl.pallas_call(body, out_shape=jax.ShapeDtypeStruct((R, C), F32), name=name)(parts)


WEIGHTS = ["fox_w_in", "fox_b_f", "fox_q_gain", "fox_k_gain", "fox_w_out", "sgu_w_in", "sgu_b_in", "sgu_v_gain",
           "sgu_v_bias", "sgu_w_s", "sgu_b_s", "sgu_w_out", "ffn_w_up", "ffn_conv_w", "ffn_conv_b", "ffn_w_down",
           "ada_w", "ada_b", "norm1_g", "norm2_g", "final_g"]
BIG_AXIS = dict(fox_w_in=1, fox_w_out=0, sgu_w_in=1, sgu_w_out=0, ffn_w_up=1, ffn_w_down=0, ada_w=1)
SMALL_SHARDED = ["sgu_b_in", "sgu_v_gain", "sgu_v_bias", "ffn_conv_w"]
SINGLE_LAYER = ("fox_", "sgu_")
BEFORE_ATTENTION = ["fox_w_in"]
AFTER_ATTENTION = ["fox_w_out", "sgu_w_in", "sgu_w_out", "ffn_w_up", "ffn_w_down"]
SMALL_EARLY = ["sgu_b_in", "sgu_v_gain", "sgu_v_bias", "sgu_w_s", "sgu_b_s", "ffn_conv_w", "ffn_conv_b", "norm2_g", "final_g"]


def _assemble(stacked, layers, axis):
    _, lr, cc = stacked.shape
    r = lr // layers
    s4 = stacked.reshape(N_DEV, layers, r, cc)
    if axis == 0:
        return s4.transpose(1, 0, 2, 3).reshape(layers, N_DEV * r, cc)
    return s4.transpose(1, 2, 0, 3).reshape(layers, r, N_DEV * cc)


def _disassemble(full, axis):
    layers, R, C = full.shape
    if axis == 0:
        r = R // N_DEV
        return full.reshape(layers, N_DEV, r, C).transpose(1, 0, 2, 3).reshape(N_DEV, layers * r, C)
    cc = C // N_DEV
    return full.reshape(layers, R, N_DEV, cc).transpose(2, 0, 1, 3).reshape(N_DEV, layers * R, cc)


def kernel(x, c, fox_w_in, fox_b_f, fox_q_gain, fox_k_gain, fox_w_out, sgu_w_in, sgu_b_in, sgu_v_gain, sgu_v_bias, sgu_w_s, sgu_b_s, sgu_w_out, ffn_w_up, ffn_conv_w, ffn_conv_b, ffn_w_down, ada_w, ada_b, norm1_g, norm2_g, final_g, loss_target, m_fox_w_in, m_fox_b_f, m_fox_q_gain, m_fox_k_gain, m_fox_w_out, m_sgu_w_in, m_sgu_b_in, m_sgu_v_gain, m_sgu_v_bias, m_sgu_w_s, m_sgu_b_s, m_sgu_w_out, m_ffn_w_up, m_ffn_conv_w, m_ffn_conv_b, m_ffn_w_down, m_ada_w, m_ada_b, m_norm1_g, m_norm2_g, m_final_g, v_fox_w_in, v_fox_b_f, v_fox_q_gain, v_fox_k_gain, v_fox_w_out, v_sgu_w_in, v_sgu_b_in, v_sgu_v_gain, v_sgu_v_bias, v_sgu_w_s, v_sgu_b_s, v_sgu_w_out, v_ffn_w_up, v_ffn_conv_w, v_ffn_conv_b, v_ffn_w_down, v_ada_w, v_ada_b, v_norm1_g, v_norm2_g, v_final_g):
    args = dict(locals())
    wts = {n: args[n] for n in WEIGHTS}
    ms = {n: args["m_" + n] for n in WEIGHTS}
    vs = {n: args["v_" + n] for n in WEIGHTS}
    me = 4 * lax.axis_index("x") + 2 * lax.axis_index("y") + lax.axis_index("c")

    shard2d = lambda n: wts[n].astype(BF16).reshape(-1, wts[n].shape[-1])

    def assemble_big(names, got):
        out = {}
        for n, g in zip(names, got):
            f = _assemble(g, wts[n].shape[0], BIG_AXIS[n])
            out[n] = f[0] if n.startswith(SINGLE_LAYER) else f
        return out

    def blocks_of(names, grads):
        return [_disassemble(grads[n] if grads[n].ndim == 3 else grads[n][None], BIG_AXIS[n]) for n in names]

    send = [c] + [shard2d(n) for n in BEFORE_ATTENTION] + [wts[n].reshape(-1, wts[n].shape[-1]) for n in SMALL_SHARDED]
    got = _exchange(send, [False] * len(send), name="gather_first")
    c_all = got[0].reshape(N_DEV, -1)
    full = assemble_big(BEFORE_ATTENTION, got[1:1 + len(BEFORE_ATTENTION)])
    for n, g in zip(SMALL_SHARDED, got[1 + len(BEFORE_ATTENTION):]):
        lead = wts[n].shape[:-1]
        f = jnp.moveaxis(g.reshape((N_DEV,) + wts[n].shape), 0, -2).reshape(lead + (-1,))
        full[n] = f[0] if n.startswith(SINGLE_LAYER) else f
    for n in WEIGHTS:
        if n not in full and n not in BIG_AXIS:
            full[n] = wts[n][0] if n.startswith(SINGLE_LAYER) else wts[n]

    ada_cols = wts["ada_w"].shape[-1]
    mod_rows = []
    for i in range(2):
        b_mine = lax.dynamic_slice_in_dim(wts["ada_b"][i], me * ada_cols, ada_cols).reshape(1, ada_cols)
        m, c_act = _ada_mod(c_all, wts["ada_w"][i].astype(BF16), b_mine, name=f"ada_mod_{i}")
        mod_rows.append(m)
    got = _exchange([jnp.concatenate(mod_rows, axis=1)[:, None, :]], [True], name="exchange_mods")[0]
    d_model = x.shape[-1]
    mods = []
    for i in range(2):
        mod = got[:, 0, i * ada_cols:(i + 1) * ada_cols].reshape(1, N_DEV * ada_cols)
        mods.append([mod[:, k * d_model:(k + 1) * d_model] for k in range(6)])

    small = [n for n in WEIGHTS if n not in BIG_AXIS]
    small_late = [n for n in small if n not in SMALL_EARLY]

    def pack_flat(arrays):
        flat = jnp.concatenate([a.reshape(-1).astype(F32) for a in arrays])
        rows = -(-flat.shape[0] // (8 * LANES)) * 8
        return jnp.pad(flat, (0, rows * LANES - flat.shape[0])).reshape(rows, LANES)

    late = ([shard2d(n) for n in AFTER_ATTENTION], lambda g: assemble_big(AFTER_ATTENTION, g))
    loss, grad_x, grads, got_late, got_last = _local_step(
        x[0], loss_target[0], full, mods, late,
        lambda gr: (blocks_of(AFTER_ATTENTION, gr) + [pack_flat([gr[n] for n in SMALL_EARLY])],
                    [True] * len(AFTER_ATTENTION) + [False]),
        lambda gr: (blocks_of(BEFORE_ATTENTION, gr), [True] * len(BEFORE_ATTENTION)))

    flat_late_all = _exchange([pack_flat([loss] + [grads[n] for n in small_late])], [False], name="gather_small_grads")[0]
    total_late = _sum_parts(flat_late_all, name="sum_small_grads_late").reshape(-1)
    total_early = _sum_parts(got_late[len(AFTER_ATTENTION)], name="sum_small_grads_early").reshape(-1)
    loss_out = total_late[0]
    summed, offs = {}, {}
    for vec, names, off in ((total_early, SMALL_EARLY, 0), (total_late, small_late, 1)):
        for n in names:
            size = math.prod(grads[n].shape)
            summed[n], offs[n] = vec[off:off + size].reshape(grads[n].shape), off
            off += size

    off_ada = offs["ada_b"]
    dmod_all = flat_late_all.reshape(N_DEV, -1)[:, off_ada:off_ada + 2 * N_DEV * ada_cols].reshape(N_DEV, 2, N_DEV * ada_cols)
    dmod_mine = lax.dynamic_slice_in_dim(dmod_all, me * ada_cols, ada_cols, axis=2)
    d_ada = [_mm(c_act, jnp.pad(dmod_mine[:, i], ((0, c_act.shape[0] - N_DEV), (0, 0))).astype(BF16), ta=True,
                 name=f"ada_dw_{i}") for i in range(2)]

    out_g, out_d, out_m, out_v = {}, {}, {}, {}
    summands = dict(zip(BEFORE_ATTENTION, got_last))
    summands.update(zip(AFTER_ATTENTION, got_late))
    summands["ada_w"] = jnp.concatenate(d_ada, axis=0)[None]
    for n, p in summands.items():
        out_g[n], out_d[n], out_m[n], out_v[n] = _adamw(wts[n], p, ms[n], vs[n], name=f"adamw_{n}")
    small_g = {}
    for n in small:
        g = summed[n]
        if n in SMALL_SHARDED:
            blk = g.shape[-1] // N_DEV
            g = lax.dynamic_slice_in_dim(g, me * blk, blk, axis=g.ndim - 1)
        small_g[n] = g.reshape(wts[n].shape)
    cat = lambda d: jnp.concatenate([d[n].reshape(-1) for n in small])
    n_small = sum(math.prod(wts[n].shape) for n in small)
    rows2 = -(-n_small // (256 * LANES)) * 256
    pack = lambda d, fill: jnp.pad(cat(d), (0, rows2 * LANES - n_small), constant_values=fill).reshape(1, rows2, LANES)
    g, d, mn, vn = _adamw(pack(wts, 0.0), pack(small_g, 0.0), pack(ms, 0.0), pack(vs, 1.0), name="adamw_small")
    off = 0
    for n in small:
        size = math.prod(wts[n].shape)
        for src, dst in ((g, out_g), (d, out_d), (mn, out_m), (vn, out_v)):
            dst[n] = src.reshape(-1)[off:off + size].reshape(wts[n].shape)
        off += size

    return (loss_out, grad_x[None], *[out_g[n] for n in WEIGHTS], *[out_d[n] for n in WEIGHTS],
            *[out_m[n] for n in WEIGHTS], *[out_v[n] for n in WEIGHTS])
```
